```python
import jax, jax.numpy as jnp
from jax import lax
import numpy as np

D_MODEL = 1024
BATCH = 8
SEQ = 8192
DEPTH = 1

N_META = 16
GLA_HEADS = 4
GLA_DK = D_MODEL // 2
GLA_DV = D_MODEL
GLA_DKH = GLA_DK // GLA_HEADS
GLA_DVH = GLA_DV // GLA_HEADS
GLA_RANK = 16
GATE_TAU = 16.0
CHUNK = 64
META_PAD = CHUNK - N_META
CONF_CH = D_MODEL
CONF_K = 31
D_FF = 2816
FFN_K = 3
IN_WIDTHS = (GLA_DK, GLA_DK, GLA_DV, GLA_DV, GLA_RANK, 2 * CONF_CH, D_MODEL, D_MODEL)
N_IN = sum(IN_WIDTHS)
RMS_EPS = 1e-6
LN_EPS = 1e-5

kernel_name = "gla_conformer_gated_hybrid_block"


def split_points(widths):
    pts, acc = [], 0
    for w in widths[:-1]:
        acc += w
        pts.append(acc)
    return pts


def rms_norm(x, g):
    xf = x.astype(jnp.float32)
    y = xf * lax.rsqrt(jnp.mean(xf * xf, axis=-1, keepdims=True) + RMS_EPS)
    return (y * g.astype(jnp.float32)).astype(x.dtype)


def layer_norm(x, g, b):
    xf = x.astype(jnp.float32)
    mu = jnp.mean(xf, axis=-1, keepdims=True)
    var = jnp.mean(jnp.square(xf - mu), axis=-1, keepdims=True)
    y = (xf - mu) * lax.rsqrt(var + LN_EPS)
    return (y * g.astype(jnp.float32) + b.astype(jnp.float32)).astype(x.dtype)


def causal_dwconv(x, w, b):
    K, C = w.shape
    out = lax.conv_general_dilated(
        x, w[:, None, :].astype(x.dtype), window_strides=(1,), padding=[(K - 1, 0)],
        dimension_numbers=("NWC", "WIO", "NWC"), feature_group_count=C)
    return out + b.astype(x.dtype)


def gla_chunked(q, k, v, log_a):
    out_dtype = v.dtype
    q, k, v, log_a = (t.astype(jnp.float32) for t in (q, k, v, log_a))
    B, H, T, dk = q.shape
    dv = v.shape[-1]
    n = T // CHUNK

    def to_chunks(t):
        return jnp.moveaxis(t.reshape(B, H, n, CHUNK, t.shape[-1]), 2, 0)

    qc, kc, vc, ac = (to_chunks(t) for t in (q, k, v, log_a))
    mask = jnp.tril(jnp.ones((CHUNK, CHUNK), dtype=bool))[:, :, None]

    def step(S, inp):
        qi, ki, vi, ai = inp
        b = jnp.cumsum(ai, axis=-2)
        b_last = b[..., -1:, :]
        o_inter = jnp.einsum('bhck,bhkv->bhcv', qi * jnp.exp(b), S)
        diff = b[..., :, None, :] - b[..., None, :, :]
        decay = jnp.where(mask, jnp.exp(jnp.where(mask, diff, 0.0)), 0.0)
        scores = jnp.einsum('bhik,bhjk,bhijk->bhij', qi, ki, decay)
        o = o_inter + jnp.einsum('bhij,bhjv->bhiv', scores, vi)
        S_new = S * jnp.exp(b_last)[..., 0, :, None] + jnp.einsum(
            'bhck,bhcv->bhkv', ki * jnp.exp(b_last - b), vi)
        return S_new, o

    S0 = jnp.zeros((B, H, dk, dv), jnp.float32)
    _, oc = lax.scan(step, S0, (qc, kc, vc, ac))
    return jnp.moveaxis(oc, 0, 2).reshape(B, H, T, dv).astype(out_dtype)


def _fwd_setup_inputs(seed: int = 0) -> dict:
    key = jax.random.key(seed)
    ks = jax.random.split(key, 24)
    nrm = lambda k, shape, s: jax.random.normal(k, shape, jnp.float32) * s
    return {
        "x": nrm(ks[0], (BATCH, SEQ, D_MODEL), 1.0),
        "meta_tokens": nrm(ks[1], (N_META, D_MODEL), 1.0),
        "norm_mix_g": 1.0 + nrm(ks[2], (DEPTH, D_MODEL), 0.02),
        "w_in": nrm(ks[3], (DEPTH, D_MODEL, N_IN), D_MODEL ** -0.5),
        "w_alpha_up": nrm(ks[4], (DEPTH, GLA_RANK, GLA_DK), GLA_RANK ** -0.5),
        "b_alpha": nrm(ks[5], (DEPTH, GLA_DK), 0.1),
        "gla_norm_g": 1.0 + nrm(ks[6], (DEPTH, GLA_DV), 0.02),
        "w_gla_o": nrm(ks[7], (DEPTH, GLA_DV, D_MODEL), GLA_DV ** -0.5),
        "conf_dw_w": nrm(ks[8], (DEPTH, CONF_K, CONF_CH), CONF_K ** -0.5),
        "conf_dw_b": nrm(ks[9], (DEPTH, CONF_CH), 0.02),
        "conf_ln_g": 1.0 + nrm(ks[10], (DEPTH, CONF_CH), 0.02),
        "conf_ln_b": nrm(ks[11], (DEPTH, CONF_CH), 0.02),
        "w_conf_o": nrm(ks[12], (DEPTH, CONF_CH, D_MODEL), CONF_CH ** -0.5),
        "w_out": nrm(ks[13], (DEPTH, D_MODEL, D_MODEL), D_MODEL ** -0.5),
        "norm_ffn_g": 1.0 + nrm(ks[14], (DEPTH, D_MODEL), 0.02),
        "w_up": nrm(ks[15], (DEPTH, D_MODEL, 2 * D_FF), D_MODEL ** -0.5),
        "ffn_dw_w": nrm(ks[16], (DEPTH, FFN_K, D_FF), FFN_K ** -0.5),
        "ffn_dw_b": nrm(ks[17], (DEPTH, D_FF), 0.02),
        "w_down": nrm(ks[18], (DEPTH, D_FF, D_MODEL), D_FF ** -0.5),
        "final_norm_g": 1.0 + nrm(ks[19], (D_MODEL,), 0.02),
    }


def _fwd_reference(x, meta_tokens, norm_mix_g, w_in, w_alpha_up, b_alpha, gla_norm_g, w_gla_o,
              conf_dw_w, conf_dw_b, conf_ln_g, conf_ln_b, w_conf_o, w_out, norm_ffn_g,
              w_up, ffn_dw_w, ffn_dw_b, w_down, final_norm_g):
    B, S, D = x.shape
    L = S + N_META
    meta = jnp.broadcast_to(meta_tokens[None].astype(x.dtype), (B, N_META, D))
    h = jnp.concatenate([meta, x], axis=1)
    pts = split_points(IN_WIDTHS)
    seq_pad = [(0, 0), (0, 0), (META_PAD, 0), (0, 0)]

    def heads(t):
        return t.reshape(B, L, GLA_HEADS, -1).transpose(0, 2, 1, 3)

    for l in range(DEPTH):
        u = rms_norm(h, norm_mix_g[l])
        proj = u @ w_in[l]
        q, k, v, r, a_lr, c_in, g_gla, g_conf = jnp.split(proj, pts, axis=-1)

        log_a = jax.nn.log_sigmoid(
            (a_lr @ w_alpha_up[l] + b_alpha[l]).astype(jnp.float32)) / GATE_TAU
        qh = jnp.pad(heads(q) * (GLA_DKH ** -0.5), seq_pad)
        kh = jnp.pad(heads(k), seq_pad)
        vh = jnp.pad(heads(v), seq_pad)
        ah = jnp.pad(heads(log_a), seq_pad)
        o = gla_chunked(qh, kh, vh, ah)[:, :, META_PAD:]
        o = rms_norm(o.transpose(0, 2, 1, 3), gla_norm_g[l].reshape(GLA_HEADS, GLA_DVH))
        o = o.reshape(B, L, GLA_DV) * jax.nn.silu(r)
        br_gla = o @ w_gla_o[l]

        c1, c2 = jnp.split(c_in, 2, axis=-1)
        c = c1 * jax.nn.sigmoid(c2)
        c = causal_dwconv(c, conf_dw_w[l], conf_dw_b[l])
        c = layer_norm(c, conf_ln_g[l], conf_ln_b[l])
        br_conf = jax.nn.silu(c) @ w_conf_o[l]

        merged = jax.nn.sigmoid(g_gla) * br_gla + jax.nn.sigmoid(g_conf) * br_conf
        h = h + merged @ w_out[l]

        u = rms_norm(h, norm_ffn_g[l])
        a, bv = jnp.split(u @ w_up[l], 2, axis=-1)
        a = causal_dwconv(a, ffn_dw_w[l], ffn_dw_b[l])
        h = h + (jax.nn.silu(a) * bv) @ w_down[l]

    h = rms_norm(h, final_norm_g)
    return h[:, N_META:]


import jax as _jax
import jax.numpy as _jnp

TWIN_FORMAT = 'train_step'
FWD_PARAMS = ['x', 'meta_tokens', 'norm_mix_g', 'w_in', 'w_alpha_up', 'b_alpha', 'gla_norm_g', 'w_gla_o', 'conf_dw_w', 'conf_dw_b', 'conf_ln_g', 'conf_ln_b', 'w_conf_o', 'w_out', 'norm_ffn_g', 'w_up', 'ffn_dw_w', 'ffn_dw_b', 'w_down', 'final_norm_g']
TWIN_WEIGHTS = ['meta_tokens', 'norm_mix_g', 'w_in', 'w_alpha_up', 'b_alpha', 'gla_norm_g', 'w_gla_o', 'conf_dw_w', 'conf_dw_b', 'conf_ln_g', 'conf_ln_b', 'w_conf_o', 'w_out', 'norm_ffn_g', 'w_up', 'ffn_dw_w', 'ffn_dw_b', 'w_down', 'final_norm_g']
TWIN_DIFF_INPUT = 'x'
TWIN_INPUTS = ['x', 'meta_tokens', 'norm_mix_g', 'w_in', 'w_alpha_up', 'b_alpha', 'gla_norm_g', 'w_gla_o', 'conf_dw_w', 'conf_dw_b', 'conf_ln_g', 'conf_ln_b', 'w_conf_o', 'w_out', 'norm_ffn_g', 'w_up', 'ffn_dw_w', 'ffn_dw_b', 'w_down', 'final_norm_g', 'loss_target', 'm_meta_tokens', 'm_norm_mix_g', 'm_w_in', 'm_w_alpha_up', 'm_b_alpha', 'm_gla_norm_g', 'm_w_gla_o', 'm_conf_dw_w', 'm_conf_dw_b', 'm_conf_ln_g', 'm_conf_ln_b', 'm_w_conf_o', 'm_w_out', 'm_norm_ffn_g', 'm_w_up', 'm_ffn_dw_w', 'm_ffn_dw_b', 'm_w_down', 'm_final_norm_g', 'v_meta_tokens', 'v_norm_mix_g', 'v_w_in', 'v_w_alpha_up', 'v_b_alpha', 'v_gla_norm_g', 'v_w_gla_o', 'v_conf_dw_w', 'v_conf_dw_b', 'v_conf_ln_g', 'v_conf_ln_b', 'v_w_conf_o', 'v_w_out', 'v_norm_ffn_g', 'v_w_up', 'v_ffn_dw_w', 'v_ffn_dw_b', 'v_w_down', 'v_final_norm_g']
TWIN_OUTPUTS = ['loss', 'grad_x', 'grad_meta_tokens', 'grad_norm_mix_g', 'grad_w_in', 'grad_w_alpha_up', 'grad_b_alpha', 'grad_gla_norm_g', 'grad_w_gla_o', 'grad_conf_dw_w', 'grad_conf_dw_b', 'grad_conf_ln_g', 'grad_conf_ln_b', 'grad_w_conf_o', 'grad_w_out', 'grad_norm_ffn_g', 'grad_w_up', 'grad_ffn_dw_w', 'grad_ffn_dw_b', 'grad_w_down', 'grad_final_norm_g', 'delta_meta_tokens', 'delta_norm_mix_g', 'delta_w_in', 'delta_w_alpha_up', 'delta_b_alpha', 'delta_gla_norm_g', 'delta_w_gla_o', 'delta_conf_dw_w', 'delta_conf_dw_b', 'delta_conf_ln_g', 'delta_conf_ln_b', 'delta_w_conf_o', 'delta_w_out', 'delta_norm_ffn_g', 'delta_w_up', 'delta_ffn_dw_w', 'delta_ffn_dw_b', 'delta_w_down', 'delta_final_norm_g', 'new_m_meta_tokens', 'new_m_norm_mix_g', 'new_m_w_in', 'new_m_w_alpha_up', 'new_m_b_alpha', 'new_m_gla_norm_g', 'new_m_w_gla_o', 'new_m_conf_dw_w', 'new_m_conf_dw_b', 'new_m_conf_ln_g', 'new_m_conf_ln_b', 'new_m_w_conf_o', 'new_m_w_out', 'new_m_norm_ffn_g', 'new_m_w_up', 'new_m_ffn_dw_w', 'new_m_ffn_dw_b', 'new_m_w_down', 'new_m_final_norm_g', 'new_v_meta_tokens', 'new_v_norm_mix_g', 'new_v_w_in', 'new_v_w_alpha_up', 'new_v_b_alpha', 'new_v_gla_norm_g', 'new_v_w_gla_o', 'new_v_conf_dw_w', 'new_v_conf_dw_b', 'new_v_conf_ln_g', 'new_v_conf_ln_b', 'new_v_w_conf_o', 'new_v_w_out', 'new_v_norm_ffn_g', 'new_v_w_up', 'new_v_ffn_dw_w', 'new_v_ffn_dw_b', 'new_v_w_down', 'new_v_final_norm_g']
TWIN_LEAF_KINDS = {'loss': 'loss', 'grad_x': 'grad_x', 'grad_meta_tokens': 'grad_w', 'grad_norm_mix_g': 'grad_w', 'grad_w_in': 'grad_w', 'grad_w_alpha_up': 'grad_w', 'grad_b_alpha': 'grad_w', 'grad_gla_norm_g': 'grad_w', 'grad_w_gla_o': 'grad_w', 'grad_conf_dw_w': 'grad_w', 'grad_conf_dw_b': 'grad_w', 'grad_conf_ln_g': 'grad_w', 'grad_conf_ln_b': 'grad_w', 'grad_w_conf_o': 'grad_w', 'grad_w_out': 'grad_w', 'grad_norm_ffn_g': 'grad_w', 'grad_w_up': 'grad_w', 'grad_ffn_dw_w': 'grad_w', 'grad_ffn_dw_b': 'grad_w', 'grad_w_down': 'grad_w', 'grad_final_norm_g': 'grad_w', 'delta_meta_tokens': 'delta_w', 'delta_norm_mix_g': 'delta_w', 'delta_w_in': 'delta_w', 'delta_w_alpha_up': 'delta_w', 'delta_b_alpha': 'delta_w', 'delta_gla_norm_g': 'delta_w', 'delta_w_gla_o': 'delta_w', 'delta_conf_dw_w': 'delta_w', 'delta_conf_dw_b': 'delta_w', 'delta_conf_ln_g': 'delta_w', 'delta_conf_ln_b': 'delta_w', 'delta_w_conf_o': 'delta_w', 'delta_w_out': 'delta_w', 'delta_norm_ffn_g': 'delta_w', 'delta_w_up': 'delta_w', 'delta_ffn_dw_w': 'delta_w', 'delta_ffn_dw_b': 'delta_w', 'delta_w_down': 'delta_w', 'delta_final_norm_g': 'delta_w', 'new_m_meta_tokens': 'new_m', 'new_m_norm_mix_g': 'new_m', 'new_m_w_in': 'new_m', 'new_m_w_alpha_up': 'new_m', 'new_m_b_alpha': 'new_m', 'new_m_gla_norm_g': 'new_m', 'new_m_w_gla_o': 'new_m', 'new_m_conf_dw_w': 'new_m', 'new_m_conf_dw_b': 'new_m', 'new_m_conf_ln_g': 'new_m', 'new_m_conf_ln_b': 'new_m', 'new_m_w_conf_o': 'new_m', 'new_m_w_out': 'new_m', 'new_m_norm_ffn_g': 'new_m', 'new_m_w_up': 'new_m', 'new_m_ffn_dw_w': 'new_m', 'new_m_ffn_dw_b': 'new_m', 'new_m_w_down': 'new_m', 'new_m_final_norm_g': 'new_m', 'new_v_meta_tokens': 'new_v', 'new_v_norm_mix_g': 'new_v', 'new_v_w_in': 'new_v', 'new_v_w_alpha_up': 'new_v', 'new_v_b_alpha': 'new_v', 'new_v_gla_norm_g': 'new_v', 'new_v_w_gla_o': 'new_v', 'new_v_conf_dw_w': 'new_v', 'new_v_conf_dw_b': 'new_v', 'new_v_conf_ln_g': 'new_v', 'new_v_conf_ln_b': 'new_v', 'new_v_w_conf_o': 'new_v', 'new_v_w_out': 'new_v', 'new_v_norm_ffn_g': 'new_v', 'new_v_w_up': 'new_v', 'new_v_ffn_dw_w': 'new_v', 'new_v_ffn_dw_b': 'new_v', 'new_v_w_down': 'new_v', 'new_v_final_norm_g': 'new_v'}


def _forward(args):
    return _fwd_reference(*[args[k] for k in FWD_PARAMS])


def _output_shape():
    def fwd():
        inp = _fwd_setup_inputs(0)
        return _fwd_reference(*[inp[k] for k in FWD_PARAMS])
    out = _jax.eval_shape(fwd)
    return out.shape, out.dtype

N_MICROBATCH = 1
ADAM_LR = 0.001
ADAM_B1 = 0.9
ADAM_B2 = 0.999
ADAM_EPS = 1e-08
ADAM_WD = 0.01
ADAM_STEP = 10
PER_EXAMPLE_BATCH_AXIS = {'x': 0, 'loss_target': 0}
SHARED_INPUTS = []
_WEIGHT_DTYPES = {'meta_tokens': _jnp.float32, 'norm_mix_g': _jnp.float32, 'w_in': _jnp.float32, 'w_alpha_up': _jnp.float32, 'b_alpha': _jnp.float32, 'gla_norm_g': _jnp.float32, 'w_gla_o': _jnp.float32, 'conf_dw_w': _jnp.float32, 'conf_dw_b': _jnp.float32, 'conf_ln_g': _jnp.float32, 'conf_ln_b': _jnp.float32, 'w_conf_o': _jnp.float32, 'w_out': _jnp.float32, 'norm_ffn_g': _jnp.float32, 'w_up': _jnp.float32, 'ffn_dw_w': _jnp.float32, 'ffn_dw_b': _jnp.float32, 'w_down': _jnp.float32, 'final_norm_g': _jnp.float32}
MOMENT_SCALE = {'meta_tokens': 8.110323e-03, 'norm_mix_g': 2.134545e-01, 'w_in': 7.457716e-02, 'w_alpha_up': 1.377920e-02, 'b_alpha': 5.741224e-02, 'gla_norm_g': 8.279911e-02, 'w_gla_o': 8.195008e-02, 'conf_dw_w': 8.615917e-02, 'conf_dw_b': 1.639185e-01, 'conf_ln_g': 1.016300e-01, 'conf_ln_b': 8.756706e-02, 'w_conf_o': 8.511819e-02, 'w_out': 1.182700e-01, 'norm_ffn_g': 1.769210e-01, 'w_up': 7.449544e-02, 'ffn_dw_w': 7.639496e-02, 'ffn_dw_b': 7.388076e-02, 'w_down': 1.223338e-01, 'final_norm_g': 6.400227e+01}


def _to_microbatches(a, axis):
    t = _jnp.moveaxis(a, axis, 0)
    t = t.reshape((N_MICROBATCH, t.shape[0] // N_MICROBATCH) + t.shape[1:])
    return _jnp.moveaxis(t, 1, axis + 1)


def setup_inputs(seed: int = 0) -> dict:
    inp = _fwd_setup_inputs(seed)
    key = _jax.random.fold_in(_jax.random.key(seed), 7919)
    shape, _ = _output_shape()
    out = dict(inp)
    out["loss_target"] = _jax.random.normal(_jax.random.fold_in(key, 0), shape, _jnp.float32)
    for i, name in enumerate(TWIN_WEIGHTS):
        w = inp[name].astype(_jnp.float32)
        if MOMENT_SCALE is None:
            s = _jnp.sqrt(_jnp.mean(_jnp.square(w)) + 1e-30)
        else:
            s = MOMENT_SCALE[name]
        km, kv = _jax.random.split(_jax.random.fold_in(key, i + 1))
        out[name] = w
        out["m_" + name] = s * _jax.random.normal(km, w.shape, _jnp.float32)
        out["v_" + name] = (s * s) * _jax.random.uniform(kv, w.shape, _jnp.float32, 0.5, 1.5)
    if N_MICROBATCH > 1:
        for name, axis in PER_EXAMPLE_BATCH_AXIS.items():
            out[name] = _to_microbatches(out[name], axis)
    return {'x': out['x'], 'meta_tokens': out['meta_tokens'], 'norm_mix_g': out['norm_mix_g'], 'w_in': out['w_in'], 'w_alpha_up': out['w_alpha_up'], 'b_alpha': out['b_alpha'], 'gla_norm_g': out['gla_norm_g'], 'w_gla_o': out['w_gla_o'], 'conf_dw_w': out['conf_dw_w'], 'conf_dw_b': out['conf_dw_b'], 'conf_ln_g': out['conf_ln_g'], 'conf_ln_b': out['conf_ln_b'], 'w_conf_o': out['w_conf_o'], 'w_out': out['w_out'], 'norm_ffn_g': out['norm_ffn_g'], 'w_up': out['w_up'], 'ffn_dw_w': out['ffn_dw_w'], 'ffn_dw_b': out['ffn_dw_b'], 'w_down': out['w_down'], 'final_norm_g': out['final_norm_g'], 'loss_target': out['loss_target'], 'm_meta_tokens': out['m_meta_tokens'], 'm_norm_mix_g': out['m_norm_mix_g'], 'm_w_in': out['m_w_in'], 'm_w_alpha_up': out['m_w_alpha_up'], 'm_b_alpha': out['m_b_alpha'], 'm_gla_norm_g': out['m_gla_norm_g'], 'm_w_gla_o': out['m_w_gla_o'], 'm_conf_dw_w': out['m_conf_dw_w'], 'm_conf_dw_b': out['m_conf_dw_b'], 'm_conf_ln_g': out['m_conf_ln_g'], 'm_conf_ln_b': out['m_conf_ln_b'], 'm_w_conf_o': out['m_w_conf_o'], 'm_w_out': out['m_w_out'], 'm_norm_ffn_g': out['m_norm_ffn_g'], 'm_w_up': out['m_w_up'], 'm_ffn_dw_w': out['m_ffn_dw_w'], 'm_ffn_dw_b': out['m_ffn_dw_b'], 'm_w_down': out['m_w_down'], 'm_final_norm_g': out['m_final_norm_g'], 'v_meta_tokens': out['v_meta_tokens'], 'v_norm_mix_g': out['v_norm_mix_g'], 'v_w_in': out['v_w_in'], 'v_w_alpha_up': out['v_w_alpha_up'], 'v_b_alpha': out['v_b_alpha'], 'v_gla_norm_g': out['v_gla_norm_g'], 'v_w_gla_o': out['v_w_gla_o'], 'v_conf_dw_w': out['v_conf_dw_w'], 'v_conf_dw_b': out['v_conf_dw_b'], 'v_conf_ln_g': out['v_conf_ln_g'], 'v_conf_ln_b': out['v_conf_ln_b'], 'v_w_conf_o': out['v_w_conf_o'], 'v_w_out': out['v_w_out'], 'v_norm_ffn_g': out['v_norm_ffn_g'], 'v_w_up': out['v_w_up'], 'v_ffn_dw_w': out['v_ffn_dw_w'], 'v_ffn_dw_b': out['v_ffn_dw_b'], 'v_w_down': out['v_w_down'], 'v_final_norm_g': out['v_final_norm_g']}


def _loss(weights, diff, rest, loss_target):
    with _jax.named_scope("forward"):
        args = {**rest, TWIN_DIFF_INPUT: diff, **{k: w.astype(_WEIGHT_DTYPES[k]) for k, w in weights.items()}}
        y = _forward(args)
    with _jax.named_scope("loss_head"):
        err = _jnp.square(y.astype(_jnp.float32) - loss_target)
        return 0.5 * _jnp.sum(_jnp.mean(err, axis=-1)) if err.ndim else 0.5 * err


def _adamw(w, g, m, v):
    m = ADAM_B1 * m + (1.0 - ADAM_B1) * g
    v = ADAM_B2 * v + (1.0 - ADAM_B2) * _jnp.square(g)
    m_hat = m / (1.0 - ADAM_B1 ** ADAM_STEP)
    v_hat = v / (1.0 - ADAM_B2 ** ADAM_STEP)
    delta = -ADAM_LR * (m_hat / (_jnp.sqrt(v_hat) + ADAM_EPS) + ADAM_WD * w)
    return delta, m, v


def reference(x, meta_tokens, norm_mix_g, w_in, w_alpha_up, b_alpha, gla_norm_g, w_gla_o, conf_dw_w, conf_dw_b, conf_ln_g, conf_ln_b, w_conf_o, w_out, norm_ffn_g, w_up, ffn_dw_w, ffn_dw_b, w_down, final_norm_g, loss_target, m_meta_tokens, m_norm_mix_g, m_w_in, m_w_alpha_up, m_b_alpha, m_gla_norm_g, m_w_gla_o, m_conf_dw_w, m_conf_dw_b, m_conf_ln_g, m_conf_ln_b, m_w_conf_o, m_w_out, m_norm_ffn_g, m_w_up, m_ffn_dw_w, m_ffn_dw_b, m_w_down, m_final_norm_g, v_meta_tokens, v_norm_mix_g, v_w_in, v_w_alpha_up, v_b_alpha, v_gla_norm_g, v_w_gla_o, v_conf_dw_w, v_conf_dw_b, v_conf_ln_g, v_conf_ln_b, v_w_conf_o, v_w_out, v_norm_ffn_g, v_w_up, v_ffn_dw_w, v_ffn_dw_b, v_w_down, v_final_norm_g):
    given = dict(x=x, meta_tokens=meta_tokens, norm_mix_g=norm_mix_g, w_in=w_in, w_alpha_up=w_alpha_up, b_alpha=b_alpha, gla_norm_g=gla_norm_g, w_gla_o=w_gla_o, conf_dw_w=conf_dw_w, conf_dw_b=conf_dw_b, conf_ln_g=conf_ln_g, conf_ln_b=conf_ln_b, w_conf_o=w_conf_o, w_out=w_out, norm_ffn_g=norm_ffn_g, w_up=w_up, ffn_dw_w=ffn_dw_w, ffn_dw_b=ffn_dw_b, w_down=w_down, final_norm_g=final_norm_g, loss_target=loss_target, m_meta_tokens=m_meta_tokens, m_norm_mix_g=m_norm_mix_g, m_w_in=m_w_in, m_w_alpha_up=m_w_alpha_up, m_b_alpha=m_b_alpha, m_gla_norm_g=m_gla_norm_g, m_w_gla_o=m_w_gla_o, m_conf_dw_w=m_conf_dw_w, m_conf_dw_b=m_conf_dw_b, m_conf_ln_g=m_conf_ln_g, m_conf_ln_b=m_conf_ln_b, m_w_conf_o=m_w_conf_o, m_w_out=m_w_out, m_norm_ffn_g=m_norm_ffn_g, m_w_up=m_w_up, m_ffn_dw_w=m_ffn_dw_w, m_ffn_dw_b=m_ffn_dw_b, m_w_down=m_w_down, m_final_norm_g=m_final_norm_g, v_meta_tokens=v_meta_tokens, v_norm_mix_g=v_norm_mix_g, v_w_in=v_w_in, v_w_alpha_up=v_w_alpha_up, v_b_alpha=v_b_alpha, v_gla_norm_g=v_gla_norm_g, v_w_gla_o=v_w_gla_o, v_conf_dw_w=v_conf_dw_w, v_conf_dw_b=v_conf_dw_b, v_conf_ln_g=v_conf_ln_g, v_conf_ln_b=v_conf_ln_b, v_w_conf_o=v_w_conf_o, v_w_out=v_w_out, v_norm_ffn_g=v_norm_ffn_g, v_w_up=v_w_up, v_ffn_dw_w=v_ffn_dw_w, v_ffn_dw_b=v_ffn_dw_b, v_w_down=v_w_down, v_final_norm_g=v_final_norm_g)
    weights = {n: given[n] for n in TWIN_WEIGHTS}
    shared = {n: given[n] for n in SHARED_INPUTS}
    per_example = {n: given[n] for n in ['x']}
    grad_fn = _jax.value_and_grad(_loss, argnums=(0, 1))

    def one_microbatch(ex, loss_target):
        ex = dict(ex)
        diff = ex.pop(TWIN_DIFF_INPUT)
        return grad_fn(weights, diff, {**shared, **ex}, loss_target)

    if N_MICROBATCH == 1:
        loss, (grad_w, grad_x) = one_microbatch(per_example, given["loss_target"])
    else:
        def body(carry, xs):
            loss_sum, grad_sum = carry
            l_k, (gw_k, gx_k) = one_microbatch(xs[0], xs[1])
            with _jax.named_scope("update"):
                return (loss_sum + l_k, _jax.tree.map(_jnp.add, grad_sum, gw_k)), gx_k

        init = (_jnp.zeros((), _jnp.float32), _jax.tree.map(_jnp.zeros_like, weights))
        (loss, grad_w), grad_x = _jax.lax.scan(body, init, (per_example, given["loss_target"]))
    with _jax.named_scope("update"):
        delta_w, new_m, new_v = {}, {}, {}
        for n in TWIN_WEIGHTS:
            delta_w[n], new_m[n], new_v[n] = _adamw(weights[n], grad_w[n], given["m_" + n], given["v_" + n])
    return (loss, grad_x, *[grad_w[n] for n in TWIN_WEIGHTS], *[delta_w[n] for n in TWIN_WEIGHTS],
            *[new_m[n] for n in TWIN_WEIGHTS], *[new_v[n] for n in TWIN_WEIGHTS])
```

```python
import functools

import jax
import jax.numpy as jnp
from jax import lax
from jax.experimental import pallas as pl
from jax.experimental.pallas import tpu as pltpu

F32, BF16 = jnp.float32, jnp.bfloat16
SDS = jax.ShapeDtypeStruct

D = 1024
N_META = 16
HEADS = 4
DK, DKH, DV, DVH = 512, 128, 1024, 256
RANK = 16
TAU = 16.0
CONF_K = 31
DFF = 2816
FFN_K = 3
IN_WIDTHS = (DK, DK, DV, DV, RANK, 2 * D, D, D)
RMS_EPS, LN_EPS = 1e-6, 1e-5
ADAM_LR, ADAM_B1, ADAM_B2, ADAM_EPS, ADAM_WD, ADAM_STEP = 0.001, 0.9, 0.999, 1e-08, 0.01, 10

NPROJ = 7 * D
LANES = 128
CH = 128
TM = 640
TE = 320
HALO = 32
HALO_F = 16
N_DEV = 8
VMEM_LIMIT = 60 * 1024 * 1024
MESH_T = pl.DeviceIdType.MESH

_ARB1 = pltpu.CompilerParams(dimension_semantics=("arbitrary",), vmem_limit_bytes=VMEM_LIMIT)
_ARB2 = pltpu.CompilerParams(dimension_semantics=("arbitrary", "arbitrary"), vmem_limit_bytes=VMEM_LIMIT)


def _dot(a, b):
    return jnp.dot(a, b, preferred_element_type=F32)


def _dot_nt(a, b):
    return lax.dot_general(a, b, (((1,), (1,)), ((), ())), preferred_element_type=F32)


def _dot_tn(a, b):
    return lax.dot_general(a, b, (((0,), (0,)), ((), ())), preferred_element_type=F32)


def _sigmoid(x):
    return 1.0 / (1.0 + jnp.exp(-x))


def _rows8(x):
    return x.reshape(x.shape[0] // 8, 8, x.shape[1]).sum(axis=0)


def _row_loop(n_rows, rb, fn):
    def step(i, carry):
        fn(pl.multiple_of(i * rb, rb))
        return carry
    lax.fori_loop(0, n_rows // rb, step, 0)


def _resident(shape):
    return pl.BlockSpec(shape, lambda *_: (0,) * len(shape))


def _norm_matmul(h, g, w, tn, name, w_extra=None):
    t, n = h.shape[0], w.shape[1]
    nt, nb = t // TM, n // tn

    def body(*refs):
        if w_extra is None:
            h_ref, g_ref, w_ref, u_ref, p_ref = refs
        else:
            h_ref, g_ref, w_ref, we_ref, u_ref, p_ref, e_ref = refs

        @pl.when(pl.program_id(1) == 0)
        def _():
            def blk(r0):
                x = h_ref[pl.ds(r0, 32), :]
                rinv = lax.rsqrt(jnp.mean(x * x, axis=-1, keepdims=True) + RMS_EPS)
                u_ref[pl.ds(r0, 32), :] = (x * rinv * g_ref[...]).astype(BF16)
            _row_loop(TM, 32, blk)
            if w_extra is not None:
                e_ref[...] = _dot(u_ref[...], we_ref[...]).astype(BF16)

        p_ref[...] = _dot(u_ref[...], w_ref[...]).astype(BF16)

    in_specs = [pl.BlockSpec((TM, D), lambda i, j: (i, 0)), _resident((1, D)), pl.BlockSpec((D, tn), lambda i, j: (0, j))]
    out_specs = [pl.BlockSpec((TM, D), lambda i, j: (i, 0)), pl.BlockSpec((TM, tn), lambda i, j: (i, j))]
    out_shape = [SDS((t, D), BF16), SDS((t, n), BF16)]
    args = [h, g, w]
    if w_extra is not None:
        in_specs.append(_resident(w_extra.shape))
        out_specs.append(pl.BlockSpec((TM, w_extra.shape[1]), lambda i, j: (i, 0)))
        out_shape.append(SDS((t, w_extra.shape[1]), BF16))
        args.append(w_extra)
    return pl.pallas_call(body, name=name, grid=(nt, nb), in_specs=in_specs, out_specs=out_specs,
                          out_shape=out_shape, compiler_params=_ARB2)(*args)


def _gla_decay(a_ref, wau_ref, ba_ref):
    z = _dot(a_ref[...], wau_ref[...]) + ba_ref[...]
    la = (jnp.minimum(z, 0.0) - jnp.log(1.0 + jnp.exp(-jnp.abs(z)))) * (1.0 / TAU)
    r = lax.broadcasted_iota(jnp.int32, (CH, CH), 0)
    c = lax.broadcasted_iota(jnp.int32, (CH, CH), 1)
    sel = jnp.concatenate([(r >= c).astype(F32), (c <= CH // 2).astype(F32), jnp.ones((CH, CH), F32)], axis=0)
    cum = jnp.dot(sel, la, precision=lax.Precision.HIGHEST, preferred_element_type=F32)
    return z, cum[:CH], cum[CH:2 * CH], cum[2 * CH:], r >= c


def _gla_fwd(proj, alr, wau, balpha, gn):
    t = proj.shape[0]
    nc = t // CH

    def body(qk_ref, v_ref, r_ref, a_ref, wau_ref, ba_ref, gn_ref, o_ref, og_ref, sall_ref, s_scr):
        @pl.when(pl.program_id(0) == 0)
        def _():
            s_scr[...] = jnp.zeros_like(s_scr)

        sall_ref[0] = s_scr[...]
        _, b, bmid, blast, causal = _gla_decay(a_ref, wau_ref, ba_ref)
        for h in range(HEADS):
            ks = slice(h * DKH, (h + 1) * DKH)
            vs = slice(h * DVH, (h + 1) * DVH)
            bh, mh, lh = b[:, ks], bmid[:, ks], blast[:, ks]
            q = qk_ref[:, ks].astype(F32) * (DKH ** -0.5)
            k = qk_ref[:, DK + h * DKH:DK + (h + 1) * DKH].astype(F32)
            v = v_ref[:, vs]
            qt = (q * jnp.exp(bh - mh)).astype(BF16)
            kt = (k * jnp.exp(mh - bh)).astype(BF16)
            qg = (q * jnp.exp(bh)).astype(BF16)
            kg = (k * jnp.exp(lh - bh)).astype(BF16)
            a = jnp.where(causal, _dot_nt(qt, kt), 0.0)
            st = s_scr[vs, :]
            o = _dot(a.astype(BF16), v) + _dot_nt(qg, st.astype(BF16))
            el = jnp.exp(lh)
            s_scr[vs, :] = st * jnp.concatenate([el, el], axis=0) + _dot_tn(v, kg)
            o_ref[:, vs] = o
            on = o * lax.rsqrt(jnp.mean(o * o, axis=-1, keepdims=True) + RMS_EPS) * gn_ref[:, vs]
            rr = r_ref[:, vs].astype(F32)
            og_ref[:, vs] = (on * (rr * _sigmoid(rr))).astype(BF16)

    return pl.pallas_call(
        body, name="gla_fwd", grid=(nc,),
        in_specs=[pl.BlockSpec((CH, D), lambda c: (c, 0)), pl.BlockSpec((CH, D), lambda c: (c, 1)),
                  pl.BlockSpec((CH, D), lambda c: (c, 6)), pl.BlockSpec((CH, LANES), lambda c: (c, 0)),
                  _resident((LANES, DK)), _resident((1, DK)), _resident((1, DV))],
        out_specs=[pl.BlockSpec((CH, DV), lambda c: (c, 0)), pl.BlockSpec((CH, DV), lambda c: (c, 0)),
                   pl.BlockSpec((1, DV, DKH), lambda c: (c, 0, 0))],
        out_shape=[SDS((t, DV), F32), SDS((t, DV), BF16), SDS((nc, DV, DKH), F32)],
        scratch_shapes=[pltpu.VMEM((DV, DKH), F32)], compiler_params=_ARB1)(proj, proj, proj, alr, wau, balpha, gn)


def _conf_fwd(proj, cw, cb, lg, lb):
    t = proj.shape[0]
    nt = t // TE

    def body(c1_ref, c2_ref, cw_ref, cb_ref, lg_ref, lb_ref, cc_ref, cs_ref, cext):
        i = pl.program_id(0)

        @pl.when(i == 0)
        def _():
            cext[0:HALO, :] = jnp.zeros((HALO, D), F32)

        @pl.when(i > 0)
        def _():
            cext[0:HALO, :] = cext[TE:TE + HALO, :]

        def glu(r0):
            c2 = c2_ref[pl.ds(r0, 32), :].astype(F32)
            cext[pl.ds(HALO + r0, 32), :] = c1_ref[pl.ds(r0, 32), :].astype(F32) * _sigmoid(c2)
        _row_loop(TE, 32, glu)

        def conv(r0):
            win = cext[pl.ds(r0, 16 + HALO), :]
            acc = jnp.zeros((16, D), F32) + cb_ref[...]
            for j in range(CONF_K):
                acc = acc + cw_ref[j:j + 1, :] * win[2 + j:2 + j + 16, :]
            cc_ref[pl.ds(r0, 16), :] = acc
            mu = jnp.mean(acc, axis=-1, keepdims=True)
            xc = acc - mu
            var = jnp.mean(xc * xc, axis=-1, keepdims=True)
            ln = xc * lax.rsqrt(var + LN_EPS) * lg_ref[...] + lb_ref[...]
            cs_ref[pl.ds(r0, 16), :] = (ln * _sigmoid(ln)).astype(BF16)
        _row_loop(TE, 16, conv)

    return pl.pallas_call(
        body, name="conf_fwd", grid=(nt,),
        in_specs=[pl.BlockSpec((TE, D), lambda i: (i, 2)), pl.BlockSpec((TE, D), lambda i: (i, 3)),
                  _resident((32, D)), _resident((1, D)), _resident((1, D)), _resident((1, D))],
        out_specs=[pl.BlockSpec((TE, D), lambda i: (i, 0)), pl.BlockSpec((TE, D), lambda i: (i, 0))],
        out_shape=[SDS((t, D), F32), SDS((t, D), BF16)],
        scratch_shapes=[pltpu.VMEM((TE + HALO, D), F32)], compiler_params=_ARB1)(proj, proj, cw, cb, lg, lb)


def _mix_fwd(og, cs, proj, h0, wg, wc, wo):
    t = h0.shape[0]
    nt = t // TE

    def body(og_ref, cs_ref, g_ref, h0_ref, wg_ref, wc_ref, wo_ref, brg_ref, brc_ref, mg_ref, h1_ref, acc):
        acc[...] = _dot(og_ref[...], wg_ref[...])
        brg_ref[...] = acc[...].astype(BF16)
        acc[...] = _dot(cs_ref[...], wc_ref[...])
        brc_ref[...] = acc[...].astype(BF16)

        def blk(r0):
            rows = pl.ds(r0, 32)
            gg = g_ref[rows, 0:D].astype(F32)
            gc = g_ref[rows, D:2 * D].astype(F32)
            m = _sigmoid(gg) * brg_ref[rows, :].astype(F32) + _sigmoid(gc) * brc_ref[rows, :].astype(F32)
            mg_ref[rows, :] = m.astype(BF16)
        _row_loop(TE, 32, blk)
        h1_ref[...] = h0_ref[...] + _dot(mg_ref[...], wo_ref[...])

    row = lambda w: pl.BlockSpec((TE, w), lambda i: (i, 0))
    return pl.pallas_call(
        body, name="mix_fwd", grid=(nt,),
        in_specs=[row(D), row(D), pl.BlockSpec((TE, 2 * D), lambda i: (i, 2)), row(D),
                  _resident((D, D)), _resident((D, D)), _resident((D, D))],
        out_specs=[row(D), row(D), row(D), row(D)],
        out_shape=[SDS((t, D), BF16), SDS((t, D), BF16), SDS((t, D), BF16), SDS((t, D), F32)],
        scratch_shapes=[pltpu.VMEM((TE, D), F32)], compiler_params=_ARB1)(og, cs, proj, h0, wg, wc, wo)


def _ffn_out(up, fw, fb, h1, wd, gf, tgt, n_real):
    t = h1.shape[0]
    nt = t // TE

    def body(a_ref, bv_ref, fw_ref, fb_ref, h1_ref, wd_ref, gf_ref, tg_ref, f_ref, dh2_ref, red_ref, aext, hs):
        i = pl.program_id(0)

        @pl.when(i == 0)
        def _():
            aext[0:HALO_F, :] = jnp.zeros((HALO_F, DFF), F32)
            red_ref[...] = jnp.zeros_like(red_ref)

        @pl.when(i > 0)
        def _():
            aext[0:HALO_F, :] = aext[TE:TE + HALO_F, :]

        def cp(r0):
            aext[pl.ds(HALO_F + r0, 16), :] = a_ref[pl.ds(r0, 16), :].astype(F32)
        _row_loop(TE, 16, cp)

        def conv(r0):
            win = aext[pl.ds(r0, 32), :]
            ac = fb_ref[...] + fw_ref[0:1, :] * win[14:30, :] + fw_ref[1:2, :] * win[15:31, :] + fw_ref[2:3, :] * win[16:32, :]
            f_ref[pl.ds(r0, 16), :] = (ac * _sigmoid(ac) * bv_ref[pl.ds(r0, 16), :].astype(F32)).astype(BF16)
        _row_loop(TE, 16, conv)
        hs[...] = h1_ref[...] + _dot(f_ref[...], wd_ref[...])

        def head(r0):
            rows = pl.ds(r0, 32)
            h2 = hs[rows, :]
            rinv = lax.rsqrt(jnp.mean(h2 * h2, axis=-1, keepdims=True) + RMS_EPS)
            hh = h2 * rinv
            gid = i * TE + r0 + lax.broadcasted_iota(jnp.int32, (32, 1), 0)
            live = jnp.logical_and(gid >= N_META, gid < n_real)
            err = jnp.where(live, hh * gf_ref[...] - tg_ref[rows, :], 0.0)
            dy = err * (1.0 / D)
            red_ref[0:8, :] += _rows8(err * err)
            red_ref[8:16, :] += _rows8(dy * hh)
            dhh = dy * gf_ref[...]
            dh2_ref[rows, :] = rinv * (dhh - hh * jnp.mean(dhh * hh, axis=-1, keepdims=True))
        _row_loop(TE, 32, head)

    row = lambda w: pl.BlockSpec((TE, w), lambda i: (i, 0))
    return pl.pallas_call(
        body, name="ffn_out", grid=(nt,),
        in_specs=[pl.BlockSpec((TE, DFF), lambda i: (i, 0)), pl.BlockSpec((TE, DFF), lambda i: (i, 1)),
                  _resident((8, DFF)), _resident((1, DFF)), row(D), _resident((DFF, D)), _resident((1, D)), row(D)],
        out_specs=[row(DFF), row(D), _resident((16, D))],
        out_shape=[SDS((t, DFF), BF16), SDS((t, D), F32), SDS((16, D), F32)],
        scratch_shapes=[pltpu.VMEM((TE + HALO_F, DFF), F32), pltpu.VMEM((TE, D), F32)],
        compiler_params=_ARB1)(up, up, fw, fb, h1, wd, gf, tgt)


def _ffn_bwd(dh2, wd, up, fw, fb):
    t = dh2.shape[0]
    nt = t // TE
    hb = TE // HALO_F

    def body(dh_ref, wd_ref, a_ref, ah_ref, bv_ref, fw_ref, fb_ref, dup_ref, dw_ref, aext, dax, dfs):
        i = pl.program_id(0)
        ti = nt - 1 - i

        @pl.when(i == 0)
        def _():
            dax[TE:TE + HALO_F, :] = jnp.zeros((HALO_F, DFF), F32)
            dw_ref[...] = jnp.zeros_like(dw_ref)

        @pl.when(i > 0)
        def _():
            dax[TE:TE + HALO_F, :] = dax[0:HALO_F, :]

        aext[0:HALO_F, :] = jnp.where(ti > 0, ah_ref[...].astype(F32), 0.0)
        dfs[...] = _dot_nt(dh_ref[...].astype(BF16), wd_ref[...])

        def cp(r0):
            aext[pl.ds(HALO_F + r0, 16), :] = a_ref[pl.ds(r0, 16), :].astype(F32)
        _row_loop(TE, 16, cp)

        def act(r0):
            rows = pl.ds(r0, 16)
            win = aext[pl.ds(r0, 32), :]
            ac = fb_ref[...] + fw_ref[0:1, :] * win[14:30, :] + fw_ref[1:2, :] * win[15:31, :] + fw_ref[2:3, :] * win[16:32, :]
            sg = _sigmoid(ac)
            df = dfs[rows, :]
            dup_ref[rows, DFF:2 * DFF] = (df * ac * sg).astype(BF16)
            dac = df * bv_ref[rows, :].astype(F32) * sg * (1.0 + ac * (1.0 - sg))
            dax[rows, :] = dac
            dw_ref[3] += _rows8(dac)
            for j in range(FFN_K):
                dw_ref[j] += _rows8(dac * win[14 + j:30 + j, :])
        _row_loop(TE, 16, act)

        def convt(r0):
            win = dax[pl.ds(r0, 32), :]
            da = fw_ref[2:3, :] * win[0:16, :] + fw_ref[1:2, :] * win[1:17, :] + fw_ref[0:1, :] * win[2:18, :]
            dup_ref[pl.ds(r0, 16), 0:DFF] = da.astype(BF16)
        _row_loop(TE, 16, convt)

    rev = lambda w: pl.BlockSpec((TE, w), lambda i: (nt - 1 - i, 0))
    return pl.pallas_call(
        body, name="ffn_bwd", grid=(nt,),
        in_specs=[rev(D), _resident((DFF, D)), rev(DFF),
                  pl.BlockSpec((HALO_F, DFF), lambda i: (jnp.maximum((nt - 1 - i) * hb - 1, 0), 0)),
                  pl.BlockSpec((TE, DFF), lambda i: (nt - 1 - i, 1)), _resident((8, DFF)), _resident((1, DFF))],
        out_specs=[rev(2 * DFF), _resident((4, 8, DFF))],
        out_shape=[SDS((t, 2 * DFF), BF16), SDS((4, 8, DFF), F32)],
        scratch_shapes=[pltpu.VMEM((TE + HALO_F, DFF), F32), pltpu.VMEM((TE + HALO_F, DFF), F32),
                        pltpu.VMEM((TE, DFF), F32)],
        compiler_params=_ARB1)(dh2, wd, up, up, up, fw, fb)


def _dgrad_norm(dy, w, tk, h, g, dres, name, dy_extra=None, w_extra=None):
    t, k = dy.shape
    nt, nk = t // TM, k // tk

    def body(*refs):
        if dy_extra is None:
            dy_ref, w_ref, h_ref, g_ref, dr_ref, dh_ref, dg_ref, acc = refs
        else:
            dy_ref, w_ref, h_ref, g_ref, dr_ref, de_ref, we_ref, dh_ref, dg_ref, acc = refs
        i, j = pl.program_id(0), pl.program_id(1)

        @pl.when(jnp.logical_and(i == 0, j == 0))
        def _():
            dg_ref[...] = jnp.zeros_like(dg_ref)

        @pl.when(j == 0)
        def _():
            if dy_extra is None:
                acc[...] = jnp.zeros_like(acc)
            else:
                acc[...] = _dot_nt(de_ref[...], we_ref[...])

        acc[...] += _dot_nt(dy_ref[...], w_ref[...])

        @pl.when(j == nk - 1)
        def _():
            def blk(r0):
                rows = pl.ds(r0, 32)
                x = h_ref[rows, :]
                rinv = lax.rsqrt(jnp.mean(x * x, axis=-1, keepdims=True) + RMS_EPS)
                hh = x * rinv
                du = acc[rows, :]
                dg_ref[...] += _rows8(du * hh)
                dhh = du * g_ref[...]
                dh_ref[rows, :] = dr_ref[rows, :] + rinv * (dhh - hh * jnp.mean(dhh * hh, axis=-1, keepdims=True))
            _row_loop(TM, 32, blk)

    row = pl.BlockSpec((TM, D), lambda i, j: (i, 0))
    in_specs = [pl.BlockSpec((TM, tk), lambda i, j: (i, j)), pl.BlockSpec((D, tk), lambda i, j: (0, j)), row, _resident((1, D)), row]
    args = [dy, w, h, g, dres]
    if dy_extra is not None:
        in_specs += [pl.BlockSpec((TM, dy_extra.shape[1]), lambda i, j: (i, 0)), _resident(w_extra.shape)]
        args += [dy_extra, w_extra]
    return pl.pallas_call(
        body, name=name, grid=(nt, nk), in_specs=in_specs, out_specs=[row, _resident((8, D))],
        out_shape=[SDS((t, D), F32), SDS((8, D), F32)],
        scratch_shapes=[pltpu.VMEM((TM, D), F32)], compiler_params=_ARB2)(*args)


def _wgrad(x, dy, tn, name):
    t, k = x.shape
    n = dy.shape[1]
    nn, nt = n // tn, t // TM

    def body(x_ref, dy_ref, o_ref):
        @pl.when(pl.program_id(1) == 0)
        def _():
            o_ref[...] = jnp.zeros_like(o_ref)
        o_ref[...] += _dot_tn(x_ref[...], dy_ref[...].astype(BF16))

    return pl.pallas_call(
        body, name=name, grid=(nn, nt),
        in_specs=[pl.BlockSpec((TM, k), lambda j, i: (i, 0)), pl.BlockSpec((TM, tn), lambda j, i: (i, j))],
        out_specs=pl.BlockSpec((k, tn), lambda j, i: (0, j)), out_shape=SDS((k, n), F32),
        compiler_params=_ARB2)(x, dy)


def _mix_bwd(dh1, wo, wg, wc, proj, brg, brc):
    t = dh1.shape[0]
    nt = t // TE

    def body(dh_ref, wo_ref, wg_ref, wc_ref, g_ref, brg_ref, brc_ref, dbg_ref, dbc_ref, dog_ref, dcs_ref, dp_ref, dm):
        dm[...] = _dot_nt(dh_ref[...].astype(BF16), wo_ref[...])

        def blk(r0):
            rows = pl.ds(r0, 32)
            d = dm[rows, :]
            sg = _sigmoid(g_ref[rows, 0:D].astype(F32))
            sc = _sigmoid(g_ref[rows, D:2 * D].astype(F32))
            dbg_ref[rows, :] = (d * sg).astype(BF16)
            dbc_ref[rows, :] = (d * sc).astype(BF16)
            dp_ref[rows, 0:D] = (d * brg_ref[rows, :].astype(F32) * sg * (1.0 - sg)).astype(BF16)
            dp_ref[rows, D:2 * D] = (d * brc_ref[rows, :].astype(F32) * sc * (1.0 - sc)).astype(BF16)
        _row_loop(TE, 32, blk)
        dog_ref[...] = _dot_nt(dbg_ref[...], wg_ref[...]).astype(BF16)
        dcs_ref[...] = _dot_nt(dbc_ref[...], wc_ref[...]).astype(BF16)

    row = pl.BlockSpec((TE, D), lambda i: (i, 0))
    wide = pl.BlockSpec((TE, 2 * D), lambda i: (i, 2))
    return pl.pallas_call(
        body, name="mix_bwd", grid=(nt,),
        in_specs=[row, _resident((D, D)), _resident((D, D)), _resident((D, D)), wide, row, row],
        out_specs=[row, row, row, row, wide],
        out_shape=[SDS((t, D), BF16)] * 4 + [SDS((t, NPROJ), BF16)],
        scratch_shapes=[pltpu.VMEM((TE, D), F32)], compiler_params=_ARB1)(dh1, wo, wg, wc, proj, brg, brc)


def _glapost_bwd(dog, o, proj, gn, dproj):
    t = o.shape[0]
    nt = t // TE

    def body(dog_ref, o_ref, r_ref, gn_ref, dp_in, do_ref, dp_ref, dgn_ref):
        del dp_in

        @pl.when(pl.program_id(0) == 0)
        def _():
            dgn_ref[...] = jnp.zeros_like(dgn_ref)

        def blk(r0):
            rows = pl.ds(r0, 32)
            for h in range(HEADS):
                vs = slice(h * DVH, (h + 1) * DVH)
                x = o_ref[rows, vs]
                rinv = lax.rsqrt(jnp.mean(x * x, axis=-1, keepdims=True) + RMS_EPS)
                oh = x * rinv
                g = gn_ref[:, vs]
                rr = r_ref[rows, vs].astype(F32)
                sr = _sigmoid(rr)
                d = dog_ref[rows, vs].astype(F32)
                dp_ref[rows, vs] = (d * oh * g * sr * (1.0 + rr * (1.0 - sr))).astype(BF16)
                don = d * rr * sr
                dgn_ref[:, vs] += _rows8(don * oh)
                doh = don * g
                do_ref[rows, vs] = (rinv * (doh - oh * jnp.mean(doh * oh, axis=-1, keepdims=True))).astype(BF16)
        _row_loop(TE, 32, blk)

    row = pl.BlockSpec((TE, D), lambda i: (i, 0))
    rcol = pl.BlockSpec((TE, D), lambda i: (i, 6))
    return pl.pallas_call(
        body, name="glapost_bwd", grid=(nt,),
        in_specs=[row, row, rcol, _resident((1, D)), pl.BlockSpec(memory_space=pl.ANY)],
        out_specs=[row, rcol, _resident((8, D))],
        out_shape=[SDS((t, D), BF16), SDS((t, NPROJ), BF16), SDS((8, D), F32)],
        input_output_aliases={4: 1}, compiler_params=_ARB1)(dog, o, proj, gn, dproj)


def _conf_bwd(dcs, cc, proj, cw, lg, lb, dproj):
    t = cc.shape[0]
    nt = t // TE
    hb = TE // HALO

    def body(dcs_ref, cc_ref, c1_ref, c2_ref, c1h_ref, c2h_ref, cw_ref, lg_ref, lb_ref, dp_in,
             dp_ref, dw_ref, ds_ref, cext, dext):
        del dp_in
        i = pl.program_id(0)
        ti = nt - 1 - i

        @pl.when(i == 0)
        def _():
            dext[TE:TE + HALO, :] = jnp.zeros((HALO, D), F32)
            dw_ref[...] = jnp.zeros_like(dw_ref)
            ds_ref[...] = jnp.zeros_like(ds_ref)

        @pl.when(i > 0)
        def _():
            dext[TE:TE + HALO, :] = dext[0:HALO, :]

        ch = c1h_ref[...].astype(F32) * _sigmoid(c2h_ref[...].astype(F32))
        cext[0:HALO, :] = jnp.where(ti > 0, ch, 0.0)

        def pre(r0):
            rows = pl.ds(r0, 32)
            cext[pl.ds(HALO + r0, 32), :] = c1_ref[rows, :].astype(F32) * _sigmoid(c2_ref[rows, :].astype(F32))
            x = cc_ref[rows, :]
            mu = jnp.mean(x, axis=-1, keepdims=True)
            xc = x - mu
            rstd = lax.rsqrt(jnp.mean(xc * xc, axis=-1, keepdims=True) + LN_EPS)
            xh = xc * rstd
            ln = xh * lg_ref[...] + lb_ref[...]
            sg = _sigmoid(ln)
            dln = dcs_ref[rows, :].astype(F32) * sg * (1.0 + ln * (1.0 - sg))
            ds_ref[0] += _rows8(dln * xh)
            ds_ref[1] += _rows8(dln)
            dxh = dln * lg_ref[...]
            dcc = rstd * (dxh - jnp.mean(dxh, axis=-1, keepdims=True) - xh * jnp.mean(dxh * xh, axis=-1, keepdims=True))
            dext[rows, :] = dcc
            ds_ref[2] += _rows8(dcc)
        _row_loop(TE, 32, pre)

        def convt(r0):
            rows = pl.ds(r0, 16)
            wd = dext[pl.ds(r0, 16 + HALO), :]
            wc = cext[pl.ds(r0, 16 + HALO), :]
            dcc = wd[0:16, :]
            dc = jnp.zeros((16, D), F32)
            for j in range(CONF_K):
                dc = dc + cw_ref[j:j + 1, :] * wd[30 - j:46 - j, :]
                dw_ref[j] += _rows8(dcc * wc[2 + j:18 + j, :])
            c1 = c1_ref[rows, :].astype(F32)
            s2 = _sigmoid(c2_ref[rows, :].astype(F32))
            dp_ref[rows, 0:D] = (dc * s2).astype(BF16)
            dp_ref[rows, D:2 * D] = (dc * c1 * s2 * (1.0 - s2)).astype(BF16)
        _row_loop(TE, 16, convt)

    rev = lambda col: pl.BlockSpec((TE, D), lambda i: (nt - 1 - i, col))
    halo = lambda col: pl.BlockSpec((HALO, D), lambda i: (jnp.maximum((nt - 1 - i) * hb - 1, 0), col))
    return pl.pallas_call(
        body, name="conf_bwd", grid=(nt,),
        in_specs=[rev(0), rev(0), rev(2), rev(3), halo(2), halo(3), _resident((32, D)), _resident((1, D)),
                  _resident((1, D)), pl.BlockSpec(memory_space=pl.ANY)],
        out_specs=[pl.BlockSpec((TE, 2 * D), lambda i: (nt - 1 - i, 1)), _resident((32, 8, D)), _resident((3, 8, D))],
        out_shape=[SDS((t, NPROJ), BF16), SDS((32, 8, D), F32), SDS((3, 8, D), F32)],
        scratch_shapes=[pltpu.VMEM((TE + HALO, D), F32), pltpu.VMEM((TE + HALO, D), F32)],
        input_output_aliases={9: 0}, compiler_params=_ARB1)(dcs, cc, proj, proj, proj, proj, cw, lg, lb, dproj)


def _gla_bwd(proj, alr, wau, balpha, do, sall, dproj):
    t = proj.shape[0]
    nc = t // CH

    def body(qk_ref, v_ref, a_ref, wau_ref, ba_ref, do_ref, s_ref, dp_in, dp_ref, da_ref, dwau_ref, dba_ref, ds_scr, dla_scr):
        del dp_in

        @pl.when(pl.program_id(0) == 0)
        def _():
            ds_scr[...] = jnp.zeros_like(ds_scr)
            dwau_ref[...] = jnp.zeros_like(dwau_ref)
            dba_ref[...] = jnp.zeros_like(dba_ref)

        z, b, bmid, blast, causal = _gla_decay(a_ref, wau_ref, ba_ref)
        r = lax.broadcasted_iota(jnp.int32, (CH, CH), 0)
        c = lax.broadcasted_iota(jnp.int32, (CH, CH), 1)
        upper = (r <= c).astype(F32)
        for h in range(HEADS):
            ks = slice(h * DKH, (h + 1) * DKH)
            vs = slice(h * DVH, (h + 1) * DVH)
            bh, mh, lh = b[:, ks], bmid[:, ks], blast[:, ks]
            q = qk_ref[:, ks].astype(F32) * (DKH ** -0.5)
            k = qk_ref[:, DK + h * DKH:DK + (h + 1) * DKH].astype(F32)
            v = v_ref[:, vs]
            dout = do_ref[:, vs]
            eq, ek, eb, eg, el = jnp.exp(bh - mh), jnp.exp(mh - bh), jnp.exp(bh), jnp.exp(lh - bh), jnp.exp(lh)
            qt, kt = (q * eq).astype(BF16), (k * ek).astype(BF16)
            qg, kg = (q * eb).astype(BF16), (k * eg).astype(BF16)
            st = s_ref[0, vs, :]
            dsn = ds_scr[vs, :]
            st16, dsn16 = st.astype(BF16), dsn.astype(BF16)
            a = jnp.where(causal, _dot_nt(qt, kt), 0.0).astype(BF16)
            da = jnp.where(causal, _dot_nt(dout, v), 0.0).astype(BF16)
            dq_inter = _dot(dout, st16) * eb
            dk_inter = _dot(v, dsn16) * eg
            dq = _dot(da, kt) * eq + dq_inter
            dk = _dot_tn(da, qt) * ek + dk_inter
            dv = _dot_tn(a, dout) + _dot_nt(kg, dsn16)
            dlast = jnp.sum(k * dk_inter, axis=0, keepdims=True) + jnp.sum(st * dsn, axis=0, keepdims=True) * el[0:1, :]
            db = q * dq - k * dk
            dla_scr[:, ks] = jnp.dot(upper, db, precision=lax.Precision.HIGHEST, preferred_element_type=F32) + dlast
            ds_scr[vs, :] = dsn * jnp.concatenate([el, el], axis=0) + _dot_tn(dout, qg)
            dp_ref[:, ks] = (dq * (DKH ** -0.5)).astype(BF16)
            dp_ref[:, DK + h * DKH:DK + (h + 1) * DKH] = dk.astype(BF16)
            dp_ref[:, D + h * DVH:D + (h + 1) * DVH] = dv.astype(BF16)
        dz = (dla_scr[...] * (1.0 / TAU) * _sigmoid(-z)).astype(BF16)
        da_ref[...] = _dot_nt(dz, wau_ref[...]).astype(BF16)
        dwau_ref[...] += _dot_tn(a_ref[...], dz)
        dba_ref[...] += _rows8(dz.astype(F32))

    rev = lambda w, col: pl.BlockSpec((CH, w), lambda c: (nc - 1 - c, col))
    return pl.pallas_call(
        body, name="gla_bwd", grid=(nc,),
        in_specs=[rev(D, 0), rev(D, 1), rev(LANES, 0), _resident((LANES, DK)), _resident((1, DK)), rev(D, 0),
                  pl.BlockSpec((1, DV, DKH), lambda c: (nc - 1 - c, 0, 0)), pl.BlockSpec(memory_space=pl.ANY)],
        out_specs=[rev(2 * D, 0), rev(LANES, 0), _resident((LANES, DK)), _resident((8, DK))],
        out_shape=[SDS((t, NPROJ), BF16), SDS((t, LANES), BF16), SDS((LANES, DK), F32), SDS((8, DK), F32)],
        scratch_shapes=[pltpu.VMEM((DV, DKH), F32), pltpu.VMEM((CH, DK), F32)],
        input_output_aliases={7: 0}, compiler_params=_ARB1)(proj, proj, alr, wau, balpha, do, sall, dproj)


def _my_place():
    return lax.axis_index("x"), lax.axis_index("y"), lax.axis_index("c")


def _flip(v, bit):
    return 1 - v if bit else v


def _all_gather(xs, name):
    rows, cols = xs.shape

    def body(x_ref, out_ref, send_sems, recv_sems, local_sem):
        x, y, c = _my_place()
        me, sibling = (x, y, c), (x, y, 1 - c)
        chips = [(1 - x, y), (x, 1 - y), (1 - x, 1 - y)]

        def slot(px, py, pc):
            return out_ref.at[4 * px + 2 * py + pc]

        def copy(k, block, to, src=None):
            return pltpu.make_async_remote_copy(
                src_ref=slot(*block) if src is None else src, dst_ref=slot(*block),
                send_sem=send_sems.at[k], recv_sem=recv_sems.at[k], device_id=to, device_id_type=MESH_T)

        mine = pltpu.make_async_copy(x_ref, slot(*me), local_sem)
        mine.start()
        first = [copy(0, me, sibling, src=x_ref)]
        first += [copy(1 + j, me, (*chip, c), src=x_ref) for j, chip in enumerate(chips)]
        for cp in first:
            cp.start()
        passed = [copy(4 + j, (*chip, c), sibling) for j, chip in enumerate(chips)]
        for j, chip in enumerate(chips):
            copy(1 + j, (*chip, c), me).wait_recv()
            passed[j].start()
        copy(0, sibling, me).wait_recv()
        for j, chip in enumerate(chips):
            copy(4 + j, (*chip, 1 - c), me).wait_recv()
        for cp in first + passed:
            cp.wait_send()
        mine.wait()

    return pl.pallas_call(
        body, name=name, out_shape=SDS((N_DEV, rows, cols), xs.dtype),
        in_specs=[pl.BlockSpec(memory_space=pl.ANY)], out_specs=pl.BlockSpec(memory_space=pl.ANY),
        scratch_shapes=[pltpu.SemaphoreType.DMA((7,)), pltpu.SemaphoreType.DMA((7,)), pltpu.SemaphoreType.DMA(())])(xs)


def _exchange(g):
    _, rows, cols = g.shape

    def body(g_ref, land_ref, send_sems, recv_sems, local_sem):
        x, y, c = _my_place()
        my_idx = 4 * x + 2 * y + c
        mine = pltpu.make_async_copy(g_ref.at[my_idx], land_ref.at[my_idx], local_sem)
        mine.start()
        copies = []
        for k in range(1, N_DEV):
            px, py, pc = _flip(x, k & 4), _flip(y, k & 2), _flip(c, k & 1)
            p_idx = 4 * px + 2 * py + pc
            cp = pltpu.make_async_remote_copy(
                src_ref=g_ref.at[p_idx], dst_ref=land_ref.at[my_idx], send_sem=send_sems.at[k - 1],
                recv_sem=recv_sems.at[k - 1], device_id=(px, py, pc), device_id_type=MESH_T)
            cp.start()
            arrival = pltpu.make_async_remote_copy(
                src_ref=g_ref.at[p_idx], dst_ref=land_ref.at[p_idx], send_sem=send_sems.at[k - 1],
                recv_sem=recv_sems.at[k - 1], device_id=(px, py, pc), device_id_type=MESH_T)
            copies.append((cp, arrival))
        for cp, arrival in copies:
            arrival.wait_recv()
        for cp, arrival in copies:
            cp.wait_send()
        mine.wait()

    return pl.pallas_call(
        body, name="grad_exchange", out_shape=SDS((N_DEV, rows, cols), g.dtype),
        in_specs=[pl.BlockSpec(memory_space=pl.ANY)], out_specs=pl.BlockSpec(memory_space=pl.ANY),
        scratch_shapes=[pltpu.SemaphoreType.DMA((7,)), pltpu.SemaphoreType.DMA((7,)), pltpu.SemaphoreType.DMA(())])(g)


ADAM_ROWS = 512


def _adamw(land, w, m, v):
    rows = w.shape[0]

    def body(l_ref, w_ref, m_ref, v_ref, g_ref, d_ref, nm_ref, nv_ref):
        g = l_ref[0]
        for s in range(1, N_DEV):
            g = g + l_ref[s]
        nm = ADAM_B1 * m_ref[...] + (1.0 - ADAM_B1) * g
        nv = ADAM_B2 * v_ref[...] + (1.0 - ADAM_B2) * (g * g)
        m_hat = nm / (1.0 - ADAM_B1 ** ADAM_STEP)
        v_hat = nv / (1.0 - ADAM_B2 ** ADAM_STEP)
        g_ref[...] = g
        d_ref[...] = -ADAM_LR * (m_hat / (jnp.sqrt(v_hat) + ADAM_EPS) + ADAM_WD * w_ref[...])
        nm_ref[...] = nm
        nv_ref[...] = nv

    blk = pl.BlockSpec((ADAM_ROWS, LANES), lambda i: (i, 0))
    return pl.pallas_call(
        body, name="adamw", grid=(rows // ADAM_ROWS,),
        in_specs=[pl.BlockSpec((N_DEV, ADAM_ROWS, LANES), lambda i: (0, i, 0)), blk, blk, blk],
        out_specs=[blk] * 4, out_shape=[SDS((rows, LANES), F32)] * 4, compiler_params=_ARB1)(land, w, m, v)


SHARDED = (("w_in", "col"), ("w_up", "col"), ("w_down", "row"), ("w_gla_o", "row"), ("w_conf_o", "row"), ("w_out", "row"),
           ("w_alpha_up", "col"), ("meta_tokens", "col"), ("conf_dw_w", "col"), ("ffn_dw_w", "col"))
REPLICATED = ("norm_mix_g", "b_alpha", "gla_norm_g", "conf_dw_b", "conf_ln_g", "conf_ln_b", "norm_ffn_g", "ffn_dw_b",
              "final_norm_g")
GATHER_BF16 = ("w_in", "w_up", "w_down", "w_gla_o", "w_conf_o", "w_out", "w_alpha_up")
GATHER_F32 = ("meta_tokens", "conf_dw_w", "ffn_dw_w")
PIECE = 1024


def _pad_to(v, mult):
    return jnp.pad(v, (0, (-v.shape[0]) % mult))


def _pack(arrs, total_mult):
    flat = jnp.concatenate([_pad_to(a.reshape(-1), PIECE) for a in arrs])
    return _pad_to(flat, total_mult).reshape(-1, LANES)


def _unpack(panel, shapes):
    flat, out, off = panel.reshape(-1), [], 0
    for shp in shapes:
        n = 1
        for s in shp:
            n *= s
        out.append(flat[off:off + n].reshape(shp))
        off += n + (-n) % PIECE
    return out


def _to_blocks(full, kind):
    if kind == "col":
        k, n = full.shape
        return full.reshape(k, N_DEV, n // N_DEV).transpose(1, 0, 2).reshape(N_DEV, -1)
    return full.reshape(N_DEV, -1)


def _from_blocks(blocks, kind, shard2d):
    k, n = shard2d
    if kind == "col":
        return blocks.reshape(N_DEV, k, n).transpose(1, 0, 2).reshape(k, N_DEV * n)
    return blocks.reshape(N_DEV * k, n)


def _shard2d(a):
    return a.reshape(-1, a.shape[-1]).shape


def _local_step(x, target, w):
    s = x.shape[0]
    n_real = s + N_META
    t = -(-n_real // TM) * TM
    h0 = jnp.concatenate([w["meta_tokens"], x, jnp.zeros((t - n_real, D), F32)], axis=0)
    tgt = jnp.concatenate([jnp.zeros((N_META, D), F32), target, jnp.zeros((t - n_real, D), F32)], axis=0)

    q_, k_, v_, r_, a_, c_, gg_, gc_ = jnp.split(w["w_in"], [sum(IN_WIDTHS[:i + 1]) for i in range(7)], axis=1)
    w_main = jnp.concatenate([q_, k_, v_, c_, gg_, gc_, r_], axis=1)
    w_a = jnp.pad(a_, ((0, 0), (0, LANES - RANK)))
    wau = jnp.pad(w["w_alpha_up"], ((0, LANES - RANK), (0, 0)))
    row = lambda name: w[name].reshape(1, -1)
    cw = jnp.pad(w["conf_dw_w"], ((0, 32 - CONF_K), (0, 0)))
    fw = jnp.pad(w["ffn_dw_w"], ((0, 8 - FFN_K), (0, 0)))

    u1, proj, alr = _norm_matmul(h0, row("norm_mix_g"), w_main, 1024, "in_proj", w_extra=w_a)
    o, og, sall = _gla_fwd(proj, alr, wau, row("b_alpha"), row("gla_norm_g"))
    cc, cs = _conf_fwd(proj, cw, row("conf_dw_b"), row("conf_ln_g"), row("conf_ln_b"))
    brg, brc, merged, h1 = _mix_fwd(og, cs, proj, h0, w["w_gla_o"], w["w_conf_o"], w["w_out"])
    u2, up = _norm_matmul(h1, row("norm_ffn_g"), w["w_up"], 512, "up_proj")
    f, dh2, red = _ffn_out(up, fw, row("ffn_dw_b"), h1, w["w_down"], row("final_norm_g"), tgt, n_real)
    loss = 0.5 / D * jnp.sum(red[0:8])

    g = {"final_norm_g": jnp.sum(red[8:16], axis=0)}
    dup, dfw = _ffn_bwd(dh2, w["w_down"], up, fw, row("ffn_dw_b"))
    g["ffn_dw_w"] = jnp.sum(dfw[0:FFN_K], axis=1)
    g["ffn_dw_b"] = jnp.sum(dfw[3], axis=0)
    g["w_down"] = _wgrad(f, dh2, 512, "wgrad_down")
    dh1, dg2 = _dgrad_norm(dup, w["w_up"], 512, h1, row("norm_ffn_g"), dh2, "up_dgrad")
    g["norm_ffn_g"] = jnp.sum(dg2, axis=0)
    g["w_up"] = _wgrad(u2, dup, 512, "wgrad_up")
    dbrg, dbrc, dog, dcs, dproj = _mix_bwd(dh1, w["w_out"], w["w_gla_o"], w["w_conf_o"], proj, brg, brc)
    g["w_out"] = _wgrad(merged, dh1, 512, "wgrad_out")
    g["w_gla_o"] = _wgrad(og, dbrg, 512, "wgrad_gla_o")
    g["w_conf_o"] = _wgrad(cs, dbrc, 512, "wgrad_conf_o")
    do, dproj, dgn = _glapost_bwd(dog, o, proj, row("gla_norm_g"), dproj)
    g["gla_norm_g"] = jnp.sum(dgn, axis=0)
    dproj, dcw, dst = _conf_bwd(dcs, cc, proj, cw, row("conf_ln_g"), row("conf_ln_b"), dproj)
    g["conf_dw_w"] = jnp.sum(dcw[0:CONF_K], axis=1)
    g["conf_ln_g"], g["conf_ln_b"], g["conf_dw_b"] = jnp.sum(dst[0], axis=0), jnp.sum(dst[1], axis=0), jnp.sum(dst[2], axis=0)
    dproj, dalr, dwau, dba = _gla_bwd(proj, alr, wau, row("b_alpha"), do, sall, dproj)
    g["w_alpha_up"] = dwau[0:RANK]
    g["b_alpha"] = jnp.sum(dba, axis=0)
    dh0, dg1 = _dgrad_norm(dproj, w_main, 1024, h0, row("norm_mix_g"), dh1, "in_dgrad", dy_extra=dalr, w_extra=w_a)
    g["norm_mix_g"] = jnp.sum(dg1, axis=0)
    dw_main = _wgrad(u1, dproj, 512, "wgrad_in")
    dw_a = _wgrad(u1, dalr, LANES, "wgrad_alr")
    pieces = jnp.split(dw_main, [DK, 2 * DK, 2 * D, 4 * D, 5 * D, 6 * D], axis=1)
    g["w_in"] = jnp.concatenate([pieces[0], pieces[1], pieces[2], pieces[6], dw_a[:, 0:RANK], pieces[3], pieces[4],
                                 pieces[5]], axis=1)
    g["meta_tokens"] = dh0[0:N_META]
    return loss, dh0[N_META:n_real], g


def kernel(x, meta_tokens, norm_mix_g, w_in, w_alpha_up, b_alpha, gla_norm_g, w_gla_o, conf_dw_w, conf_dw_b, conf_ln_g, conf_ln_b, w_conf_o, w_out, norm_ffn_g, w_up, ffn_dw_w, ffn_dw_b, w_down, final_norm_g, loss_target, m_meta_tokens, m_norm_mix_g, m_w_in, m_w_alpha_up, m_b_alpha, m_gla_norm_g, m_w_gla_o, m_conf_dw_w, m_conf_dw_b, m_conf_ln_g, m_conf_ln_b, m_w_conf_o, m_w_out, m_norm_ffn_g, m_w_up, m_ffn_dw_w, m_ffn_dw_b, m_w_down, m_final_norm_g, v_meta_tokens, v_norm_mix_g, v_w_in, v_w_alpha_up, v_b_alpha, v_gla_norm_g, v_w_gla_o, v_conf_dw_w, v_conf_dw_b, v_conf_ln_g, v_conf_ln_b, v_w_conf_o, v_w_out, v_norm_ffn_g, v_w_up, v_ffn_dw_w, v_ffn_dw_b, v_w_down, v_final_norm_g):
    ws = dict(meta_tokens=meta_tokens, norm_mix_g=norm_mix_g, w_in=w_in, w_alpha_up=w_alpha_up, b_alpha=b_alpha,
              gla_norm_g=gla_norm_g, w_gla_o=w_gla_o, conf_dw_w=conf_dw_w, conf_dw_b=conf_dw_b, conf_ln_g=conf_ln_g,
              conf_ln_b=conf_ln_b, w_conf_o=w_conf_o, w_out=w_out, norm_ffn_g=norm_ffn_g, w_up=w_up, ffn_dw_w=ffn_dw_w,
              ffn_dw_b=ffn_dw_b, w_down=w_down, final_norm_g=final_norm_g)
    ms = dict(meta_tokens=m_meta_tokens, norm_mix_g=m_norm_mix_g, w_in=m_w_in, w_alpha_up=m_w_alpha_up, b_alpha=m_b_alpha,
              gla_norm_g=m_gla_norm_g, w_gla_o=m_w_gla_o, conf_dw_w=m_conf_dw_w, conf_dw_b=m_conf_dw_b,
              conf_ln_g=m_conf_ln_g, conf_ln_b=m_conf_ln_b, w_conf_o=m_w_conf_o, w_out=m_w_out, norm_ffn_g=m_norm_ffn_g,
              w_up=m_w_up, ffn_dw_w=m_ffn_dw_w, ffn_dw_b=m_ffn_dw_b, w_down=m_w_down, final_norm_g=m_final_norm_g)
    vs = dict(meta_tokens=v_meta_tokens, norm_mix_g=v_norm_mix_g, w_in=v_w_in, w_alpha_up=v_w_alpha_up, b_alpha=v_b_alpha,
              gla_norm_g=v_gla_norm_g, w_gla_o=v_w_gla_o, conf_dw_w=v_conf_dw_w, conf_dw_b=v_conf_dw_b,
              conf_ln_g=v_conf_ln_g, conf_ln_b=v_conf_ln_b, w_conf_o=v_w_conf_o, w_out=v_w_out, norm_ffn_g=v_norm_ffn_g,
              w_up=v_w_up, ffn_dw_w=v_ffn_dw_w, ffn_dw_b=v_ffn_dw_b, w_down=v_w_down, final_norm_g=v_final_norm_g)
    kinds = dict(SHARDED)
    names = [n for n, _ in SHARDED] + list(REPLICATED)

    full = {n: ws[n].reshape(-1) for n in REPLICATED}
    for group, dtype, mult in ((GATHER_BF16, BF16, 32 * LANES), (GATHER_F32, F32, 8 * LANES)):
        gathered = _all_gather(_pack([ws[n].astype(dtype) for n in group], mult), "gather_" + jnp.dtype(dtype).name)
        off = 0
        flat = gathered.reshape(N_DEV, -1)
        for n in group:
            k, c = _shard2d(ws[n])
            full[n] = _from_blocks(flat[:, off:off + k * c], kinds[n], (k, c))
            off += k * c + (-(k * c)) % PIECE

    loss, grad_x, g = _local_step(x[0], loss_target[0], full)

    blocks = []
    for n, kind in SHARDED:
        b = _to_blocks(g[n], kind)
        blocks.append(jnp.pad(b, ((0, 0), (0, (-b.shape[1]) % PIECE))))
    for n in REPLICATED:
        b = jnp.broadcast_to(g[n].reshape(1, -1), (N_DEV, g[n].size))
        blocks.append(jnp.pad(b, ((0, 0), (0, (-b.shape[1]) % PIECE))))
    gb = jnp.concatenate(blocks, axis=1)
    gb = jnp.pad(gb, ((0, 0), (0, (-gb.shape[1]) % (ADAM_ROWS * LANES)))).reshape(N_DEV, -1, LANES)
    land = _exchange(gb)
    pack_all = lambda d: _pack([d[n] for n in names], ADAM_ROWS * LANES)
    outs = _adamw(land, pack_all(ws), pack_all(ms), pack_all(vs))
    shapes = [ws[n].shape for n in names]
    grad, delta, new_m, new_v = [dict(zip(names, _unpack(p, shapes))) for p in outs]

    order = ("meta_tokens", "norm_mix_g", "w_in", "w_alpha_up", "b_alpha", "gla_norm_g", "w_gla_o", "conf_dw_w", "conf_dw_b",
             "conf_ln_g", "conf_ln_b", "w_conf_o", "w_out", "norm_ffn_g", "w_up", "ffn_dw_w", "ffn_dw_b", "w_down",
             "final_norm_g")
    loss = lax.psum(loss, ("x", "y", "c"))
    return (loss, grad_x[None], *[grad[n] for n in order], *[delta[n] for n in order], *[new_m[n] for n in order],
            *[new_v[n] for n in order])
```

```python
import functools

import jax
import jax.numpy as jnp
from jax import lax
from jax.experimental import pallas as pl
from jax.experimental.pallas import tpu as pltpu

F32, BF16 = jnp.float32, jnp.bfloat16
SDS = jax.ShapeDtypeStruct

D = 1024
N_META = 16
HEADS = 4
DK, DKH, DV, DVH = 512, 128, 1024, 256
RANK = 16
TAU = 16.0
CONF_K = 31
DFF = 2816
FFN_K = 3
IN_WIDTHS = (DK, DK, DV, DV, RANK, 2 * D, D, D)
RMS_EPS, LN_EPS = 1e-6, 1e-5
ADAM_LR, ADAM_B1, ADAM_B2, ADAM_EPS, ADAM_WD, ADAM_STEP = 0.001, 0.9, 0.999, 1e-08, 0.01, 10

NPROJ = 7 * D
LANES = 128
CH = 128
TM = 640
TE = 320
HALO = 32
HALO_F = 16
N_DEV = 8
VMEM_LIMIT = 60 * 1024 * 1024
MESH_T = pl.DeviceIdType.MESH

_ARB1 = pltpu.CompilerParams(dimension_semantics=("arbitrary",), vmem_limit_bytes=VMEM_LIMIT)
_ARB2 = pltpu.CompilerParams(dimension_semantics=("arbitrary", "arbitrary"), vmem_limit_bytes=VMEM_LIMIT)


def _dot(a, b):
    return jnp.dot(a, b, preferred_element_type=F32)


def _dot_nt(a, b):
    return lax.dot_general(a, b, (((1,), (1,)), ((), ())), preferred_element_type=F32)


def _dot_tn(a, b):
    return lax.dot_general(a, b, (((0,), (0,)), ((), ())), preferred_element_type=F32)


def _sigmoid(x):
    return 1.0 / (1.0 + jnp.exp(-x))


def _rows8(x):
    return x.reshape(x.shape[0] // 8, 8, x.shape[1]).sum(axis=0)


def _row_loop(n_rows, rb, fn):
    def step(i, carry):
        fn(pl.multiple_of(i * rb, rb))
        return carry
    lax.fori_loop(0, n_rows // rb, step, 0)


def _resident(shape):
    return pl.BlockSpec(shape, lambda *_: (0,) * len(shape))


def _norm_matmul(h, g, w_t, tn, name, w_extra_t=None):
    t, n = h.shape[0], w_t.shape[0]
    nt, nb = t // TM, n // tn

    def body(*refs):
        if w_extra_t is None:
            h_ref, g_ref, w_ref, u_ref, p_ref = refs
        else:
            h_ref, g_ref, w_ref, we_ref, u_ref, p_ref, e_ref = refs

        @pl.when(pl.program_id(1) == 0)
        def _():
            def blk(r0):
                x = h_ref[pl.ds(r0, 32), :]
                rinv = lax.rsqrt(jnp.mean(x * x, axis=-1, keepdims=True) + RMS_EPS)
                u_ref[pl.ds(r0, 32), :] = (x * rinv * g_ref[...]).astype(BF16)
            _row_loop(TM, 32, blk)
            if w_extra_t is not None:
                e_ref[...] = _dot_nt(u_ref[...], we_ref[...]).astype(BF16)

        p_ref[...] = _dot_nt(u_ref[...], w_ref[...]).astype(BF16)

    in_specs = [pl.BlockSpec((TM, D), lambda i, j: (i, 0)), _resident((1, D)), pl.BlockSpec((tn, D), lambda i, j: (j, 0))]
    out_specs = [pl.BlockSpec((TM, D), lambda i, j: (i, 0)), pl.BlockSpec((TM, tn), lambda i, j: (i, j))]
    out_shape = [SDS((t, D), BF16), SDS((t, n), BF16)]
    args = [h, g, w_t]
    if w_extra_t is not None:
        in_specs.append(_resident(w_extra_t.shape))
        out_specs.append(pl.BlockSpec((TM, w_extra_t.shape[0]), lambda i, j: (i, 0)))
        out_shape.append(SDS((t, w_extra_t.shape[0]), BF16))
        args.append(w_extra_t)
    return pl.pallas_call(body, name=name, grid=(nt, nb), in_specs=in_specs, out_specs=out_specs,
                          out_shape=out_shape, compiler_params=_ARB2)(*args)


def _gla_decay(a_ref, wau_ref, ba_ref):
    z = _dot(a_ref[...], wau_ref[...]) + ba_ref[...]
    la = (jnp.minimum(z, 0.0) - jnp.log(1.0 + jnp.exp(-jnp.abs(z)))) * (1.0 / TAU)
    r = lax.broadcasted_iota(jnp.int32, (CH, CH), 0)
    c = lax.broadcasted_iota(jnp.int32, (CH, CH), 1)
    sel = jnp.concatenate([(r >= c).astype(F32), (c <= CH // 2).astype(F32), jnp.ones((CH, CH), F32)], axis=0)
    cum = jnp.dot(sel, la, precision=lax.Precision.HIGHEST, preferred_element_type=F32)
    return z, cum[:CH], cum[CH:2 * CH], cum[2 * CH:], r >= c


def _gla_fwd(proj, alr, wau, balpha, gn):
    t = proj.shape[0]
    nc = t // CH

    def body(qk_ref, v_ref, r_ref, a_ref, wau_ref, ba_ref, gn_ref, o_ref, og_ref, sall_ref, s_scr):
        @pl.when(pl.program_id(0) == 0)
        def _():
            s_scr[...] = jnp.zeros_like(s_scr)

        sall_ref[0] = s_scr[...]
        _, b, bmid, blast, causal = _gla_decay(a_ref, wau_ref, ba_ref)
        for h in range(HEADS):
            ks = slice(h * DKH, (h + 1) * DKH)
            vs = slice(h * DVH, (h + 1) * DVH)
            bh, mh, lh = b[:, ks], bmid[:, ks], blast[:, ks]
            q = qk_ref[:, ks].astype(F32) * (DKH ** -0.5)
            k = qk_ref[:, DK + h * DKH:DK + (h + 1) * DKH].astype(F32)
            v = v_ref[:, vs]
            qt = (q * jnp.exp(bh - mh)).astype(BF16)
            kt = (k * jnp.exp(mh - bh)).astype(BF16)
            qg = (q * jnp.exp(bh)).astype(BF16)
            kg = (k * jnp.exp(lh - bh)).astype(BF16)
            a = jnp.where(causal, _dot_nt(qt, kt), 0.0)
            st = s_scr[vs, :]
            o = _dot(a.astype(BF16), v) + _dot_nt(qg, st.astype(BF16))
            el = jnp.exp(lh)
            s_scr[vs, :] = st * jnp.concatenate([el, el], axis=0) + _dot_tn(v, kg)
            o_ref[:, vs] = o
            on = o * lax.rsqrt(jnp.mean(o * o, axis=-1, keepdims=True) + RMS_EPS) * gn_ref[:, vs]
            rr = r_ref[:, vs].astype(F32)
            og_ref[:, vs] = (on * (rr * _sigmoid(rr))).astype(BF16)

    return pl.pallas_call(
        body, name="gla_fwd", grid=(nc,),
        in_specs=[pl.BlockSpec((CH, D), lambda c: (c, 0)), pl.BlockSpec((CH, D), lambda c: (c, 1)),
                  pl.BlockSpec((CH, D), lambda c: (c, 6)), pl.BlockSpec((CH, LANES), lambda c: (c, 0)),
                  _resident((LANES, DK)), _resident((1, DK)), _resident((1, DV))],
        out_specs=[pl.BlockSpec((CH, DV), lambda c: (c, 0)), pl.BlockSpec((CH, DV), lambda c: (c, 0)),
                   pl.BlockSpec((1, DV, DKH), lambda c: (c, 0, 0))],
        out_shape=[SDS((t, DV), F32), SDS((t, DV), BF16), SDS((nc, DV, DKH), F32)],
        scratch_shapes=[pltpu.VMEM((DV, DKH), F32)], compiler_params=_ARB1)(proj, proj, proj, alr, wau, balpha, gn)


def _conf_fwd(proj, cw, cb, lg, lb):
    t = proj.shape[0]
    nt = t // TE

    def body(c1_ref, c2_ref, cw_ref, cb_ref, lg_ref, lb_ref, cc_ref, cs_ref, cext):
        i = pl.program_id(0)

        @pl.when(i == 0)
        def _():
            cext[0:HALO, :] = jnp.zeros((HALO, D), F32)

        @pl.when(i > 0)
        def _():
            cext[0:HALO, :] = cext[TE:TE + HALO, :]

        def glu(r0):
            c2 = c2_ref[pl.ds(r0, 32), :].astype(F32)
            cext[pl.ds(HALO + r0, 32), :] = c1_ref[pl.ds(r0, 32), :].astype(F32) * _sigmoid(c2)
        _row_loop(TE, 32, glu)

        def conv(r0):
            win = cext[pl.ds(r0, 16 + HALO), :]
            acc = jnp.zeros((16, D), F32) + cb_ref[...]
            for j in range(CONF_K):
                acc = acc + cw_ref[j:j + 1, :] * win[2 + j:2 + j + 16, :]
            cc_ref[pl.ds(r0, 16), :] = acc
            mu = jnp.mean(acc, axis=-1, keepdims=True)
            xc = acc - mu
            var = jnp.mean(xc * xc, axis=-1, keepdims=True)
            ln = xc * lax.rsqrt(var + LN_EPS) * lg_ref[...] + lb_ref[...]
            cs_ref[pl.ds(r0, 16), :] = (ln * _sigmoid(ln)).astype(BF16)
        _row_loop(TE, 16, conv)

    return pl.pallas_call(
        body, name="conf_fwd", grid=(nt,),
        in_specs=[pl.BlockSpec((TE, D), lambda i: (i, 2)), pl.BlockSpec((TE, D), lambda i: (i, 3)),
                  _resident((32, D)), _resident((1, D)), _resident((1, D)), _resident((1, D))],
        out_specs=[pl.BlockSpec((TE, D), lambda i: (i, 0)), pl.BlockSpec((TE, D), lambda i: (i, 0))],
        out_shape=[SDS((t, D), F32), SDS((t, D), BF16)],
        scratch_shapes=[pltpu.VMEM((TE + HALO, D), F32)], compiler_params=_ARB1)(proj, proj, cw, cb, lg, lb)


def _mix_fwd(og, cs, proj, h0, wg, wc, wo):
    t = h0.shape[0]
    nt = t // TE

    def body(og_ref, cs_ref, g_ref, h0_ref, wg_ref, wc_ref, wo_ref, brg_ref, brc_ref, mg_ref, h1_ref, acc):
        acc[...] = _dot(og_ref[...], wg_ref[...])
        brg_ref[...] = acc[...].astype(BF16)
        acc[...] = _dot(cs_ref[...], wc_ref[...])
        brc_ref[...] = acc[...].astype(BF16)

        def blk(r0):
            rows = pl.ds(r0, 32)
            gg = g_ref[rows, 0:D].astype(F32)
            gc = g_ref[rows, D:2 * D].astype(F32)
            m = _sigmoid(gg) * brg_ref[rows, :].astype(F32) + _sigmoid(gc) * brc_ref[rows, :].astype(F32)
            mg_ref[rows, :] = m.astype(BF16)
        _row_loop(TE, 32, blk)
        h1_ref[...] = h0_ref[...] + _dot(mg_ref[...], wo_ref[...])

    row = lambda w: pl.BlockSpec((TE, w), lambda i: (i, 0))
    return pl.pallas_call(
        body, name="mix_fwd", grid=(nt,),
        in_specs=[row(D), row(D), pl.BlockSpec((TE, 2 * D), lambda i: (i, 2)), row(D),
                  _resident((D, D)), _resident((D, D)), _resident((D, D))],
        out_specs=[row(D), row(D), row(D), row(D)],
        out_shape=[SDS((t, D), BF16), SDS((t, D), BF16), SDS((t, D), BF16), SDS((t, D), F32)],
        scratch_shapes=[pltpu.VMEM((TE, D), F32)], compiler_params=_ARB1)(og, cs, proj, h0, wg, wc, wo)


def _ffn_out(up, fw, fb, h1, wd, gf, tgt, n_real):
    t = h1.shape[0]
    nt = t // TE

    def body(a_ref, bv_ref, fw_ref, fb_ref, h1_ref, wd_ref, gf_ref, tg_ref, f_ref, dh2_ref, red_ref, aext, hs):
        i = pl.program_id(0)

        @pl.when(i == 0)
        def _():
            aext[0:HALO_F, :] = jnp.zeros((HALO_F, DFF), F32)
            red_ref[...] = jnp.zeros_like(red_ref)

        @pl.when(i > 0)
        def _():
            aext[0:HALO_F, :] = aext[TE:TE + HALO_F, :]

        def cp(r0):
            aext[pl.ds(HALO_F + r0, 16), :] = a_ref[pl.ds(r0, 16), :].astype(F32)
        _row_loop(TE, 16, cp)

        def conv(r0):
            win = aext[pl.ds(r0, 32), :]
            ac = fb_ref[...] + fw_ref[0:1, :] * win[14:30, :] + fw_ref[1:2, :] * win[15:31, :] + fw_ref[2:3, :] * win[16:32, :]
            f_ref[pl.ds(r0, 16), :] = (ac * _sigmoid(ac) * bv_ref[pl.ds(r0, 16), :].astype(F32)).astype(BF16)
        _row_loop(TE, 16, conv)
        hs[...] = h1_ref[...] + _dot(f_ref[...], wd_ref[...])

        def head(r0):
            rows = pl.ds(r0, 32)
            h2 = hs[rows, :]
            rinv = lax.rsqrt(jnp.mean(h2 * h2, axis=-1, keepdims=True) + RMS_EPS)
            hh = h2 * rinv
            gid = i * TE + r0 + lax.broadcasted_iota(jnp.int32, (32, 1), 0)
            live = jnp.logical_and(gid >= N_META, gid < n_real)
            err = jnp.where(live, hh * gf_ref[...] - tg_ref[rows, :], 0.0)
            dy = err * (1.0 / D)
            red_ref[0:8, :] += _rows8(err * err)
            red_ref[8:16, :] += _rows8(dy * hh)
            dhh = dy * gf_ref[...]
            dh2_ref[rows, :] = rinv * (dhh - hh * jnp.mean(dhh * hh, axis=-1, keepdims=True))
        _row_loop(TE, 32, head)

    row = lambda w: pl.BlockSpec((TE, w), lambda i: (i, 0))
    return pl.pallas_call(
        body, name="ffn_out", grid=(nt,),
        in_specs=[pl.BlockSpec((TE, DFF), lambda i: (i, 0)), pl.BlockSpec((TE, DFF), lambda i: (i, 1)),
                  _resident((8, DFF)), _resident((1, DFF)), row(D), _resident((DFF, D)), _resident((1, D)), row(D)],
        out_specs=[row(DFF), row(D), _resident((16, D))],
        out_shape=[SDS((t, DFF), BF16), SDS((t, D), F32), SDS((16, D), F32)],
        scratch_shapes=[pltpu.VMEM((TE + HALO_F, DFF), F32), pltpu.VMEM((TE, D), F32)],
        compiler_params=_ARB1)(up, up, fw, fb, h1, wd, gf, tgt)


def _ffn_bwd(dh2, wd, up, fw, fb):
    t = dh2.shape[0]
    nt = t // TE
    hb = TE // HALO_F

    def body(dh_ref, wd_ref, a_ref, ah_ref, bv_ref, fw_ref, fb_ref, dup_ref, dw_ref, aext, dax, dfs):
        i = pl.program_id(0)
        ti = nt - 1 - i

        @pl.when(i == 0)
        def _():
            dax[TE:TE + HALO_F, :] = jnp.zeros((HALO_F, DFF), F32)
            dw_ref[...] = jnp.zeros_like(dw_ref)

        @pl.when(i > 0)
        def _():
            dax[TE:TE + HALO_F, :] = dax[0:HALO_F, :]

        aext[0:HALO_F, :] = jnp.where(ti > 0, ah_ref[...].astype(F32), 0.0)
        dfs[...] = _dot_nt(dh_ref[...].astype(BF16), wd_ref[...])

        def cp(r0):
            aext[pl.ds(HALO_F + r0, 16), :] = a_ref[pl.ds(r0, 16), :].astype(F32)
        _row_loop(TE, 16, cp)

        def act(r0):
            rows = pl.ds(r0, 16)
            win = aext[pl.ds(r0, 32), :]
            ac = fb_ref[...] + fw_ref[0:1, :] * win[14:30, :] + fw_ref[1:2, :] * win[15:31, :] + fw_ref[2:3, :] * win[16:32, :]
            sg = _sigmoid(ac)
            df = dfs[rows, :]
            dup_ref[rows, DFF:2 * DFF] = (df * ac * sg).astype(BF16)
            dac = df * bv_ref[rows, :].astype(F32) * sg * (1.0 + ac * (1.0 - sg))
            dax[rows, :] = dac
            dw_ref[3] += _rows8(dac)
            for j in range(FFN_K):
                dw_ref[j] += _rows8(dac * win[14 + j:30 + j, :])
        _row_loop(TE, 16, act)

        def convt(r0):
            win = dax[pl.ds(r0, 32), :]
            da = fw_ref[2:3, :] * win[0:16, :] + fw_ref[1:2, :] * win[1:17, :] + fw_ref[0:1, :] * win[2:18, :]
            dup_ref[pl.ds(r0, 16), 0:DFF] = da.astype(BF16)
        _row_loop(TE, 16, convt)

    rev = lambda w: pl.BlockSpec((TE, w), lambda i: (nt - 1 - i, 0))
    return pl.pallas_call(
        body, name="ffn_bwd", grid=(nt,),
        in_specs=[rev(D), _resident((DFF, D)), rev(DFF),
                  pl.BlockSpec((HALO_F, DFF), lambda i: (jnp.maximum((nt - 1 - i) * hb - 1, 0), 0)),
                  pl.BlockSpec((TE, DFF), lambda i: (nt - 1 - i, 1)), _resident((8, DFF)), _resident((1, DFF))],
        out_specs=[rev(2 * DFF), _resident((4, 8, DFF))],
        out_shape=[SDS((t, 2 * DFF), BF16), SDS((4, 8, DFF), F32)],
        scratch_shapes=[pltpu.VMEM((TE + HALO_F, DFF), F32), pltpu.VMEM((TE + HALO_F, DFF), F32),
                        pltpu.VMEM((TE, DFF), F32)],
        compiler_params=_ARB1)(dh2, wd, up, up, up, fw, fb)


def _dgrad_norm(dy, w_t, tk, h, g, dres, name, dy_extra=None, w_extra_t=None):
    t, k = dy.shape
    nt, nk = t // TM, k // tk

    def body(*refs):
        if dy_extra is None:
            dy_ref, w_ref, h_ref, g_ref, dr_ref, dh_ref, dg_ref, acc = refs
        else:
            dy_ref, w_ref, h_ref, g_ref, dr_ref, de_ref, we_ref, dh_ref, dg_ref, acc = refs
        i, j = pl.program_id(0), pl.program_id(1)

        @pl.when(jnp.logical_and(i == 0, j == 0))
        def _():
            dg_ref[...] = jnp.zeros_like(dg_ref)

        @pl.when(j == 0)
        def _():
            if dy_extra is None:
                acc[...] = jnp.zeros_like(acc)
            else:
                acc[...] = _dot(de_ref[...], we_ref[...])

        acc[...] += _dot(dy_ref[...], w_ref[...])

        @pl.when(j == nk - 1)
        def _():
            def blk(r0):
                rows = pl.ds(r0, 32)
                x = h_ref[rows, :]
                rinv = lax.rsqrt(jnp.mean(x * x, axis=-1, keepdims=True) + RMS_EPS)
                hh = x * rinv
                du = acc[rows, :]
                dg_ref[...] += _rows8(du * hh)
                dhh = du * g_ref[...]
                dh_ref[rows, :] = dr_ref[rows, :] + rinv * (dhh - hh * jnp.mean(dhh * hh, axis=-1, keepdims=True))
            _row_loop(TM, 32, blk)

    row = pl.BlockSpec((TM, D), lambda i, j: (i, 0))
    in_specs = [pl.BlockSpec((TM, tk), lambda i, j: (i, j)), pl.BlockSpec((tk, D), lambda i, j: (j, 0)), row, _resident((1, D)), row]
    args = [dy, w_t, h, g, dres]
    if dy_extra is not None:
        in_specs += [pl.BlockSpec((TM, dy_extra.shape[1]), lambda i, j: (i, 0)), _resident(w_extra_t.shape)]
        args += [dy_extra, w_extra_t]
    return pl.pallas_call(
        body, name=name, grid=(nt, nk), in_specs=in_specs, out_specs=[row, _resident((8, D))],
        out_shape=[SDS((t, D), F32), SDS((8, D), F32)],
        scratch_shapes=[pltpu.VMEM((TM, D), F32)], compiler_params=_ARB2)(*args)


def _wgrad(x, dy, tk, name):
    t, k = x.shape
    n = dy.shape[1]
    nk, nt = k // tk, t // TM

    def body(x_ref, dy_ref, o_ref):
        @pl.when(pl.program_id(1) == 0)
        def _():
            o_ref[...] = jnp.zeros_like(o_ref)
        o_ref[...] += _dot_tn(x_ref[...], dy_ref[...].astype(BF16))

    return pl.pallas_call(
        body, name=name, grid=(nk, nt),
        in_specs=[pl.BlockSpec((TM, tk), lambda j, i: (i, j)), pl.BlockSpec((TM, n), lambda j, i: (i, 0))],
        out_specs=pl.BlockSpec((tk, n), lambda j, i: (j, 0)), out_shape=SDS((k, n), F32),
        compiler_params=_ARB2)(x, dy)


def _mix_bwd(dh1, wo, wg, wc, proj, brg, brc):
    t = dh1.shape[0]
    nt = t // TE

    def body(dh_ref, wo_ref, wg_ref, wc_ref, g_ref, brg_ref, brc_ref, dbg_ref, dbc_ref, dog_ref, dcs_ref, dp_ref, dm):
        dm[...] = _dot_nt(dh_ref[...].astype(BF16), wo_ref[...])

        def blk(r0):
            rows = pl.ds(r0, 32)
            d = dm[rows, :]
            sg = _sigmoid(g_ref[rows, 0:D].astype(F32))
            sc = _sigmoid(g_ref[rows, D:2 * D].astype(F32))
            dbg_ref[rows, :] = (d * sg).astype(BF16)
            dbc_ref[rows, :] = (d * sc).astype(BF16)
            dp_ref[rows, 0:D] = (d * brg_ref[rows, :].astype(F32) * sg * (1.0 - sg)).astype(BF16)
            dp_ref[rows, D:2 * D] = (d * brc_ref[rows, :].astype(F32) * sc * (1.0 - sc)).astype(BF16)
        _row_loop(TE, 32, blk)
        dog_ref[...] = _dot_nt(dbg_ref[...], wg_ref[...]).astype(BF16)
        dcs_ref[...] = _dot_nt(dbc_ref[...], wc_ref[...]).astype(BF16)

    row = pl.BlockSpec((TE, D), lambda i: (i, 0))
    wide = pl.BlockSpec((TE, 2 * D), lambda i: (i, 2))
    return pl.pallas_call(
        body, name="mix_bwd", grid=(nt,),
        in_specs=[row, _resident((D, D)), _resident((D, D)), _resident((D, D)), wide, row, row],
        out_specs=[row, row, row, row, wide],
        out_shape=[SDS((t, D), BF16)] * 4 + [SDS((t, NPROJ), BF16)],
        scratch_shapes=[pltpu.VMEM((TE, D), F32)], compiler_params=_ARB1)(dh1, wo, wg, wc, proj, brg, brc)


def _glapost_bwd(dog, o, proj, gn, dproj):
    t = o.shape[0]
    nt = t // TE

    def body(dog_ref, o_ref, r_ref, gn_ref, dp_in, do_ref, dp_ref, dgn_ref):
        del dp_in

        @pl.when(pl.program_id(0) == 0)
        def _():
            dgn_ref[...] = jnp.zeros_like(dgn_ref)

        def blk(r0):
            rows = pl.ds(r0, 32)
            for h in range(HEADS):
                vs = slice(h * DVH, (h + 1) * DVH)
                x = o_ref[rows, vs]
                rinv = lax.rsqrt(jnp.mean(x * x, axis=-1, keepdims=True) + RMS_EPS)
                oh = x * rinv
                g = gn_ref[:, vs]
                rr = r_ref[rows, vs].astype(F32)
                sr = _sigmoid(rr)
                d = dog_ref[rows, vs].astype(F32)
                dp_ref[rows, vs] = (d * oh * g * sr * (1.0 + rr * (1.0 - sr))).astype(BF16)
                don = d * rr * sr
                dgn_ref[:, vs] += _rows8(don * oh)
                doh = don * g
                do_ref[rows, vs] = (rinv * (doh - oh * jnp.mean(doh * oh, axis=-1, keepdims=True))).astype(BF16)
        _row_loop(TE, 32, blk)

    row = pl.BlockSpec((TE, D), lambda i: (i, 0))
    rcol = pl.BlockSpec((TE, D), lambda i: (i, 6))
    return pl.pallas_call(
        body, name="glapost_bwd", grid=(nt,),
        in_specs=[row, row, rcol, _resident((1, D)), pl.BlockSpec(memory_space=pl.ANY)],
        out_specs=[row, rcol, _resident((8, D))],
        out_shape=[SDS((t, D), BF16), SDS((t, NPROJ), BF16), SDS((8, D), F32)],
        input_output_aliases={4: 1}, compiler_params=_ARB1)(dog, o, proj, gn, dproj)


def _conf_bwd(dcs, cc, proj, cw, lg, lb, dproj):
    t = cc.shape[0]
    nt = t // TE
    hb = TE // HALO

    def body(dcs_ref, cc_ref, c1_ref, c2_ref, c1h_ref, c2h_ref, cw_ref, lg_ref, lb_ref, dp_in,
             dp_ref, dw_ref, ds_ref, cext, dext):
        del dp_in
        i = pl.program_id(0)
        ti = nt - 1 - i

        @pl.when(i == 0)
        def _():
            dext[TE:TE + HALO, :] = jnp.zeros((HALO, D), F32)
            dw_ref[...] = jnp.zeros_like(dw_ref)
            ds_ref[...] = jnp.zeros_like(ds_ref)

        @pl.when(i > 0)
        def _():
            dext[TE:TE + HALO, :] = dext[0:HALO, :]

        ch = c1h_ref[...].astype(F32) * _sigmoid(c2h_ref[...].astype(F32))
        cext[0:HALO, :] = jnp.where(ti > 0, ch, 0.0)

        def pre(r0):
            rows = pl.ds(r0, 32)
            cext[pl.ds(HALO + r0, 32), :] = c1_ref[rows, :].astype(F32) * _sigmoid(c2_ref[rows, :].astype(F32))
            x = cc_ref[rows, :]
            mu = jnp.mean(x, axis=-1, keepdims=True)
            xc = x - mu
            rstd = lax.rsqrt(jnp.mean(xc * xc, axis=-1, keepdims=True) + LN_EPS)
            xh = xc * rstd
            ln = xh * lg_ref[...] + lb_ref[...]
            sg = _sigmoid(ln)
            dln = dcs_ref[rows, :].astype(F32) * sg * (1.0 + ln * (1.0 - sg))
            ds_ref[0] += _rows8(dln * xh)
            ds_ref[1] += _rows8(dln)
            dxh = dln * lg_ref[...]
            dcc = rstd * (dxh - jnp.mean(dxh, axis=-1, keepdims=True) - xh * jnp.mean(dxh * xh, axis=-1, keepdims=True))
            dext[rows, :] = dcc
            ds_ref[2] += _rows8(dcc)
        _row_loop(TE, 32, pre)

        def convt(r0):
            rows = pl.ds(r0, 16)
            wd = dext[pl.ds(r0, 16 + HALO), :]
            wc = cext[pl.ds(r0, 16 + HALO), :]
            dcc = wd[0:16, :]
            dc = jnp.zeros((16, D), F32)
            for j in range(CONF_K):
                dc = dc + cw_ref[j:j + 1, :] * wd[30 - j:46 - j, :]
                dw_ref[j] += _rows8(dcc * wc[2 + j:18 + j, :])
            c1 = c1_ref[rows, :].astype(F32)
            s2 = _sigmoid(c2_ref[rows, :].astype(F32))
            dp_ref[rows, 0:D] = (dc * s2).astype(BF16)
            dp_ref[rows, D:2 * D] = (dc * c1 * s2 * (1.0 - s2)).astype(BF16)
        _row_loop(TE, 16, convt)

    rev = lambda col: pl.BlockSpec((TE, D), lambda i: (nt - 1 - i, col))
    halo = lambda col: pl.BlockSpec((HALO, D), lambda i: (jnp.maximum((nt - 1 - i) * hb - 1, 0), col))
    return pl.pallas_call(
        body, name="conf_bwd", grid=(nt,),
        in_specs=[rev(0), rev(0), rev(2), rev(3), halo(2), halo(3), _resident((32, D)), _resident((1, D)),
                  _resident((1, D)), pl.BlockSpec(memory_space=pl.ANY)],
        out_specs=[pl.BlockSpec((TE, 2 * D), lambda i: (nt - 1 - i, 1)), _resident((32, 8, D)), _resident((3, 8, D))],
        out_shape=[SDS((t, NPROJ), BF16), SDS((32, 8, D), F32), SDS((3, 8, D), F32)],
        scratch_shapes=[pltpu.VMEM((TE + HALO, D), F32), pltpu.VMEM((TE + HALO, D), F32)],
        input_output_aliases={9: 0}, compiler_params=_ARB1)(dcs, cc, proj, proj, proj, proj, cw, lg, lb, dproj)


def _gla_bwd(proj, alr, wau, balpha, do, sall, dproj):
    t = proj.shape[0]
    nc = t // CH

    def body(qk_ref, v_ref, a_ref, wau_ref, ba_ref, do_ref, s_ref, dp_in, dp_ref, da_ref, dwau_ref, dba_ref, ds_scr, dla_scr):
        del dp_in

        @pl.when(pl.program_id(0) == 0)
        def _():
            ds_scr[...] = jnp.zeros_like(ds_scr)
            dwau_ref[...] = jnp.zeros_like(dwau_ref)
            dba_ref[...] = jnp.zeros_like(dba_ref)

        z, b, bmid, blast, causal = _gla_decay(a_ref, wau_ref, ba_ref)
        r = lax.broadcasted_iota(jnp.int32, (CH, CH), 0)
        c = lax.broadcasted_iota(jnp.int32, (CH, CH), 1)
        upper = (r <= c).astype(F32)
        for h in range(HEADS):
            ks = slice(h * DKH, (h + 1) * DKH)
            vs = slice(h * DVH, (h + 1) * DVH)
            bh, mh, lh = b[:, ks], bmid[:, ks], blast[:, ks]
            q = qk_ref[:, ks].astype(F32) * (DKH ** -0.5)
            k = qk_ref[:, DK + h * DKH:DK + (h + 1) * DKH].astype(F32)
            v = v_ref[:, vs]
            dout = do_ref[:, vs]
            eq, ek, eb, eg, el = jnp.exp(bh - mh), jnp.exp(mh - bh), jnp.exp(bh), jnp.exp(lh - bh), jnp.exp(lh)
            qt, kt = (q * eq).astype(BF16), (k * ek).astype(BF16)
            qg, kg = (q * eb).astype(BF16), (k * eg).astype(BF16)
            st = s_ref[0, vs, :]
            dsn = ds_scr[vs, :]
            st16, dsn16 = st.astype(BF16), dsn.astype(BF16)
            a = jnp.where(causal, _dot_nt(qt, kt), 0.0).astype(BF16)
            da = jnp.where(causal, _dot_nt(dout, v), 0.0).astype(BF16)
            dq_inter = _dot(dout, st16) * eb
            dk_inter = _dot(v, dsn16) * eg
            dq = _dot(da, kt) * eq + dq_inter
            dk = _dot_tn(da, qt) * ek + dk_inter
            dv = _dot_tn(a, dout) + _dot_nt(kg, dsn16)
            dlast = jnp.sum(k * dk_inter, axis=0, keepdims=True) + jnp.sum(st * dsn, axis=0, keepdims=True) * el[0:1, :]
            db = q * dq - k * dk
            dla_scr[:, ks] = jnp.dot(upper, db, precision=lax.Precision.HIGHEST, preferred_element_type=F32) + dlast
            ds_scr[vs, :] = dsn * jnp.concatenate([el, el], axis=0) + _dot_tn(dout, qg)
            dp_ref[:, ks] = (dq * (DKH ** -0.5)).astype(BF16)
            dp_ref[:, DK + h * DKH:DK + (h + 1) * DKH] = dk.astype(BF16)
            dp_ref[:, D + h * DVH:D + (h + 1) * DVH] = dv.astype(BF16)
        dz = (dla_scr[...] * (1.0 / TAU) * _sigmoid(-z)).astype(BF16)
        da_ref[...] = _dot_nt(dz, wau_ref[...]).astype(BF16)
        dwau_ref[...] += _dot_tn(a_ref[...], dz)
        dba_ref[...] += _rows8(dz.astype(F32))

    rev = lambda w, col: pl.BlockSpec((CH, w), lambda c: (nc - 1 - c, col))
    return pl.pallas_call(
        body, name="gla_bwd", grid=(nc,),
        in_specs=[rev(D, 0), rev(D, 1), rev(LANES, 0), _resident((LANES, DK)), _resident((1, DK)), rev(D, 0),
                  pl.BlockSpec((1, DV, DKH), lambda c: (nc - 1 - c, 0, 0)), pl.BlockSpec(memory_space=pl.ANY)],
        out_specs=[rev(2 * D, 0), rev(LANES, 0), _resident((LANES, DK)), _resident((8, DK))],
        out_shape=[SDS((t, NPROJ), BF16), SDS((t, LANES), BF16), SDS((LANES, DK), F32), SDS((8, DK), F32)],
        scratch_shapes=[pltpu.VMEM((DV, DKH), F32), pltpu.VMEM((CH, DK), F32)],
        input_output_aliases={7: 0}, compiler_params=_ARB1)(proj, proj, alr, wau, balpha, do, sall, dproj)


def _my_place():
    return lax.axis_index("x"), lax.axis_index("y"), lax.axis_index("c")


def _flip(v, bit):
    return 1 - v if bit else v


def _all_gather(xs, name):
    n = len(xs)

    def body(*refs):
        x_refs, out_refs = refs[:n], refs[n:2 * n]
        send_sems, recv_sems, local_sems = refs[2 * n:]
        x, y, c = _my_place()
        me, sibling = (x, y, c), (x, y, 1 - c)
        chips = [(1 - x, y), (x, 1 - y), (1 - x, 1 - y)]

        def slot(p, px, py, pc):
            return out_refs[p].at[4 * px + 2 * py + pc]

        def copy(p, k, block, to, src=None):
            return pltpu.make_async_remote_copy(
                src_ref=slot(p, *block) if src is None else src, dst_ref=slot(p, *block),
                send_sem=send_sems.at[7 * p + k], recv_sem=recv_sems.at[7 * p + k], device_id=to, device_id_type=MESH_T)

        mine = [pltpu.make_async_copy(x_refs[p], slot(p, *me), local_sems.at[p]) for p in range(n)]
        for cp in mine:
            cp.start()
        first = []
        for p in range(n):
            first.append(copy(p, 0, me, sibling, src=x_refs[p]))
            first += [copy(p, 1 + j, me, (*chip, c), src=x_refs[p]) for j, chip in enumerate(chips)]
        for cp in first:
            cp.start()
        passed = []
        for p in range(n):
            for j, chip in enumerate(chips):
                copy(p, 1 + j, (*chip, c), me).wait_recv()
                fwd = copy(p, 4 + j, (*chip, c), sibling)
                fwd.start()
                passed.append(fwd)
        for p in range(n):
            copy(p, 0, sibling, me).wait_recv()
            for j, chip in enumerate(chips):
                copy(p, 4 + j, (*chip, 1 - c), me).wait_recv()
        for cp in first + passed:
            cp.wait_send()
        for cp in mine:
            cp.wait()

    hbm = pl.BlockSpec(memory_space=pl.ANY)
    return pl.pallas_call(
        body, name=name, out_shape=[SDS((N_DEV, *a.shape), a.dtype) for a in xs],
        in_specs=[hbm] * n, out_specs=[hbm] * n,
        scratch_shapes=[pltpu.SemaphoreType.DMA((7 * n,)), pltpu.SemaphoreType.DMA((7 * n,)),
                        pltpu.SemaphoreType.DMA((n,))])(*xs)


def _exchange(gs):
    n = len(gs)

    def body(*refs):
        g_refs, land_refs = refs[:n], refs[n:2 * n]
        send_sems, recv_sems, local_sems = refs[2 * n:]
        x, y, c = _my_place()
        my_idx = 4 * x + 2 * y + c
        mine = [pltpu.make_async_copy(g_refs[p].at[my_idx], land_refs[p].at[my_idx], local_sems.at[p]) for p in range(n)]
        for cp in mine:
            cp.start()
        copies = []
        for k in range(1, N_DEV):
            px, py, pc = _flip(x, k & 4), _flip(y, k & 2), _flip(c, k & 1)
            p_idx = 4 * px + 2 * py + pc
            for p in range(n):
                s = 7 * p + k - 1
                cp = pltpu.make_async_remote_copy(
                    src_ref=g_refs[p].at[p_idx], dst_ref=land_refs[p].at[my_idx], send_sem=send_sems.at[s],
                    recv_sem=recv_sems.at[s], device_id=(px, py, pc), device_id_type=MESH_T)
                cp.start()
                arrival = pltpu.make_async_remote_copy(
                    src_ref=g_refs[p].at[p_idx], dst_ref=land_refs[p].at[p_idx], send_sem=send_sems.at[s],
                    recv_sem=recv_sems.at[s], device_id=(px, py, pc), device_id_type=MESH_T)
                copies.append((cp, arrival))
        for cp, arrival in copies:
            arrival.wait_recv()
        for cp, arrival in copies:
            cp.wait_send()
        for cp in mine:
            cp.wait()

    hbm = pl.BlockSpec(memory_space=pl.ANY)
    return pl.pallas_call(
        body, name="grad_exchange", out_shape=[SDS(g.shape, g.dtype) for g in gs],
        in_specs=[hbm] * n, out_specs=[hbm] * n,
        scratch_shapes=[pltpu.SemaphoreType.DMA((7 * n,)), pltpu.SemaphoreType.DMA((7 * n,)),
                        pltpu.SemaphoreType.DMA((n,))])(*gs)


def _adamw(land, w, m, v, rows_blk, name):
    rows = w.shape[0]

    def body(l_ref, w_ref, m_ref, v_ref, g_ref, d_ref, nm_ref, nv_ref):
        g = l_ref[0]
        for s in range(1, N_DEV):
            g = g + l_ref[s]
        nm = ADAM_B1 * m_ref[...] + (1.0 - ADAM_B1) * g
        nv = ADAM_B2 * v_ref[...] + (1.0 - ADAM_B2) * (g * g)
        m_hat = nm / (1.0 - ADAM_B1 ** ADAM_STEP)
        v_hat = nv / (1.0 - ADAM_B2 ** ADAM_STEP)
        g_ref[...] = g
        d_ref[...] = -ADAM_LR * (m_hat / (jnp.sqrt(v_hat) + ADAM_EPS) + ADAM_WD * w_ref[...])
        nm_ref[...] = nm
        nv_ref[...] = nv

    blk = pl.BlockSpec((rows_blk, D), lambda i: (i, 0))
    return pl.pallas_call(
        body, name=name, grid=(rows // rows_blk,),
        in_specs=[pl.BlockSpec((N_DEV, rows_blk, D), lambda i: (0, i, 0)), blk, blk, blk],
        out_specs=[blk] * 4, out_shape=[SDS((rows, D), F32)] * 4, compiler_params=_ARB1)(land, w, m, v)


BIG = ("w_in", "w_up", "w_down", "w_gla_o", "w_conf_o", "w_out")
BIG_TRANSPOSED = ("w_in", "w_up")
SMALL_SHARDED = ("meta_tokens", "conf_dw_w", "ffn_dw_w", "w_alpha_up")
REPLICATED = ("norm_mix_g", "b_alpha", "gla_norm_g", "conf_dw_b", "conf_ln_g", "conf_ln_b", "norm_ffn_g", "ffn_dw_b",
              "final_norm_g")
N_IN = sum(IN_WIDTHS)
W_IN_ROWS = N_IN // N_DEV
W_IN_PAD = -(-W_IN_ROWS // 16) * 16
ADAM_BLOCK = {"w_in": W_IN_PAD // 6, "w_up": 176, "w_down": 176, "w_gla_o": 128, "w_conf_o": 128, "w_out": 128}
SMALL_ROWS = 32


def _to_panel(name, shard):
    a = shard.reshape(shard.shape[-2], shard.shape[-1])
    if name in BIG_TRANSPOSED:
        a = a.T
    if name == "w_in":
        a = jnp.pad(a, ((0, W_IN_PAD - W_IN_ROWS), (0, 0)))
    return a


def _from_panel(name, panel, shape):
    a = panel[0:W_IN_ROWS] if name == "w_in" else panel
    if name in BIG_TRANSPOSED:
        a = a.T
    return a.reshape(shape)


def _pack_small(arrs):
    flat = jnp.concatenate([jnp.pad(a.reshape(-1), (0, (-a.size) % D)) for a in arrs])
    return jnp.pad(flat, (0, SMALL_ROWS * D - flat.shape[0])).reshape(SMALL_ROWS, D)


def _unpack_small(panel, shapes):
    flat, out, off = panel.reshape(-1), [], 0
    for shp in shapes:
        n = 1
        for s in shp:
            n *= s
        out.append(flat[off:off + n].reshape(shp))
        off += n + (-n) % D
    return out


def _local_step(x, target, w):
    s = x.shape[0]
    n_real = s + N_META
    t = -(-n_real // TM) * TM
    h0 = jnp.concatenate([w["meta_tokens"], x, jnp.zeros((t - n_real, D), F32)], axis=0)
    tgt = jnp.concatenate([jnp.zeros((N_META, D), F32), target, jnp.zeros((t - n_real, D), F32)], axis=0)

    q0, r0, a0, c0 = 0, 2 * DK + DV, 2 * DK + 2 * DV, 2 * DK + 2 * DV + RANK
    wt = w["w_in_t"]
    w_main = jnp.concatenate([wt[q0:r0], wt[c0:N_IN], wt[r0:a0]], axis=0)
    w_a = jnp.pad(wt[a0:c0], ((0, LANES - RANK), (0, 0)))
    wau = jnp.pad(w["w_alpha_up"].astype(BF16), ((0, LANES - RANK), (0, 0)))
    row = lambda name: w[name].reshape(1, -1)
    cw = jnp.pad(w["conf_dw_w"], ((0, 32 - CONF_K), (0, 0)))
    fw = jnp.pad(w["ffn_dw_w"], ((0, 8 - FFN_K), (0, 0)))

    u1, proj, alr = _norm_matmul(h0, row("norm_mix_g"), w_main, 1024, "in_proj", w_extra_t=w_a)
    o, og, sall = _gla_fwd(proj, alr, wau, row("b_alpha"), row("gla_norm_g"))
    cc, cs = _conf_fwd(proj, cw, row("conf_dw_b"), row("conf_ln_g"), row("conf_ln_b"))
    brg, brc, merged, h1 = _mix_fwd(og, cs, proj, h0, w["w_gla_o"], w["w_conf_o"], w["w_out"])
    u2, up = _norm_matmul(h1, row("norm_ffn_g"), w["w_up_t"], 512, "up_proj")
    f, dh2, red = _ffn_out(up, fw, row("ffn_dw_b"), h1, w["w_down"], row("final_norm_g"), tgt, n_real)
    loss = 0.5 / D * jnp.sum(red[0:8])

    g = {"final_norm_g": jnp.sum(red[8:16], axis=0)}
    dup, dfw = _ffn_bwd(dh2, w["w_down"], up, fw, row("ffn_dw_b"))
    g["ffn_dw_w"] = jnp.sum(dfw[0:FFN_K], axis=1)
    g["ffn_dw_b"] = jnp.sum(dfw[3], axis=0)
    g["w_down"] = _wgrad(f, dh2, 1408, "wgrad_down")
    dh1, dg2 = _dgrad_norm(dup, w["w_up_t"], 512, h1, row("norm_ffn_g"), dh2, "up_dgrad")
    g["norm_ffn_g"] = jnp.sum(dg2, axis=0)
    g["w_up_t"] = _wgrad(dup, u2, 1408, "wgrad_up")
    dbrg, dbrc, dog, dcs, dproj = _mix_bwd(dh1, w["w_out"], w["w_gla_o"], w["w_conf_o"], proj, brg, brc)
    g["w_out"] = _wgrad(merged, dh1, 1024, "wgrad_out")
    g["w_gla_o"] = _wgrad(og, dbrg, 1024, "wgrad_gla_o")
    g["w_conf_o"] = _wgrad(cs, dbrc, 1024, "wgrad_conf_o")
    do, dproj, dgn = _glapost_bwd(dog, o, proj, row("gla_norm_g"), dproj)
    g["gla_norm_g"] = jnp.sum(dgn, axis=0)
    dproj, dcw, dst = _conf_bwd(dcs, cc, proj, cw, row("conf_ln_g"), row("conf_ln_b"), dproj)
    g["conf_dw_w"] = jnp.sum(dcw[0:CONF_K], axis=1)
    g["conf_ln_g"], g["conf_ln_b"], g["conf_dw_b"] = jnp.sum(dst[0], axis=0), jnp.sum(dst[1], axis=0), jnp.sum(dst[2], axis=0)
    dproj, dalr, dwau, dba = _gla_bwd(proj, alr, wau, row("b_alpha"), do, sall, dproj)
    g["w_alpha_up"] = dwau[0:RANK]
    g["b_alpha"] = jnp.sum(dba, axis=0)
    dh0, dg1 = _dgrad_norm(dproj, w_main, 1024, h0, row("norm_mix_g"), dh1, "in_dgrad", dy_extra=dalr, w_extra_t=w_a)
    g["norm_mix_g"] = jnp.sum(dg1, axis=0)
    dw_main = _wgrad(dproj, u1, 1024, "wgrad_in")
    dw_a = _wgrad(dalr, u1, LANES, "wgrad_alr")
    g["w_in_t"] = jnp.concatenate([dw_main[0:r0], dw_main[NPROJ - DV:NPROJ], dw_a[0:RANK], dw_main[r0:NPROJ - DV]], axis=0)
    g["meta_tokens"] = dh0[0:N_META]
    return loss, dh0[N_META:n_real], g


def kernel(x, meta_tokens, norm_mix_g, w_in, w_alpha_up, b_alpha, gla_norm_g, w_gla_o, conf_dw_w, conf_dw_b, conf_ln_g, conf_ln_b, w_conf_o, w_out, norm_ffn_g, w_up, ffn_dw_w, ffn_dw_b, w_down, final_norm_g, loss_target, m_meta_tokens, m_norm_mix_g, m_w_in, m_w_alpha_up, m_b_alpha, m_gla_norm_g, m_w_gla_o, m_conf_dw_w, m_conf_dw_b, m_conf_ln_g, m_conf_ln_b, m_w_conf_o, m_w_out, m_norm_ffn_g, m_w_up, m_ffn_dw_w, m_ffn_dw_b, m_w_down, m_final_norm_g, v_meta_tokens, v_norm_mix_g, v_w_in, v_w_alpha_up, v_b_alpha, v_gla_norm_g, v_w_gla_o, v_conf_dw_w, v_conf_dw_b, v_conf_ln_g, v_conf_ln_b, v_w_conf_o, v_w_out, v_norm_ffn_g, v_w_up, v_ffn_dw_w, v_ffn_dw_b, v_w_down, v_final_norm_g):
    ws = dict(meta_tokens=meta_tokens, norm_mix_g=norm_mix_g, w_in=w_in, w_alpha_up=w_alpha_up, b_alpha=b_alpha,
              gla_norm_g=gla_norm_g, w_gla_o=w_gla_o, conf_dw_w=conf_dw_w, conf_dw_b=conf_dw_b, conf_ln_g=conf_ln_g,
              conf_ln_b=conf_ln_b, w_conf_o=w_conf_o, w_out=w_out, norm_ffn_g=norm_ffn_g, w_up=w_up, ffn_dw_w=ffn_dw_w,
              ffn_dw_b=ffn_dw_b, w_down=w_down, final_norm_g=final_norm_g)
    ms = dict(meta_tokens=m_meta_tokens, norm_mix_g=m_norm_mix_g, w_in=m_w_in, w_alpha_up=m_w_alpha_up, b_alpha=m_b_alpha,
              gla_norm_g=m_gla_norm_g, w_gla_o=m_w_gla_o, conf_dw_w=m_conf_dw_w, conf_dw_b=m_conf_dw_b,
              conf_ln_g=m_conf_ln_g, conf_ln_b=m_conf_ln_b, w_conf_o=m_w_conf_o, w_out=m_w_out, norm_ffn_g=m_norm_ffn_g,
              w_up=m_w_up, ffn_dw_w=m_ffn_dw_w, ffn_dw_b=m_ffn_dw_b, w_down=m_w_down, final_norm_g=m_final_norm_g)
    vs = dict(meta_tokens=v_meta_tokens, norm_mix_g=v_norm_mix_g, w_in=v_w_in, w_alpha_up=v_w_alpha_up, b_alpha=v_b_alpha,
              gla_norm_g=v_gla_norm_g, w_gla_o=v_w_gla_o, conf_dw_w=v_conf_dw_w, conf_dw_b=v_conf_dw_b,
              conf_ln_g=v_conf_ln_g, conf_ln_b=v_conf_ln_b, w_conf_o=v_w_conf_o, w_out=v_w_out, norm_ffn_g=v_norm_ffn_g,
              w_up=v_w_up, ffn_dw_w=v_ffn_dw_w, ffn_dw_b=v_ffn_dw_b, w_down=v_w_down, final_norm_g=v_final_norm_g)
    small = SMALL_SHARDED + REPLICATED
    pack_small = lambda d: _pack_small([d[n] for n in small])

    gathered = _all_gather([_to_panel(n, ws[n]).astype(BF16) for n in BIG] + [pack_small(ws)], "weight_gather")
    full = {n: ws[n].reshape(-1) for n in REPLICATED}
    full["w_in_t"] = gathered[0][:, 0:W_IN_ROWS].reshape(N_IN, D)
    full["w_up_t"] = gathered[1].reshape(-1, D)
    for i, n in enumerate(BIG[2:]):
        full[n] = gathered[2 + i].reshape(-1, D)
    flat, off = gathered[len(BIG)].reshape(N_DEV, -1), 0
    for n in SMALL_SHARDED:
        k, c = ws[n].shape[-2], ws[n].shape[-1]
        full[n] = flat[:, off:off + k * c].reshape(N_DEV, k, c).transpose(1, 0, 2).reshape(k, N_DEV * c)
        off += k * c + (-(k * c)) % D

    loss, grad_x, g = _local_step(x[0], loss_target[0], full)

    gs = [jnp.pad(g["w_in_t"].reshape(N_DEV, W_IN_ROWS, D), ((0, 0), (0, W_IN_PAD - W_IN_ROWS), (0, 0))),
          g["w_up_t"].reshape(N_DEV, -1, D)]
    gs += [g[n].reshape(N_DEV, -1, D) for n in BIG[2:]]
    blocks = []
    for n in SMALL_SHARDED:
        k, c = ws[n].shape[-2], ws[n].shape[-1]
        b = g[n].reshape(k, N_DEV, c).transpose(1, 0, 2).reshape(N_DEV, k * c)
        blocks.append(jnp.pad(b, ((0, 0), (0, (-(k * c)) % D))))
    for n in REPLICATED:
        b = jnp.broadcast_to(g[n].reshape(1, -1), (N_DEV, g[n].size))
        blocks.append(jnp.pad(b, ((0, 0), (0, (-b.shape[1]) % D))))
    gsm = jnp.concatenate(blocks, axis=1)
    gs.append(jnp.pad(gsm, ((0, 0), (0, SMALL_ROWS * D - gsm.shape[1]))).reshape(N_DEV, SMALL_ROWS, D))
    lands = _exchange(gs)

    grad, delta, new_m, new_v = {}, {}, {}, {}
    for i, n in enumerate(BIG):
        outs = _adamw(lands[i], _to_panel(n, ws[n]), _to_panel(n, ms[n]), _to_panel(n, vs[n]), ADAM_BLOCK[n], "adamw_" + n)
        grad[n], delta[n], new_m[n], new_v[n] = [_from_panel(n, p, ws[n].shape) for p in outs]
    outs = _adamw(lands[len(BIG)], pack_small(ws), pack_small(ms), pack_small(vs), SMALL_ROWS, "adamw_small")
    shapes = [ws[n].shape for n in small]
    for d, p in zip((grad, delta, new_m, new_v), outs):
        d.update(zip(small, _unpack_small(p, shapes)))

    order = ("meta_tokens", "norm_mix_g", "w_in", "w_alpha_up", "b_alpha", "gla_norm_g", "w_gla_o", "conf_dw_w", "conf_dw_b",
             "conf_ln_g", "conf_ln_b", "w_conf_o", "w_out", "norm_ffn_g", "w_up", "ffn_dw_w", "ffn_dw_b", "w_down",
             "final_norm_g")
    loss = lax.psum(loss, ("x", "y", "c"))
    return (loss, grad_x[None], *[grad[n] for n in order], *[delta[n] for n in order], *[new_m[n] for n in order],
            *[new_v[n] for n in order])
```

```python
import functools

import jax
import jax.numpy as jnp
from jax import lax
from jax.experimental import pallas as pl
from jax.experimental.pallas import tpu as pltpu

F32, BF16 = jnp.float32, jnp.bfloat16
SDS = jax.ShapeDtypeStruct

D = 1024
N_META = 16
HEADS = 4
DK, DKH, DV, DVH = 512, 128, 1024, 256
RANK = 16
TAU = 16.0
CONF_K = 31
DFF = 2816
FFN_K = 3
IN_WIDTHS = (DK, DK, DV, DV, RANK, 2 * D, D, D)
RMS_EPS, LN_EPS = 1e-6, 1e-5
ADAM_LR, ADAM_B1, ADAM_B2, ADAM_EPS, ADAM_WD, ADAM_STEP = 0.001, 0.9, 0.999, 1e-08, 0.01, 10

NPROJ = 7 * D
LANES = 128
CH = 128
TM = 640
TE = 320
HALO = 32
HALO_F = 16
N_DEV = 8
VMEM_LIMIT = 60 * 1024 * 1024
MESH_T = pl.DeviceIdType.MESH

_ARB1 = pltpu.CompilerParams(dimension_semantics=("arbitrary",), vmem_limit_bytes=VMEM_LIMIT)
_ARB2 = pltpu.CompilerParams(dimension_semantics=("arbitrary", "arbitrary"), vmem_limit_bytes=VMEM_LIMIT)


def _dot(a, b):
    return jnp.dot(a, b, preferred_element_type=F32)


def _dot_nt(a, b):
    return lax.dot_general(a, b, (((1,), (1,)), ((), ())), preferred_element_type=F32)


def _dot_tn(a, b):
    return lax.dot_general(a, b, (((0,), (0,)), ((), ())), preferred_element_type=F32)


def _sigmoid(x):
    return 1.0 / (1.0 + jnp.exp(-x))


def _rows8(x):
    return x.reshape(x.shape[0] // 8, 8, x.shape[1]).sum(axis=0)


def _row_loop(n_rows, rb, fn):
    def step(i, carry):
        fn(pl.multiple_of(i * rb, rb))
        return carry
    lax.fori_loop(0, n_rows // rb, step, 0)


def _resident(shape):
    return pl.BlockSpec(shape, lambda *_: (0,) * len(shape))


def _my_place():
    return lax.axis_index("x"), lax.axis_index("y"), lax.axis_index("c")


def _flip(v, bit):
    return 1 - v if bit else v


def _exchange_copies(src_refs, land_refs, scatter, send_sems, recv_sems, local_sems):
    x, y, c = _my_place()
    my_idx = 4 * x + 2 * y + c
    local, remote = [], []
    for p, (src, land) in enumerate(zip(src_refs, land_refs)):
        local.append(pltpu.make_async_copy(src.at[my_idx] if scatter[p] else src, land.at[my_idx], local_sems.at[p]))
    for k in range(1, N_DEV):
        px, py, pc = _flip(x, k & 4), _flip(y, k & 2), _flip(c, k & 1)
        p_idx = 4 * px + 2 * py + pc
        for p, (src, land) in enumerate(zip(src_refs, land_refs)):
            s = 7 * p + k - 1
            out = src.at[p_idx] if scatter[p] else src
            send = pltpu.make_async_remote_copy(src_ref=out, dst_ref=land.at[my_idx], send_sem=send_sems.at[s],
                                                recv_sem=recv_sems.at[s], device_id=(px, py, pc), device_id_type=MESH_T)
            arrival = pltpu.make_async_remote_copy(src_ref=out, dst_ref=land.at[p_idx], send_sem=send_sems.at[s],
                                                   recv_sem=recv_sems.at[s], device_id=(px, py, pc), device_id_type=MESH_T)
            remote.append((send, arrival))
    return local, remote


def _carried_call(core, carry, *, grid, in_specs, out_specs, out_shape, scratch_shapes=(), **kw):
    n_in, n_out, nc, n_scr = len(in_specs), len(out_specs), len(carry), len(scratch_shapes)
    if nc == 0:
        return pl.pallas_call(core, grid=grid, in_specs=in_specs, out_specs=out_specs, out_shape=out_shape,
                              scratch_shapes=list(scratch_shapes), **kw)
    scatter = [sc for _, sc in carry]

    def body(*refs):
        ins, cin = refs[:n_in], refs[n_in:n_in + nc]
        outs, cout = refs[n_in + nc:n_in + nc + n_out], refs[n_in + nc + n_out:n_in + 2 * nc + n_out]
        scr, sems = refs[n_in + 2 * nc + n_out:n_in + 2 * nc + n_out + n_scr], refs[-3:]
        first = functools.reduce(jnp.logical_and, [pl.program_id(a) == 0 for a in range(len(grid))])
        last = functools.reduce(jnp.logical_and, [pl.program_id(a) == grid[a] - 1 for a in range(len(grid))])

        @pl.when(first)
        def _():
            local, remote = _exchange_copies(cin, cout, scatter, *sems)
            for cp in local:
                cp.start()
            for send, _ in remote:
                send.start()

        core(*ins, *outs, *scr)

        @pl.when(last)
        def _():
            local, remote = _exchange_copies(cin, cout, scatter, *sems)
            for _, arrival in remote:
                arrival.wait_recv()
            for send, _ in remote:
                send.wait_send()
            for cp in local:
                cp.wait()

    hbm = pl.BlockSpec(memory_space=pl.ANY)
    land_shape = [SDS((N_DEV, *(a.shape[1:] if sc else a.shape)), a.dtype) for a, sc in carry]
    sems = [pltpu.SemaphoreType.DMA((7 * nc,)), pltpu.SemaphoreType.DMA((7 * nc,)), pltpu.SemaphoreType.DMA((nc,))]
    call = pl.pallas_call(body, grid=grid, in_specs=list(in_specs) + [hbm] * nc, out_specs=list(out_specs) + [hbm] * nc,
                          out_shape=list(out_shape) + land_shape, scratch_shapes=list(scratch_shapes) + sems, **kw)
    return lambda *args: call(*args, *[a for a, _ in carry])


def _norm_matmul(h, g, w_t, tn, name, w_extra_t=None, carry=()):
    t, n = h.shape[0], w_t.shape[0]
    nt, nb = t // TM, n // tn

    def body(*refs):
        if w_extra_t is None:
            h_ref, g_ref, w_ref, u_ref, p_ref = refs
        else:
            h_ref, g_ref, w_ref, we_ref, u_ref, p_ref, e_ref = refs

        @pl.when(pl.program_id(1) == 0)
        def _():
            def blk(r0):
                x = h_ref[pl.ds(r0, 32), :]
                rinv = lax.rsqrt(jnp.mean(x * x, axis=-1, keepdims=True) + RMS_EPS)
                u_ref[pl.ds(r0, 32), :] = (x * rinv * g_ref[...]).astype(BF16)
            _row_loop(TM, 32, blk)
            if w_extra_t is not None:
                e_ref[...] = _dot_nt(u_ref[...], we_ref[...]).astype(BF16)

        p_ref[...] = _dot_nt(u_ref[...], w_ref[...]).astype(BF16)

    in_specs = [pl.BlockSpec((TM, D), lambda i, j: (i, 0)), _resident((1, D)), pl.BlockSpec((tn, D), lambda i, j: (j, 0))]
    out_specs = [pl.BlockSpec((TM, D), lambda i, j: (i, 0)), pl.BlockSpec((TM, tn), lambda i, j: (i, j))]
    out_shape = [SDS((t, D), BF16), SDS((t, n), BF16)]
    args = [h, g, w_t]
    if w_extra_t is not None:
        in_specs.append(_resident(w_extra_t.shape))
        out_specs.append(pl.BlockSpec((TM, w_extra_t.shape[0]), lambda i, j: (i, 0)))
        out_shape.append(SDS((t, w_extra_t.shape[0]), BF16))
        args.append(w_extra_t)
    return _carried_call(body, carry, name=name, grid=(nt, nb), in_specs=in_specs, out_specs=out_specs,
                         out_shape=out_shape, compiler_params=_ARB2)(*args)


def _gla_decay(a_ref, wau_ref, ba_ref):
    z = _dot(a_ref[...], wau_ref[...]) + ba_ref[...]
    la = (jnp.minimum(z, 0.0) - jnp.log(1.0 + jnp.exp(-jnp.abs(z)))) * (1.0 / TAU)
    r = lax.broadcasted_iota(jnp.int32, (CH, CH), 0)
    c = lax.broadcasted_iota(jnp.int32, (CH, CH), 1)
    sel = jnp.concatenate([(r >= c).astype(F32), (c <= CH // 2).astype(F32), jnp.ones((CH, CH), F32)], axis=0)
    cum = jnp.dot(sel, la, precision=lax.Precision.HIGHEST, preferred_element_type=F32)
    return z, cum[:CH], cum[CH:2 * CH], cum[2 * CH:], r >= c


def _gla_fwd(proj, alr, wau, balpha, gn):
    t = proj.shape[0]
    nc = t // CH

    def body(qk_ref, v_ref, r_ref, a_ref, wau_ref, ba_ref, gn_ref, o_ref, og_ref, sall_ref, s_scr):
        @pl.when(pl.program_id(0) == 0)
        def _():
            s_scr[...] = jnp.zeros_like(s_scr)

        sall_ref[0] = s_scr[...]
        _, b, bmid, blast, causal = _gla_decay(a_ref, wau_ref, ba_ref)
        for h in range(HEADS):
            ks = slice(h * DKH, (h + 1) * DKH)
            vs = slice(h * DVH, (h + 1) * DVH)
            bh, mh, lh = b[:, ks], bmid[:, ks], blast[:, ks]
            q = qk_ref[:, ks].astype(F32) * (DKH ** -0.5)
            k = qk_ref[:, DK + h * DKH:DK + (h + 1) * DKH].astype(F32)
            v = v_ref[:, vs]
            qt = (q * jnp.exp(bh - mh)).astype(BF16)
            kt = (k * jnp.exp(mh - bh)).astype(BF16)
            qg = (q * jnp.exp(bh)).astype(BF16)
            kg = (k * jnp.exp(lh - bh)).astype(BF16)
            a = jnp.where(causal, _dot_nt(qt, kt), 0.0)
            st = s_scr[vs, :]
            o = _dot(a.astype(BF16), v) + _dot_nt(qg, st.astype(BF16))
            el = jnp.exp(lh)
            s_scr[vs, :] = st * jnp.concatenate([el, el], axis=0) + _dot_tn(v, kg)
            o_ref[:, vs] = o
            on = o * lax.rsqrt(jnp.mean(o * o, axis=-1, keepdims=True) + RMS_EPS) * gn_ref[:, vs]
            rr = r_ref[:, vs].astype(F32)
            og_ref[:, vs] = (on * (rr * _sigmoid(rr))).astype(BF16)

    return pl.pallas_call(
        body, name="gla_fwd", grid=(nc,),
        in_specs=[pl.BlockSpec((CH, D), lambda c: (c, 0)), pl.BlockSpec((CH, D), lambda c: (c, 1)),
                  pl.BlockSpec((CH, D), lambda c: (c, 6)), pl.BlockSpec((CH, LANES), lambda c: (c, 0)),
                  _resident((LANES, DK)), _resident((1, DK)), _resident((1, DV))],
        out_specs=[pl.BlockSpec((CH, DV), lambda c: (c, 0)), pl.BlockSpec((CH, DV), lambda c: (c, 0)),
                   pl.BlockSpec((1, DV, DKH), lambda c: (c, 0, 0))],
        out_shape=[SDS((t, DV), F32), SDS((t, DV), BF16), SDS((nc, DV, DKH), F32)],
        scratch_shapes=[pltpu.VMEM((DV, DKH), F32)], compiler_params=_ARB1)(proj, proj, proj, alr, wau, balpha, gn)


def _conf_fwd(proj, cw, cb, lg, lb, carry=()):
    t = proj.shape[0]
    nt = t // TE

    def body(c1_ref, c2_ref, cw_ref, cb_ref, lg_ref, lb_ref, cc_ref, cs_ref, cext):
        i = pl.program_id(0)

        @pl.when(i == 0)
        def _():
            cext[0:HALO, :] = jnp.zeros((HALO, D), F32)

        @pl.when(i > 0)
        def _():
            cext[0:HALO, :] = cext[TE:TE + HALO, :]

        def glu(r0):
            c2 = c2_ref[pl.ds(r0, 32), :].astype(F32)
            cext[pl.ds(HALO + r0, 32), :] = c1_ref[pl.ds(r0, 32), :].astype(F32) * _sigmoid(c2)
        _row_loop(TE, 32, glu)

        def conv(r0):
            win = cext[pl.ds(r0, 16 + HALO), :]
            acc = jnp.zeros((16, D), F32) + cb_ref[...]
            for j in range(CONF_K):
                acc = acc + cw_ref[j:j + 1, :] * win[2 + j:2 + j + 16, :]
            cc_ref[pl.ds(r0, 16), :] = acc
            mu = jnp.mean(acc, axis=-1, keepdims=True)
            xc = acc - mu
            var = jnp.mean(xc * xc, axis=-1, keepdims=True)
            ln = xc * lax.rsqrt(var + LN_EPS) * lg_ref[...] + lb_ref[...]
            cs_ref[pl.ds(r0, 16), :] = (ln * _sigmoid(ln)).astype(BF16)
        _row_loop(TE, 16, conv)

    return _carried_call(
        body, carry, name="conf_fwd", grid=(nt,),
        in_specs=[pl.BlockSpec((TE, D), lambda i: (i, 2)), pl.BlockSpec((TE, D), lambda i: (i, 3)),
                  _resident((32, D)), _resident((1, D)), _resident((1, D)), _resident((1, D))],
        out_specs=[pl.BlockSpec((TE, D), lambda i: (i, 0)), pl.BlockSpec((TE, D), lambda i: (i, 0))],
        out_shape=[SDS((t, D), F32), SDS((t, D), BF16)],
        scratch_shapes=[pltpu.VMEM((TE + HALO, D), F32)], compiler_params=_ARB1)(proj, proj, cw, cb, lg, lb)


def _mix_fwd(og, cs, proj, h0, wg, wc, wo):
    t = h0.shape[0]
    nt = t // TE

    def body(og_ref, cs_ref, g_ref, h0_ref, wg_ref, wc_ref, wo_ref, brg_ref, brc_ref, mg_ref, h1_ref, acc):
        acc[...] = _dot(og_ref[...], wg_ref[...])
        brg_ref[...] = acc[...].astype(BF16)
        acc[...] = _dot(cs_ref[...], wc_ref[...])
        brc_ref[...] = acc[...].astype(BF16)

        def blk(r0):
            rows = pl.ds(r0, 32)
            gg = g_ref[rows, 0:D].astype(F32)
            gc = g_ref[rows, D:2 * D].astype(F32)
            m = _sigmoid(gg) * brg_ref[rows, :].astype(F32) + _sigmoid(gc) * brc_ref[rows, :].astype(F32)
            mg_ref[rows, :] = m.astype(BF16)
        _row_loop(TE, 32, blk)
        h1_ref[...] = h0_ref[...] + _dot(mg_ref[...], wo_ref[...])

    row = lambda w: pl.BlockSpec((TE, w), lambda i: (i, 0))
    return pl.pallas_call(
        body, name="mix_fwd", grid=(nt,),
        in_specs=[row(D), row(D), pl.BlockSpec((TE, 2 * D), lambda i: (i, 2)), row(D),
                  _resident((D, D)), _resident((D, D)), _resident((D, D))],
        out_specs=[row(D), row(D), row(D), row(D)],
        out_shape=[SDS((t, D), BF16), SDS((t, D), BF16), SDS((t, D), BF16), SDS((t, D), F32)],
        scratch_shapes=[pltpu.VMEM((TE, D), F32)], compiler_params=_ARB1)(og, cs, proj, h0, wg, wc, wo)


def _ffn_out(up, fw, fb, h1, wd, gf, tgt, n_real):
    t = h1.shape[0]
    nt = t // TE

    def body(a_ref, bv_ref, fw_ref, fb_ref, h1_ref, wd_ref, gf_ref, tg_ref, f_ref, dh2_ref, red_ref, aext, hs):
        i = pl.program_id(0)

        @pl.when(i == 0)
        def _():
            aext[0:HALO_F, :] = jnp.zeros((HALO_F, DFF), F32)
            red_ref[...] = jnp.zeros_like(red_ref)

        @pl.when(i > 0)
        def _():
            aext[0:HALO_F, :] = aext[TE:TE + HALO_F, :]

        def cp(r0):
            aext[pl.ds(HALO_F + r0, 16), :] = a_ref[pl.ds(r0, 16), :].astype(F32)
        _row_loop(TE, 16, cp)

        def conv(r0):
            win = aext[pl.ds(r0, 32), :]
            ac = fb_ref[...] + fw_ref[0:1, :] * win[14:30, :] + fw_ref[1:2, :] * win[15:31, :] + fw_ref[2:3, :] * win[16:32, :]
            f_ref[pl.ds(r0, 16), :] = (ac * _sigmoid(ac) * bv_ref[pl.ds(r0, 16), :].astype(F32)).astype(BF16)
        _row_loop(TE, 16, conv)
        hs[...] = h1_ref[...] + _dot(f_ref[...], wd_ref[...])

        def head(r0):
            rows = pl.ds(r0, 32)
            h2 = hs[rows, :]
            rinv = lax.rsqrt(jnp.mean(h2 * h2, axis=-1, keepdims=True) + RMS_EPS)
            hh = h2 * rinv
            gid = i * TE + r0 + lax.broadcasted_iota(jnp.int32, (32, 1), 0)
            live = jnp.logical_and(gid >= N_META, gid < n_real)
            err = jnp.where(live, hh * gf_ref[...] - tg_ref[rows, :], 0.0)
            dy = err * (1.0 / D)
            red_ref[0:8, :] += _rows8(err * err)
            red_ref[8:16, :] += _rows8(dy * hh)
            dhh = dy * gf_ref[...]
            dh2_ref[rows, :] = rinv * (dhh - hh * jnp.mean(dhh * hh, axis=-1, keepdims=True))
        _row_loop(TE, 32, head)

    row = lambda w: pl.BlockSpec((TE, w), lambda i: (i, 0))
    return pl.pallas_call(
        body, name="ffn_out", grid=(nt,),
        in_specs=[pl.BlockSpec((TE, DFF), lambda i: (i, 0)), pl.BlockSpec((TE, DFF), lambda i: (i, 1)),
                  _resident((8, DFF)), _resident((1, DFF)), row(D), _resident((DFF, D)), _resident((1, D)), row(D)],
        out_specs=[row(DFF), row(D), _resident((16, D))],
        out_shape=[SDS((t, DFF), BF16), SDS((t, D), F32), SDS((16, D), F32)],
        scratch_shapes=[pltpu.VMEM((TE + HALO_F, DFF), F32), pltpu.VMEM((TE, D), F32)],
        compiler_params=_ARB1)(up, up, fw, fb, h1, wd, gf, tgt)


def _ffn_bwd(dh2, wd, up, fw, fb):
    t = dh2.shape[0]
    nt = t // TE
    hb = TE // HALO_F

    def body(dh_ref, wd_ref, a_ref, ah_ref, bv_ref, fw_ref, fb_ref, dup_ref, dw_ref, aext, dax, dfs):
        i = pl.program_id(0)
        ti = nt - 1 - i

        @pl.when(i == 0)
        def _():
            dax[TE:TE + HALO_F, :] = jnp.zeros((HALO_F, DFF), F32)
            dw_ref[...] = jnp.zeros_like(dw_ref)

        @pl.when(i > 0)
        def _():
            dax[TE:TE + HALO_F, :] = dax[0:HALO_F, :]

        aext[0:HALO_F, :] = jnp.where(ti > 0, ah_ref[...].astype(F32), 0.0)
        dfs[...] = _dot_nt(dh_ref[...].astype(BF16), wd_ref[...])

        def cp(r0):
            aext[pl.ds(HALO_F + r0, 16), :] = a_ref[pl.ds(r0, 16), :].astype(F32)
        _row_loop(TE, 16, cp)

        def act(r0):
            rows = pl.ds(r0, 16)
            win = aext[pl.ds(r0, 32), :]
            ac = fb_ref[...] + fw_ref[0:1, :] * win[14:30, :] + fw_ref[1:2, :] * win[15:31, :] + fw_ref[2:3, :] * win[16:32, :]
            sg = _sigmoid(ac)
            df = dfs[rows, :]
            dup_ref[rows, DFF:2 * DFF] = (df * ac * sg).astype(BF16)
            dac = df * bv_ref[rows, :].astype(F32) * sg * (1.0 + ac * (1.0 - sg))
            dax[rows, :] = dac
            dw_ref[3] += _rows8(dac)
            for j in range(FFN_K):
                dw_ref[j] += _rows8(dac * win[14 + j:30 + j, :])
        _row_loop(TE, 16, act)

        def convt(r0):
            win = dax[pl.ds(r0, 32), :]
            da = fw_ref[2:3, :] * win[0:16, :] + fw_ref[1:2, :] * win[1:17, :] + fw_ref[0:1, :] * win[2:18, :]
            dup_ref[pl.ds(r0, 16), 0:DFF] = da.astype(BF16)
        _row_loop(TE, 16, convt)

    rev = lambda w: pl.BlockSpec((TE, w), lambda i: (nt - 1 - i, 0))
    return pl.pallas_call(
        body, name="ffn_bwd", grid=(nt,),
        in_specs=[rev(D), _resident((DFF, D)), rev(DFF),
                  pl.BlockSpec((HALO_F, DFF), lambda i: (jnp.maximum((nt - 1 - i) * hb - 1, 0), 0)),
                  pl.BlockSpec((TE, DFF), lambda i: (nt - 1 - i, 1)), _resident((8, DFF)), _resident((1, DFF))],
        out_specs=[rev(2 * DFF), _resident((4, 8, DFF))],
        out_shape=[SDS((t, 2 * DFF), BF16), SDS((4, 8, DFF), F32)],
        scratch_shapes=[pltpu.VMEM((TE + HALO_F, DFF), F32), pltpu.VMEM((TE + HALO_F, DFF), F32),
                        pltpu.VMEM((TE, DFF), F32)],
        compiler_params=_ARB1)(dh2, wd, up, up, up, fw, fb)


def _dgrad_norm(dy, w_t, tk, h, g, dres, name, dy_extra=None, w_extra_t=None, carry=()):
    t, k = dy.shape
    nt, nk = t // TM, k // tk

    def body(*refs):
        if dy_extra is None:
            dy_ref, w_ref, h_ref, g_ref, dr_ref, dh_ref, dg_ref, acc = refs
        else:
            dy_ref, w_ref, h_ref, g_ref, dr_ref, de_ref, we_ref, dh_ref, dg_ref, acc = refs
        i, j = pl.program_id(0), pl.program_id(1)

        @pl.when(jnp.logical_and(i == 0, j == 0))
        def _():
            dg_ref[...] = jnp.zeros_like(dg_ref)

        @pl.when(j == 0)
        def _():
            if dy_extra is None:
                acc[...] = jnp.zeros_like(acc)
            else:
                acc[...] = _dot(de_ref[...], we_ref[...])

        acc[...] += _dot(dy_ref[...], w_ref[...])

        @pl.when(j == nk - 1)
        def _():
            def blk(r0):
                rows = pl.ds(r0, 32)
                x = h_ref[rows, :]
                rinv = lax.rsqrt(jnp.mean(x * x, axis=-1, keepdims=True) + RMS_EPS)
                hh = x * rinv
                du = acc[rows, :]
                dg_ref[...] += _rows8(du * hh)
                dhh = du * g_ref[...]
                dh_ref[rows, :] = dr_ref[rows, :] + rinv * (dhh - hh * jnp.mean(dhh * hh, axis=-1, keepdims=True))
            _row_loop(TM, 32, blk)

    row = pl.BlockSpec((TM, D), lambda i, j: (i, 0))
    in_specs = [pl.BlockSpec((TM, tk), lambda i, j: (i, j)), pl.BlockSpec((tk, D), lambda i, j: (j, 0)), row, _resident((1, D)), row]
    args = [dy, w_t, h, g, dres]
    if dy_extra is not None:
        in_specs += [pl.BlockSpec((TM, dy_extra.shape[1]), lambda i, j: (i, 0)), _resident(w_extra_t.shape)]
        args += [dy_extra, w_extra_t]
    return _carried_call(
        body, carry, name=name, grid=(nt, nk), in_specs=in_specs, out_specs=[row, _resident((8, D))],
        out_shape=[SDS((t, D), F32), SDS((8, D), F32)],
        scratch_shapes=[pltpu.VMEM((TM, D), F32)], compiler_params=_ARB2)(*args)


def _wgrad(x, dy, tk, name):
    t, k = x.shape
    n = dy.shape[1]
    nk, nt = k // tk, t // TM

    def body(x_ref, dy_ref, o_ref, acc):
        @pl.when(pl.program_id(1) == 0)
        def _():
            acc[...] = jnp.zeros_like(acc)
        acc[...] += _dot_tn(x_ref[...], dy_ref[...].astype(BF16))

        @pl.when(pl.program_id(1) == nt - 1)
        def _():
            o_ref[...] = acc[...].astype(BF16)

    return pl.pallas_call(
        body, name=name, grid=(nk, nt),
        in_specs=[pl.BlockSpec((TM, tk), lambda j, i: (i, j)), pl.BlockSpec((TM, n), lambda j, i: (i, 0))],
        out_specs=pl.BlockSpec((tk, n), lambda j, i: (j, 0)), out_shape=SDS((k, n), BF16),
        scratch_shapes=[pltpu.VMEM((tk, n), F32)], compiler_params=_ARB2)(x, dy)


def _mix_bwd(dh1, wo, wg, wc, proj, brg, brc):
    t = dh1.shape[0]
    nt = t // TE

    def body(dh_ref, wo_ref, wg_ref, wc_ref, g_ref, brg_ref, brc_ref, dbg_ref, dbc_ref, dog_ref, dcs_ref, dp_ref, dm):
        dm[...] = _dot_nt(dh_ref[...].astype(BF16), wo_ref[...])

        def blk(r0):
            rows = pl.ds(r0, 32)
            d = dm[rows, :]
            sg = _sigmoid(g_ref[rows, 0:D].astype(F32))
            sc = _sigmoid(g_ref[rows, D:2 * D].astype(F32))
            dbg_ref[rows, :] = (d * sg).astype(BF16)
            dbc_ref[rows, :] = (d * sc).astype(BF16)
            dp_ref[rows, 0:D] = (d * brg_ref[rows, :].astype(F32) * sg * (1.0 - sg)).astype(BF16)
            dp_ref[rows, D:2 * D] = (d * brc_ref[rows, :].astype(F32) * sc * (1.0 - sc)).astype(BF16)
        _row_loop(TE, 32, blk)
        dog_ref[...] = _dot_nt(dbg_ref[...], wg_ref[...]).astype(BF16)
        dcs_ref[...] = _dot_nt(dbc_ref[...], wc_ref[...]).astype(BF16)

    row = pl.BlockSpec((TE, D), lambda i: (i, 0))
    wide = pl.BlockSpec((TE, 2 * D), lambda i: (i, 2))
    return pl.pallas_call(
        body, name="mix_bwd", grid=(nt,),
        in_specs=[row, _resident((D, D)), _resident((D, D)), _resident((D, D)), wide, row, row],
        out_specs=[row, row, row, row, wide],
        out_shape=[SDS((t, D), BF16)] * 4 + [SDS((t, NPROJ), BF16)],
        scratch_shapes=[pltpu.VMEM((TE, D), F32)], compiler_params=_ARB1)(dh1, wo, wg, wc, proj, brg, brc)


def _glapost_bwd(dog, o, proj, gn, dproj):
    t = o.shape[0]
    nt = t // TE

    def body(dog_ref, o_ref, r_ref, gn_ref, dp_in, do_ref, dp_ref, dgn_ref):
        del dp_in

        @pl.when(pl.program_id(0) == 0)
        def _():
            dgn_ref[...] = jnp.zeros_like(dgn_ref)

        def blk(r0):
            rows = pl.ds(r0, 32)
            for h in range(HEADS):
                vs = slice(h * DVH, (h + 1) * DVH)
                x = o_ref[rows, vs]
                rinv = lax.rsqrt(jnp.mean(x * x, axis=-1, keepdims=True) + RMS_EPS)
                oh = x * rinv
                g = gn_ref[:, vs]
                rr = r_ref[rows, vs].astype(F32)
                sr = _sigmoid(rr)
                d = dog_ref[rows, vs].astype(F32)
                dp_ref[rows, vs] = (d * oh * g * sr * (1.0 + rr * (1.0 - sr))).astype(BF16)
                don = d * rr * sr
                dgn_ref[:, vs] += _rows8(don * oh)
                doh = don * g
                do_ref[rows, vs] = (rinv * (doh - oh * jnp.mean(doh * oh, axis=-1, keepdims=True))).astype(BF16)
        _row_loop(TE, 32, blk)

    row = pl.BlockSpec((TE, D), lambda i: (i, 0))
    rcol = pl.BlockSpec((TE, D), lambda i: (i, 6))
    return pl.pallas_call(
        body, name="glapost_bwd", grid=(nt,),
        in_specs=[row, row, rcol, _resident((1, D)), pl.BlockSpec(memory_space=pl.ANY)],
        out_specs=[row, rcol, _resident((8, D))],
        out_shape=[SDS((t, D), BF16), SDS((t, NPROJ), BF16), SDS((8, D), F32)],
        input_output_aliases={4: 1}, compiler_params=_ARB1)(dog, o, proj, gn, dproj)


def _conf_bwd(dcs, cc, proj, cw, lg, lb, dproj, carry=()):
    t = cc.shape[0]
    nt = t // TE
    hb = TE // HALO

    def body(dcs_ref, cc_ref, c1_ref, c2_ref, c1h_ref, c2h_ref, cw_ref, lg_ref, lb_ref, dp_in,
             dp_ref, dw_ref, ds_ref, cext, dext):
        del dp_in
        i = pl.program_id(0)
        ti = nt - 1 - i

        @pl.when(i == 0)
        def _():
            dext[TE:TE + HALO, :] = jnp.zeros((HALO, D), F32)
            dw_ref[...] = jnp.zeros_like(dw_ref)
            ds_ref[...] = jnp.zeros_like(ds_ref)

        @pl.when(i > 0)
        def _():
            dext[TE:TE + HALO, :] = dext[0:HALO, :]

        ch = c1h_ref[...].astype(F32) * _sigmoid(c2h_ref[...].astype(F32))
        cext[0:HALO, :] = jnp.where(ti > 0, ch, 0.0)

        def pre(r0):
            rows = pl.ds(r0, 32)
            cext[pl.ds(HALO + r0, 32), :] = c1_ref[rows, :].astype(F32) * _sigmoid(c2_ref[rows, :].astype(F32))
            x = cc_ref[rows, :]
            mu = jnp.mean(x, axis=-1, keepdims=True)
            xc = x - mu
            rstd = lax.rsqrt(jnp.mean(xc * xc, axis=-1, keepdims=True) + LN_EPS)
            xh = xc * rstd
            ln = xh * lg_ref[...] + lb_ref[...]
            sg = _sigmoid(ln)
            dln = dcs_ref[rows, :].astype(F32) * sg * (1.0 + ln * (1.0 - sg))
            ds_ref[0] += _rows8(dln * xh)
            ds_ref[1] += _rows8(dln)
            dxh = dln * lg_ref[...]
            dcc = rstd * (dxh - jnp.mean(dxh, axis=-1, keepdims=True) - xh * jnp.mean(dxh * xh, axis=-1, keepdims=True))
            dext[rows, :] = dcc
            ds_ref[2] += _rows8(dcc)
        _row_loop(TE, 32, pre)

        def convt(r0):
            rows = pl.ds(r0, 16)
            wd = dext[pl.ds(r0, 16 + HALO), :]
            wc = cext[pl.ds(r0, 16 + HALO), :]
            dcc = wd[0:16, :]
            dc = jnp.zeros((16, D), F32)
            for j in range(CONF_K):
                dc = dc + cw_ref[j:j + 1, :] * wd[30 - j:46 - j, :]
                dw_ref[j] += _rows8(dcc * wc[2 + j:18 + j, :])
            c1 = c1_ref[rows, :].astype(F32)
            s2 = _sigmoid(c2_ref[rows, :].astype(F32))
            dp_ref[rows, 0:D] = (dc * s2).astype(BF16)
            dp_ref[rows, D:2 * D] = (dc * c1 * s2 * (1.0 - s2)).astype(BF16)
        _row_loop(TE, 16, convt)

    rev = lambda col: pl.BlockSpec((TE, D), lambda i: (nt - 1 - i, col))
    halo = lambda col: pl.BlockSpec((HALO, D), lambda i: (jnp.maximum((nt - 1 - i) * hb - 1, 0), col))
    return _carried_call(
        body, carry, name="conf_bwd", grid=(nt,),
        in_specs=[rev(0), rev(0), rev(2), rev(3), halo(2), halo(3), _resident((32, D)), _resident((1, D)),
                  _resident((1, D)), pl.BlockSpec(memory_space=pl.ANY)],
        out_specs=[pl.BlockSpec((TE, 2 * D), lambda i: (nt - 1 - i, 1)), _resident((32, 8, D)), _resident((3, 8, D))],
        out_shape=[SDS((t, NPROJ), BF16), SDS((32, 8, D), F32), SDS((3, 8, D), F32)],
        scratch_shapes=[pltpu.VMEM((TE + HALO, D), F32), pltpu.VMEM((TE + HALO, D), F32)],
        input_output_aliases={9: 0}, compiler_params=_ARB1)(dcs, cc, proj, proj, proj, proj, cw, lg, lb, dproj)


def _gla_bwd(proj, alr, wau, balpha, do, sall, dproj, carry=()):
    t = proj.shape[0]
    nc = t // CH

    def body(qk_ref, v_ref, a_ref, wau_ref, ba_ref, do_ref, s_ref, dp_in, dp_ref, da_ref, dwau_ref, dba_ref, ds_scr, dla_scr):
        del dp_in

        @pl.when(pl.program_id(0) == 0)
        def _():
            ds_scr[...] = jnp.zeros_like(ds_scr)
            dwau_ref[...] = jnp.zeros_like(dwau_ref)
            dba_ref[...] = jnp.zeros_like(dba_ref)

        z, b, bmid, blast, causal = _gla_decay(a_ref, wau_ref, ba_ref)
        r = lax.broadcasted_iota(jnp.int32, (CH, CH), 0)
        c = lax.broadcasted_iota(jnp.int32, (CH, CH), 1)
        upper = (r <= c).astype(F32)
        for h in range(HEADS):
            ks = slice(h * DKH, (h + 1) * DKH)
            vs = slice(h * DVH, (h + 1) * DVH)
            bh, mh, lh = b[:, ks], bmid[:, ks], blast[:, ks]
            q = qk_ref[:, ks].astype(F32) * (DKH ** -0.5)
            k = qk_ref[:, DK + h * DKH:DK + (h + 1) * DKH].astype(F32)
            v = v_ref[:, vs]
            dout = do_ref[:, vs]
            eq, ek, eb, eg, el = jnp.exp(bh - mh), jnp.exp(mh - bh), jnp.exp(bh), jnp.exp(lh - bh), jnp.exp(lh)
            qt, kt = (q * eq).astype(BF16), (k * ek).astype(BF16)
            qg, kg = (q * eb).astype(BF16), (k * eg).astype(BF16)
            st = s_ref[0, vs, :]
            dsn = ds_scr[vs, :]
            st16, dsn16 = st.astype(BF16), dsn.astype(BF16)
            a = jnp.where(causal, _dot_nt(qt, kt), 0.0).astype(BF16)
            da = jnp.where(causal, _dot_nt(dout, v), 0.0).astype(BF16)
            dq_inter = _dot(dout, st16) * eb
            dk_inter = _dot(v, dsn16) * eg
            dq = _dot(da, kt) * eq + dq_inter
            dk = _dot_tn(da, qt) * ek + dk_inter
            dv = _dot_tn(a, dout) + _dot_nt(kg, dsn16)
            dlast = jnp.sum(k * dk_inter, axis=0, keepdims=True) + jnp.sum(st * dsn, axis=0, keepdims=True) * el[0:1, :]
            db = q * dq - k * dk
            dla_scr[:, ks] = jnp.dot(upper, db, precision=lax.Precision.HIGHEST, preferred_element_type=F32) + dlast
            ds_scr[vs, :] = dsn * jnp.concatenate([el, el], axis=0) + _dot_tn(dout, qg)
            dp_ref[:, ks] = (dq * (DKH ** -0.5)).astype(BF16)
            dp_ref[:, DK + h * DKH:DK + (h + 1) * DKH] = dk.astype(BF16)
            dp_ref[:, D + h * DVH:D + (h + 1) * DVH] = dv.astype(BF16)
        dz = (dla_scr[...] * (1.0 / TAU) * _sigmoid(-z)).astype(BF16)
        da_ref[...] = _dot_nt(dz, wau_ref[...]).astype(BF16)
        dwau_ref[...] += _dot_tn(a_ref[...], dz)
        dba_ref[...] += _rows8(dz.astype(F32))

    rev = lambda w, col: pl.BlockSpec((CH, w), lambda c: (nc - 1 - c, col))
    return _carried_call(
        body, carry, name="gla_bwd", grid=(nc,),
        in_specs=[rev(D, 0), rev(D, 1), rev(LANES, 0), _resident((LANES, DK)), _resident((1, DK)), rev(D, 0),
                  pl.BlockSpec((1, DV, DKH), lambda c: (nc - 1 - c, 0, 0)), pl.BlockSpec(memory_space=pl.ANY)],
        out_specs=[rev(2 * D, 0), rev(LANES, 0), _resident((LANES, DK)), _resident((8, DK))],
        out_shape=[SDS((t, NPROJ), BF16), SDS((t, LANES), BF16), SDS((LANES, DK), F32), SDS((8, DK), F32)],
        scratch_shapes=[pltpu.VMEM((DV, DKH), F32), pltpu.VMEM((CH, DK), F32)],
        input_output_aliases={7: 0}, compiler_params=_ARB1)(proj, proj, alr, wau, balpha, do, sall, dproj)


def _all_gather(xs, name):
    n = len(xs)

    def body(*refs):
        x_refs, out_refs = refs[:n], refs[n:2 * n]
        send_sems, recv_sems, local_sems = refs[2 * n:]
        x, y, c = _my_place()
        me, sibling = (x, y, c), (x, y, 1 - c)
        chips = [(1 - x, y), (x, 1 - y), (1 - x, 1 - y)]

        def slot(p, px, py, pc):
            return out_refs[p].at[4 * px + 2 * py + pc]

        def copy(p, k, block, to, src=None):
            return pltpu.make_async_remote_copy(
                src_ref=slot(p, *block) if src is None else src, dst_ref=slot(p, *block),
                send_sem=send_sems.at[7 * p + k], recv_sem=recv_sems.at[7 * p + k], device_id=to, device_id_type=MESH_T)

        mine = [pltpu.make_async_copy(x_refs[p], slot(p, *me), local_sems.at[p]) for p in range(n)]
        for cp in mine:
            cp.start()
        first = []
        for p in range(n):
            first.append(copy(p, 0, me, sibling, src=x_refs[p]))
            first += [copy(p, 1 + j, me, (*chip, c), src=x_refs[p]) for j, chip in enumerate(chips)]
        for cp in first:
            cp.start()
        passed = []
        for p in range(n):
            for j, chip in enumerate(chips):
                copy(p, 1 + j, (*chip, c), me).wait_recv()
                fwd = copy(p, 4 + j, (*chip, c), sibling)
                fwd.start()
                passed.append(fwd)
        for p in range(n):
            copy(p, 0, sibling, me).wait_recv()
            for j, chip in enumerate(chips):
                copy(p, 4 + j, (*chip, 1 - c), me).wait_recv()
        for cp in first + passed:
            cp.wait_send()
        for cp in mine:
            cp.wait()

    hbm = pl.BlockSpec(memory_space=pl.ANY)
    return pl.pallas_call(
        body, name=name, out_shape=[SDS((N_DEV, *a.shape), a.dtype) for a in xs],
        in_specs=[hbm] * n, out_specs=[hbm] * n,
        scratch_shapes=[pltpu.SemaphoreType.DMA((7 * n,)), pltpu.SemaphoreType.DMA((7 * n,)),
                        pltpu.SemaphoreType.DMA((n,))])(*xs)


def _exchange(gs):
    n = len(gs)

    def body(*refs):
        g_refs, land_refs = refs[:n], refs[n:2 * n]
        send_sems, recv_sems, local_sems = refs[2 * n:]
        x, y, c = _my_place()
        my_idx = 4 * x + 2 * y + c
        mine = [pltpu.make_async_copy(g_refs[p].at[my_idx], land_refs[p].at[my_idx], local_sems.at[p]) for p in range(n)]
        for cp in mine:
            cp.start()
        copies = []
        for k in range(1, N_DEV):
            px, py, pc = _flip(x, k & 4), _flip(y, k & 2), _flip(c, k & 1)
            p_idx = 4 * px + 2 * py + pc
            for p in range(n):
                s = 7 * p + k - 1
                cp = pltpu.make_async_remote_copy(
                    src_ref=g_refs[p].at[p_idx], dst_ref=land_refs[p].at[my_idx], send_sem=send_sems.at[s],
                    recv_sem=recv_sems.at[s], device_id=(px, py, pc), device_id_type=MESH_T)
                cp.start()
                arrival = pltpu.make_async_remote_copy(
                    src_ref=g_refs[p].at[p_idx], dst_ref=land_refs[p].at[p_idx], send_sem=send_sems.at[s],
                    recv_sem=recv_sems.at[s], device_id=(px, py, pc), device_id_type=MESH_T)
                copies.append((cp, arrival))
        for cp, arrival in copies:
            arrival.wait_recv()
        for cp, arrival in copies:
            cp.wait_send()
        for cp in mine:
            cp.wait()

    hbm = pl.BlockSpec(memory_space=pl.ANY)
    return pl.pallas_call(
        body, name="grad_exchange", out_shape=[SDS(g.shape, g.dtype) for g in gs],
        in_specs=[hbm] * n, out_specs=[hbm] * n,
        scratch_shapes=[pltpu.SemaphoreType.DMA((7 * n,)), pltpu.SemaphoreType.DMA((7 * n,)),
                        pltpu.SemaphoreType.DMA((n,))])(*gs)


def _adamw(land, w, m, v, rows_blk, name):
    rows = w.shape[0]

    def body(l_ref, w_ref, m_ref, v_ref, g_ref, d_ref, nm_ref, nv_ref):
        g = l_ref[0].astype(F32)
        for s in range(1, N_DEV):
            g = g + l_ref[s].astype(F32)
        nm = ADAM_B1 * m_ref[...] + (1.0 - ADAM_B1) * g
        nv = ADAM_B2 * v_ref[...] + (1.0 - ADAM_B2) * (g * g)
        m_hat = nm / (1.0 - ADAM_B1 ** ADAM_STEP)
        v_hat = nv / (1.0 - ADAM_B2 ** ADAM_STEP)
        g_ref[...] = g
        d_ref[...] = -ADAM_LR * (m_hat / (jnp.sqrt(v_hat) + ADAM_EPS) + ADAM_WD * w_ref[...])
        nm_ref[...] = nm
        nv_ref[...] = nv

    blk = pl.BlockSpec((rows_blk, D), lambda i: (i, 0))
    return pl.pallas_call(
        body, name=name, grid=(rows // rows_blk,),
        in_specs=[pl.BlockSpec((N_DEV, rows_blk, D), lambda i: (0, i, 0)), blk, blk, blk],
        out_specs=[blk] * 4, out_shape=[SDS((rows, D), F32)] * 4, compiler_params=_ARB1)(land, w, m, v)


BIG = ("w_in", "w_up", "w_down", "w_gla_o", "w_conf_o", "w_out")
BIG_TRANSPOSED = ("w_in", "w_up")
SMALL_SHARDED = ("meta_tokens", "conf_dw_w", "ffn_dw_w", "w_alpha_up")
REPLICATED = ("norm_mix_g", "b_alpha", "gla_norm_g", "conf_dw_b", "conf_ln_g", "conf_ln_b", "norm_ffn_g", "ffn_dw_b",
              "final_norm_g")
N_IN = sum(IN_WIDTHS)
W_IN_ROWS = N_IN // N_DEV
W_IN_PAD = -(-W_IN_ROWS // 16) * 16
ADAM_BLOCK = {"w_in": W_IN_PAD // 3, "w_up": 176, "w_down": 176, "w_gla_o": 128, "w_conf_o": 128, "w_out": 128}
SMALL_ROWS = 32


def _to_panel(name, shard):
    a = shard.reshape(shard.shape[-2], shard.shape[-1])
    if name in BIG_TRANSPOSED:
        a = a.T
    if name == "w_in":
        a = jnp.pad(a, ((0, W_IN_PAD - W_IN_ROWS), (0, 0)))
    return a


def _from_panel(name, panel, shape):
    a = panel[0:W_IN_ROWS] if name == "w_in" else panel
    if name in BIG_TRANSPOSED:
        a = a.T
    return a.reshape(shape)


def _pack_small(arrs):
    flat = jnp.concatenate([jnp.pad(a.reshape(-1), (0, (-a.size) % D)) for a in arrs])
    return jnp.pad(flat, (0, SMALL_ROWS * D - flat.shape[0])).reshape(SMALL_ROWS, D)


def _unpack_small(panel, shapes):
    flat, out, off = panel.reshape(-1), [], 0
    for shp in shapes:
        n = 1
        for s in shp:
            n *= s
        out.append(flat[off:off + n].reshape(shp))
        off += n + (-n) % D
    return out


def _local_step(x, target, w, shards=None):
    dist = shards is not None
    w = dict(w)

    def gather(names):
        return [(shards[n], False) for n in names] if dist else []

    def scatter(*arrs):
        return [(a.reshape(N_DEV, -1, D), True) for a in arrs] if dist else []

    s = x.shape[0]
    n_real = s + N_META
    t = -(-n_real // TM) * TM
    h0 = jnp.concatenate([w["meta_tokens"], x, jnp.zeros((t - n_real, D), F32)], axis=0)
    tgt = jnp.concatenate([jnp.zeros((N_META, D), F32), target, jnp.zeros((t - n_real, D), F32)], axis=0)

    q0, r0, a0, c0 = 0, 2 * DK + DV, 2 * DK + 2 * DV, 2 * DK + 2 * DV + RANK
    wt = w["w_in_t"]
    w_main = jnp.concatenate([wt[q0:r0], wt[c0:N_IN], wt[r0:a0]], axis=0)
    w_a = jnp.pad(wt[a0:c0], ((0, LANES - RANK), (0, 0)))
    wau = jnp.pad(w["w_alpha_up"].astype(BF16), ((0, LANES - RANK), (0, 0)))
    row = lambda name: w[name].reshape(1, -1)
    cw = jnp.pad(w["conf_dw_w"], ((0, 32 - CONF_K), (0, 0)))
    fw = jnp.pad(w["ffn_dw_w"], ((0, 8 - FFN_K), (0, 0)))

    early = ("w_gla_o", "w_conf_o", "w_out", "w_up")
    u1, proj, alr, *landed = _norm_matmul(h0, row("norm_mix_g"), w_main, 1024, "in_proj", w_extra_t=w_a, carry=gather(early))
    for n, land in zip(early, landed):
        w["w_up_t" if n == "w_up" else n] = land.reshape(-1, D)
    o, og, sall = _gla_fwd(proj, alr, wau, row("b_alpha"), row("gla_norm_g"))
    cc, cs, *landed = _conf_fwd(proj, cw, row("conf_dw_b"), row("conf_ln_g"), row("conf_ln_b"), carry=gather(("w_down",)))
    if dist:
        w["w_down"] = landed[0].reshape(-1, D)
    brg, brc, merged, h1 = _mix_fwd(og, cs, proj, h0, w["w_gla_o"], w["w_conf_o"], w["w_out"])
    u2, up = _norm_matmul(h1, row("norm_ffn_g"), w["w_up_t"], 512, "up_proj")
    f, dh2, red = _ffn_out(up, fw, row("ffn_dw_b"), h1, w["w_down"], row("final_norm_g"), tgt, n_real)
    loss = 0.5 / D * jnp.sum(red[0:8])

    g = {"final_norm_g": jnp.sum(red[8:16], axis=0)}
    dup, dfw = _ffn_bwd(dh2, w["w_down"], up, fw, row("ffn_dw_b"))
    g["ffn_dw_w"] = jnp.sum(dfw[0:FFN_K], axis=1)
    g["ffn_dw_b"] = jnp.sum(dfw[3], axis=0)
    g["w_down"] = _wgrad(f, dh2, 1408, "wgrad_down")
    dh1, dg2, *landed = _dgrad_norm(dup, w["w_up_t"], 512, h1, row("norm_ffn_g"), dh2, "up_dgrad", carry=scatter(g["w_down"]))
    if dist:
        g["w_down"] = landed[0]
    g["norm_ffn_g"] = jnp.sum(dg2, axis=0)
    g["w_up_t"] = _wgrad(dup, u2, 1408, "wgrad_up")
    dbrg, dbrc, dog, dcs, dproj = _mix_bwd(dh1, w["w_out"], w["w_gla_o"], w["w_conf_o"], proj, brg, brc)
    g["w_out"] = _wgrad(merged, dh1, 1024, "wgrad_out")
    g["w_gla_o"] = _wgrad(og, dbrg, 1024, "wgrad_gla_o")
    g["w_conf_o"] = _wgrad(cs, dbrc, 1024, "wgrad_conf_o")
    do, dproj, dgn = _glapost_bwd(dog, o, proj, row("gla_norm_g"), dproj)
    g["gla_norm_g"] = jnp.sum(dgn, axis=0)
    dproj, dcw, dst, *landed = _conf_bwd(dcs, cc, proj, cw, row("conf_ln_g"), row("conf_ln_b"), dproj,
                                         carry=scatter(g["w_up_t"]))
    if dist:
        g["w_up_t"] = landed[0]
    g["conf_dw_w"] = jnp.sum(dcw[0:CONF_K], axis=1)
    g["conf_ln_g"], g["conf_ln_b"], g["conf_dw_b"] = jnp.sum(dst[0], axis=0), jnp.sum(dst[1], axis=0), jnp.sum(dst[2], axis=0)
    dproj, dalr, dwau, dba, *landed = _gla_bwd(proj, alr, wau, row("b_alpha"), do, sall, dproj,
                                               carry=scatter(g["w_out"], g["w_gla_o"], g["w_conf_o"]))
    if dist:
        g["w_out"], g["w_gla_o"], g["w_conf_o"] = landed
    g["w_alpha_up"] = dwau[0:RANK]
    g["b_alpha"] = jnp.sum(dba, axis=0)
    dw_main = _wgrad(dproj, u1, 1024, "wgrad_in")
    dw_a = _wgrad(dalr, u1, LANES, "wgrad_alr")
    g["w_in_t"] = jnp.concatenate([dw_main[0:r0], dw_main[NPROJ - DV:NPROJ], dw_a[0:RANK], dw_main[r0:NPROJ - DV]], axis=0)
    w_in_blocks = []
    if dist:
        pad = ((0, 0), (0, W_IN_PAD - W_IN_ROWS), (0, 0))
        w_in_blocks = [(jnp.pad(g["w_in_t"].reshape(N_DEV, W_IN_ROWS, D), pad), True)]
    dh0, dg1, *landed = _dgrad_norm(dproj, w_main, 1024, h0, row("norm_mix_g"), dh1, "in_dgrad", dy_extra=dalr,
                                    w_extra_t=w_a, carry=w_in_blocks)
    if dist:
        g["w_in_t"] = landed[0]
    g["norm_mix_g"] = jnp.sum(dg1, axis=0)
    g["meta_tokens"] = dh0[0:N_META]
    return loss, dh0[N_META:n_real], g


def kernel(x, meta_tokens, norm_mix_g, w_in, w_alpha_up, b_alpha, gla_norm_g, w_gla_o, conf_dw_w, conf_dw_b, conf_ln_g, conf_ln_b, w_conf_o, w_out, norm_ffn_g, w_up, ffn_dw_w, ffn_dw_b, w_down, final_norm_g, loss_target, m_meta_tokens, m_norm_mix_g, m_w_in, m_w_alpha_up, m_b_alpha, m_gla_norm_g, m_w_gla_o, m_conf_dw_w, m_conf_dw_b, m_conf_ln_g, m_conf_ln_b, m_w_conf_o, m_w_out, m_norm_ffn_g, m_w_up, m_ffn_dw_w, m_ffn_dw_b, m_w_down, m_final_norm_g, v_meta_tokens, v_norm_mix_g, v_w_in, v_w_alpha_up, v_b_alpha, v_gla_norm_g, v_w_gla_o, v_conf_dw_w, v_conf_dw_b, v_conf_ln_g, v_conf_ln_b, v_w_conf_o, v_w_out, v_norm_ffn_g, v_w_up, v_ffn_dw_w, v_ffn_dw_b, v_w_down, v_final_norm_g):
    ws = dict(meta_tokens=meta_tokens, norm_mix_g=norm_mix_g, w_in=w_in, w_alpha_up=w_alpha_up, b_alpha=b_alpha,
              gla_norm_g=gla_norm_g, w_gla_o=w_gla_o, conf_dw_w=conf_dw_w, conf_dw_b=conf_dw_b, conf_ln_g=conf_ln_g,
              conf_ln_b=conf_ln_b, w_conf_o=w_conf_o, w_out=w_out, norm_ffn_g=norm_ffn_g, w_up=w_up, ffn_dw_w=ffn_dw_w,
              ffn_dw_b=ffn_dw_b, w_down=w_down, final_norm_g=final_norm_g)
    ms = dict(meta_tokens=m_meta_tokens, norm_mix_g=m_norm_mix_g, w_in=m_w_in, w_alpha_up=m_w_alpha_up, b_alpha=m_b_alpha,
              gla_norm_g=m_gla_norm_g, w_gla_o=m_w_gla_o, conf_dw_w=m_conf_dw_w, conf_dw_b=m_conf_dw_b,
              conf_ln_g=m_conf_ln_g, conf_ln_b=m_conf_ln_b, w_conf_o=m_w_conf_o, w_out=m_w_out, norm_ffn_g=m_norm_ffn_g,
              w_up=m_w_up, ffn_dw_w=m_ffn_dw_w, ffn_dw_b=m_ffn_dw_b, w_down=m_w_down, final_norm_g=m_final_norm_g)
    vs = dict(meta_tokens=v_meta_tokens, norm_mix_g=v_norm_mix_g, w_in=v_w_in, w_alpha_up=v_w_alpha_up, b_alpha=v_b_alpha,
              gla_norm_g=v_gla_norm_g, w_gla_o=v_w_gla_o, conf_dw_w=v_conf_dw_w, conf_dw_b=v_conf_dw_b,
              conf_ln_g=v_conf_ln_g, conf_ln_b=v_conf_ln_b, w_conf_o=v_w_conf_o, w_out=v_w_out, norm_ffn_g=v_norm_ffn_g,
              w_up=v_w_up, ffn_dw_w=v_ffn_dw_w, ffn_dw_b=v_ffn_dw_b, w_down=v_w_down, final_norm_g=v_final_norm_g)
    small = SMALL_SHARDED + REPLICATED
    pack_small = lambda d: _pack_small([d[n] for n in small])

    shards = {n: _to_panel(n, ws[n]).astype(BF16) for n in BIG}
    gathered = _all_gather([shards["w_in"], pack_small(ws)], "weight_gather")
    full = {n: ws[n].reshape(-1) for n in REPLICATED}
    full["w_in_t"] = gathered[0][:, 0:W_IN_ROWS].reshape(N_IN, D)
    flat, off = gathered[1].reshape(N_DEV, -1), 0
    for n in SMALL_SHARDED:
        k, c = ws[n].shape[-2], ws[n].shape[-1]
        full[n] = flat[:, off:off + k * c].reshape(N_DEV, k, c).transpose(1, 0, 2).reshape(k, N_DEV * c)
        off += k * c + (-(k * c)) % D

    loss, grad_x, g = _local_step(x[0], loss_target[0], full, shards)

    lands = [g["w_in_t"], g["w_up_t"]] + [g[n] for n in BIG[2:]]
    blocks = []
    for n in SMALL_SHARDED:
        k, c = ws[n].shape[-2], ws[n].shape[-1]
        b = g[n].reshape(k, N_DEV, c).transpose(1, 0, 2).reshape(N_DEV, k * c)
        blocks.append(jnp.pad(b, ((0, 0), (0, (-(k * c)) % D))))
    for n in REPLICATED:
        b = jnp.broadcast_to(g[n].reshape(1, -1), (N_DEV, g[n].size))
        blocks.append(jnp.pad(b, ((0, 0), (0, (-b.shape[1]) % D))))
    gsm = jnp.concatenate(blocks, axis=1)
    lands += _exchange([jnp.pad(gsm, ((0, 0), (0, SMALL_ROWS * D - gsm.shape[1]))).reshape(N_DEV, SMALL_ROWS, D)])

    grad, delta, new_m, new_v = {}, {}, {}, {}
    for i, n in enumerate(BIG):
        outs = _adamw(lands[i], _to_panel(n, ws[n]), _to_panel(n, ms[n]), _to_panel(n, vs[n]), ADAM_BLOCK[n], "adamw_" + n)
        grad[n], delta[n], new_m[n], new_v[n] = [_from_panel(n, p, ws[n].shape) for p in outs]
    outs = _adamw(lands[len(BIG)], pack_small(ws), pack_small(ms), pack_small(vs), SMALL_ROWS, "adamw_small")
    shapes = [ws[n].shape for n in small]
    for d, p in zip((grad, delta, new_m, new_v), outs):
        d.update(zip(small, _unpack_small(p, shapes)))

    order = ("meta_tokens", "norm_mix_g", "w_in", "w_alpha_up", "b_alpha", "gla_norm_g", "w_gla_o", "conf_dw_w", "conf_dw_b",
             "conf_ln_g", "conf_ln_b", "w_conf_o", "w_out", "norm_ffn_g", "w_up", "ffn_dw_w", "ffn_dw_b", "w_down",
             "final_norm_g")
    loss = lax.psum(loss, ("x", "y", "c"))
    return (loss, grad_x[None], *[grad[n] for n in order], *[delta[n] for n in order], *[new_m[n] for n in order],
            *[new_v[n] for n in order])
```

```python
import functools

import jax
import jax.numpy as jnp
from jax import lax
from jax.experimental import pallas as pl
from jax.experimental.pallas import tpu as pltpu

F32, BF16 = jnp.float32, jnp.bfloat16
SDS = jax.ShapeDtypeStruct

D = 1024
N_META = 16
HEADS = 4
DK, DKH, DV, DVH = 512, 128, 1024, 256
RANK = 16
TAU = 16.0
CONF_K = 31
DFF = 2816
FFN_K = 3
IN_WIDTHS = (DK, DK, DV, DV, RANK, 2 * D, D, D)
RMS_EPS, LN_EPS = 1e-6, 1e-5
ADAM_LR, ADAM_B1, ADAM_B2, ADAM_EPS, ADAM_WD, ADAM_STEP = 0.001, 0.9, 0.999, 1e-08, 0.01, 10

NPROJ = 7 * D
LANES = 128
CH = 128
TM = 640
TM_FWD = 1664
TM_DGRAD = 1040
TE = 320
HALO = 32
HALO_F = 16
N_DEV = 8
VMEM_LIMIT = 60 * 1024 * 1024
MESH_T = pl.DeviceIdType.MESH

_ARB1 = pltpu.CompilerParams(dimension_semantics=("arbitrary",), vmem_limit_bytes=VMEM_LIMIT)
_ARB2 = pltpu.CompilerParams(dimension_semantics=("arbitrary", "arbitrary"), vmem_limit_bytes=VMEM_LIMIT)


def _dot(a, b):
    return jnp.dot(a, b, preferred_element_type=F32)


def _dot_nt(a, b):
    return lax.dot_general(a, b, (((1,), (1,)), ((), ())), preferred_element_type=F32)


def _dot_tn(a, b):
    return lax.dot_general(a, b, (((0,), (0,)), ((), ())), preferred_element_type=F32)


def _sigmoid(x):
    return 1.0 / (1.0 + jnp.exp(-x))


def _rows8(x):
    return x.reshape(x.shape[0] // 8, 8, x.shape[1]).sum(axis=0)


def _row_tile(t, preferred):
    return preferred if t % preferred == 0 else TM


def _row_loop(n_rows, rb, fn):
    def step(i, carry):
        fn(pl.multiple_of(i * rb, rb))
        return carry
    lax.fori_loop(0, n_rows // rb, step, 0)


def _resident(shape):
    return pl.BlockSpec(shape, lambda *_: (0,) * len(shape))


CONV_ROWS, CONV_LANES = 64, 256


def _shift_classes(offset_taps):
    return [(s, [(o - s, j) for o, j in offset_taps if o % 8 == s]) for s in range(8)]


def _shifted(win, s):
    return win if s == 0 else win[s:s + CONV_ROWS + HALO - 8, :]


def _my_place():
    return lax.axis_index("x"), lax.axis_index("y"), lax.axis_index("c")


def _flip(v, bit):
    return 1 - v if bit else v


def _exchange_copies(src_refs, land_refs, scatter, send_sems, recv_sems, local_sems, arrivals):
    x, y, c = _my_place()
    my_idx = 4 * x + 2 * y + c
    local, remote = [], []
    for p, (src, land) in enumerate(zip(src_refs, land_refs)):
        local.append(pltpu.make_async_copy(src.at[my_idx] if scatter[p] else src, land.at[my_idx], local_sems.at[p]))
    for k in range(1, N_DEV):
        px, py, pc = _flip(x, k & 4), _flip(y, k & 2), _flip(c, k & 1)
        p_idx = 4 * px + 2 * py + pc
        for p, (src, land) in enumerate(zip(src_refs, land_refs)):
            s = 7 * p + k - 1
            out = src.at[p_idx] if scatter[p] else src

            def copy(dst):
                return pltpu.make_async_remote_copy(src_ref=out, dst_ref=dst, send_sem=send_sems.at[s],
                                                    recv_sem=recv_sems.at[s], device_id=(px, py, pc), device_id_type=MESH_T)
            remote.append((copy(land.at[my_idx]), copy(land.at[p_idx]) if arrivals else None))
    return local, remote


def _carried_call(core, carry, *, grid, in_specs, out_specs, out_shape, scratch_shapes=(), **kw):
    n_in, n_out, nc, n_scr = len(in_specs), len(out_specs), len(carry), len(scratch_shapes)
    if nc == 0:
        return pl.pallas_call(core, grid=grid, in_specs=in_specs, out_specs=out_specs, out_shape=out_shape,
                              scratch_shapes=list(scratch_shapes), **kw)
    scatter = [sc for _, sc in carry]

    def body(*refs):
        ins, cin = refs[:n_in], refs[n_in:n_in + nc]
        outs, cout = refs[n_in + nc:n_in + nc + n_out], refs[n_in + nc + n_out:n_in + 2 * nc + n_out]
        scr, sems = refs[n_in + 2 * nc + n_out:n_in + 2 * nc + n_out + n_scr], refs[-3:]
        first = functools.reduce(jnp.logical_and, [pl.program_id(a) == 0 for a in range(len(grid))])
        last = functools.reduce(jnp.logical_and, [pl.program_id(a) == grid[a] - 1 for a in range(len(grid))])

        @pl.when(first)
        def _():
            local, remote = _exchange_copies(cin, cout, scatter, *sems, arrivals=False)
            for cp in local:
                cp.start()
            for send, _ in remote:
                send.start()

        core(*ins, *outs, *scr)

        @pl.when(last)
        def _():
            local, remote = _exchange_copies(cin, cout, scatter, *sems, arrivals=True)
            for _, arrival in remote:
                arrival.wait_recv()
            for send, _ in remote:
                send.wait_send()
            for cp in local:
                cp.wait()

    hbm = pl.BlockSpec(memory_space=pl.ANY)
    land_shape = [SDS((N_DEV, *(a.shape[1:] if sc else a.shape)), a.dtype) for a, sc in carry]
    sems = [pltpu.SemaphoreType.DMA((7 * nc,)), pltpu.SemaphoreType.DMA((7 * nc,)), pltpu.SemaphoreType.DMA((nc,))]
    call = pl.pallas_call(body, grid=grid, in_specs=list(in_specs) + [hbm] * nc, out_specs=list(out_specs) + [hbm] * nc,
                          out_shape=list(out_shape) + land_shape, scratch_shapes=list(scratch_shapes) + sems, **kw)
    return lambda *args: call(*args, *[a for a, _ in carry])


def _norm_matmul(h, g, w_t, tn, name, w_extra_t=None, carry=()):
    t, n = h.shape[0], w_t.shape[0]
    tm = _row_tile(t, TM_FWD)
    nt, nb = t // tm, n // tn

    def body(*refs):
        if w_extra_t is None:
            h_ref, g_ref, w_ref, u_ref, p_ref = refs
        else:
            h_ref, g_ref, w_ref, we_ref, u_ref, p_ref, e_ref = refs

        @pl.when(pl.program_id(1) == 0)
        def _():
            def blk(r0):
                x = h_ref[pl.ds(r0, 32), :]
                rinv = lax.rsqrt(jnp.mean(x * x, axis=-1, keepdims=True) + RMS_EPS)
                u_ref[pl.ds(r0, 32), :] = (x * rinv * g_ref[...]).astype(BF16)
            _row_loop(tm, 32, blk)
            if w_extra_t is not None:
                e_ref[...] = _dot_nt(u_ref[...], we_ref[...]).astype(BF16)

        p_ref[...] = _dot_nt(u_ref[...], w_ref[...]).astype(BF16)

    in_specs = [pl.BlockSpec((tm, D), lambda i, j: (i, 0)), _resident((1, D)), pl.BlockSpec((tn, D), lambda i, j: (j, 0))]
    out_specs = [pl.BlockSpec((tm, D), lambda i, j: (i, 0)), pl.BlockSpec((tm, tn), lambda i, j: (i, j))]
    out_shape = [SDS((t, D), BF16), SDS((t, n), BF16)]
    args = [h, g, w_t]
    if w_extra_t is not None:
        in_specs.append(_resident(w_extra_t.shape))
        out_specs.append(pl.BlockSpec((tm, w_extra_t.shape[0]), lambda i, j: (i, 0)))
        out_shape.append(SDS((t, w_extra_t.shape[0]), BF16))
        args.append(w_extra_t)
    return _carried_call(body, carry, name=name, grid=(nt, nb), in_specs=in_specs, out_specs=out_specs,
                         out_shape=out_shape, compiler_params=_ARB2)(*args)


def _gla_decay(a_ref, wau_ref, ba_ref):
    z = _dot(a_ref[...], wau_ref[...]) + ba_ref[...]
    la = (jnp.minimum(z, 0.0) - jnp.log(1.0 + jnp.exp(-jnp.abs(z)))) * (1.0 / TAU)
    r = lax.broadcasted_iota(jnp.int32, (CH, CH), 0)
    c = lax.broadcasted_iota(jnp.int32, (CH, CH), 1)
    sel = jnp.concatenate([(r >= c).astype(F32), (c <= CH // 2).astype(F32), jnp.ones((CH, CH), F32)], axis=0)
    cum = jnp.dot(sel, la, precision=lax.Precision.HIGHEST, preferred_element_type=F32)
    return z, cum[:CH], cum[CH:2 * CH], cum[2 * CH:], r >= c


def _gla_fwd(proj, alr, wau, balpha, gn):
    t = proj.shape[0]
    nc = t // CH

    def body(qk_ref, v_ref, r_ref, a_ref, wau_ref, ba_ref, gn_ref, o_ref, og_ref, sall_ref, s_scr):
        @pl.when(pl.program_id(0) == 0)
        def _():
            s_scr[...] = jnp.zeros_like(s_scr)

        sall_ref[0] = s_scr[...]
        _, b, bmid, blast, causal = _gla_decay(a_ref, wau_ref, ba_ref)
        for h in range(HEADS):
            ks = slice(h * DKH, (h + 1) * DKH)
            vs = slice(h * DVH, (h + 1) * DVH)
            bh, mh, lh = b[:, ks], bmid[:, ks], blast[:, ks]
            q = qk_ref[:, ks].astype(F32) * (DKH ** -0.5)
            k = qk_ref[:, DK + h * DKH:DK + (h + 1) * DKH].astype(F32)
            v = v_ref[:, vs]
            qt = (q * jnp.exp(bh - mh)).astype(BF16)
            kt = (k * jnp.exp(mh - bh)).astype(BF16)
            qg = (q * jnp.exp(bh)).astype(BF16)
            kg = (k * jnp.exp(lh - bh)).astype(BF16)
            a = jnp.where(causal, _dot_nt(qt, kt), 0.0)
            st = s_scr[vs, :]
            o = _dot(a.astype(BF16), v) + _dot_nt(qg, st.astype(BF16))
            el = jnp.exp(lh)
            s_scr[vs, :] = st * jnp.concatenate([el, el], axis=0) + _dot_tn(v, kg)
            o_ref[:, vs] = o
            on = o * lax.rsqrt(jnp.mean(o * o, axis=-1, keepdims=True) + RMS_EPS) * gn_ref[:, vs]
            rr = r_ref[:, vs].astype(F32)
            og_ref[:, vs] = (on * (rr * _sigmoid(rr))).astype(BF16)

    return pl.pallas_call(
        body, name="gla_fwd", grid=(nc,),
        in_specs=[pl.BlockSpec((CH, D), lambda c: (c, 0)), pl.BlockSpec((CH, D), lambda c: (c, 1)),
                  pl.BlockSpec((CH, D), lambda c: (c, 6)), pl.BlockSpec((CH, LANES), lambda c: (c, 0)),
                  _resident((LANES, DK)), _resident((1, DK)), _resident((1, DV))],
        out_specs=[pl.BlockSpec((CH, DV), lambda c: (c, 0)), pl.BlockSpec((CH, DV), lambda c: (c, 0)),
                   pl.BlockSpec((1, DV, DKH), lambda c: (c, 0, 0))],
        out_shape=[SDS((t, DV), F32), SDS((t, DV), BF16), SDS((nc, DV, DKH), F32)],
        scratch_shapes=[pltpu.VMEM((DV, DKH), F32)], compiler_params=_ARB1)(proj, proj, proj, alr, wau, balpha, gn)


def _conf_fwd(proj, cw, cb, lg, lb, carry=()):
    t = proj.shape[0]
    nt = t // TE

    def body(c1_ref, c2_ref, cw_ref, cb_ref, lg_ref, lb_ref, cc_ref, cs_ref, cext):
        i = pl.program_id(0)

        @pl.when(i == 0)
        def _():
            cext[0:HALO, :] = jnp.zeros((HALO, D), F32)

        @pl.when(i > 0)
        def _():
            cext[0:HALO, :] = cext[TE:TE + HALO, :]

        def glu(r0):
            c2 = c2_ref[pl.ds(r0, 32), :].astype(F32)
            cext[pl.ds(HALO + r0, 32), :] = c1_ref[pl.ds(r0, 32), :].astype(F32) * _sigmoid(c2)
        _row_loop(TE, 32, glu)

        def conv(r0):
            for part in range(D // CONV_LANES):
                cols = slice(part * CONV_LANES, (part + 1) * CONV_LANES)
                win = cext[pl.ds(r0, CONV_ROWS + HALO), cols]
                acc = jnp.zeros((CONV_ROWS, CONV_LANES), F32) + cb_ref[:, cols]
                for s, taps in _shift_classes([(2 + j, j) for j in range(CONF_K)]):
                    ws = _shifted(win, s)
                    for a8, j in taps:
                        acc = acc + jnp.tile(cw_ref[j, :, cols], (CONV_ROWS // 8, 1)) * ws[a8:a8 + CONV_ROWS, :]
                cc_ref[pl.ds(r0, CONV_ROWS), cols] = acc
            for sub in range(CONV_ROWS // 32):
                rows = pl.ds(r0 + 32 * sub, 32)
                x = cc_ref[rows, :]
                xc = x - jnp.mean(x, axis=-1, keepdims=True)
                var = jnp.mean(xc * xc, axis=-1, keepdims=True)
                ln = xc * lax.rsqrt(var + LN_EPS) * lg_ref[...] + lb_ref[...]
                cs_ref[rows, :] = (ln * _sigmoid(ln)).astype(BF16)
        _row_loop(TE, CONV_ROWS, conv)

    return _carried_call(
        body, carry, name="conf_fwd", grid=(nt,),
        in_specs=[pl.BlockSpec((TE, D), lambda i: (i, 2)), pl.BlockSpec((TE, D), lambda i: (i, 3)),
                  _resident((32, 8, D)), _resident((1, D)), _resident((1, D)), _resident((1, D))],
        out_specs=[pl.BlockSpec((TE, D), lambda i: (i, 0)), pl.BlockSpec((TE, D), lambda i: (i, 0))],
        out_shape=[SDS((t, D), F32), SDS((t, D), BF16)],
        scratch_shapes=[pltpu.VMEM((TE + HALO, D), F32)], compiler_params=_ARB1)(proj, proj, cw, cb, lg, lb)


def _mix_fwd(og, cs, proj, h0, wg, wc, wo):
    t = h0.shape[0]
    nt = t // TE

    def body(og_ref, cs_ref, g_ref, h0_ref, wg_ref, wc_ref, wo_ref, brg_ref, brc_ref, mg_ref, h1_ref, acc):
        acc[...] = _dot(og_ref[...], wg_ref[...])
        brg_ref[...] = acc[...].astype(BF16)
        acc[...] = _dot(cs_ref[...], wc_ref[...])
        brc_ref[...] = acc[...].astype(BF16)

        def blk(r0):
            rows = pl.ds(r0, 32)
            gg = g_ref[rows, 0:D].astype(F32)
            gc = g_ref[rows, D:2 * D].astype(F32)
            m = _sigmoid(gg) * brg_ref[rows, :].astype(F32) + _sigmoid(gc) * brc_ref[rows, :].astype(F32)
            mg_ref[rows, :] = m.astype(BF16)
        _row_loop(TE, 32, blk)
        h1_ref[...] = h0_ref[...] + _dot(mg_ref[...], wo_ref[...])

    row = lambda w: pl.BlockSpec((TE, w), lambda i: (i, 0))
    return pl.pallas_call(
        body, name="mix_fwd", grid=(nt,),
        in_specs=[row(D), row(D), pl.BlockSpec((TE, 2 * D), lambda i: (i, 2)), row(D),
                  _resident((D, D)), _resident((D, D)), _resident((D, D))],
        out_specs=[row(D), row(D), row(D), row(D)],
        out_shape=[SDS((t, D), BF16), SDS((t, D), BF16), SDS((t, D), BF16), SDS((t, D), F32)],
        scratch_shapes=[pltpu.VMEM((TE, D), F32)], compiler_params=_ARB1)(og, cs, proj, h0, wg, wc, wo)


def _ffn_out(up, fw, fb, h1, wd, gf, tgt, n_real):
    t = h1.shape[0]
    nt = t // TE

    def body(a_ref, bv_ref, fw_ref, fb_ref, h1_ref, wd_ref, gf_ref, tg_ref, f_ref, dh2_ref, red_ref, aext, hs):
        i = pl.program_id(0)

        @pl.when(i == 0)
        def _():
            aext[0:HALO_F, :] = jnp.zeros((HALO_F, DFF), F32)
            red_ref[...] = jnp.zeros_like(red_ref)

        @pl.when(i > 0)
        def _():
            aext[0:HALO_F, :] = aext[TE:TE + HALO_F, :]

        def cp(r0):
            aext[pl.ds(HALO_F + r0, 16), :] = a_ref[pl.ds(r0, 16), :].astype(F32)
        _row_loop(TE, 16, cp)

        def conv(r0):
            win = aext[pl.ds(r0, 32), :]
            ac = fb_ref[...] + fw_ref[0:1, :] * win[14:30, :] + fw_ref[1:2, :] * win[15:31, :] + fw_ref[2:3, :] * win[16:32, :]
            f_ref[pl.ds(r0, 16), :] = (ac * _sigmoid(ac) * bv_ref[pl.ds(r0, 16), :].astype(F32)).astype(BF16)
        _row_loop(TE, 16, conv)
        hs[...] = h1_ref[...] + _dot(f_ref[...], wd_ref[...])

        def head(r0):
            rows = pl.ds(r0, 32)
            h2 = hs[rows, :]
            rinv = lax.rsqrt(jnp.mean(h2 * h2, axis=-1, keepdims=True) + RMS_EPS)
            hh = h2 * rinv
            gid = i * TE + r0 + lax.broadcasted_iota(jnp.int32, (32, 1), 0)
            live = jnp.logical_and(gid >= N_META, gid < n_real)
            err = jnp.where(live, hh * gf_ref[...] - tg_ref[rows, :], 0.0)
            dy = err * (1.0 / D)
            red_ref[0:8, :] += _rows8(err * err)
            red_ref[8:16, :] += _rows8(dy * hh)
            dhh = dy * gf_ref[...]
            dh2_ref[rows, :] = rinv * (dhh - hh * jnp.mean(dhh * hh, axis=-1, keepdims=True))
        _row_loop(TE, 32, head)

    row = lambda w: pl.BlockSpec((TE, w), lambda i: (i, 0))
    return pl.pallas_call(
        body, name="ffn_out", grid=(nt,),
        in_specs=[pl.BlockSpec((TE, DFF), lambda i: (i, 0)), pl.BlockSpec((TE, DFF), lambda i: (i, 1)),
                  _resident((8, DFF)), _resident((1, DFF)), row(D), _resident((DFF, D)), _resident((1, D)), row(D)],
        out_specs=[row(DFF), row(D), _resident((16, D))],
        out_shape=[SDS((t, DFF), BF16), SDS((t, D), F32), SDS((16, D), F32)],
        scratch_shapes=[pltpu.VMEM((TE + HALO_F, DFF), F32), pltpu.VMEM((TE, D), F32)],
        compiler_params=_ARB1)(up, up, fw, fb, h1, wd, gf, tgt)


def _ffn_bwd(dh2, wd, up, fw, fb):
    t = dh2.shape[0]
    nt = t // TE
    hb = TE // HALO_F

    def body(dh_ref, wd_ref, a_ref, ah_ref, bv_ref, fw_ref, fb_ref, dup_ref, dw_ref, aext, dax, dfs):
        i = pl.program_id(0)
        ti = nt - 1 - i

        @pl.when(i == 0)
        def _():
            dax[TE:TE + HALO_F, :] = jnp.zeros((HALO_F, DFF), F32)
            dw_ref[...] = jnp.zeros_like(dw_ref)

        @pl.when(i > 0)
        def _():
            dax[TE:TE + HALO_F, :] = dax[0:HALO_F, :]

        aext[0:HALO_F, :] = jnp.where(ti > 0, ah_ref[...].astype(F32), 0.0)
        dfs[...] = _dot_nt(dh_ref[...].astype(BF16), wd_ref[...])

        def cp(r0):
            aext[pl.ds(HALO_F + r0, 16), :] = a_ref[pl.ds(r0, 16), :].astype(F32)
        _row_loop(TE, 16, cp)

        def act(r0):
            rows = pl.ds(r0, 16)
            win = aext[pl.ds(r0, 32), :]
            ac = fb_ref[...] + fw_ref[0:1, :] * win[14:30, :] + fw_ref[1:2, :] * win[15:31, :] + fw_ref[2:3, :] * win[16:32, :]
            sg = _sigmoid(ac)
            df = dfs[rows, :]
            dup_ref[rows, DFF:2 * DFF] = (df * ac * sg).astype(BF16)
            dac = df * bv_ref[rows, :].astype(F32) * sg * (1.0 + ac * (1.0 - sg))
            dax[rows, :] = dac
            dw_ref[3] += _rows8(dac)
            for j in range(FFN_K):
                dw_ref[j] += _rows8(dac * win[14 + j:30 + j, :])
        _row_loop(TE, 16, act)

        def convt(r0):
            win = dax[pl.ds(r0, 32), :]
            da = fw_ref[2:3, :] * win[0:16, :] + fw_ref[1:2, :] * win[1:17, :] + fw_ref[0:1, :] * win[2:18, :]
            dup_ref[pl.ds(r0, 16), 0:DFF] = da.astype(BF16)
        _row_loop(TE, 16, convt)

    rev = lambda w: pl.BlockSpec((TE, w), lambda i: (nt - 1 - i, 0))
    return pl.pallas_call(
        body, name="ffn_bwd", grid=(nt,),
        in_specs=[rev(D), _resident((DFF, D)), rev(DFF),
                  pl.BlockSpec((HALO_F, DFF), lambda i: (jnp.maximum((nt - 1 - i) * hb - 1, 0), 0)),
                  pl.BlockSpec((TE, DFF), lambda i: (nt - 1 - i, 1)), _resident((8, DFF)), _resident((1, DFF))],
        out_specs=[rev(2 * DFF), _resident((4, 8, DFF))],
        out_shape=[SDS((t, 2 * DFF), BF16), SDS((4, 8, DFF), F32)],
        scratch_shapes=[pltpu.VMEM((TE + HALO_F, DFF), F32), pltpu.VMEM((TE + HALO_F, DFF), F32),
                        pltpu.VMEM((TE, DFF), F32)],
        compiler_params=_ARB1)(dh2, wd, up, up, up, fw, fb)


def _dgrad_norm(dy, w_t, tk, h, g, dres, name, dy_extra=None, w_extra_t=None, carry=()):
    t, k = dy.shape
    tm = _row_tile(t, TM_DGRAD)
    nt, nk = t // tm, k // tk

    def body(*refs):
        if dy_extra is None:
            dy_ref, w_ref, h_ref, g_ref, dr_ref, dh_ref, dg_ref, acc = refs
        else:
            dy_ref, w_ref, h_ref, g_ref, dr_ref, de_ref, we_ref, dh_ref, dg_ref, acc = refs
        i, j = pl.program_id(0), pl.program_id(1)

        @pl.when(jnp.logical_and(i == 0, j == 0))
        def _():
            dg_ref[...] = jnp.zeros_like(dg_ref)

        @pl.when(j == 0)
        def _():
            if dy_extra is None:
                acc[...] = jnp.zeros_like(acc)
            else:
                acc[...] = _dot(de_ref[...], we_ref[...])

        acc[...] += _dot(dy_ref[...], w_ref[...])

        @pl.when(j == nk - 1)
        def _():
            def blk(r0):
                rows = pl.ds(r0, 16)
                x = h_ref[rows, :]
                rinv = lax.rsqrt(jnp.mean(x * x, axis=-1, keepdims=True) + RMS_EPS)
                hh = x * rinv
                du = acc[rows, :]
                dg_ref[...] += _rows8(du * hh)
                dhh = du * g_ref[...]
                dh_ref[rows, :] = dr_ref[rows, :] + rinv * (dhh - hh * jnp.mean(dhh * hh, axis=-1, keepdims=True))
            _row_loop(tm, 16, blk)

    row = pl.BlockSpec((tm, D), lambda i, j: (i, 0))
    in_specs = [pl.BlockSpec((tm, tk), lambda i, j: (i, j)), pl.BlockSpec((tk, D), lambda i, j: (j, 0)), row, _resident((1, D)), row]
    args = [dy, w_t, h, g, dres]
    if dy_extra is not None:
        in_specs += [pl.BlockSpec((tm, dy_extra.shape[1]), lambda i, j: (i, 0)), _resident(w_extra_t.shape)]
        args += [dy_extra, w_extra_t]
    return _carried_call(
        body, carry, name=name, grid=(nt, nk), in_specs=in_specs, out_specs=[row, _resident((8, D))],
        out_shape=[SDS((t, D), F32), SDS((8, D), F32)],
        scratch_shapes=[pltpu.VMEM((tm, D), F32)], compiler_params=_ARB2)(*args)


def _wgrad(x, dy, tk, name):
    t, k = x.shape
    n = dy.shape[1]
    nk, nt = k // tk, t // TM

    def body(x_ref, dy_ref, o_ref, acc):
        @pl.when(pl.program_id(1) == 0)
        def _():
            acc[...] = jnp.zeros_like(acc)
        acc[...] += _dot_tn(x_ref[...], dy_ref[...].astype(BF16))

        @pl.when(pl.program_id(1) == nt - 1)
        def _():
            o_ref[...] = acc[...].astype(BF16)

    return pl.pallas_call(
        body, name=name, grid=(nk, nt),
        in_specs=[pl.BlockSpec((TM, tk), lambda j, i: (i, j)), pl.BlockSpec((TM, n), lambda j, i: (i, 0))],
        out_specs=pl.BlockSpec((tk, n), lambda j, i: (j, 0)), out_shape=SDS((k, n), BF16),
        scratch_shapes=[pltpu.VMEM((tk, n), F32)], compiler_params=_ARB2)(x, dy)


def _mix_bwd(dh1, wo, wg, wc, proj, brg, brc):
    t = dh1.shape[0]
    nt = t // TE

    def body(dh_ref, wo_ref, wg_ref, wc_ref, g_ref, brg_ref, brc_ref, dbg_ref, dbc_ref, dog_ref, dcs_ref, dp_ref, dm):
        dm[...] = _dot_nt(dh_ref[...].astype(BF16), wo_ref[...])

        def blk(r0):
            rows = pl.ds(r0, 32)
            d = dm[rows, :]
            sg = _sigmoid(g_ref[rows, 0:D].astype(F32))
            sc = _sigmoid(g_ref[rows, D:2 * D].astype(F32))
            dbg_ref[rows, :] = (d * sg).astype(BF16)
            dbc_ref[rows, :] = (d * sc).astype(BF16)
            dp_ref[rows, 0:D] = (d * brg_ref[rows, :].astype(F32) * sg * (1.0 - sg)).astype(BF16)
            dp_ref[rows, D:2 * D] = (d * brc_ref[rows, :].astype(F32) * sc * (1.0 - sc)).astype(BF16)
        _row_loop(TE, 32, blk)
        dog_ref[...] = _dot_nt(dbg_ref[...], wg_ref[...]).astype(BF16)
        dcs_ref[...] = _dot_nt(dbc_ref[...], wc_ref[...]).astype(BF16)

    row = pl.BlockSpec((TE, D), lambda i: (i, 0))
    wide = pl.BlockSpec((TE, 2 * D), lambda i: (i, 2))
    return pl.pallas_call(
        body, name="mix_bwd", grid=(nt,),
        in_specs=[row, _resident((D, D)), _resident((D, D)), _resident((D, D)), wide, row, row],
        out_specs=[row, row, row, row, wide],
        out_shape=[SDS((t, D), BF16)] * 4 + [SDS((t, NPROJ), BF16)],
        scratch_shapes=[pltpu.VMEM((TE, D), F32)], compiler_params=_ARB1)(dh1, wo, wg, wc, proj, brg, brc)


def _glapost_bwd(dog, o, proj, gn, dproj):
    t = o.shape[0]
    nt = t // TE

    def body(dog_ref, o_ref, r_ref, gn_ref, dp_in, do_ref, dp_ref, dgn_ref):
        del dp_in

        @pl.when(pl.program_id(0) == 0)
        def _():
            dgn_ref[...] = jnp.zeros_like(dgn_ref)

        def blk(r0):
            rows = pl.ds(r0, 32)
            for h in range(HEADS):
                vs = slice(h * DVH, (h + 1) * DVH)
                x = o_ref[rows, vs]
                rinv = lax.rsqrt(jnp.mean(x * x, axis=-1, keepdims=True) + RMS_EPS)
                oh = x * rinv
                g = gn_ref[:, vs]
                rr = r_ref[rows, vs].astype(F32)
                sr = _sigmoid(rr)
                d = dog_ref[rows, vs].astype(F32)
                dp_ref[rows, vs] = (d * oh * g * sr * (1.0 + rr * (1.0 - sr))).astype(BF16)
                don = d * rr * sr
                dgn_ref[:, vs] += _rows8(don * oh)
                doh = don * g
                do_ref[rows, vs] = (rinv * (doh - oh * jnp.mean(doh * oh, axis=-1, keepdims=True))).astype(BF16)
        _row_loop(TE, 32, blk)

    row = pl.BlockSpec((TE, D), lambda i: (i, 0))
    rcol = pl.BlockSpec((TE, D), lambda i: (i, 6))
    return pl.pallas_call(
        body, name="glapost_bwd", grid=(nt,),
        in_specs=[row, row, rcol, _resident((1, D)), pl.BlockSpec(memory_space=pl.ANY)],
        out_specs=[row, rcol, _resident((8, D))],
        out_shape=[SDS((t, D), BF16), SDS((t, NPROJ), BF16), SDS((8, D), F32)],
        input_output_aliases={4: 1}, compiler_params=_ARB1)(dog, o, proj, gn, dproj)


def _conf_bwd(dcs, cc, proj, cw, lg, lb, dproj, carry=()):
    t = cc.shape[0]
    nt = t // TE
    hb = TE // HALO

    def body(dcs_ref, cc_ref, c1_ref, c2_ref, c1h_ref, c2h_ref, cw_ref, lg_ref, lb_ref, dp_in,
             dp_ref, dw_ref, ds_ref, cext, dext):
        del dp_in
        i = pl.program_id(0)
        ti = nt - 1 - i

        @pl.when(i == 0)
        def _():
            dext[TE:TE + HALO, :] = jnp.zeros((HALO, D), F32)
            dw_ref[...] = jnp.zeros_like(dw_ref)
            ds_ref[...] = jnp.zeros_like(ds_ref)

        @pl.when(i > 0)
        def _():
            dext[TE:TE + HALO, :] = dext[0:HALO, :]

        ch = c1h_ref[...].astype(F32) * _sigmoid(c2h_ref[...].astype(F32))
        cext[0:HALO, :] = jnp.where(ti > 0, ch, 0.0)

        def pre(r0):
            rows = pl.ds(r0, 32)
            cext[pl.ds(HALO + r0, 32), :] = c1_ref[rows, :].astype(F32) * _sigmoid(c2_ref[rows, :].astype(F32))
            x = cc_ref[rows, :]
            mu = jnp.mean(x, axis=-1, keepdims=True)
            xc = x - mu
            rstd = lax.rsqrt(jnp.mean(xc * xc, axis=-1, keepdims=True) + LN_EPS)
            xh = xc * rstd
            ln = xh * lg_ref[...] + lb_ref[...]
            sg = _sigmoid(ln)
            dln = dcs_ref[rows, :].astype(F32) * sg * (1.0 + ln * (1.0 - sg))
            ds_ref[0] += _rows8(dln * xh)
            ds_ref[1] += _rows8(dln)
            dxh = dln * lg_ref[...]
            dcc = rstd * (dxh - jnp.mean(dxh, axis=-1, keepdims=True) - xh * jnp.mean(dxh * xh, axis=-1, keepdims=True))
            dext[rows, :] = dcc
            ds_ref[2] += _rows8(dcc)
        _row_loop(TE, 32, pre)

        def convt(r0):
            rows = pl.ds(r0, CONV_ROWS)
            for part in range(D // CONV_LANES):
                cols = slice(part * CONV_LANES, (part + 1) * CONV_LANES)
                wd = dext[pl.ds(r0, CONV_ROWS + HALO), cols]
                dc = jnp.zeros((CONV_ROWS, CONV_LANES), F32)
                for s, taps in _shift_classes([(CONF_K - 1 - j, j) for j in range(CONF_K)]):
                    ws = _shifted(wd, s)
                    for a8, j in taps:
                        dc = dc + jnp.tile(cw_ref[j, :, cols], (CONV_ROWS // 8, 1)) * ws[a8:a8 + CONV_ROWS, :]
                dcc = wd[0:CONV_ROWS, :]
                wc = cext[pl.ds(r0, CONV_ROWS + HALO), cols]
                for s, taps in _shift_classes([(2 + j, j) for j in range(CONF_K)]):
                    ws = _shifted(wc, s)
                    for a8, j in taps:
                        dw_ref[j, :, cols] += _rows8(dcc * ws[a8:a8 + CONV_ROWS, :])
                c1 = c1_ref[rows, cols].astype(F32)
                s2 = _sigmoid(c2_ref[rows, cols].astype(F32))
                dp_ref[rows, cols] = (dc * s2).astype(BF16)
                dp_ref[rows, D + part * CONV_LANES:D + (part + 1) * CONV_LANES] = (dc * c1 * s2 * (1.0 - s2)).astype(BF16)
        _row_loop(TE, CONV_ROWS, convt)

    rev = lambda col: pl.BlockSpec((TE, D), lambda i: (nt - 1 - i, col))
    halo = lambda col: pl.BlockSpec((HALO, D), lambda i: (jnp.maximum((nt - 1 - i) * hb - 1, 0), col))
    return _carried_call(
        body, carry, name="conf_bwd", grid=(nt,),
        in_specs=[rev(0), rev(0), rev(2), rev(3), halo(2), halo(3), _resident((32, 8, D)), _resident((1, D)),
                  _resident((1, D)), pl.BlockSpec(memory_space=pl.ANY)],
        out_specs=[pl.BlockSpec((TE, 2 * D), lambda i: (nt - 1 - i, 1)), _resident((32, 8, D)), _resident((3, 8, D))],
        out_shape=[SDS((t, NPROJ), BF16), SDS((32, 8, D), F32), SDS((3, 8, D), F32)],
        scratch_shapes=[pltpu.VMEM((TE + HALO, D), F32), pltpu.VMEM((TE + HALO, D), F32)],
        input_output_aliases={9: 0}, compiler_params=_ARB1)(dcs, cc, proj, proj, proj, proj, cw, lg, lb, dproj)


def _gla_bwd(proj, alr, wau, balpha, do, sall, dproj, carry=()):
    t = proj.shape[0]
    nc = t // CH

    def body(qk_ref, v_ref, a_ref, wau_ref, ba_ref, do_ref, s_ref, dp_in, dp_ref, da_ref, dwau_ref, dba_ref, ds_scr, dla_scr):
        del dp_in

        @pl.when(pl.program_id(0) == 0)
        def _():
            ds_scr[...] = jnp.zeros_like(ds_scr)
            dwau_ref[...] = jnp.zeros_like(dwau_ref)
            dba_ref[...] = jnp.zeros_like(dba_ref)

        z, b, bmid, blast, causal = _gla_decay(a_ref, wau_ref, ba_ref)
        r = lax.broadcasted_iota(jnp.int32, (CH, CH), 0)
        c = lax.broadcasted_iota(jnp.int32, (CH, CH), 1)
        upper = (r <= c).astype(F32)
        for h in range(HEADS):
            ks = slice(h * DKH, (h + 1) * DKH)
            vs = slice(h * DVH, (h + 1) * DVH)
            bh, mh, lh = b[:, ks], bmid[:, ks], blast[:, ks]
            q = qk_ref[:, ks].astype(F32) * (DKH ** -0.5)
            k = qk_ref[:, DK + h * DKH:DK + (h + 1) * DKH].astype(F32)
            v = v_ref[:, vs]
            dout = do_ref[:, vs]
            eq, ek, eb, eg, el = jnp.exp(bh - mh), jnp.exp(mh - bh), jnp.exp(bh), jnp.exp(lh - bh), jnp.exp(lh)
            qt, kt = (q * eq).astype(BF16), (k * ek).astype(BF16)
            qg, kg = (q * eb).astype(BF16), (k * eg).astype(BF16)
            st = s_ref[0, vs, :]
            dsn = ds_scr[vs, :]
            st16, dsn16 = st.astype(BF16), dsn.astype(BF16)
            a = jnp.where(causal, _dot_nt(qt, kt), 0.0).astype(BF16)
            da = jnp.where(causal, _dot_nt(dout, v), 0.0).astype(BF16)
            dq_inter = _dot(dout, st16) * eb
            dk_inter = _dot(v, dsn16) * eg
            dq = _dot(da, kt) * eq + dq_inter
            dk = _dot_tn(da, qt) * ek + dk_inter
            dv = _dot_tn(a, dout) + _dot_nt(kg, dsn16)
            dlast = jnp.sum(k * dk_inter, axis=0, keepdims=True) + jnp.sum(st * dsn, axis=0, keepdims=True) * el[0:1, :]
            db = q * dq - k * dk
            dla_scr[:, ks] = jnp.dot(upper, db, precision=lax.Precision.HIGHEST, preferred_element_type=F32) + dlast
            ds_scr[vs, :] = dsn * jnp.concatenate([el, el], axis=0) + _dot_tn(dout, qg)
            dp_ref[:, ks] = (dq * (DKH ** -0.5)).astype(BF16)
            dp_ref[:, DK + h * DKH:DK + (h + 1) * DKH] = dk.astype(BF16)
            dp_ref[:, D + h * DVH:D + (h + 1) * DVH] = dv.astype(BF16)
        dz = (dla_scr[...] * (1.0 / TAU) * _sigmoid(-z)).astype(BF16)
        da_ref[...] = _dot_nt(dz, wau_ref[...]).astype(BF16)
        dwau_ref[...] += _dot_tn(a_ref[...], dz)
        dba_ref[...] += _rows8(dz.astype(F32))

    rev = lambda w, col: pl.BlockSpec((CH, w), lambda c: (nc - 1 - c, col))
    return _carried_call(
        body, carry, name="gla_bwd", grid=(nc,),
        in_specs=[rev(D, 0), rev(D, 1), rev(LANES, 0), _resident((LANES, DK)), _resident((1, DK)), rev(D, 0),
                  pl.BlockSpec((1, DV, DKH), lambda c: (nc - 1 - c, 0, 0)), pl.BlockSpec(memory_space=pl.ANY)],
        out_specs=[rev(2 * D, 0), rev(LANES, 0), _resident((LANES, DK)), _resident((8, DK))],
        out_shape=[SDS((t, NPROJ), BF16), SDS((t, LANES), BF16), SDS((LANES, DK), F32), SDS((8, DK), F32)],
        scratch_shapes=[pltpu.VMEM((DV, DKH), F32), pltpu.VMEM((CH, DK), F32)],
        input_output_aliases={7: 0}, compiler_params=_ARB1)(proj, proj, alr, wau, balpha, do, sall, dproj)


def _all_gather(xs, name):
    n = len(xs)

    def body(*refs):
        x_refs, out_refs = refs[:n], refs[n:2 * n]
        send_sems, recv_sems, local_sems = refs[2 * n:]
        x, y, c = _my_place()
        me, sibling = (x, y, c), (x, y, 1 - c)
        chips = [(1 - x, y), (x, 1 - y), (1 - x, 1 - y)]

        def slot(p, px, py, pc):
            return out_refs[p].at[4 * px + 2 * py + pc]

        def copy(p, k, block, to, src=None):
            return pltpu.make_async_remote_copy(
                src_ref=slot(p, *block) if src is None else src, dst_ref=slot(p, *block),
                send_sem=send_sems.at[7 * p + k], recv_sem=recv_sems.at[7 * p + k], device_id=to, device_id_type=MESH_T)

        mine = [pltpu.make_async_copy(x_refs[p], slot(p, *me), local_sems.at[p]) for p in range(n)]
        for cp in mine:
            cp.start()
        first = []
        for p in range(n):
            first.append(copy(p, 0, me, sibling, src=x_refs[p]))
            first += [copy(p, 1 + j, me, (*chip, c), src=x_refs[p]) for j, chip in enumerate(chips)]
        for cp in first:
            cp.start()
        passed = []
        for p in range(n):
            for j, chip in enumerate(chips):
                copy(p, 1 + j, (*chip, c), me).wait_recv()
                fwd = copy(p, 4 + j, (*chip, c), sibling)
                fwd.start()
                passed.append(fwd)
        for p in range(n):
            copy(p, 0, sibling, me).wait_recv()
            for j, chip in enumerate(chips):
                copy(p, 4 + j, (*chip, 1 - c), me).wait_recv()
        for cp in first + passed:
            cp.wait_send()
        for cp in mine:
            cp.wait()

    hbm = pl.BlockSpec(memory_space=pl.ANY)
    return pl.pallas_call(
        body, name=name, out_shape=[SDS((N_DEV, *a.shape), a.dtype) for a in xs],
        in_specs=[hbm] * n, out_specs=[hbm] * n,
        scratch_shapes=[pltpu.SemaphoreType.DMA((7 * n,)), pltpu.SemaphoreType.DMA((7 * n,)),
                        pltpu.SemaphoreType.DMA((n,))])(*xs)


def _exchange(gs):
    n = len(gs)

    def body(*refs):
        g_refs, land_refs = refs[:n], refs[n:2 * n]
        send_sems, recv_sems, local_sems = refs[2 * n:]
        x, y, c = _my_place()
        my_idx = 4 * x + 2 * y + c
        mine = [pltpu.make_async_copy(g_refs[p].at[my_idx], land_refs[p].at[my_idx], local_sems.at[p]) for p in range(n)]
        for cp in mine:
            cp.start()
        copies = []
        for k in range(1, N_DEV):
            px, py, pc = _flip(x, k & 4), _flip(y, k & 2), _flip(c, k & 1)
            p_idx = 4 * px + 2 * py + pc
            for p in range(n):
                s = 7 * p + k - 1
                cp = pltpu.make_async_remote_copy(
                    src_ref=g_refs[p].at[p_idx], dst_ref=land_refs[p].at[my_idx], send_sem=send_sems.at[s],
                    recv_sem=recv_sems.at[s], device_id=(px, py, pc), device_id_type=MESH_T)
                cp.start()
                arrival = pltpu.make_async_remote_copy(
                    src_ref=g_refs[p].at[p_idx], dst_ref=land_refs[p].at[p_idx], send_sem=send_sems.at[s],
                    recv_sem=recv_sems.at[s], device_id=(px, py, pc), device_id_type=MESH_T)
                copies.append((cp, arrival))
        for cp, arrival in copies:
            arrival.wait_recv()
        for cp, arrival in copies:
            cp.wait_send()
        for cp in mine:
            cp.wait()

    hbm = pl.BlockSpec(memory_space=pl.ANY)
    return pl.pallas_call(
        body, name="grad_exchange", out_shape=[SDS(g.shape, g.dtype) for g in gs],
        in_specs=[hbm] * n, out_specs=[hbm] * n,
        scratch_shapes=[pltpu.SemaphoreType.DMA((7 * n,)), pltpu.SemaphoreType.DMA((7 * n,)),
                        pltpu.SemaphoreType.DMA((n,))])(*gs)


def _adamw(land, w, m, v, rows_blk, name):
    rows = w.shape[0]

    def body(l_ref, w_ref, m_ref, v_ref, g_ref, d_ref, nm_ref, nv_ref):
        g = l_ref[0].astype(F32)
        for s in range(1, N_DEV):
            g = g + l_ref[s].astype(F32)
        nm = ADAM_B1 * m_ref[...] + (1.0 - ADAM_B1) * g
        nv = ADAM_B2 * v_ref[...] + (1.0 - ADAM_B2) * (g * g)
        m_hat = nm / (1.0 - ADAM_B1 ** ADAM_STEP)
        v_hat = nv / (1.0 - ADAM_B2 ** ADAM_STEP)
        g_ref[...] = g
        d_ref[...] = -ADAM_LR * (m_hat / (jnp.sqrt(v_hat) + ADAM_EPS) + ADAM_WD * w_ref[...])
        nm_ref[...] = nm
        nv_ref[...] = nv

    blk = pl.BlockSpec((rows_blk, D), lambda i: (i, 0))
    return pl.pallas_call(
        body, name=name, grid=(rows // rows_blk,),
        in_specs=[pl.BlockSpec((N_DEV, rows_blk, D), lambda i: (0, i, 0)), blk, blk, blk],
        out_specs=[blk] * 4, out_shape=[SDS((rows, D), F32)] * 4, compiler_params=_ARB1)(land, w, m, v)


BIG = ("w_in", "w_up", "w_down", "w_gla_o", "w_conf_o", "w_out")
BIG_TRANSPOSED = ("w_in", "w_up")
SMALL_SHARDED = ("meta_tokens", "conf_dw_w", "ffn_dw_w", "w_alpha_up")
REPLICATED = ("norm_mix_g", "b_alpha", "gla_norm_g", "conf_dw_b", "conf_ln_g", "conf_ln_b", "norm_ffn_g", "ffn_dw_b",
              "final_norm_g")
N_IN = sum(IN_WIDTHS)
W_IN_ROWS = N_IN // N_DEV
W_IN_PAD = -(-W_IN_ROWS // 16) * 16
ADAM_BLOCK = {"w_in": W_IN_PAD // 3, "w_up": 176, "w_down": 176, "w_gla_o": 128, "w_conf_o": 128, "w_out": 128}
SMALL_ROWS = 32


def _to_panel(name, shard):
    a = shard.reshape(shard.shape[-2], shard.shape[-1])
    if name in BIG_TRANSPOSED:
        a = a.T
    if name == "w_in":
        a = jnp.pad(a, ((0, W_IN_PAD - W_IN_ROWS), (0, 0)))
    return a


def _from_panel(name, panel, shape):
    a = panel[0:W_IN_ROWS] if name == "w_in" else panel
    if name in BIG_TRANSPOSED:
        a = a.T
    return a.reshape(shape)


def _pack_small(arrs):
    flat = jnp.concatenate([jnp.pad(a.reshape(-1), (0, (-a.size) % D)) for a in arrs])
    return jnp.pad(flat, (0, SMALL_ROWS * D - flat.shape[0])).reshape(SMALL_ROWS, D)


def _unpack_small(panel, shapes):
    flat, out, off = panel.reshape(-1), [], 0
    for shp in shapes:
        n = 1
        for s in shp:
            n *= s
        out.append(flat[off:off + n].reshape(shp))
        off += n + (-n) % D
    return out


def _local_step(x, target, w, shards=None):
    dist = shards is not None
    w = dict(w)

    def gather(names):
        return [(shards[n], False) for n in names] if dist else []

    def scatter(*arrs):
        return [(a.reshape(N_DEV, -1, D), True) for a in arrs] if dist else []

    s = x.shape[0]
    n_real = s + N_META
    t = -(-n_real // TM) * TM
    h0 = jnp.concatenate([w["meta_tokens"], x, jnp.zeros((t - n_real, D), F32)], axis=0)
    tgt = jnp.concatenate([jnp.zeros((N_META, D), F32), target, jnp.zeros((t - n_real, D), F32)], axis=0)

    q0, r0, a0, c0 = 0, 2 * DK + DV, 2 * DK + 2 * DV, 2 * DK + 2 * DV + RANK
    wt = w["w_in_t"]
    w_main = jnp.concatenate([wt[q0:r0], wt[c0:N_IN], wt[r0:a0]], axis=0)
    w_a = jnp.pad(wt[a0:c0], ((0, LANES - RANK), (0, 0)))
    wau = jnp.pad(w["w_alpha_up"].astype(BF16), ((0, LANES - RANK), (0, 0)))
    row = lambda name: w[name].reshape(1, -1)
    cw = jnp.broadcast_to(jnp.pad(w["conf_dw_w"], ((0, 32 - CONF_K), (0, 0)))[:, None, :], (32, 8, D))
    fw = jnp.pad(w["ffn_dw_w"], ((0, 8 - FFN_K), (0, 0)))

    early = ("w_gla_o", "w_conf_o", "w_out", "w_up")
    u1, proj, alr, *landed = _norm_matmul(h0, row("norm_mix_g"), w_main, 1024, "in_proj", w_extra_t=w_a, carry=gather(early))
    for n, land in zip(early, landed):
        w["w_up_t" if n == "w_up" else n] = land.reshape(-1, D)
    o, og, sall = _gla_fwd(proj, alr, wau, row("b_alpha"), row("gla_norm_g"))
    cc, cs, *landed = _conf_fwd(proj, cw, row("conf_dw_b"), row("conf_ln_g"), row("conf_ln_b"), carry=gather(("w_down",)))
    if dist:
        w["w_down"] = landed[0].reshape(-1, D)
    brg, brc, merged, h1 = _mix_fwd(og, cs, proj, h0, w["w_gla_o"], w["w_conf_o"], w["w_out"])
    u2, up = _norm_matmul(h1, row("norm_ffn_g"), w["w_up_t"], 512, "up_proj")
    f, dh2, red = _ffn_out(up, fw, row("ffn_dw_b"), h1, w["w_down"], row("final_norm_g"), tgt, n_real)
    loss = 0.5 / D * jnp.sum(red[0:8])

    g = {"final_norm_g": jnp.sum(red[8:16], axis=0)}
    dup, dfw = _ffn_bwd(dh2, w["w_down"], up, fw, row("ffn_dw_b"))
    g["ffn_dw_w"] = jnp.sum(dfw[0:FFN_K], axis=1)
    g["ffn_dw_b"] = jnp.sum(dfw[3], axis=0)
    g["w_down"] = _wgrad(f, dh2, 1408, "wgrad_down")
    dh1, dg2, *landed = _dgrad_norm(dup, w["w_up_t"], 512, h1, row("norm_ffn_g"), dh2, "up_dgrad", carry=scatter(g["w_down"]))
    if dist:
        g["w_down"] = landed[0]
    g["norm_ffn_g"] = jnp.sum(dg2, axis=0)
    g["w_up_t"] = _wgrad(dup, u2, 1408, "wgrad_up")
    dbrg, dbrc, dog, dcs, dproj = _mix_bwd(dh1, w["w_out"], w["w_gla_o"], w["w_conf_o"], proj, brg, brc)
    g["w_out"] = _wgrad(merged, dh1, 1024, "wgrad_out")
    g["w_gla_o"] = _wgrad(og, dbrg, 1024, "wgrad_gla_o")
    g["w_conf_o"] = _wgrad(cs, dbrc, 1024, "wgrad_conf_o")
    do, dproj, dgn = _glapost_bwd(dog, o, proj, row("gla_norm_g"), dproj)
    g["gla_norm_g"] = jnp.sum(dgn, axis=0)
    dproj, dcw, dst, *landed = _conf_bwd(dcs, cc, proj, cw, row("conf_ln_g"), row("conf_ln_b"), dproj,
                                         carry=scatter(g["w_up_t"]))
    if dist:
        g["w_up_t"] = landed[0]
    g["conf_dw_w"] = jnp.sum(dcw[0:CONF_K], axis=1)
    g["conf_ln_g"], g["conf_ln_b"], g["conf_dw_b"] = jnp.sum(dst[0], axis=0), jnp.sum(dst[1], axis=0), jnp.sum(dst[2], axis=0)
    dproj, dalr, dwau, dba, *landed = _gla_bwd(proj, alr, wau, row("b_alpha"), do, sall, dproj,
                                               carry=scatter(g["w_out"], g["w_gla_o"], g["w_conf_o"]))
    if dist:
        g["w_out"], g["w_gla_o"], g["w_conf_o"] = landed
    g["w_alpha_up"] = dwau[0:RANK]
    g["b_alpha"] = jnp.sum(dba, axis=0)
    dw_main = _wgrad(dproj, u1, 1024, "wgrad_in")
    dw_a = _wgrad(dalr, u1, LANES, "wgrad_alr")
    g["w_in_t"] = jnp.concatenate([dw_main[0:r0], dw_main[NPROJ - DV:NPROJ], dw_a[0:RANK], dw_main[r0:NPROJ - DV]], axis=0)
    w_in_blocks = []
    if dist:
        pad = ((0, 0), (0, W_IN_PAD - W_IN_ROWS), (0, 0))
        w_in_blocks = [(jnp.pad(g["w_in_t"].reshape(N_DEV, W_IN_ROWS, D), pad), True)]
    dh0, dg1, *landed = _dgrad_norm(dproj, w_main, 1024, h0, row("norm_mix_g"), dh1, "in_dgrad", dy_extra=dalr,
                                    w_extra_t=w_a, carry=w_in_blocks)
    if dist:
        g["w_in_t"] = landed[0]
    g["norm_mix_g"] = jnp.sum(dg1, axis=0)
    g["meta_tokens"] = dh0[0:N_META]
    return loss, dh0[N_META:n_real], g


def kernel(x, meta_tokens, norm_mix_g, w_in, w_alpha_up, b_alpha, gla_norm_g, w_gla_o, conf_dw_w, conf_dw_b, conf_ln_g, conf_ln_b, w_conf_o, w_out, norm_ffn_g, w_up, ffn_dw_w, ffn_dw_b, w_down, final_norm_g, loss_target, m_meta_tokens, m_norm_mix_g, m_w_in, m_w_alpha_up, m_b_alpha, m_gla_norm_g, m_w_gla_o, m_conf_dw_w, m_conf_dw_b, m_conf_ln_g, m_conf_ln_b, m_w_conf_o, m_w_out, m_norm_ffn_g, m_w_up, m_ffn_dw_w, m_ffn_dw_b, m_w_down, m_final_norm_g, v_meta_tokens, v_norm_mix_g, v_w_in, v_w_alpha_up, v_b_alpha, v_gla_norm_g, v_w_gla_o, v_conf_dw_w, v_conf_dw_b, v_conf_ln_g, v_conf_ln_b, v_w_conf_o, v_w_out, v_norm_ffn_g, v_w_up, v_ffn_dw_w, v_ffn_dw_b, v_w_down, v_final_norm_g):
    ws = dict(meta_tokens=meta_tokens, norm_mix_g=norm_mix_g, w_in=w_in, w_alpha_up=w_alpha_up, b_alpha=b_alpha,
              gla_norm_g=gla_norm_g, w_gla_o=w_gla_o, conf_dw_w=conf_dw_w, conf_dw_b=conf_dw_b, conf_ln_g=conf_ln_g,
              conf_ln_b=conf_ln_b, w_conf_o=w_conf_o, w_out=w_out, norm_ffn_g=norm_ffn_g, w_up=w_up, ffn_dw_w=ffn_dw_w,
              ffn_dw_b=ffn_dw_b, w_down=w_down, final_norm_g=final_norm_g)
    ms = dict(meta_tokens=m_meta_tokens, norm_mix_g=m_norm_mix_g, w_in=m_w_in, w_alpha_up=m_w_alpha_up, b_alpha=m_b_alpha,
              gla_norm_g=m_gla_norm_g, w_gla_o=m_w_gla_o, conf_dw_w=m_conf_dw_w, conf_dw_b=m_conf_dw_b,
              conf_ln_g=m_conf_ln_g, conf_ln_b=m_conf_ln_b, w_conf_o=m_w_conf_o, w_out=m_w_out, norm_ffn_g=m_norm_ffn_g,
              w_up=m_w_up, ffn_dw_w=m_ffn_dw_w, ffn_dw_b=m_ffn_dw_b, w_down=m_w_down, final_norm_g=m_final_norm_g)
    vs = dict(meta_tokens=v_meta_tokens, norm_mix_g=v_norm_mix_g, w_in=v_w_in, w_alpha_up=v_w_alpha_up, b_alpha=v_b_alpha,
              gla_norm_g=v_gla_norm_g, w_gla_o=v_w_gla_o, conf_dw_w=v_conf_dw_w, conf_dw_b=v_conf_dw_b,
              conf_ln_g=v_conf_ln_g, conf_ln_b=v_conf_ln_b, w_conf_o=v_w_conf_o, w_out=v_w_out, norm_ffn_g=v_norm_ffn_g,
              w_up=v_w_up, ffn_dw_w=v_ffn_dw_w, ffn_dw_b=v_ffn_dw_b, w_down=v_w_down, final_norm_g=v_final_norm_g)
    small = SMALL_SHARDED + REPLICATED
    pack_small = lambda d: _pack_small([d[n] for n in small])

    shards = {n: _to_panel(n, ws[n]).astype(BF16) for n in BIG}
    gathered = _all_gather([shards["w_in"], pack_small(ws)], "weight_gather")
    full = {n: ws[n].reshape(-1) for n in REPLICATED}
    full["w_in_t"] = gathered[0][:, 0:W_IN_ROWS].reshape(N_IN, D)
    flat, off = gathered[1].reshape(N_DEV, -1), 0
    for n in SMALL_SHARDED:
        k, c = ws[n].shape[-2], ws[n].shape[-1]
        full[n] = flat[:, off:off + k * c].reshape(N_DEV, k, c).transpose(1, 0, 2).reshape(k, N_DEV * c)
        off += k * c + (-(k * c)) % D

    loss, grad_x, g = _local_step(x[0], loss_target[0], full, shards)

    lands = [g["w_in_t"], g["w_up_t"]] + [g[n] for n in BIG[2:]]
    blocks = []
    for n in SMALL_SHARDED:
        k, c = ws[n].shape[-2], ws[n].shape[-1]
        b = g[n].reshape(k, N_DEV, c).transpose(1, 0, 2).reshape(N_DEV, k * c)
        blocks.append(jnp.pad(b, ((0, 0), (0, (-(k * c)) % D))))
    for n in REPLICATED:
        b = jnp.broadcast_to(g[n].reshape(1, -1), (N_DEV, g[n].size))
        blocks.append(jnp.pad(b, ((0, 0), (0, (-b.shape[1]) % D))))
    gsm = jnp.concatenate(blocks, axis=1)
    lands += _exchange([jnp.pad(gsm, ((0, 0), (0, SMALL_ROWS * D - gsm.shape[1]))).reshape(N_DEV, SMALL_ROWS, D)])

    grad, delta, new_m, new_v = {}, {}, {}, {}
    for i, n in enumerate(BIG):
        outs = _adamw(lands[i], _to_panel(n, ws[n]), _to_panel(n, ms[n]), _to_panel(n, vs[n]), ADAM_BLOCK[n], "adamw_" + n)
        grad[n], delta[n], new_m[n], new_v[n] = [_from_panel(n, p, ws[n].shape) for p in outs]
    outs = _adamw(lands[len(BIG)], pack_small(ws), pack_small(ms), pack_small(vs), SMALL_ROWS, "adamw_small")
    shapes = [ws[n].shape for n in small]
    for d, p in zip((grad, delta, new_m, new_v), outs):
        d.update(zip(small, _unpack_small(p, shapes)))

    order = ("meta_tokens", "norm_mix_g", "w_in", "w_alpha_up", "b_alpha", "gla_norm_g", "w_gla_o", "conf_dw_w", "conf_dw_b",
             "conf_ln_g", "conf_ln_b", "w_conf_o", "w_out", "norm_ffn_g", "w_up", "ffn_dw_w", "ffn_dw_b", "w_down",
             "final_norm_g")
    loss = lax.psum(loss, ("x", "y", "c"))
    return (loss, grad_x[None], *[grad[n] for n in order], *[delta[n] for n in order], *[new_m[n] for n in order],
            *[new_v[n] for n in order])
```

```python
import functools

import jax
import jax.numpy as jnp
from jax import lax
from jax.experimental import pallas as pl
from jax.experimental.pallas import tpu as pltpu

F32, BF16 = jnp.float32, jnp.bfloat16
SDS = jax.ShapeDtypeStruct

D = 1024
N_META = 16
HEADS = 4
DK, DKH, DV, DVH = 512, 128, 1024, 256
RANK = 16
TAU = 16.0
CONF_K = 31
DFF = 2816
FFN_K = 3
IN_WIDTHS = (DK, DK, DV, DV, RANK, 2 * D, D, D)
RMS_EPS, LN_EPS = 1e-6, 1e-5
ADAM_LR, ADAM_B1, ADAM_B2, ADAM_EPS, ADAM_WD, ADAM_STEP = 0.001, 0.9, 0.999, 1e-08, 0.01, 10

NPROJ = 7 * D
LANES = 128
CH = 128
TM = 640
TM_BIG = 1664
TE = 320
HALO = 32
HALO_F = 16
N_DEV = 8
VMEM_LIMIT = 60 * 1024 * 1024
MESH_T = pl.DeviceIdType.MESH

_ARB1 = pltpu.CompilerParams(dimension_semantics=("arbitrary",), vmem_limit_bytes=VMEM_LIMIT)
_ARB2 = pltpu.CompilerParams(dimension_semantics=("arbitrary", "arbitrary"), vmem_limit_bytes=VMEM_LIMIT)


def _dot(a, b):
    return jnp.dot(a, b, preferred_element_type=F32)


def _dot_nt(a, b):
    return lax.dot_general(a, b, (((1,), (1,)), ((), ())), preferred_element_type=F32)


def _dot_tn(a, b):
    return lax.dot_general(a, b, (((0,), (0,)), ((), ())), preferred_element_type=F32)


def _sigmoid(x):
    return 0.5 * jnp.tanh(0.5 * x) + 0.5


def _rows8(x):
    return x.reshape(x.shape[0] // 8, 8, x.shape[1]).sum(axis=0)


def _row_tile(t, preferred):
    return preferred if t % preferred == 0 else TM


def _row_loop(n_rows, rb, fn, unroll=1):
    def step(i, carry):
        fn(pl.multiple_of(i * rb, rb))
        return carry
    lax.fori_loop(0, n_rows // rb, step, 0, unroll=unroll)


def _resident(shape):
    return pl.BlockSpec(shape, lambda *_: (0,) * len(shape))


CONV_ROWS, CONV_LANES = 64, 256


def _shift_classes(offset_taps):
    return [(s, [(o - s, j) for o, j in offset_taps if o % 8 == s]) for s in range(8)]


def _shifted(win, s):
    return win if s == 0 else win[s:s + CONV_ROWS + HALO - 8, :]


def _my_place():
    return lax.axis_index("x"), lax.axis_index("y"), lax.axis_index("c")


def _flip(v, bit):
    return 1 - v if bit else v


def _exchange_copies(src_refs, land_refs, scatter, send_sems, recv_sems, local_sems, arrivals):
    x, y, c = _my_place()
    my_idx = 4 * x + 2 * y + c
    local, remote = [], []
    for p, (src, land) in enumerate(zip(src_refs, land_refs)):
        local.append(pltpu.make_async_copy(src.at[my_idx] if scatter[p] else src, land.at[my_idx], local_sems.at[p]))
    for k in range(1, N_DEV):
        px, py, pc = _flip(x, k & 4), _flip(y, k & 2), _flip(c, k & 1)
        p_idx = 4 * px + 2 * py + pc
        for p, (src, land) in enumerate(zip(src_refs, land_refs)):
            s = 7 * p + k - 1
            out = src.at[p_idx] if scatter[p] else src

            def copy(dst):
                return pltpu.make_async_remote_copy(src_ref=out, dst_ref=dst, send_sem=send_sems.at[s],
                                                    recv_sem=recv_sems.at[s], device_id=(px, py, pc), device_id_type=MESH_T)
            remote.append((copy(land.at[my_idx]), copy(land.at[p_idx]) if arrivals else None))
    return local, remote


def _carried_call(core, carry, *, grid, in_specs, out_specs, out_shape, scratch_shapes=(), **kw):
    n_in, n_out, nc, n_scr = len(in_specs), len(out_specs), len(carry), len(scratch_shapes)
    if nc == 0:
        return pl.pallas_call(core, grid=grid, in_specs=in_specs, out_specs=out_specs, out_shape=out_shape,
                              scratch_shapes=list(scratch_shapes), **kw)
    scatter = [sc for _, sc in carry]

    def body(*refs):
        ins, cin = refs[:n_in], refs[n_in:n_in + nc]
        outs, cout = refs[n_in + nc:n_in + nc + n_out], refs[n_in + nc + n_out:n_in + 2 * nc + n_out]
        scr, sems = refs[n_in + 2 * nc + n_out:n_in + 2 * nc + n_out + n_scr], refs[-3:]
        first = functools.reduce(jnp.logical_and, [pl.program_id(a) == 0 for a in range(len(grid))])
        last = functools.reduce(jnp.logical_and, [pl.program_id(a) == grid[a] - 1 for a in range(len(grid))])

        @pl.when(first)
        def _():
            local, remote = _exchange_copies(cin, cout, scatter, *sems, arrivals=False)
            for cp in local:
                cp.start()
            for send, _ in remote:
                send.start()

        core(*ins, *outs, *scr)

        @pl.when(last)
        def _():
            local, remote = _exchange_copies(cin, cout, scatter, *sems, arrivals=True)
            for _, arrival in remote:
                arrival.wait_recv()
            for send, _ in remote:
                send.wait_send()
            for cp in local:
                cp.wait()

    hbm = pl.BlockSpec(memory_space=pl.ANY)
    land_shape = [SDS((N_DEV, *(a.shape[1:] if sc else a.shape)), a.dtype) for a, sc in carry]
    sems = [pltpu.SemaphoreType.DMA((7 * nc,)), pltpu.SemaphoreType.DMA((7 * nc,)), pltpu.SemaphoreType.DMA((nc,))]
    call = pl.pallas_call(body, grid=grid, in_specs=list(in_specs) + [hbm] * nc, out_specs=list(out_specs) + [hbm] * nc,
                          out_shape=list(out_shape) + land_shape, scratch_shapes=list(scratch_shapes) + sems, **kw)
    return lambda *args: call(*args, *[a for a, _ in carry])


def _norm_matmul(h, g, w_t, tn, name, w_extra_t=None, carry=()):
    t, n = h.shape[0], w_t.shape[0]
    tm = _row_tile(t, TM_BIG)
    nt, nb = t // tm, n // tn

    def body(*refs):
        if w_extra_t is None:
            h_ref, g_ref, w_ref, u_ref, p_ref = refs
        else:
            h_ref, g_ref, w_ref, we_ref, u_ref, p_ref, e_ref = refs

        @pl.when(pl.program_id(1) == 0)
        def _():
            def blk(r0):
                x = h_ref[pl.ds(r0, 32), :]
                rinv = lax.rsqrt(jnp.mean(x * x, axis=-1, keepdims=True) + RMS_EPS)
                u_ref[pl.ds(r0, 32), :] = (x * rinv * g_ref[...]).astype(BF16)
            _row_loop(tm, 32, blk, unroll=2)
            if w_extra_t is not None:
                e_ref[...] = _dot_nt(u_ref[...], we_ref[...]).astype(BF16)

        p_ref[...] = _dot_nt(u_ref[...], w_ref[...]).astype(BF16)

    in_specs = [pl.BlockSpec((tm, D), lambda i, j: (i, 0)), _resident((1, D)), pl.BlockSpec((tn, D), lambda i, j: (j, 0))]
    out_specs = [pl.BlockSpec((tm, D), lambda i, j: (i, 0)), pl.BlockSpec((tm, tn), lambda i, j: (i, j))]
    out_shape = [SDS((t, D), BF16), SDS((t, n), BF16)]
    args = [h, g, w_t]
    if w_extra_t is not None:
        in_specs.append(_resident(w_extra_t.shape))
        out_specs.append(pl.BlockSpec((tm, w_extra_t.shape[0]), lambda i, j: (i, 0)))
        out_shape.append(SDS((t, w_extra_t.shape[0]), BF16))
        args.append(w_extra_t)
    return _carried_call(body, carry, name=name, grid=(nt, nb), in_specs=in_specs, out_specs=out_specs,
                         out_shape=out_shape, compiler_params=_ARB2)(*args)


def _gla_decay(a_ref, wau_ref, ba_ref):
    z = _dot(a_ref[...], wau_ref[...]) + ba_ref[...]
    la = (jnp.minimum(z, 0.0) - jnp.log(1.0 + jnp.exp(-jnp.abs(z)))) * (1.0 / TAU)
    r = lax.broadcasted_iota(jnp.int32, (CH, CH), 0)
    c = lax.broadcasted_iota(jnp.int32, (CH, CH), 1)
    b = _tri_matmul((r >= c).astype(BF16), la)
    mid = jnp.broadcast_to(b[CH // 2:CH // 2 + 1, :], b.shape)
    last = jnp.broadcast_to(b[CH - 1:CH, :], b.shape)
    return z, b, mid, last, r >= c


def _tri_matmul(tri, x):
    n = x.shape[1]
    x1 = x.astype(BF16)
    r1 = x - x1.astype(F32)
    x2 = r1.astype(BF16)
    x3 = (r1 - x2.astype(F32)).astype(BF16)
    y = _dot(tri, jnp.concatenate([x1, x2, x3], axis=1))
    return y[:, 0:n] + y[:, n:2 * n] + y[:, 2 * n:3 * n]


def _gla_fwd(proj, alr, wau, balpha, gn):
    t = proj.shape[0]
    nc = t // CH

    def body(qk_ref, v_ref, r_ref, a_ref, wau_ref, ba_ref, gn_ref, o_ref, og_ref, sall_ref, s_scr):
        @pl.when(pl.program_id(0) == 0)
        def _():
            s_scr[...] = jnp.zeros_like(s_scr)

        sall_ref[0] = s_scr[...]
        _, b, bmid, blast, causal = _gla_decay(a_ref, wau_ref, ba_ref)
        for h in range(HEADS):
            ks = slice(h * DKH, (h + 1) * DKH)
            vs = slice(h * DVH, (h + 1) * DVH)
            bh, mh, lh = b[:, ks], bmid[:, ks], blast[:, ks]
            q = qk_ref[:, ks].astype(F32) * (DKH ** -0.5)
            k = qk_ref[:, DK + h * DKH:DK + (h + 1) * DKH].astype(F32)
            v = v_ref[:, vs]
            qt = (q * jnp.exp(bh - mh)).astype(BF16)
            kt = (k * jnp.exp(mh - bh)).astype(BF16)
            qg = (q * jnp.exp(bh)).astype(BF16)
            kg = (k * jnp.exp(lh - bh)).astype(BF16)
            a = jnp.where(causal, _dot_nt(qt, kt), 0.0)
            st = s_scr[vs, :]
            o = _dot(a.astype(BF16), v) + _dot_nt(qg, st.astype(BF16))
            el = jnp.exp(lh)
            s_scr[vs, :] = st * jnp.concatenate([el, el], axis=0) + _dot_tn(v, kg)
            o_ref[:, vs] = o
            on = o * lax.rsqrt(jnp.mean(o * o, axis=-1, keepdims=True) + RMS_EPS) * gn_ref[:, vs]
            rr = r_ref[:, vs].astype(F32)
            og_ref[:, vs] = (on * (rr * _sigmoid(rr))).astype(BF16)

    return pl.pallas_call(
        body, name="gla_fwd", grid=(nc,),
        in_specs=[pl.BlockSpec((CH, D), lambda c: (c, 0)), pl.BlockSpec((CH, D), lambda c: (c, 1)),
                  pl.BlockSpec((CH, D), lambda c: (c, 6)), pl.BlockSpec((CH, LANES), lambda c: (c, 0)),
                  _resident((LANES, DK)), _resident((1, DK)), _resident((1, DV))],
        out_specs=[pl.BlockSpec((CH, DV), lambda c: (c, 0)), pl.BlockSpec((CH, DV), lambda c: (c, 0)),
                   pl.BlockSpec((1, DV, DKH), lambda c: (c, 0, 0))],
        out_shape=[SDS((t, DV), F32), SDS((t, DV), BF16), SDS((nc, DV, DKH), F32)],
        scratch_shapes=[pltpu.VMEM((DV, DKH), F32)], compiler_params=_ARB1)(proj, proj, proj, alr, wau, balpha, gn)


def _conf_fwd(proj, cw, cb, lg, lb, carry=()):
    t = proj.shape[0]
    nt = t // TE

    def body(c1_ref, c2_ref, cw_ref, cb_ref, lg_ref, lb_ref, cc_ref, cs_ref, cext):
        i = pl.program_id(0)

        @pl.when(i == 0)
        def _():
            cext[0:HALO, :] = jnp.zeros((HALO, D), F32)

        @pl.when(i > 0)
        def _():
            cext[0:HALO, :] = cext[TE:TE + HALO, :]

        def glu(r0):
            c2 = c2_ref[pl.ds(r0, 32), :].astype(F32)
            cext[pl.ds(HALO + r0, 32), :] = c1_ref[pl.ds(r0, 32), :].astype(F32) * _sigmoid(c2)
        _row_loop(TE, 32, glu)

        def conv(r0):
            for part in range(D // CONV_LANES):
                cols = slice(part * CONV_LANES, (part + 1) * CONV_LANES)
                win = cext[pl.ds(r0, CONV_ROWS + HALO), cols]
                acc = jnp.zeros((CONV_ROWS, CONV_LANES), F32) + cb_ref[:, cols]
                for s, taps in _shift_classes([(2 + j, j) for j in range(CONF_K)]):
                    ws = _shifted(win, s)
                    for a8, j in taps:
                        acc = acc + jnp.tile(cw_ref[j, :, cols], (CONV_ROWS // 8, 1)) * ws[a8:a8 + CONV_ROWS, :]
                cc_ref[pl.ds(r0, CONV_ROWS), cols] = acc
            for sub in range(CONV_ROWS // 32):
                rows = pl.ds(r0 + 32 * sub, 32)
                x = cc_ref[rows, :]
                xc = x - jnp.mean(x, axis=-1, keepdims=True)
                var = jnp.mean(xc * xc, axis=-1, keepdims=True)
                ln = xc * lax.rsqrt(var + LN_EPS) * lg_ref[...] + lb_ref[...]
                cs_ref[rows, :] = (ln * _sigmoid(ln)).astype(BF16)
        _row_loop(TE, CONV_ROWS, conv)

    return _carried_call(
        body, carry, name="conf_fwd", grid=(nt,),
        in_specs=[pl.BlockSpec((TE, D), lambda i: (i, 2)), pl.BlockSpec((TE, D), lambda i: (i, 3)),
                  _resident((32, 8, D)), _resident((1, D)), _resident((1, D)), _resident((1, D))],
        out_specs=[pl.BlockSpec((TE, D), lambda i: (i, 0)), pl.BlockSpec((TE, D), lambda i: (i, 0))],
        out_shape=[SDS((t, D), F32), SDS((t, D), BF16)],
        scratch_shapes=[pltpu.VMEM((TE + HALO, D), F32)], compiler_params=_ARB1)(proj, proj, cw, cb, lg, lb)


def _mix_fwd(og, cs, proj, h0, wg, wc, wo):
    t = h0.shape[0]
    nt = t // TE

    def body(og_ref, cs_ref, g_ref, h0_ref, wg_ref, wc_ref, wo_ref, brg_ref, brc_ref, mg_ref, h1_ref, acc):
        acc[...] = _dot(og_ref[...], wg_ref[...])
        brg_ref[...] = acc[...].astype(BF16)
        acc[...] = _dot(cs_ref[...], wc_ref[...])
        brc_ref[...] = acc[...].astype(BF16)

        def blk(r0):
            rows = pl.ds(r0, 32)
            gg = g_ref[rows, 0:D].astype(F32)
            gc = g_ref[rows, D:2 * D].astype(F32)
            m = _sigmoid(gg) * brg_ref[rows, :].astype(F32) + _sigmoid(gc) * brc_ref[rows, :].astype(F32)
            mg_ref[rows, :] = m.astype(BF16)
        _row_loop(TE, 32, blk)
        h1_ref[...] = h0_ref[...] + _dot(mg_ref[...], wo_ref[...])

    row = lambda w: pl.BlockSpec((TE, w), lambda i: (i, 0))
    return pl.pallas_call(
        body, name="mix_fwd", grid=(nt,),
        in_specs=[row(D), row(D), pl.BlockSpec((TE, 2 * D), lambda i: (i, 2)), row(D),
                  _resident((D, D)), _resident((D, D)), _resident((D, D))],
        out_specs=[row(D), row(D), row(D), row(D)],
        out_shape=[SDS((t, D), BF16), SDS((t, D), BF16), SDS((t, D), BF16), SDS((t, D), F32)],
        scratch_shapes=[pltpu.VMEM((TE, D), F32)], compiler_params=_ARB1)(og, cs, proj, h0, wg, wc, wo)


def _ffn_out(up, fw, fb, h1, wd, gf, tgt, n_real):
    t = h1.shape[0]
    nt = t // TE

    def body(a_ref, bv_ref, fw_ref, fb_ref, h1_ref, wd_ref, gf_ref, tg_ref, f_ref, dh2_ref, red_ref, aext, hs):
        i = pl.program_id(0)

        @pl.when(i == 0)
        def _():
            aext[0:HALO_F, :] = jnp.zeros((HALO_F, DFF), F32)
            red_ref[...] = jnp.zeros_like(red_ref)

        @pl.when(i > 0)
        def _():
            aext[0:HALO_F, :] = aext[TE:TE + HALO_F, :]

        def cp(r0):
            aext[pl.ds(HALO_F + r0, 16), :] = a_ref[pl.ds(r0, 16), :].astype(F32)
        _row_loop(TE, 16, cp)

        def conv(r0):
            win = aext[pl.ds(r0, 32), :]
            ac = fb_ref[...] + fw_ref[0:1, :] * win[14:30, :] + fw_ref[1:2, :] * win[15:31, :] + fw_ref[2:3, :] * win[16:32, :]
            f_ref[pl.ds(r0, 16), :] = (ac * _sigmoid(ac) * bv_ref[pl.ds(r0, 16), :].astype(F32)).astype(BF16)
        _row_loop(TE, 16, conv)
        hs[...] = h1_ref[...] + _dot(f_ref[...], wd_ref[...])

        def head(r0):
            rows = pl.ds(r0, 32)
            h2 = hs[rows, :]
            rinv = lax.rsqrt(jnp.mean(h2 * h2, axis=-1, keepdims=True) + RMS_EPS)
            hh = h2 * rinv
            gid = i * TE + r0 + lax.broadcasted_iota(jnp.int32, (32, 1), 0)
            live = jnp.logical_and(gid >= N_META, gid < n_real)
            err = jnp.where(live, hh * gf_ref[...] - tg_ref[rows, :], 0.0)
            dy = err * (1.0 / D)
            red_ref[0:8, :] += _rows8(err * err)
            red_ref[8:16, :] += _rows8(dy * hh)
            dhh = dy * gf_ref[...]
            dh2_ref[rows, :] = rinv * (dhh - hh * jnp.mean(dhh * hh, axis=-1, keepdims=True))
        _row_loop(TE, 32, head, unroll=2)

    row = lambda w: pl.BlockSpec((TE, w), lambda i: (i, 0))
    return pl.pallas_call(
        body, name="ffn_out", grid=(nt,),
        in_specs=[pl.BlockSpec((TE, DFF), lambda i: (i, 0)), pl.BlockSpec((TE, DFF), lambda i: (i, 1)),
                  _resident((8, DFF)), _resident((1, DFF)), row(D), _resident((DFF, D)), _resident((1, D)), row(D)],
        out_specs=[row(DFF), row(D), _resident((16, D))],
        out_shape=[SDS((t, DFF), BF16), SDS((t, D), F32), SDS((16, D), F32)],
        scratch_shapes=[pltpu.VMEM((TE + HALO_F, DFF), F32), pltpu.VMEM((TE, D), F32)],
        compiler_params=_ARB1)(up, up, fw, fb, h1, wd, gf, tgt)


def _ffn_bwd(dh2, wd, up, fw, fb):
    t = dh2.shape[0]
    nt = t // TE
    hb = TE // HALO_F

    def body(dh_ref, wd_ref, a_ref, ah_ref, bv_ref, fw_ref, fb_ref, dup_ref, dw_ref, aext, dax, dfs):
        i = pl.program_id(0)
        ti = nt - 1 - i

        @pl.when(i == 0)
        def _():
            dax[TE:TE + HALO_F, :] = jnp.zeros((HALO_F, DFF), F32)
            dw_ref[...] = jnp.zeros_like(dw_ref)

        @pl.when(i > 0)
        def _():
            dax[TE:TE + HALO_F, :] = dax[0:HALO_F, :]

        aext[0:HALO_F, :] = jnp.where(ti > 0, ah_ref[...].astype(F32), 0.0)
        dfs[...] = _dot_nt(dh_ref[...].astype(BF16), wd_ref[...])

        def cp(r0):
            aext[pl.ds(HALO_F + r0, 16), :] = a_ref[pl.ds(r0, 16), :].astype(F32)
        _row_loop(TE, 16, cp)

        def act(r0):
            rows = pl.ds(r0, 16)
            win = aext[pl.ds(r0, 32), :]
            ac = fb_ref[...] + fw_ref[0:1, :] * win[14:30, :] + fw_ref[1:2, :] * win[15:31, :] + fw_ref[2:3, :] * win[16:32, :]
            sg = _sigmoid(ac)
            df = dfs[rows, :]
            dup_ref[rows, DFF:2 * DFF] = (df * ac * sg).astype(BF16)
            dac = df * bv_ref[rows, :].astype(F32) * sg * (1.0 + ac * (1.0 - sg))
            dax[rows, :] = dac
            dw_ref[3] += _rows8(dac)
            for j in range(FFN_K):
                dw_ref[j] += _rows8(dac * win[14 + j:30 + j, :])
        _row_loop(TE, 16, act)

        def convt(r0):
            win = dax[pl.ds(r0, 32), :]
            da = fw_ref[2:3, :] * win[0:16, :] + fw_ref[1:2, :] * win[1:17, :] + fw_ref[0:1, :] * win[2:18, :]
            dup_ref[pl.ds(r0, 16), 0:DFF] = da.astype(BF16)
        _row_loop(TE, 16, convt)

    rev = lambda w: pl.BlockSpec((TE, w), lambda i: (nt - 1 - i, 0))
    return pl.pallas_call(
        body, name="ffn_bwd", grid=(nt,),
        in_specs=[rev(D), _resident((DFF, D)), rev(DFF),
                  pl.BlockSpec((HALO_F, DFF), lambda i: (jnp.maximum((nt - 1 - i) * hb - 1, 0), 0)),
                  pl.BlockSpec((TE, DFF), lambda i: (nt - 1 - i, 1)), _resident((8, DFF)), _resident((1, DFF))],
        out_specs=[rev(2 * DFF), _resident((4, 8, DFF))],
        out_shape=[SDS((t, 2 * DFF), BF16), SDS((4, 8, DFF), F32)],
        scratch_shapes=[pltpu.VMEM((TE + HALO_F, DFF), F32), pltpu.VMEM((TE + HALO_F, DFF), F32),
                        pltpu.VMEM((TE, DFF), F32)],
        compiler_params=_ARB1)(dh2, wd, up, up, up, fw, fb)


def _dgrad_norm(dy, w_t, h, g, dres, name, dy_extra=None, w_extra_t=None, carry=()):
    t, k = dy.shape
    nt = t // TM

    def body(*refs):
        if dy_extra is None:
            dy_ref, w_ref, h_ref, g_ref, dr_ref, dh_ref, dg_ref, acc = refs
        else:
            dy_ref, w_ref, h_ref, g_ref, dr_ref, de_ref, we_ref, dh_ref, dg_ref, acc = refs

        @pl.when(pl.program_id(0) == 0)
        def _():
            dg_ref[...] = jnp.zeros_like(dg_ref)

        acc[...] = _dot(dy_ref[...], w_ref[...])
        if dy_extra is not None:
            acc[...] += _dot(de_ref[...], we_ref[...])

        def blk(r0):
            rows = pl.ds(r0, 32)
            x = h_ref[rows, :]
            rinv = lax.rsqrt(jnp.mean(x * x, axis=-1, keepdims=True) + RMS_EPS)
            hh = x * rinv
            du = acc[rows, :]
            dg_ref[...] += _rows8(du * hh)
            dhh = du * g_ref[...]
            dh_ref[rows, :] = dr_ref[rows, :] + rinv * (dhh - hh * jnp.mean(dhh * hh, axis=-1, keepdims=True))
        _row_loop(TM, 32, blk, unroll=2)

    row = pl.BlockSpec((TM, D), lambda i: (i, 0))
    once = lambda shape: pl.BlockSpec(shape, lambda i: (0, 0), pipeline_mode=pl.Buffered(1))
    in_specs = [pl.BlockSpec((TM, k), lambda i: (i, 0)), once((k, D)), row, _resident((1, D)), row]
    args = [dy, w_t, h, g, dres]
    if dy_extra is not None:
        in_specs += [pl.BlockSpec((TM, dy_extra.shape[1]), lambda i: (i, 0)), once(w_extra_t.shape)]
        args += [dy_extra, w_extra_t]
    return _carried_call(
        body, carry, name=name, grid=(nt,), in_specs=in_specs, out_specs=[row, _resident((8, D))],
        out_shape=[SDS((t, D), F32), SDS((8, D), F32)],
        scratch_shapes=[pltpu.VMEM((TM, D), F32)], compiler_params=_ARB1)(*args)


def _wgrad(x, dy, tk, name):
    t, k = x.shape
    n = dy.shape[1]
    tm = _row_tile(t, TM_BIG)
    nk, nt = k // tk, t // tm

    def body(x_ref, dy_ref, o_ref, acc):
        @pl.when(pl.program_id(1) == 0)
        def _():
            acc[...] = jnp.zeros_like(acc)
        acc[...] += _dot_tn(x_ref[...], dy_ref[...].astype(BF16))

        @pl.when(pl.program_id(1) == nt - 1)
        def _():
            o_ref[...] = acc[...].astype(BF16)

    return pl.pallas_call(
        body, name=name, grid=(nk, nt),
        in_specs=[pl.BlockSpec((tm, tk), lambda j, i: (i, j)), pl.BlockSpec((tm, n), lambda j, i: (i, 0))],
        out_specs=pl.BlockSpec((tk, n), lambda j, i: (j, 0)), out_shape=SDS((k, n), BF16),
        scratch_shapes=[pltpu.VMEM((tk, n), F32)], compiler_params=_ARB2)(x, dy)


def _mix_bwd(dh1, wo, wg, wc, proj, brg, brc):
    t = dh1.shape[0]
    nt = t // TE

    def body(dh_ref, wo_ref, wg_ref, wc_ref, g_ref, brg_ref, brc_ref, dbg_ref, dbc_ref, dog_ref, dcs_ref, dp_ref, dm):
        dm[...] = _dot_nt(dh_ref[...].astype(BF16), wo_ref[...])

        def blk(r0):
            rows = pl.ds(r0, 32)
            d = dm[rows, :]
            sg = _sigmoid(g_ref[rows, 0:D].astype(F32))
            sc = _sigmoid(g_ref[rows, D:2 * D].astype(F32))
            dbg_ref[rows, :] = (d * sg).astype(BF16)
            dbc_ref[rows, :] = (d * sc).astype(BF16)
            dp_ref[rows, 0:D] = (d * brg_ref[rows, :].astype(F32) * sg * (1.0 - sg)).astype(BF16)
            dp_ref[rows, D:2 * D] = (d * brc_ref[rows, :].astype(F32) * sc * (1.0 - sc)).astype(BF16)
        _row_loop(TE, 32, blk)
        dog_ref[...] = _dot_nt(dbg_ref[...], wg_ref[...]).astype(BF16)
        dcs_ref[...] = _dot_nt(dbc_ref[...], wc_ref[...]).astype(BF16)

    row = pl.BlockSpec((TE, D), lambda i: (i, 0))
    wide = pl.BlockSpec((TE, 2 * D), lambda i: (i, 2))
    return pl.pallas_call(
        body, name="mix_bwd", grid=(nt,),
        in_specs=[row, _resident((D, D)), _resident((D, D)), _resident((D, D)), wide, row, row],
        out_specs=[row, row, row, row, wide],
        out_shape=[SDS((t, D), BF16)] * 4 + [SDS((t, NPROJ), BF16)],
        scratch_shapes=[pltpu.VMEM((TE, D), F32)], compiler_params=_ARB1)(dh1, wo, wg, wc, proj, brg, brc)


def _glapost_bwd(dog, o, proj, gn, dproj):
    t = o.shape[0]
    nt = t // TE

    def body(dog_ref, o_ref, r_ref, gn_ref, dp_in, do_ref, dp_ref, dgn_ref):
        del dp_in

        @pl.when(pl.program_id(0) == 0)
        def _():
            dgn_ref[...] = jnp.zeros_like(dgn_ref)

        def blk(r0):
            rows = pl.ds(r0, 32)
            for h in range(HEADS):
                vs = slice(h * DVH, (h + 1) * DVH)
                x = o_ref[rows, vs]
                rinv = lax.rsqrt(jnp.mean(x * x, axis=-1, keepdims=True) + RMS_EPS)
                oh = x * rinv
                g = gn_ref[:, vs]
                rr = r_ref[rows, vs].astype(F32)
                sr = _sigmoid(rr)
                d = dog_ref[rows, vs].astype(F32)
                dp_ref[rows, vs] = (d * oh * g * sr * (1.0 + rr * (1.0 - sr))).astype(BF16)
                don = d * rr * sr
                dgn_ref[:, vs] += _rows8(don * oh)
                doh = don * g
                do_ref[rows, vs] = (rinv * (doh - oh * jnp.mean(doh * oh, axis=-1, keepdims=True))).astype(BF16)
        _row_loop(TE, 32, blk)

    row = pl.BlockSpec((TE, D), lambda i: (i, 0))
    rcol = pl.BlockSpec((TE, D), lambda i: (i, 6))
    return pl.pallas_call(
        body, name="glapost_bwd", grid=(nt,),
        in_specs=[row, row, rcol, _resident((1, D)), pl.BlockSpec(memory_space=pl.ANY)],
        out_specs=[row, rcol, _resident((8, D))],
        out_shape=[SDS((t, D), BF16), SDS((t, NPROJ), BF16), SDS((8, D), F32)],
        input_output_aliases={4: 1}, compiler_params=_ARB1)(dog, o, proj, gn, dproj)


def _conf_bwd(dcs, cc, proj, cw, lg, lb, dproj, carry=()):
    t = cc.shape[0]
    nt = t // TE
    hb = TE // HALO

    def body(dcs_ref, cc_ref, c1_ref, c2_ref, c1h_ref, c2h_ref, cw_ref, lg_ref, lb_ref, dp_in,
             dp_ref, dw_ref, ds_ref, cext, dext):
        del dp_in
        i = pl.program_id(0)
        ti = nt - 1 - i

        @pl.when(i == 0)
        def _():
            dext[TE:TE + HALO, :] = jnp.zeros((HALO, D), F32)
            dw_ref[...] = jnp.zeros_like(dw_ref)
            ds_ref[...] = jnp.zeros_like(ds_ref)

        @pl.when(i > 0)
        def _():
            dext[TE:TE + HALO, :] = dext[0:HALO, :]

        ch = c1h_ref[...].astype(F32) * _sigmoid(c2h_ref[...].astype(F32))
        cext[0:HALO, :] = jnp.where(ti > 0, ch, 0.0)

        def pre(r0):
            rows = pl.ds(r0, 32)
            cext[pl.ds(HALO + r0, 32), :] = c1_ref[rows, :].astype(F32) * _sigmoid(c2_ref[rows, :].astype(F32))
            x = cc_ref[rows, :]
            mu = jnp.mean(x, axis=-1, keepdims=True)
            xc = x - mu
            rstd = lax.rsqrt(jnp.mean(xc * xc, axis=-1, keepdims=True) + LN_EPS)
            xh = xc * rstd
            ln = xh * lg_ref[...] + lb_ref[...]
            sg = _sigmoid(ln)
            dln = dcs_ref[rows, :].astype(F32) * sg * (1.0 + ln * (1.0 - sg))
            ds_ref[0] += _rows8(dln * xh)
            ds_ref[1] += _rows8(dln)
            dxh = dln * lg_ref[...]
            dcc = rstd * (dxh - jnp.mean(dxh, axis=-1, keepdims=True) - xh * jnp.mean(dxh * xh, axis=-1, keepdims=True))
            dext[rows, :] = dcc
            ds_ref[2] += _rows8(dcc)
        _row_loop(TE, 32, pre, unroll=2)

        def convt(r0):
            rows = pl.ds(r0, CONV_ROWS)
            for part in range(D // CONV_LANES):
                cols = slice(part * CONV_LANES, (part + 1) * CONV_LANES)
                wd = dext[pl.ds(r0, CONV_ROWS + HALO), cols]
                dc = jnp.zeros((CONV_ROWS, CONV_LANES), F32)
                for s, taps in _shift_classes([(CONF_K - 1 - j, j) for j in range(CONF_K)]):
                    ws = _shifted(wd, s)
                    for a8, j in taps:
                        dc = dc + jnp.tile(cw_ref[j, :, cols], (CONV_ROWS // 8, 1)) * ws[a8:a8 + CONV_ROWS, :]
                dcc = wd[0:CONV_ROWS, :]
                wc = cext[pl.ds(r0, CONV_ROWS + HALO), cols]
                for s, taps in _shift_classes([(2 + j, j) for j in range(CONF_K)]):
                    ws = _shifted(wc, s)
                    for a8, j in taps:
                        dw_ref[j, :, cols] += _rows8(dcc * ws[a8:a8 + CONV_ROWS, :])
                c1 = c1_ref[rows, cols].astype(F32)
                s2 = _sigmoid(c2_ref[rows, cols].astype(F32))
                dp_ref[rows, cols] = (dc * s2).astype(BF16)
                dp_ref[rows, D + part * CONV_LANES:D + (part + 1) * CONV_LANES] = (dc * c1 * s2 * (1.0 - s2)).astype(BF16)
        _row_loop(TE, CONV_ROWS, convt)

    rev = lambda col: pl.BlockSpec((TE, D), lambda i: (nt - 1 - i, col))
    halo = lambda col: pl.BlockSpec((HALO, D), lambda i: (jnp.maximum((nt - 1 - i) * hb - 1, 0), col))
    return _carried_call(
        body, carry, name="conf_bwd", grid=(nt,),
        in_specs=[rev(0), rev(0), rev(2), rev(3), halo(2), halo(3), _resident((32, 8, D)), _resident((1, D)),
                  _resident((1, D)), pl.BlockSpec(memory_space=pl.ANY)],
        out_specs=[pl.BlockSpec((TE, 2 * D), lambda i: (nt - 1 - i, 1)), _resident((32, 8, D)), _resident((3, 8, D))],
        out_shape=[SDS((t, NPROJ), BF16), SDS((32, 8, D), F32), SDS((3, 8, D), F32)],
        scratch_shapes=[pltpu.VMEM((TE + HALO, D), F32), pltpu.VMEM((TE + HALO, D), F32)],
        input_output_aliases={9: 0}, compiler_params=_ARB1)(dcs, cc, proj, proj, proj, proj, cw, lg, lb, dproj)


def _gla_bwd(proj, alr, wau, balpha, do, sall, dproj, carry=()):
    t = proj.shape[0]
    nc = t // CH

    def body(qk_ref, v_ref, a_ref, wau_ref, ba_ref, do_ref, s_ref, dp_in, dp_ref, da_ref, dwau_ref, dba_ref, ds_scr, dla_scr):
        del dp_in

        @pl.when(pl.program_id(0) == 0)
        def _():
            ds_scr[...] = jnp.zeros_like(ds_scr)
            dwau_ref[...] = jnp.zeros_like(dwau_ref)
            dba_ref[...] = jnp.zeros_like(dba_ref)

        z, b, bmid, blast, causal = _gla_decay(a_ref, wau_ref, ba_ref)
        dlasts = []
        for h in range(HEADS):
            ks = slice(h * DKH, (h + 1) * DKH)
            vs = slice(h * DVH, (h + 1) * DVH)
            bh, mh, lh = b[:, ks], bmid[:, ks], blast[:, ks]
            q = qk_ref[:, ks].astype(F32) * (DKH ** -0.5)
            k = qk_ref[:, DK + h * DKH:DK + (h + 1) * DKH].astype(F32)
            v = v_ref[:, vs]
            dout = do_ref[:, vs]
            eq, ek, eb, eg, el = jnp.exp(bh - mh), jnp.exp(mh - bh), jnp.exp(bh), jnp.exp(lh - bh), jnp.exp(lh)
            qt, kt = (q * eq).astype(BF16), (k * ek).astype(BF16)
            qg, kg = (q * eb).astype(BF16), (k * eg).astype(BF16)
            st = s_ref[0, vs, :]
            dsn = ds_scr[vs, :]
            st16, dsn16 = st.astype(BF16), dsn.astype(BF16)
            a = jnp.where(causal, _dot_nt(qt, kt), 0.0).astype(BF16)
            da = jnp.where(causal, _dot_nt(dout, v), 0.0).astype(BF16)
            dq_inter = _dot(dout, st16) * eb
            dk_inter = _dot(v, dsn16) * eg
            dq = _dot(da, kt) * eq + dq_inter
            dk = _dot_tn(da, qt) * ek + dk_inter
            dv = _dot_tn(a, dout) + _dot_nt(kg, dsn16)
            dlasts.append(jnp.sum(k * dk_inter, axis=0, keepdims=True) + jnp.sum(st * dsn, axis=0, keepdims=True) * el[0:1, :])
            dla_scr[:, ks] = q * dq - k * dk
            ds_scr[vs, :] = dsn * jnp.concatenate([el, el], axis=0) + _dot_tn(dout, qg)
            dp_ref[:, ks] = (dq * (DKH ** -0.5)).astype(BF16)
            dp_ref[:, DK + h * DKH:DK + (h + 1) * DKH] = dk.astype(BF16)
            dp_ref[:, D + h * DVH:D + (h + 1) * DVH] = dv.astype(BF16)
        r = lax.broadcasted_iota(jnp.int32, (CH, CH), 0)
        c = lax.broadcasted_iota(jnp.int32, (CH, CH), 1)
        dla = _tri_matmul((r <= c).astype(BF16), dla_scr[...]) + jnp.concatenate(dlasts, axis=1)
        dz = (dla * (1.0 / TAU) * _sigmoid(-z)).astype(BF16)
        da_ref[...] = _dot_nt(dz, wau_ref[...]).astype(BF16)
        dwau_ref[...] += _dot_tn(a_ref[...], dz)
        dba_ref[...] += _rows8(dz.astype(F32))

    rev = lambda w, col: pl.BlockSpec((CH, w), lambda c: (nc - 1 - c, col))
    return _carried_call(
        body, carry, name="gla_bwd", grid=(nc,),
        in_specs=[rev(D, 0), rev(D, 1), rev(LANES, 0), _resident((LANES, DK)), _resident((1, DK)), rev(D, 0),
                  pl.BlockSpec((1, DV, DKH), lambda c: (nc - 1 - c, 0, 0)), pl.BlockSpec(memory_space=pl.ANY)],
        out_specs=[rev(2 * D, 0), rev(LANES, 0), _resident((LANES, DK)), _resident((8, DK))],
        out_shape=[SDS((t, NPROJ), BF16), SDS((t, LANES), BF16), SDS((LANES, DK), F32), SDS((8, DK), F32)],
        scratch_shapes=[pltpu.VMEM((DV, DKH), F32), pltpu.VMEM((CH, DK), F32)],
        input_output_aliases={7: 0}, compiler_params=_ARB1)(proj, proj, alr, wau, balpha, do, sall, dproj)


def _all_gather(xs, name):
    n = len(xs)

    def body(*refs):
        x_refs, out_refs = refs[:n], refs[n:2 * n]
        send_sems, recv_sems, local_sems = refs[2 * n:]
        x, y, c = _my_place()
        me, sibling = (x, y, c), (x, y, 1 - c)
        chips = [(1 - x, y), (x, 1 - y), (1 - x, 1 - y)]

        def slot(p, px, py, pc):
            return out_refs[p].at[4 * px + 2 * py + pc]

        def copy(p, k, block, to, src=None):
            return pltpu.make_async_remote_copy(
                src_ref=slot(p, *block) if src is None else src, dst_ref=slot(p, *block),
                send_sem=send_sems.at[7 * p + k], recv_sem=recv_sems.at[7 * p + k], device_id=to, device_id_type=MESH_T)

        mine = [pltpu.make_async_copy(x_refs[p], slot(p, *me), local_sems.at[p]) for p in range(n)]
        for cp in mine:
            cp.start()
        first = []
        for p in range(n):
            first.append(copy(p, 0, me, sibling, src=x_refs[p]))
            first += [copy(p, 1 + j, me, (*chip, c), src=x_refs[p]) for j, chip in enumerate(chips)]
        for cp in first:
            cp.start()
        passed = []
        for p in range(n):
            for j, chip in enumerate(chips):
                copy(p, 1 + j, (*chip, c), me).wait_recv()
                fwd = copy(p, 4 + j, (*chip, c), sibling)
                fwd.start()
                passed.append(fwd)
        for p in range(n):
            copy(p, 0, sibling, me).wait_recv()
            for j, chip in enumerate(chips):
                copy(p, 4 + j, (*chip, 1 - c), me).wait_recv()
        for cp in first + passed:
            cp.wait_send()
        for cp in mine:
            cp.wait()

    hbm = pl.BlockSpec(memory_space=pl.ANY)
    return pl.pallas_call(
        body, name=name, out_shape=[SDS((N_DEV, *a.shape), a.dtype) for a in xs],
        in_specs=[hbm] * n, out_specs=[hbm] * n,
        scratch_shapes=[pltpu.SemaphoreType.DMA((7 * n,)), pltpu.SemaphoreType.DMA((7 * n,)),
                        pltpu.SemaphoreType.DMA((n,))])(*xs)


def _exchange(gs):
    n = len(gs)

    def body(*refs):
        g_refs, land_refs = refs[:n], refs[n:2 * n]
        send_sems, recv_sems, local_sems = refs[2 * n:]
        x, y, c = _my_place()
        my_idx = 4 * x + 2 * y + c
        mine = [pltpu.make_async_copy(g_refs[p].at[my_idx], land_refs[p].at[my_idx], local_sems.at[p]) for p in range(n)]
        for cp in mine:
            cp.start()
        copies = []
        for k in range(1, N_DEV):
            px, py, pc = _flip(x, k & 4), _flip(y, k & 2), _flip(c, k & 1)
            p_idx = 4 * px + 2 * py + pc
            for p in range(n):
                s = 7 * p + k - 1
                cp = pltpu.make_async_remote_copy(
                    src_ref=g_refs[p].at[p_idx], dst_ref=land_refs[p].at[my_idx], send_sem=send_sems.at[s],
                    recv_sem=recv_sems.at[s], device_id=(px, py, pc), device_id_type=MESH_T)
                cp.start()
                arrival = pltpu.make_async_remote_copy(
                    src_ref=g_refs[p].at[p_idx], dst_ref=land_refs[p].at[p_idx], send_sem=send_sems.at[s],
                    recv_sem=recv_sems.at[s], device_id=(px, py, pc), device_id_type=MESH_T)
                copies.append((cp, arrival))
        for cp, arrival in copies:
            arrival.wait_recv()
        for cp, arrival in copies:
            cp.wait_send()
        for cp in mine:
            cp.wait()

    hbm = pl.BlockSpec(memory_space=pl.ANY)
    return pl.pallas_call(
        body, name="grad_exchange", out_shape=[SDS(g.shape, g.dtype) for g in gs],
        in_specs=[hbm] * n, out_specs=[hbm] * n,
        scratch_shapes=[pltpu.SemaphoreType.DMA((7 * n,)), pltpu.SemaphoreType.DMA((7 * n,)),
                        pltpu.SemaphoreType.DMA((n,))])(*gs)


def _adamw(land, w, m, v, rows_blk, name):
    rows = w.shape[0]

    def body(l_ref, w_ref, m_ref, v_ref, g_ref, d_ref, nm_ref, nv_ref):
        g = l_ref[0].astype(F32)
        for s in range(1, N_DEV):
            g = g + l_ref[s].astype(F32)
        nm = ADAM_B1 * m_ref[...] + (1.0 - ADAM_B1) * g
        nv = ADAM_B2 * v_ref[...] + (1.0 - ADAM_B2) * (g * g)
        m_hat = nm / (1.0 - ADAM_B1 ** ADAM_STEP)
        v_hat = nv / (1.0 - ADAM_B2 ** ADAM_STEP)
        g_ref[...] = g
        d_ref[...] = -ADAM_LR * (m_hat / (jnp.sqrt(v_hat) + ADAM_EPS) + ADAM_WD * w_ref[...])
        nm_ref[...] = nm
        nv_ref[...] = nv

    blk = pl.BlockSpec((rows_blk, D), lambda i: (i, 0))
    return pl.pallas_call(
        body, name=name, grid=(rows // rows_blk,),
        in_specs=[pl.BlockSpec((N_DEV, rows_blk, D), lambda i: (0, i, 0)), blk, blk, blk],
        out_specs=[blk] * 4, out_shape=[SDS((rows, D), F32)] * 4, compiler_params=_ARB1)(land, w, m, v)


BIG = ("w_in", "w_up", "w_down", "w_gla_o", "w_conf_o", "w_out")
BIG_TRANSPOSED = ("w_in", "w_up")
SMALL_SHARDED = ("meta_tokens", "conf_dw_w", "ffn_dw_w", "w_alpha_up")
REPLICATED = ("norm_mix_g", "b_alpha", "gla_norm_g", "conf_dw_b", "conf_ln_g", "conf_ln_b", "norm_ffn_g", "ffn_dw_b",
              "final_norm_g")
N_IN = sum(IN_WIDTHS)
W_IN_ROWS = N_IN // N_DEV
W_IN_PAD = -(-W_IN_ROWS // 16) * 16
ADAM_BLOCK = {"w_in": W_IN_PAD // 3, "w_up": 176, "w_down": 176, "w_gla_o": 128, "w_conf_o": 128, "w_out": 128}
SMALL_ROWS = 32


def _to_panel(name, shard):
    a = shard.reshape(shard.shape[-2], shard.shape[-1])
    if name in BIG_TRANSPOSED:
        a = a.T
    if name == "w_in":
        a = jnp.pad(a, ((0, W_IN_PAD - W_IN_ROWS), (0, 0)))
    return a


def _from_panel(name, panel, shape):
    a = panel[0:W_IN_ROWS] if name == "w_in" else panel
    if name in BIG_TRANSPOSED:
        a = a.T
    return a.reshape(shape)


def _pack_small(arrs):
    flat = jnp.concatenate([jnp.pad(a.reshape(-1), (0, (-a.size) % D)) for a in arrs])
    return jnp.pad(flat, (0, SMALL_ROWS * D - flat.shape[0])).reshape(SMALL_ROWS, D)


def _unpack_small(panel, shapes):
    flat, out, off = panel.reshape(-1), [], 0
    for shp in shapes:
        n = 1
        for s in shp:
            n *= s
        out.append(flat[off:off + n].reshape(shp))
        off += n + (-n) % D
    return out


def _local_step(x, target, w, shards=None):
    dist = shards is not None
    w = dict(w)

    def gather(names):
        return [(shards[n], False) for n in names] if dist else []

    def scatter(*arrs):
        return [(a.reshape(N_DEV, -1, D), True) for a in arrs] if dist else []

    s = x.shape[0]
    n_real = s + N_META
    t = -(-n_real // TM) * TM
    h0 = jnp.concatenate([w["meta_tokens"], x, jnp.zeros((t - n_real, D), F32)], axis=0)
    tgt = jnp.concatenate([jnp.zeros((N_META, D), F32), target, jnp.zeros((t - n_real, D), F32)], axis=0)

    q0, r0, a0, c0 = 0, 2 * DK + DV, 2 * DK + 2 * DV, 2 * DK + 2 * DV + RANK
    wt = w["w_in_t"]
    w_main = jnp.concatenate([wt[q0:r0], wt[c0:N_IN], wt[r0:a0]], axis=0)
    w_a = jnp.pad(wt[a0:c0], ((0, LANES - RANK), (0, 0)))
    wau = jnp.pad(w["w_alpha_up"].astype(BF16), ((0, LANES - RANK), (0, 0)))
    row = lambda name: w[name].reshape(1, -1)
    cw = jnp.broadcast_to(jnp.pad(w["conf_dw_w"], ((0, 32 - CONF_K), (0, 0)))[:, None, :], (32, 8, D))
    fw = jnp.pad(w["ffn_dw_w"], ((0, 8 - FFN_K), (0, 0)))

    early = ("w_gla_o", "w_conf_o", "w_out", "w_up")
    u1, proj, alr, *landed = _norm_matmul(h0, row("norm_mix_g"), w_main, 1024, "in_proj", w_extra_t=w_a, carry=gather(early))
    for n, land in zip(early, landed):
        w["w_up_t" if n == "w_up" else n] = land.reshape(-1, D)
    o, og, sall = _gla_fwd(proj, alr, wau, row("b_alpha"), row("gla_norm_g"))
    cc, cs, *landed = _conf_fwd(proj, cw, row("conf_dw_b"), row("conf_ln_g"), row("conf_ln_b"), carry=gather(("w_down",)))
    if dist:
        w["w_down"] = landed[0].reshape(-1, D)
    brg, brc, merged, h1 = _mix_fwd(og, cs, proj, h0, w["w_gla_o"], w["w_conf_o"], w["w_out"])
    u2, up = _norm_matmul(h1, row("norm_ffn_g"), w["w_up_t"], 512, "up_proj")
    f, dh2, red = _ffn_out(up, fw, row("ffn_dw_b"), h1, w["w_down"], row("final_norm_g"), tgt, n_real)
    loss = 0.5 / D * jnp.sum(red[0:8])

    g = {"final_norm_g": jnp.sum(red[8:16], axis=0)}
    dup, dfw = _ffn_bwd(dh2, w["w_down"], up, fw, row("ffn_dw_b"))
    g["ffn_dw_w"] = jnp.sum(dfw[0:FFN_K], axis=1)
    g["ffn_dw_b"] = jnp.sum(dfw[3], axis=0)
    g["w_down"] = _wgrad(f, dh2, 1408, "wgrad_down")
    dh1, dg2, *landed = _dgrad_norm(dup, w["w_up_t"], h1, row("norm_ffn_g"), dh2, "up_dgrad", carry=scatter(g["w_down"]))
    if dist:
        g["w_down"] = landed[0]
    g["norm_ffn_g"] = jnp.sum(dg2, axis=0)
    g["w_up_t"] = _wgrad(dup, u2, 1408, "wgrad_up")
    dbrg, dbrc, dog, dcs, dproj = _mix_bwd(dh1, w["w_out"], w["w_gla_o"], w["w_conf_o"], proj, brg, brc)
    g["w_out"] = _wgrad(merged, dh1, 1024, "wgrad_out")
    g["w_gla_o"] = _wgrad(og, dbrg, 1024, "wgrad_gla_o")
    g["w_conf_o"] = _wgrad(cs, dbrc, 1024, "wgrad_conf_o")
    do, dproj, dgn = _glapost_bwd(dog, o, proj, row("gla_norm_g"), dproj)
    g["gla_norm_g"] = jnp.sum(dgn, axis=0)
    dproj, dcw, dst, *landed = _conf_bwd(dcs, cc, proj, cw, row("conf_ln_g"), row("conf_ln_b"), dproj,
                                         carry=scatter(g["w_up_t"]))
    if dist:
        g["w_up_t"] = landed[0]
    g["conf_dw_w"] = jnp.sum(dcw[0:CONF_K], axis=1)
    g["conf_ln_g"], g["conf_ln_b"], g["conf_dw_b"] = jnp.sum(dst[0], axis=0), jnp.sum(dst[1], axis=0), jnp.sum(dst[2], axis=0)
    dproj, dalr, dwau, dba, *landed = _gla_bwd(proj, alr, wau, row("b_alpha"), do, sall, dproj,
                                               carry=scatter(g["w_out"], g["w_gla_o"], g["w_conf_o"]))
    if dist:
        g["w_out"], g["w_gla_o"], g["w_conf_o"] = landed
    g["w_alpha_up"] = dwau[0:RANK]
    g["b_alpha"] = jnp.sum(dba, axis=0)
    dw_main = _wgrad(dproj, u1, 1024, "wgrad_in")
    dw_a = _wgrad(dalr, u1, LANES, "wgrad_alr")
    g["w_in_t"] = jnp.concatenate([dw_main[0:r0], dw_main[NPROJ - DV:NPROJ], dw_a[0:RANK], dw_main[r0:NPROJ - DV]], axis=0)
    w_in_blocks = []
    if dist:
        pad = ((0, 0), (0, W_IN_PAD - W_IN_ROWS), (0, 0))
        w_in_blocks = [(jnp.pad(g["w_in_t"].reshape(N_DEV, W_IN_ROWS, D), pad), True)]
    dh0, dg1, *landed = _dgrad_norm(dproj, w_main, h0, row("norm_mix_g"), dh1, "in_dgrad", dy_extra=dalr,
                                    w_extra_t=w_a, carry=w_in_blocks)
    if dist:
        g["w_in_t"] = landed[0]
    g["norm_mix_g"] = jnp.sum(dg1, axis=0)
    g["meta_tokens"] = dh0[0:N_META]
    return loss, dh0[N_META:n_real], g


def kernel(x, meta_tokens, norm_mix_g, w_in, w_alpha_up, b_alpha, gla_norm_g, w_gla_o, conf_dw_w, conf_dw_b, conf_ln_g, conf_ln_b, w_conf_o, w_out, norm_ffn_g, w_up, ffn_dw_w, ffn_dw_b, w_down, final_norm_g, loss_target, m_meta_tokens, m_norm_mix_g, m_w_in, m_w_alpha_up, m_b_alpha, m_gla_norm_g, m_w_gla_o, m_conf_dw_w, m_conf_dw_b, m_conf_ln_g, m_conf_ln_b, m_w_conf_o, m_w_out, m_norm_ffn_g, m_w_up, m_ffn_dw_w, m_ffn_dw_b, m_w_down, m_final_norm_g, v_meta_tokens, v_norm_mix_g, v_w_in, v_w_alpha_up, v_b_alpha, v_gla_norm_g, v_w_gla_o, v_conf_dw_w, v_conf_dw_b, v_conf_ln_g, v_conf_ln_b, v_w_conf_o, v_w_out, v_norm_ffn_g, v_w_up, v_ffn_dw_w, v_ffn_dw_b, v_w_down, v_final_norm_g):
    ws = dict(meta_tokens=meta_tokens, norm_mix_g=norm_mix_g, w_in=w_in, w_alpha_up=w_alpha_up, b_alpha=b_alpha,
              gla_norm_g=gla_norm_g, w_gla_o=w_gla_o, conf_dw_w=conf_dw_w, conf_dw_b=conf_dw_b, conf_ln_g=conf_ln_g,
              conf_ln_b=conf_ln_b, w_conf_o=w_conf_o, w_out=w_out, norm_ffn_g=norm_ffn_g, w_up=w_up, ffn_dw_w=ffn_dw_w,
              ffn_dw_b=ffn_dw_b, w_down=w_down, final_norm_g=final_norm_g)
    ms = dict(meta_tokens=m_meta_tokens, norm_mix_g=m_norm_mix_g, w_in=m_w_in, w_alpha_up=m_w_alpha_up, b_alpha=m_b_alpha,
              gla_norm_g=m_gla_norm_g, w_gla_o=m_w_gla_o, conf_dw_w=m_conf_dw_w, conf_dw_b=m_conf_dw_b,
              conf_ln_g=m_conf_ln_g, conf_ln_b=m_conf_ln_b, w_conf_o=m_w_conf_o, w_out=m_w_out, norm_ffn_g=m_norm_ffn_g,
              w_up=m_w_up, ffn_dw_w=m_ffn_dw_w, ffn_dw_b=m_ffn_dw_b, w_down=m_w_down, final_norm_g=m_final_norm_g)
    vs = dict(meta_tokens=v_meta_tokens, norm_mix_g=v_norm_mix_g, w_in=v_w_in, w_alpha_up=v_w_alpha_up, b_alpha=v_b_alpha,
              gla_norm_g=v_gla_norm_g, w_gla_o=v_w_gla_o, conf_dw_w=v_conf_dw_w, conf_dw_b=v_conf_dw_b,
              conf_ln_g=v_conf_ln_g, conf_ln_b=v_conf_ln_b, w_conf_o=v_w_conf_o, w_out=v_w_out, norm_ffn_g=v_norm_ffn_g,
              w_up=v_w_up, ffn_dw_w=v_ffn_dw_w, ffn_dw_b=v_ffn_dw_b, w_down=v_w_down, final_norm_g=v_final_norm_g)
    small = SMALL_SHARDED + REPLICATED
    pack_small = lambda d: _pack_small([d[n] for n in small])

    shards = {n: _to_panel(n, ws[n]).astype(BF16) for n in BIG}
    gathered = _all_gather([shards["w_in"], pack_small(ws)], "weight_gather")
    full = {n: ws[n].reshape(-1) for n in REPLICATED}
    full["w_in_t"] = gathered[0][:, 0:W_IN_ROWS].reshape(N_IN, D)
    flat, off = gathered[1].reshape(N_DEV, -1), 0
    for n in SMALL_SHARDED:
        k, c = ws[n].shape[-2], ws[n].shape[-1]
        full[n] = flat[:, off:off + k * c].reshape(N_DEV, k, c).transpose(1, 0, 2).reshape(k, N_DEV * c)
        off += k * c + (-(k * c)) % D

    loss, grad_x, g = _local_step(x[0], loss_target[0], full, shards)

    lands = [g["w_in_t"], g["w_up_t"]] + [g[n] for n in BIG[2:]]
    blocks = []
    for n in SMALL_SHARDED:
        k, c = ws[n].shape[-2], ws[n].shape[-1]
        b = g[n].reshape(k, N_DEV, c).transpose(1, 0, 2).reshape(N_DEV, k * c)
        blocks.append(jnp.pad(b, ((0, 0), (0, (-(k * c)) % D))))
    for n in REPLICATED:
        b = jnp.broadcast_to(g[n].reshape(1, -1), (N_DEV, g[n].size))
        blocks.append(jnp.pad(b, ((0, 0), (0, (-b.shape[1]) % D))))
    gsm = jnp.concatenate(blocks, axis=1)
    lands += _exchange([jnp.pad(gsm, ((0, 0), (0, SMALL_ROWS * D - gsm.shape[1]))).reshape(N_DEV, SMALL_ROWS, D)])

    grad, delta, new_m, new_v = {}, {}, {}, {}
    for i, n in enumerate(BIG):
        outs = _adamw(lands[i], _to_panel(n, ws[n]), _to_panel(n, ms[n]), _to_panel(n, vs[n]), ADAM_BLOCK[n], "adamw_" + n)
        grad[n], delta[n], new_m[n], new_v[n] = [_from_panel(n, p, ws[n].shape) for p in outs]
    outs = _adamw(lands[len(BIG)], pack_small(ws), pack_small(ms), pack_small(vs), SMALL_ROWS, "adamw_small")
    shapes = [ws[n].shape for n in small]
    for d, p in zip((grad, delta, new_m, new_v), outs):
        d.update(zip(small, _unpack_small(p, shapes)))

    order = ("meta_tokens", "norm_mix_g", "w_in", "w_alpha_up", "b_alpha", "gla_norm_g", "w_gla_o", "conf_dw_w", "conf_dw_b",
             "conf_ln_g", "conf_ln_b", "w_conf_o", "w_out", "norm_ffn_g", "w_up", "ffn_dw_w", "ffn_dw_b", "w_down",
             "final_norm_g")
    loss = lax.psum(loss, ("x", "y", "c"))
    return (loss, grad_x[None], *[grad[n] for n in order], *[delta[n] for n in order], *[new_m[n] for n in order],
            *[new_v[n] for n in order])
```

```python
import functools

import jax
import jax.numpy as jnp
from jax import lax
from jax.experimental import pallas as pl
from jax.experimental.pallas import tpu as pltpu

F32, BF16 = jnp.float32, jnp.bfloat16
SDS = jax.ShapeDtypeStruct

D = 1024
N_META = 16
HEADS = 4
DK, DKH, DV, DVH = 512, 128, 1024, 256
RANK = 16
TAU = 16.0
CONF_K = 31
DFF = 2816
FFN_K = 3
IN_WIDTHS = (DK, DK, DV, DV, RANK, 2 * D, D, D)
RMS_EPS, LN_EPS = 1e-6, 1e-5
ADAM_LR, ADAM_B1, ADAM_B2, ADAM_EPS, ADAM_WD, ADAM_STEP = 0.001, 0.9, 0.999, 1e-08, 0.01, 10

NPROJ = 7 * D
LANES = 128
CH = 128
TM = 640
TM_BIG = 1664
TE = 320
HALO = 32
HALO_F = 16
N_DEV = 8
VMEM_LIMIT = 60 * 1024 * 1024
MESH_T = pl.DeviceIdType.MESH

_ARB1 = pltpu.CompilerParams(dimension_semantics=("arbitrary",), vmem_limit_bytes=VMEM_LIMIT)
_ARB2 = pltpu.CompilerParams(dimension_semantics=("arbitrary", "arbitrary"), vmem_limit_bytes=VMEM_LIMIT)


def _dot(a, b):
    return jnp.dot(a, b, preferred_element_type=F32)


def _dot_nt(a, b):
    return lax.dot_general(a, b, (((1,), (1,)), ((), ())), preferred_element_type=F32)


def _dot_tn(a, b):
    return lax.dot_general(a, b, (((0,), (0,)), ((), ())), preferred_element_type=F32)


def _sigmoid(x):
    return 0.5 * jnp.tanh(0.5 * x) + 0.5


def _rows8(x):
    return x.reshape(x.shape[0] // 8, 8, x.shape[1]).sum(axis=0)


def _row_tile(t, preferred):
    return preferred if t % preferred == 0 else TM


def _row_loop(n_rows, rb, fn, unroll=1):
    def step(i, carry):
        fn(pl.multiple_of(i * rb, rb))
        return carry
    lax.fori_loop(0, n_rows // rb, step, 0, unroll=unroll)


def _resident(shape):
    return pl.BlockSpec(shape, lambda *_: (0,) * len(shape))


CONV_ROWS, CONV_LANES = 64, 256


def _shift_classes(offset_taps):
    return [(s, [(o - s, j) for o, j in offset_taps if o % 8 == s]) for s in range(8)]


def _shifted(win, s):
    return win if s == 0 else win[s:s + CONV_ROWS + HALO - 8, :]


def _my_place():
    return lax.axis_index("x"), lax.axis_index("y"), lax.axis_index("c")


def _flip(v, bit):
    return 1 - v if bit else v


def _exchange_copies(src_refs, land_refs, scatter, send_sems, recv_sems, local_sems, arrivals):
    x, y, c = _my_place()
    my_idx = 4 * x + 2 * y + c
    local, remote = [], []
    for p, (src, land) in enumerate(zip(src_refs, land_refs)):
        local.append(pltpu.make_async_copy(src.at[my_idx] if scatter[p] else src, land.at[my_idx], local_sems.at[p]))
    for k in range(1, N_DEV):
        px, py, pc = _flip(x, k & 4), _flip(y, k & 2), _flip(c, k & 1)
        p_idx = 4 * px + 2 * py + pc
        for p, (src, land) in enumerate(zip(src_refs, land_refs)):
            s = 7 * p + k - 1
            out = src.at[p_idx] if scatter[p] else src

            def copy(dst):
                return pltpu.make_async_remote_copy(src_ref=out, dst_ref=dst, send_sem=send_sems.at[s],
                                                    recv_sem=recv_sems.at[s], device_id=(px, py, pc), device_id_type=MESH_T)
            remote.append((copy(land.at[my_idx]), copy(land.at[p_idx]) if arrivals else None))
    return local, remote


def _carried_call(core, carry, *, grid, in_specs, out_specs, out_shape, scratch_shapes=(), **kw):
    n_in, n_out, nc, n_scr = len(in_specs), len(out_specs), len(carry), len(scratch_shapes)
    if nc == 0:
        return pl.pallas_call(core, grid=grid, in_specs=in_specs, out_specs=out_specs, out_shape=out_shape,
                              scratch_shapes=list(scratch_shapes), **kw)
    scatter = [sc for _, sc in carry]

    def body(*refs):
        ins, cin = refs[:n_in], refs[n_in:n_in + nc]
        outs, cout = refs[n_in + nc:n_in + nc + n_out], refs[n_in + nc + n_out:n_in + 2 * nc + n_out]
        scr, sems = refs[n_in + 2 * nc + n_out:n_in + 2 * nc + n_out + n_scr], refs[-3:]
        first = functools.reduce(jnp.logical_and, [pl.program_id(a) == 0 for a in range(len(grid))])
        last = functools.reduce(jnp.logical_and, [pl.program_id(a) == grid[a] - 1 for a in range(len(grid))])

        @pl.when(first)
        def _():
            local, remote = _exchange_copies(cin, cout, scatter, *sems, arrivals=False)
            for cp in local:
                cp.start()
            for send, _ in remote:
                send.start()

        core(*ins, *outs, *scr)

        @pl.when(last)
        def _():
            local, remote = _exchange_copies(cin, cout, scatter, *sems, arrivals=True)
            for _, arrival in remote:
                arrival.wait_recv()
            for send, _ in remote:
                send.wait_send()
            for cp in local:
                cp.wait()

    hbm = pl.BlockSpec(memory_space=pl.ANY)
    land_shape = [SDS((N_DEV, *(a.shape[1:] if sc else a.shape)), a.dtype) for a, sc in carry]
    sems = [pltpu.SemaphoreType.DMA((7 * nc,)), pltpu.SemaphoreType.DMA((7 * nc,)), pltpu.SemaphoreType.DMA((nc,))]
    call = pl.pallas_call(body, grid=grid, in_specs=list(in_specs) + [hbm] * nc, out_specs=list(out_specs) + [hbm] * nc,
                          out_shape=list(out_shape) + land_shape, scratch_shapes=list(scratch_shapes) + sems, **kw)
    return lambda *args: call(*args, *[a for a, _ in carry])


def _norm_matmul(h, g, w_t, tn, name, w_extra_t=None, carry=()):
    t, n = h.shape[0], w_t.shape[0]
    tm = _row_tile(t, TM_BIG)
    nt, nb = t // tm, n // tn

    def body(*refs):
        if w_extra_t is None:
            h_ref, g_ref, w_ref, u_ref, p_ref = refs
        else:
            h_ref, g_ref, w_ref, we_ref, u_ref, p_ref, e_ref = refs

        @pl.when(pl.program_id(1) == 0)
        def _():
            def blk(r0):
                x = h_ref[pl.ds(r0, 32), :]
                rinv = lax.rsqrt(jnp.mean(x * x, axis=-1, keepdims=True) + RMS_EPS)
                u_ref[pl.ds(r0, 32), :] = (x * rinv * g_ref[...]).astype(BF16)
            _row_loop(tm, 32, blk, unroll=2)
            if w_extra_t is not None:
                e_ref[...] = _dot_nt(u_ref[...], we_ref[...]).astype(BF16)

        p_ref[...] = _dot_nt(u_ref[...], w_ref[...]).astype(BF16)

    in_specs = [pl.BlockSpec((tm, D), lambda i, j: (i, 0)), _resident((1, D)), pl.BlockSpec((tn, D), lambda i, j: (j, 0))]
    out_specs = [pl.BlockSpec((tm, D), lambda i, j: (i, 0)), pl.BlockSpec((tm, tn), lambda i, j: (i, j))]
    out_shape = [SDS((t, D), BF16), SDS((t, n), BF16)]
    args = [h, g, w_t]
    if w_extra_t is not None:
        in_specs.append(_resident(w_extra_t.shape))
        out_specs.append(pl.BlockSpec((tm, w_extra_t.shape[0]), lambda i, j: (i, 0)))
        out_shape.append(SDS((t, w_extra_t.shape[0]), BF16))
        args.append(w_extra_t)
    return _carried_call(body, carry, name=name, grid=(nt, nb), in_specs=in_specs, out_specs=out_specs,
                         out_shape=out_shape, compiler_params=_ARB2)(*args)


def _gla_decay(a_ref, wau_ref, ba_ref):
    z = _dot(a_ref[...], wau_ref[...]) + ba_ref[...]
    la = (jnp.minimum(z, 0.0) - jnp.log(1.0 + jnp.exp(-jnp.abs(z)))) * (1.0 / TAU)
    r = lax.broadcasted_iota(jnp.int32, (CH, CH), 0)
    c = lax.broadcasted_iota(jnp.int32, (CH, CH), 1)
    b = _tri_matmul((r >= c).astype(BF16), la)
    mid = jnp.broadcast_to(b[CH // 2:CH // 2 + 1, :], b.shape)
    last = jnp.broadcast_to(b[CH - 1:CH, :], b.shape)
    return z, b, mid, last, r >= c


def _tri_matmul(tri, x):
    n = x.shape[1]
    x1 = x.astype(BF16)
    r1 = x - x1.astype(F32)
    x2 = r1.astype(BF16)
    x3 = (r1 - x2.astype(F32)).astype(BF16)
    y = _dot(tri, jnp.concatenate([x1, x2, x3], axis=1))
    return y[:, 0:n] + y[:, n:2 * n] + y[:, 2 * n:3 * n]


def _gla_fwd(proj, alr, wau, balpha, gn):
    t = proj.shape[0]
    nc = t // CH

    def body(qk_ref, v_ref, r_ref, a_ref, wau_ref, ba_ref, gn_ref, o_ref, og_ref, sall_ref, s_scr):
        @pl.when(pl.program_id(0) == 0)
        def _():
            s_scr[...] = jnp.zeros_like(s_scr)

        sall_ref[0] = s_scr[...]
        _, b, bmid, blast, causal = _gla_decay(a_ref, wau_ref, ba_ref)
        for h in range(HEADS):
            ks = slice(h * DKH, (h + 1) * DKH)
            vs = slice(h * DVH, (h + 1) * DVH)
            bh, mh, lh = b[:, ks], bmid[:, ks], blast[:, ks]
            q = qk_ref[:, ks].astype(F32) * (DKH ** -0.5)
            k = qk_ref[:, DK + h * DKH:DK + (h + 1) * DKH].astype(F32)
            v = v_ref[:, vs]
            qt = (q * jnp.exp(bh - mh)).astype(BF16)
            kt = (k * jnp.exp(mh - bh)).astype(BF16)
            qg = (q * jnp.exp(bh)).astype(BF16)
            kg = (k * jnp.exp(lh - bh)).astype(BF16)
            a = jnp.where(causal, _dot_nt(qt, kt), 0.0)
            st = s_scr[vs, :]
            o = _dot(a.astype(BF16), v) + _dot_nt(qg, st.astype(BF16))
            el = jnp.exp(lh)
            s_scr[vs, :] = st * jnp.concatenate([el, el], axis=0) + _dot_tn(v, kg)
            o_ref[:, vs] = o
            on = o * lax.rsqrt(jnp.mean(o * o, axis=-1, keepdims=True) + RMS_EPS) * gn_ref[:, vs]
            rr = r_ref[:, vs].astype(F32)
            og_ref[:, vs] = (on * (rr * _sigmoid(rr))).astype(BF16)

    return pl.pallas_call(
        body, name="gla_fwd", grid=(nc,),
        in_specs=[pl.BlockSpec((CH, D), lambda c: (c, 0)), pl.BlockSpec((CH, D), lambda c: (c, 1)),
                  pl.BlockSpec((CH, D), lambda c: (c, 6)), pl.BlockSpec((CH, LANES), lambda c: (c, 0)),
                  _resident((LANES, DK)), _resident((1, DK)), _resident((1, DV))],
        out_specs=[pl.BlockSpec((CH, DV), lambda c: (c, 0)), pl.BlockSpec((CH, DV), lambda c: (c, 0)),
                   pl.BlockSpec((1, DV, DKH), lambda c: (c, 0, 0))],
        out_shape=[SDS((t, DV), F32), SDS((t, DV), BF16), SDS((nc, DV, DKH), F32)],
        scratch_shapes=[pltpu.VMEM((DV, DKH), F32)], compiler_params=_ARB1)(proj, proj, proj, alr, wau, balpha, gn)


def _conf_fwd(proj, cw, cb, lg, lb, carry=()):
    t = proj.shape[0]
    nt = t // TE

    def body(c1_ref, c2_ref, cw_ref, cb_ref, lg_ref, lb_ref, cc_ref, cs_ref, cext):
        i = pl.program_id(0)

        @pl.when(i == 0)
        def _():
            cext[0:HALO, :] = jnp.zeros((HALO, D), F32)

        @pl.when(i > 0)
        def _():
            cext[0:HALO, :] = cext[TE:TE + HALO, :]

        def glu(r0):
            c2 = c2_ref[pl.ds(r0, 32), :].astype(F32)
            cext[pl.ds(HALO + r0, 32), :] = c1_ref[pl.ds(r0, 32), :].astype(F32) * _sigmoid(c2)
        _row_loop(TE, 32, glu)

        def conv(r0):
            for part in range(D // CONV_LANES):
                cols = slice(part * CONV_LANES, (part + 1) * CONV_LANES)
                win = cext[pl.ds(r0, CONV_ROWS + HALO), cols]
                acc = jnp.zeros((CONV_ROWS, CONV_LANES), F32) + cb_ref[:, cols]
                for s, taps in _shift_classes([(2 + j, j) for j in range(CONF_K)]):
                    ws = _shifted(win, s)
                    for a8, j in taps:
                        acc = acc + jnp.tile(cw_ref[j, :, cols], (CONV_ROWS // 8, 1)) * ws[a8:a8 + CONV_ROWS, :]
                cc_ref[pl.ds(r0, CONV_ROWS), cols] = acc
            for sub in range(CONV_ROWS // 32):
                rows = pl.ds(r0 + 32 * sub, 32)
                x = cc_ref[rows, :]
                xc = x - jnp.mean(x, axis=-1, keepdims=True)
                var = jnp.mean(xc * xc, axis=-1, keepdims=True)
                ln = xc * lax.rsqrt(var + LN_EPS) * lg_ref[...] + lb_ref[...]
                cs_ref[rows, :] = (ln * _sigmoid(ln)).astype(BF16)
        _row_loop(TE, CONV_ROWS, conv)

    return _carried_call(
        body, carry, name="conf_fwd", grid=(nt,),
        in_specs=[pl.BlockSpec((TE, D), lambda i: (i, 2)), pl.BlockSpec((TE, D), lambda i: (i, 3)),
                  _resident((32, 8, D)), _resident((1, D)), _resident((1, D)), _resident((1, D))],
        out_specs=[pl.BlockSpec((TE, D), lambda i: (i, 0)), pl.BlockSpec((TE, D), lambda i: (i, 0))],
        out_shape=[SDS((t, D), F32), SDS((t, D), BF16)],
        scratch_shapes=[pltpu.VMEM((TE + HALO, D), F32)], compiler_params=_ARB1)(proj, proj, cw, cb, lg, lb)


def _mix_fwd(og, cs, proj, h0, wg, wc, wo):
    t = h0.shape[0]
    nt = t // TE

    def body(og_ref, cs_ref, g_ref, h0_ref, wg_ref, wc_ref, wo_ref, brg_ref, brc_ref, mg_ref, h1_ref, acc):
        acc[...] = _dot(og_ref[...], wg_ref[...])
        brg_ref[...] = acc[...].astype(BF16)
        acc[...] = _dot(cs_ref[...], wc_ref[...])
        brc_ref[...] = acc[...].astype(BF16)

        def blk(r0):
            rows = pl.ds(r0, 32)
            gg = g_ref[rows, 0:D].astype(F32)
            gc = g_ref[rows, D:2 * D].astype(F32)
            m = _sigmoid(gg) * brg_ref[rows, :].astype(F32) + _sigmoid(gc) * brc_ref[rows, :].astype(F32)
            mg_ref[rows, :] = m.astype(BF16)
        _row_loop(TE, 32, blk)
        h1_ref[...] = h0_ref[...] + _dot(mg_ref[...], wo_ref[...])

    row = lambda w: pl.BlockSpec((TE, w), lambda i: (i, 0))
    return pl.pallas_call(
        body, name="mix_fwd", grid=(nt,),
        in_specs=[row(D), row(D), pl.BlockSpec((TE, 2 * D), lambda i: (i, 2)), row(D),
                  _resident((D, D)), _resident((D, D)), _resident((D, D))],
        out_specs=[row(D), row(D), row(D), row(D)],
        out_shape=[SDS((t, D), BF16), SDS((t, D), BF16), SDS((t, D), BF16), SDS((t, D), F32)],
        scratch_shapes=[pltpu.VMEM((TE, D), F32)], compiler_params=_ARB1)(og, cs, proj, h0, wg, wc, wo)


def _ffn_out(up, fw, h1, wd, gf, tgt, n_real):
    t = h1.shape[0]
    nt = t // TE

    def body(a_ref, bv_ref, fw_ref, h1_ref, wd_ref, gf_ref, tg_ref, f_ref, dh2_ref, red_ref, aext, hs):
        i = pl.program_id(0)

        @pl.when(i == 0)
        def _():
            aext[0:HALO_F, :] = jnp.zeros((HALO_F, DFF), F32)
            red_ref[...] = jnp.zeros_like(red_ref)

        @pl.when(i > 0)
        def _():
            aext[0:HALO_F, :] = aext[TE:TE + HALO_F, :]

        def cp(r0):
            aext[pl.ds(HALO_F + r0, 16), :] = a_ref[pl.ds(r0, 16), :].astype(F32)
        _row_loop(TE, 16, cp)

        def conv(r0):
            win = aext[pl.ds(r0, 32), :]
            ac = fw_ref[3] + fw_ref[0] * win[14:30, :] + fw_ref[1] * win[15:31, :] + fw_ref[2] * win[16:32, :]
            f_ref[pl.ds(r0, 16), :] = (ac * _sigmoid(ac) * bv_ref[pl.ds(r0, 16), :].astype(F32)).astype(BF16)
        _row_loop(TE, 16, conv)
        hs[...] = h1_ref[...] + _dot(f_ref[...], wd_ref[...])

        def head(r0):
            rows = pl.ds(r0, 32)
            h2 = hs[rows, :]
            rinv = lax.rsqrt(jnp.mean(h2 * h2, axis=-1, keepdims=True) + RMS_EPS)
            hh = h2 * rinv
            gid = i * TE + r0 + lax.broadcasted_iota(jnp.int32, (32, 1), 0)
            live = jnp.logical_and(gid >= N_META, gid < n_real)
            err = jnp.where(live, hh * gf_ref[...] - tg_ref[rows, :], 0.0)
            dy = err * (1.0 / D)
            red_ref[0:8, :] += _rows8(err * err)
            red_ref[8:16, :] += _rows8(dy * hh)
            dhh = dy * gf_ref[...]
            dh2_ref[rows, :] = rinv * (dhh - hh * jnp.mean(dhh * hh, axis=-1, keepdims=True))
        _row_loop(TE, 32, head, unroll=2)

    row = lambda w: pl.BlockSpec((TE, w), lambda i: (i, 0))
    return pl.pallas_call(
        body, name="ffn_out", grid=(nt,),
        in_specs=[pl.BlockSpec((TE, DFF), lambda i: (i, 0)), pl.BlockSpec((TE, DFF), lambda i: (i, 1)),
                  _resident((4, 16, DFF)), row(D), _resident((DFF, D)), _resident((1, D)), row(D)],
        out_specs=[row(DFF), row(D), _resident((16, D))],
        out_shape=[SDS((t, DFF), BF16), SDS((t, D), F32), SDS((16, D), F32)],
        scratch_shapes=[pltpu.VMEM((TE + HALO_F, DFF), F32), pltpu.VMEM((TE, D), F32)],
        compiler_params=_ARB1)(up, up, fw, h1, wd, gf, tgt)


def _ffn_bwd(dh2, wd, up, fw):
    t = dh2.shape[0]
    nt = t // TE
    hb = TE // HALO_F

    def body(dh_ref, wd_ref, a_ref, ah_ref, bv_ref, fw_ref, dup_ref, dw_ref, aext, dax, dfs):
        i = pl.program_id(0)
        ti = nt - 1 - i

        @pl.when(i == 0)
        def _():
            dax[TE:TE + HALO_F, :] = jnp.zeros((HALO_F, DFF), F32)
            dw_ref[...] = jnp.zeros_like(dw_ref)

        @pl.when(i > 0)
        def _():
            dax[TE:TE + HALO_F, :] = dax[0:HALO_F, :]

        aext[0:HALO_F, :] = jnp.where(ti > 0, ah_ref[...].astype(F32), 0.0)
        dfs[...] = _dot_nt(dh_ref[...].astype(BF16), wd_ref[...])

        def cp(r0):
            aext[pl.ds(HALO_F + r0, 16), :] = a_ref[pl.ds(r0, 16), :].astype(F32)
        _row_loop(TE, 16, cp)

        def act(r0):
            rows = pl.ds(r0, 16)
            win = aext[pl.ds(r0, 32), :]
            ac = fw_ref[3] + fw_ref[0] * win[14:30, :] + fw_ref[1] * win[15:31, :] + fw_ref[2] * win[16:32, :]
            sg = _sigmoid(ac)
            df = dfs[rows, :]
            dup_ref[rows, DFF:2 * DFF] = (df * ac * sg).astype(BF16)
            dac = df * bv_ref[rows, :].astype(F32) * sg * (1.0 + ac * (1.0 - sg))
            dax[rows, :] = dac
            dw_ref[3] += _rows8(dac)
            for j in range(FFN_K):
                dw_ref[j] += _rows8(dac * win[14 + j:30 + j, :])
        _row_loop(TE, 16, act)

        def convt(r0):
            win = dax[pl.ds(r0, 32), :]
            da = fw_ref[2] * win[0:16, :] + fw_ref[1] * win[1:17, :] + fw_ref[0] * win[2:18, :]
            dup_ref[pl.ds(r0, 16), 0:DFF] = da.astype(BF16)
        _row_loop(TE, 16, convt)

    rev = lambda w: pl.BlockSpec((TE, w), lambda i: (nt - 1 - i, 0))
    return pl.pallas_call(
        body, name="ffn_bwd", grid=(nt,),
        in_specs=[rev(D), _resident((DFF, D)), rev(DFF),
                  pl.BlockSpec((HALO_F, DFF), lambda i: (jnp.maximum((nt - 1 - i) * hb - 1, 0), 0)),
                  pl.BlockSpec((TE, DFF), lambda i: (nt - 1 - i, 1)), _resident((4, 16, DFF))],
        out_specs=[rev(2 * DFF), _resident((4, 8, DFF))],
        out_shape=[SDS((t, 2 * DFF), BF16), SDS((4, 8, DFF), F32)],
        scratch_shapes=[pltpu.VMEM((TE + HALO_F, DFF), F32), pltpu.VMEM((TE + HALO_F, DFF), F32),
                        pltpu.VMEM((TE, DFF), F32)],
        compiler_params=_ARB1)(dh2, wd, up, up, up, fw)


def _dgrad_norm(dy, w_t, h, g, dres, name, dy_extra=None, w_extra_t=None, carry=()):
    t, k = dy.shape
    nt = t // TM

    def body(*refs):
        if dy_extra is None:
            dy_ref, w_ref, h_ref, g_ref, dr_ref, dh_ref, dg_ref, acc = refs
        else:
            dy_ref, w_ref, h_ref, g_ref, dr_ref, de_ref, we_ref, dh_ref, dg_ref, acc = refs

        @pl.when(pl.program_id(0) == 0)
        def _():
            dg_ref[...] = jnp.zeros_like(dg_ref)

        acc[...] = _dot(dy_ref[...], w_ref[...])
        if dy_extra is not None:
            acc[...] += _dot(de_ref[...], we_ref[...])

        def blk(r0):
            rows = pl.ds(r0, 32)
            x = h_ref[rows, :]
            rinv = lax.rsqrt(jnp.mean(x * x, axis=-1, keepdims=True) + RMS_EPS)
            hh = x * rinv
            du = acc[rows, :]
            dg_ref[...] += _rows8(du * hh)
            dhh = du * g_ref[...]
            dh_ref[rows, :] = dr_ref[rows, :] + rinv * (dhh - hh * jnp.mean(dhh * hh, axis=-1, keepdims=True))
        _row_loop(TM, 32, blk, unroll=2)

    row = pl.BlockSpec((TM, D), lambda i: (i, 0))
    once = lambda shape: pl.BlockSpec(shape, lambda i: (0, 0), pipeline_mode=pl.Buffered(1))
    in_specs = [pl.BlockSpec((TM, k), lambda i: (i, 0)), once((k, D)), row, _resident((1, D)), row]
    args = [dy, w_t, h, g, dres]
    if dy_extra is not None:
        in_specs += [pl.BlockSpec((TM, dy_extra.shape[1]), lambda i: (i, 0)), once(w_extra_t.shape)]
        args += [dy_extra, w_extra_t]
    return _carried_call(
        body, carry, name=name, grid=(nt,), in_specs=in_specs, out_specs=[row, _resident((8, D))],
        out_shape=[SDS((t, D), F32), SDS((8, D), F32)],
        scratch_shapes=[pltpu.VMEM((TM, D), F32)], compiler_params=_ARB1)(*args)


def _wgrad(x, dy, tk, name):
    t, k = x.shape
    n = dy.shape[1]
    tm = _row_tile(t, TM_BIG)
    nk, nt = k // tk, t // tm

    def body(x_ref, dy_ref, o_ref, acc):
        @pl.when(pl.program_id(1) == 0)
        def _():
            acc[...] = jnp.zeros_like(acc)
        acc[...] += _dot_tn(x_ref[...], dy_ref[...].astype(BF16))

        @pl.when(pl.program_id(1) == nt - 1)
        def _():
            o_ref[...] = acc[...].astype(BF16)

    return pl.pallas_call(
        body, name=name, grid=(nk, nt),
        in_specs=[pl.BlockSpec((tm, tk), lambda j, i: (i, j)), pl.BlockSpec((tm, n), lambda j, i: (i, 0))],
        out_specs=pl.BlockSpec((tk, n), lambda j, i: (j, 0)), out_shape=SDS((k, n), BF16),
        scratch_shapes=[pltpu.VMEM((tk, n), F32)], compiler_params=_ARB2)(x, dy)


def _mix_bwd(dh1, wo, wg, wc, proj, brg, brc):
    t = dh1.shape[0]
    nt = t // TE

    def body(dh_ref, wo_ref, wg_ref, wc_ref, g_ref, brg_ref, brc_ref, dbg_ref, dbc_ref, dog_ref, dcs_ref, dp_ref, dm):
        dm[...] = _dot_nt(dh_ref[...].astype(BF16), wo_ref[...])

        def blk(r0):
            rows = pl.ds(r0, 32)
            d = dm[rows, :]
            sg = _sigmoid(g_ref[rows, 0:D].astype(F32))
            sc = _sigmoid(g_ref[rows, D:2 * D].astype(F32))
            dbg_ref[rows, :] = (d * sg).astype(BF16)
            dbc_ref[rows, :] = (d * sc).astype(BF16)
            dp_ref[rows, 0:D] = (d * brg_ref[rows, :].astype(F32) * sg * (1.0 - sg)).astype(BF16)
            dp_ref[rows, D:2 * D] = (d * brc_ref[rows, :].astype(F32) * sc * (1.0 - sc)).astype(BF16)
        _row_loop(TE, 32, blk)
        dog_ref[...] = _dot_nt(dbg_ref[...], wg_ref[...]).astype(BF16)
        dcs_ref[...] = _dot_nt(dbc_ref[...], wc_ref[...]).astype(BF16)

    row = pl.BlockSpec((TE, D), lambda i: (i, 0))
    wide = pl.BlockSpec((TE, 2 * D), lambda i: (i, 2))
    return pl.pallas_call(
        body, name="mix_bwd", grid=(nt,),
        in_specs=[row, _resident((D, D)), _resident((D, D)), _resident((D, D)), wide, row, row],
        out_specs=[row, row, row, row, wide],
        out_shape=[SDS((t, D), BF16)] * 4 + [SDS((t, NPROJ), BF16)],
        scratch_shapes=[pltpu.VMEM((TE, D), F32)], compiler_params=_ARB1)(dh1, wo, wg, wc, proj, brg, brc)


def _glapost_bwd(dog, o, proj, gn, dproj):
    t = o.shape[0]
    nt = t // TE

    def body(dog_ref, o_ref, r_ref, gn_ref, dp_in, do_ref, dp_ref, dgn_ref):
        del dp_in

        @pl.when(pl.program_id(0) == 0)
        def _():
            dgn_ref[...] = jnp.zeros_like(dgn_ref)

        def blk(r0):
            rows = pl.ds(r0, 32)
            for h in range(HEADS):
                vs = slice(h * DVH, (h + 1) * DVH)
                x = o_ref[rows, vs]
                rinv = lax.rsqrt(jnp.mean(x * x, axis=-1, keepdims=True) + RMS_EPS)
                oh = x * rinv
                g = gn_ref[:, vs]
                rr = r_ref[rows, vs].astype(F32)
                sr = _sigmoid(rr)
                d = dog_ref[rows, vs].astype(F32)
                dp_ref[rows, vs] = (d * oh * g * sr * (1.0 + rr * (1.0 - sr))).astype(BF16)
                don = d * rr * sr
                dgn_ref[:, vs] += _rows8(don * oh)
                doh = don * g
                do_ref[rows, vs] = (rinv * (doh - oh * jnp.mean(doh * oh, axis=-1, keepdims=True))).astype(BF16)
        _row_loop(TE, 32, blk)

    row = pl.BlockSpec((TE, D), lambda i: (i, 0))
    rcol = pl.BlockSpec((TE, D), lambda i: (i, 6))
    return pl.pallas_call(
        body, name="glapost_bwd", grid=(nt,),
        in_specs=[row, row, rcol, _resident((1, D)), pl.BlockSpec(memory_space=pl.ANY)],
        out_specs=[row, rcol, _resident((8, D))],
        out_shape=[SDS((t, D), BF16), SDS((t, NPROJ), BF16), SDS((8, D), F32)],
        input_output_aliases={4: 1}, compiler_params=_ARB1)(dog, o, proj, gn, dproj)


def _conf_bwd(dcs, cc, proj, cw, lg, lb, dproj, carry=()):
    t = cc.shape[0]
    nt = t // TE
    hb = TE // HALO

    def body(dcs_ref, cc_ref, c1_ref, c2_ref, c1h_ref, c2h_ref, cw_ref, lg_ref, lb_ref, dp_in,
             dp_ref, dw_ref, ds_ref, cext, dext):
        del dp_in
        i = pl.program_id(0)
        ti = nt - 1 - i

        @pl.when(i == 0)
        def _():
            dext[TE:TE + HALO, :] = jnp.zeros((HALO, D), F32)
            dw_ref[...] = jnp.zeros_like(dw_ref)
            ds_ref[...] = jnp.zeros_like(ds_ref)

        @pl.when(i > 0)
        def _():
            dext[TE:TE + HALO, :] = dext[0:HALO, :]

        ch = c1h_ref[...].astype(F32) * _sigmoid(c2h_ref[...].astype(F32))
        cext[0:HALO, :] = jnp.where(ti > 0, ch, 0.0)

        def pre(r0):
            rows = pl.ds(r0, 32)
            cext[pl.ds(HALO + r0, 32), :] = c1_ref[rows, :].astype(F32) * _sigmoid(c2_ref[rows, :].astype(F32))
            x = cc_ref[rows, :]
            mu = jnp.mean(x, axis=-1, keepdims=True)
            xc = x - mu
            rstd = lax.rsqrt(jnp.mean(xc * xc, axis=-1, keepdims=True) + LN_EPS)
            xh = xc * rstd
            ln = xh * lg_ref[...] + lb_ref[...]
            sg = _sigmoid(ln)
            dln = dcs_ref[rows, :].astype(F32) * sg * (1.0 + ln * (1.0 - sg))
            ds_ref[0] += _rows8(dln * xh)
            ds_ref[1] += _rows8(dln)
            dxh = dln * lg_ref[...]
            dcc = rstd * (dxh - jnp.mean(dxh, axis=-1, keepdims=True) - xh * jnp.mean(dxh * xh, axis=-1, keepdims=True))
            dext[rows, :] = dcc
            ds_ref[2] += _rows8(dcc)
        _row_loop(TE, 32, pre, unroll=2)

        def convt(r0):
            rows = pl.ds(r0, CONV_ROWS)
            for part in range(D // CONV_LANES):
                cols = slice(part * CONV_LANES, (part + 1) * CONV_LANES)
                wd = dext[pl.ds(r0, CONV_ROWS + HALO), cols]
                dc = jnp.zeros((CONV_ROWS, CONV_LANES), F32)
                for s, taps in _shift_classes([(CONF_K - 1 - j, j) for j in range(CONF_K)]):
                    ws = _shifted(wd, s)
                    for a8, j in taps:
                        dc = dc + jnp.tile(cw_ref[j, :, cols], (CONV_ROWS // 8, 1)) * ws[a8:a8 + CONV_ROWS, :]
                dcc = wd[0:CONV_ROWS, :]
                wc = cext[pl.ds(r0, CONV_ROWS + HALO), cols]
                for s, taps in _shift_classes([(2 + j, j) for j in range(CONF_K)]):
                    ws = _shifted(wc, s)
                    for a8, j in taps:
                        dw_ref[j, :, cols] += _rows8(dcc * ws[a8:a8 + CONV_ROWS, :])
                c1 = c1_ref[rows, cols].astype(F32)
                s2 = _sigmoid(c2_ref[rows, cols].astype(F32))
                dp_ref[rows, cols] = (dc * s2).astype(BF16)
                dp_ref[rows, D + part * CONV_LANES:D + (part + 1) * CONV_LANES] = (dc * c1 * s2 * (1.0 - s2)).astype(BF16)
        _row_loop(TE, CONV_ROWS, convt)

    rev = lambda col: pl.BlockSpec((TE, D), lambda i: (nt - 1 - i, col))
    halo = lambda col: pl.BlockSpec((HALO, D), lambda i: (jnp.maximum((nt - 1 - i) * hb - 1, 0), col))
    return _carried_call(
        body, carry, name="conf_bwd", grid=(nt,),
        in_specs=[rev(0), rev(0), rev(2), rev(3), halo(2), halo(3), _resident((32, 8, D)), _resident((1, D)),
                  _resident((1, D)), pl.BlockSpec(memory_space=pl.ANY)],
        out_specs=[pl.BlockSpec((TE, 2 * D), lambda i: (nt - 1 - i, 1)), _resident((32, 8, D)), _resident((3, 8, D))],
        out_shape=[SDS((t, NPROJ), BF16), SDS((32, 8, D), F32), SDS((3, 8, D), F32)],
        scratch_shapes=[pltpu.VMEM((TE + HALO, D), F32), pltpu.VMEM((TE + HALO, D), F32)],
        input_output_aliases={9: 0}, compiler_params=_ARB1)(dcs, cc, proj, proj, proj, proj, cw, lg, lb, dproj)


def _gla_bwd(proj, alr, wau, balpha, do, sall, dproj, carry=()):
    t = proj.shape[0]
    nc = t // CH

    def body(qk_ref, v_ref, a_ref, wau_ref, ba_ref, do_ref, s_ref, dp_in, dp_ref, da_ref, dwau_ref, dba_ref, ds_scr, dla_scr):
        del dp_in

        @pl.when(pl.program_id(0) == 0)
        def _():
            ds_scr[...] = jnp.zeros_like(ds_scr)
            dwau_ref[...] = jnp.zeros_like(dwau_ref)
            dba_ref[...] = jnp.zeros_like(dba_ref)

        z, b, bmid, blast, causal = _gla_decay(a_ref, wau_ref, ba_ref)
        dlasts = []
        for h in range(HEADS):
            ks = slice(h * DKH, (h + 1) * DKH)
            vs = slice(h * DVH, (h + 1) * DVH)
            bh, mh, lh = b[:, ks], bmid[:, ks], blast[:, ks]
            q = qk_ref[:, ks].astype(F32) * (DKH ** -0.5)
            k = qk_ref[:, DK + h * DKH:DK + (h + 1) * DKH].astype(F32)
            v = v_ref[:, vs]
            dout = do_ref[:, vs]
            eq, ek, eb, eg, el = jnp.exp(bh - mh), jnp.exp(mh - bh), jnp.exp(bh), jnp.exp(lh - bh), jnp.exp(lh)
            qt, kt = (q * eq).astype(BF16), (k * ek).astype(BF16)
            qg, kg = (q * eb).astype(BF16), (k * eg).astype(BF16)
            st = s_ref[0, vs, :]
            dsn = ds_scr[vs, :]
            st16, dsn16 = st.astype(BF16), dsn.astype(BF16)
            a = jnp.where(causal, _dot_nt(qt, kt), 0.0).astype(BF16)
            da = jnp.where(causal, _dot_nt(dout, v), 0.0).astype(BF16)
            dq_inter = _dot(dout, st16) * eb
            dk_inter = _dot(v, dsn16) * eg
            dq = _dot(da, kt) * eq + dq_inter
            dk = _dot_tn(da, qt) * ek + dk_inter
            dv = _dot_tn(a, dout) + _dot_nt(kg, dsn16)
            dlasts.append(jnp.sum(k * dk_inter, axis=0, keepdims=True) + jnp.sum(st * dsn, axis=0, keepdims=True) * el[0:1, :])
            dla_scr[:, ks] = q * dq - k * dk
            ds_scr[vs, :] = dsn * jnp.concatenate([el, el], axis=0) + _dot_tn(dout, qg)
            dp_ref[:, ks] = (dq * (DKH ** -0.5)).astype(BF16)
            dp_ref[:, DK + h * DKH:DK + (h + 1) * DKH] = dk.astype(BF16)
            dp_ref[:, D + h * DVH:D + (h + 1) * DVH] = dv.astype(BF16)
        r = lax.broadcasted_iota(jnp.int32, (CH, CH), 0)
        c = lax.broadcasted_iota(jnp.int32, (CH, CH), 1)
        dla = _tri_matmul((r <= c).astype(BF16), dla_scr[...]) + jnp.concatenate(dlasts, axis=1)
        dz = (dla * (1.0 / TAU) * _sigmoid(-z)).astype(BF16)
        da_ref[...] = _dot_nt(dz, wau_ref[...]).astype(BF16)
        dwau_ref[...] += _dot_tn(a_ref[...], dz)
        dba_ref[...] += _rows8(dz.astype(F32))

    rev = lambda w, col: pl.BlockSpec((CH, w), lambda c: (nc - 1 - c, col))
    return _carried_call(
        body, carry, name="gla_bwd", grid=(nc,),
        in_specs=[rev(D, 0), rev(D, 1), rev(LANES, 0), _resident((LANES, DK)), _resident((1, DK)), rev(D, 0),
                  pl.BlockSpec((1, DV, DKH), lambda c: (nc - 1 - c, 0, 0)), pl.BlockSpec(memory_space=pl.ANY)],
        out_specs=[rev(2 * D, 0), rev(LANES, 0), _resident((LANES, DK)), _resident((8, DK))],
        out_shape=[SDS((t, NPROJ), BF16), SDS((t, LANES), BF16), SDS((LANES, DK), F32), SDS((8, DK), F32)],
        scratch_shapes=[pltpu.VMEM((DV, DKH), F32), pltpu.VMEM((CH, DK), F32)],
        input_output_aliases={7: 0}, compiler_params=_ARB1)(proj, proj, alr, wau, balpha, do, sall, dproj)


def _all_gather(xs, name, tokens, target, t):
    n = len(xs)
    s = tokens.shape[0]
    tail = t - N_META - s
    zeros = jnp.zeros((max(tail, N_META), D), F32)

    def body(*refs):
        x_refs, (tok_ref, tgt_ref, zero_ref), out_refs = refs[:n], refs[n:n + 3], refs[n + 3:2 * n + 3]
        h_ref, tg_ref = refs[2 * n + 3:2 * n + 5]
        send_sems, recv_sems, local_sems, fill_sems = refs[2 * n + 5:]
        body_rows, tail_rows = pl.ds(N_META, s), pl.ds(N_META + s, tail)
        fills = [pltpu.make_async_copy(tok_ref, h_ref.at[body_rows], fill_sems.at[0]),
                 pltpu.make_async_copy(tgt_ref, tg_ref.at[body_rows], fill_sems.at[1]),
                 pltpu.make_async_copy(zero_ref.at[pl.ds(0, tail)], h_ref.at[tail_rows], fill_sems.at[2]),
                 pltpu.make_async_copy(zero_ref.at[pl.ds(0, tail)], tg_ref.at[tail_rows], fill_sems.at[3]),
                 pltpu.make_async_copy(zero_ref.at[pl.ds(0, N_META)], tg_ref.at[pl.ds(0, N_META)], fill_sems.at[4])]
        for cp in fills:
            cp.start()
        x, y, c = _my_place()
        me, sibling = (x, y, c), (x, y, 1 - c)
        chips = [(1 - x, y), (x, 1 - y), (1 - x, 1 - y)]

        def slot(p, px, py, pc):
            return out_refs[p].at[4 * px + 2 * py + pc]

        def copy(p, k, block, to, src=None):
            return pltpu.make_async_remote_copy(
                src_ref=slot(p, *block) if src is None else src, dst_ref=slot(p, *block),
                send_sem=send_sems.at[7 * p + k], recv_sem=recv_sems.at[7 * p + k], device_id=to, device_id_type=MESH_T)

        mine = [pltpu.make_async_copy(x_refs[p], slot(p, *me), local_sems.at[p]) for p in range(n)]
        for cp in mine:
            cp.start()
        first = []
        for p in range(n):
            first.append(copy(p, 0, me, sibling, src=x_refs[p]))
            first += [copy(p, 1 + j, me, (*chip, c), src=x_refs[p]) for j, chip in enumerate(chips)]
        for cp in first:
            cp.start()
        passed = []
        for p in range(n):
            for j, chip in enumerate(chips):
                copy(p, 1 + j, (*chip, c), me).wait_recv()
                fwd = copy(p, 4 + j, (*chip, c), sibling)
                fwd.start()
                passed.append(fwd)
        for p in range(n):
            copy(p, 0, sibling, me).wait_recv()
            for j, chip in enumerate(chips):
                copy(p, 4 + j, (*chip, 1 - c), me).wait_recv()
        for cp in first + passed:
            cp.wait_send()
        for cp in mine + fills:
            cp.wait()

    hbm = pl.BlockSpec(memory_space=pl.ANY)
    return pl.pallas_call(
        body, name=name, out_shape=[SDS((N_DEV, *a.shape), a.dtype) for a in xs] + [SDS((t, D), F32)] * 2,
        in_specs=[hbm] * (n + 3), out_specs=[hbm] * (n + 2),
        scratch_shapes=[pltpu.SemaphoreType.DMA((7 * n,)), pltpu.SemaphoreType.DMA((7 * n,)),
                        pltpu.SemaphoreType.DMA((n,)), pltpu.SemaphoreType.DMA((5,))])(*xs, tokens, target, zeros)


def _exchange(gs):
    n = len(gs)

    def body(*refs):
        g_refs, land_refs = refs[:n], refs[n:2 * n]
        send_sems, recv_sems, local_sems = refs[2 * n:]
        x, y, c = _my_place()
        my_idx = 4 * x + 2 * y + c
        mine = [pltpu.make_async_copy(g_refs[p].at[my_idx], land_refs[p].at[my_idx], local_sems.at[p]) for p in range(n)]
        for cp in mine:
            cp.start()
        copies = []
        for k in range(1, N_DEV):
            px, py, pc = _flip(x, k & 4), _flip(y, k & 2), _flip(c, k & 1)
            p_idx = 4 * px + 2 * py + pc
            for p in range(n):
                s = 7 * p + k - 1
                cp = pltpu.make_async_remote_copy(
                    src_ref=g_refs[p].at[p_idx], dst_ref=land_refs[p].at[my_idx], send_sem=send_sems.at[s],
                    recv_sem=recv_sems.at[s], device_id=(px, py, pc), device_id_type=MESH_T)
                cp.start()
                arrival = pltpu.make_async_remote_copy(
                    src_ref=g_refs[p].at[p_idx], dst_ref=land_refs[p].at[p_idx], send_sem=send_sems.at[s],
                    recv_sem=recv_sems.at[s], device_id=(px, py, pc), device_id_type=MESH_T)
                copies.append((cp, arrival))
        for cp, arrival in copies:
            arrival.wait_recv()
        for cp, arrival in copies:
            cp.wait_send()
        for cp in mine:
            cp.wait()

    hbm = pl.BlockSpec(memory_space=pl.ANY)
    return pl.pallas_call(
        body, name="grad_exchange", out_shape=[SDS(g.shape, g.dtype) for g in gs],
        in_specs=[hbm] * n, out_specs=[hbm] * n,
        scratch_shapes=[pltpu.SemaphoreType.DMA((7 * n,)), pltpu.SemaphoreType.DMA((7 * n,)),
                        pltpu.SemaphoreType.DMA((n,))])(*gs)


def _adamw(land, w, m, v, rows_blk, name):
    rows = w.shape[0]

    def body(l_ref, w_ref, m_ref, v_ref, g_ref, d_ref, nm_ref, nv_ref):
        g = l_ref[0].astype(F32)
        for s in range(1, N_DEV):
            g = g + l_ref[s].astype(F32)
        nm = ADAM_B1 * m_ref[...] + (1.0 - ADAM_B1) * g
        nv = ADAM_B2 * v_ref[...] + (1.0 - ADAM_B2) * (g * g)
        m_hat = nm / (1.0 - ADAM_B1 ** ADAM_STEP)
        v_hat = nv / (1.0 - ADAM_B2 ** ADAM_STEP)
        g_ref[...] = g
        d_ref[...] = -ADAM_LR * (m_hat / (jnp.sqrt(v_hat) + ADAM_EPS) + ADAM_WD * w_ref[...])
        nm_ref[...] = nm
        nv_ref[...] = nv

    blk = pl.BlockSpec((rows_blk, D), lambda i: (i, 0))
    return pl.pallas_call(
        body, name=name, grid=(rows // rows_blk,),
        in_specs=[pl.BlockSpec((N_DEV, rows_blk, D), lambda i: (0, i, 0)), blk, blk, blk],
        out_specs=[blk] * 4, out_shape=[SDS((rows, D), F32)] * 4, compiler_params=_ARB1)(land, w, m, v)


BIG = ("w_in", "w_up", "w_down", "w_gla_o", "w_conf_o", "w_out")
BIG_TRANSPOSED = ("w_in", "w_up")
SMALL_SHARDED = ("meta_tokens", "conf_dw_w", "ffn_dw_w", "w_alpha_up")
REPLICATED = ("norm_mix_g", "b_alpha", "gla_norm_g", "conf_dw_b", "conf_ln_g", "conf_ln_b", "norm_ffn_g", "ffn_dw_b",
              "final_norm_g")
N_IN = sum(IN_WIDTHS)
W_IN_ROWS = N_IN // N_DEV
W_IN_PAD = -(-W_IN_ROWS // 16) * 16
ADAM_BLOCK = {"w_in": W_IN_PAD // 3, "w_up": 176, "w_down": 176, "w_gla_o": 128, "w_conf_o": 128, "w_out": 128}
SMALL_ROWS = 32


def _to_panel(name, shard):
    a = shard.reshape(shard.shape[-2], shard.shape[-1])
    if name in BIG_TRANSPOSED:
        a = a.T
    if name == "w_in":
        a = jnp.pad(a, ((0, W_IN_PAD - W_IN_ROWS), (0, 0)))
    return a


def _from_panel(name, panel, shape):
    a = panel[0:W_IN_ROWS] if name == "w_in" else panel
    if name in BIG_TRANSPOSED:
        a = a.T
    return a.reshape(shape)


def _pack_small(arrs):
    flat = jnp.concatenate([jnp.pad(a.reshape(-1), (0, (-a.size) % D)) for a in arrs])
    return jnp.pad(flat, (0, SMALL_ROWS * D - flat.shape[0])).reshape(SMALL_ROWS, D)


def _unpack_small(panel, shapes):
    flat, out, off = panel.reshape(-1), [], 0
    for shp in shapes:
        n = 1
        for s in shp:
            n *= s
        out.append(flat[off:off + n].reshape(shp))
        off += n + (-n) % D
    return out


def _local_step(x, target, w, shards=None, padded=None):
    dist = shards is not None
    w = dict(w)

    def gather(names):
        return [(shards[n], False) for n in names] if dist else []

    def scatter(*arrs):
        return [(a.reshape(N_DEV, -1, D), True) for a in arrs] if dist else []

    s = x.shape[0]
    n_real = s + N_META
    t = -(-n_real // TM) * TM
    if padded is None:
        h0 = jnp.concatenate([w["meta_tokens"], x, jnp.zeros((t - n_real, D), F32)], axis=0)
        tgt = jnp.concatenate([jnp.zeros((N_META, D), F32), target, jnp.zeros((t - n_real, D), F32)], axis=0)
    else:
        h0, tgt = lax.dynamic_update_slice(padded[0], w["meta_tokens"], (0, 0)), padded[1]

    q0, r0, a0, c0 = 0, 2 * DK + DV, 2 * DK + 2 * DV, 2 * DK + 2 * DV + RANK
    wt = w["w_in_t"]
    w_main = jnp.concatenate([wt[q0:r0], wt[c0:N_IN], wt[r0:a0]], axis=0)
    w_a = jnp.pad(wt[a0:c0], ((0, LANES - RANK), (0, 0)))
    wau = jnp.pad(w["w_alpha_up"].astype(BF16), ((0, LANES - RANK), (0, 0)))
    row = lambda name: w[name].reshape(1, -1)
    cw = jnp.broadcast_to(jnp.pad(w["conf_dw_w"], ((0, 32 - CONF_K), (0, 0)))[:, None, :], (32, 8, D))
    fw = jnp.broadcast_to(jnp.concatenate([w["ffn_dw_w"], w["ffn_dw_b"].reshape(1, -1)], axis=0)[:, None, :],
                          (FFN_K + 1, 16, DFF))

    early = ("w_gla_o", "w_conf_o", "w_out", "w_up")
    u1, proj, alr, *landed = _norm_matmul(h0, row("norm_mix_g"), w_main, 1024, "in_proj", w_extra_t=w_a, carry=gather(early))
    for n, land in zip(early, landed):
        w["w_up_t" if n == "w_up" else n] = land.reshape(-1, D)
    o, og, sall = _gla_fwd(proj, alr, wau, row("b_alpha"), row("gla_norm_g"))
    cc, cs, *landed = _conf_fwd(proj, cw, row("conf_dw_b"), row("conf_ln_g"), row("conf_ln_b"), carry=gather(("w_down",)))
    if dist:
        w["w_down"] = landed[0].reshape(-1, D)
    brg, brc, merged, h1 = _mix_fwd(og, cs, proj, h0, w["w_gla_o"], w["w_conf_o"], w["w_out"])
    u2, up = _norm_matmul(h1, row("norm_ffn_g"), w["w_up_t"], 512, "up_proj")
    f, dh2, red = _ffn_out(up, fw, h1, w["w_down"], row("final_norm_g"), tgt, n_real)
    loss = 0.5 / D * jnp.sum(red[0:8])

    g = {"final_norm_g": jnp.sum(red[8:16], axis=0)}
    dup, dfw = _ffn_bwd(dh2, w["w_down"], up, fw)
    g["ffn_dw_w"] = jnp.sum(dfw[0:FFN_K], axis=1)
    g["ffn_dw_b"] = jnp.sum(dfw[3], axis=0)
    g["w_down"] = _wgrad(f, dh2, 1408, "wgrad_down")
    dh1, dg2, *landed = _dgrad_norm(dup, w["w_up_t"], h1, row("norm_ffn_g"), dh2, "up_dgrad", carry=scatter(g["w_down"]))
    if dist:
        g["w_down"] = landed[0]
    g["norm_ffn_g"] = jnp.sum(dg2, axis=0)
    g["w_up_t"] = _wgrad(dup, u2, 1408, "wgrad_up")
    dbrg, dbrc, dog, dcs, dproj = _mix_bwd(dh1, w["w_out"], w["w_gla_o"], w["w_conf_o"], proj, brg, brc)
    g["w_out"] = _wgrad(merged, dh1, 1024, "wgrad_out")
    g["w_gla_o"] = _wgrad(og, dbrg, 1024, "wgrad_gla_o")
    g["w_conf_o"] = _wgrad(cs, dbrc, 1024, "wgrad_conf_o")
    do, dproj, dgn = _glapost_bwd(dog, o, proj, row("gla_norm_g"), dproj)
    g["gla_norm_g"] = jnp.sum(dgn, axis=0)
    dproj, dcw, dst, *landed = _conf_bwd(dcs, cc, proj, cw, row("conf_ln_g"), row("conf_ln_b"), dproj,
                                         carry=scatter(g["w_up_t"]))
    if dist:
        g["w_up_t"] = landed[0]
    g["conf_dw_w"] = jnp.sum(dcw[0:CONF_K], axis=1)
    g["conf_ln_g"], g["conf_ln_b"], g["conf_dw_b"] = jnp.sum(dst[0], axis=0), jnp.sum(dst[1], axis=0), jnp.sum(dst[2], axis=0)
    dproj, dalr, dwau, dba, *landed = _gla_bwd(proj, alr, wau, row("b_alpha"), do, sall, dproj,
                                               carry=scatter(g["w_out"], g["w_gla_o"], g["w_conf_o"]))
    if dist:
        g["w_out"], g["w_gla_o"], g["w_conf_o"] = landed
    g["w_alpha_up"] = dwau[0:RANK]
    g["b_alpha"] = jnp.sum(dba, axis=0)
    dw_main = _wgrad(dproj, u1, 1024, "wgrad_in")
    dw_a = _wgrad(dalr, u1, LANES, "wgrad_alr")
    g["w_in_t"] = jnp.concatenate([dw_main[0:r0], dw_main[NPROJ - DV:NPROJ], dw_a[0:RANK], dw_main[r0:NPROJ - DV]], axis=0)
    w_in_blocks = []
    if dist:
        pad = ((0, 0), (0, W_IN_PAD - W_IN_ROWS), (0, 0))
        w_in_blocks = [(jnp.pad(g["w_in_t"].reshape(N_DEV, W_IN_ROWS, D), pad), True)]
    dh0, dg1, *landed = _dgrad_norm(dproj, w_main, h0, row("norm_mix_g"), dh1, "in_dgrad", dy_extra=dalr,
                                    w_extra_t=w_a, carry=w_in_blocks)
    if dist:
        g["w_in_t"] = landed[0]
    g["norm_mix_g"] = jnp.sum(dg1, axis=0)
    g["meta_tokens"] = dh0[0:N_META]
    return loss, dh0[N_META:n_real], g


def kernel(x, meta_tokens, norm_mix_g, w_in, w_alpha_up, b_alpha, gla_norm_g, w_gla_o, conf_dw_w, conf_dw_b, conf_ln_g, conf_ln_b, w_conf_o, w_out, norm_ffn_g, w_up, ffn_dw_w, ffn_dw_b, w_down, final_norm_g, loss_target, m_meta_tokens, m_norm_mix_g, m_w_in, m_w_alpha_up, m_b_alpha, m_gla_norm_g, m_w_gla_o, m_conf_dw_w, m_conf_dw_b, m_conf_ln_g, m_conf_ln_b, m_w_conf_o, m_w_out, m_norm_ffn_g, m_w_up, m_ffn_dw_w, m_ffn_dw_b, m_w_down, m_final_norm_g, v_meta_tokens, v_norm_mix_g, v_w_in, v_w_alpha_up, v_b_alpha, v_gla_norm_g, v_w_gla_o, v_conf_dw_w, v_conf_dw_b, v_conf_ln_g, v_conf_ln_b, v_w_conf_o, v_w_out, v_norm_ffn_g, v_w_up, v_ffn_dw_w, v_ffn_dw_b, v_w_down, v_final_norm_g):
    ws = dict(meta_tokens=meta_tokens, norm_mix_g=norm_mix_g, w_in=w_in, w_alpha_up=w_alpha_up, b_alpha=b_alpha,
              gla_norm_g=gla_norm_g, w_gla_o=w_gla_o, conf_dw_w=conf_dw_w, conf_dw_b=conf_dw_b, conf_ln_g=conf_ln_g,
              conf_ln_b=conf_ln_b, w_conf_o=w_conf_o, w_out=w_out, norm_ffn_g=norm_ffn_g, w_up=w_up, ffn_dw_w=ffn_dw_w,
              ffn_dw_b=ffn_dw_b, w_down=w_down, final_norm_g=final_norm_g)
    ms = dict(meta_tokens=m_meta_tokens, norm_mix_g=m_norm_mix_g, w_in=m_w_in, w_alpha_up=m_w_alpha_up, b_alpha=m_b_alpha,
              gla_norm_g=m_gla_norm_g, w_gla_o=m_w_gla_o, conf_dw_w=m_conf_dw_w, conf_dw_b=m_conf_dw_b,
              conf_ln_g=m_conf_ln_g, conf_ln_b=m_conf_ln_b, w_conf_o=m_w_conf_o, w_out=m_w_out, norm_ffn_g=m_norm_ffn_g,
              w_up=m_w_up, ffn_dw_w=m_ffn_dw_w, ffn_dw_b=m_ffn_dw_b, w_down=m_w_down, final_norm_g=m_final_norm_g)
    vs = dict(meta_tokens=v_meta_tokens, norm_mix_g=v_norm_mix_g, w_in=v_w_in, w_alpha_up=v_w_alpha_up, b_alpha=v_b_alpha,
              gla_norm_g=v_gla_norm_g, w_gla_o=v_w_gla_o, conf_dw_w=v_conf_dw_w, conf_dw_b=v_conf_dw_b,
              conf_ln_g=v_conf_ln_g, conf_ln_b=v_conf_ln_b, w_conf_o=v_w_conf_o, w_out=v_w_out, norm_ffn_g=v_norm_ffn_g,
              w_up=v_w_up, ffn_dw_w=v_ffn_dw_w, ffn_dw_b=v_ffn_dw_b, w_down=v_w_down, final_norm_g=v_final_norm_g)
    small = SMALL_SHARDED + REPLICATED
    pack_small = lambda d: _pack_small([d[n] for n in small])

    shards = {n: _to_panel(n, ws[n]).astype(BF16) for n in BIG}
    t = -(-(x.shape[1] + N_META) // TM) * TM
    gathered = _all_gather([shards["w_in"], pack_small(ws)], "weight_gather", x[0], loss_target[0], t)
    full = {n: ws[n].reshape(-1) for n in REPLICATED}
    full["w_in_t"] = gathered[0][:, 0:W_IN_ROWS].reshape(N_IN, D)
    flat, off = gathered[1].reshape(N_DEV, -1), 0
    for n in SMALL_SHARDED:
        k, c = ws[n].shape[-2], ws[n].shape[-1]
        full[n] = flat[:, off:off + k * c].reshape(N_DEV, k, c).transpose(1, 0, 2).reshape(k, N_DEV * c)
        off += k * c + (-(k * c)) % D

    loss, grad_x, g = _local_step(x[0], loss_target[0], full, shards, padded=gathered[2:4])

    lands = [g["w_in_t"], g["w_up_t"]] + [g[n] for n in BIG[2:]]
    blocks = []
    for n in SMALL_SHARDED:
        k, c = ws[n].shape[-2], ws[n].shape[-1]
        b = g[n].reshape(k, N_DEV, c).transpose(1, 0, 2).reshape(N_DEV, k * c)
        blocks.append(jnp.pad(b, ((0, 0), (0, (-(k * c)) % D))))
    for n in REPLICATED:
        b = jnp.broadcast_to(g[n].reshape(1, -1), (N_DEV, g[n].size))
        blocks.append(jnp.pad(b, ((0, 0), (0, (-b.shape[1]) % D))))
    gsm = jnp.concatenate(blocks, axis=1)
    lands += _exchange([jnp.pad(gsm, ((0, 0), (0, SMALL_ROWS * D - gsm.shape[1]))).reshape(N_DEV, SMALL_ROWS, D)])

    grad, delta, new_m, new_v = {}, {}, {}, {}
    for i, n in enumerate(BIG):
        outs = _adamw(lands[i], _to_panel(n, ws[n]), _to_panel(n, ms[n]), _to_panel(n, vs[n]), ADAM_BLOCK[n], "adamw_" + n)
        grad[n], delta[n], new_m[n], new_v[n] = [_from_panel(n, p, ws[n].shape) for p in outs]
    outs = _adamw(lands[len(BIG)], pack_small(ws), pack_small(ms), pack_small(vs), SMALL_ROWS, "adamw_small")
    shapes = [ws[n].shape for n in small]
    for d, p in zip((grad, delta, new_m, new_v), outs):
        d.update(zip(small, _unpack_small(p, shapes)))

    order = ("meta_tokens", "norm_mix_g", "w_in", "w_alpha_up", "b_alpha", "gla_norm_g", "w_gla_o", "conf_dw_w", "conf_dw_b",
             "conf_ln_g", "conf_ln_b", "w_conf_o", "w_out", "norm_ffn_g", "w_up", "ffn_dw_w", "ffn_dw_b", "w_down",
             "final_norm_g")
    loss = lax.psum(loss, ("x", "y", "c"))
    return (loss, grad_x[None], *[grad[n] for n in order], *[delta[n] for n in order], *[new_m[n] for n in order],
            *[new_v[n] for n in order])
```

```python
import functools

import jax
import jax.numpy as jnp
from jax import lax
from jax.experimental import pallas as pl
from jax.experimental.pallas import tpu as pltpu

F32, BF16 = jnp.float32, jnp.bfloat16
SDS = jax.ShapeDtypeStruct

D = 1024
N_META = 16
HEADS = 4
DK, DKH, DV, DVH = 512, 128, 1024, 256
RANK = 16
TAU = 16.0
CONF_K = 31
DFF = 2816
FFN_K = 3
IN_WIDTHS = (DK, DK, DV, DV, RANK, 2 * D, D, D)
RMS_EPS, LN_EPS = 1e-6, 1e-5
ADAM_LR, ADAM_B1, ADAM_B2, ADAM_EPS, ADAM_WD, ADAM_STEP = 0.001, 0.9, 0.999, 1e-08, 0.01, 10

NPROJ = 7 * D
LANES = 128
CH = 128
TM = 640
TM_BIG = 1664
TE = 320
HALO = 32
HALO_F = 16
N_DEV = 8
VMEM_LIMIT = 60 * 1024 * 1024
MESH_T = pl.DeviceIdType.MESH

_ARB1 = pltpu.CompilerParams(dimension_semantics=("arbitrary",), vmem_limit_bytes=VMEM_LIMIT)
_ARB2 = pltpu.CompilerParams(dimension_semantics=("arbitrary", "arbitrary"), vmem_limit_bytes=VMEM_LIMIT)


def _dot(a, b):
    return jnp.dot(a, b, preferred_element_type=F32)


def _dot_nt(a, b):
    return lax.dot_general(a, b, (((1,), (1,)), ((), ())), preferred_element_type=F32)


def _dot_tn(a, b):
    return lax.dot_general(a, b, (((0,), (0,)), ((), ())), preferred_element_type=F32)


def _sigmoid(x):
    return 0.5 * jnp.tanh(0.5 * x) + 0.5


def _rows8(x):
    return x.reshape(x.shape[0] // 8, 8, x.shape[1]).sum(axis=0)


def _row_tile(t, preferred):
    return preferred if t % preferred == 0 else TM


def _row_loop(n_rows, rb, fn, unroll=1):
    def step(i, carry):
        fn(pl.multiple_of(i * rb, rb))
        return carry
    lax.fori_loop(0, n_rows // rb, step, 0, unroll=unroll)


def _resident(shape):
    return pl.BlockSpec(shape, lambda *_: (0,) * len(shape))


def _once(shape):
    return pl.BlockSpec(shape, lambda *_: (0,) * len(shape), pipeline_mode=pl.Buffered(1))


CONV_ROWS, CONV_LANES = 64, 256


def _shift_classes(offset_taps):
    return [(s, [(o - s, j) for o, j in offset_taps if o % 8 == s]) for s in range(8)]


def _shifted(win, s):
    return win if s == 0 else win[s:s + CONV_ROWS + HALO - 8, :]


def _my_place():
    return lax.axis_index("x"), lax.axis_index("y"), lax.axis_index("c")


def _flip(v, bit):
    return 1 - v if bit else v


def _exchange_copies(src_refs, land_refs, scatter, send_sems, recv_sems, local_sems, arrivals):
    x, y, c = _my_place()
    my_idx = 4 * x + 2 * y + c
    local, remote = [], []
    for p, (src, land) in enumerate(zip(src_refs, land_refs)):
        local.append(pltpu.make_async_copy(src.at[my_idx] if scatter[p] else src, land.at[my_idx], local_sems.at[p]))
    for k in range(1, N_DEV):
        px, py, pc = _flip(x, k & 4), _flip(y, k & 2), _flip(c, k & 1)
        p_idx = 4 * px + 2 * py + pc
        for p, (src, land) in enumerate(zip(src_refs, land_refs)):
            s = 7 * p + k - 1
            out = src.at[p_idx] if scatter[p] else src

            def copy(dst):
                return pltpu.make_async_remote_copy(src_ref=out, dst_ref=dst, send_sem=send_sems.at[s],
                                                    recv_sem=recv_sems.at[s], device_id=(px, py, pc), device_id_type=MESH_T)
            remote.append((copy(land.at[my_idx]), copy(land.at[p_idx]) if arrivals else None))
    return local, remote


def _carried_call(core, carry, *, grid, in_specs, out_specs, out_shape, scratch_shapes=(), **kw):
    n_in, n_out, nc, n_scr = len(in_specs), len(out_specs), len(carry), len(scratch_shapes)
    if nc == 0:
        return pl.pallas_call(core, grid=grid, in_specs=in_specs, out_specs=out_specs, out_shape=out_shape,
                              scratch_shapes=list(scratch_shapes), **kw)
    scatter = [sc for _, sc in carry]

    def body(*refs):
        ins, cin = refs[:n_in], refs[n_in:n_in + nc]
        outs, cout = refs[n_in + nc:n_in + nc + n_out], refs[n_in + nc + n_out:n_in + 2 * nc + n_out]
        scr, sems = refs[n_in + 2 * nc + n_out:n_in + 2 * nc + n_out + n_scr], refs[-3:]
        first = functools.reduce(jnp.logical_and, [pl.program_id(a) == 0 for a in range(len(grid))])
        last = functools.reduce(jnp.logical_and, [pl.program_id(a) == grid[a] - 1 for a in range(len(grid))])

        @pl.when(first)
        def _():
            local, remote = _exchange_copies(cin, cout, scatter, *sems, arrivals=False)
            for cp in local:
                cp.start()
            for send, _ in remote:
                send.start()

        core(*ins, *outs, *scr)

        @pl.when(last)
        def _():
            local, remote = _exchange_copies(cin, cout, scatter, *sems, arrivals=True)
            for _, arrival in remote:
                arrival.wait_recv()
            for send, _ in remote:
                send.wait_send()
            for cp in local:
                cp.wait()

    hbm = pl.BlockSpec(memory_space=pl.ANY)
    land_shape = [SDS((N_DEV, *(a.shape[1:] if sc else a.shape)), a.dtype) for a, sc in carry]
    sems = [pltpu.SemaphoreType.DMA((7 * nc,)), pltpu.SemaphoreType.DMA((7 * nc,)), pltpu.SemaphoreType.DMA((nc,))]
    call = pl.pallas_call(body, grid=grid, in_specs=list(in_specs) + [hbm] * nc, out_specs=list(out_specs) + [hbm] * nc,
                          out_shape=list(out_shape) + land_shape, scratch_shapes=list(scratch_shapes) + sems, **kw)
    return lambda *args: call(*args, *[a for a, _ in carry])


def _norm_matmul(h, g, w_t, tn, name, w_extra_t=None, carry=()):
    t, n = h.shape[0], w_t.shape[0]
    tm = _row_tile(t, TM_BIG)
    nt, nb = t // tm, n // tn

    def body(*refs):
        if w_extra_t is None:
            h_ref, g_ref, w_ref, u_ref, p_ref = refs
        else:
            h_ref, g_ref, w_ref, we_ref, u_ref, p_ref, e_ref = refs

        @pl.when(pl.program_id(1) == 0)
        def _():
            def blk(r0):
                x = h_ref[pl.ds(r0, 32), :]
                rinv = lax.rsqrt(jnp.mean(x * x, axis=-1, keepdims=True) + RMS_EPS)
                u_ref[pl.ds(r0, 32), :] = (x * rinv * g_ref[...]).astype(BF16)
            _row_loop(tm, 32, blk, unroll=2)
            if w_extra_t is not None:
                e_ref[...] = _dot_nt(u_ref[...], we_ref[...]).astype(BF16)

        p_ref[...] = _dot_nt(u_ref[...], w_ref[...]).astype(BF16)

    in_specs = [pl.BlockSpec((tm, D), lambda i, j: (i, 0)), _resident((1, D)), pl.BlockSpec((tn, D), lambda i, j: (j, 0))]
    out_specs = [pl.BlockSpec((tm, D), lambda i, j: (i, 0)), pl.BlockSpec((tm, tn), lambda i, j: (i, j))]
    out_shape = [SDS((t, D), BF16), SDS((t, n), BF16)]
    args = [h, g, w_t]
    if w_extra_t is not None:
        in_specs.append(_resident(w_extra_t.shape))
        out_specs.append(pl.BlockSpec((tm, w_extra_t.shape[0]), lambda i, j: (i, 0)))
        out_shape.append(SDS((t, w_extra_t.shape[0]), BF16))
        args.append(w_extra_t)
    return _carried_call(body, carry, name=name, grid=(nt, nb), in_specs=in_specs, out_specs=out_specs,
                         out_shape=out_shape, compiler_params=_ARB2)(*args)


def _gla_decay(a_ref, wau_ref, ba_ref):
    z = _dot(a_ref[...], wau_ref[...]) + ba_ref[...]
    la = (jnp.minimum(z, 0.0) - jnp.log(1.0 + jnp.exp(-jnp.abs(z)))) * (1.0 / TAU)
    r = lax.broadcasted_iota(jnp.int32, (CH, CH), 0)
    c = lax.broadcasted_iota(jnp.int32, (CH, CH), 1)
    b = _tri_matmul((r >= c).astype(BF16), la)
    mid = jnp.broadcast_to(b[CH // 2:CH // 2 + 1, :], b.shape)
    last = jnp.broadcast_to(b[CH - 1:CH, :], b.shape)
    return z, b, mid, last, r >= c


def _tri_matmul(tri, x):
    n = x.shape[1]
    x1 = x.astype(BF16)
    r1 = x - x1.astype(F32)
    x2 = r1.astype(BF16)
    x3 = (r1 - x2.astype(F32)).astype(BF16)
    y = _dot(tri, jnp.concatenate([x1, x2, x3], axis=1))
    return y[:, 0:n] + y[:, n:2 * n] + y[:, 2 * n:3 * n]


def _gla_fwd(proj, alr, wau, balpha, gn):
    t = proj.shape[0]
    nc = t // CH

    def body(qk_ref, v_ref, r_ref, a_ref, wau_ref, ba_ref, gn_ref, o_ref, og_ref, sall_ref, s_scr):
        @pl.when(pl.program_id(0) == 0)
        def _():
            s_scr[...] = jnp.zeros_like(s_scr)

        sall_ref[0] = s_scr[...]
        _, b, bmid, blast, causal = _gla_decay(a_ref, wau_ref, ba_ref)
        for h in range(HEADS):
            ks = slice(h * DKH, (h + 1) * DKH)
            vs = slice(h * DVH, (h + 1) * DVH)
            bh, mh, lh = b[:, ks], bmid[:, ks], blast[:, ks]
            q = qk_ref[:, ks].astype(F32) * (DKH ** -0.5)
            k = qk_ref[:, DK + h * DKH:DK + (h + 1) * DKH].astype(F32)
            v = v_ref[:, vs]
            qt = (q * jnp.exp(bh - mh)).astype(BF16)
            kt = (k * jnp.exp(mh - bh)).astype(BF16)
            qg = (q * jnp.exp(bh)).astype(BF16)
            kg = (k * jnp.exp(lh - bh)).astype(BF16)
            a = jnp.where(causal, _dot_nt(qt, kt), 0.0)
            st = s_scr[vs, :]
            o = _dot(a.astype(BF16), v) + _dot_nt(qg, st.astype(BF16))
            el = jnp.exp(lh)
            s_scr[vs, :] = st * jnp.concatenate([el, el], axis=0) + _dot_tn(v, kg)
            o_ref[:, vs] = o
            on = o * lax.rsqrt(jnp.mean(o * o, axis=-1, keepdims=True) + RMS_EPS) * gn_ref[:, vs]
            rr = r_ref[:, vs].astype(F32)
            og_ref[:, vs] = (on * (rr * _sigmoid(rr))).astype(BF16)

    return pl.pallas_call(
        body, name="gla_fwd", grid=(nc,),
        in_specs=[pl.BlockSpec((CH, D), lambda c: (c, 0)), pl.BlockSpec((CH, D), lambda c: (c, 1)),
                  pl.BlockSpec((CH, D), lambda c: (c, 6)), pl.BlockSpec((CH, LANES), lambda c: (c, 0)),
                  _resident((LANES, DK)), _resident((1, DK)), _resident((1, DV))],
        out_specs=[pl.BlockSpec((CH, DV), lambda c: (c, 0)), pl.BlockSpec((CH, DV), lambda c: (c, 0)),
                   pl.BlockSpec((1, DV, DKH), lambda c: (c, 0, 0))],
        out_shape=[SDS((t, DV), F32), SDS((t, DV), BF16), SDS((nc, DV, DKH), F32)],
        scratch_shapes=[pltpu.VMEM((DV, DKH), F32)], compiler_params=_ARB1)(proj, proj, proj, alr, wau, balpha, gn)


def _conf_fwd(proj, cw, cb, lg, lb, carry=()):
    t = proj.shape[0]
    nt = t // TE

    def body(c1_ref, c2_ref, cw_ref, cb_ref, lg_ref, lb_ref, cc_ref, cs_ref, cext):
        i = pl.program_id(0)

        @pl.when(i == 0)
        def _():
            cext[0:HALO, :] = jnp.zeros((HALO, D), F32)

        @pl.when(i > 0)
        def _():
            cext[0:HALO, :] = cext[TE:TE + HALO, :]

        def glu(r0):
            c2 = c2_ref[pl.ds(r0, 32), :].astype(F32)
            cext[pl.ds(HALO + r0, 32), :] = c1_ref[pl.ds(r0, 32), :].astype(F32) * _sigmoid(c2)
        _row_loop(TE, 32, glu)

        def conv(r0):
            for part in range(D // CONV_LANES):
                cols = slice(part * CONV_LANES, (part + 1) * CONV_LANES)
                win = cext[pl.ds(r0, CONV_ROWS + HALO), cols]
                acc = jnp.zeros((CONV_ROWS, CONV_LANES), F32) + cb_ref[:, cols]
                for s, taps in _shift_classes([(2 + j, j) for j in range(CONF_K)]):
                    ws = _shifted(win, s)
                    for a8, j in taps:
                        acc = acc + jnp.tile(cw_ref[j, :, cols], (CONV_ROWS // 8, 1)) * ws[a8:a8 + CONV_ROWS, :]
                cc_ref[pl.ds(r0, CONV_ROWS), cols] = acc
            for sub in range(CONV_ROWS // 32):
                rows = pl.ds(r0 + 32 * sub, 32)
                x = cc_ref[rows, :]
                xc = x - jnp.mean(x, axis=-1, keepdims=True)
                var = jnp.mean(xc * xc, axis=-1, keepdims=True)
                ln = xc * lax.rsqrt(var + LN_EPS) * lg_ref[...] + lb_ref[...]
                cs_ref[rows, :] = (ln * _sigmoid(ln)).astype(BF16)
        _row_loop(TE, CONV_ROWS, conv)

    return _carried_call(
        body, carry, name="conf_fwd", grid=(nt,),
        in_specs=[pl.BlockSpec((TE, D), lambda i: (i, 2)), pl.BlockSpec((TE, D), lambda i: (i, 3)),
                  _resident((32, 8, D)), _resident((1, D)), _resident((1, D)), _resident((1, D))],
        out_specs=[pl.BlockSpec((TE, D), lambda i: (i, 0)), pl.BlockSpec((TE, D), lambda i: (i, 0))],
        out_shape=[SDS((t, D), F32), SDS((t, D), BF16)],
        scratch_shapes=[pltpu.VMEM((TE + HALO, D), F32)], compiler_params=_ARB1)(proj, proj, cw, cb, lg, lb)


def _mix_fwd(og, cs, proj, h0, wg, wc, wo):
    t = h0.shape[0]
    nt = t // TM

    def body(og_ref, cs_ref, g_ref, h0_ref, wg_ref, wc_ref, wo_ref, brg_ref, brc_ref, mg_ref, h1_ref):
        brg_ref[...] = _dot(og_ref[...], wg_ref[...]).astype(BF16)
        brc_ref[...] = _dot(cs_ref[...], wc_ref[...]).astype(BF16)

        def blk(r0):
            rows = pl.ds(r0, 32)
            gg = g_ref[rows, 0:D].astype(F32)
            gc = g_ref[rows, D:2 * D].astype(F32)
            m = _sigmoid(gg) * brg_ref[rows, :].astype(F32) + _sigmoid(gc) * brc_ref[rows, :].astype(F32)
            mg_ref[rows, :] = m.astype(BF16)
        _row_loop(TM, 32, blk)
        h1_ref[...] = h0_ref[...] + _dot(mg_ref[...], wo_ref[...])

    row = lambda w: pl.BlockSpec((TM, w), lambda i: (i, 0))
    return pl.pallas_call(
        body, name="mix_fwd", grid=(nt,),
        in_specs=[row(D), row(D), pl.BlockSpec((TM, 2 * D), lambda i: (i, 2)), row(D),
                  _once((D, D)), _once((D, D)), _once((D, D))],
        out_specs=[row(D), row(D), row(D), row(D)],
        out_shape=[SDS((t, D), BF16), SDS((t, D), BF16), SDS((t, D), BF16), SDS((t, D), F32)],
        compiler_params=_ARB1)(og, cs, proj, h0, wg, wc, wo)


def _ffn_out(up, fw, h1, wd, gf, tgt, n_real):
    t = h1.shape[0]
    nt = t // TE

    def body(a_ref, bv_ref, fw_ref, h1_ref, wd_ref, gf_ref, tg_ref, f_ref, dh2_ref, red_ref, aext, hs):
        i = pl.program_id(0)

        @pl.when(i == 0)
        def _():
            aext[0:HALO_F, :] = jnp.zeros((HALO_F, DFF), F32)
            red_ref[...] = jnp.zeros_like(red_ref)

        @pl.when(i > 0)
        def _():
            aext[0:HALO_F, :] = aext[TE:TE + HALO_F, :]

        def cp(r0):
            aext[pl.ds(HALO_F + r0, 16), :] = a_ref[pl.ds(r0, 16), :].astype(F32)
        _row_loop(TE, 16, cp)

        def conv(r0):
            win = aext[pl.ds(r0, 32), :]
            ac = fw_ref[3] + fw_ref[0] * win[14:30, :] + fw_ref[1] * win[15:31, :] + fw_ref[2] * win[16:32, :]
            f_ref[pl.ds(r0, 16), :] = (ac * _sigmoid(ac) * bv_ref[pl.ds(r0, 16), :].astype(F32)).astype(BF16)
        _row_loop(TE, 16, conv)
        hs[...] = h1_ref[...] + _dot(f_ref[...], wd_ref[...])

        def head(r0):
            rows = pl.ds(r0, 32)
            h2 = hs[rows, :]
            rinv = lax.rsqrt(jnp.mean(h2 * h2, axis=-1, keepdims=True) + RMS_EPS)
            hh = h2 * rinv
            gid = i * TE + r0 + lax.broadcasted_iota(jnp.int32, (32, 1), 0)
            live = jnp.logical_and(gid >= N_META, gid < n_real)
            err = jnp.where(live, hh * gf_ref[...] - tg_ref[rows, :], 0.0)
            dy = err * (1.0 / D)
            red_ref[0:8, :] += _rows8(err * err)
            red_ref[8:16, :] += _rows8(dy * hh)
            dhh = dy * gf_ref[...]
            dh2_ref[rows, :] = rinv * (dhh - hh * jnp.mean(dhh * hh, axis=-1, keepdims=True))
        _row_loop(TE, 32, head, unroll=2)

    row = lambda w: pl.BlockSpec((TE, w), lambda i: (i, 0))
    return pl.pallas_call(
        body, name="ffn_out", grid=(nt,),
        in_specs=[pl.BlockSpec((TE, DFF), lambda i: (i, 0)), pl.BlockSpec((TE, DFF), lambda i: (i, 1)),
                  _resident((4, 16, DFF)), row(D), _resident((DFF, D)), _resident((1, D)), row(D)],
        out_specs=[row(DFF), row(D), _resident((16, D))],
        out_shape=[SDS((t, DFF), BF16), SDS((t, D), F32), SDS((16, D), F32)],
        scratch_shapes=[pltpu.VMEM((TE + HALO_F, DFF), F32), pltpu.VMEM((TE, D), F32)],
        compiler_params=_ARB1)(up, up, fw, h1, wd, gf, tgt)


def _ffn_bwd(dh2, wd, up, fw):
    t = dh2.shape[0]
    nt = t // TE
    hb = TE // HALO_F

    def body(dh_ref, wd_ref, a_ref, ah_ref, bv_ref, fw_ref, dup_ref, dw_ref, aext, dax, dfs):
        i = pl.program_id(0)
        ti = nt - 1 - i

        @pl.when(i == 0)
        def _():
            dax[TE:TE + HALO_F, :] = jnp.zeros((HALO_F, DFF), F32)
            dw_ref[...] = jnp.zeros_like(dw_ref)

        @pl.when(i > 0)
        def _():
            dax[TE:TE + HALO_F, :] = dax[0:HALO_F, :]

        aext[0:HALO_F, :] = jnp.where(ti > 0, ah_ref[...].astype(F32), 0.0)
        dfs[...] = _dot_nt(dh_ref[...].astype(BF16), wd_ref[...])

        def cp(r0):
            aext[pl.ds(HALO_F + r0, 16), :] = a_ref[pl.ds(r0, 16), :].astype(F32)
        _row_loop(TE, 16, cp)

        def act(r0):
            rows = pl.ds(r0, 16)
            win = aext[pl.ds(r0, 32), :]
            ac = fw_ref[3] + fw_ref[0] * win[14:30, :] + fw_ref[1] * win[15:31, :] + fw_ref[2] * win[16:32, :]
            sg = _sigmoid(ac)
            df = dfs[rows, :]
            dup_ref[rows, DFF:2 * DFF] = (df * ac * sg).astype(BF16)
            dac = df * bv_ref[rows, :].astype(F32) * sg * (1.0 + ac * (1.0 - sg))
            dax[rows, :] = dac
            dw_ref[3] += _rows8(dac)
            for j in range(FFN_K):
                dw_ref[j] += _rows8(dac * win[14 + j:30 + j, :])
        _row_loop(TE, 16, act)

        def convt(r0):
            win = dax[pl.ds(r0, 32), :]
            da = fw_ref[2] * win[0:16, :] + fw_ref[1] * win[1:17, :] + fw_ref[0] * win[2:18, :]
            dup_ref[pl.ds(r0, 16), 0:DFF] = da.astype(BF16)
        _row_loop(TE, 16, convt)

    rev = lambda w: pl.BlockSpec((TE, w), lambda i: (nt - 1 - i, 0))
    return pl.pallas_call(
        body, name="ffn_bwd", grid=(nt,),
        in_specs=[rev(D), _resident((DFF, D)), rev(DFF),
                  pl.BlockSpec((HALO_F, DFF), lambda i: (jnp.maximum((nt - 1 - i) * hb - 1, 0), 0)),
                  pl.BlockSpec((TE, DFF), lambda i: (nt - 1 - i, 1)), _resident((4, 16, DFF))],
        out_specs=[rev(2 * DFF), _resident((4, 8, DFF))],
        out_shape=[SDS((t, 2 * DFF), BF16), SDS((4, 8, DFF), F32)],
        scratch_shapes=[pltpu.VMEM((TE + HALO_F, DFF), F32), pltpu.VMEM((TE + HALO_F, DFF), F32),
                        pltpu.VMEM((TE, DFF), F32)],
        compiler_params=_ARB1)(dh2, wd, up, up, up, fw)


def _dgrad_norm(dy, w_t, h, g, dres, name, dy_extra=None, w_extra_t=None, carry=()):
    t, k = dy.shape
    nt = t // TM

    def body(*refs):
        if dy_extra is None:
            dy_ref, w_ref, h_ref, g_ref, dr_ref, dh_ref, dg_ref, acc = refs
        else:
            dy_ref, w_ref, h_ref, g_ref, dr_ref, de_ref, we_ref, dh_ref, dg_ref, acc = refs

        @pl.when(pl.program_id(0) == 0)
        def _():
            dg_ref[...] = jnp.zeros_like(dg_ref)

        acc[...] = _dot(dy_ref[...], w_ref[...])
        if dy_extra is not None:
            acc[...] += _dot(de_ref[...], we_ref[...])

        def blk(r0):
            rows = pl.ds(r0, 32)
            x = h_ref[rows, :]
            rinv = lax.rsqrt(jnp.mean(x * x, axis=-1, keepdims=True) + RMS_EPS)
            hh = x * rinv
            du = acc[rows, :]
            dg_ref[...] += _rows8(du * hh)
            dhh = du * g_ref[...]
            dh_ref[rows, :] = dr_ref[rows, :] + rinv * (dhh - hh * jnp.mean(dhh * hh, axis=-1, keepdims=True))
        _row_loop(TM, 32, blk, unroll=2)

    row = pl.BlockSpec((TM, D), lambda i: (i, 0))
    in_specs = [pl.BlockSpec((TM, k), lambda i: (i, 0)), _once((k, D)), row, _resident((1, D)), row]
    args = [dy, w_t, h, g, dres]
    if dy_extra is not None:
        in_specs += [pl.BlockSpec((TM, dy_extra.shape[1]), lambda i: (i, 0)), _once(w_extra_t.shape)]
        args += [dy_extra, w_extra_t]
    return _carried_call(
        body, carry, name=name, grid=(nt,), in_specs=in_specs, out_specs=[row, _resident((8, D))],
        out_shape=[SDS((t, D), F32), SDS((8, D), F32)],
        scratch_shapes=[pltpu.VMEM((TM, D), F32)], compiler_params=_ARB1)(*args)


def _wgrad(x, dy, tk, name):
    t, k = x.shape
    n = dy.shape[1]
    tm = _row_tile(t, TM_BIG)
    nk, nt = k // tk, t // tm

    def body(x_ref, dy_ref, o_ref, acc):
        @pl.when(pl.program_id(1) == 0)
        def _():
            acc[...] = jnp.zeros_like(acc)
        acc[...] += _dot_tn(x_ref[...], dy_ref[...].astype(BF16))

        @pl.when(pl.program_id(1) == nt - 1)
        def _():
            o_ref[...] = acc[...].astype(BF16)

    return pl.pallas_call(
        body, name=name, grid=(nk, nt),
        in_specs=[pl.BlockSpec((tm, tk), lambda j, i: (i, j)), pl.BlockSpec((tm, n), lambda j, i: (i, 0))],
        out_specs=pl.BlockSpec((tk, n), lambda j, i: (j, 0)), out_shape=SDS((k, n), BF16),
        scratch_shapes=[pltpu.VMEM((tk, n), F32)], compiler_params=_ARB2)(x, dy)


def _mix_bwd(dh1, wo, wg, wc, proj, brg, brc):
    t = dh1.shape[0]
    nt = t // TM

    def body(dh_ref, wo_ref, wg_ref, wc_ref, g_ref, brg_ref, brc_ref, dbg_ref, dbc_ref, dog_ref, dcs_ref, dp_ref, dm):
        dm[...] = _dot_nt(dh_ref[...].astype(BF16), wo_ref[...])

        def blk(r0):
            rows = pl.ds(r0, 32)
            d = dm[rows, :]
            sg = _sigmoid(g_ref[rows, 0:D].astype(F32))
            sc = _sigmoid(g_ref[rows, D:2 * D].astype(F32))
            dbg_ref[rows, :] = (d * sg).astype(BF16)
            dbc_ref[rows, :] = (d * sc).astype(BF16)
            dp_ref[rows, 0:D] = (d * brg_ref[rows, :].astype(F32) * sg * (1.0 - sg)).astype(BF16)
            dp_ref[rows, D:2 * D] = (d * brc_ref[rows, :].astype(F32) * sc * (1.0 - sc)).astype(BF16)
        _row_loop(TM, 32, blk)
        dog_ref[...] = _dot_nt(dbg_ref[...], wg_ref[...]).astype(BF16)
        dcs_ref[...] = _dot_nt(dbc_ref[...], wc_ref[...]).astype(BF16)

    row = pl.BlockSpec((TM, D), lambda i: (i, 0))
    wide = pl.BlockSpec((TM, 2 * D), lambda i: (i, 2))
    return pl.pallas_call(
        body, name="mix_bwd", grid=(nt,),
        in_specs=[row, _once((D, D)), _once((D, D)), _once((D, D)), wide, row, row],
        out_specs=[row, row, row, row, wide],
        out_shape=[SDS((t, D), BF16)] * 4 + [SDS((t, NPROJ), BF16)],
        scratch_shapes=[pltpu.VMEM((TM, D), F32)], compiler_params=_ARB1)(dh1, wo, wg, wc, proj, brg, brc)


def _glapost_bwd(dog, o, proj, gn, dproj):
    t = o.shape[0]
    nt = t // TE

    def body(dog_ref, o_ref, r_ref, gn_ref, dp_in, do_ref, dp_ref, dgn_ref):
        del dp_in

        @pl.when(pl.program_id(0) == 0)
        def _():
            dgn_ref[...] = jnp.zeros_like(dgn_ref)

        def blk(r0):
            rows = pl.ds(r0, 32)
            for h in range(HEADS):
                vs = slice(h * DVH, (h + 1) * DVH)
                x = o_ref[rows, vs]
                rinv = lax.rsqrt(jnp.mean(x * x, axis=-1, keepdims=True) + RMS_EPS)
                oh = x * rinv
                g = gn_ref[:, vs]
                rr = r_ref[rows, vs].astype(F32)
                sr = _sigmoid(rr)
                d = dog_ref[rows, vs].astype(F32)
                dp_ref[rows, vs] = (d * oh * g * sr * (1.0 + rr * (1.0 - sr))).astype(BF16)
                don = d * rr * sr
                dgn_ref[:, vs] += _rows8(don * oh)
                doh = don * g
                do_ref[rows, vs] = (rinv * (doh - oh * jnp.mean(doh * oh, axis=-1, keepdims=True))).astype(BF16)
        _row_loop(TE, 32, blk)

    row = pl.BlockSpec((TE, D), lambda i: (i, 0))
    rcol = pl.BlockSpec((TE, D), lambda i: (i, 6))
    return pl.pallas_call(
        body, name="glapost_bwd", grid=(nt,),
        in_specs=[row, row, rcol, _resident((1, D)), pl.BlockSpec(memory_space=pl.ANY)],
        out_specs=[row, rcol, _resident((8, D))],
        out_shape=[SDS((t, D), BF16), SDS((t, NPROJ), BF16), SDS((8, D), F32)],
        input_output_aliases={4: 1}, compiler_params=_ARB1)(dog, o, proj, gn, dproj)


def _conf_bwd(dcs, cc, proj, cw, lg, lb, dproj, carry=()):
    t = cc.shape[0]
    nt = t // TE
    hb = TE // HALO

    def body(dcs_ref, cc_ref, c1_ref, c2_ref, c1h_ref, c2h_ref, cw_ref, lg_ref, lb_ref, dp_in,
             dp_ref, dw_ref, ds_ref, cext, dext):
        del dp_in
        i = pl.program_id(0)
        ti = nt - 1 - i

        @pl.when(i == 0)
        def _():
            dext[TE:TE + HALO, :] = jnp.zeros((HALO, D), F32)
            dw_ref[...] = jnp.zeros_like(dw_ref)
            ds_ref[...] = jnp.zeros_like(ds_ref)

        @pl.when(i > 0)
        def _():
            dext[TE:TE + HALO, :] = dext[0:HALO, :]

        ch = c1h_ref[...].astype(F32) * _sigmoid(c2h_ref[...].astype(F32))
        cext[0:HALO, :] = jnp.where(ti > 0, ch, 0.0)

        def pre(r0):
            rows = pl.ds(r0, 32)
            cext[pl.ds(HALO + r0, 32), :] = c1_ref[rows, :].astype(F32) * _sigmoid(c2_ref[rows, :].astype(F32))
            x = cc_ref[rows, :]
            mu = jnp.mean(x, axis=-1, keepdims=True)
            xc = x - mu
            rstd = lax.rsqrt(jnp.mean(xc * xc, axis=-1, keepdims=True) + LN_EPS)
            xh = xc * rstd
            ln = xh * lg_ref[...] + lb_ref[...]
            sg = _sigmoid(ln)
            dln = dcs_ref[rows, :].astype(F32) * sg * (1.0 + ln * (1.0 - sg))
            ds_ref[0] += _rows8(dln * xh)
            ds_ref[1] += _rows8(dln)
            dxh = dln * lg_ref[...]
            dcc = rstd * (dxh - jnp.mean(dxh, axis=-1, keepdims=True) - xh * jnp.mean(dxh * xh, axis=-1, keepdims=True))
            dext[rows, :] = dcc
            ds_ref[2] += _rows8(dcc)
        _row_loop(TE, 32, pre, unroll=2)

        def convt(r0):
            rows = pl.ds(r0, CONV_ROWS)
            for part in range(D // CONV_LANES):
                cols = slice(part * CONV_LANES, (part + 1) * CONV_LANES)
                wd = dext[pl.ds(r0, CONV_ROWS + HALO), cols]
                dc = jnp.zeros((CONV_ROWS, CONV_LANES), F32)
                for s, taps in _shift_classes([(CONF_K - 1 - j, j) for j in range(CONF_K)]):
                    ws = _shifted(wd, s)
                    for a8, j in taps:
                        dc = dc + jnp.tile(cw_ref[j, :, cols], (CONV_ROWS // 8, 1)) * ws[a8:a8 + CONV_ROWS, :]
                dcc = wd[0:CONV_ROWS, :]
                wc = cext[pl.ds(r0, CONV_ROWS + HALO), cols]
                for s, taps in _shift_classes([(2 + j, j) for j in range(CONF_K)]):
                    ws = _shifted(wc, s)
                    for a8, j in taps:
                        dw_ref[j, :, cols] += _rows8(dcc * ws[a8:a8 + CONV_ROWS, :])
                c1 = c1_ref[rows, cols].astype(F32)
                s2 = _sigmoid(c2_ref[rows, cols].astype(F32))
                dp_ref[rows, cols] = (dc * s2).astype(BF16)
                dp_ref[rows, D + part * CONV_LANES:D + (part + 1) * CONV_LANES] = (dc * c1 * s2 * (1.0 - s2)).astype(BF16)
        _row_loop(TE, CONV_ROWS, convt)

    rev = lambda col: pl.BlockSpec((TE, D), lambda i: (nt - 1 - i, col))
    halo = lambda col: pl.BlockSpec((HALO, D), lambda i: (jnp.maximum((nt - 1 - i) * hb - 1, 0), col))
    return _carried_call(
        body, carry, name="conf_bwd", grid=(nt,),
        in_specs=[rev(0), rev(0), rev(2), rev(3), halo(2), halo(3), _resident((32, 8, D)), _resident((1, D)),
                  _resident((1, D)), pl.BlockSpec(memory_space=pl.ANY)],
        out_specs=[pl.BlockSpec((TE, 2 * D), lambda i: (nt - 1 - i, 1)), _resident((32, 8, D)), _resident((3, 8, D))],
        out_shape=[SDS((t, NPROJ), BF16), SDS((32, 8, D), F32), SDS((3, 8, D), F32)],
        scratch_shapes=[pltpu.VMEM((TE + HALO, D), F32), pltpu.VMEM((TE + HALO, D), F32)],
        input_output_aliases={9: 0}, compiler_params=_ARB1)(dcs, cc, proj, proj, proj, proj, cw, lg, lb, dproj)


def _gla_bwd(proj, alr, wau, balpha, do, sall, dproj, carry=()):
    t = proj.shape[0]
    nc = t // CH

    def body(qk_ref, v_ref, a_ref, wau_ref, ba_ref, do_ref, s_ref, dp_in, dp_ref, da_ref, dwau_ref, dba_ref, ds_scr, dla_scr):
        del dp_in

        @pl.when(pl.program_id(0) == 0)
        def _():
            ds_scr[...] = jnp.zeros_like(ds_scr)
            dwau_ref[...] = jnp.zeros_like(dwau_ref)
            dba_ref[...] = jnp.zeros_like(dba_ref)

        z, b, bmid, blast, causal = _gla_decay(a_ref, wau_ref, ba_ref)
        dlasts = []
        for h in range(HEADS):
            ks = slice(h * DKH, (h + 1) * DKH)
            vs = slice(h * DVH, (h + 1) * DVH)
            bh, mh, lh = b[:, ks], bmid[:, ks], blast[:, ks]
            q = qk_ref[:, ks].astype(F32) * (DKH ** -0.5)
            k = qk_ref[:, DK + h * DKH:DK + (h + 1) * DKH].astype(F32)
            v = v_ref[:, vs]
            dout = do_ref[:, vs]
            eq, ek, eb, eg, el = jnp.exp(bh - mh), jnp.exp(mh - bh), jnp.exp(bh), jnp.exp(lh - bh), jnp.exp(lh)
            qt, kt = (q * eq).astype(BF16), (k * ek).astype(BF16)
            qg, kg = (q * eb).astype(BF16), (k * eg).astype(BF16)
            st = s_ref[0, vs, :]
            dsn = ds_scr[vs, :]
            st16, dsn16 = st.astype(BF16), dsn.astype(BF16)
            a = jnp.where(causal, _dot_nt(qt, kt), 0.0).astype(BF16)
            da = jnp.where(causal, _dot_nt(dout, v), 0.0).astype(BF16)
            dq_inter = _dot(dout, st16) * eb
            dk_inter = _dot(v, dsn16) * eg
            dq = _dot(da, kt) * eq + dq_inter
            dk = _dot_tn(da, qt) * ek + dk_inter
            dv = _dot_tn(a, dout) + _dot_nt(kg, dsn16)
            dlasts.append(jnp.sum(k * dk_inter, axis=0, keepdims=True) + jnp.sum(st * dsn, axis=0, keepdims=True) * el[0:1, :])
            dla_scr[:, ks] = q * dq - k * dk
            ds_scr[vs, :] = dsn * jnp.concatenate([el, el], axis=0) + _dot_tn(dout, qg)
            dp_ref[:, ks] = (dq * (DKH ** -0.5)).astype(BF16)
            dp_ref[:, DK + h * DKH:DK + (h + 1) * DKH] = dk.astype(BF16)
            dp_ref[:, D + h * DVH:D + (h + 1) * DVH] = dv.astype(BF16)
        r = lax.broadcasted_iota(jnp.int32, (CH, CH), 0)
        c = lax.broadcasted_iota(jnp.int32, (CH, CH), 1)
        dla = _tri_matmul((r <= c).astype(BF16), dla_scr[...]) + jnp.concatenate(dlasts, axis=1)
        dz = (dla * (1.0 / TAU) * _sigmoid(-z)).astype(BF16)
        da_ref[...] = _dot_nt(dz, wau_ref[...]).astype(BF16)
        dwau_ref[...] += _dot_tn(a_ref[...], dz)
        dba_ref[...] += _rows8(dz.astype(F32))

    rev = lambda w, col: pl.BlockSpec((CH, w), lambda c: (nc - 1 - c, col))
    return _carried_call(
        body, carry, name="gla_bwd", grid=(nc,),
        in_specs=[rev(D, 0), rev(D, 1), rev(LANES, 0), _resident((LANES, DK)), _resident((1, DK)), rev(D, 0),
                  pl.BlockSpec((1, DV, DKH), lambda c: (nc - 1 - c, 0, 0)), pl.BlockSpec(memory_space=pl.ANY)],
        out_specs=[rev(2 * D, 0), rev(LANES, 0), _resident((LANES, DK)), _resident((8, DK))],
        out_shape=[SDS((t, NPROJ), BF16), SDS((t, LANES), BF16), SDS((LANES, DK), F32), SDS((8, DK), F32)],
        scratch_shapes=[pltpu.VMEM((DV, DKH), F32), pltpu.VMEM((CH, DK), F32)],
        input_output_aliases={7: 0}, compiler_params=_ARB1)(proj, proj, alr, wau, balpha, do, sall, dproj)


def _all_gather(xs, name):
    n = len(xs)

    def body(*refs):
        x_refs, out_refs = refs[:n], refs[n:2 * n]
        send_sems, recv_sems = refs[2 * n:]
        x, y, c = _my_place()
        me, sibling = (x, y, c), (x, y, 1 - c)
        chips = [(1 - x, y), (x, 1 - y), (1 - x, 1 - y)]

        def slot(p, px, py, pc):
            return out_refs[p].at[4 * px + 2 * py + pc]

        def copy(p, k, block, to, src=None):
            return pltpu.make_async_remote_copy(
                src_ref=slot(p, *block) if src is None else src, dst_ref=slot(p, *block),
                send_sem=send_sems.at[7 * p + k], recv_sem=recv_sems.at[7 * p + k], device_id=to, device_id_type=MESH_T)

        first = []
        for p in range(n):
            first.append(copy(p, 0, me, sibling, src=x_refs[p]))
            first += [copy(p, 1 + j, me, (*chip, c), src=x_refs[p]) for j, chip in enumerate(chips)]
        for cp in first:
            cp.start()
        passed = []
        for p in range(n):
            for j, chip in enumerate(chips):
                copy(p, 1 + j, (*chip, c), me).wait_recv()
                fwd = copy(p, 4 + j, (*chip, c), sibling)
                fwd.start()
                passed.append(fwd)
        for p in range(n):
            copy(p, 0, sibling, me).wait_recv()
            for j, chip in enumerate(chips):
                copy(p, 4 + j, (*chip, 1 - c), me).wait_recv()
        for cp in first + passed:
            cp.wait_send()

    hbm = pl.BlockSpec(memory_space=pl.ANY)
    return pl.pallas_call(
        body, name=name, out_shape=[SDS((N_DEV, *a.shape), a.dtype) for a in xs],
        in_specs=[hbm] * n, out_specs=[hbm] * n,
        scratch_shapes=[pltpu.SemaphoreType.DMA((7 * n,)), pltpu.SemaphoreType.DMA((7 * n,))])(*xs)


def _exchange(gs):
    n = len(gs)

    def body(*refs):
        g_refs, land_refs = refs[:n], refs[n:2 * n]
        send_sems, recv_sems, local_sems = refs[2 * n:]
        x, y, c = _my_place()
        my_idx = 4 * x + 2 * y + c
        mine = [pltpu.make_async_copy(g_refs[p].at[my_idx], land_refs[p].at[my_idx], local_sems.at[p]) for p in range(n)]
        for cp in mine:
            cp.start()
        copies = []
        for k in range(1, N_DEV):
            px, py, pc = _flip(x, k & 4), _flip(y, k & 2), _flip(c, k & 1)
            p_idx = 4 * px + 2 * py + pc
            for p in range(n):
                s = 7 * p + k - 1
                cp = pltpu.make_async_remote_copy(
                    src_ref=g_refs[p].at[p_idx], dst_ref=land_refs[p].at[my_idx], send_sem=send_sems.at[s],
                    recv_sem=recv_sems.at[s], device_id=(px, py, pc), device_id_type=MESH_T)
                cp.start()
                arrival = pltpu.make_async_remote_copy(
                    src_ref=g_refs[p].at[p_idx], dst_ref=land_refs[p].at[p_idx], send_sem=send_sems.at[s],
                    recv_sem=recv_sems.at[s], device_id=(px, py, pc), device_id_type=MESH_T)
                copies.append((cp, arrival))
        for cp, arrival in copies:
            arrival.wait_recv()
        for cp, arrival in copies:
            cp.wait_send()
        for cp in mine:
            cp.wait()

    hbm = pl.BlockSpec(memory_space=pl.ANY)
    return pl.pallas_call(
        body, name="grad_exchange", out_shape=[SDS(g.shape, g.dtype) for g in gs],
        in_specs=[hbm] * n, out_specs=[hbm] * n,
        scratch_shapes=[pltpu.SemaphoreType.DMA((7 * n,)), pltpu.SemaphoreType.DMA((7 * n,)),
                        pltpu.SemaphoreType.DMA((n,))])(*gs)


def _adamw(land, w, m, v, rows_blk, name):
    rows = w.shape[0]

    def body(l_ref, w_ref, m_ref, v_ref, g_ref, d_ref, nm_ref, nv_ref):
        g = l_ref[0].astype(F32)
        for s in range(1, N_DEV):
            g = g + l_ref[s].astype(F32)
        nm = ADAM_B1 * m_ref[...] + (1.0 - ADAM_B1) * g
        nv = ADAM_B2 * v_ref[...] + (1.0 - ADAM_B2) * (g * g)
        m_hat = nm / (1.0 - ADAM_B1 ** ADAM_STEP)
        v_hat = nv / (1.0 - ADAM_B2 ** ADAM_STEP)
        g_ref[...] = g
        d_ref[...] = -ADAM_LR * (m_hat / (jnp.sqrt(v_hat) + ADAM_EPS) + ADAM_WD * w_ref[...])
        nm_ref[...] = nm
        nv_ref[...] = nv

    blk = pl.BlockSpec((rows_blk, D), lambda i: (i, 0))
    return pl.pallas_call(
        body, name=name, grid=(rows // rows_blk,),
        in_specs=[pl.BlockSpec((N_DEV, rows_blk, D), lambda i: (0, i, 0)), blk, blk, blk],
        out_specs=[blk] * 4, out_shape=[SDS((rows, D), F32)] * 4, compiler_params=_ARB1)(land, w, m, v)


BIG = ("w_in", "w_up", "w_down", "w_gla_o", "w_conf_o", "w_out")
BIG_TRANSPOSED = ("w_in", "w_up")
SMALL_SHARDED = ("meta_tokens", "conf_dw_w", "ffn_dw_w", "w_alpha_up")
REPLICATED = ("norm_mix_g", "b_alpha", "gla_norm_g", "conf_dw_b", "conf_ln_g", "conf_ln_b", "norm_ffn_g", "ffn_dw_b",
              "final_norm_g")
N_IN = sum(IN_WIDTHS)
W_IN_ROWS = N_IN // N_DEV
W_IN_PAD = -(-W_IN_ROWS // 16) * 16
ADAM_BLOCK = {"w_in": W_IN_PAD // 3, "w_up": 176, "w_down": 176, "w_gla_o": 128, "w_conf_o": 128, "w_out": 128}
SMALL_ROWS = 32


def _to_panel(name, shard):
    a = shard.reshape(shard.shape[-2], shard.shape[-1])
    if name in BIG_TRANSPOSED:
        a = a.T
    if name == "w_in":
        a = jnp.pad(a, ((0, W_IN_PAD - W_IN_ROWS), (0, 0)))
    return a


def _from_panel(name, panel, shape):
    a = panel[0:W_IN_ROWS] if name == "w_in" else panel
    if name in BIG_TRANSPOSED:
        a = a.T
    return a.reshape(shape)


def _pack_small(arrs):
    flat = jnp.concatenate([jnp.pad(a.reshape(-1), (0, (-a.size) % D)) for a in arrs])
    return jnp.pad(flat, (0, SMALL_ROWS * D - flat.shape[0])).reshape(SMALL_ROWS, D)


def _unpack_small(panel, shapes):
    flat, out, off = panel.reshape(-1), [], 0
    for shp in shapes:
        n = 1
        for s in shp:
            n *= s
        out.append(flat[off:off + n].reshape(shp))
        off += n + (-n) % D
    return out


def _local_step(x, target, w, shards=None):
    dist = shards is not None
    w = dict(w)

    def gather(names):
        return [(shards[n], False) for n in names] if dist else []

    def scatter(*arrs):
        return [(a.reshape(N_DEV, -1, D), True) for a in arrs] if dist else []

    s = x.shape[0]
    n_real = s + N_META
    t = -(-n_real // TM) * TM
    h0 = jnp.concatenate([w["meta_tokens"], x, jnp.zeros((t - n_real, D), F32)], axis=0)
    tgt = jnp.concatenate([jnp.zeros((N_META, D), F32), target, jnp.zeros((t - n_real, D), F32)], axis=0)

    q0, r0, a0, c0 = 0, 2 * DK + DV, 2 * DK + 2 * DV, 2 * DK + 2 * DV + RANK
    wt = w["w_in_t"]
    w_main = jnp.concatenate([wt[q0:r0], wt[c0:N_IN], wt[r0:a0]], axis=0)
    w_a = jnp.pad(wt[a0:c0], ((0, LANES - RANK), (0, 0)))
    wau = jnp.pad(w["w_alpha_up"].astype(BF16), ((0, LANES - RANK), (0, 0)))
    row = lambda name: w[name].reshape(1, -1)
    cw = jnp.broadcast_to(jnp.pad(w["conf_dw_w"], ((0, 32 - CONF_K), (0, 0)))[:, None, :], (32, 8, D))
    fw = jnp.broadcast_to(jnp.concatenate([w["ffn_dw_w"], w["ffn_dw_b"].reshape(1, -1)], axis=0)[:, None, :],
                          (FFN_K + 1, 16, DFF))

    early = ("w_gla_o", "w_conf_o", "w_out", "w_up")
    u1, proj, alr, *landed = _norm_matmul(h0, row("norm_mix_g"), w_main, 1024, "in_proj", w_extra_t=w_a, carry=gather(early))
    for n, land in zip(early, landed):
        w["w_up_t" if n == "w_up" else n] = land.reshape(-1, D)
    o, og, sall = _gla_fwd(proj, alr, wau, row("b_alpha"), row("gla_norm_g"))
    cc, cs, *landed = _conf_fwd(proj, cw, row("conf_dw_b"), row("conf_ln_g"), row("conf_ln_b"), carry=gather(("w_down",)))
    if dist:
        w["w_down"] = landed[0].reshape(-1, D)
    brg, brc, merged, h1 = _mix_fwd(og, cs, proj, h0, w["w_gla_o"], w["w_conf_o"], w["w_out"])
    u2, up = _norm_matmul(h1, row("norm_ffn_g"), w["w_up_t"], 512, "up_proj")
    f, dh2, red = _ffn_out(up, fw, h1, w["w_down"], row("final_norm_g"), tgt, n_real)
    loss = 0.5 / D * jnp.sum(red[0:8])

    g = {"final_norm_g": jnp.sum(red[8:16], axis=0)}
    dup, dfw = _ffn_bwd(dh2, w["w_down"], up, fw)
    g["ffn_dw_w"] = jnp.sum(dfw[0:FFN_K], axis=1)
    g["ffn_dw_b"] = jnp.sum(dfw[3], axis=0)
    g["w_down"] = _wgrad(f, dh2, 1408, "wgrad_down")
    dh1, dg2, *landed = _dgrad_norm(dup, w["w_up_t"], h1, row("norm_ffn_g"), dh2, "up_dgrad", carry=scatter(g["w_down"]))
    if dist:
        g["w_down"] = landed[0]
    g["norm_ffn_g"] = jnp.sum(dg2, axis=0)
    g["w_up_t"] = _wgrad(dup, u2, 1408, "wgrad_up")
    dbrg, dbrc, dog, dcs, dproj = _mix_bwd(dh1, w["w_out"], w["w_gla_o"], w["w_conf_o"], proj, brg, brc)
    g["w_out"] = _wgrad(merged, dh1, 1024, "wgrad_out")
    g["w_gla_o"] = _wgrad(og, dbrg, 1024, "wgrad_gla_o")
    g["w_conf_o"] = _wgrad(cs, dbrc, 1024, "wgrad_conf_o")
    do, dproj, dgn = _glapost_bwd(dog, o, proj, row("gla_norm_g"), dproj)
    g["gla_norm_g"] = jnp.sum(dgn, axis=0)
    dproj, dcw, dst, *landed = _conf_bwd(dcs, cc, proj, cw, row("conf_ln_g"), row("conf_ln_b"), dproj,
                                         carry=scatter(g["w_up_t"]))
    if dist:
        g["w_up_t"] = landed[0]
    g["conf_dw_w"] = jnp.sum(dcw[0:CONF_K], axis=1)
    g["conf_ln_g"], g["conf_ln_b"], g["conf_dw_b"] = jnp.sum(dst[0], axis=0), jnp.sum(dst[1], axis=0), jnp.sum(dst[2], axis=0)
    dproj, dalr, dwau, dba, *landed = _gla_bwd(proj, alr, wau, row("b_alpha"), do, sall, dproj,
                                               carry=scatter(g["w_out"], g["w_gla_o"], g["w_conf_o"]))
    if dist:
        g["w_out"], g["w_gla_o"], g["w_conf_o"] = landed
    g["w_alpha_up"] = dwau[0:RANK]
    g["b_alpha"] = jnp.sum(dba, axis=0)
    dw_main = _wgrad(dproj, u1, 1024, "wgrad_in")
    dw_a = _wgrad(dalr, u1, LANES, "wgrad_alr")
    g["w_in_t"] = jnp.concatenate([dw_main[0:r0], dw_main[NPROJ - DV:NPROJ], dw_a[0:RANK], dw_main[r0:NPROJ - DV]], axis=0)
    w_in_blocks = []
    if dist:
        pad = ((0, 0), (0, W_IN_PAD - W_IN_ROWS), (0, 0))
        w_in_blocks = [(jnp.pad(g["w_in_t"].reshape(N_DEV, W_IN_ROWS, D), pad), True)]
    dh0, dg1, *landed = _dgrad_norm(dproj, w_main, h0, row("norm_mix_g"), dh1, "in_dgrad", dy_extra=dalr,
                                    w_extra_t=w_a, carry=w_in_blocks)
    if dist:
        g["w_in_t"] = landed[0]
    g["norm_mix_g"] = jnp.sum(dg1, axis=0)
    g["meta_tokens"] = dh0[0:N_META]
    return loss, dh0[N_META:n_real], g


def kernel(x, meta_tokens, norm_mix_g, w_in, w_alpha_up, b_alpha, gla_norm_g, w_gla_o, conf_dw_w, conf_dw_b, conf_ln_g, conf_ln_b, w_conf_o, w_out, norm_ffn_g, w_up, ffn_dw_w, ffn_dw_b, w_down, final_norm_g, loss_target, m_meta_tokens, m_norm_mix_g, m_w_in, m_w_alpha_up, m_b_alpha, m_gla_norm_g, m_w_gla_o, m_conf_dw_w, m_conf_dw_b, m_conf_ln_g, m_conf_ln_b, m_w_conf_o, m_w_out, m_norm_ffn_g, m_w_up, m_ffn_dw_w, m_ffn_dw_b, m_w_down, m_final_norm_g, v_meta_tokens, v_norm_mix_g, v_w_in, v_w_alpha_up, v_b_alpha, v_gla_norm_g, v_w_gla_o, v_conf_dw_w, v_conf_dw_b, v_conf_ln_g, v_conf_ln_b, v_w_conf_o, v_w_out, v_norm_ffn_g, v_w_up, v_ffn_dw_w, v_ffn_dw_b, v_w_down, v_final_norm_g):
    ws = dict(meta_tokens=meta_tokens, norm_mix_g=norm_mix_g, w_in=w_in, w_alpha_up=w_alpha_up, b_alpha=b_alpha,
              gla_norm_g=gla_norm_g, w_gla_o=w_gla_o, conf_dw_w=conf_dw_w, conf_dw_b=conf_dw_b, conf_ln_g=conf_ln_g,
              conf_ln_b=conf_ln_b, w_conf_o=w_conf_o, w_out=w_out, norm_ffn_g=norm_ffn_g, w_up=w_up, ffn_dw_w=ffn_dw_w,
              ffn_dw_b=ffn_dw_b, w_down=w_down, final_norm_g=final_norm_g)
    ms = dict(meta_tokens=m_meta_tokens, norm_mix_g=m_norm_mix_g, w_in=m_w_in, w_alpha_up=m_w_alpha_up, b_alpha=m_b_alpha,
              gla_norm_g=m_gla_norm_g, w_gla_o=m_w_gla_o, conf_dw_w=m_conf_dw_w, conf_dw_b=m_conf_dw_b,
              conf_ln_g=m_conf_ln_g, conf_ln_b=m_conf_ln_b, w_conf_o=m_w_conf_o, w_out=m_w_out, norm_ffn_g=m_norm_ffn_g,
              w_up=m_w_up, ffn_dw_w=m_ffn_dw_w, ffn_dw_b=m_ffn_dw_b, w_down=m_w_down, final_norm_g=m_final_norm_g)
    vs = dict(meta_tokens=v_meta_tokens, norm_mix_g=v_norm_mix_g, w_in=v_w_in, w_alpha_up=v_w_alpha_up, b_alpha=v_b_alpha,
              gla_norm_g=v_gla_norm_g, w_gla_o=v_w_gla_o, conf_dw_w=v_conf_dw_w, conf_dw_b=v_conf_dw_b,
              conf_ln_g=v_conf_ln_g, conf_ln_b=v_conf_ln_b, w_conf_o=v_w_conf_o, w_out=v_w_out, norm_ffn_g=v_norm_ffn_g,
              w_up=v_w_up, ffn_dw_w=v_ffn_dw_w, ffn_dw_b=v_ffn_dw_b, w_down=v_w_down, final_norm_g=v_final_norm_g)
    small = SMALL_SHARDED + REPLICATED
    pack_small = lambda d: _pack_small([d[n] for n in small])

    shards = {n: _to_panel(n, ws[n]).astype(BF16) for n in BIG}
    own = [shards["w_in"], pack_small(ws)]
    my_idx = 4 * lax.axis_index("x") + 2 * lax.axis_index("y") + lax.axis_index("c")
    gathered = [lax.dynamic_update_slice(full_, mine[None], (my_idx, 0, 0))
                for full_, mine in zip(_all_gather(own, "weight_gather"), own)]
    full = {n: ws[n].reshape(-1) for n in REPLICATED}
    full["w_in_t"] = gathered[0][:, 0:W_IN_ROWS].reshape(N_IN, D)
    flat, off = gathered[1].reshape(N_DEV, -1), 0
    for n in SMALL_SHARDED:
        k, c = ws[n].shape[-2], ws[n].shape[-1]
        full[n] = flat[:, off:off + k * c].reshape(N_DEV, k, c).transpose(1, 0, 2).reshape(k, N_DEV * c)
        off += k * c + (-(k * c)) % D

    loss, grad_x, g = _local_step(x[0], loss_target[0], full, shards)

    lands = [g["w_in_t"], g["w_up_t"]] + [g[n] for n in BIG[2:]]
    blocks = []
    for n in SMALL_SHARDED:
        k, c = ws[n].shape[-2], ws[n].shape[-1]
        b = g[n].reshape(k, N_DEV, c).transpose(1, 0, 2).reshape(N_DEV, k * c)
        blocks.append(jnp.pad(b, ((0, 0), (0, (-(k * c)) % D))))
    for n in REPLICATED:
        b = jnp.broadcast_to(g[n].reshape(1, -1), (N_DEV, g[n].size))
        blocks.append(jnp.pad(b, ((0, 0), (0, (-b.shape[1]) % D))))
    gsm = jnp.concatenate(blocks, axis=1)
    lands += _exchange([jnp.pad(gsm, ((0, 0), (0, SMALL_ROWS * D - gsm.shape[1]))).reshape(N_DEV, SMALL_ROWS, D)])

    grad, delta, new_m, new_v = {}, {}, {}, {}
    for i, n in enumerate(BIG):
        outs = _adamw(lands[i], _to_panel(n, ws[n]), _to_panel(n, ms[n]), _to_panel(n, vs[n]), ADAM_BLOCK[n], "adamw_" + n)
        grad[n], delta[n], new_m[n], new_v[n] = [_from_panel(n, p, ws[n].shape) for p in outs]
    outs = _adamw(lands[len(BIG)], pack_small(ws), pack_small(ms), pack_small(vs), SMALL_ROWS, "adamw_small")
    shapes = [ws[n].shape for n in small]
    for d, p in zip((grad, delta, new_m, new_v), outs):
        d.update(zip(small, _unpack_small(p, shapes)))

    order = ("meta_tokens", "norm_mix_g", "w_in", "w_alpha_up", "b_alpha", "gla_norm_g", "w_gla_o", "conf_dw_w", "conf_dw_b",
             "conf_ln_g", "conf_ln_b", "w_conf_o", "w_out", "norm_ffn_g", "w_up", "ffn_dw_w", "ffn_dw_b", "w_down",
             "final_norm_g")
    loss = lax.psum(loss, ("x", "y", "c"))
    return (loss, grad_x[None], *[grad[n] for n in order], *[delta[n] for n in order], *[new_m[n] for n in order],
            *[new_v[n] for n in order])
```

```python
import functools

import jax
import jax.numpy as jnp
from jax import lax
from jax.experimental import pallas as pl
from jax.experimental.pallas import tpu as pltpu

F32, BF16 = jnp.float32, jnp.bfloat16
SDS = jax.ShapeDtypeStruct

D = 1024
N_META = 16
HEADS = 4
DK, DKH, DV, DVH = 512, 128, 1024, 256
RANK = 16
TAU = 16.0
CONF_K = 31
DFF = 2816
FFN_K = 3
IN_WIDTHS = (DK, DK, DV, DV, RANK, 2 * D, D, D)
RMS_EPS, LN_EPS = 1e-6, 1e-5
ADAM_LR, ADAM_B1, ADAM_B2, ADAM_EPS, ADAM_WD, ADAM_STEP = 0.001, 0.9, 0.999, 1e-08, 0.01, 10

NPROJ = 7 * D
LANES = 128
CH = 128
TM = 640
TM_BIG = 1664
TE = 320
HALO = 32
HALO_F = 16
N_DEV = 8
VMEM_LIMIT = 60 * 1024 * 1024
MESH_T = pl.DeviceIdType.MESH

_ARB1 = pltpu.CompilerParams(dimension_semantics=("arbitrary",), vmem_limit_bytes=VMEM_LIMIT)
_ARB2 = pltpu.CompilerParams(dimension_semantics=("arbitrary", "arbitrary"), vmem_limit_bytes=VMEM_LIMIT)


def _dot(a, b):
    return jnp.dot(a, b, preferred_element_type=F32)


def _dot_nt(a, b):
    return lax.dot_general(a, b, (((1,), (1,)), ((), ())), preferred_element_type=F32)


def _dot_tn(a, b):
    return lax.dot_general(a, b, (((0,), (0,)), ((), ())), preferred_element_type=F32)


def _sigmoid(x):
    return 0.5 * jnp.tanh(0.5 * x) + 0.5


def _rows8(x):
    return x.reshape(x.shape[0] // 8, 8, x.shape[1]).sum(axis=0)


def _row_tile(t, preferred):
    return preferred if t % preferred == 0 else TM


def _row_loop(n_rows, rb, fn, unroll=1):
    def step(i, carry):
        fn(pl.multiple_of(i * rb, rb))
        return carry
    lax.fori_loop(0, n_rows // rb, step, 0, unroll=unroll)


def _resident(shape):
    return pl.BlockSpec(shape, lambda *_: (0,) * len(shape))


def _once(shape):
    return pl.BlockSpec(shape, lambda *_: (0,) * len(shape), pipeline_mode=pl.Buffered(1))


CONV_ROWS, CONV_LANES = 64, 256


def _shift_classes(offset_taps):
    return [(s, [(o - s, j) for o, j in offset_taps if o % 8 == s]) for s in range(8)]


def _shifted(win, s):
    return win if s == 0 else win[s:s + CONV_ROWS + HALO - 8, :]


def _my_place():
    return lax.axis_index("x"), lax.axis_index("y"), lax.axis_index("c")


def _flip(v, bit):
    return 1 - v if bit else v


def _exchange_copies(src_refs, land_refs, scatter, send_sems, recv_sems, local_sems, arrivals):
    x, y, c = _my_place()
    my_idx = 4 * x + 2 * y + c
    local, remote = [], []
    for p, (src, land) in enumerate(zip(src_refs, land_refs)):
        local.append(pltpu.make_async_copy(src.at[my_idx] if scatter[p] else src, land.at[my_idx], local_sems.at[p]))
    for k in range(1, N_DEV):
        px, py, pc = _flip(x, k & 4), _flip(y, k & 2), _flip(c, k & 1)
        p_idx = 4 * px + 2 * py + pc
        for p, (src, land) in enumerate(zip(src_refs, land_refs)):
            s = 7 * p + k - 1
            out = src.at[p_idx] if scatter[p] else src

            def copy(dst):
                return pltpu.make_async_remote_copy(src_ref=out, dst_ref=dst, send_sem=send_sems.at[s],
                                                    recv_sem=recv_sems.at[s], device_id=(px, py, pc), device_id_type=MESH_T)
            remote.append((copy(land.at[my_idx]), copy(land.at[p_idx]) if arrivals else None))
    return local, remote


def _carried_call(core, carry, *, grid, in_specs, out_specs, out_shape, scratch_shapes=(), **kw):
    n_in, n_out, nc, n_scr = len(in_specs), len(out_specs), len(carry), len(scratch_shapes)
    if nc == 0:
        return pl.pallas_call(core, grid=grid, in_specs=in_specs, out_specs=out_specs, out_shape=out_shape,
                              scratch_shapes=list(scratch_shapes), **kw)
    scatter = [sc for _, sc in carry]

    def body(*refs):
        ins, cin = refs[:n_in], refs[n_in:n_in + nc]
        outs, cout = refs[n_in + nc:n_in + nc + n_out], refs[n_in + nc + n_out:n_in + 2 * nc + n_out]
        scr, sems = refs[n_in + 2 * nc + n_out:n_in + 2 * nc + n_out + n_scr], refs[-3:]
        first = functools.reduce(jnp.logical_and, [pl.program_id(a) == 0 for a in range(len(grid))])
        last = functools.reduce(jnp.logical_and, [pl.program_id(a) == grid[a] - 1 for a in range(len(grid))])

        @pl.when(first)
        def _():
            local, remote = _exchange_copies(cin, cout, scatter, *sems, arrivals=False)
            for cp in local:
                cp.start()
            for send, _ in remote:
                send.start()

        core(*ins, *outs, *scr)

        @pl.when(last)
        def _():
            local, remote = _exchange_copies(cin, cout, scatter, *sems, arrivals=True)
            for _, arrival in remote:
                arrival.wait_recv()
            for send, _ in remote:
                send.wait_send()
            for cp in local:
                cp.wait()

    hbm = pl.BlockSpec(memory_space=pl.ANY)
    land_shape = [SDS((N_DEV, *(a.shape[1:] if sc else a.shape)), a.dtype) for a, sc in carry]
    sems = [pltpu.SemaphoreType.DMA((7 * nc,)), pltpu.SemaphoreType.DMA((7 * nc,)), pltpu.SemaphoreType.DMA((nc,))]
    call = pl.pallas_call(body, grid=grid, in_specs=list(in_specs) + [hbm] * nc, out_specs=list(out_specs) + [hbm] * nc,
                          out_shape=list(out_shape) + land_shape, scratch_shapes=list(scratch_shapes) + sems, **kw)
    return lambda *args: call(*args, *[a for a, _ in carry])


def _norm_matmul(h, g, w_t, tn, name, w_extra_t=None, carry=()):
    t, n = h.shape[0], w_t.shape[0]
    tm = _row_tile(t, TM_BIG)
    nt, nb = t // tm, n // tn

    def body(*refs):
        if w_extra_t is None:
            h_ref, g_ref, w_ref, u_ref, p_ref = refs
        else:
            h_ref, g_ref, w_ref, we_ref, u_ref, p_ref, e_ref = refs

        @pl.when(pl.program_id(1) == 0)
        def _():
            def blk(r0):
                x = h_ref[pl.ds(r0, 32), :]
                rinv = lax.rsqrt(jnp.mean(x * x, axis=-1, keepdims=True) + RMS_EPS)
                u_ref[pl.ds(r0, 32), :] = (x * rinv * g_ref[...]).astype(BF16)
            _row_loop(tm, 32, blk, unroll=2)
            if w_extra_t is not None:
                e_ref[...] = _dot_nt(u_ref[...], we_ref[...]).astype(BF16)

        p_ref[...] = _dot_nt(u_ref[...], w_ref[...]).astype(BF16)

    in_specs = [pl.BlockSpec((tm, D), lambda i, j: (i, 0)), _resident((1, D)), pl.BlockSpec((tn, D), lambda i, j: (j, 0))]
    out_specs = [pl.BlockSpec((tm, D), lambda i, j: (i, 0)), pl.BlockSpec((tm, tn), lambda i, j: (i, j))]
    out_shape = [SDS((t, D), BF16), SDS((t, n), BF16)]
    args = [h, g, w_t]
    if w_extra_t is not None:
        in_specs.append(_resident(w_extra_t.shape))
        out_specs.append(pl.BlockSpec((tm, w_extra_t.shape[0]), lambda i, j: (i, 0)))
        out_shape.append(SDS((t, w_extra_t.shape[0]), BF16))
        args.append(w_extra_t)
    return _carried_call(body, carry, name=name, grid=(nt, nb), in_specs=in_specs, out_specs=out_specs,
                         out_shape=out_shape, compiler_params=_ARB2)(*args)


def _in_proj(x, meta, g, w_t, w_a_t, t, carry=()):
    n_real = x.shape[0] + N_META
    n, tn = w_t.shape[0], 1024
    tm = _row_tile(t, TM_BIG)
    nt, nb = t // tm, n // tn

    def body(x_ref, xh_ref, m_ref, g_ref, w_ref, wa_ref, h_ref, u_ref, p_ref, e_ref):
        i = pl.program_id(0)

        @pl.when(pl.program_id(1) == 0)
        def _():
            def rows_of(r0, val):
                gid = i * tm + r0 + lax.broadcasted_iota(jnp.int32, (32, 1), 0)
                val = jnp.where(gid < n_real, val, 0.0)
                h_ref[pl.ds(r0, 32), :] = val
                rinv = lax.rsqrt(jnp.mean(val * val, axis=-1, keepdims=True) + RMS_EPS)
                u_ref[pl.ds(r0, 32), :] = (val * rinv * g_ref[...]).astype(BF16)

            before = jnp.where(i == 0, m_ref[...], xh_ref[...])
            rows_of(0, jnp.concatenate([before, x_ref[0:N_META, :]], axis=0))

            def blk(k, c):
                r0 = pl.multiple_of(k * 32, 32)
                rows_of(r0, x_ref[pl.ds(pl.multiple_of(r0 - N_META, N_META), 32), :])
                return c
            lax.fori_loop(1, tm // 32, blk, 0)
            e_ref[...] = _dot_nt(u_ref[...], wa_ref[...]).astype(BF16)

        p_ref[...] = _dot_nt(u_ref[...], w_ref[...]).astype(BF16)

    row = lambda w: pl.BlockSpec((tm, w), lambda i, j: (i, 0))
    return _carried_call(
        body, carry, name="in_proj", grid=(nt, nb),
        in_specs=[row(D), pl.BlockSpec((N_META, D), lambda i, j: (jnp.maximum(i * (tm // N_META) - 1, 0), 0)),
                  _resident((N_META, D)), _resident((1, D)), pl.BlockSpec((tn, D), lambda i, j: (j, 0)),
                  _resident(w_a_t.shape)],
        out_specs=[row(D), row(D), pl.BlockSpec((tm, tn), lambda i, j: (i, j)), row(w_a_t.shape[0])],
        out_shape=[SDS((t, D), F32), SDS((t, D), BF16), SDS((t, n), BF16), SDS((t, w_a_t.shape[0]), BF16)],
        compiler_params=_ARB2)(x, x, meta, g, w_t, w_a_t)


def _gla_decay(a_ref, wau_ref, ba_ref):
    z = _dot(a_ref[...], wau_ref[...]) + ba_ref[...]
    la = (jnp.minimum(z, 0.0) - jnp.log(1.0 + jnp.exp(-jnp.abs(z)))) * (1.0 / TAU)
    r = lax.broadcasted_iota(jnp.int32, (CH, CH), 0)
    c = lax.broadcasted_iota(jnp.int32, (CH, CH), 1)
    b = _tri_matmul((r >= c).astype(BF16), la)
    mid = jnp.broadcast_to(b[CH // 2:CH // 2 + 1, :], b.shape)
    last = jnp.broadcast_to(b[CH - 1:CH, :], b.shape)
    return z, b, mid, last, r >= c


def _tri_matmul(tri, x):
    n = x.shape[1]
    x1 = x.astype(BF16)
    r1 = x - x1.astype(F32)
    x2 = r1.astype(BF16)
    x3 = (r1 - x2.astype(F32)).astype(BF16)
    y = _dot(tri, jnp.concatenate([x1, x2, x3], axis=1))
    return y[:, 0:n] + y[:, n:2 * n] + y[:, 2 * n:3 * n]


def _gla_fwd(proj, alr, wau, balpha, gn):
    t = proj.shape[0]
    nc = t // CH

    def body(qk_ref, v_ref, r_ref, a_ref, wau_ref, ba_ref, gn_ref, o_ref, og_ref, sall_ref, s_scr):
        @pl.when(pl.program_id(0) == 0)
        def _():
            s_scr[...] = jnp.zeros_like(s_scr)

        sall_ref[0] = s_scr[...]
        _, b, bmid, blast, causal = _gla_decay(a_ref, wau_ref, ba_ref)
        for h in range(HEADS):
            ks = slice(h * DKH, (h + 1) * DKH)
            vs = slice(h * DVH, (h + 1) * DVH)
            bh, mh, lh = b[:, ks], bmid[:, ks], blast[:, ks]
            q = qk_ref[:, ks].astype(F32) * (DKH ** -0.5)
            k = qk_ref[:, DK + h * DKH:DK + (h + 1) * DKH].astype(F32)
            v = v_ref[:, vs]
            qt = (q * jnp.exp(bh - mh)).astype(BF16)
            kt = (k * jnp.exp(mh - bh)).astype(BF16)
            qg = (q * jnp.exp(bh)).astype(BF16)
            kg = (k * jnp.exp(lh - bh)).astype(BF16)
            a = jnp.where(causal, _dot_nt(qt, kt), 0.0)
            st = s_scr[vs, :]
            o = _dot(a.astype(BF16), v) + _dot_nt(qg, st.astype(BF16))
            el = jnp.exp(lh)
            s_scr[vs, :] = st * jnp.concatenate([el, el], axis=0) + _dot_tn(v, kg)
            o_ref[:, vs] = o
            on = o * lax.rsqrt(jnp.mean(o * o, axis=-1, keepdims=True) + RMS_EPS) * gn_ref[:, vs]
            rr = r_ref[:, vs].astype(F32)
            og_ref[:, vs] = (on * (rr * _sigmoid(rr))).astype(BF16)

    return pl.pallas_call(
        body, name="gla_fwd", grid=(nc,),
        in_specs=[pl.BlockSpec((CH, D), lambda c: (c, 0)), pl.BlockSpec((CH, D), lambda c: (c, 1)),
                  pl.BlockSpec((CH, D), lambda c: (c, 6)), pl.BlockSpec((CH, LANES), lambda c: (c, 0)),
                  _resident((LANES, DK)), _resident((1, DK)), _resident((1, DV))],
        out_specs=[pl.BlockSpec((CH, DV), lambda c: (c, 0)), pl.BlockSpec((CH, DV), lambda c: (c, 0)),
                   pl.BlockSpec((1, DV, DKH), lambda c: (c, 0, 0))],
        out_shape=[SDS((t, DV), F32), SDS((t, DV), BF16), SDS((nc, DV, DKH), F32)],
        scratch_shapes=[pltpu.VMEM((DV, DKH), F32)], compiler_params=_ARB1)(proj, proj, proj, alr, wau, balpha, gn)


def _conf_fwd(proj, cw, cb, lg, lb, carry=()):
    t = proj.shape[0]
    nt = t // TE

    def body(c1_ref, c2_ref, cw_ref, cb_ref, lg_ref, lb_ref, cc_ref, cs_ref, cext):
        i = pl.program_id(0)

        @pl.when(i == 0)
        def _():
            cext[0:HALO, :] = jnp.zeros((HALO, D), F32)

        @pl.when(i > 0)
        def _():
            cext[0:HALO, :] = cext[TE:TE + HALO, :]

        def glu(r0):
            c2 = c2_ref[pl.ds(r0, 32), :].astype(F32)
            cext[pl.ds(HALO + r0, 32), :] = c1_ref[pl.ds(r0, 32), :].astype(F32) * _sigmoid(c2)
        _row_loop(TE, 32, glu)

        def conv(r0):
            for part in range(D // CONV_LANES):
                cols = slice(part * CONV_LANES, (part + 1) * CONV_LANES)
                win = cext[pl.ds(r0, CONV_ROWS + HALO), cols]
                acc = jnp.zeros((CONV_ROWS, CONV_LANES), F32) + cb_ref[:, cols]
                for s, taps in _shift_classes([(2 + j, j) for j in range(CONF_K)]):
                    ws = _shifted(win, s)
                    for a8, j in taps:
                        acc = acc + jnp.tile(cw_ref[j, :, cols], (CONV_ROWS // 8, 1)) * ws[a8:a8 + CONV_ROWS, :]
                cc_ref[pl.ds(r0, CONV_ROWS), cols] = acc
            for sub in range(CONV_ROWS // 32):
                rows = pl.ds(r0 + 32 * sub, 32)
                x = cc_ref[rows, :]
                xc = x - jnp.mean(x, axis=-1, keepdims=True)
                var = jnp.mean(xc * xc, axis=-1, keepdims=True)
                ln = xc * lax.rsqrt(var + LN_EPS) * lg_ref[...] + lb_ref[...]
                cs_ref[rows, :] = (ln * _sigmoid(ln)).astype(BF16)
        _row_loop(TE, CONV_ROWS, conv)

    return _carried_call(
        body, carry, name="conf_fwd", grid=(nt,),
        in_specs=[pl.BlockSpec((TE, D), lambda i: (i, 2)), pl.BlockSpec((TE, D), lambda i: (i, 3)),
                  _resident((32, 8, D)), _resident((1, D)), _resident((1, D)), _resident((1, D))],
        out_specs=[pl.BlockSpec((TE, D), lambda i: (i, 0)), pl.BlockSpec((TE, D), lambda i: (i, 0))],
        out_shape=[SDS((t, D), F32), SDS((t, D), BF16)],
        scratch_shapes=[pltpu.VMEM((TE + HALO, D), F32)], compiler_params=_ARB1)(proj, proj, cw, cb, lg, lb)


def _mix_fwd(og, cs, proj, h0, wg, wc, wo):
    t = h0.shape[0]
    nt = t // TM

    def body(og_ref, cs_ref, g_ref, h0_ref, wg_ref, wc_ref, wo_ref, brg_ref, brc_ref, mg_ref, h1_ref):
        brg_ref[...] = _dot(og_ref[...], wg_ref[...]).astype(BF16)
        brc_ref[...] = _dot(cs_ref[...], wc_ref[...]).astype(BF16)

        def blk(r0):
            rows = pl.ds(r0, 32)
            gg = g_ref[rows, 0:D].astype(F32)
            gc = g_ref[rows, D:2 * D].astype(F32)
            m = _sigmoid(gg) * brg_ref[rows, :].astype(F32) + _sigmoid(gc) * brc_ref[rows, :].astype(F32)
            mg_ref[rows, :] = m.astype(BF16)
        _row_loop(TM, 32, blk)
        h1_ref[...] = h0_ref[...] + _dot(mg_ref[...], wo_ref[...])

    row = lambda w: pl.BlockSpec((TM, w), lambda i: (i, 0))
    return pl.pallas_call(
        body, name="mix_fwd", grid=(nt,),
        in_specs=[row(D), row(D), pl.BlockSpec((TM, 2 * D), lambda i: (i, 2)), row(D),
                  _once((D, D)), _once((D, D)), _once((D, D))],
        out_specs=[row(D), row(D), row(D), row(D)],
        out_shape=[SDS((t, D), BF16), SDS((t, D), BF16), SDS((t, D), BF16), SDS((t, D), F32)],
        compiler_params=_ARB1)(og, cs, proj, h0, wg, wc, wo)


def _ffn_out(up, fw, h1, wd, gf, target):
    t = h1.shape[0]
    nt = t // TE
    n_real = target.shape[0] + N_META

    def body(a_ref, bv_ref, fw_ref, h1_ref, wd_ref, gf_ref, tg_ref, tb_ref, f_ref, dh2_ref, red_ref, aext, hs):
        i = pl.program_id(0)

        @pl.when(i == 0)
        def _():
            aext[0:HALO_F, :] = jnp.zeros((HALO_F, DFF), F32)
            red_ref[...] = jnp.zeros_like(red_ref)

        @pl.when(i > 0)
        def _():
            aext[0:HALO_F, :] = aext[TE:TE + HALO_F, :]

        def cp(r0):
            aext[pl.ds(HALO_F + r0, 16), :] = a_ref[pl.ds(r0, 16), :].astype(F32)
        _row_loop(TE, 16, cp)

        def conv(r0):
            win = aext[pl.ds(r0, 32), :]
            ac = fw_ref[3] + fw_ref[0] * win[14:30, :] + fw_ref[1] * win[15:31, :] + fw_ref[2] * win[16:32, :]
            f_ref[pl.ds(r0, 16), :] = (ac * _sigmoid(ac) * bv_ref[pl.ds(r0, 16), :].astype(F32)).astype(BF16)
        _row_loop(TE, 16, conv)
        hs[...] = h1_ref[...] + _dot(f_ref[...], wd_ref[...])

        def head(r0, tg):
            rows = pl.ds(r0, 32)
            h2 = hs[rows, :]
            rinv = lax.rsqrt(jnp.mean(h2 * h2, axis=-1, keepdims=True) + RMS_EPS)
            hh = h2 * rinv
            gid = i * TE + r0 + lax.broadcasted_iota(jnp.int32, (32, 1), 0)
            live = jnp.logical_and(gid >= N_META, gid < n_real)
            err = jnp.where(live, hh * gf_ref[...] - tg, 0.0)
            dy = err * (1.0 / D)
            red_ref[0:8, :] += _rows8(err * err)
            red_ref[8:16, :] += _rows8(dy * hh)
            dhh = dy * gf_ref[...]
            dh2_ref[rows, :] = rinv * (dhh - hh * jnp.mean(dhh * hh, axis=-1, keepdims=True))

        head(0, jnp.concatenate([tb_ref[...], tg_ref[0:N_META, :]], axis=0))

        def blk(k, c):
            r0 = pl.multiple_of(k * 32, 32)
            head(r0, tg_ref[pl.ds(pl.multiple_of(r0 - N_META, N_META), 32), :])
            return c
        lax.fori_loop(1, TE // 32, blk, 0, unroll=3)

    row = lambda w: pl.BlockSpec((TE, w), lambda i: (i, 0))
    return pl.pallas_call(
        body, name="ffn_out", grid=(nt,),
        in_specs=[pl.BlockSpec((TE, DFF), lambda i: (i, 0)), pl.BlockSpec((TE, DFF), lambda i: (i, 1)),
                  _resident((4, 16, DFF)), row(D), _resident((DFF, D)), _resident((1, D)), row(D),
                  pl.BlockSpec((N_META, D), lambda i: (jnp.maximum(i * (TE // N_META) - 1, 0), 0))],
        out_specs=[row(DFF), row(D), _resident((16, D))],
        out_shape=[SDS((t, DFF), BF16), SDS((t, D), F32), SDS((16, D), F32)],
        scratch_shapes=[pltpu.VMEM((TE + HALO_F, DFF), F32), pltpu.VMEM((TE, D), F32)],
        compiler_params=_ARB1)(up, up, fw, h1, wd, gf, target, target)


def _ffn_bwd(dh2, wd, up, fw):
    t = dh2.shape[0]
    nt = t // TE
    hb = TE // HALO_F

    def body(dh_ref, wd_ref, a_ref, ah_ref, bv_ref, fw_ref, dup_ref, dw_ref, aext, dax, dfs):
        i = pl.program_id(0)
        ti = nt - 1 - i

        @pl.when(i == 0)
        def _():
            dax[TE:TE + HALO_F, :] = jnp.zeros((HALO_F, DFF), F32)
            dw_ref[...] = jnp.zeros_like(dw_ref)

        @pl.when(i > 0)
        def _():
            dax[TE:TE + HALO_F, :] = dax[0:HALO_F, :]

        aext[0:HALO_F, :] = jnp.where(ti > 0, ah_ref[...].astype(F32), 0.0)
        dfs[...] = _dot_nt(dh_ref[...].astype(BF16), wd_ref[...])

        def cp(r0):
            aext[pl.ds(HALO_F + r0, 16), :] = a_ref[pl.ds(r0, 16), :].astype(F32)
        _row_loop(TE, 16, cp)

        def act(r0):
            rows = pl.ds(r0, 16)
            win = aext[pl.ds(r0, 32), :]
            ac = fw_ref[3] + fw_ref[0] * win[14:30, :] + fw_ref[1] * win[15:31, :] + fw_ref[2] * win[16:32, :]
            sg = _sigmoid(ac)
            df = dfs[rows, :]
            dup_ref[rows, DFF:2 * DFF] = (df * ac * sg).astype(BF16)
            dac = df * bv_ref[rows, :].astype(F32) * sg * (1.0 + ac * (1.0 - sg))
            dax[rows, :] = dac
            dw_ref[3] += _rows8(dac)
            for j in range(FFN_K):
                dw_ref[j] += _rows8(dac * win[14 + j:30 + j, :])
        _row_loop(TE, 16, act)

        def convt(r0):
            win = dax[pl.ds(r0, 32), :]
            da = fw_ref[2] * win[0:16, :] + fw_ref[1] * win[1:17, :] + fw_ref[0] * win[2:18, :]
            dup_ref[pl.ds(r0, 16), 0:DFF] = da.astype(BF16)
        _row_loop(TE, 16, convt)

    rev = lambda w: pl.BlockSpec((TE, w), lambda i: (nt - 1 - i, 0))
    return pl.pallas_call(
        body, name="ffn_bwd", grid=(nt,),
        in_specs=[rev(D), _resident((DFF, D)), rev(DFF),
                  pl.BlockSpec((HALO_F, DFF), lambda i: (jnp.maximum((nt - 1 - i) * hb - 1, 0), 0)),
                  pl.BlockSpec((TE, DFF), lambda i: (nt - 1 - i, 1)), _resident((4, 16, DFF))],
        out_specs=[rev(2 * DFF), _resident((4, 8, DFF))],
        out_shape=[SDS((t, 2 * DFF), BF16), SDS((4, 8, DFF), F32)],
        scratch_shapes=[pltpu.VMEM((TE + HALO_F, DFF), F32), pltpu.VMEM((TE + HALO_F, DFF), F32),
                        pltpu.VMEM((TE, DFF), F32)],
        compiler_params=_ARB1)(dh2, wd, up, up, up, fw)


def _dgrad_norm(dy, w_t, h, g, dres, name, dy_extra=None, w_extra_t=None, carry=()):
    t, k = dy.shape
    nt = t // TM

    def body(*refs):
        if dy_extra is None:
            dy_ref, w_ref, h_ref, g_ref, dr_ref, dh_ref, dg_ref, acc = refs
        else:
            dy_ref, w_ref, h_ref, g_ref, dr_ref, de_ref, we_ref, dh_ref, dg_ref, acc = refs

        @pl.when(pl.program_id(0) == 0)
        def _():
            dg_ref[...] = jnp.zeros_like(dg_ref)

        acc[...] = _dot(dy_ref[...], w_ref[...])
        if dy_extra is not None:
            acc[...] += _dot(de_ref[...], we_ref[...])

        def blk(r0):
            rows = pl.ds(r0, 32)
            x = h_ref[rows, :]
            rinv = lax.rsqrt(jnp.mean(x * x, axis=-1, keepdims=True) + RMS_EPS)
            hh = x * rinv
            du = acc[rows, :]
            dg_ref[...] += _rows8(du * hh)
            dhh = du * g_ref[...]
            dh_ref[rows, :] = dr_ref[rows, :] + rinv * (dhh - hh * jnp.mean(dhh * hh, axis=-1, keepdims=True))
        _row_loop(TM, 32, blk, unroll=2)

    row = pl.BlockSpec((TM, D), lambda i: (i, 0))
    in_specs = [pl.BlockSpec((TM, k), lambda i: (i, 0)), _once((k, D)), row, _resident((1, D)), row]
    args = [dy, w_t, h, g, dres]
    if dy_extra is not None:
        in_specs += [pl.BlockSpec((TM, dy_extra.shape[1]), lambda i: (i, 0)), _once(w_extra_t.shape)]
        args += [dy_extra, w_extra_t]
    return _carried_call(
        body, carry, name=name, grid=(nt,), in_specs=in_specs, out_specs=[row, _resident((8, D))],
        out_shape=[SDS((t, D), F32), SDS((8, D), F32)],
        scratch_shapes=[pltpu.VMEM((TM, D), F32)], compiler_params=_ARB1)(*args)


def _wgrad(x, dy, tk, name):
    t, k = x.shape
    n = dy.shape[1]
    tm = _row_tile(t, TM_BIG)
    nk, nt = k // tk, t // tm

    def body(x_ref, dy_ref, o_ref, acc):
        @pl.when(pl.program_id(1) == 0)
        def _():
            acc[...] = jnp.zeros_like(acc)
        acc[...] += _dot_tn(x_ref[...], dy_ref[...].astype(BF16))

        @pl.when(pl.program_id(1) == nt - 1)
        def _():
            o_ref[...] = acc[...].astype(BF16)

    return pl.pallas_call(
        body, name=name, grid=(nk, nt),
        in_specs=[pl.BlockSpec((tm, tk), lambda j, i: (i, j)), pl.BlockSpec((tm, n), lambda j, i: (i, 0))],
        out_specs=pl.BlockSpec((tk, n), lambda j, i: (j, 0)), out_shape=SDS((k, n), BF16),
        scratch_shapes=[pltpu.VMEM((tk, n), F32)], compiler_params=_ARB2)(x, dy)


def _mix_bwd(dh1, wo, wg, wc, proj, brg, brc):
    t = dh1.shape[0]
    nt = t // TM

    def body(dh_ref, wo_ref, wg_ref, wc_ref, g_ref, brg_ref, brc_ref, dbg_ref, dbc_ref, dog_ref, dcs_ref, dp_ref, dm):
        dm[...] = _dot_nt(dh_ref[...].astype(BF16), wo_ref[...])

        def blk(r0):
            rows = pl.ds(r0, 32)
            d = dm[rows, :]
            sg = _sigmoid(g_ref[rows, 0:D].astype(F32))
            sc = _sigmoid(g_ref[rows, D:2 * D].astype(F32))
            dbg_ref[rows, :] = (d * sg).astype(BF16)
            dbc_ref[rows, :] = (d * sc).astype(BF16)
            dp_ref[rows, 0:D] = (d * brg_ref[rows, :].astype(F32) * sg * (1.0 - sg)).astype(BF16)
            dp_ref[rows, D:2 * D] = (d * brc_ref[rows, :].astype(F32) * sc * (1.0 - sc)).astype(BF16)
        _row_loop(TM, 32, blk)
        dog_ref[...] = _dot_nt(dbg_ref[...], wg_ref[...]).astype(BF16)
        dcs_ref[...] = _dot_nt(dbc_ref[...], wc_ref[...]).astype(BF16)

    row = pl.BlockSpec((TM, D), lambda i: (i, 0))
    wide = pl.BlockSpec((TM, 2 * D), lambda i: (i, 2))
    return pl.pallas_call(
        body, name="mix_bwd", grid=(nt,),
        in_specs=[row, _once((D, D)), _once((D, D)), _once((D, D)), wide, row, row],
        out_specs=[row, row, row, row, wide],
        out_shape=[SDS((t, D), BF16)] * 4 + [SDS((t, NPROJ), BF16)],
        scratch_shapes=[pltpu.VMEM((TM, D), F32)], compiler_params=_ARB1)(dh1, wo, wg, wc, proj, brg, brc)


def _glapost_bwd(dog, o, proj, gn, dproj):
    t = o.shape[0]
    nt = t // TE

    def body(dog_ref, o_ref, r_ref, gn_ref, dp_in, do_ref, dp_ref, dgn_ref):
        del dp_in

        @pl.when(pl.program_id(0) == 0)
        def _():
            dgn_ref[...] = jnp.zeros_like(dgn_ref)

        def blk(r0):
            rows = pl.ds(r0, 32)
            for h in range(HEADS):
                vs = slice(h * DVH, (h + 1) * DVH)
                x = o_ref[rows, vs]
                rinv = lax.rsqrt(jnp.mean(x * x, axis=-1, keepdims=True) + RMS_EPS)
                oh = x * rinv
                g = gn_ref[:, vs]
                rr = r_ref[rows, vs].astype(F32)
                sr = _sigmoid(rr)
                d = dog_ref[rows, vs].astype(F32)
                dp_ref[rows, vs] = (d * oh * g * sr * (1.0 + rr * (1.0 - sr))).astype(BF16)
                don = d * rr * sr
                dgn_ref[:, vs] += _rows8(don * oh)
                doh = don * g
                do_ref[rows, vs] = (rinv * (doh - oh * jnp.mean(doh * oh, axis=-1, keepdims=True))).astype(BF16)
        _row_loop(TE, 32, blk)

    row = pl.BlockSpec((TE, D), lambda i: (i, 0))
    rcol = pl.BlockSpec((TE, D), lambda i: (i, 6))
    return pl.pallas_call(
        body, name="glapost_bwd", grid=(nt,),
        in_specs=[row, row, rcol, _resident((1, D)), pl.BlockSpec(memory_space=pl.ANY)],
        out_specs=[row, rcol, _resident((8, D))],
        out_shape=[SDS((t, D), BF16), SDS((t, NPROJ), BF16), SDS((8, D), F32)],
        input_output_aliases={4: 1}, compiler_params=_ARB1)(dog, o, proj, gn, dproj)


def _conf_bwd(dcs, cc, proj, cw, lg, lb, dproj, carry=()):
    t = cc.shape[0]
    nt = t // TE
    hb = TE // HALO

    def body(dcs_ref, cc_ref, c1_ref, c2_ref, c1h_ref, c2h_ref, cw_ref, lg_ref, lb_ref, dp_in,
             dp_ref, dw_ref, ds_ref, cext, dext):
        del dp_in
        i = pl.program_id(0)
        ti = nt - 1 - i

        @pl.when(i == 0)
        def _():
            dext[TE:TE + HALO, :] = jnp.zeros((HALO, D), F32)
            dw_ref[...] = jnp.zeros_like(dw_ref)
            ds_ref[...] = jnp.zeros_like(ds_ref)

        @pl.when(i > 0)
        def _():
            dext[TE:TE + HALO, :] = dext[0:HALO, :]

        ch = c1h_ref[...].astype(F32) * _sigmoid(c2h_ref[...].astype(F32))
        cext[0:HALO, :] = jnp.where(ti > 0, ch, 0.0)

        def pre(r0):
            rows = pl.ds(r0, 32)
            cext[pl.ds(HALO + r0, 32), :] = c1_ref[rows, :].astype(F32) * _sigmoid(c2_ref[rows, :].astype(F32))
            x = cc_ref[rows, :]
            mu = jnp.mean(x, axis=-1, keepdims=True)
            xc = x - mu
            rstd = lax.rsqrt(jnp.mean(xc * xc, axis=-1, keepdims=True) + LN_EPS)
            xh = xc * rstd
            ln = xh * lg_ref[...] + lb_ref[...]
            sg = _sigmoid(ln)
            dln = dcs_ref[rows, :].astype(F32) * sg * (1.0 + ln * (1.0 - sg))
            ds_ref[0] += _rows8(dln * xh)
            ds_ref[1] += _rows8(dln)
            dxh = dln * lg_ref[...]
            dcc = rstd * (dxh - jnp.mean(dxh, axis=-1, keepdims=True) - xh * jnp.mean(dxh * xh, axis=-1, keepdims=True))
            dext[rows, :] = dcc
            ds_ref[2] += _rows8(dcc)
        _row_loop(TE, 32, pre, unroll=2)

        def convt(r0):
            rows = pl.ds(r0, CONV_ROWS)
            for part in range(D // CONV_LANES):
                cols = slice(part * CONV_LANES, (part + 1) * CONV_LANES)
                wd = dext[pl.ds(r0, CONV_ROWS + HALO), cols]
                dc = jnp.zeros((CONV_ROWS, CONV_LANES), F32)
                for s, taps in _shift_classes([(CONF_K - 1 - j, j) for j in range(CONF_K)]):
                    ws = _shifted(wd, s)
                    for a8, j in taps:
                        dc = dc + jnp.tile(cw_ref[j, :, cols], (CONV_ROWS // 8, 1)) * ws[a8:a8 + CONV_ROWS, :]
                dcc = wd[0:CONV_ROWS, :]
                wc = cext[pl.ds(r0, CONV_ROWS + HALO), cols]
                for s, taps in _shift_classes([(2 + j, j) for j in range(CONF_K)]):
                    ws = _shifted(wc, s)
                    for a8, j in taps:
                        dw_ref[j, :, cols] += _rows8(dcc * ws[a8:a8 + CONV_ROWS, :])
                c1 = c1_ref[rows, cols].astype(F32)
                s2 = _sigmoid(c2_ref[rows, cols].astype(F32))
                dp_ref[rows, cols] = (dc * s2).astype(BF16)
                dp_ref[rows, D + part * CONV_LANES:D + (part + 1) * CONV_LANES] = (dc * c1 * s2 * (1.0 - s2)).astype(BF16)
        _row_loop(TE, CONV_ROWS, convt)

    rev = lambda col: pl.BlockSpec((TE, D), lambda i: (nt - 1 - i, col))
    halo = lambda col: pl.BlockSpec((HALO, D), lambda i: (jnp.maximum((nt - 1 - i) * hb - 1, 0), col))
    return _carried_call(
        body, carry, name="conf_bwd", grid=(nt,),
        in_specs=[rev(0), rev(0), rev(2), rev(3), halo(2), halo(3), _resident((32, 8, D)), _resident((1, D)),
                  _resident((1, D)), pl.BlockSpec(memory_space=pl.ANY)],
        out_specs=[pl.BlockSpec((TE, 2 * D), lambda i: (nt - 1 - i, 1)), _resident((32, 8, D)), _resident((3, 8, D))],
        out_shape=[SDS((t, NPROJ), BF16), SDS((32, 8, D), F32), SDS((3, 8, D), F32)],
        scratch_shapes=[pltpu.VMEM((TE + HALO, D), F32), pltpu.VMEM((TE + HALO, D), F32)],
        input_output_aliases={9: 0}, compiler_params=_ARB1)(dcs, cc, proj, proj, proj, proj, cw, lg, lb, dproj)


def _gla_bwd(proj, alr, wau, balpha, do, sall, dproj, carry=()):
    t = proj.shape[0]
    nc = t // CH

    def body(qk_ref, v_ref, a_ref, wau_ref, ba_ref, do_ref, s_ref, dp_in, dp_ref, da_ref, dwau_ref, dba_ref, ds_scr, dla_scr):
        del dp_in

        @pl.when(pl.program_id(0) == 0)
        def _():
            ds_scr[...] = jnp.zeros_like(ds_scr)
            dwau_ref[...] = jnp.zeros_like(dwau_ref)
            dba_ref[...] = jnp.zeros_like(dba_ref)

        z, b, bmid, blast, causal = _gla_decay(a_ref, wau_ref, ba_ref)
        dlasts = []
        for h in range(HEADS):
            ks = slice(h * DKH, (h + 1) * DKH)
            vs = slice(h * DVH, (h + 1) * DVH)
            bh, mh, lh = b[:, ks], bmid[:, ks], blast[:, ks]
            q = qk_ref[:, ks].astype(F32) * (DKH ** -0.5)
            k = qk_ref[:, DK + h * DKH:DK + (h + 1) * DKH].astype(F32)
            v = v_ref[:, vs]
            dout = do_ref[:, vs]
            eq, ek, eb, eg, el = jnp.exp(bh - mh), jnp.exp(mh - bh), jnp.exp(bh), jnp.exp(lh - bh), jnp.exp(lh)
            qt, kt = (q * eq).astype(BF16), (k * ek).astype(BF16)
            qg, kg = (q * eb).astype(BF16), (k * eg).astype(BF16)
            st = s_ref[0, vs, :]
            dsn = ds_scr[vs, :]
            st16, dsn16 = st.astype(BF16), dsn.astype(BF16)
            a = jnp.where(causal, _dot_nt(qt, kt), 0.0).astype(BF16)
            da = jnp.where(causal, _dot_nt(dout, v), 0.0).astype(BF16)
            dq_inter = _dot(dout, st16) * eb
            dk_inter = _dot(v, dsn16) * eg
            dq = _dot(da, kt) * eq + dq_inter
            dk = _dot_tn(da, qt) * ek + dk_inter
            dv = _dot_tn(a, dout) + _dot_nt(kg, dsn16)
            dlasts.append(jnp.sum(k * dk_inter, axis=0, keepdims=True) + jnp.sum(st * dsn, axis=0, keepdims=True) * el[0:1, :])
            dla_scr[:, ks] = q * dq - k * dk
            ds_scr[vs, :] = dsn * jnp.concatenate([el, el], axis=0) + _dot_tn(dout, qg)
            dp_ref[:, ks] = (dq * (DKH ** -0.5)).astype(BF16)
            dp_ref[:, DK + h * DKH:DK + (h + 1) * DKH] = dk.astype(BF16)
            dp_ref[:, D + h * DVH:D + (h + 1) * DVH] = dv.astype(BF16)
        r = lax.broadcasted_iota(jnp.int32, (CH, CH), 0)
        c = lax.broadcasted_iota(jnp.int32, (CH, CH), 1)
        dla = _tri_matmul((r <= c).astype(BF16), dla_scr[...]) + jnp.concatenate(dlasts, axis=1)
        dz = (dla * (1.0 / TAU) * _sigmoid(-z)).astype(BF16)
        da_ref[...] = _dot_nt(dz, wau_ref[...]).astype(BF16)
        dwau_ref[...] += _dot_tn(a_ref[...], dz)
        dba_ref[...] += _rows8(dz.astype(F32))

    rev = lambda w, col: pl.BlockSpec((CH, w), lambda c: (nc - 1 - c, col))
    return _carried_call(
        body, carry, name="gla_bwd", grid=(nc,),
        in_specs=[rev(D, 0), rev(D, 1), rev(LANES, 0), _resident((LANES, DK)), _resident((1, DK)), rev(D, 0),
                  pl.BlockSpec((1, DV, DKH), lambda c: (nc - 1 - c, 0, 0)), pl.BlockSpec(memory_space=pl.ANY)],
        out_specs=[rev(2 * D, 0), rev(LANES, 0), _resident((LANES, DK)), _resident((8, DK))],
        out_shape=[SDS((t, NPROJ), BF16), SDS((t, LANES), BF16), SDS((LANES, DK), F32), SDS((8, DK), F32)],
        scratch_shapes=[pltpu.VMEM((DV, DKH), F32), pltpu.VMEM((CH, DK), F32)],
        input_output_aliases={7: 0}, compiler_params=_ARB1)(proj, proj, alr, wau, balpha, do, sall, dproj)


def _all_gather(xs, name):
    n = len(xs)

    def body(*refs):
        x_refs, out_refs = refs[:n], refs[n:2 * n]
        send_sems, recv_sems = refs[2 * n:]
        x, y, c = _my_place()
        me, sibling = (x, y, c), (x, y, 1 - c)
        chips = [(1 - x, y), (x, 1 - y), (1 - x, 1 - y)]

        def slot(p, px, py, pc):
            return out_refs[p].at[4 * px + 2 * py + pc]

        def copy(p, k, block, to, src=None):
            return pltpu.make_async_remote_copy(
                src_ref=slot(p, *block) if src is None else src, dst_ref=slot(p, *block),
                send_sem=send_sems.at[7 * p + k], recv_sem=recv_sems.at[7 * p + k], device_id=to, device_id_type=MESH_T)

        first = []
        for p in range(n):
            first.append(copy(p, 0, me, sibling, src=x_refs[p]))
            first += [copy(p, 1 + j, me, (*chip, c), src=x_refs[p]) for j, chip in enumerate(chips)]
        for cp in first:
            cp.start()
        passed = []
        for p in range(n):
            for j, chip in enumerate(chips):
                copy(p, 1 + j, (*chip, c), me).wait_recv()
                fwd = copy(p, 4 + j, (*chip, c), sibling)
                fwd.start()
                passed.append(fwd)
        for p in range(n):
            copy(p, 0, sibling, me).wait_recv()
            for j, chip in enumerate(chips):
                copy(p, 4 + j, (*chip, 1 - c), me).wait_recv()
        for cp in first + passed:
            cp.wait_send()

    hbm = pl.BlockSpec(memory_space=pl.ANY)
    return pl.pallas_call(
        body, name=name, out_shape=[SDS((N_DEV, *a.shape), a.dtype) for a in xs],
        in_specs=[hbm] * n, out_specs=[hbm] * n,
        scratch_shapes=[pltpu.SemaphoreType.DMA((7 * n,)), pltpu.SemaphoreType.DMA((7 * n,))])(*xs)


def _exchange(gs):
    n = len(gs)

    def body(*refs):
        g_refs, land_refs = refs[:n], refs[n:2 * n]
        send_sems, recv_sems, local_sems = refs[2 * n:]
        x, y, c = _my_place()
        my_idx = 4 * x + 2 * y + c
        mine = [pltpu.make_async_copy(g_refs[p].at[my_idx], land_refs[p].at[my_idx], local_sems.at[p]) for p in range(n)]
        for cp in mine:
            cp.start()
        copies = []
        for k in range(1, N_DEV):
            px, py, pc = _flip(x, k & 4), _flip(y, k & 2), _flip(c, k & 1)
            p_idx = 4 * px + 2 * py + pc
            for p in range(n):
                s = 7 * p + k - 1
                cp = pltpu.make_async_remote_copy(
                    src_ref=g_refs[p].at[p_idx], dst_ref=land_refs[p].at[my_idx], send_sem=send_sems.at[s],
                    recv_sem=recv_sems.at[s], device_id=(px, py, pc), device_id_type=MESH_T)
                cp.start()
                arrival = pltpu.make_async_remote_copy(
                    src_ref=g_refs[p].at[p_idx], dst_ref=land_refs[p].at[p_idx], send_sem=send_sems.at[s],
                    recv_sem=recv_sems.at[s], device_id=(px, py, pc), device_id_type=MESH_T)
                copies.append((cp, arrival))
        for cp, arrival in copies:
            arrival.wait_recv()
        for cp, arrival in copies:
            cp.wait_send()
        for cp in mine:
            cp.wait()

    hbm = pl.BlockSpec(memory_space=pl.ANY)
    return pl.pallas_call(
        body, name="grad_exchange", out_shape=[SDS(g.shape, g.dtype) for g in gs],
        in_specs=[hbm] * n, out_specs=[hbm] * n,
        scratch_shapes=[pltpu.SemaphoreType.DMA((7 * n,)), pltpu.SemaphoreType.DMA((7 * n,)),
                        pltpu.SemaphoreType.DMA((n,))])(*gs)


def _adamw(land, w, m, v, rows_blk, name):
    rows = w.shape[0]

    def body(l_ref, w_ref, m_ref, v_ref, g_ref, d_ref, nm_ref, nv_ref):
        g = l_ref[0].astype(F32)
        for s in range(1, N_DEV):
            g = g + l_ref[s].astype(F32)
        nm = ADAM_B1 * m_ref[...] + (1.0 - ADAM_B1) * g
        nv = ADAM_B2 * v_ref[...] + (1.0 - ADAM_B2) * (g * g)
        m_hat = nm / (1.0 - ADAM_B1 ** ADAM_STEP)
        v_hat = nv / (1.0 - ADAM_B2 ** ADAM_STEP)
        g_ref[...] = g
        d_ref[...] = -ADAM_LR * (m_hat / (jnp.sqrt(v_hat) + ADAM_EPS) + ADAM_WD * w_ref[...])
        nm_ref[...] = nm
        nv_ref[...] = nv

    blk = pl.BlockSpec((rows_blk, D), lambda i: (i, 0))
    return pl.pallas_call(
        body, name=name, grid=(rows // rows_blk,),
        in_specs=[pl.BlockSpec((N_DEV, rows_blk, D), lambda i: (0, i, 0)), blk, blk, blk],
        out_specs=[blk] * 4, out_shape=[SDS((rows, D), F32)] * 4, compiler_params=_ARB1)(land, w, m, v)


BIG = ("w_in", "w_up", "w_down", "w_gla_o", "w_conf_o", "w_out")
BIG_TRANSPOSED = ("w_in", "w_up")
SMALL_SHARDED = ("meta_tokens", "conf_dw_w", "ffn_dw_w", "w_alpha_up")
REPLICATED = ("norm_mix_g", "b_alpha", "gla_norm_g", "conf_dw_b", "conf_ln_g", "conf_ln_b", "norm_ffn_g", "ffn_dw_b",
              "final_norm_g")
N_IN = sum(IN_WIDTHS)
W_IN_ROWS = N_IN // N_DEV
W_IN_PAD = -(-W_IN_ROWS // 16) * 16
ADAM_BLOCK = {"w_in": W_IN_PAD // 3, "w_up": 176, "w_down": 176, "w_gla_o": 128, "w_conf_o": 128, "w_out": 128}
SMALL_ROWS = 32


def _to_panel(name, shard):
    a = shard.reshape(shard.shape[-2], shard.shape[-1])
    if name in BIG_TRANSPOSED:
        a = a.T
    if name == "w_in":
        a = jnp.pad(a, ((0, W_IN_PAD - W_IN_ROWS), (0, 0)))
    return a


def _from_panel(name, panel, shape):
    a = panel[0:W_IN_ROWS] if name == "w_in" else panel
    if name in BIG_TRANSPOSED:
        a = a.T
    return a.reshape(shape)


def _pack_small(arrs):
    flat = jnp.concatenate([jnp.pad(a.reshape(-1), (0, (-a.size) % D)) for a in arrs])
    return jnp.pad(flat, (0, SMALL_ROWS * D - flat.shape[0])).reshape(SMALL_ROWS, D)


def _unpack_small(panel, shapes):
    flat, out, off = panel.reshape(-1), [], 0
    for shp in shapes:
        n = 1
        for s in shp:
            n *= s
        out.append(flat[off:off + n].reshape(shp))
        off += n + (-n) % D
    return out


def _local_step(x, target, w, shards=None):
    dist = shards is not None
    w = dict(w)

    def gather(names):
        return [(shards[n], False) for n in names] if dist else []

    def scatter(*arrs):
        return [(a.reshape(N_DEV, -1, D), True) for a in arrs] if dist else []

    s = x.shape[0]
    n_real = s + N_META
    t = -(-n_real // TM) * TM

    q0, r0, a0, c0 = 0, 2 * DK + DV, 2 * DK + 2 * DV, 2 * DK + 2 * DV + RANK
    wt = w["w_in_t"]
    w_main = jnp.concatenate([wt[q0:r0], wt[c0:N_IN], wt[r0:a0]], axis=0)
    w_a = jnp.pad(wt[a0:c0], ((0, LANES - RANK), (0, 0)))
    wau = jnp.pad(w["w_alpha_up"].astype(BF16), ((0, LANES - RANK), (0, 0)))
    row = lambda name: w[name].reshape(1, -1)
    cw = jnp.broadcast_to(jnp.pad(w["conf_dw_w"], ((0, 32 - CONF_K), (0, 0)))[:, None, :], (32, 8, D))
    fw = jnp.broadcast_to(jnp.concatenate([w["ffn_dw_w"], w["ffn_dw_b"].reshape(1, -1)], axis=0)[:, None, :],
                          (FFN_K + 1, 16, DFF))

    early = ("w_gla_o", "w_conf_o", "w_out", "w_up")
    h0, u1, proj, alr, *landed = _in_proj(x, w["meta_tokens"], row("norm_mix_g"), w_main, w_a, t, carry=gather(early))
    for n, land in zip(early, landed):
        w["w_up_t" if n == "w_up" else n] = land.reshape(-1, D)
    o, og, sall = _gla_fwd(proj, alr, wau, row("b_alpha"), row("gla_norm_g"))
    cc, cs, *landed = _conf_fwd(proj, cw, row("conf_dw_b"), row("conf_ln_g"), row("conf_ln_b"), carry=gather(("w_down",)))
    if dist:
        w["w_down"] = landed[0].reshape(-1, D)
    brg, brc, merged, h1 = _mix_fwd(og, cs, proj, h0, w["w_gla_o"], w["w_conf_o"], w["w_out"])
    u2, up = _norm_matmul(h1, row("norm_ffn_g"), w["w_up_t"], 512, "up_proj")
    f, dh2, red = _ffn_out(up, fw, h1, w["w_down"], row("final_norm_g"), target)
    loss = 0.5 / D * jnp.sum(red[0:8])

    g = {"final_norm_g": jnp.sum(red[8:16], axis=0)}
    dup, dfw = _ffn_bwd(dh2, w["w_down"], up, fw)
    g["ffn_dw_w"] = jnp.sum(dfw[0:FFN_K], axis=1)
    g["ffn_dw_b"] = jnp.sum(dfw[3], axis=0)
    g["w_down"] = _wgrad(f, dh2, 1408, "wgrad_down")
    dh1, dg2, *landed = _dgrad_norm(dup, w["w_up_t"], h1, row("norm_ffn_g"), dh2, "up_dgrad", carry=scatter(g["w_down"]))
    if dist:
        g["w_down"] = landed[0]
    g["norm_ffn_g"] = jnp.sum(dg2, axis=0)
    g["w_up_t"] = _wgrad(dup, u2, 1408, "wgrad_up")
    dbrg, dbrc, dog, dcs, dproj = _mix_bwd(dh1, w["w_out"], w["w_gla_o"], w["w_conf_o"], proj, brg, brc)
    g["w_out"] = _wgrad(merged, dh1, 1024, "wgrad_out")
    g["w_gla_o"] = _wgrad(og, dbrg, 1024, "wgrad_gla_o")
    g["w_conf_o"] = _wgrad(cs, dbrc, 1024, "wgrad_conf_o")
    do, dproj, dgn = _glapost_bwd(dog, o, proj, row("gla_norm_g"), dproj)
    g["gla_norm_g"] = jnp.sum(dgn, axis=0)
    dproj, dcw, dst, *landed = _conf_bwd(dcs, cc, proj, cw, row("conf_ln_g"), row("conf_ln_b"), dproj,
                                         carry=scatter(g["w_up_t"]))
    if dist:
        g["w_up_t"] = landed[0]
    g["conf_dw_w"] = jnp.sum(dcw[0:CONF_K], axis=1)
    g["conf_ln_g"], g["conf_ln_b"], g["conf_dw_b"] = jnp.sum(dst[0], axis=0), jnp.sum(dst[1], axis=0), jnp.sum(dst[2], axis=0)
    dproj, dalr, dwau, dba, *landed = _gla_bwd(proj, alr, wau, row("b_alpha"), do, sall, dproj,
                                               carry=scatter(g["w_out"], g["w_gla_o"], g["w_conf_o"]))
    if dist:
        g["w_out"], g["w_gla_o"], g["w_conf_o"] = landed
    g["w_alpha_up"] = dwau[0:RANK]
    g["b_alpha"] = jnp.sum(dba, axis=0)
    dw_main = _wgrad(dproj, u1, 1024, "wgrad_in")
    dw_a = _wgrad(dalr, u1, LANES, "wgrad_alr")
    g["w_in_t"] = jnp.concatenate([dw_main[0:r0], dw_main[NPROJ - DV:NPROJ], dw_a[0:RANK], dw_main[r0:NPROJ - DV]], axis=0)
    w_in_blocks = []
    if dist:
        pad = ((0, 0), (0, W_IN_PAD - W_IN_ROWS), (0, 0))
        w_in_blocks = [(jnp.pad(g["w_in_t"].reshape(N_DEV, W_IN_ROWS, D), pad), True)]
    dh0, dg1, *landed = _dgrad_norm(dproj, w_main, h0, row("norm_mix_g"), dh1, "in_dgrad", dy_extra=dalr,
                                    w_extra_t=w_a, carry=w_in_blocks)
    if dist:
        g["w_in_t"] = landed[0]
    g["norm_mix_g"] = jnp.sum(dg1, axis=0)
    g["meta_tokens"] = dh0[0:N_META]
    return loss, dh0[N_META:n_real], g


def kernel(x, meta_tokens, norm_mix_g, w_in, w_alpha_up, b_alpha, gla_norm_g, w_gla_o, conf_dw_w, conf_dw_b, conf_ln_g, conf_ln_b, w_conf_o, w_out, norm_ffn_g, w_up, ffn_dw_w, ffn_dw_b, w_down, final_norm_g, loss_target, m_meta_tokens, m_norm_mix_g, m_w_in, m_w_alpha_up, m_b_alpha, m_gla_norm_g, m_w_gla_o, m_conf_dw_w, m_conf_dw_b, m_conf_ln_g, m_conf_ln_b, m_w_conf_o, m_w_out, m_norm_ffn_g, m_w_up, m_ffn_dw_w, m_ffn_dw_b, m_w_down, m_final_norm_g, v_meta_tokens, v_norm_mix_g, v_w_in, v_w_alpha_up, v_b_alpha, v_gla_norm_g, v_w_gla_o, v_conf_dw_w, v_conf_dw_b, v_conf_ln_g, v_conf_ln_b, v_w_conf_o, v_w_out, v_norm_ffn_g, v_w_up, v_ffn_dw_w, v_ffn_dw_b, v_w_down, v_final_norm_g):
    ws = dict(meta_tokens=meta_tokens, norm_mix_g=norm_mix_g, w_in=w_in, w_alpha_up=w_alpha_up, b_alpha=b_alpha,
              gla_norm_g=gla_norm_g, w_gla_o=w_gla_o, conf_dw_w=conf_dw_w, conf_dw_b=conf_dw_b, conf_ln_g=conf_ln_g,
              conf_ln_b=conf_ln_b, w_conf_o=w_conf_o, w_out=w_out, norm_ffn_g=norm_ffn_g, w_up=w_up, ffn_dw_w=ffn_dw_w,
              ffn_dw_b=ffn_dw_b, w_down=w_down, final_norm_g=final_norm_g)
    ms = dict(meta_tokens=m_meta_tokens, norm_mix_g=m_norm_mix_g, w_in=m_w_in, w_alpha_up=m_w_alpha_up, b_alpha=m_b_alpha,
              gla_norm_g=m_gla_norm_g, w_gla_o=m_w_gla_o, conf_dw_w=m_conf_dw_w, conf_dw_b=m_conf_dw_b,
              conf_ln_g=m_conf_ln_g, conf_ln_b=m_conf_ln_b, w_conf_o=m_w_conf_o, w_out=m_w_out, norm_ffn_g=m_norm_ffn_g,
              w_up=m_w_up, ffn_dw_w=m_ffn_dw_w, ffn_dw_b=m_ffn_dw_b, w_down=m_w_down, final_norm_g=m_final_norm_g)
    vs = dict(meta_tokens=v_meta_tokens, norm_mix_g=v_norm_mix_g, w_in=v_w_in, w_alpha_up=v_w_alpha_up, b_alpha=v_b_alpha,
              gla_norm_g=v_gla_norm_g, w_gla_o=v_w_gla_o, conf_dw_w=v_conf_dw_w, conf_dw_b=v_conf_dw_b,
              conf_ln_g=v_conf_ln_g, conf_ln_b=v_conf_ln_b, w_conf_o=v_w_conf_o, w_out=v_w_out, norm_ffn_g=v_norm_ffn_g,
              w_up=v_w_up, ffn_dw_w=v_ffn_dw_w, ffn_dw_b=v_ffn_dw_b, w_down=v_w_down, final_norm_g=v_final_norm_g)
    small = SMALL_SHARDED + REPLICATED
    pack_small = lambda d: _pack_small([d[n] for n in small])

    shards = {n: _to_panel(n, ws[n]).astype(BF16) for n in BIG}
    own = [shards["w_in"], pack_small(ws)]
    my_idx = 4 * lax.axis_index("x") + 2 * lax.axis_index("y") + lax.axis_index("c")
    gathered = [lax.dynamic_update_slice(full_, mine[None], (my_idx, 0, 0))
                for full_, mine in zip(_all_gather(own, "weight_gather"), own)]
    full = {n: ws[n].reshape(-1) for n in REPLICATED}
    full["w_in_t"] = gathered[0][:, 0:W_IN_ROWS].reshape(N_IN, D)
    flat, off = gathered[1].reshape(N_DEV, -1), 0
    for n in SMALL_SHARDED:
        k, c = ws[n].shape[-2], ws[n].shape[-1]
        full[n] = flat[:, off:off + k * c].reshape(N_DEV, k, c).transpose(1, 0, 2).reshape(k, N_DEV * c)
        off += k * c + (-(k * c)) % D

    loss, grad_x, g = _local_step(x[0], loss_target[0], full, shards)

    lands = [g["w_in_t"], g["w_up_t"]] + [g[n] for n in BIG[2:]]
    blocks = []
    for n in SMALL_SHARDED:
        k, c = ws[n].shape[-2], ws[n].shape[-1]
        b = g[n].reshape(k, N_DEV, c).transpose(1, 0, 2).reshape(N_DEV, k * c)
        blocks.append(jnp.pad(b, ((0, 0), (0, (-(k * c)) % D))))
    for n in REPLICATED:
        b = jnp.broadcast_to(g[n].reshape(1, -1), (N_DEV, g[n].size))
        blocks.append(jnp.pad(b, ((0, 0), (0, (-b.shape[1]) % D))))
    gsm = jnp.concatenate(blocks, axis=1)
    lands += _exchange([jnp.pad(gsm, ((0, 0), (0, SMALL_ROWS * D - gsm.shape[1]))).reshape(N_DEV, SMALL_ROWS, D)])

    grad, delta, new_m, new_v = {}, {}, {}, {}
    for i, n in enumerate(BIG):
        outs = _adamw(lands[i], _to_panel(n, ws[n]), _to_panel(n, ms[n]), _to_panel(n, vs[n]), ADAM_BLOCK[n], "adamw_" + n)
        grad[n], delta[n], new_m[n], new_v[n] = [_from_panel(n, p, ws[n].shape) for p in outs]
    outs = _adamw(lands[len(BIG)], pack_small(ws), pack_small(ms), pack_small(vs), SMALL_ROWS, "adamw_small")
    shapes = [ws[n].shape for n in small]
    for d, p in zip((grad, delta, new_m, new_v), outs):
        d.update(zip(small, _unpack_small(p, shapes)))

    order = ("meta_tokens", "norm_mix_g", "w_in", "w_alpha_up", "b_alpha", "gla_norm_g", "w_gla_o", "conf_dw_w", "conf_dw_b",
             "conf_ln_g", "conf_ln_b", "w_conf_o", "w_out", "norm_ffn_g", "w_up", "ffn_dw_w", "ffn_dw_b", "w_down",
             "final_norm_g")
    loss = lax.psum(loss, ("x", "y", "c"))
    return (loss, grad_x[None], *[grad[n] for n in order], *[delta[n] for n in order], *[new_m[n] for n in order],
            *[new_v[n] for n in order])
```

```python
import functools

import jax
import jax.numpy as jnp
from jax import lax
from jax.experimental import pallas as pl
from jax.experimental.pallas import tpu as pltpu

F32, BF16 = jnp.float32, jnp.bfloat16
SDS = jax.ShapeDtypeStruct

D = 1024
N_META = 16
HEADS = 4
DK, DKH, DV, DVH = 512, 128, 1024, 256
RANK = 16
TAU = 16.0
CONF_K = 31
DFF = 2816
FFN_K = 3
IN_WIDTHS = (DK, DK, DV, DV, RANK, 2 * D, D, D)
RMS_EPS, LN_EPS = 1e-6, 1e-5
ADAM_LR, ADAM_B1, ADAM_B2, ADAM_EPS, ADAM_WD, ADAM_STEP = 0.001, 0.9, 0.999, 1e-08, 0.01, 10

NPROJ = 7 * D
LANES = 128
CH = 128
TM = 640
TM_BIG = 1664
TE = 320
HALO = 32
HALO_F = 16
N_DEV = 8
VMEM_LIMIT = 60 * 1024 * 1024
MESH_T = pl.DeviceIdType.MESH

_ARB1 = pltpu.CompilerParams(dimension_semantics=("arbitrary",), vmem_limit_bytes=VMEM_LIMIT)
_ARB2 = pltpu.CompilerParams(dimension_semantics=("arbitrary", "arbitrary"), vmem_limit_bytes=VMEM_LIMIT)


def _dot(a, b):
    return jnp.dot(a, b, preferred_element_type=F32)


def _dot_nt(a, b):
    return lax.dot_general(a, b, (((1,), (1,)), ((), ())), preferred_element_type=F32)


def _dot_tn(a, b):
    return lax.dot_general(a, b, (((0,), (0,)), ((), ())), preferred_element_type=F32)


def _sigmoid(x):
    return 0.5 * jnp.tanh(0.5 * x) + 0.5


def _rows8(x):
    return x.reshape(x.shape[0] // 8, 8, x.shape[1]).sum(axis=0)


def _row_tile(t, preferred):
    return preferred if t % preferred == 0 else TM


def _row_loop(n_rows, rb, fn, unroll=1):
    def step(i, carry):
        fn(pl.multiple_of(i * rb, rb))
        return carry
    lax.fori_loop(0, n_rows // rb, step, 0, unroll=unroll)


def _resident(shape):
    return pl.BlockSpec(shape, lambda *_: (0,) * len(shape))


def _once(shape):
    return pl.BlockSpec(shape, lambda *_: (0,) * len(shape), pipeline_mode=pl.Buffered(1))


CONV_ROWS, CONV_LANES = 64, 256


def _shift_classes(offset_taps):
    return [(s, [(o - s, j) for o, j in offset_taps if o % 8 == s]) for s in range(8)]


def _shifted(win, s):
    return win if s == 0 else win[s:s + CONV_ROWS + HALO - 8, :]


def _my_place():
    return lax.axis_index("x"), lax.axis_index("y"), lax.axis_index("c")


def _flip(v, bit):
    return 1 - v if bit else v


def _exchange_copies(src_refs, land_refs, scatter, send_sems, recv_sems, local_sems, arrivals):
    x, y, c = _my_place()
    my_idx = 4 * x + 2 * y + c
    local, remote = [], []
    for p, (src, land) in enumerate(zip(src_refs, land_refs)):
        local.append(pltpu.make_async_copy(src.at[my_idx] if scatter[p] else src, land.at[my_idx], local_sems.at[p]))
    for k in range(1, N_DEV):
        px, py, pc = _flip(x, k & 4), _flip(y, k & 2), _flip(c, k & 1)
        p_idx = 4 * px + 2 * py + pc
        for p, (src, land) in enumerate(zip(src_refs, land_refs)):
            s = 7 * p + k - 1
            out = src.at[p_idx] if scatter[p] else src

            def copy(dst):
                return pltpu.make_async_remote_copy(src_ref=out, dst_ref=dst, send_sem=send_sems.at[s],
                                                    recv_sem=recv_sems.at[s], device_id=(px, py, pc), device_id_type=MESH_T)
            remote.append((copy(land.at[my_idx]), copy(land.at[p_idx]) if arrivals else None))
    return local, remote


def _carried_call(core, carry, *, grid, in_specs, out_specs, out_shape, scratch_shapes=(), **kw):
    n_in, n_out, nc, n_scr = len(in_specs), len(out_specs), len(carry), len(scratch_shapes)
    if nc == 0:
        return pl.pallas_call(core, grid=grid, in_specs=in_specs, out_specs=out_specs, out_shape=out_shape,
                              scratch_shapes=list(scratch_shapes), **kw)
    scatter = [sc for _, sc in carry]

    def body(*refs):
        ins, cin = refs[:n_in], refs[n_in:n_in + nc]
        outs, cout = refs[n_in + nc:n_in + nc + n_out], refs[n_in + nc + n_out:n_in + 2 * nc + n_out]
        scr, sems = refs[n_in + 2 * nc + n_out:n_in + 2 * nc + n_out + n_scr], refs[-3:]
        first = functools.reduce(jnp.logical_and, [pl.program_id(a) == 0 for a in range(len(grid))])
        last = functools.reduce(jnp.logical_and, [pl.program_id(a) == grid[a] - 1 for a in range(len(grid))])

        @pl.when(first)
        def _():
            local, remote = _exchange_copies(cin, cout, scatter, *sems, arrivals=False)
            for cp in local:
                cp.start()
            for send, _ in remote:
                send.start()

        core(*ins, *outs, *scr)

        @pl.when(last)
        def _():
            local, remote = _exchange_copies(cin, cout, scatter, *sems, arrivals=True)
            for _, arrival in remote:
                arrival.wait_recv()
            for send, _ in remote:
                send.wait_send()
            for cp in local:
                cp.wait()

    hbm = pl.BlockSpec(memory_space=pl.ANY)
    land_shape = [SDS((N_DEV, *(a.shape[1:] if sc else a.shape)), a.dtype) for a, sc in carry]
    sems = [pltpu.SemaphoreType.DMA((7 * nc,)), pltpu.SemaphoreType.DMA((7 * nc,)), pltpu.SemaphoreType.DMA((nc,))]
    call = pl.pallas_call(body, grid=grid, in_specs=list(in_specs) + [hbm] * nc, out_specs=list(out_specs) + [hbm] * nc,
                          out_shape=list(out_shape) + land_shape, scratch_shapes=list(scratch_shapes) + sems, **kw)
    return lambda *args: call(*args, *[a for a, _ in carry])


def _norm_matmul(h, g, w_t, tn, name, w_extra_t=None, carry=()):
    t, n = h.shape[0], w_t.shape[0]
    tm = _row_tile(t, TM_BIG)
    nt, nb = t // tm, n // tn

    def body(*refs):
        if w_extra_t is None:
            h_ref, g_ref, w_ref, u_ref, p_ref = refs
        else:
            h_ref, g_ref, w_ref, we_ref, u_ref, p_ref, e_ref = refs

        @pl.when(pl.program_id(1) == 0)
        def _():
            def blk(r0):
                x = h_ref[pl.ds(r0, 32), :]
                rinv = lax.rsqrt(jnp.mean(x * x, axis=-1, keepdims=True) + RMS_EPS)
                u_ref[pl.ds(r0, 32), :] = (x * rinv * g_ref[...]).astype(BF16)
            _row_loop(tm, 32, blk, unroll=2)
            if w_extra_t is not None:
                e_ref[...] = _dot_nt(u_ref[...], we_ref[...]).astype(BF16)

        p_ref[...] = _dot_nt(u_ref[...], w_ref[...]).astype(BF16)

    in_specs = [pl.BlockSpec((tm, D), lambda i, j: (i, 0)), _resident((1, D)), pl.BlockSpec((tn, D), lambda i, j: (j, 0))]
    out_specs = [pl.BlockSpec((tm, D), lambda i, j: (i, 0)), pl.BlockSpec((tm, tn), lambda i, j: (i, j))]
    out_shape = [SDS((t, D), BF16), SDS((t, n), BF16)]
    args = [h, g, w_t]
    if w_extra_t is not None:
        in_specs.append(_resident(w_extra_t.shape))
        out_specs.append(pl.BlockSpec((tm, w_extra_t.shape[0]), lambda i, j: (i, 0)))
        out_shape.append(SDS((t, w_extra_t.shape[0]), BF16))
        args.append(w_extra_t)
    return _carried_call(body, carry, name=name, grid=(nt, nb), in_specs=in_specs, out_specs=out_specs,
                         out_shape=out_shape, compiler_params=_ARB2)(*args)


def _in_proj(x, meta, g, w_t, w_a_t, t, carry=()):
    n_real = x.shape[0] + N_META
    n, tn = w_t.shape[0], 1024
    tm = _row_tile(t, TM_BIG)
    nt, nb = t // tm, n // tn

    def body(x_ref, xh_ref, m_ref, g_ref, w_ref, wa_ref, h_ref, u_ref, p_ref, e_ref):
        i = pl.program_id(0)

        @pl.when(pl.program_id(1) == 0)
        def _():
            def rows_of(r0, val):
                gid = i * tm + r0 + lax.broadcasted_iota(jnp.int32, (32, 1), 0)
                val = jnp.where(gid < n_real, val, 0.0)
                h_ref[pl.ds(r0, 32), :] = val
                rinv = lax.rsqrt(jnp.mean(val * val, axis=-1, keepdims=True) + RMS_EPS)
                u_ref[pl.ds(r0, 32), :] = (val * rinv * g_ref[...]).astype(BF16)

            before = jnp.where(i == 0, m_ref[...], xh_ref[...])
            rows_of(0, jnp.concatenate([before, x_ref[0:N_META, :]], axis=0))

            def blk(k, c):
                r0 = pl.multiple_of(k * 32, 32)
                rows_of(r0, x_ref[pl.ds(pl.multiple_of(r0 - N_META, N_META), 32), :])
                return c
            lax.fori_loop(1, tm // 32, blk, 0)
            e_ref[...] = _dot_nt(u_ref[...], wa_ref[...]).astype(BF16)

        p_ref[...] = _dot_nt(u_ref[...], w_ref[...]).astype(BF16)

    row = lambda w: pl.BlockSpec((tm, w), lambda i, j: (i, 0))
    return _carried_call(
        body, carry, name="in_proj", grid=(nt, nb),
        in_specs=[row(D), pl.BlockSpec((N_META, D), lambda i, j: (jnp.maximum(i * (tm // N_META) - 1, 0), 0)),
                  _resident((N_META, D)), _resident((1, D)), pl.BlockSpec((tn, D), lambda i, j: (j, 0)),
                  _resident(w_a_t.shape)],
        out_specs=[row(D), row(D), pl.BlockSpec((tm, tn), lambda i, j: (i, j)), row(w_a_t.shape[0])],
        out_shape=[SDS((t, D), F32), SDS((t, D), BF16), SDS((t, n), BF16), SDS((t, w_a_t.shape[0]), BF16)],
        compiler_params=_ARB2)(x, x, meta, g, w_t, w_a_t)


def _gla_decay(a_ref, wau_ref, ba_ref):
    z = _dot(a_ref[...], wau_ref[...]) + ba_ref[...]
    la = (jnp.minimum(z, 0.0) - jnp.log(1.0 + jnp.exp(-jnp.abs(z)))) * (1.0 / TAU)
    r = lax.broadcasted_iota(jnp.int32, (CH, CH), 0)
    c = lax.broadcasted_iota(jnp.int32, (CH, CH), 1)
    b = _tri_matmul((r >= c).astype(BF16), la)
    mid = jnp.broadcast_to(b[CH // 2:CH // 2 + 1, :], b.shape)
    last = jnp.broadcast_to(b[CH - 1:CH, :], b.shape)
    return z, b, mid, last, r >= c


def _tri_matmul(tri, x):
    n = x.shape[1]
    x1 = x.astype(BF16)
    r1 = x - x1.astype(F32)
    x2 = r1.astype(BF16)
    x3 = (r1 - x2.astype(F32)).astype(BF16)
    y = _dot(tri, jnp.concatenate([x1, x2, x3], axis=1))
    return y[:, 0:n] + y[:, n:2 * n] + y[:, 2 * n:3 * n]


def _gla_fwd(proj, alr, wau, balpha, gn):
    t = proj.shape[0]
    nc = t // CH

    def body(qk_ref, v_ref, r_ref, a_ref, wau_ref, ba_ref, gn_ref, o_ref, og_ref, sall_ref, s_scr):
        @pl.when(pl.program_id(0) == 0)
        def _():
            s_scr[...] = jnp.zeros_like(s_scr)

        sall_ref[0] = s_scr[...]
        _, b, bmid, blast, causal = _gla_decay(a_ref, wau_ref, ba_ref)
        heads = range(HEADS)
        ks = [slice(h * DKH, (h + 1) * DKH) for h in heads]
        vs = [slice(h * DVH, (h + 1) * DVH) for h in heads]
        scores, inter = [], []
        for h in heads:
            bh, mh, lh = b[:, ks[h]], bmid[:, ks[h]], blast[:, ks[h]]
            q = qk_ref[:, ks[h]].astype(F32) * (DKH ** -0.5)
            k = qk_ref[:, DK + h * DKH:DK + (h + 1) * DKH].astype(F32)
            qt = (q * jnp.exp(bh - mh)).astype(BF16)
            kt = (k * jnp.exp(mh - bh)).astype(BF16)
            qg = (q * jnp.exp(bh)).astype(BF16)
            kg = (k * jnp.exp(lh - bh)).astype(BF16)
            st = s_scr[vs[h], :]
            scores.append(_dot_nt(qt, kt))
            inter.append(_dot_nt(qg, st.astype(BF16)))
            el = jnp.exp(lh)
            s_scr[vs[h], :] = st * jnp.concatenate([el, el], axis=0) + _dot_tn(v_ref[:, vs[h]], kg)
        outs = [_dot(jnp.where(causal, scores[h], 0.0).astype(BF16), v_ref[:, vs[h]]) + inter[h] for h in heads]
        for h in heads:
            o = outs[h]
            o_ref[:, vs[h]] = o
            on = o * lax.rsqrt(jnp.mean(o * o, axis=-1, keepdims=True) + RMS_EPS) * gn_ref[:, vs[h]]
            rr = r_ref[:, vs[h]].astype(F32)
            og_ref[:, vs[h]] = (on * (rr * _sigmoid(rr))).astype(BF16)

    return pl.pallas_call(
        body, name="gla_fwd", grid=(nc,),
        in_specs=[pl.BlockSpec((CH, D), lambda c: (c, 0)), pl.BlockSpec((CH, D), lambda c: (c, 1)),
                  pl.BlockSpec((CH, D), lambda c: (c, 6)), pl.BlockSpec((CH, LANES), lambda c: (c, 0)),
                  _resident((LANES, DK)), _resident((1, DK)), _resident((1, DV))],
        out_specs=[pl.BlockSpec((CH, DV), lambda c: (c, 0)), pl.BlockSpec((CH, DV), lambda c: (c, 0)),
                   pl.BlockSpec((1, DV, DKH), lambda c: (c, 0, 0))],
        out_shape=[SDS((t, DV), F32), SDS((t, DV), BF16), SDS((nc, DV, DKH), F32)],
        scratch_shapes=[pltpu.VMEM((DV, DKH), F32)], compiler_params=_ARB1)(proj, proj, proj, alr, wau, balpha, gn)


def _conf_fwd(proj, cw, cb, lg, lb, carry=()):
    t = proj.shape[0]
    nt = t // TE

    def body(c1_ref, c2_ref, cw_ref, cb_ref, lg_ref, lb_ref, cc_ref, cs_ref, cext):
        i = pl.program_id(0)

        @pl.when(i == 0)
        def _():
            cext[0:HALO, :] = jnp.zeros((HALO, D), F32)

        @pl.when(i > 0)
        def _():
            cext[0:HALO, :] = cext[TE:TE + HALO, :]

        def glu(r0):
            c2 = c2_ref[pl.ds(r0, 32), :].astype(F32)
            cext[pl.ds(HALO + r0, 32), :] = c1_ref[pl.ds(r0, 32), :].astype(F32) * _sigmoid(c2)
        _row_loop(TE, 32, glu)

        def conv(r0):
            for part in range(D // CONV_LANES):
                cols = slice(part * CONV_LANES, (part + 1) * CONV_LANES)
                win = cext[pl.ds(r0, CONV_ROWS + HALO), cols]
                acc = jnp.zeros((CONV_ROWS, CONV_LANES), F32) + cb_ref[:, cols]
                for s, taps in _shift_classes([(2 + j, j) for j in range(CONF_K)]):
                    ws = _shifted(win, s)
                    for a8, j in taps:
                        acc = acc + jnp.tile(cw_ref[j, :, cols], (CONV_ROWS // 8, 1)) * ws[a8:a8 + CONV_ROWS, :]
                cc_ref[pl.ds(r0, CONV_ROWS), cols] = acc
            for sub in range(CONV_ROWS // 32):
                rows = pl.ds(r0 + 32 * sub, 32)
                x = cc_ref[rows, :]
                xc = x - jnp.mean(x, axis=-1, keepdims=True)
                var = jnp.mean(xc * xc, axis=-1, keepdims=True)
                ln = xc * lax.rsqrt(var + LN_EPS) * lg_ref[...] + lb_ref[...]
                cs_ref[rows, :] = (ln * _sigmoid(ln)).astype(BF16)
        _row_loop(TE, CONV_ROWS, conv)

    return _carried_call(
        body, carry, name="conf_fwd", grid=(nt,),
        in_specs=[pl.BlockSpec((TE, D), lambda i: (i, 2)), pl.BlockSpec((TE, D), lambda i: (i, 3)),
                  _resident((32, 8, D)), _resident((1, D)), _resident((1, D)), _resident((1, D))],
        out_specs=[pl.BlockSpec((TE, D), lambda i: (i, 0)), pl.BlockSpec((TE, D), lambda i: (i, 0))],
        out_shape=[SDS((t, D), F32), SDS((t, D), BF16)],
        scratch_shapes=[pltpu.VMEM((TE + HALO, D), F32)], compiler_params=_ARB1)(proj, proj, cw, cb, lg, lb)


def _mix_fwd(og, cs, proj, h0, wg, wc, wo):
    t = h0.shape[0]
    nt = t // TM

    def body(og_ref, cs_ref, g_ref, h0_ref, wg_ref, wc_ref, wo_ref, brg_ref, brc_ref, mg_ref, h1_ref):
        brg_ref[...] = _dot(og_ref[...], wg_ref[...]).astype(BF16)
        brc_ref[...] = _dot(cs_ref[...], wc_ref[...]).astype(BF16)

        def blk(r0):
            rows = pl.ds(r0, 32)
            gg = g_ref[rows, 0:D].astype(F32)
            gc = g_ref[rows, D:2 * D].astype(F32)
            m = _sigmoid(gg) * brg_ref[rows, :].astype(F32) + _sigmoid(gc) * brc_ref[rows, :].astype(F32)
            mg_ref[rows, :] = m.astype(BF16)
        _row_loop(TM, 32, blk)
        h1_ref[...] = h0_ref[...] + _dot(mg_ref[...], wo_ref[...])

    row = lambda w: pl.BlockSpec((TM, w), lambda i: (i, 0))
    return pl.pallas_call(
        body, name="mix_fwd", grid=(nt,),
        in_specs=[row(D), row(D), pl.BlockSpec((TM, 2 * D), lambda i: (i, 2)), row(D),
                  _once((D, D)), _once((D, D)), _once((D, D))],
        out_specs=[row(D), row(D), row(D), row(D)],
        out_shape=[SDS((t, D), BF16), SDS((t, D), BF16), SDS((t, D), BF16), SDS((t, D), F32)],
        compiler_params=_ARB1)(og, cs, proj, h0, wg, wc, wo)


def _ffn_out(up, fw, h1, wd, gf, target):
    t = h1.shape[0]
    nt = t // TE
    n_real = target.shape[0] + N_META

    def body(a_ref, bv_ref, fw_ref, h1_ref, wd_ref, gf_ref, tg_ref, tb_ref, f_ref, dh2_ref, red_ref, aext, hs):
        i = pl.program_id(0)

        @pl.when(i == 0)
        def _():
            aext[0:HALO_F, :] = jnp.zeros((HALO_F, DFF), F32)
            red_ref[...] = jnp.zeros_like(red_ref)

        @pl.when(i > 0)
        def _():
            aext[0:HALO_F, :] = aext[TE:TE + HALO_F, :]

        def cp(r0):
            aext[pl.ds(HALO_F + r0, 16), :] = a_ref[pl.ds(r0, 16), :].astype(F32)
        _row_loop(TE, 16, cp)

        def conv(r0):
            win = aext[pl.ds(r0, 32), :]
            ac = fw_ref[3] + fw_ref[0] * win[14:30, :] + fw_ref[1] * win[15:31, :] + fw_ref[2] * win[16:32, :]
            f_ref[pl.ds(r0, 16), :] = (ac * _sigmoid(ac) * bv_ref[pl.ds(r0, 16), :].astype(F32)).astype(BF16)
        _row_loop(TE, 16, conv)
        hs[...] = h1_ref[...] + _dot(f_ref[...], wd_ref[...])

        def head(r0, tg):
            rows = pl.ds(r0, 32)
            h2 = hs[rows, :]
            rinv = lax.rsqrt(jnp.mean(h2 * h2, axis=-1, keepdims=True) + RMS_EPS)
            hh = h2 * rinv
            gid = i * TE + r0 + lax.broadcasted_iota(jnp.int32, (32, 1), 0)
            live = jnp.logical_and(gid >= N_META, gid < n_real)
            err = jnp.where(live, hh * gf_ref[...] - tg, 0.0)
            dy = err * (1.0 / D)
            red_ref[0:8, :] += _rows8(err * err)
            red_ref[8:16, :] += _rows8(dy * hh)
            dhh = dy * gf_ref[...]
            dh2_ref[rows, :] = rinv * (dhh - hh * jnp.mean(dhh * hh, axis=-1, keepdims=True))

        head(0, jnp.concatenate([tb_ref[...], tg_ref[0:N_META, :]], axis=0))

        def blk(k, c):
            r0 = pl.multiple_of(k * 32, 32)
            head(r0, tg_ref[pl.ds(pl.multiple_of(r0 - N_META, N_META), 32), :])
            return c
        lax.fori_loop(1, TE // 32, blk, 0, unroll=3)

    row = lambda w: pl.BlockSpec((TE, w), lambda i: (i, 0))
    return pl.pallas_call(
        body, name="ffn_out", grid=(nt,),
        in_specs=[pl.BlockSpec((TE, DFF), lambda i: (i, 0)), pl.BlockSpec((TE, DFF), lambda i: (i, 1)),
                  _resident((4, 16, DFF)), row(D), _resident((DFF, D)), _resident((1, D)), row(D),
                  pl.BlockSpec((N_META, D), lambda i: (jnp.maximum(i * (TE // N_META) - 1, 0), 0))],
        out_specs=[row(DFF), row(D), _resident((16, D))],
        out_shape=[SDS((t, DFF), BF16), SDS((t, D), F32), SDS((16, D), F32)],
        scratch_shapes=[pltpu.VMEM((TE + HALO_F, DFF), F32), pltpu.VMEM((TE, D), F32)],
        compiler_params=_ARB1)(up, up, fw, h1, wd, gf, target, target)


def _ffn_bwd(dh2, wd, up, fw):
    t = dh2.shape[0]
    nt = t // TE
    hb = TE // HALO_F

    def body(dh_ref, wd_ref, a_ref, ah_ref, bv_ref, fw_ref, dup_ref, dw_ref, aext, dax, dfs):
        i = pl.program_id(0)
        ti = nt - 1 - i

        @pl.when(i == 0)
        def _():
            dax[TE:TE + HALO_F, :] = jnp.zeros((HALO_F, DFF), F32)
            dw_ref[...] = jnp.zeros_like(dw_ref)

        @pl.when(i > 0)
        def _():
            dax[TE:TE + HALO_F, :] = dax[0:HALO_F, :]

        aext[0:HALO_F, :] = jnp.where(ti > 0, ah_ref[...].astype(F32), 0.0)
        dfs[...] = _dot_nt(dh_ref[...].astype(BF16), wd_ref[...])

        def cp(r0):
            aext[pl.ds(HALO_F + r0, 16), :] = a_ref[pl.ds(r0, 16), :].astype(F32)
        _row_loop(TE, 16, cp)

        def act(r0):
            rows = pl.ds(r0, 16)
            win = aext[pl.ds(r0, 32), :]
            ac = fw_ref[3] + fw_ref[0] * win[14:30, :] + fw_ref[1] * win[15:31, :] + fw_ref[2] * win[16:32, :]
            sg = _sigmoid(ac)
            df = dfs[rows, :]
            dup_ref[rows, DFF:2 * DFF] = (df * ac * sg).astype(BF16)
            dac = df * bv_ref[rows, :].astype(F32) * sg * (1.0 + ac * (1.0 - sg))
            dax[rows, :] = dac
            dw_ref[3] += _rows8(dac)
            for j in range(FFN_K):
                dw_ref[j] += _rows8(dac * win[14 + j:30 + j, :])
        _row_loop(TE, 16, act)

        def convt(r0):
            win = dax[pl.ds(r0, 32), :]
            da = fw_ref[2] * win[0:16, :] + fw_ref[1] * win[1:17, :] + fw_ref[0] * win[2:18, :]
            dup_ref[pl.ds(r0, 16), 0:DFF] = da.astype(BF16)
        _row_loop(TE, 16, convt)

    rev = lambda w: pl.BlockSpec((TE, w), lambda i: (nt - 1 - i, 0))
    return pl.pallas_call(
        body, name="ffn_bwd", grid=(nt,),
        in_specs=[rev(D), _resident((DFF, D)), rev(DFF),
                  pl.BlockSpec((HALO_F, DFF), lambda i: (jnp.maximum((nt - 1 - i) * hb - 1, 0), 0)),
                  pl.BlockSpec((TE, DFF), lambda i: (nt - 1 - i, 1)), _resident((4, 16, DFF))],
        out_specs=[rev(2 * DFF), _resident((4, 8, DFF))],
        out_shape=[SDS((t, 2 * DFF), BF16), SDS((4, 8, DFF), F32)],
        scratch_shapes=[pltpu.VMEM((TE + HALO_F, DFF), F32), pltpu.VMEM((TE + HALO_F, DFF), F32),
                        pltpu.VMEM((TE, DFF), F32)],
        compiler_params=_ARB1)(dh2, wd, up, up, up, fw)


def _dgrad_norm(dy, w_t, h, g, dres, name, carry=()):
    t, k = dy.shape
    nt = t // TM

    def body(dy_ref, w_ref, h_ref, g_ref, dr_ref, dh_ref, dg_ref, acc):
        @pl.when(pl.program_id(0) == 0)
        def _():
            dg_ref[...] = jnp.zeros_like(dg_ref)

        acc[...] = _dot(dy_ref[...], w_ref[...])

        def blk(r0):
            rows = pl.ds(r0, 32)
            x = h_ref[rows, :]
            rinv = lax.rsqrt(jnp.mean(x * x, axis=-1, keepdims=True) + RMS_EPS)
            hh = x * rinv
            du = acc[rows, :]
            dg_ref[...] += _rows8(du * hh)
            dhh = du * g_ref[...]
            dh_ref[rows, :] = dr_ref[rows, :] + rinv * (dhh - hh * jnp.mean(dhh * hh, axis=-1, keepdims=True))
        _row_loop(TM, 32, blk, unroll=2)

    row = pl.BlockSpec((TM, D), lambda i: (i, 0))
    in_specs = [pl.BlockSpec((TM, k), lambda i: (i, 0)), _once((k, D)), row, _resident((1, D)), row]
    return _carried_call(
        body, carry, name=name, grid=(nt,), in_specs=in_specs, out_specs=[row, _resident((8, D))],
        out_shape=[SDS((t, D), F32), SDS((8, D), F32)],
        scratch_shapes=[pltpu.VMEM((TM, D), F32)], compiler_params=_ARB1)(dy, w_t, h, g, dres)


def _in_dgrad(dy, w_t, h, g, dres, dy_extra, w_extra_t, s, carry=()):
    t, k = dy.shape
    nt = t // TM
    assert -(-s // TM) == nt

    def body(dy_ref, w_ref, h_ref, g_ref, dr_ref, de_ref, we_ref, gx_ref, dm_ref, dg_ref, acc, after):
        i = pl.program_id(0)

        @pl.when(i == 0)
        def _():
            dg_ref[...] = jnp.zeros_like(dg_ref)
            after[...] = jnp.zeros_like(after)

        acc[...] = _dot(dy_ref[...], w_ref[...])
        acc[...] += _dot(de_ref[...], we_ref[...])

        def blk(r0):
            rows = pl.ds(r0, 32)
            x = h_ref[rows, :]
            rinv = lax.rsqrt(jnp.mean(x * x, axis=-1, keepdims=True) + RMS_EPS)
            hh = x * rinv
            du = acc[rows, :]
            dg_ref[...] += _rows8(du * hh)
            dhh = du * g_ref[...]
            acc[rows, :] = dr_ref[rows, :] + rinv * (dhh - hh * jnp.mean(dhh * hh, axis=-1, keepdims=True))
        _row_loop(TM, 32, blk, unroll=2)

        def move(r0):
            gx_ref[pl.ds(r0, N_META), :] = acc[pl.ds(r0 + N_META, N_META), :]
        _row_loop(TM - N_META, N_META, move)
        gx_ref[TM - N_META:TM, :] = after[...]
        after[...] = acc[0:N_META, :]

        @pl.when(i == nt - 1)
        def _():
            dm_ref[...] = acc[0:N_META, :]

    rev = lambda w: pl.BlockSpec((TM, w), lambda i: (nt - 1 - i, 0))
    return _carried_call(
        body, carry, name="in_dgrad", grid=(nt,),
        in_specs=[rev(k), _once((k, D)), rev(D), _resident((1, D)), rev(D), rev(dy_extra.shape[1]),
                  _once(w_extra_t.shape)],
        out_specs=[rev(D), _resident((N_META, D)), _resident((8, D))],
        out_shape=[SDS((s, D), F32), SDS((N_META, D), F32), SDS((8, D), F32)],
        scratch_shapes=[pltpu.VMEM((TM, D), F32), pltpu.VMEM((N_META, D), F32)], compiler_params=_ARB1)(
            dy, w_t, h, g, dres, dy_extra, w_extra_t)


def _wgrad(x, dy, tk, name):
    t, k = x.shape
    n = dy.shape[1]
    tm = _row_tile(t, TM_BIG)
    nk, nt = k // tk, t // tm

    def body(x_ref, dy_ref, o_ref, acc):
        @pl.when(pl.program_id(1) == 0)
        def _():
            acc[...] = jnp.zeros_like(acc)
        acc[...] += _dot_tn(x_ref[...], dy_ref[...].astype(BF16))

        @pl.when(pl.program_id(1) == nt - 1)
        def _():
            o_ref[...] = acc[...].astype(BF16)

    return pl.pallas_call(
        body, name=name, grid=(nk, nt),
        in_specs=[pl.BlockSpec((tm, tk), lambda j, i: (i, j)), pl.BlockSpec((tm, n), lambda j, i: (i, 0))],
        out_specs=pl.BlockSpec((tk, n), lambda j, i: (j, 0)), out_shape=SDS((k, n), BF16),
        scratch_shapes=[pltpu.VMEM((tk, n), F32)], compiler_params=_ARB2)(x, dy)


def _mix_bwd(dh1, wo, wg, wc, proj, brg, brc):
    t = dh1.shape[0]
    nt = t // TM

    def body(dh_ref, wo_ref, wg_ref, wc_ref, g_ref, brg_ref, brc_ref, dbg_ref, dbc_ref, dog_ref, dcs_ref, dp_ref, dm):
        dm[...] = _dot_nt(dh_ref[...].astype(BF16), wo_ref[...])

        def blk(r0):
            rows = pl.ds(r0, 32)
            d = dm[rows, :]
            sg = _sigmoid(g_ref[rows, 0:D].astype(F32))
            sc = _sigmoid(g_ref[rows, D:2 * D].astype(F32))
            dbg_ref[rows, :] = (d * sg).astype(BF16)
            dbc_ref[rows, :] = (d * sc).astype(BF16)
            dp_ref[rows, 0:D] = (d * brg_ref[rows, :].astype(F32) * sg * (1.0 - sg)).astype(BF16)
            dp_ref[rows, D:2 * D] = (d * brc_ref[rows, :].astype(F32) * sc * (1.0 - sc)).astype(BF16)
        _row_loop(TM, 32, blk)
        dog_ref[...] = _dot_nt(dbg_ref[...], wg_ref[...]).astype(BF16)
        dcs_ref[...] = _dot_nt(dbc_ref[...], wc_ref[...]).astype(BF16)

    row = pl.BlockSpec((TM, D), lambda i: (i, 0))
    wide = pl.BlockSpec((TM, 2 * D), lambda i: (i, 2))
    return pl.pallas_call(
        body, name="mix_bwd", grid=(nt,),
        in_specs=[row, _once((D, D)), _once((D, D)), _once((D, D)), wide, row, row],
        out_specs=[row, row, row, row, wide],
        out_shape=[SDS((t, D), BF16)] * 4 + [SDS((t, NPROJ), BF16)],
        scratch_shapes=[pltpu.VMEM((TM, D), F32)], compiler_params=_ARB1)(dh1, wo, wg, wc, proj, brg, brc)


def _glapost_bwd(dog, o, proj, gn, dproj):
    t = o.shape[0]
    nt = t // TE

    def body(dog_ref, o_ref, r_ref, gn_ref, dp_in, do_ref, dp_ref, dgn_ref):
        del dp_in

        @pl.when(pl.program_id(0) == 0)
        def _():
            dgn_ref[...] = jnp.zeros_like(dgn_ref)

        def blk(r0):
            rows = pl.ds(r0, 32)
            for h in range(HEADS):
                vs = slice(h * DVH, (h + 1) * DVH)
                x = o_ref[rows, vs]
                rinv = lax.rsqrt(jnp.mean(x * x, axis=-1, keepdims=True) + RMS_EPS)
                oh = x * rinv
                g = gn_ref[:, vs]
                rr = r_ref[rows, vs].astype(F32)
                sr = _sigmoid(rr)
                d = dog_ref[rows, vs].astype(F32)
                dp_ref[rows, vs] = (d * oh * g * sr * (1.0 + rr * (1.0 - sr))).astype(BF16)
                don = d * rr * sr
                dgn_ref[:, vs] += _rows8(don * oh)
                doh = don * g
                do_ref[rows, vs] = (rinv * (doh - oh * jnp.mean(doh * oh, axis=-1, keepdims=True))).astype(BF16)
        _row_loop(TE, 32, blk, unroll=2)

    row = pl.BlockSpec((TE, D), lambda i: (i, 0))
    rcol = pl.BlockSpec((TE, D), lambda i: (i, 6))
    return pl.pallas_call(
        body, name="glapost_bwd", grid=(nt,),
        in_specs=[row, row, rcol, _resident((1, D)), pl.BlockSpec(memory_space=pl.ANY)],
        out_specs=[row, rcol, _resident((8, D))],
        out_shape=[SDS((t, D), BF16), SDS((t, NPROJ), BF16), SDS((8, D), F32)],
        input_output_aliases={4: 1}, compiler_params=_ARB1)(dog, o, proj, gn, dproj)


def _conf_bwd(dcs, cc, proj, cw, lg, lb, dproj, carry=()):
    t = cc.shape[0]
    nt = t // TE
    hb = TE // HALO

    def body(dcs_ref, cc_ref, c1_ref, c2_ref, c1h_ref, c2h_ref, cw_ref, lg_ref, lb_ref, dp_in,
             dp_ref, dw_ref, ds_ref, cext, dext):
        del dp_in
        i = pl.program_id(0)
        ti = nt - 1 - i

        @pl.when(i == 0)
        def _():
            dext[TE:TE + HALO, :] = jnp.zeros((HALO, D), F32)
            dw_ref[...] = jnp.zeros_like(dw_ref)
            ds_ref[...] = jnp.zeros_like(ds_ref)

        @pl.when(i > 0)
        def _():
            dext[TE:TE + HALO, :] = dext[0:HALO, :]

        ch = c1h_ref[...].astype(F32) * _sigmoid(c2h_ref[...].astype(F32))
        cext[0:HALO, :] = jnp.where(ti > 0, ch, 0.0)

        def pre(r0):
            rows = pl.ds(r0, 32)
            cext[pl.ds(HALO + r0, 32), :] = c1_ref[rows, :].astype(F32) * _sigmoid(c2_ref[rows, :].astype(F32))
            x = cc_ref[rows, :]
            mu = jnp.mean(x, axis=-1, keepdims=True)
            xc = x - mu
            rstd = lax.rsqrt(jnp.mean(xc * xc, axis=-1, keepdims=True) + LN_EPS)
            xh = xc * rstd
            ln = xh * lg_ref[...] + lb_ref[...]
            sg = _sigmoid(ln)
            dln = dcs_ref[rows, :].astype(F32) * sg * (1.0 + ln * (1.0 - sg))
            ds_ref[0] += _rows8(dln * xh)
            ds_ref[1] += _rows8(dln)
            dxh = dln * lg_ref[...]
            dcc = rstd * (dxh - jnp.mean(dxh, axis=-1, keepdims=True) - xh * jnp.mean(dxh * xh, axis=-1, keepdims=True))
            dext[rows, :] = dcc
            ds_ref[2] += _rows8(dcc)
        _row_loop(TE, 32, pre, unroll=2)

        def convt(r0):
            rows = pl.ds(r0, CONV_ROWS)
            for part in range(D // CONV_LANES):
                cols = slice(part * CONV_LANES, (part + 1) * CONV_LANES)
                wd = dext[pl.ds(r0, CONV_ROWS + HALO), cols]
                dc = jnp.zeros((CONV_ROWS, CONV_LANES), F32)
                for s, taps in _shift_classes([(CONF_K - 1 - j, j) for j in range(CONF_K)]):
                    ws = _shifted(wd, s)
                    for a8, j in taps:
                        dc = dc + jnp.tile(cw_ref[j, :, cols], (CONV_ROWS // 8, 1)) * ws[a8:a8 + CONV_ROWS, :]
                dcc = wd[0:CONV_ROWS, :]
                wc = cext[pl.ds(r0, CONV_ROWS + HALO), cols]
                for s, taps in _shift_classes([(2 + j, j) for j in range(CONF_K)]):
                    ws = _shifted(wc, s)
                    for a8, j in taps:
                        dw_ref[j, :, cols] += _rows8(dcc * ws[a8:a8 + CONV_ROWS, :])
                c1 = c1_ref[rows, cols].astype(F32)
                s2 = _sigmoid(c2_ref[rows, cols].astype(F32))
                dp_ref[rows, cols] = (dc * s2).astype(BF16)
                dp_ref[rows, D + part * CONV_LANES:D + (part + 1) * CONV_LANES] = (dc * c1 * s2 * (1.0 - s2)).astype(BF16)
        _row_loop(TE, CONV_ROWS, convt)

    rev = lambda col: pl.BlockSpec((TE, D), lambda i: (nt - 1 - i, col))
    halo = lambda col: pl.BlockSpec((HALO, D), lambda i: (jnp.maximum((nt - 1 - i) * hb - 1, 0), col))
    return _carried_call(
        body, carry, name="conf_bwd", grid=(nt,),
        in_specs=[rev(0), rev(0), rev(2), rev(3), halo(2), halo(3), _resident((32, 8, D)), _resident((1, D)),
                  _resident((1, D)), pl.BlockSpec(memory_space=pl.ANY)],
        out_specs=[pl.BlockSpec((TE, 2 * D), lambda i: (nt - 1 - i, 1)), _resident((32, 8, D)), _resident((3, 8, D))],
        out_shape=[SDS((t, NPROJ), BF16), SDS((32, 8, D), F32), SDS((3, 8, D), F32)],
        scratch_shapes=[pltpu.VMEM((TE + HALO, D), F32), pltpu.VMEM((TE + HALO, D), F32)],
        input_output_aliases={9: 0}, compiler_params=_ARB1)(dcs, cc, proj, proj, proj, proj, cw, lg, lb, dproj)


def _gla_bwd(proj, alr, wau, balpha, do, sall, dproj, carry=()):
    t = proj.shape[0]
    nc = t // CH

    def body(qk_ref, v_ref, a_ref, wau_ref, ba_ref, do_ref, s_ref, dp_in, dp_ref, da_ref, dwau_ref, dba_ref, ds_scr, dla_scr):
        del dp_in

        @pl.when(pl.program_id(0) == 0)
        def _():
            ds_scr[...] = jnp.zeros_like(ds_scr)
            dwau_ref[...] = jnp.zeros_like(dwau_ref)
            dba_ref[...] = jnp.zeros_like(dba_ref)

        z, b, bmid, blast, causal = _gla_decay(a_ref, wau_ref, ba_ref)
        dlasts = []
        for h in range(HEADS):
            ks = slice(h * DKH, (h + 1) * DKH)
            vs = slice(h * DVH, (h + 1) * DVH)
            bh, mh, lh = b[:, ks], bmid[:, ks], blast[:, ks]
            q = qk_ref[:, ks].astype(F32) * (DKH ** -0.5)
            k = qk_ref[:, DK + h * DKH:DK + (h + 1) * DKH].astype(F32)
            v = v_ref[:, vs]
            dout = do_ref[:, vs]
            eq, ek, eb, eg, el = jnp.exp(bh - mh), jnp.exp(mh - bh), jnp.exp(bh), jnp.exp(lh - bh), jnp.exp(lh)
            qt, kt = (q * eq).astype(BF16), (k * ek).astype(BF16)
            qg, kg = (q * eb).astype(BF16), (k * eg).astype(BF16)
            st = s_ref[0, vs, :]
            dsn = ds_scr[vs, :]
            st16, dsn16 = st.astype(BF16), dsn.astype(BF16)
            a = jnp.where(causal, _dot_nt(qt, kt), 0.0).astype(BF16)
            da = jnp.where(causal, _dot_nt(dout, v), 0.0).astype(BF16)
            dq_inter = _dot(dout, st16) * eb
            dk_inter = _dot(v, dsn16) * eg
            dq = _dot(da, kt) * eq + dq_inter
            dk = _dot_tn(da, qt) * ek + dk_inter
            dv = _dot_tn(a, dout) + _dot_nt(kg, dsn16)
            dlasts.append(jnp.sum(k * dk_inter, axis=0, keepdims=True) + jnp.sum(st * dsn, axis=0, keepdims=True) * el[0:1, :])
            dla_scr[:, ks] = q * dq - k * dk
            ds_scr[vs, :] = dsn * jnp.concatenate([el, el], axis=0) + _dot_tn(dout, qg)
            dp_ref[:, ks] = (dq * (DKH ** -0.5)).astype(BF16)
            dp_ref[:, DK + h * DKH:DK + (h + 1) * DKH] = dk.astype(BF16)
            dp_ref[:, D + h * DVH:D + (h + 1) * DVH] = dv.astype(BF16)
        r = lax.broadcasted_iota(jnp.int32, (CH, CH), 0)
        c = lax.broadcasted_iota(jnp.int32, (CH, CH), 1)
        dla = _tri_matmul((r <= c).astype(BF16), dla_scr[...]) + jnp.concatenate(dlasts, axis=1)
        dz = (dla * (1.0 / TAU) * _sigmoid(-z)).astype(BF16)
        da_ref[...] = _dot_nt(dz, wau_ref[...]).astype(BF16)
        dwau_ref[...] += _dot_tn(a_ref[...], dz)
        dba_ref[...] += _rows8(dz.astype(F32))

    rev = lambda w, col: pl.BlockSpec((CH, w), lambda c: (nc - 1 - c, col))
    return _carried_call(
        body, carry, name="gla_bwd", grid=(nc,),
        in_specs=[rev(D, 0), rev(D, 1), rev(LANES, 0), _resident((LANES, DK)), _resident((1, DK)), rev(D, 0),
                  pl.BlockSpec((1, DV, DKH), lambda c: (nc - 1 - c, 0, 0)), pl.BlockSpec(memory_space=pl.ANY)],
        out_specs=[rev(2 * D, 0), rev(LANES, 0), _resident((LANES, DK)), _resident((8, DK))],
        out_shape=[SDS((t, NPROJ), BF16), SDS((t, LANES), BF16), SDS((LANES, DK), F32), SDS((8, DK), F32)],
        scratch_shapes=[pltpu.VMEM((DV, DKH), F32), pltpu.VMEM((CH, DK), F32)],
        input_output_aliases={7: 0}, compiler_params=_ARB1)(proj, proj, alr, wau, balpha, do, sall, dproj)


def _all_gather(xs, name):
    n = len(xs)

    def body(*refs):
        x_refs, out_refs = refs[:n], refs[n:2 * n]
        send_sems, recv_sems = refs[2 * n:]
        x, y, c = _my_place()
        me, sibling = (x, y, c), (x, y, 1 - c)
        chips = [(1 - x, y), (x, 1 - y), (1 - x, 1 - y)]

        def slot(p, px, py, pc):
            return out_refs[p].at[4 * px + 2 * py + pc]

        def copy(p, k, block, to, src=None):
            return pltpu.make_async_remote_copy(
                src_ref=slot(p, *block) if src is None else src, dst_ref=slot(p, *block),
                send_sem=send_sems.at[7 * p + k], recv_sem=recv_sems.at[7 * p + k], device_id=to, device_id_type=MESH_T)

        first = []
        for p in range(n):
            first.append(copy(p, 0, me, sibling, src=x_refs[p]))
            first += [copy(p, 1 + j, me, (*chip, c), src=x_refs[p]) for j, chip in enumerate(chips)]
        for cp in first:
            cp.start()
        passed = []
        for p in range(n):
            for j, chip in enumerate(chips):
                copy(p, 1 + j, (*chip, c), me).wait_recv()
                fwd = copy(p, 4 + j, (*chip, c), sibling)
                fwd.start()
                passed.append(fwd)
        for p in range(n):
            copy(p, 0, sibling, me).wait_recv()
            for j, chip in enumerate(chips):
                copy(p, 4 + j, (*chip, 1 - c), me).wait_recv()
        for cp in first + passed:
            cp.wait_send()

    hbm = pl.BlockSpec(memory_space=pl.ANY)
    return pl.pallas_call(
        body, name=name, out_shape=[SDS((N_DEV, *a.shape), a.dtype) for a in xs],
        in_specs=[hbm] * n, out_specs=[hbm] * n,
        scratch_shapes=[pltpu.SemaphoreType.DMA((7 * n,)), pltpu.SemaphoreType.DMA((7 * n,))])(*xs)


def _exchange(gs):
    n = len(gs)

    def body(*refs):
        g_refs, land_refs = refs[:n], refs[n:2 * n]
        send_sems, recv_sems, local_sems = refs[2 * n:]
        x, y, c = _my_place()
        my_idx = 4 * x + 2 * y + c
        mine = [pltpu.make_async_copy(g_refs[p].at[my_idx], land_refs[p].at[my_idx], local_sems.at[p]) for p in range(n)]
        for cp in mine:
            cp.start()
        copies = []
        for k in range(1, N_DEV):
            px, py, pc = _flip(x, k & 4), _flip(y, k & 2), _flip(c, k & 1)
            p_idx = 4 * px + 2 * py + pc
            for p in range(n):
                s = 7 * p + k - 1
                cp = pltpu.make_async_remote_copy(
                    src_ref=g_refs[p].at[p_idx], dst_ref=land_refs[p].at[my_idx], send_sem=send_sems.at[s],
                    recv_sem=recv_sems.at[s], device_id=(px, py, pc), device_id_type=MESH_T)
                cp.start()
                arrival = pltpu.make_async_remote_copy(
                    src_ref=g_refs[p].at[p_idx], dst_ref=land_refs[p].at[p_idx], send_sem=send_sems.at[s],
                    recv_sem=recv_sems.at[s], device_id=(px, py, pc), device_id_type=MESH_T)
                copies.append((cp, arrival))
        for cp, arrival in copies:
            arrival.wait_recv()
        for cp, arrival in copies:
            cp.wait_send()
        for cp in mine:
            cp.wait()

    hbm = pl.BlockSpec(memory_space=pl.ANY)
    return pl.pallas_call(
        body, name="grad_exchange", out_shape=[SDS(g.shape, g.dtype) for g in gs],
        in_specs=[hbm] * n, out_specs=[hbm] * n,
        scratch_shapes=[pltpu.SemaphoreType.DMA((7 * n,)), pltpu.SemaphoreType.DMA((7 * n,)),
                        pltpu.SemaphoreType.DMA((n,))])(*gs)


def _adamw(land, w, m, v, rows_blk, name):
    rows = w.shape[0]

    def body(l_ref, w_ref, m_ref, v_ref, g_ref, d_ref, nm_ref, nv_ref):
        g = l_ref[0].astype(F32)
        for s in range(1, N_DEV):
            g = g + l_ref[s].astype(F32)
        nm = ADAM_B1 * m_ref[...] + (1.0 - ADAM_B1) * g
        nv = ADAM_B2 * v_ref[...] + (1.0 - ADAM_B2) * (g * g)
        m_hat = nm / (1.0 - ADAM_B1 ** ADAM_STEP)
        v_hat = nv / (1.0 - ADAM_B2 ** ADAM_STEP)
        g_ref[...] = g
        d_ref[...] = -ADAM_LR * (m_hat / (jnp.sqrt(v_hat) + ADAM_EPS) + ADAM_WD * w_ref[...])
        nm_ref[...] = nm
        nv_ref[...] = nv

    blk = pl.BlockSpec((rows_blk, D), lambda i: (i, 0))
    return pl.pallas_call(
        body, name=name, grid=(rows // rows_blk,),
        in_specs=[pl.BlockSpec((N_DEV, rows_blk, D), lambda i: (0, i, 0)), blk, blk, blk],
        out_specs=[blk] * 4, out_shape=[SDS((rows, D), F32)] * 4, compiler_params=_ARB1)(land, w, m, v)


BIG = ("w_in", "w_up", "w_down", "w_gla_o", "w_conf_o", "w_out")
BIG_TRANSPOSED = ("w_in", "w_up")
SMALL_SHARDED = ("meta_tokens", "conf_dw_w", "ffn_dw_w", "w_alpha_up")
REPLICATED = ("norm_mix_g", "b_alpha", "gla_norm_g", "conf_dw_b", "conf_ln_g", "conf_ln_b", "norm_ffn_g", "ffn_dw_b",
              "final_norm_g")
N_IN = sum(IN_WIDTHS)
W_IN_ROWS = N_IN // N_DEV
W_IN_PAD = -(-W_IN_ROWS // 16) * 16
ADAM_BLOCK = {"w_in": W_IN_PAD // 3, "w_up": 176, "w_down": 176, "w_gla_o": 128, "w_conf_o": 128, "w_out": 128}
SMALL_ROWS = 32


def _to_panel(name, shard):
    a = shard.reshape(shard.shape[-2], shard.shape[-1])
    if name in BIG_TRANSPOSED:
        a = a.T
    if name == "w_in":
        a = jnp.pad(a, ((0, W_IN_PAD - W_IN_ROWS), (0, 0)))
    return a


def _from_panel(name, panel, shape):
    a = panel[0:W_IN_ROWS] if name == "w_in" else panel
    if name in BIG_TRANSPOSED:
        a = a.T
    return a.reshape(shape)


def _pack_small(arrs):
    flat = jnp.concatenate([jnp.pad(a.reshape(-1), (0, (-a.size) % D)) for a in arrs])
    return jnp.pad(flat, (0, SMALL_ROWS * D - flat.shape[0])).reshape(SMALL_ROWS, D)


def _unpack_small(panel, shapes):
    flat, out, off = panel.reshape(-1), [], 0
    for shp in shapes:
        n = 1
        for s in shp:
            n *= s
        out.append(flat[off:off + n].reshape(shp))
        off += n + (-n) % D
    return out


def _local_step(x, target, w, shards=None):
    dist = shards is not None
    w = dict(w)

    def gather(names):
        return [(shards[n], False) for n in names] if dist else []

    def scatter(*arrs):
        return [(a.reshape(N_DEV, -1, D), True) for a in arrs] if dist else []

    s = x.shape[0]
    n_real = s + N_META
    t = -(-n_real // TM) * TM

    q0, r0, a0, c0 = 0, 2 * DK + DV, 2 * DK + 2 * DV, 2 * DK + 2 * DV + RANK
    wt = w["w_in_t"]
    w_main = jnp.concatenate([wt[q0:r0], wt[c0:N_IN], wt[r0:a0]], axis=0)
    w_a = jnp.pad(wt[a0:c0], ((0, LANES - RANK), (0, 0)))
    wau = jnp.pad(w["w_alpha_up"].astype(BF16), ((0, LANES - RANK), (0, 0)))
    row = lambda name: w[name].reshape(1, -1)
    cw = jnp.broadcast_to(jnp.pad(w["conf_dw_w"], ((0, 32 - CONF_K), (0, 0)))[:, None, :], (32, 8, D))
    fw = jnp.broadcast_to(jnp.concatenate([w["ffn_dw_w"], w["ffn_dw_b"].reshape(1, -1)], axis=0)[:, None, :],
                          (FFN_K + 1, 16, DFF))

    early = ("w_gla_o", "w_conf_o", "w_out", "w_up")
    h0, u1, proj, alr, *landed = _in_proj(x, w["meta_tokens"], row("norm_mix_g"), w_main, w_a, t, carry=gather(early))
    for n, land in zip(early, landed):
        w["w_up_t" if n == "w_up" else n] = land.reshape(-1, D)
    o, og, sall = _gla_fwd(proj, alr, wau, row("b_alpha"), row("gla_norm_g"))
    cc, cs, *landed = _conf_fwd(proj, cw, row("conf_dw_b"), row("conf_ln_g"), row("conf_ln_b"), carry=gather(("w_down",)))
    if dist:
        w["w_down"] = landed[0].reshape(-1, D)
    brg, brc, merged, h1 = _mix_fwd(og, cs, proj, h0, w["w_gla_o"], w["w_conf_o"], w["w_out"])
    u2, up = _norm_matmul(h1, row("norm_ffn_g"), w["w_up_t"], 512, "up_proj")
    f, dh2, red = _ffn_out(up, fw, h1, w["w_down"], row("final_norm_g"), target)
    loss = 0.5 / D * jnp.sum(red[0:8])

    g = {"final_norm_g": jnp.sum(red[8:16], axis=0)}
    dup, dfw = _ffn_bwd(dh2, w["w_down"], up, fw)
    g["ffn_dw_w"] = jnp.sum(dfw[0:FFN_K], axis=1)
    g["ffn_dw_b"] = jnp.sum(dfw[3], axis=0)
    g["w_down"] = _wgrad(f, dh2, 1408, "wgrad_down")
    dh1, dg2, *landed = _dgrad_norm(dup, w["w_up_t"], h1, row("norm_ffn_g"), dh2, "up_dgrad", carry=scatter(g["w_down"]))
    if dist:
        g["w_down"] = landed[0]
    g["norm_ffn_g"] = jnp.sum(dg2, axis=0)
    g["w_up_t"] = _wgrad(dup, u2, 1408, "wgrad_up")
    dbrg, dbrc, dog, dcs, dproj = _mix_bwd(dh1, w["w_out"], w["w_gla_o"], w["w_conf_o"], proj, brg, brc)
    g["w_out"] = _wgrad(merged, dh1, 1024, "wgrad_out")
    g["w_gla_o"] = _wgrad(og, dbrg, 1024, "wgrad_gla_o")
    g["w_conf_o"] = _wgrad(cs, dbrc, 1024, "wgrad_conf_o")
    do, dproj, dgn = _glapost_bwd(dog, o, proj, row("gla_norm_g"), dproj)
    g["gla_norm_g"] = jnp.sum(dgn, axis=0)
    dproj, dcw, dst, *landed = _conf_bwd(dcs, cc, proj, cw, row("conf_ln_g"), row("conf_ln_b"), dproj,
                                         carry=scatter(g["w_up_t"]))
    if dist:
        g["w_up_t"] = landed[0]
    g["conf_dw_w"] = jnp.sum(dcw[0:CONF_K], axis=1)
    g["conf_ln_g"], g["conf_ln_b"], g["conf_dw_b"] = jnp.sum(dst[0], axis=0), jnp.sum(dst[1], axis=0), jnp.sum(dst[2], axis=0)
    dproj, dalr, dwau, dba, *landed = _gla_bwd(proj, alr, wau, row("b_alpha"), do, sall, dproj,
                                               carry=scatter(g["w_out"], g["w_gla_o"], g["w_conf_o"]))
    if dist:
        g["w_out"], g["w_gla_o"], g["w_conf_o"] = landed
    g["w_alpha_up"] = dwau[0:RANK]
    g["b_alpha"] = jnp.sum(dba, axis=0)
    dw_main = _wgrad(dproj, u1, 1024, "wgrad_in")
    dw_a = _wgrad(dalr, u1, LANES, "wgrad_alr")
    g["w_in_t"] = jnp.concatenate([dw_main[0:r0], dw_main[NPROJ - DV:NPROJ], dw_a[0:RANK], dw_main[r0:NPROJ - DV]], axis=0)
    w_in_blocks = []
    if dist:
        pad = ((0, 0), (0, W_IN_PAD - W_IN_ROWS), (0, 0))
        w_in_blocks = [(jnp.pad(g["w_in_t"].reshape(N_DEV, W_IN_ROWS, D), pad), True)]
    grad_x, g["meta_tokens"], dg1, *landed = _in_dgrad(dproj, w_main, h0, row("norm_mix_g"), dh1, dalr, w_a, s,
                                                       carry=w_in_blocks)
    if dist:
        g["w_in_t"] = landed[0]
    g["norm_mix_g"] = jnp.sum(dg1, axis=0)
    return loss, grad_x, g


def kernel(x, meta_tokens, norm_mix_g, w_in, w_alpha_up, b_alpha, gla_norm_g, w_gla_o, conf_dw_w, conf_dw_b, conf_ln_g, conf_ln_b, w_conf_o, w_out, norm_ffn_g, w_up, ffn_dw_w, ffn_dw_b, w_down, final_norm_g, loss_target, m_meta_tokens, m_norm_mix_g, m_w_in, m_w_alpha_up, m_b_alpha, m_gla_norm_g, m_w_gla_o, m_conf_dw_w, m_conf_dw_b, m_conf_ln_g, m_conf_ln_b, m_w_conf_o, m_w_out, m_norm_ffn_g, m_w_up, m_ffn_dw_w, m_ffn_dw_b, m_w_down, m_final_norm_g, v_meta_tokens, v_norm_mix_g, v_w_in, v_w_alpha_up, v_b_alpha, v_gla_norm_g, v_w_gla_o, v_conf_dw_w, v_conf_dw_b, v_conf_ln_g, v_conf_ln_b, v_w_conf_o, v_w_out, v_norm_ffn_g, v_w_up, v_ffn_dw_w, v_ffn_dw_b, v_w_down, v_final_norm_g):
    ws = dict(meta_tokens=meta_tokens, norm_mix_g=norm_mix_g, w_in=w_in, w_alpha_up=w_alpha_up, b_alpha=b_alpha,
              gla_norm_g=gla_norm_g, w_gla_o=w_gla_o, conf_dw_w=conf_dw_w, conf_dw_b=conf_dw_b, conf_ln_g=conf_ln_g,
              conf_ln_b=conf_ln_b, w_conf_o=w_conf_o, w_out=w_out, norm_ffn_g=norm_ffn_g, w_up=w_up, ffn_dw_w=ffn_dw_w,
              ffn_dw_b=ffn_dw_b, w_down=w_down, final_norm_g=final_norm_g)
    ms = dict(meta_tokens=m_meta_tokens, norm_mix_g=m_norm_mix_g, w_in=m_w_in, w_alpha_up=m_w_alpha_up, b_alpha=m_b_alpha,
              gla_norm_g=m_gla_norm_g, w_gla_o=m_w_gla_o, conf_dw_w=m_conf_dw_w, conf_dw_b=m_conf_dw_b,
              conf_ln_g=m_conf_ln_g, conf_ln_b=m_conf_ln_b, w_conf_o=m_w_conf_o, w_out=m_w_out, norm_ffn_g=m_norm_ffn_g,
              w_up=m_w_up, ffn_dw_w=m_ffn_dw_w, ffn_dw_b=m_ffn_dw_b, w_down=m_w_down, final_norm_g=m_final_norm_g)
    vs = dict(meta_tokens=v_meta_tokens, norm_mix_g=v_norm_mix_g, w_in=v_w_in, w_alpha_up=v_w_alpha_up, b_alpha=v_b_alpha,
              gla_norm_g=v_gla_norm_g, w_gla_o=v_w_gla_o, conf_dw_w=v_conf_dw_w, conf_dw_b=v_conf_dw_b,
              conf_ln_g=v_conf_ln_g, conf_ln_b=v_conf_ln_b, w_conf_o=v_w_conf_o, w_out=v_w_out, norm_ffn_g=v_norm_ffn_g,
              w_up=v_w_up, ffn_dw_w=v_ffn_dw_w, ffn_dw_b=v_ffn_dw_b, w_down=v_w_down, final_norm_g=v_final_norm_g)
    small = SMALL_SHARDED + REPLICATED
    pack_small = lambda d: _pack_small([d[n] for n in small])

    shards = {n: _to_panel(n, ws[n]).astype(BF16) for n in BIG}
    own = [shards["w_in"], pack_small(ws)]
    my_idx = 4 * lax.axis_index("x") + 2 * lax.axis_index("y") + lax.axis_index("c")
    gathered = [lax.dynamic_update_slice(full_, mine[None], (my_idx, 0, 0))
                for full_, mine in zip(_all_gather(own, "weight_gather"), own)]
    full = {n: ws[n].reshape(-1) for n in REPLICATED}
    full["w_in_t"] = gathered[0][:, 0:W_IN_ROWS].reshape(N_IN, D)
    flat, off = gathered[1].reshape(N_DEV, -1), 0
    for n in SMALL_SHARDED:
        k, c = ws[n].shape[-2], ws[n].shape[-1]
        full[n] = flat[:, off:off + k * c].reshape(N_DEV, k, c).transpose(1, 0, 2).reshape(k, N_DEV * c)
        off += k * c + (-(k * c)) % D

    loss, grad_x, g = _local_step(x[0], loss_target[0], full, shards)

    lands = [g["w_in_t"], g["w_up_t"]] + [g[n] for n in BIG[2:]]
    blocks = []
    for n in SMALL_SHARDED:
        k, c = ws[n].shape[-2], ws[n].shape[-1]
        b = g[n].reshape(k, N_DEV, c).transpose(1, 0, 2).reshape(N_DEV, k * c)
        blocks.append(jnp.pad(b, ((0, 0), (0, (-(k * c)) % D))))
    for n in REPLICATED:
        b = jnp.broadcast_to(g[n].reshape(1, -1), (N_DEV, g[n].size))
        blocks.append(jnp.pad(b, ((0, 0), (0, (-b.shape[1]) % D))))
    gsm = jnp.concatenate(blocks, axis=1)
    lands += _exchange([jnp.pad(gsm, ((0, 0), (0, SMALL_ROWS * D - gsm.shape[1]))).reshape(N_DEV, SMALL_ROWS, D)])

    grad, delta, new_m, new_v = {}, {}, {}, {}
    for i, n in enumerate(BIG):
        outs = _adamw(lands[i], _to_panel(n, ws[n]), _to_panel(n, ms[n]), _to_panel(n, vs[n]), ADAM_BLOCK[n], "adamw_" + n)
        grad[n], delta[n], new_m[n], new_v[n] = [_from_panel(n, p, ws[n].shape) for p in outs]
    outs = _adamw(lands[len(BIG)], pack_small(ws), pack_small(ms), pack_small(vs), SMALL_ROWS, "adamw_small")
    shapes = [ws[n].shape for n in small]
    for d, p in zip((grad, delta, new_m, new_v), outs):
        d.update(zip(small, _unpack_small(p, shapes)))

    order = ("meta_tokens", "norm_mix_g", "w_in", "w_alpha_up", "b_alpha", "gla_norm_g", "w_gla_o", "conf_dw_w", "conf_dw_b",
             "conf_ln_g", "conf_ln_b", "w_conf_o", "w_out", "norm_ffn_g", "w_up", "ffn_dw_w", "ffn_dw_b", "w_down",
             "final_norm_g")
    loss = lax.psum(loss, ("x", "y", "c"))
    return (loss, grad_x[None], *[grad[n] for n in order], *[delta[n] for n in order], *[new_m[n] for n in order],
            *[new_v[n] for n in order])
```

```python
import functools

import jax
import jax.numpy as jnp
from jax import lax
from jax.experimental import pallas as pl
from jax.experimental.pallas import tpu as pltpu

F32, BF16 = jnp.float32, jnp.bfloat16
SDS = jax.ShapeDtypeStruct

D = 1024
N_META = 16
HEADS = 4
DK, DKH, DV, DVH = 512, 128, 1024, 256
RANK = 16
TAU = 16.0
CONF_K = 31
DFF = 2816
FFN_K = 3
IN_WIDTHS = (DK, DK, DV, DV, RANK, 2 * D, D, D)
RMS_EPS, LN_EPS = 1e-6, 1e-5
ADAM_LR, ADAM_B1, ADAM_B2, ADAM_EPS, ADAM_WD, ADAM_STEP = 0.001, 0.9, 0.999, 1e-08, 0.01, 10

NPROJ = 7 * D
LANES = 128
CH = 128
TM = 640
TM_BIG = 1664
TE = 320
HALO = 32
HALO_F = 16
N_DEV = 8
VMEM_LIMIT = 60 * 1024 * 1024
MESH_T = pl.DeviceIdType.MESH

_ARB1 = pltpu.CompilerParams(dimension_semantics=("arbitrary",), vmem_limit_bytes=VMEM_LIMIT)
_ARB2 = pltpu.CompilerParams(dimension_semantics=("arbitrary", "arbitrary"), vmem_limit_bytes=VMEM_LIMIT)


def _dot(a, b):
    return jnp.dot(a, b, preferred_element_type=F32)


def _dot_nt(a, b):
    return lax.dot_general(a, b, (((1,), (1,)), ((), ())), preferred_element_type=F32)


def _dot_tn(a, b):
    return lax.dot_general(a, b, (((0,), (0,)), ((), ())), preferred_element_type=F32)


def _sigmoid(x):
    return 0.5 * jnp.tanh(0.5 * x) + 0.5


def _rows8(x):
    return x.reshape(x.shape[0] // 8, 8, x.shape[1]).sum(axis=0)


def _row_tile(t, preferred):
    return preferred if t % preferred == 0 else TM


def _row_loop(n_rows, rb, fn, unroll=1):
    def step(i, carry):
        fn(pl.multiple_of(i * rb, rb))
        return carry
    lax.fori_loop(0, n_rows // rb, step, 0, unroll=unroll)


def _resident(shape):
    return pl.BlockSpec(shape, lambda *_: (0,) * len(shape))


def _once(shape):
    return pl.BlockSpec(shape, lambda *_: (0,) * len(shape), pipeline_mode=pl.Buffered(1))


CONV_ROWS, CONV_LANES = 64, 256


def _shift_classes(offset_taps):
    return [(s, [(o - s, j) for o, j in offset_taps if o % 8 == s]) for s in range(8)]


def _shifted(win, s):
    return win if s == 0 else win[s:s + CONV_ROWS + HALO - 8, :]


def _my_place():
    return lax.axis_index("x"), lax.axis_index("y"), lax.axis_index("c")


def _flip(v, bit):
    return 1 - v if bit else v


def _exchange_copies(src_refs, land_refs, scatter, send_sems, recv_sems, local_sems, arrivals):
    x, y, c = _my_place()
    my_idx = 4 * x + 2 * y + c
    local, remote = [], []
    for p, (src, land) in enumerate(zip(src_refs, land_refs)):
        local.append(pltpu.make_async_copy(src.at[my_idx] if scatter[p] else src, land.at[my_idx], local_sems.at[p]))
    for k in range(1, N_DEV):
        px, py, pc = _flip(x, k & 4), _flip(y, k & 2), _flip(c, k & 1)
        p_idx = 4 * px + 2 * py + pc
        for p, (src, land) in enumerate(zip(src_refs, land_refs)):
            s = 7 * p + k - 1
            out = src.at[p_idx] if scatter[p] else src

            def copy(dst):
                return pltpu.make_async_remote_copy(src_ref=out, dst_ref=dst, send_sem=send_sems.at[s],
                                                    recv_sem=recv_sems.at[s], device_id=(px, py, pc), device_id_type=MESH_T)
            remote.append((copy(land.at[my_idx]), copy(land.at[p_idx]) if arrivals else None))
    return local, remote


def _carried_call(core, carry, *, grid, in_specs, out_specs, out_shape, scratch_shapes=(), **kw):
    n_in, n_out, nc, n_scr = len(in_specs), len(out_specs), len(carry), len(scratch_shapes)
    if nc == 0:
        return pl.pallas_call(core, grid=grid, in_specs=in_specs, out_specs=out_specs, out_shape=out_shape,
                              scratch_shapes=list(scratch_shapes), **kw)
    scatter = [sc for _, sc in carry]

    def body(*refs):
        ins, cin = refs[:n_in], refs[n_in:n_in + nc]
        outs, cout = refs[n_in + nc:n_in + nc + n_out], refs[n_in + nc + n_out:n_in + 2 * nc + n_out]
        scr, sems = refs[n_in + 2 * nc + n_out:n_in + 2 * nc + n_out + n_scr], refs[-3:]
        first = functools.reduce(jnp.logical_and, [pl.program_id(a) == 0 for a in range(len(grid))])
        last = functools.reduce(jnp.logical_and, [pl.program_id(a) == grid[a] - 1 for a in range(len(grid))])

        @pl.when(first)
        def _():
            local, remote = _exchange_copies(cin, cout, scatter, *sems, arrivals=False)
            for cp in local:
                cp.start()
            for send, _ in remote:
                send.start()

        core(*ins, *outs, *scr)

        @pl.when(last)
        def _():
            local, remote = _exchange_copies(cin, cout, scatter, *sems, arrivals=True)
            for _, arrival in remote:
                arrival.wait_recv()
            for send, _ in remote:
                send.wait_send()
            for cp in local:
                cp.wait()

    hbm = pl.BlockSpec(memory_space=pl.ANY)
    land_shape = [SDS((N_DEV, *(a.shape[1:] if sc else a.shape)), a.dtype) for a, sc in carry]
    sems = [pltpu.SemaphoreType.DMA((7 * nc,)), pltpu.SemaphoreType.DMA((7 * nc,)), pltpu.SemaphoreType.DMA((nc,))]
    call = pl.pallas_call(body, grid=grid, in_specs=list(in_specs) + [hbm] * nc, out_specs=list(out_specs) + [hbm] * nc,
                          out_shape=list(out_shape) + land_shape, scratch_shapes=list(scratch_shapes) + sems, **kw)
    return lambda *args: call(*args, *[a for a, _ in carry])


def _norm_matmul(h, g, w_t, tn, name, w_extra_t=None, carry=()):
    t, n = h.shape[0], w_t.shape[0]
    tm = _row_tile(t, TM_BIG)
    nt, nb = t // tm, n // tn

    def body(*refs):
        if w_extra_t is None:
            h_ref, g_ref, w_ref, u_ref, p_ref = refs
        else:
            h_ref, g_ref, w_ref, we_ref, u_ref, p_ref, e_ref = refs

        @pl.when(pl.program_id(1) == 0)
        def _():
            def blk(r0):
                x = h_ref[pl.ds(r0, 32), :]
                rinv = lax.rsqrt(jnp.mean(x * x, axis=-1, keepdims=True) + RMS_EPS)
                u_ref[pl.ds(r0, 32), :] = (x * rinv * g_ref[...]).astype(BF16)
            _row_loop(tm, 32, blk, unroll=4)
            if w_extra_t is not None:
                e_ref[...] = _dot_nt(u_ref[...], we_ref[...]).astype(BF16)

        p_ref[...] = _dot_nt(u_ref[...], w_ref[...]).astype(BF16)

    in_specs = [pl.BlockSpec((tm, D), lambda i, j: (i, 0)), _resident((1, D)), pl.BlockSpec((tn, D), lambda i, j: (j, 0))]
    out_specs = [pl.BlockSpec((tm, D), lambda i, j: (i, 0)), pl.BlockSpec((tm, tn), lambda i, j: (i, j))]
    out_shape = [SDS((t, D), BF16), SDS((t, n), BF16)]
    args = [h, g, w_t]
    if w_extra_t is not None:
        in_specs.append(_resident(w_extra_t.shape))
        out_specs.append(pl.BlockSpec((tm, w_extra_t.shape[0]), lambda i, j: (i, 0)))
        out_shape.append(SDS((t, w_extra_t.shape[0]), BF16))
        args.append(w_extra_t)
    return _carried_call(body, carry, name=name, grid=(nt, nb), in_specs=in_specs, out_specs=out_specs,
                         out_shape=out_shape, compiler_params=_ARB2)(*args)


def _in_proj(x, meta, g, w_t, w_a_t, t, carry=()):
    n_real = x.shape[0] + N_META
    n, tn = w_t.shape[0], 1024
    tm = _row_tile(t, TM_BIG)
    nt, nb = t // tm, n // tn

    def body(x_ref, xh_ref, m_ref, g_ref, w_ref, wa_ref, h_ref, u_ref, p_ref, e_ref):
        i = pl.program_id(0)

        @pl.when(pl.program_id(1) == 0)
        def _():
            def rows_of(r0, val):
                gid = i * tm + r0 + lax.broadcasted_iota(jnp.int32, (32, 1), 0)
                val = jnp.where(gid < n_real, val, 0.0)
                h_ref[pl.ds(r0, 32), :] = val
                rinv = lax.rsqrt(jnp.mean(val * val, axis=-1, keepdims=True) + RMS_EPS)
                u_ref[pl.ds(r0, 32), :] = (val * rinv * g_ref[...]).astype(BF16)

            before = jnp.where(i == 0, m_ref[...], xh_ref[...])
            rows_of(0, jnp.concatenate([before, x_ref[0:N_META, :]], axis=0))

            def blk(k, c):
                r0 = pl.multiple_of(k * 32, 32)
                rows_of(r0, x_ref[pl.ds(pl.multiple_of(r0 - N_META, N_META), 32), :])
                return c
            lax.fori_loop(1, tm // 32, blk, 0, unroll=3)
            e_ref[...] = _dot_nt(u_ref[...], wa_ref[...]).astype(BF16)

        p_ref[...] = _dot_nt(u_ref[...], w_ref[...]).astype(BF16)

    row = lambda w: pl.BlockSpec((tm, w), lambda i, j: (i, 0))
    return _carried_call(
        body, carry, name="in_proj", grid=(nt, nb),
        in_specs=[row(D), pl.BlockSpec((N_META, D), lambda i, j: (jnp.maximum(i * (tm // N_META) - 1, 0), 0)),
                  _resident((N_META, D)), _resident((1, D)), pl.BlockSpec((tn, D), lambda i, j: (j, 0)),
                  _resident(w_a_t.shape)],
        out_specs=[row(D), row(D), pl.BlockSpec((tm, tn), lambda i, j: (i, j)), row(w_a_t.shape[0])],
        out_shape=[SDS((t, D), F32), SDS((t, D), BF16), SDS((t, n), BF16), SDS((t, w_a_t.shape[0]), BF16)],
        compiler_params=_ARB2)(x, x, meta, g, w_t, w_a_t)


def _gla_decay(a_ref, wau_ref, ba_ref):
    z = _dot(a_ref[...], wau_ref[...]) + ba_ref[...]
    la = (jnp.minimum(z, 0.0) - jnp.log(1.0 + jnp.exp(-jnp.abs(z)))) * (1.0 / TAU)
    r = lax.broadcasted_iota(jnp.int32, (CH, CH), 0)
    c = lax.broadcasted_iota(jnp.int32, (CH, CH), 1)
    b = _tri_matmul((r >= c).astype(BF16), la)
    mid = jnp.broadcast_to(b[CH // 2:CH // 2 + 1, :], b.shape)
    last = jnp.broadcast_to(b[CH - 1:CH, :], b.shape)
    return z, b, mid, last, r >= c


def _tri_matmul(tri, x):
    n = x.shape[1]
    x1 = x.astype(BF16)
    r1 = x - x1.astype(F32)
    x2 = r1.astype(BF16)
    x3 = (r1 - x2.astype(F32)).astype(BF16)
    y = _dot(tri, jnp.concatenate([x1, x2, x3], axis=1))
    return y[:, 0:n] + y[:, n:2 * n] + y[:, 2 * n:3 * n]


def _gla_fwd(proj, alr, wau, balpha, gn):
    t = proj.shape[0]
    nc = t // CH

    def body(qk_ref, v_ref, r_ref, a_ref, wau_ref, ba_ref, gn_ref, o_ref, og_ref, sall_ref, s_scr):
        @pl.when(pl.program_id(0) == 0)
        def _():
            s_scr[...] = jnp.zeros_like(s_scr)

        sall_ref[0] = s_scr[...]
        _, b, bmid, blast, causal = _gla_decay(a_ref, wau_ref, ba_ref)
        for h in range(HEADS):
            ks = slice(h * DKH, (h + 1) * DKH)
            vs = slice(h * DVH, (h + 1) * DVH)
            bh, mh, lh = b[:, ks], bmid[:, ks], blast[:, ks]
            q = qk_ref[:, ks].astype(F32) * (DKH ** -0.5)
            k = qk_ref[:, DK + h * DKH:DK + (h + 1) * DKH].astype(F32)
            v = v_ref[:, vs]
            qt = (q * jnp.exp(bh - mh)).astype(BF16)
            kt = (k * jnp.exp(mh - bh)).astype(BF16)
            qg = (q * jnp.exp(bh)).astype(BF16)
            kg = (k * jnp.exp(lh - bh)).astype(BF16)
            a = jnp.where(causal, _dot_nt(qt, kt), 0.0)
            st = s_scr[vs, :]
            o = _dot(a.astype(BF16), v) + _dot_nt(qg, st.astype(BF16))
            el = jnp.exp(lh)
            s_scr[vs, :] = st * jnp.concatenate([el, el], axis=0) + _dot_tn(v, kg)
            o_ref[:, vs] = o
            on = o * lax.rsqrt(jnp.mean(o * o, axis=-1, keepdims=True) + RMS_EPS) * gn_ref[:, vs]
            rr = r_ref[:, vs].astype(F32)
            og_ref[:, vs] = (on * (rr * _sigmoid(rr))).astype(BF16)

    return pl.pallas_call(
        body, name="gla_fwd", grid=(nc,),
        in_specs=[pl.BlockSpec((CH, D), lambda c: (c, 0)), pl.BlockSpec((CH, D), lambda c: (c, 1)),
                  pl.BlockSpec((CH, D), lambda c: (c, 6)), pl.BlockSpec((CH, LANES), lambda c: (c, 0)),
                  _resident((LANES, DK)), _resident((1, DK)), _resident((1, DV))],
        out_specs=[pl.BlockSpec((CH, DV), lambda c: (c, 0)), pl.BlockSpec((CH, DV), lambda c: (c, 0)),
                   pl.BlockSpec((1, DV, DKH), lambda c: (c, 0, 0))],
        out_shape=[SDS((t, DV), F32), SDS((t, DV), BF16), SDS((nc, DV, DKH), F32)],
        scratch_shapes=[pltpu.VMEM((DV, DKH), F32)], compiler_params=_ARB1)(proj, proj, proj, alr, wau, balpha, gn)


def _conf_fwd(proj, cw, cb, lg, lb, carry=()):
    t = proj.shape[0]
    nt = t // TE

    def body(c1_ref, c2_ref, cw_ref, cb_ref, lg_ref, lb_ref, cc_ref, cs_ref, cext):
        i = pl.program_id(0)

        @pl.when(i == 0)
        def _():
            cext[0:HALO, :] = jnp.zeros((HALO, D), F32)

        @pl.when(i > 0)
        def _():
            cext[0:HALO, :] = cext[TE:TE + HALO, :]

        def glu(r0):
            c2 = c2_ref[pl.ds(r0, 32), :].astype(F32)
            cext[pl.ds(HALO + r0, 32), :] = c1_ref[pl.ds(r0, 32), :].astype(F32) * _sigmoid(c2)
        _row_loop(TE, 32, glu)

        def conv(r0):
            for part in range(D // CONV_LANES):
                cols = slice(part * CONV_LANES, (part + 1) * CONV_LANES)
                win = cext[pl.ds(r0, CONV_ROWS + HALO), cols]
                acc = jnp.zeros((CONV_ROWS, CONV_LANES), F32) + cb_ref[:, cols]
                for s, taps in _shift_classes([(2 + j, j) for j in range(CONF_K)]):
                    ws = _shifted(win, s)
                    for a8, j in taps:
                        acc = acc + jnp.tile(cw_ref[j, :, cols], (CONV_ROWS // 8, 1)) * ws[a8:a8 + CONV_ROWS, :]
                cc_ref[pl.ds(r0, CONV_ROWS), cols] = acc
            for sub in range(CONV_ROWS // 32):
                rows = pl.ds(r0 + 32 * sub, 32)
                x = cc_ref[rows, :]
                xc = x - jnp.mean(x, axis=-1, keepdims=True)
                var = jnp.mean(xc * xc, axis=-1, keepdims=True)
                ln = xc * lax.rsqrt(var + LN_EPS) * lg_ref[...] + lb_ref[...]
                cs_ref[rows, :] = (ln * _sigmoid(ln)).astype(BF16)
        _row_loop(TE, CONV_ROWS, conv)

    return _carried_call(
        body, carry, name="conf_fwd", grid=(nt,),
        in_specs=[pl.BlockSpec((TE, D), lambda i: (i, 2)), pl.BlockSpec((TE, D), lambda i: (i, 3)),
                  _resident((32, 8, D)), _resident((1, D)), _resident((1, D)), _resident((1, D))],
        out_specs=[pl.BlockSpec((TE, D), lambda i: (i, 0)), pl.BlockSpec((TE, D), lambda i: (i, 0))],
        out_shape=[SDS((t, D), F32), SDS((t, D), BF16)],
        scratch_shapes=[pltpu.VMEM((TE + HALO, D), F32)], compiler_params=_ARB1)(proj, proj, cw, cb, lg, lb)


def _mix_fwd(og, cs, proj, h0, wg, wc, wo):
    t = h0.shape[0]
    nt = t // TM

    def body(og_ref, cs_ref, g_ref, h0_ref, wg_ref, wc_ref, wo_ref, brg_ref, brc_ref, mg_ref, h1_ref):
        brg_ref[...] = _dot(og_ref[...], wg_ref[...]).astype(BF16)
        brc_ref[...] = _dot(cs_ref[...], wc_ref[...]).astype(BF16)

        def blk(r0):
            rows = pl.ds(r0, 32)
            gg = g_ref[rows, 0:D].astype(F32)
            gc = g_ref[rows, D:2 * D].astype(F32)
            m = _sigmoid(gg) * brg_ref[rows, :].astype(F32) + _sigmoid(gc) * brc_ref[rows, :].astype(F32)
            mg_ref[rows, :] = m.astype(BF16)
        _row_loop(TM, 32, blk)
        h1_ref[...] = h0_ref[...] + _dot(mg_ref[...], wo_ref[...])

    row = lambda w: pl.BlockSpec((TM, w), lambda i: (i, 0))
    return pl.pallas_call(
        body, name="mix_fwd", grid=(nt,),
        in_specs=[row(D), row(D), pl.BlockSpec((TM, 2 * D), lambda i: (i, 2)), row(D),
                  _once((D, D)), _once((D, D)), _once((D, D))],
        out_specs=[row(D), row(D), row(D), row(D)],
        out_shape=[SDS((t, D), BF16), SDS((t, D), BF16), SDS((t, D), BF16), SDS((t, D), F32)],
        compiler_params=_ARB1)(og, cs, proj, h0, wg, wc, wo)


def _ffn_out(up, fw, h1, wd, gf, target):
    t = h1.shape[0]
    nt = t // TE
    n_real = target.shape[0] + N_META

    def body(a_ref, bv_ref, fw_ref, h1_ref, wd_ref, gf_ref, tg_ref, tb_ref, f_ref, dh2_ref, red_ref, aext, hs):
        i = pl.program_id(0)

        @pl.when(i == 0)
        def _():
            aext[0:HALO_F, :] = jnp.zeros((HALO_F, DFF), F32)
            red_ref[...] = jnp.zeros_like(red_ref)

        @pl.when(i > 0)
        def _():
            aext[0:HALO_F, :] = aext[TE:TE + HALO_F, :]

        def cp(r0):
            aext[pl.ds(HALO_F + r0, 16), :] = a_ref[pl.ds(r0, 16), :].astype(F32)
        _row_loop(TE, 16, cp)

        def conv(r0):
            win = aext[pl.ds(r0, 32), :]
            ac = fw_ref[3] + fw_ref[0] * win[14:30, :] + fw_ref[1] * win[15:31, :] + fw_ref[2] * win[16:32, :]
            f_ref[pl.ds(r0, 16), :] = (ac * _sigmoid(ac) * bv_ref[pl.ds(r0, 16), :].astype(F32)).astype(BF16)
        _row_loop(TE, 16, conv)
        hs[...] = h1_ref[...] + _dot(f_ref[...], wd_ref[...])

        def head(r0, tg):
            rows = pl.ds(r0, 32)
            h2 = hs[rows, :]
            rinv = lax.rsqrt(jnp.mean(h2 * h2, axis=-1, keepdims=True) + RMS_EPS)
            hh = h2 * rinv
            gid = i * TE + r0 + lax.broadcasted_iota(jnp.int32, (32, 1), 0)
            live = jnp.logical_and(gid >= N_META, gid < n_real)
            err = jnp.where(live, hh * gf_ref[...] - tg, 0.0)
            dy = err * (1.0 / D)
            red_ref[0:8, :] += _rows8(err * err)
            red_ref[8:16, :] += _rows8(dy * hh)
            dhh = dy * gf_ref[...]
            dh2_ref[rows, :] = rinv * (dhh - hh * jnp.mean(dhh * hh, axis=-1, keepdims=True))

        head(0, jnp.concatenate([tb_ref[...], tg_ref[0:N_META, :]], axis=0))

        def blk(k, c):
            r0 = pl.multiple_of(k * 32, 32)
            head(r0, tg_ref[pl.ds(pl.multiple_of(r0 - N_META, N_META), 32), :])
            return c
        lax.fori_loop(1, TE // 32, blk, 0, unroll=3)

    row = lambda w: pl.BlockSpec((TE, w), lambda i: (i, 0))
    return pl.pallas_call(
        body, name="ffn_out", grid=(nt,),
        in_specs=[pl.BlockSpec((TE, DFF), lambda i: (i, 0)), pl.BlockSpec((TE, DFF), lambda i: (i, 1)),
                  _resident((4, 16, DFF)), row(D), _resident((DFF, D)), _resident((1, D)), row(D),
                  pl.BlockSpec((N_META, D), lambda i: (jnp.maximum(i * (TE // N_META) - 1, 0), 0))],
        out_specs=[row(DFF), row(D), _resident((16, D))],
        out_shape=[SDS((t, DFF), BF16), SDS((t, D), F32), SDS((16, D), F32)],
        scratch_shapes=[pltpu.VMEM((TE + HALO_F, DFF), F32), pltpu.VMEM((TE, D), F32)],
        compiler_params=_ARB1)(up, up, fw, h1, wd, gf, target, target)


def _ffn_bwd(dh2, wd, up, fw):
    t = dh2.shape[0]
    nt = t // TE
    hb = TE // HALO_F

    def body(dh_ref, wd_ref, a_ref, ah_ref, bv_ref, fw_ref, dup_ref, dw_ref, aext, dax, dfs):
        i = pl.program_id(0)
        ti = nt - 1 - i

        @pl.when(i == 0)
        def _():
            dax[TE:TE + HALO_F, :] = jnp.zeros((HALO_F, DFF), F32)
            dw_ref[...] = jnp.zeros_like(dw_ref)

        @pl.when(i > 0)
        def _():
            dax[TE:TE + HALO_F, :] = dax[0:HALO_F, :]

        aext[0:HALO_F, :] = jnp.where(ti > 0, ah_ref[...].astype(F32), 0.0)
        dfs[...] = _dot_nt(dh_ref[...].astype(BF16), wd_ref[...])

        def cp(r0):
            aext[pl.ds(HALO_F + r0, 16), :] = a_ref[pl.ds(r0, 16), :].astype(F32)
        _row_loop(TE, 16, cp)

        def act(r0):
            rows = pl.ds(r0, 16)
            win = aext[pl.ds(r0, 32), :]
            ac = fw_ref[3] + fw_ref[0] * win[14:30, :] + fw_ref[1] * win[15:31, :] + fw_ref[2] * win[16:32, :]
            sg = _sigmoid(ac)
            df = dfs[rows, :]
            dup_ref[rows, DFF:2 * DFF] = (df * ac * sg).astype(BF16)
            dac = df * bv_ref[rows, :].astype(F32) * sg * (1.0 + ac * (1.0 - sg))
            dax[rows, :] = dac
            dw_ref[3] += _rows8(dac)
            for j in range(FFN_K):
                dw_ref[j] += _rows8(dac * win[14 + j:30 + j, :])
        _row_loop(TE, 16, act)

        def convt(r0):
            win = dax[pl.ds(r0, 32), :]
            da = fw_ref[2] * win[0:16, :] + fw_ref[1] * win[1:17, :] + fw_ref[0] * win[2:18, :]
            dup_ref[pl.ds(r0, 16), 0:DFF] = da.astype(BF16)
        _row_loop(TE, 16, convt)

    rev = lambda w: pl.BlockSpec((TE, w), lambda i: (nt - 1 - i, 0))
    return pl.pallas_call(
        body, name="ffn_bwd", grid=(nt,),
        in_specs=[rev(D), _resident((DFF, D)), rev(DFF),
                  pl.BlockSpec((HALO_F, DFF), lambda i: (jnp.maximum((nt - 1 - i) * hb - 1, 0), 0)),
                  pl.BlockSpec((TE, DFF), lambda i: (nt - 1 - i, 1)), _resident((4, 16, DFF))],
        out_specs=[rev(2 * DFF), _resident((4, 8, DFF))],
        out_shape=[SDS((t, 2 * DFF), BF16), SDS((4, 8, DFF), F32)],
        scratch_shapes=[pltpu.VMEM((TE + HALO_F, DFF), F32), pltpu.VMEM((TE + HALO_F, DFF), F32),
                        pltpu.VMEM((TE, DFF), F32)],
        compiler_params=_ARB1)(dh2, wd, up, up, up, fw)


def _dgrad_norm(dy, w_t, h, g, dres, name, carry=()):
    t, k = dy.shape
    nt = t // TM

    def body(dy_ref, w_ref, h_ref, g_ref, dr_ref, dh_ref, dg_ref, acc):
        @pl.when(pl.program_id(0) == 0)
        def _():
            dg_ref[...] = jnp.zeros_like(dg_ref)

        acc[...] = _dot(dy_ref[...], w_ref[...])

        def blk(r0):
            rows = pl.ds(r0, 32)
            x = h_ref[rows, :]
            rinv = lax.rsqrt(jnp.mean(x * x, axis=-1, keepdims=True) + RMS_EPS)
            hh = x * rinv
            du = acc[rows, :]
            dg_ref[...] += _rows8(du * hh)
            dhh = du * g_ref[...]
            dh_ref[rows, :] = dr_ref[rows, :] + rinv * (dhh - hh * jnp.mean(dhh * hh, axis=-1, keepdims=True))
        _row_loop(TM, 32, blk, unroll=4)

    row = pl.BlockSpec((TM, D), lambda i: (i, 0))
    in_specs = [pl.BlockSpec((TM, k), lambda i: (i, 0)), _once((k, D)), row, _resident((1, D)), row]
    return _carried_call(
        body, carry, name=name, grid=(nt,), in_specs=in_specs, out_specs=[row, _resident((8, D))],
        out_shape=[SDS((t, D), F32), SDS((8, D), F32)],
        scratch_shapes=[pltpu.VMEM((TM, D), F32)], compiler_params=_ARB1)(dy, w_t, h, g, dres)


def _in_dgrad(dy, w_t, h, g, dres, dy_extra, w_extra_t, carry=()):
    t, k = dy.shape
    nt = t // TM

    def body(dy_ref, w_ref, h_ref, g_ref, dr_ref, de_ref, we_ref, dh_ref, dg_ref, acc):
        @pl.when(pl.program_id(0) == 0)
        def _():
            dg_ref[...] = jnp.zeros_like(dg_ref)

        acc[...] = _dot(dy_ref[...], w_ref[...])
        acc[...] += _dot(de_ref[...], we_ref[...])

        def blk(r0):
            rows = pl.ds(r0, 32)
            x = h_ref[rows, :]
            rinv = lax.rsqrt(jnp.mean(x * x, axis=-1, keepdims=True) + RMS_EPS)
            hh = x * rinv
            du = acc[rows, :]
            dg_ref[...] += _rows8(du * hh)
            dhh = du * g_ref[...]
            dh_ref[rows, :] = dr_ref[rows, :] + rinv * (dhh - hh * jnp.mean(dhh * hh, axis=-1, keepdims=True))
        _row_loop(TM, 32, blk, unroll=4)

    tile = lambda w: pl.BlockSpec((TM, w), lambda i: (i, 0))
    return _carried_call(
        body, carry, name="in_dgrad", grid=(nt,),
        in_specs=[tile(k), _once((k, D)), tile(D), _resident((1, D)), tile(D), tile(dy_extra.shape[1]),
                  _once(w_extra_t.shape)],
        out_specs=[tile(D), _resident((8, D))], out_shape=[SDS((t, D), F32), SDS((8, D), F32)],
        scratch_shapes=[pltpu.VMEM((TM, D), F32)], compiler_params=_ARB1)(dy, w_t, h, g, dres, dy_extra, w_extra_t)


def _wgrad(x, dy, tk, name):
    t, k = x.shape
    n = dy.shape[1]
    tm = _row_tile(t, TM_BIG)
    nk, nt = k // tk, t // tm

    def body(x_ref, dy_ref, o_ref, acc):
        @pl.when(pl.program_id(1) == 0)
        def _():
            acc[...] = jnp.zeros_like(acc)
        acc[...] += _dot_tn(x_ref[...], dy_ref[...].astype(BF16))

        @pl.when(pl.program_id(1) == nt - 1)
        def _():
            o_ref[...] = acc[...].astype(BF16)

    return pl.pallas_call(
        body, name=name, grid=(nk, nt),
        in_specs=[pl.BlockSpec((tm, tk), lambda j, i: (i, j)), pl.BlockSpec((tm, n), lambda j, i: (i, 0))],
        out_specs=pl.BlockSpec((tk, n), lambda j, i: (j, 0)), out_shape=SDS((k, n), BF16),
        scratch_shapes=[pltpu.VMEM((tk, n), F32)], compiler_params=_ARB2)(x, dy)


def _mix_bwd(dh1, wo, wg, wc, proj, brg, brc):
    t = dh1.shape[0]
    nt = t // TM

    def body(dh_ref, wo_ref, wg_ref, wc_ref, g_ref, brg_ref, brc_ref, dbg_ref, dbc_ref, dog_ref, dcs_ref, dp_ref, dm):
        dm[...] = _dot_nt(dh_ref[...].astype(BF16), wo_ref[...])

        def blk(r0):
            rows = pl.ds(r0, 32)
            d = dm[rows, :]
            sg = _sigmoid(g_ref[rows, 0:D].astype(F32))
            sc = _sigmoid(g_ref[rows, D:2 * D].astype(F32))
            dbg_ref[rows, :] = (d * sg).astype(BF16)
            dbc_ref[rows, :] = (d * sc).astype(BF16)
            dp_ref[rows, 0:D] = (d * brg_ref[rows, :].astype(F32) * sg * (1.0 - sg)).astype(BF16)
            dp_ref[rows, D:2 * D] = (d * brc_ref[rows, :].astype(F32) * sc * (1.0 - sc)).astype(BF16)
        _row_loop(TM, 32, blk)
        dog_ref[...] = _dot_nt(dbg_ref[...], wg_ref[...]).astype(BF16)
        dcs_ref[...] = _dot_nt(dbc_ref[...], wc_ref[...]).astype(BF16)

    row = pl.BlockSpec((TM, D), lambda i: (i, 0))
    wide = pl.BlockSpec((TM, 2 * D), lambda i: (i, 2))
    return pl.pallas_call(
        body, name="mix_bwd", grid=(nt,),
        in_specs=[row, _once((D, D)), _once((D, D)), _once((D, D)), wide, row, row],
        out_specs=[row, row, row, row, wide],
        out_shape=[SDS((t, D), BF16)] * 4 + [SDS((t, NPROJ), BF16)],
        scratch_shapes=[pltpu.VMEM((TM, D), F32)], compiler_params=_ARB1)(dh1, wo, wg, wc, proj, brg, brc)


def _glapost_bwd(dog, o, proj, gn, dproj):
    t = o.shape[0]
    nt = t // TE

    def body(dog_ref, o_ref, r_ref, gn_ref, dp_in, do_ref, dp_ref, dgn_ref):
        del dp_in

        @pl.when(pl.program_id(0) == 0)
        def _():
            dgn_ref[...] = jnp.zeros_like(dgn_ref)

        def blk(r0):
            rows = pl.ds(r0, 32)
            for h in range(HEADS):
                vs = slice(h * DVH, (h + 1) * DVH)
                x = o_ref[rows, vs]
                rinv = lax.rsqrt(jnp.mean(x * x, axis=-1, keepdims=True) + RMS_EPS)
                oh = x * rinv
                g = gn_ref[:, vs]
                rr = r_ref[rows, vs].astype(F32)
                sr = _sigmoid(rr)
                d = dog_ref[rows, vs].astype(F32)
                dp_ref[rows, vs] = (d * oh * g * sr * (1.0 + rr * (1.0 - sr))).astype(BF16)
                don = d * rr * sr
                dgn_ref[:, vs] += _rows8(don * oh)
                doh = don * g
                do_ref[rows, vs] = (rinv * (doh - oh * jnp.mean(doh * oh, axis=-1, keepdims=True))).astype(BF16)
        _row_loop(TE, 32, blk, unroll=5)

    row = pl.BlockSpec((TE, D), lambda i: (i, 0))
    rcol = pl.BlockSpec((TE, D), lambda i: (i, 6))
    return pl.pallas_call(
        body, name="glapost_bwd", grid=(nt,),
        in_specs=[row, row, rcol, _resident((1, D)), pl.BlockSpec(memory_space=pl.ANY)],
        out_specs=[row, rcol, _resident((8, D))],
        out_shape=[SDS((t, D), BF16), SDS((t, NPROJ), BF16), SDS((8, D), F32)],
        input_output_aliases={4: 1}, compiler_params=_ARB1)(dog, o, proj, gn, dproj)


def _conf_bwd(dcs, cc, proj, cw, lg, lb, dproj, carry=()):
    t = cc.shape[0]
    nt = t // TE
    hb = TE // HALO

    def body(dcs_ref, cc_ref, c1_ref, c2_ref, c1h_ref, c2h_ref, cw_ref, lg_ref, lb_ref, dp_in,
             dp_ref, dw_ref, ds_ref, cext, dext):
        del dp_in
        i = pl.program_id(0)
        ti = nt - 1 - i

        @pl.when(i == 0)
        def _():
            dext[TE:TE + HALO, :] = jnp.zeros((HALO, D), F32)
            dw_ref[...] = jnp.zeros_like(dw_ref)
            ds_ref[...] = jnp.zeros_like(ds_ref)

        @pl.when(i > 0)
        def _():
            dext[TE:TE + HALO, :] = dext[0:HALO, :]

        ch = c1h_ref[...].astype(F32) * _sigmoid(c2h_ref[...].astype(F32))
        cext[0:HALO, :] = jnp.where(ti > 0, ch, 0.0)

        def pre(r0):
            rows = pl.ds(r0, 32)
            cext[pl.ds(HALO + r0, 32), :] = c1_ref[rows, :].astype(F32) * _sigmoid(c2_ref[rows, :].astype(F32))
            x = cc_ref[rows, :]
            mu = jnp.mean(x, axis=-1, keepdims=True)
            xc = x - mu
            rstd = lax.rsqrt(jnp.mean(xc * xc, axis=-1, keepdims=True) + LN_EPS)
            xh = xc * rstd
            ln = xh * lg_ref[...] + lb_ref[...]
            sg = _sigmoid(ln)
            dln = dcs_ref[rows, :].astype(F32) * sg * (1.0 + ln * (1.0 - sg))
            ds_ref[0] += _rows8(dln * xh)
            ds_ref[1] += _rows8(dln)
            dxh = dln * lg_ref[...]
            dcc = rstd * (dxh - jnp.mean(dxh, axis=-1, keepdims=True) - xh * jnp.mean(dxh * xh, axis=-1, keepdims=True))
            dext[rows, :] = dcc
            ds_ref[2] += _rows8(dcc)
        _row_loop(TE, 32, pre, unroll=5)

        def convt(r0):
            rows = pl.ds(r0, CONV_ROWS)
            for part in range(D // CONV_LANES):
                cols = slice(part * CONV_LANES, (part + 1) * CONV_LANES)
                wd = dext[pl.ds(r0, CONV_ROWS + HALO), cols]
                dc = jnp.zeros((CONV_ROWS, CONV_LANES), F32)
                for s, taps in _shift_classes([(CONF_K - 1 - j, j) for j in range(CONF_K)]):
                    ws = _shifted(wd, s)
                    for a8, j in taps:
                        dc = dc + jnp.tile(cw_ref[j, :, cols], (CONV_ROWS // 8, 1)) * ws[a8:a8 + CONV_ROWS, :]
                dcc = wd[0:CONV_ROWS, :]
                wc = cext[pl.ds(r0, CONV_ROWS + HALO), cols]
                for s, taps in _shift_classes([(2 + j, j) for j in range(CONF_K)]):
                    ws = _shifted(wc, s)
                    for a8, j in taps:
                        dw_ref[j, :, cols] += _rows8(dcc * ws[a8:a8 + CONV_ROWS, :])
                c1 = c1_ref[rows, cols].astype(F32)
                s2 = _sigmoid(c2_ref[rows, cols].astype(F32))
                dp_ref[rows, cols] = (dc * s2).astype(BF16)
                dp_ref[rows, D + part * CONV_LANES:D + (part + 1) * CONV_LANES] = (dc * c1 * s2 * (1.0 - s2)).astype(BF16)
        _row_loop(TE, CONV_ROWS, convt)

    rev = lambda col: pl.BlockSpec((TE, D), lambda i: (nt - 1 - i, col))
    halo = lambda col: pl.BlockSpec((HALO, D), lambda i: (jnp.maximum((nt - 1 - i) * hb - 1, 0), col))
    return _carried_call(
        body, carry, name="conf_bwd", grid=(nt,),
        in_specs=[rev(0), rev(0), rev(2), rev(3), halo(2), halo(3), _resident((32, 8, D)), _resident((1, D)),
                  _resident((1, D)), pl.BlockSpec(memory_space=pl.ANY)],
        out_specs=[pl.BlockSpec((TE, 2 * D), lambda i: (nt - 1 - i, 1)), _resident((32, 8, D)), _resident((3, 8, D))],
        out_shape=[SDS((t, NPROJ), BF16), SDS((32, 8, D), F32), SDS((3, 8, D), F32)],
        scratch_shapes=[pltpu.VMEM((TE + HALO, D), F32), pltpu.VMEM((TE + HALO, D), F32)],
        input_output_aliases={9: 0}, compiler_params=_ARB1)(dcs, cc, proj, proj, proj, proj, cw, lg, lb, dproj)


def _gla_bwd(proj, alr, wau, balpha, do, sall, dproj, carry=()):
    t = proj.shape[0]
    nc = t // CH

    def body(qk_ref, v_ref, a_ref, wau_ref, ba_ref, do_ref, s_ref, dp_in, dp_ref, da_ref, dwau_ref, dba_ref, ds_scr, dla_scr):
        del dp_in

        @pl.when(pl.program_id(0) == 0)
        def _():
            ds_scr[...] = jnp.zeros_like(ds_scr)
            dwau_ref[...] = jnp.zeros_like(dwau_ref)
            dba_ref[...] = jnp.zeros_like(dba_ref)

        z, b, bmid, blast, causal = _gla_decay(a_ref, wau_ref, ba_ref)
        dlasts = []
        for h in range(HEADS):
            ks = slice(h * DKH, (h + 1) * DKH)
            vs = slice(h * DVH, (h + 1) * DVH)
            bh, mh, lh = b[:, ks], bmid[:, ks], blast[:, ks]
            q = qk_ref[:, ks].astype(F32) * (DKH ** -0.5)
            k = qk_ref[:, DK + h * DKH:DK + (h + 1) * DKH].astype(F32)
            v = v_ref[:, vs]
            dout = do_ref[:, vs]
            eq, ek, eb, eg, el = jnp.exp(bh - mh), jnp.exp(mh - bh), jnp.exp(bh), jnp.exp(lh - bh), jnp.exp(lh)
            qt, kt = (q * eq).astype(BF16), (k * ek).astype(BF16)
            qg, kg = (q * eb).astype(BF16), (k * eg).astype(BF16)
            st = s_ref[0, vs, :]
            dsn = ds_scr[vs, :]
            st16, dsn16 = st.astype(BF16), dsn.astype(BF16)
            a = jnp.where(causal, _dot_nt(qt, kt), 0.0).astype(BF16)
            da = jnp.where(causal, _dot_nt(dout, v), 0.0).astype(BF16)
            dq_inter = _dot(dout, st16) * eb
            dk_inter = _dot(v, dsn16) * eg
            dq = _dot(da, kt) * eq + dq_inter
            dk = _dot_tn(da, qt) * ek + dk_inter
            dv = _dot_tn(a, dout) + _dot_nt(kg, dsn16)
            dlasts.append(jnp.sum(k * dk_inter, axis=0, keepdims=True) + jnp.sum(st * dsn, axis=0, keepdims=True) * el[0:1, :])
            dla_scr[:, ks] = q * dq - k * dk
            ds_scr[vs, :] = dsn * jnp.concatenate([el, el], axis=0) + _dot_tn(dout, qg)
            dp_ref[:, ks] = (dq * (DKH ** -0.5)).astype(BF16)
            dp_ref[:, DK + h * DKH:DK + (h + 1) * DKH] = dk.astype(BF16)
            dp_ref[:, D + h * DVH:D + (h + 1) * DVH] = dv.astype(BF16)
        r = lax.broadcasted_iota(jnp.int32, (CH, CH), 0)
        c = lax.broadcasted_iota(jnp.int32, (CH, CH), 1)
        dla = _tri_matmul((r <= c).astype(BF16), dla_scr[...]) + jnp.concatenate(dlasts, axis=1)
        dz = (dla * (1.0 / TAU) * _sigmoid(-z)).astype(BF16)
        da_ref[...] = _dot_nt(dz, wau_ref[...]).astype(BF16)
        dwau_ref[...] += _dot_tn(a_ref[...], dz)
        dba_ref[...] += _rows8(dz.astype(F32))

    rev = lambda w, col: pl.BlockSpec((CH, w), lambda c: (nc - 1 - c, col))
    return _carried_call(
        body, carry, name="gla_bwd", grid=(nc,),
        in_specs=[rev(D, 0), rev(D, 1), rev(LANES, 0), _resident((LANES, DK)), _resident((1, DK)), rev(D, 0),
                  pl.BlockSpec((1, DV, DKH), lambda c: (nc - 1 - c, 0, 0)), pl.BlockSpec(memory_space=pl.ANY)],
        out_specs=[rev(2 * D, 0), rev(LANES, 0), _resident((LANES, DK)), _resident((8, DK))],
        out_shape=[SDS((t, NPROJ), BF16), SDS((t, LANES), BF16), SDS((LANES, DK), F32), SDS((8, DK), F32)],
        scratch_shapes=[pltpu.VMEM((DV, DKH), F32), pltpu.VMEM((CH, DK), F32)],
        input_output_aliases={7: 0}, compiler_params=_ARB1)(proj, proj, alr, wau, balpha, do, sall, dproj)


def _all_gather(xs, name):
    n = len(xs)

    def body(*refs):
        x_refs, out_refs = refs[:n], refs[n:2 * n]
        send_sems, recv_sems = refs[2 * n:]
        x, y, c = _my_place()
        me, sibling = (x, y, c), (x, y, 1 - c)
        chips = [(1 - x, y), (x, 1 - y), (1 - x, 1 - y)]

        def slot(p, px, py, pc):
            return out_refs[p].at[4 * px + 2 * py + pc]

        def copy(p, k, block, to, src=None):
            return pltpu.make_async_remote_copy(
                src_ref=slot(p, *block) if src is None else src, dst_ref=slot(p, *block),
                send_sem=send_sems.at[7 * p + k], recv_sem=recv_sems.at[7 * p + k], device_id=to, device_id_type=MESH_T)

        first = []
        for p in range(n):
            first.append(copy(p, 0, me, sibling, src=x_refs[p]))
            first += [copy(p, 1 + j, me, (*chip, c), src=x_refs[p]) for j, chip in enumerate(chips)]
        for cp in first:
            cp.start()
        passed = []
        for p in range(n):
            for j, chip in enumerate(chips):
                copy(p, 1 + j, (*chip, c), me).wait_recv()
                fwd = copy(p, 4 + j, (*chip, c), sibling)
                fwd.start()
                passed.append(fwd)
        for p in range(n):
            copy(p, 0, sibling, me).wait_recv()
            for j, chip in enumerate(chips):
                copy(p, 4 + j, (*chip, 1 - c), me).wait_recv()
        for cp in first + passed:
            cp.wait_send()

    hbm = pl.BlockSpec(memory_space=pl.ANY)
    return pl.pallas_call(
        body, name=name, out_shape=[SDS((N_DEV, *a.shape), a.dtype) for a in xs],
        in_specs=[hbm] * n, out_specs=[hbm] * n,
        scratch_shapes=[pltpu.SemaphoreType.DMA((7 * n,)), pltpu.SemaphoreType.DMA((7 * n,))])(*xs)


def _exchange(gs):
    n = len(gs)

    def body(*refs):
        g_refs, land_refs = refs[:n], refs[n:2 * n]
        send_sems, recv_sems, local_sems = refs[2 * n:]
        x, y, c = _my_place()
        my_idx = 4 * x + 2 * y + c
        mine = [pltpu.make_async_copy(g_refs[p].at[my_idx], land_refs[p].at[my_idx], local_sems.at[p]) for p in range(n)]
        for cp in mine:
            cp.start()
        copies = []
        for k in range(1, N_DEV):
            px, py, pc = _flip(x, k & 4), _flip(y, k & 2), _flip(c, k & 1)
            p_idx = 4 * px + 2 * py + pc
            for p in range(n):
                s = 7 * p + k - 1
                cp = pltpu.make_async_remote_copy(
                    src_ref=g_refs[p].at[p_idx], dst_ref=land_refs[p].at[my_idx], send_sem=send_sems.at[s],
                    recv_sem=recv_sems.at[s], device_id=(px, py, pc), device_id_type=MESH_T)
                cp.start()
                arrival = pltpu.make_async_remote_copy(
                    src_ref=g_refs[p].at[p_idx], dst_ref=land_refs[p].at[p_idx], send_sem=send_sems.at[s],
                    recv_sem=recv_sems.at[s], device_id=(px, py, pc), device_id_type=MESH_T)
                copies.append((cp, arrival))
        for cp, arrival in copies:
            arrival.wait_recv()
        for cp, arrival in copies:
            cp.wait_send()
        for cp in mine:
            cp.wait()

    hbm = pl.BlockSpec(memory_space=pl.ANY)
    return pl.pallas_call(
        body, name="grad_exchange", out_shape=[SDS(g.shape, g.dtype) for g in gs],
        in_specs=[hbm] * n, out_specs=[hbm] * n,
        scratch_shapes=[pltpu.SemaphoreType.DMA((7 * n,)), pltpu.SemaphoreType.DMA((7 * n,)),
                        pltpu.SemaphoreType.DMA((n,))])(*gs)


def _adamw(land, w, m, v, rows_blk, name):
    rows = w.shape[0]

    def body(l_ref, w_ref, m_ref, v_ref, g_ref, d_ref, nm_ref, nv_ref):
        g = l_ref[0].astype(F32)
        for s in range(1, N_DEV):
            g = g + l_ref[s].astype(F32)
        nm = ADAM_B1 * m_ref[...] + (1.0 - ADAM_B1) * g
        nv = ADAM_B2 * v_ref[...] + (1.0 - ADAM_B2) * (g * g)
        m_hat = nm / (1.0 - ADAM_B1 ** ADAM_STEP)
        v_hat = nv / (1.0 - ADAM_B2 ** ADAM_STEP)
        g_ref[...] = g
        d_ref[...] = -ADAM_LR * (m_hat / (jnp.sqrt(v_hat) + ADAM_EPS) + ADAM_WD * w_ref[...])
        nm_ref[...] = nm
        nv_ref[...] = nv

    blk = pl.BlockSpec((rows_blk, D), lambda i: (i, 0))
    return pl.pallas_call(
        body, name=name, grid=(rows // rows_blk,),
        in_specs=[pl.BlockSpec((N_DEV, rows_blk, D), lambda i: (0, i, 0)), blk, blk, blk],
        out_specs=[blk] * 4, out_shape=[SDS((rows, D), F32)] * 4, compiler_params=_ARB1)(land, w, m, v)


BIG = ("w_in", "w_up", "w_down", "w_gla_o", "w_conf_o", "w_out")
BIG_TRANSPOSED = ("w_in", "w_up")
SMALL_SHARDED = ("meta_tokens", "conf_dw_w", "ffn_dw_w", "w_alpha_up")
REPLICATED = ("norm_mix_g", "b_alpha", "gla_norm_g", "conf_dw_b", "conf_ln_g", "conf_ln_b", "norm_ffn_g", "ffn_dw_b",
              "final_norm_g")
N_IN = sum(IN_WIDTHS)
W_IN_ROWS = N_IN // N_DEV
W_IN_PAD = -(-W_IN_ROWS // 16) * 16
ADAM_BLOCK = {"w_in": W_IN_PAD // 3, "w_up": 176, "w_down": 176, "w_gla_o": 128, "w_conf_o": 128, "w_out": 128}
SMALL_ROWS = 32


def _to_panel(name, shard):
    a = shard.reshape(shard.shape[-2], shard.shape[-1])
    if name in BIG_TRANSPOSED:
        a = a.T
    if name == "w_in":
        a = jnp.pad(a, ((0, W_IN_PAD - W_IN_ROWS), (0, 0)))
    return a


def _from_panel(name, panel, shape):
    a = panel[0:W_IN_ROWS] if name == "w_in" else panel
    if name in BIG_TRANSPOSED:
        a = a.T
    return a.reshape(shape)


def _pack_small(arrs):
    flat = jnp.concatenate([jnp.pad(a.reshape(-1), (0, (-a.size) % D)) for a in arrs])
    return jnp.pad(flat, (0, SMALL_ROWS * D - flat.shape[0])).reshape(SMALL_ROWS, D)


def _unpack_small(panel, shapes):
    flat, out, off = panel.reshape(-1), [], 0
    for shp in shapes:
        n = 1
        for s in shp:
            n *= s
        out.append(flat[off:off + n].reshape(shp))
        off += n + (-n) % D
    return out


def _local_step(x, target, w, shards=None):
    dist = shards is not None
    w = dict(w)

    def gather(names):
        return [(shards[n], False) for n in names] if dist else []

    def scatter(*arrs):
        return [(a.reshape(N_DEV, -1, D), True) for a in arrs] if dist else []

    s = x.shape[0]
    n_real = s + N_META
    t = -(-n_real // TM) * TM

    q0, r0, a0, c0 = 0, 2 * DK + DV, 2 * DK + 2 * DV, 2 * DK + 2 * DV + RANK
    wt = w["w_in_t"]
    w_main = jnp.concatenate([wt[q0:r0], wt[c0:N_IN], wt[r0:a0]], axis=0)
    w_a = jnp.pad(wt[a0:c0], ((0, LANES - RANK), (0, 0)))
    wau = jnp.pad(w["w_alpha_up"].astype(BF16), ((0, LANES - RANK), (0, 0)))
    row = lambda name: w[name].reshape(1, -1)
    cw = jnp.broadcast_to(jnp.pad(w["conf_dw_w"], ((0, 32 - CONF_K), (0, 0)))[:, None, :], (32, 8, D))
    fw = jnp.broadcast_to(jnp.concatenate([w["ffn_dw_w"], w["ffn_dw_b"].reshape(1, -1)], axis=0)[:, None, :],
                          (FFN_K + 1, 16, DFF))

    early = ("w_gla_o", "w_conf_o", "w_out", "w_up")
    h0, u1, proj, alr, *landed = _in_proj(x, w["meta_tokens"], row("norm_mix_g"), w_main, w_a, t, carry=gather(early))
    for n, land in zip(early, landed):
        w["w_up_t" if n == "w_up" else n] = land.reshape(-1, D)
    o, og, sall = _gla_fwd(proj, alr, wau, row("b_alpha"), row("gla_norm_g"))
    cc, cs, *landed = _conf_fwd(proj, cw, row("conf_dw_b"), row("conf_ln_g"), row("conf_ln_b"), carry=gather(("w_down",)))
    if dist:
        w["w_down"] = landed[0].reshape(-1, D)
    brg, brc, merged, h1 = _mix_fwd(og, cs, proj, h0, w["w_gla_o"], w["w_conf_o"], w["w_out"])
    u2, up = _norm_matmul(h1, row("norm_ffn_g"), w["w_up_t"], 512, "up_proj")
    f, dh2, red = _ffn_out(up, fw, h1, w["w_down"], row("final_norm_g"), target)
    loss = 0.5 / D * jnp.sum(red[0:8])

    g = {"final_norm_g": jnp.sum(red[8:16], axis=0)}
    dup, dfw = _ffn_bwd(dh2, w["w_down"], up, fw)
    g["ffn_dw_w"] = jnp.sum(dfw[0:FFN_K], axis=1)
    g["ffn_dw_b"] = jnp.sum(dfw[3], axis=0)
    g["w_down"] = _wgrad(f, dh2, 1408, "wgrad_down")
    dh1, dg2, *landed = _dgrad_norm(dup, w["w_up_t"], h1, row("norm_ffn_g"), dh2, "up_dgrad", carry=scatter(g["w_down"]))
    if dist:
        g["w_down"] = landed[0]
    g["norm_ffn_g"] = jnp.sum(dg2, axis=0)
    g["w_up_t"] = _wgrad(dup, u2, 1408, "wgrad_up")
    dbrg, dbrc, dog, dcs, dproj = _mix_bwd(dh1, w["w_out"], w["w_gla_o"], w["w_conf_o"], proj, brg, brc)
    g["w_out"] = _wgrad(merged, dh1, 1024, "wgrad_out")
    g["w_gla_o"] = _wgrad(og, dbrg, 1024, "wgrad_gla_o")
    g["w_conf_o"] = _wgrad(cs, dbrc, 1024, "wgrad_conf_o")
    do, dproj, dgn = _glapost_bwd(dog, o, proj, row("gla_norm_g"), dproj)
    g["gla_norm_g"] = jnp.sum(dgn, axis=0)
    dproj, dcw, dst, *landed = _conf_bwd(dcs, cc, proj, cw, row("conf_ln_g"), row("conf_ln_b"), dproj,
                                         carry=scatter(g["w_up_t"]))
    if dist:
        g["w_up_t"] = landed[0]
    g["conf_dw_w"] = jnp.sum(dcw[0:CONF_K], axis=1)
    g["conf_ln_g"], g["conf_ln_b"], g["conf_dw_b"] = jnp.sum(dst[0], axis=0), jnp.sum(dst[1], axis=0), jnp.sum(dst[2], axis=0)
    dproj, dalr, dwau, dba, *landed = _gla_bwd(proj, alr, wau, row("b_alpha"), do, sall, dproj,
                                               carry=scatter(g["w_out"], g["w_gla_o"], g["w_conf_o"]))
    if dist:
        g["w_out"], g["w_gla_o"], g["w_conf_o"] = landed
    g["w_alpha_up"] = dwau[0:RANK]
    g["b_alpha"] = jnp.sum(dba, axis=0)
    dw_main = _wgrad(dproj, u1, 1024, "wgrad_in")
    dw_a = _wgrad(dalr, u1, LANES, "wgrad_alr")
    g["w_in_t"] = jnp.concatenate([dw_main[0:r0], dw_main[NPROJ - DV:NPROJ], dw_a[0:RANK], dw_main[r0:NPROJ - DV]], axis=0)
    w_in_blocks = []
    if dist:
        pad = ((0, 0), (0, W_IN_PAD - W_IN_ROWS), (0, 0))
        w_in_blocks = [(jnp.pad(g["w_in_t"].reshape(N_DEV, W_IN_ROWS, D), pad), True)]
    dh0, dg1, *landed = _in_dgrad(dproj, w_main, h0, row("norm_mix_g"), dh1, dalr, w_a, carry=w_in_blocks)
    if dist:
        g["w_in_t"] = landed[0]
    g["norm_mix_g"] = jnp.sum(dg1, axis=0)
    g["meta_tokens"] = dh0[0:N_META]
    return loss, dh0[N_META:n_real], g


def kernel(x, meta_tokens, norm_mix_g, w_in, w_alpha_up, b_alpha, gla_norm_g, w_gla_o, conf_dw_w, conf_dw_b, conf_ln_g, conf_ln_b, w_conf_o, w_out, norm_ffn_g, w_up, ffn_dw_w, ffn_dw_b, w_down, final_norm_g, loss_target, m_meta_tokens, m_norm_mix_g, m_w_in, m_w_alpha_up, m_b_alpha, m_gla_norm_g, m_w_gla_o, m_conf_dw_w, m_conf_dw_b, m_conf_ln_g, m_conf_ln_b, m_w_conf_o, m_w_out, m_norm_ffn_g, m_w_up, m_ffn_dw_w, m_ffn_dw_b, m_w_down, m_final_norm_g, v_meta_tokens, v_norm_mix_g, v_w_in, v_w_alpha_up, v_b_alpha, v_gla_norm_g, v_w_gla_o, v_conf_dw_w, v_conf_dw_b, v_conf_ln_g, v_conf_ln_b, v_w_conf_o, v_w_out, v_norm_ffn_g, v_w_up, v_ffn_dw_w, v_ffn_dw_b, v_w_down, v_final_norm_g):
    ws = dict(meta_tokens=meta_tokens, norm_mix_g=norm_mix_g, w_in=w_in, w_alpha_up=w_alpha_up, b_alpha=b_alpha,
              gla_norm_g=gla_norm_g, w_gla_o=w_gla_o, conf_dw_w=conf_dw_w, conf_dw_b=conf_dw_b, conf_ln_g=conf_ln_g,
              conf_ln_b=conf_ln_b, w_conf_o=w_conf_o, w_out=w_out, norm_ffn_g=norm_ffn_g, w_up=w_up, ffn_dw_w=ffn_dw_w,
              ffn_dw_b=ffn_dw_b, w_down=w_down, final_norm_g=final_norm_g)
    ms = dict(meta_tokens=m_meta_tokens, norm_mix_g=m_norm_mix_g, w_in=m_w_in, w_alpha_up=m_w_alpha_up, b_alpha=m_b_alpha,
              gla_norm_g=m_gla_norm_g, w_gla_o=m_w_gla_o, conf_dw_w=m_conf_dw_w, conf_dw_b=m_conf_dw_b,
              conf_ln_g=m_conf_ln_g, conf_ln_b=m_conf_ln_b, w_conf_o=m_w_conf_o, w_out=m_w_out, norm_ffn_g=m_norm_ffn_g,
              w_up=m_w_up, ffn_dw_w=m_ffn_dw_w, ffn_dw_b=m_ffn_dw_b, w_down=m_w_down, final_norm_g=m_final_norm_g)
    vs = dict(meta_tokens=v_meta_tokens, norm_mix_g=v_norm_mix_g, w_in=v_w_in, w_alpha_up=v_w_alpha_up, b_alpha=v_b_alpha,
              gla_norm_g=v_gla_norm_g, w_gla_o=v_w_gla_o, conf_dw_w=v_conf_dw_w, conf_dw_b=v_conf_dw_b,
              conf_ln_g=v_conf_ln_g, conf_ln_b=v_conf_ln_b, w_conf_o=v_w_conf_o, w_out=v_w_out, norm_ffn_g=v_norm_ffn_g,
              w_up=v_w_up, ffn_dw_w=v_ffn_dw_w, ffn_dw_b=v_ffn_dw_b, w_down=v_w_down, final_norm_g=v_final_norm_g)
    small = SMALL_SHARDED + REPLICATED
    pack_small = lambda d: _pack_small([d[n] for n in small])

    shards = {n: _to_panel(n, ws[n]).astype(BF16) for n in BIG}
    own = [shards["w_in"], pack_small(ws)]
    my_idx = 4 * lax.axis_index("x") + 2 * lax.axis_index("y") + lax.axis_index("c")
    gathered = [lax.dynamic_update_slice(full_, mine[None], (my_idx, 0, 0))
                for full_, mine in zip(_all_gather(own, "weight_gather"), own)]
    full = {n: ws[n].reshape(-1) for n in REPLICATED}
    full["w_in_t"] = gathered[0][:, 0:W_IN_ROWS].reshape(N_IN, D)
    flat, off = gathered[1].reshape(N_DEV, -1), 0
    for n in SMALL_SHARDED:
        k, c = ws[n].shape[-2], ws[n].shape[-1]
        full[n] = flat[:, off:off + k * c].reshape(N_DEV, k, c).transpose(1, 0, 2).reshape(k, N_DEV * c)
        off += k * c + (-(k * c)) % D

    loss, grad_x, g = _local_step(x[0], loss_target[0], full, shards)

    lands = [g["w_in_t"], g["w_up_t"]] + [g[n] for n in BIG[2:]]
    blocks = []
    for n in SMALL_SHARDED:
        k, c = ws[n].shape[-2], ws[n].shape[-1]
        b = g[n].reshape(k, N_DEV, c).transpose(1, 0, 2).reshape(N_DEV, k * c)
        blocks.append(jnp.pad(b, ((0, 0), (0, (-(k * c)) % D))))
    for n in REPLICATED:
        b = jnp.broadcast_to(g[n].reshape(1, -1), (N_DEV, g[n].size))
        blocks.append(jnp.pad(b, ((0, 0), (0, (-b.shape[1]) % D))))
    gsm = jnp.concatenate(blocks, axis=1)
    lands += _exchange([jnp.pad(gsm, ((0, 0), (0, SMALL_ROWS * D - gsm.shape[1]))).reshape(N_DEV, SMALL_ROWS, D)])

    grad, delta, new_m, new_v = {}, {}, {}, {}
    for i, n in enumerate(BIG):
        outs = _adamw(lands[i], _to_panel(n, ws[n]), _to_panel(n, ms[n]), _to_panel(n, vs[n]), ADAM_BLOCK[n], "adamw_" + n)
        grad[n], delta[n], new_m[n], new_v[n] = [_from_panel(n, p, ws[n].shape) for p in outs]
    outs = _adamw(lands[len(BIG)], pack_small(ws), pack_small(ms), pack_small(vs), SMALL_ROWS, "adamw_small")
    shapes = [ws[n].shape for n in small]
    for d, p in zip((grad, delta, new_m, new_v), outs):
        d.update(zip(small, _unpack_small(p, shapes)))

    order = ("meta_tokens", "norm_mix_g", "w_in", "w_alpha_up", "b_alpha", "gla_norm_g", "w_gla_o", "conf_dw_w", "conf_dw_b",
             "conf_ln_g", "conf_ln_b", "w_conf_o", "w_out", "norm_ffn_g", "w_up", "ffn_dw_w", "ffn_dw_b", "w_down",
             "final_norm_g")
    loss = lax.psum(loss, ("x", "y", "c"))
    return (loss, grad_x[None], *[grad[n] for n in order], *[delta[n] for n in order], *[new_m[n] for n in order],
            *[new_v[n] for n in order])
```

```python
import functools

import jax
import jax.numpy as jnp
from jax import lax
from jax.experimental import pallas as pl
from jax.experimental.pallas import tpu as pltpu

F32, BF16 = jnp.float32, jnp.bfloat16
SDS = jax.ShapeDtypeStruct

D = 1024
N_META = 16
HEADS = 4
DK, DKH, DV, DVH = 512, 128, 1024, 256
RANK = 16
TAU = 16.0
CONF_K = 31
DFF = 2816
FFN_K = 3
IN_WIDTHS = (DK, DK, DV, DV, RANK, 2 * D, D, D)
RMS_EPS, LN_EPS = 1e-6, 1e-5
ADAM_LR, ADAM_B1, ADAM_B2, ADAM_EPS, ADAM_WD, ADAM_STEP = 0.001, 0.9, 0.999, 1e-08, 0.01, 10

NPROJ = 7 * D
LANES = 128
CH = 128
TM = 640
TM_BIG = 1664
TE = 320
HALO = 32
HALO_F = 16
N_DEV = 8
VMEM_LIMIT = 60 * 1024 * 1024
MESH_T = pl.DeviceIdType.MESH

_ARB1 = pltpu.CompilerParams(dimension_semantics=("arbitrary",), vmem_limit_bytes=VMEM_LIMIT)
_ARB2 = pltpu.CompilerParams(dimension_semantics=("arbitrary", "arbitrary"), vmem_limit_bytes=VMEM_LIMIT)


def _dot(a, b):
    return jnp.dot(a, b, preferred_element_type=F32)


def _dot_nt(a, b):
    return lax.dot_general(a, b, (((1,), (1,)), ((), ())), preferred_element_type=F32)


def _dot_tn(a, b):
    return lax.dot_general(a, b, (((0,), (0,)), ((), ())), preferred_element_type=F32)


def _sigmoid(x):
    return 0.5 * jnp.tanh(0.5 * x) + 0.5


def _rows8(x):
    return x.reshape(x.shape[0] // 8, 8, x.shape[1]).sum(axis=0)


def _row_tile(t, preferred):
    return preferred if t % preferred == 0 else TM


def _row_loop(n_rows, rb, fn, unroll=1):
    def step(i, carry):
        fn(pl.multiple_of(i * rb, rb))
        return carry
    lax.fori_loop(0, n_rows // rb, step, 0, unroll=unroll)


def _resident(shape):
    return pl.BlockSpec(shape, lambda *_: (0,) * len(shape))


def _once(shape):
    return pl.BlockSpec(shape, lambda *_: (0,) * len(shape), pipeline_mode=pl.Buffered(1))


CONV_ROWS, CONV_LANES = 64, 256


def _shift_classes(offset_taps):
    return [(s, [(o - s, j) for o, j in offset_taps if o % 8 == s]) for s in range(8)]


def _shifted(win, s):
    return win if s == 0 else win[s:s + CONV_ROWS + HALO - 8, :]


def _my_place():
    return lax.axis_index("x"), lax.axis_index("y"), lax.axis_index("c")


def _flip(v, bit):
    return 1 - v if bit else v


def _exchange_copies(src_refs, land_refs, scatter, send_sems, recv_sems, local_sems, arrivals):
    x, y, c = _my_place()
    my_idx = 4 * x + 2 * y + c
    local, remote = [], []
    for p, (src, land) in enumerate(zip(src_refs, land_refs)):
        local.append(pltpu.make_async_copy(src.at[my_idx] if scatter[p] else src, land.at[my_idx], local_sems.at[p]))
    for k in range(1, N_DEV):
        px, py, pc = _flip(x, k & 4), _flip(y, k & 2), _flip(c, k & 1)
        p_idx = 4 * px + 2 * py + pc
        for p, (src, land) in enumerate(zip(src_refs, land_refs)):
            s = 7 * p + k - 1
            out = src.at[p_idx] if scatter[p] else src

            def copy(dst):
                return pltpu.make_async_remote_copy(src_ref=out, dst_ref=dst, send_sem=send_sems.at[s],
                                                    recv_sem=recv_sems.at[s], device_id=(px, py, pc), device_id_type=MESH_T)
            remote.append((copy(land.at[my_idx]), copy(land.at[p_idx]) if arrivals else None))
    return local, remote


def _carried_call(core, carry, *, grid, in_specs, out_specs, out_shape, scratch_shapes=(), **kw):
    n_in, n_out, nc, n_scr = len(in_specs), len(out_specs), len(carry), len(scratch_shapes)
    if nc == 0:
        return pl.pallas_call(core, grid=grid, in_specs=in_specs, out_specs=out_specs, out_shape=out_shape,
                              scratch_shapes=list(scratch_shapes), **kw)
    scatter = [sc for _, sc in carry]

    def body(*refs):
        ins, cin = refs[:n_in], refs[n_in:n_in + nc]
        outs, cout = refs[n_in + nc:n_in + nc + n_out], refs[n_in + nc + n_out:n_in + 2 * nc + n_out]
        scr, sems = refs[n_in + 2 * nc + n_out:n_in + 2 * nc + n_out + n_scr], refs[-3:]
        first = functools.reduce(jnp.logical_and, [pl.program_id(a) == 0 for a in range(len(grid))])
        last = functools.reduce(jnp.logical_and, [pl.program_id(a) == grid[a] - 1 for a in range(len(grid))])

        @pl.when(first)
        def _():
            local, remote = _exchange_copies(cin, cout, scatter, *sems, arrivals=False)
            for cp in local:
                cp.start()
            for send, _ in remote:
                send.start()

        core(*ins, *outs, *scr)

        @pl.when(last)
        def _():
            local, remote = _exchange_copies(cin, cout, scatter, *sems, arrivals=True)
            for _, arrival in remote:
                arrival.wait_recv()
            for send, _ in remote:
                send.wait_send()
            for cp in local:
                cp.wait()

    hbm = pl.BlockSpec(memory_space=pl.ANY)
    land_shape = [SDS((N_DEV, *(a.shape[1:] if sc else a.shape)), a.dtype) for a, sc in carry]
    sems = [pltpu.SemaphoreType.DMA((7 * nc,)), pltpu.SemaphoreType.DMA((7 * nc,)), pltpu.SemaphoreType.DMA((nc,))]
    call = pl.pallas_call(body, grid=grid, in_specs=list(in_specs) + [hbm] * nc, out_specs=list(out_specs) + [hbm] * nc,
                          out_shape=list(out_shape) + land_shape, scratch_shapes=list(scratch_shapes) + sems, **kw)
    return lambda *args: call(*args, *[a for a, _ in carry])


def _norm_matmul(h, g, w_t, tn, name, w_extra_t=None, carry=()):
    t, n = h.shape[0], w_t.shape[0]
    tm = _row_tile(t, TM_BIG)
    nt, nb = t // tm, n // tn

    def body(*refs):
        if w_extra_t is None:
            h_ref, g_ref, w_ref, u_ref, p_ref = refs
        else:
            h_ref, g_ref, w_ref, we_ref, u_ref, p_ref, e_ref = refs

        @pl.when(pl.program_id(1) == 0)
        def _():
            def blk(r0):
                x = h_ref[pl.ds(r0, 32), :]
                rinv = lax.rsqrt(jnp.mean(x * x, axis=-1, keepdims=True) + RMS_EPS)
                u_ref[pl.ds(r0, 32), :] = (x * rinv * g_ref[...]).astype(BF16)
            _row_loop(tm, 32, blk, unroll=4)
            if w_extra_t is not None:
                e_ref[...] = _dot_nt(u_ref[...], we_ref[...]).astype(BF16)

        p_ref[...] = _dot_nt(u_ref[...], w_ref[...]).astype(BF16)

    in_specs = [pl.BlockSpec((tm, D), lambda i, j: (i, 0)), _resident((1, D)), pl.BlockSpec((tn, D), lambda i, j: (j, 0))]
    out_specs = [pl.BlockSpec((tm, D), lambda i, j: (i, 0)), pl.BlockSpec((tm, tn), lambda i, j: (i, j))]
    out_shape = [SDS((t, D), BF16), SDS((t, n), BF16)]
    args = [h, g, w_t]
    if w_extra_t is not None:
        in_specs.append(_resident(w_extra_t.shape))
        out_specs.append(pl.BlockSpec((tm, w_extra_t.shape[0]), lambda i, j: (i, 0)))
        out_shape.append(SDS((t, w_extra_t.shape[0]), BF16))
        args.append(w_extra_t)
    return _carried_call(body, carry, name=name, grid=(nt, nb), in_specs=in_specs, out_specs=out_specs,
                         out_shape=out_shape, compiler_params=_ARB2)(*args)


def _in_proj(x, meta, g, w_t, w_a_t, t, carry=()):
    n_real = x.shape[0] + N_META
    n, tn = w_t.shape[0], 1024
    tm = _row_tile(t, TM_BIG)
    nt, nb = t // tm, n // tn

    def body(x_ref, xh_ref, m_ref, g_ref, w_ref, wa_ref, h_ref, u_ref, p_ref, e_ref):
        i = pl.program_id(0)

        @pl.when(pl.program_id(1) == 0)
        def _():
            def rows_of(r0, val):
                gid = i * tm + r0 + lax.broadcasted_iota(jnp.int32, (32, 1), 0)
                val = jnp.where(gid < n_real, val, 0.0)
                h_ref[pl.ds(r0, 32), :] = val
                rinv = lax.rsqrt(jnp.mean(val * val, axis=-1, keepdims=True) + RMS_EPS)
                u_ref[pl.ds(r0, 32), :] = (val * rinv * g_ref[...]).astype(BF16)

            before = jnp.where(i == 0, m_ref[...], xh_ref[...])
            rows_of(0, jnp.concatenate([before, x_ref[0:N_META, :]], axis=0))

            def blk(k, c):
                r0 = pl.multiple_of(k * 32, 32)
                rows_of(r0, x_ref[pl.ds(pl.multiple_of(r0 - N_META, N_META), 32), :])
                return c
            lax.fori_loop(1, tm // 32, blk, 0, unroll=3)
            e_ref[...] = _dot_nt(u_ref[...], wa_ref[...]).astype(BF16)

        p_ref[...] = _dot_nt(u_ref[...], w_ref[...]).astype(BF16)

    row = lambda w: pl.BlockSpec((tm, w), lambda i, j: (i, 0))
    return _carried_call(
        body, carry, name="in_proj", grid=(nt, nb),
        in_specs=[row(D), pl.BlockSpec((N_META, D), lambda i, j: (jnp.maximum(i * (tm // N_META) - 1, 0), 0)),
                  _resident((N_META, D)), _resident((1, D)), pl.BlockSpec((tn, D), lambda i, j: (j, 0)),
                  _resident(w_a_t.shape)],
        out_specs=[row(D), row(D), pl.BlockSpec((tm, tn), lambda i, j: (i, j)), row(w_a_t.shape[0])],
        out_shape=[SDS((t, D), F32), SDS((t, D), BF16), SDS((t, n), BF16), SDS((t, w_a_t.shape[0]), BF16)],
        compiler_params=_ARB2)(x, x, meta, g, w_t, w_a_t)


def _gla_decay(a_ref, wau_ref, ba_ref):
    z = _dot(a_ref[...], wau_ref[...]) + ba_ref[...]
    la = (jnp.minimum(z, 0.0) - jnp.log(1.0 + jnp.exp(-jnp.abs(z)))) * (1.0 / TAU)
    r = lax.broadcasted_iota(jnp.int32, (CH, CH), 0)
    c = lax.broadcasted_iota(jnp.int32, (CH, CH), 1)
    b = _tri_matmul((r >= c).astype(BF16), la)
    mid = jnp.broadcast_to(b[CH // 2:CH // 2 + 1, :], b.shape)
    last = jnp.broadcast_to(b[CH - 1:CH, :], b.shape)
    return z, b, mid, last, r >= c


def _tri_matmul(tri, x):
    n = x.shape[1]
    x1 = x.astype(BF16)
    r1 = x - x1.astype(F32)
    x2 = r1.astype(BF16)
    x3 = (r1 - x2.astype(F32)).astype(BF16)
    y = _dot(tri, jnp.concatenate([x1, x2, x3], axis=1))
    return y[:, 0:n] + y[:, n:2 * n] + y[:, 2 * n:3 * n]


def _gla_fwd(proj, alr, wau, balpha, gn):
    t = proj.shape[0]
    nc = t // CH

    def body(qk_ref, v_ref, r_ref, a_ref, wau_ref, ba_ref, gn_ref, o_ref, og_ref, sall_ref, s_scr):
        @pl.when(pl.program_id(0) == 0)
        def _():
            s_scr[...] = jnp.zeros_like(s_scr)

        sall_ref[0] = s_scr[...]
        _, b, bmid, blast, causal = _gla_decay(a_ref, wau_ref, ba_ref)
        for h in range(HEADS):
            ks = slice(h * DKH, (h + 1) * DKH)
            vs = slice(h * DVH, (h + 1) * DVH)
            bh, mh, lh = b[:, ks], bmid[:, ks], blast[:, ks]
            q = qk_ref[:, ks].astype(F32) * (DKH ** -0.5)
            k = qk_ref[:, DK + h * DKH:DK + (h + 1) * DKH].astype(F32)
            v = v_ref[:, vs]
            qt = (q * jnp.exp(bh - mh)).astype(BF16)
            kt = (k * jnp.exp(mh - bh)).astype(BF16)
            qg = (q * jnp.exp(bh)).astype(BF16)
            kg = (k * jnp.exp(lh - bh)).astype(BF16)
            a = jnp.where(causal, _dot_nt(qt, kt), 0.0)
            st = s_scr[vs, :]
            o = _dot(a.astype(BF16), v) + _dot_nt(qg, st.astype(BF16))
            el = jnp.exp(lh)
            s_scr[vs, :] = st * jnp.concatenate([el, el], axis=0) + _dot_tn(v, kg)
            o_ref[:, vs] = o
            on = o * lax.rsqrt(jnp.mean(o * o, axis=-1, keepdims=True) + RMS_EPS) * gn_ref[:, vs]
            rr = r_ref[:, vs].astype(F32)
            og_ref[:, vs] = (on * (rr * _sigmoid(rr))).astype(BF16)

    return pl.pallas_call(
        body, name="gla_fwd", grid=(nc,),
        in_specs=[pl.BlockSpec((CH, D), lambda c: (c, 0)), pl.BlockSpec((CH, D), lambda c: (c, 1)),
                  pl.BlockSpec((CH, D), lambda c: (c, 6)), pl.BlockSpec((CH, LANES), lambda c: (c, 0)),
                  _resident((LANES, DK)), _resident((1, DK)), _resident((1, DV))],
        out_specs=[pl.BlockSpec((CH, DV), lambda c: (c, 0)), pl.BlockSpec((CH, DV), lambda c: (c, 0)),
                   pl.BlockSpec((1, DV, DKH), lambda c: (c, 0, 0))],
        out_shape=[SDS((t, DV), F32), SDS((t, DV), BF16), SDS((nc, DV, DKH), F32)],
        scratch_shapes=[pltpu.VMEM((DV, DKH), F32)], compiler_params=_ARB1)(proj, proj, proj, alr, wau, balpha, gn)


def _conf_fwd(proj, cw, cb, lg, lb, carry=()):
    t = proj.shape[0]
    nt = t // TE

    def body(c1_ref, c2_ref, cw_ref, cb_ref, lg_ref, lb_ref, cc_ref, cs_ref, cext):
        i = pl.program_id(0)

        @pl.when(i == 0)
        def _():
            cext[0:HALO, :] = jnp.zeros((HALO, D), F32)

        @pl.when(i > 0)
        def _():
            cext[0:HALO, :] = cext[TE:TE + HALO, :]

        def glu(r0):
            c2 = c2_ref[pl.ds(r0, 32), :].astype(F32)
            cext[pl.ds(HALO + r0, 32), :] = c1_ref[pl.ds(r0, 32), :].astype(F32) * _sigmoid(c2)
        _row_loop(TE, 32, glu)

        def conv(r0):
            for part in range(D // CONV_LANES):
                cols = slice(part * CONV_LANES, (part + 1) * CONV_LANES)
                win = cext[pl.ds(r0, CONV_ROWS + HALO), cols]
                acc = jnp.zeros((CONV_ROWS, CONV_LANES), F32) + cb_ref[:, cols]
                for s, taps in _shift_classes([(2 + j, j) for j in range(CONF_K)]):
                    ws = _shifted(win, s)
                    for a8, j in taps:
                        acc = acc + jnp.tile(cw_ref[j, :, cols], (CONV_ROWS // 8, 1)) * ws[a8:a8 + CONV_ROWS, :]
                cc_ref[pl.ds(r0, CONV_ROWS), cols] = acc
            for sub in range(CONV_ROWS // 32):
                rows = pl.ds(r0 + 32 * sub, 32)
                x = cc_ref[rows, :]
                xc = x - jnp.mean(x, axis=-1, keepdims=True)
                var = jnp.mean(xc * xc, axis=-1, keepdims=True)
                ln = xc * lax.rsqrt(var + LN_EPS) * lg_ref[...] + lb_ref[...]
                cs_ref[rows, :] = (ln * _sigmoid(ln)).astype(BF16)
        _row_loop(TE, CONV_ROWS, conv)

    return _carried_call(
        body, carry, name="conf_fwd", grid=(nt,),
        in_specs=[pl.BlockSpec((TE, D), lambda i: (i, 2)), pl.BlockSpec((TE, D), lambda i: (i, 3)),
                  _resident((32, 8, D)), _resident((1, D)), _resident((1, D)), _resident((1, D))],
        out_specs=[pl.BlockSpec((TE, D), lambda i: (i, 0)), pl.BlockSpec((TE, D), lambda i: (i, 0))],
        out_shape=[SDS((t, D), F32), SDS((t, D), BF16)],
        scratch_shapes=[pltpu.VMEM((TE + HALO, D), F32)], compiler_params=_ARB1)(proj, proj, cw, cb, lg, lb)


def _mix_fwd(og, cs, proj, h0, wg, wc, wo):
    t = h0.shape[0]
    nt = t // TM

    def body(og_ref, cs_ref, g_ref, h0_ref, wg_ref, wc_ref, wo_ref, brg_ref, brc_ref, mg_ref, h1_ref):
        brg_ref[...] = _dot(og_ref[...], wg_ref[...]).astype(BF16)
        brc_ref[...] = _dot(cs_ref[...], wc_ref[...]).astype(BF16)

        def blk(r0):
            rows = pl.ds(r0, 32)
            gg = g_ref[rows, 0:D].astype(F32)
            gc = g_ref[rows, D:2 * D].astype(F32)
            m = _sigmoid(gg) * brg_ref[rows, :].astype(F32) + _sigmoid(gc) * brc_ref[rows, :].astype(F32)
            mg_ref[rows, :] = m.astype(BF16)
        _row_loop(TM, 32, blk)
        h1_ref[...] = h0_ref[...] + _dot(mg_ref[...], wo_ref[...])

    row = lambda w: pl.BlockSpec((TM, w), lambda i: (i, 0))
    return pl.pallas_call(
        body, name="mix_fwd", grid=(nt,),
        in_specs=[row(D), row(D), pl.BlockSpec((TM, 2 * D), lambda i: (i, 2)), row(D),
                  _once((D, D)), _once((D, D)), _once((D, D))],
        out_specs=[row(D), row(D), row(D), row(D)],
        out_shape=[SDS((t, D), BF16), SDS((t, D), BF16), SDS((t, D), BF16), SDS((t, D), F32)],
        compiler_params=_ARB1)(og, cs, proj, h0, wg, wc, wo)


def _ffn_out(up, fw, h1, wd, gf, target):
    t = h1.shape[0]
    nt = t // TE
    n_real = target.shape[0] + N_META

    def body(a_ref, bv_ref, fw_ref, h1_ref, wd_ref, gf_ref, tg_ref, tb_ref, f_ref, dh2_ref, red_ref, aext, hs):
        i = pl.program_id(0)

        @pl.when(i == 0)
        def _():
            aext[0:HALO_F, :] = jnp.zeros((HALO_F, DFF), F32)
            red_ref[...] = jnp.zeros_like(red_ref)

        @pl.when(i > 0)
        def _():
            aext[0:HALO_F, :] = aext[TE:TE + HALO_F, :]

        def cp(r0):
            aext[pl.ds(HALO_F + r0, 16), :] = a_ref[pl.ds(r0, 16), :].astype(F32)
        _row_loop(TE, 16, cp)

        def conv(r0):
            win = aext[pl.ds(r0, 32), :]
            ac = fw_ref[3] + fw_ref[0] * win[14:30, :] + fw_ref[1] * win[15:31, :] + fw_ref[2] * win[16:32, :]
            f_ref[pl.ds(r0, 16), :] = (ac * _sigmoid(ac) * bv_ref[pl.ds(r0, 16), :].astype(F32)).astype(BF16)
        _row_loop(TE, 16, conv)
        hs[...] = h1_ref[...] + _dot(f_ref[...], wd_ref[...])

        def head(r0, tg):
            rows = pl.ds(r0, 32)
            h2 = hs[rows, :]
            rinv = lax.rsqrt(jnp.mean(h2 * h2, axis=-1, keepdims=True) + RMS_EPS)
            hh = h2 * rinv
            gid = i * TE + r0 + lax.broadcasted_iota(jnp.int32, (32, 1), 0)
            live = jnp.logical_and(gid >= N_META, gid < n_real)
            err = jnp.where(live, hh * gf_ref[...] - tg, 0.0)
            dy = err * (1.0 / D)
            red_ref[0:8, :] += _rows8(err * err)
            red_ref[8:16, :] += _rows8(dy * hh)
            dhh = dy * gf_ref[...]
            dh2_ref[rows, :] = rinv * (dhh - hh * jnp.mean(dhh * hh, axis=-1, keepdims=True))

        head(0, jnp.concatenate([tb_ref[...], tg_ref[0:N_META, :]], axis=0))

        def blk(k, c):
            r0 = pl.multiple_of(k * 32, 32)
            head(r0, tg_ref[pl.ds(pl.multiple_of(r0 - N_META, N_META), 32), :])
            return c
        lax.fori_loop(1, TE // 32, blk, 0, unroll=3)

    row = lambda w: pl.BlockSpec((TE, w), lambda i: (i, 0))
    return pl.pallas_call(
        body, name="ffn_out", grid=(nt,),
        in_specs=[pl.BlockSpec((TE, DFF), lambda i: (i, 0)), pl.BlockSpec((TE, DFF), lambda i: (i, 1)),
                  _resident((4, 16, DFF)), row(D), _resident((DFF, D)), _resident((1, D)), row(D),
                  pl.BlockSpec((N_META, D), lambda i: (jnp.maximum(i * (TE // N_META) - 1, 0), 0))],
        out_specs=[row(DFF), row(D), _resident((16, D))],
        out_shape=[SDS((t, DFF), BF16), SDS((t, D), F32), SDS((16, D), F32)],
        scratch_shapes=[pltpu.VMEM((TE + HALO_F, DFF), F32), pltpu.VMEM((TE, D), F32)],
        compiler_params=_ARB1)(up, up, fw, h1, wd, gf, target, target)


def _ffn_bwd(dh2, wd, up, fw):
    t = dh2.shape[0]
    nt = t // TE
    hb = TE // HALO_F

    def body(dh_ref, wd_ref, a_ref, ah_ref, bv_ref, fw_ref, dup_ref, dw_ref, aext, dax, dfs):
        i = pl.program_id(0)
        ti = nt - 1 - i

        @pl.when(i == 0)
        def _():
            dax[TE:TE + HALO_F, :] = jnp.zeros((HALO_F, DFF), F32)
            dw_ref[...] = jnp.zeros_like(dw_ref)

        @pl.when(i > 0)
        def _():
            dax[TE:TE + HALO_F, :] = dax[0:HALO_F, :]

        aext[0:HALO_F, :] = jnp.where(ti > 0, ah_ref[...].astype(F32), 0.0)
        dfs[...] = _dot_nt(dh_ref[...].astype(BF16), wd_ref[...])

        def cp(r0):
            aext[pl.ds(HALO_F + r0, 16), :] = a_ref[pl.ds(r0, 16), :].astype(F32)
        _row_loop(TE, 16, cp)

        def act(r0):
            rows = pl.ds(r0, 16)
            win = aext[pl.ds(r0, 32), :]
            ac = fw_ref[3] + fw_ref[0] * win[14:30, :] + fw_ref[1] * win[15:31, :] + fw_ref[2] * win[16:32, :]
            sg = _sigmoid(ac)
            df = dfs[rows, :]
            dup_ref[rows, DFF:2 * DFF] = (df * ac * sg).astype(BF16)
            dac = df * bv_ref[rows, :].astype(F32) * sg * (1.0 + ac * (1.0 - sg))
            dax[rows, :] = dac
            dw_ref[3] += _rows8(dac)
            for j in range(FFN_K):
                dw_ref[j] += _rows8(dac * win[14 + j:30 + j, :])
        _row_loop(TE, 16, act)

        def convt(r0):
            win = dax[pl.ds(r0, 32), :]
            da = fw_ref[2] * win[0:16, :] + fw_ref[1] * win[1:17, :] + fw_ref[0] * win[2:18, :]
            dup_ref[pl.ds(r0, 16), 0:DFF] = da.astype(BF16)
        _row_loop(TE, 16, convt)

    rev = lambda w: pl.BlockSpec((TE, w), lambda i: (nt - 1 - i, 0))
    return pl.pallas_call(
        body, name="ffn_bwd", grid=(nt,),
        in_specs=[rev(D), _resident((DFF, D)), rev(DFF),
                  pl.BlockSpec((HALO_F, DFF), lambda i: (jnp.maximum((nt - 1 - i) * hb - 1, 0), 0)),
                  pl.BlockSpec((TE, DFF), lambda i: (nt - 1 - i, 1)), _resident((4, 16, DFF))],
        out_specs=[rev(2 * DFF), _resident((4, 8, DFF))],
        out_shape=[SDS((t, 2 * DFF), BF16), SDS((4, 8, DFF), F32)],
        scratch_shapes=[pltpu.VMEM((TE + HALO_F, DFF), F32), pltpu.VMEM((TE + HALO_F, DFF), F32),
                        pltpu.VMEM((TE, DFF), F32)],
        compiler_params=_ARB1)(dh2, wd, up, up, up, fw)


def _dgrad_norm(dy, w_t, h, g, dres, name, carry=()):
    t, k = dy.shape
    nt = t // TM

    def body(dy_ref, w_ref, h_ref, g_ref, dr_ref, dh_ref, dg_ref, acc):
        @pl.when(pl.program_id(0) == 0)
        def _():
            dg_ref[...] = jnp.zeros_like(dg_ref)

        acc[...] = _dot(dy_ref[...], w_ref[...])

        def blk(r0):
            rows = pl.ds(r0, 32)
            x = h_ref[rows, :]
            rinv = lax.rsqrt(jnp.mean(x * x, axis=-1, keepdims=True) + RMS_EPS)
            hh = x * rinv
            du = acc[rows, :]
            dg_ref[...] += _rows8(du * hh)
            dhh = du * g_ref[...]
            dh_ref[rows, :] = dr_ref[rows, :] + rinv * (dhh - hh * jnp.mean(dhh * hh, axis=-1, keepdims=True))
        _row_loop(TM, 32, blk, unroll=4)

    row = pl.BlockSpec((TM, D), lambda i: (i, 0))
    in_specs = [pl.BlockSpec((TM, k), lambda i: (i, 0)), _once((k, D)), row, _resident((1, D)), row]
    return _carried_call(
        body, carry, name=name, grid=(nt,), in_specs=in_specs, out_specs=[row, _resident((8, D))],
        out_shape=[SDS((t, D), F32), SDS((8, D), F32)],
        scratch_shapes=[pltpu.VMEM((TM, D), F32)], compiler_params=_ARB1)(dy, w_t, h, g, dres)


def _in_dgrad(dy, w_t, h, g, dres, dy_extra, w_extra_t, carry=()):
    t, k = dy.shape
    nt = t // TM

    def body(dy_ref, w_ref, h_ref, g_ref, dr_ref, de_ref, we_ref, dh_ref, dg_ref, acc):
        @pl.when(pl.program_id(0) == 0)
        def _():
            dg_ref[...] = jnp.zeros_like(dg_ref)

        acc[...] = _dot(dy_ref[...], w_ref[...])
        acc[...] += _dot(de_ref[...], we_ref[...])

        def blk(r0):
            rows = pl.ds(r0, 32)
            x = h_ref[rows, :]
            rinv = lax.rsqrt(jnp.mean(x * x, axis=-1, keepdims=True) + RMS_EPS)
            hh = x * rinv
            du = acc[rows, :]
            dg_ref[...] += _rows8(du * hh)
            dhh = du * g_ref[...]
            dh_ref[rows, :] = dr_ref[rows, :] + rinv * (dhh - hh * jnp.mean(dhh * hh, axis=-1, keepdims=True))
        _row_loop(TM, 32, blk, unroll=4)

    tile = lambda w: pl.BlockSpec((TM, w), lambda i: (i, 0))
    return _carried_call(
        body, carry, name="in_dgrad", grid=(nt,),
        in_specs=[tile(k), _once((k, D)), tile(D), _resident((1, D)), tile(D), tile(dy_extra.shape[1]),
                  _once(w_extra_t.shape)],
        out_specs=[tile(D), _resident((8, D))], out_shape=[SDS((t, D), F32), SDS((8, D), F32)],
        scratch_shapes=[pltpu.VMEM((TM, D), F32)], compiler_params=_ARB1)(dy, w_t, h, g, dres, dy_extra, w_extra_t)


def _wgrad(x, dy, tk, name):
    t, k = x.shape
    n = dy.shape[1]
    tm = _row_tile(t, TM_BIG)
    nk, nt = k // tk, t // tm

    def body(x_ref, dy_ref, o_ref, acc):
        @pl.when(pl.program_id(1) == 0)
        def _():
            acc[...] = jnp.zeros_like(acc)
        acc[...] += _dot_tn(x_ref[...], dy_ref[...].astype(BF16))

        @pl.when(pl.program_id(1) == nt - 1)
        def _():
            o_ref[...] = acc[...].astype(BF16)

    return pl.pallas_call(
        body, name=name, grid=(nk, nt),
        in_specs=[pl.BlockSpec((tm, tk), lambda j, i: (i, j)), pl.BlockSpec((tm, n), lambda j, i: (i, 0))],
        out_specs=pl.BlockSpec((tk, n), lambda j, i: (j, 0)), out_shape=SDS((k, n), BF16),
        scratch_shapes=[pltpu.VMEM((tk, n), F32)], compiler_params=_ARB2)(x, dy)


def _mix_bwd(dh1, wo, wg, wc, proj, brg, brc):
    t = dh1.shape[0]
    nt = t // TM

    def body(dh_ref, wo_ref, wg_ref, wc_ref, g_ref, brg_ref, brc_ref, dbg_ref, dbc_ref, dog_ref, dcs_ref, dp_ref, dm):
        dm[...] = _dot_nt(dh_ref[...].astype(BF16), wo_ref[...])

        def blk(r0):
            rows = pl.ds(r0, 32)
            d = dm[rows, :]
            sg = _sigmoid(g_ref[rows, 0:D].astype(F32))
            sc = _sigmoid(g_ref[rows, D:2 * D].astype(F32))
            dbg_ref[rows, :] = (d * sg).astype(BF16)
            dbc_ref[rows, :] = (d * sc).astype(BF16)
            dp_ref[rows, 0:D] = (d * brg_ref[rows, :].astype(F32) * sg * (1.0 - sg)).astype(BF16)
            dp_ref[rows, D:2 * D] = (d * brc_ref[rows, :].astype(F32) * sc * (1.0 - sc)).astype(BF16)
        _row_loop(TM, 32, blk)
        dog_ref[...] = _dot_nt(dbg_ref[...], wg_ref[...]).astype(BF16)
        dcs_ref[...] = _dot_nt(dbc_ref[...], wc_ref[...]).astype(BF16)

    row = pl.BlockSpec((TM, D), lambda i: (i, 0))
    wide = pl.BlockSpec((TM, 2 * D), lambda i: (i, 2))
    return pl.pallas_call(
        body, name="mix_bwd", grid=(nt,),
        in_specs=[row, _once((D, D)), _once((D, D)), _once((D, D)), wide, row, row],
        out_specs=[row, row, row, row, wide],
        out_shape=[SDS((t, D), BF16)] * 4 + [SDS((t, NPROJ), BF16)],
        scratch_shapes=[pltpu.VMEM((TM, D), F32)], compiler_params=_ARB1)(dh1, wo, wg, wc, proj, brg, brc)


def _glapost_bwd(dog, o, proj, gn, dproj):
    t = o.shape[0]
    nt = t // TE

    def body(dog_ref, o_ref, r_ref, gn_ref, dp_in, do_ref, dp_ref, dgn_ref):
        del dp_in

        @pl.when(pl.program_id(0) == 0)
        def _():
            dgn_ref[...] = jnp.zeros_like(dgn_ref)

        def blk(r0):
            rows = pl.ds(r0, 32)
            for h in range(HEADS):
                vs = slice(h * DVH, (h + 1) * DVH)
                x = o_ref[rows, vs]
                rinv = lax.rsqrt(jnp.mean(x * x, axis=-1, keepdims=True) + RMS_EPS)
                oh = x * rinv
                g = gn_ref[:, vs]
                rr = r_ref[rows, vs].astype(F32)
                sr = _sigmoid(rr)
                d = dog_ref[rows, vs].astype(F32)
                dp_ref[rows, vs] = (d * oh * g * sr * (1.0 + rr * (1.0 - sr))).astype(BF16)
                don = d * rr * sr
                dgn_ref[:, vs] += _rows8(don * oh)
                doh = don * g
                do_ref[rows, vs] = (rinv * (doh - oh * jnp.mean(doh * oh, axis=-1, keepdims=True))).astype(BF16)
        _row_loop(TE, 32, blk, unroll=5)

    row = pl.BlockSpec((TE, D), lambda i: (i, 0))
    rcol = pl.BlockSpec((TE, D), lambda i: (i, 6))
    return pl.pallas_call(
        body, name="glapost_bwd", grid=(nt,),
        in_specs=[row, row, rcol, _resident((1, D)), pl.BlockSpec(memory_space=pl.ANY)],
        out_specs=[row, rcol, _resident((8, D))],
        out_shape=[SDS((t, D), BF16), SDS((t, NPROJ), BF16), SDS((8, D), F32)],
        input_output_aliases={4: 1}, compiler_params=_ARB1)(dog, o, proj, gn, dproj)


def _conf_bwd(dcs, cc, proj, cw, lg, lb, dproj, carry=()):
    t = cc.shape[0]
    nt = t // TE
    hb = TE // HALO

    def body(dcs_ref, cc_ref, c1_ref, c2_ref, c1h_ref, c2h_ref, cw_ref, lg_ref, lb_ref, dp_in,
             dp_ref, dw_ref, ds_ref, cext, dext):
        del dp_in
        i = pl.program_id(0)
        ti = nt - 1 - i

        @pl.when(i == 0)
        def _():
            dext[TE:TE + HALO, :] = jnp.zeros((HALO, D), F32)
            dw_ref[...] = jnp.zeros_like(dw_ref)
            ds_ref[...] = jnp.zeros_like(ds_ref)

        @pl.when(i > 0)
        def _():
            dext[TE:TE + HALO, :] = dext[0:HALO, :]

        ch = c1h_ref[...].astype(F32) * _sigmoid(c2h_ref[...].astype(F32))
        cext[0:HALO, :] = jnp.where(ti > 0, ch, 0.0)

        def pre(r0):
            rows = pl.ds(r0, 32)
            cext[pl.ds(HALO + r0, 32), :] = c1_ref[rows, :].astype(F32) * _sigmoid(c2_ref[rows, :].astype(F32))
            x = cc_ref[rows, :]
            mu = jnp.mean(x, axis=-1, keepdims=True)
            xc = x - mu
            rstd = lax.rsqrt(jnp.mean(xc * xc, axis=-1, keepdims=True) + LN_EPS)
            xh = xc * rstd
            ln = xh * lg_ref[...] + lb_ref[...]
            sg = _sigmoid(ln)
            dln = dcs_ref[rows, :].astype(F32) * sg * (1.0 + ln * (1.0 - sg))
            ds_ref[0] += _rows8(dln * xh)
            ds_ref[1] += _rows8(dln)
            dxh = dln * lg_ref[...]
            dcc = rstd * (dxh - jnp.mean(dxh, axis=-1, keepdims=True) - xh * jnp.mean(dxh * xh, axis=-1, keepdims=True))
            dext[rows, :] = dcc
            ds_ref[2] += _rows8(dcc)
        _row_loop(TE, 32, pre, unroll=5)

        def convt(r0):
            rows = pl.ds(r0, CONV_ROWS)
            for part in range(D // CONV_LANES):
                cols = slice(part * CONV_LANES, (part + 1) * CONV_LANES)
                wd = dext[pl.ds(r0, CONV_ROWS + HALO), cols]
                dc = jnp.zeros((CONV_ROWS, CONV_LANES), F32)
                for s, taps in _shift_classes([(CONF_K - 1 - j, j) for j in range(CONF_K)]):
                    ws = _shifted(wd, s)
                    for a8, j in taps:
                        dc = dc + jnp.tile(cw_ref[j, :, cols], (CONV_ROWS // 8, 1)) * ws[a8:a8 + CONV_ROWS, :]
                dcc = wd[0:CONV_ROWS, :]
                wc = cext[pl.ds(r0, CONV_ROWS + HALO), cols]
                for s, taps in _shift_classes([(2 + j, j) for j in range(CONF_K)]):
                    ws = _shifted(wc, s)
                    for a8, j in taps:
                        dw_ref[j, :, cols] += _rows8(dcc * ws[a8:a8 + CONV_ROWS, :])
                c1 = c1_ref[rows, cols].astype(F32)
                s2 = _sigmoid(c2_ref[rows, cols].astype(F32))
                dp_ref[rows, cols] = (dc * s2).astype(BF16)
                dp_ref[rows, D + part * CONV_LANES:D + (part + 1) * CONV_LANES] = (dc * c1 * s2 * (1.0 - s2)).astype(BF16)
        _row_loop(TE, CONV_ROWS, convt)

    rev = lambda col: pl.BlockSpec((TE, D), lambda i: (nt - 1 - i, col))
    halo = lambda col: pl.BlockSpec((HALO, D), lambda i: (jnp.maximum((nt - 1 - i) * hb - 1, 0), col))
    return _carried_call(
        body, carry, name="conf_bwd", grid=(nt,),
        in_specs=[rev(0), rev(0), rev(2), rev(3), halo(2), halo(3), _resident((32, 8, D)), _resident((1, D)),
                  _resident((1, D)), pl.BlockSpec(memory_space=pl.ANY)],
        out_specs=[pl.BlockSpec((TE, 2 * D), lambda i: (nt - 1 - i, 1)), _resident((32, 8, D)), _resident((3, 8, D))],
        out_shape=[SDS((t, NPROJ), BF16), SDS((32, 8, D), F32), SDS((3, 8, D), F32)],
        scratch_shapes=[pltpu.VMEM((TE + HALO, D), F32), pltpu.VMEM((TE + HALO, D), F32)],
        input_output_aliases={9: 0}, compiler_params=_ARB1)(dcs, cc, proj, proj, proj, proj, cw, lg, lb, dproj)


def _gla_bwd(proj, alr, wau, balpha, do, sall, dproj, carry=()):
    t = proj.shape[0]
    nc = t // CH

    def body(qk_ref, v_ref, a_ref, wau_ref, ba_ref, do_ref, s_ref, dp_in, dp_ref, da_ref, dwau_ref, dba_ref, ds_scr, dla_scr):
        del dp_in

        @pl.when(pl.program_id(0) == 0)
        def _():
            ds_scr[...] = jnp.zeros_like(ds_scr)
            dwau_ref[...] = jnp.zeros_like(dwau_ref)
            dba_ref[...] = jnp.zeros_like(dba_ref)

        z, b, bmid, blast, causal = _gla_decay(a_ref, wau_ref, ba_ref)
        dlasts = []
        for h in range(HEADS):
            ks = slice(h * DKH, (h + 1) * DKH)
            vs = slice(h * DVH, (h + 1) * DVH)
            bh, mh, lh = b[:, ks], bmid[:, ks], blast[:, ks]
            q = qk_ref[:, ks].astype(F32) * (DKH ** -0.5)
            k = qk_ref[:, DK + h * DKH:DK + (h + 1) * DKH].astype(F32)
            v = v_ref[:, vs]
            dout = do_ref[:, vs]
            eq, ek, eb, eg, el = jnp.exp(bh - mh), jnp.exp(mh - bh), jnp.exp(bh), jnp.exp(lh - bh), jnp.exp(lh)
            qt, kt = (q * eq).astype(BF16), (k * ek).astype(BF16)
            qg, kg = (q * eb).astype(BF16), (k * eg).astype(BF16)
            st = s_ref[0, vs, :]
            dsn = ds_scr[vs, :]
            st16, dsn16 = st.astype(BF16), dsn.astype(BF16)
            a = jnp.where(causal, _dot_nt(qt, kt), 0.0).astype(BF16)
            da = jnp.where(causal, _dot_nt(dout, v), 0.0).astype(BF16)
            dq_inter = _dot(dout, st16) * eb
            dk_inter = _dot(v, dsn16) * eg
            dq = _dot(da, kt) * eq + dq_inter
            dk = _dot_tn(da, qt) * ek + dk_inter
            dv = _dot_tn(a, dout) + _dot_nt(kg, dsn16)
            dlasts.append(jnp.sum(k * dk_inter, axis=0, keepdims=True) + jnp.sum(st * dsn, axis=0, keepdims=True) * el[0:1, :])
            dla_scr[:, ks] = q * dq - k * dk
            ds_scr[vs, :] = dsn * jnp.concatenate([el, el], axis=0) + _dot_tn(dout, qg)
            dp_ref[:, ks] = (dq * (DKH ** -0.5)).astype(BF16)
            dp_ref[:, DK + h * DKH:DK + (h + 1) * DKH] = dk.astype(BF16)
            dp_ref[:, D + h * DVH:D + (h + 1) * DVH] = dv.astype(BF16)
        r = lax.broadcasted_iota(jnp.int32, (CH, CH), 0)
        c = lax.broadcasted_iota(jnp.int32, (CH, CH), 1)
        dla = _tri_matmul((r <= c).astype(BF16), dla_scr[...]) + jnp.concatenate(dlasts, axis=1)
        dz = (dla * (1.0 / TAU) * _sigmoid(-z)).astype(BF16)
        da_ref[...] = _dot_nt(dz, wau_ref[...]).astype(BF16)
        dwau_ref[...] += _dot_tn(a_ref[...], dz)
        dba_ref[...] += _rows8(dz.astype(F32))

    rev = lambda w, col: pl.BlockSpec((CH, w), lambda c: (nc - 1 - c, col))
    return _carried_call(
        body, carry, name="gla_bwd", grid=(nc,),
        in_specs=[rev(D, 0), rev(D, 1), rev(LANES, 0), _resident((LANES, DK)), _resident((1, DK)), rev(D, 0),
                  pl.BlockSpec((1, DV, DKH), lambda c: (nc - 1 - c, 0, 0)), pl.BlockSpec(memory_space=pl.ANY)],
        out_specs=[rev(2 * D, 0), rev(LANES, 0), _resident((LANES, DK)), _resident((8, DK))],
        out_shape=[SDS((t, NPROJ), BF16), SDS((t, LANES), BF16), SDS((LANES, DK), F32), SDS((8, DK), F32)],
        scratch_shapes=[pltpu.VMEM((DV, DKH), F32), pltpu.VMEM((CH, DK), F32)],
        input_output_aliases={7: 0}, compiler_params=_ARB1)(proj, proj, alr, wau, balpha, do, sall, dproj)


def _all_gather(xs, name):
    n = len(xs)

    def body(*refs):
        x_refs, out_refs = refs[:n], refs[n:2 * n]
        send_sems, recv_sems = refs[2 * n:]
        x, y, c = _my_place()
        me, sibling = (x, y, c), (x, y, 1 - c)
        x_nbr, y_nbr, diagonal = (1 - x, y), (x, 1 - y), (1 - x, 1 - y)

        def slot(p, px, py, pc):
            return out_refs[p].at[4 * px + 2 * py + pc]

        def copy(p, k, block, to, src=None):
            return pltpu.make_async_remote_copy(
                src_ref=slot(p, *block) if src is None else src, dst_ref=slot(p, *block),
                send_sem=send_sems.at[7 * p + k], recv_sem=recv_sems.at[7 * p + k], device_id=to, device_id_type=MESH_T)

        for p in range(n):
            for k, to in enumerate((sibling, (*x_nbr, c), (*y_nbr, c))):
                copy(p, k, me, to, src=x_refs[p]).start()
        for p in range(n):
            @pl.when(c == 1)
            def _():
                copy(p, 2, (*y_nbr, c), me).wait_recv()
                copy(p, 3, (*y_nbr, c), (*x_nbr, c)).start()
                copy(p, 1, (*x_nbr, c), me).wait_recv()

            @pl.when(c == 0)
            def _():
                copy(p, 1, (*x_nbr, c), me).wait_recv()
                copy(p, 3, (*x_nbr, c), (*y_nbr, c)).start()
                copy(p, 2, (*y_nbr, c), me).wait_recv()
        for p in range(n):
            copy(p, 4, (*x_nbr, c), sibling).start()
            copy(p, 5, (*y_nbr, c), sibling).start()
        for p in range(n):
            copy(p, 3, (*diagonal, c), me).wait_recv()
            copy(p, 6, (*diagonal, c), sibling).start()
        for p in range(n):
            copy(p, 0, sibling, me).wait_recv()
            for j, chip in enumerate((x_nbr, y_nbr, diagonal)):
                copy(p, 4 + j, (*chip, 1 - c), me).wait_recv()
        for p in range(n):
            for k in range(7):
                copy(p, k, me, sibling, src=x_refs[p]).wait_send()

    hbm = pl.BlockSpec(memory_space=pl.ANY)
    return pl.pallas_call(
        body, name=name, out_shape=[SDS((N_DEV, *a.shape), a.dtype) for a in xs],
        in_specs=[hbm] * n, out_specs=[hbm] * n,
        scratch_shapes=[pltpu.SemaphoreType.DMA((7 * n,)), pltpu.SemaphoreType.DMA((7 * n,))])(*xs)


def _exchange(gs):
    n = len(gs)

    def body(*refs):
        g_refs, land_refs = refs[:n], refs[n:2 * n]
        send_sems, recv_sems, local_sems = refs[2 * n:]
        x, y, c = _my_place()
        my_idx = 4 * x + 2 * y + c
        mine = [pltpu.make_async_copy(g_refs[p].at[my_idx], land_refs[p].at[my_idx], local_sems.at[p]) for p in range(n)]
        for cp in mine:
            cp.start()
        copies = []
        for k in range(1, N_DEV):
            px, py, pc = _flip(x, k & 4), _flip(y, k & 2), _flip(c, k & 1)
            p_idx = 4 * px + 2 * py + pc
            for p in range(n):
                s = 7 * p + k - 1
                cp = pltpu.make_async_remote_copy(
                    src_ref=g_refs[p].at[p_idx], dst_ref=land_refs[p].at[my_idx], send_sem=send_sems.at[s],
                    recv_sem=recv_sems.at[s], device_id=(px, py, pc), device_id_type=MESH_T)
                cp.start()
                arrival = pltpu.make_async_remote_copy(
                    src_ref=g_refs[p].at[p_idx], dst_ref=land_refs[p].at[p_idx], send_sem=send_sems.at[s],
                    recv_sem=recv_sems.at[s], device_id=(px, py, pc), device_id_type=MESH_T)
                copies.append((cp, arrival))
        for cp, arrival in copies:
            arrival.wait_recv()
        for cp, arrival in copies:
            cp.wait_send()
        for cp in mine:
            cp.wait()

    hbm = pl.BlockSpec(memory_space=pl.ANY)
    return pl.pallas_call(
        body, name="grad_exchange", out_shape=[SDS(g.shape, g.dtype) for g in gs],
        in_specs=[hbm] * n, out_specs=[hbm] * n,
        scratch_shapes=[pltpu.SemaphoreType.DMA((7 * n,)), pltpu.SemaphoreType.DMA((7 * n,)),
                        pltpu.SemaphoreType.DMA((n,))])(*gs)


def _adamw(land, w, m, v, rows_blk, name):
    rows = w.shape[0]

    def body(l_ref, w_ref, m_ref, v_ref, g_ref, d_ref, nm_ref, nv_ref):
        g = l_ref[0].astype(F32)
        for s in range(1, N_DEV):
            g = g + l_ref[s].astype(F32)
        nm = ADAM_B1 * m_ref[...] + (1.0 - ADAM_B1) * g
        nv = ADAM_B2 * v_ref[...] + (1.0 - ADAM_B2) * (g * g)
        m_hat = nm / (1.0 - ADAM_B1 ** ADAM_STEP)
        v_hat = nv / (1.0 - ADAM_B2 ** ADAM_STEP)
        g_ref[...] = g
        d_ref[...] = -ADAM_LR * (m_hat / (jnp.sqrt(v_hat) + ADAM_EPS) + ADAM_WD * w_ref[...])
        nm_ref[...] = nm
        nv_ref[...] = nv

    blk = pl.BlockSpec((rows_blk, D), lambda i: (i, 0))
    return pl.pallas_call(
        body, name=name, grid=(rows // rows_blk,),
        in_specs=[pl.BlockSpec((N_DEV, rows_blk, D), lambda i: (0, i, 0)), blk, blk, blk],
        out_specs=[blk] * 4, out_shape=[SDS((rows, D), F32)] * 4, compiler_params=_ARB1)(land, w, m, v)


BIG = ("w_in", "w_up", "w_down", "w_gla_o", "w_conf_o", "w_out")
BIG_TRANSPOSED = ("w_in", "w_up")
SMALL_SHARDED = ("meta_tokens", "conf_dw_w", "ffn_dw_w", "w_alpha_up")
REPLICATED = ("norm_mix_g", "b_alpha", "gla_norm_g", "conf_dw_b", "conf_ln_g", "conf_ln_b", "norm_ffn_g", "ffn_dw_b",
              "final_norm_g")
N_IN = sum(IN_WIDTHS)
W_IN_ROWS = N_IN // N_DEV
W_IN_PAD = -(-W_IN_ROWS // 16) * 16
ADAM_BLOCK = {"w_in": W_IN_PAD // 3, "w_up": 176, "w_down": 176, "w_gla_o": 128, "w_conf_o": 128, "w_out": 128}
SMALL_ROWS = 32


def _to_panel(name, shard):
    a = shard.reshape(shard.shape[-2], shard.shape[-1])
    if name in BIG_TRANSPOSED:
        a = a.T
    if name == "w_in":
        a = jnp.pad(a, ((0, W_IN_PAD - W_IN_ROWS), (0, 0)))
    return a


def _from_panel(name, panel, shape):
    a = panel[0:W_IN_ROWS] if name == "w_in" else panel
    if name in BIG_TRANSPOSED:
        a = a.T
    return a.reshape(shape)


def _pack_small(arrs):
    flat = jnp.concatenate([jnp.pad(a.reshape(-1), (0, (-a.size) % D)) for a in arrs])
    return jnp.pad(flat, (0, SMALL_ROWS * D - flat.shape[0])).reshape(SMALL_ROWS, D)


def _unpack_small(panel, shapes):
    flat, out, off = panel.reshape(-1), [], 0
    for shp in shapes:
        n = 1
        for s in shp:
            n *= s
        out.append(flat[off:off + n].reshape(shp))
        off += n + (-n) % D
    return out


def _local_step(x, target, w, shards=None):
    dist = shards is not None
    w = dict(w)

    def gather(names):
        return [(shards[n], False) for n in names] if dist else []

    def scatter(*arrs):
        return [(a.reshape(N_DEV, -1, D), True) for a in arrs] if dist else []

    s = x.shape[0]
    n_real = s + N_META
    t = -(-n_real // TM) * TM

    q0, r0, a0, c0 = 0, 2 * DK + DV, 2 * DK + 2 * DV, 2 * DK + 2 * DV + RANK
    wt = w["w_in_t"]
    w_main = jnp.concatenate([wt[q0:r0], wt[c0:N_IN], wt[r0:a0]], axis=0)
    w_a = jnp.pad(wt[a0:c0], ((0, LANES - RANK), (0, 0)))
    wau = jnp.pad(w["w_alpha_up"].astype(BF16), ((0, LANES - RANK), (0, 0)))
    row = lambda name: w[name].reshape(1, -1)
    cw = jnp.broadcast_to(jnp.pad(w["conf_dw_w"], ((0, 32 - CONF_K), (0, 0)))[:, None, :], (32, 8, D))
    fw = jnp.broadcast_to(jnp.concatenate([w["ffn_dw_w"], w["ffn_dw_b"].reshape(1, -1)], axis=0)[:, None, :],
                          (FFN_K + 1, 16, DFF))

    early = ("w_gla_o", "w_conf_o", "w_out", "w_up")
    h0, u1, proj, alr, *landed = _in_proj(x, w["meta_tokens"], row("norm_mix_g"), w_main, w_a, t, carry=gather(early))
    for n, land in zip(early, landed):
        w["w_up_t" if n == "w_up" else n] = land.reshape(-1, D)
    o, og, sall = _gla_fwd(proj, alr, wau, row("b_alpha"), row("gla_norm_g"))
    cc, cs, *landed = _conf_fwd(proj, cw, row("conf_dw_b"), row("conf_ln_g"), row("conf_ln_b"), carry=gather(("w_down",)))
    if dist:
        w["w_down"] = landed[0].reshape(-1, D)
    brg, brc, merged, h1 = _mix_fwd(og, cs, proj, h0, w["w_gla_o"], w["w_conf_o"], w["w_out"])
    u2, up = _norm_matmul(h1, row("norm_ffn_g"), w["w_up_t"], 512, "up_proj")
    f, dh2, red = _ffn_out(up, fw, h1, w["w_down"], row("final_norm_g"), target)
    loss = 0.5 / D * jnp.sum(red[0:8])

    g = {"final_norm_g": jnp.sum(red[8:16], axis=0)}
    dup, dfw = _ffn_bwd(dh2, w["w_down"], up, fw)
    g["ffn_dw_w"] = jnp.sum(dfw[0:FFN_K], axis=1)
    g["ffn_dw_b"] = jnp.sum(dfw[3], axis=0)
    g["w_down"] = _wgrad(f, dh2, 1408, "wgrad_down")
    dh1, dg2, *landed = _dgrad_norm(dup, w["w_up_t"], h1, row("norm_ffn_g"), dh2, "up_dgrad", carry=scatter(g["w_down"]))
    if dist:
        g["w_down"] = landed[0]
    g["norm_ffn_g"] = jnp.sum(dg2, axis=0)
    g["w_up_t"] = _wgrad(dup, u2, 1408, "wgrad_up")
    dbrg, dbrc, dog, dcs, dproj = _mix_bwd(dh1, w["w_out"], w["w_gla_o"], w["w_conf_o"], proj, brg, brc)
    g["w_out"] = _wgrad(merged, dh1, 1024, "wgrad_out")
    g["w_gla_o"] = _wgrad(og, dbrg, 1024, "wgrad_gla_o")
    g["w_conf_o"] = _wgrad(cs, dbrc, 1024, "wgrad_conf_o")
    do, dproj, dgn = _glapost_bwd(dog, o, proj, row("gla_norm_g"), dproj)
    g["gla_norm_g"] = jnp.sum(dgn, axis=0)
    dproj, dcw, dst, *landed = _conf_bwd(dcs, cc, proj, cw, row("conf_ln_g"), row("conf_ln_b"), dproj,
                                         carry=scatter(g["w_up_t"]))
    if dist:
        g["w_up_t"] = landed[0]
    g["conf_dw_w"] = jnp.sum(dcw[0:CONF_K], axis=1)
    g["conf_ln_g"], g["conf_ln_b"], g["conf_dw_b"] = jnp.sum(dst[0], axis=0), jnp.sum(dst[1], axis=0), jnp.sum(dst[2], axis=0)
    dproj, dalr, dwau, dba, *landed = _gla_bwd(proj, alr, wau, row("b_alpha"), do, sall, dproj,
                                               carry=scatter(g["w_out"], g["w_gla_o"], g["w_conf_o"]))
    if dist:
        g["w_out"], g["w_gla_o"], g["w_conf_o"] = landed
    g["w_alpha_up"] = dwau[0:RANK]
    g["b_alpha"] = jnp.sum(dba, axis=0)
    dw_main = _wgrad(dproj, u1, 1024, "wgrad_in")
    dw_a = _wgrad(dalr, u1, LANES, "wgrad_alr")
    g["w_in_t"] = jnp.concatenate([dw_main[0:r0], dw_main[NPROJ - DV:NPROJ], dw_a[0:RANK], dw_main[r0:NPROJ - DV]], axis=0)
    w_in_blocks = []
    if dist:
        pad = ((0, 0), (0, W_IN_PAD - W_IN_ROWS), (0, 0))
        w_in_blocks = [(jnp.pad(g["w_in_t"].reshape(N_DEV, W_IN_ROWS, D), pad), True)]
    dh0, dg1, *landed = _in_dgrad(dproj, w_main, h0, row("norm_mix_g"), dh1, dalr, w_a, carry=w_in_blocks)
    if dist:
        g["w_in_t"] = landed[0]
    g["norm_mix_g"] = jnp.sum(dg1, axis=0)
    g["meta_tokens"] = dh0[0:N_META]
    return loss, dh0[N_META:n_real], g


def kernel(x, meta_tokens, norm_mix_g, w_in, w_alpha_up, b_alpha, gla_norm_g, w_gla_o, conf_dw_w, conf_dw_b, conf_ln_g, conf_ln_b, w_conf_o, w_out, norm_ffn_g, w_up, ffn_dw_w, ffn_dw_b, w_down, final_norm_g, loss_target, m_meta_tokens, m_norm_mix_g, m_w_in, m_w_alpha_up, m_b_alpha, m_gla_norm_g, m_w_gla_o, m_conf_dw_w, m_conf_dw_b, m_conf_ln_g, m_conf_ln_b, m_w_conf_o, m_w_out, m_norm_ffn_g, m_w_up, m_ffn_dw_w, m_ffn_dw_b, m_w_down, m_final_norm_g, v_meta_tokens, v_norm_mix_g, v_w_in, v_w_alpha_up, v_b_alpha, v_gla_norm_g, v_w_gla_o, v_conf_dw_w, v_conf_dw_b, v_conf_ln_g, v_conf_ln_b, v_w_conf_o, v_w_out, v_norm_ffn_g, v_w_up, v_ffn_dw_w, v_ffn_dw_b, v_w_down, v_final_norm_g):
    ws = dict(meta_tokens=meta_tokens, norm_mix_g=norm_mix_g, w_in=w_in, w_alpha_up=w_alpha_up, b_alpha=b_alpha,
              gla_norm_g=gla_norm_g, w_gla_o=w_gla_o, conf_dw_w=conf_dw_w, conf_dw_b=conf_dw_b, conf_ln_g=conf_ln_g,
              conf_ln_b=conf_ln_b, w_conf_o=w_conf_o, w_out=w_out, norm_ffn_g=norm_ffn_g, w_up=w_up, ffn_dw_w=ffn_dw_w,
              ffn_dw_b=ffn_dw_b, w_down=w_down, final_norm_g=final_norm_g)
    ms = dict(meta_tokens=m_meta_tokens, norm_mix_g=m_norm_mix_g, w_in=m_w_in, w_alpha_up=m_w_alpha_up, b_alpha=m_b_alpha,
              gla_norm_g=m_gla_norm_g, w_gla_o=m_w_gla_o, conf_dw_w=m_conf_dw_w, conf_dw_b=m_conf_dw_b,
              conf_ln_g=m_conf_ln_g, conf_ln_b=m_conf_ln_b, w_conf_o=m_w_conf_o, w_out=m_w_out, norm_ffn_g=m_norm_ffn_g,
              w_up=m_w_up, ffn_dw_w=m_ffn_dw_w, ffn_dw_b=m_ffn_dw_b, w_down=m_w_down, final_norm_g=m_final_norm_g)
    vs = dict(meta_tokens=v_meta_tokens, norm_mix_g=v_norm_mix_g, w_in=v_w_in, w_alpha_up=v_w_alpha_up, b_alpha=v_b_alpha,
              gla_norm_g=v_gla_norm_g, w_gla_o=v_w_gla_o, conf_dw_w=v_conf_dw_w, conf_dw_b=v_conf_dw_b,
              conf_ln_g=v_conf_ln_g, conf_ln_b=v_conf_ln_b, w_conf_o=v_w_conf_o, w_out=v_w_out, norm_ffn_g=v_norm_ffn_g,
              w_up=v_w_up, ffn_dw_w=v_ffn_dw_w, ffn_dw_b=v_ffn_dw_b, w_down=v_w_down, final_norm_g=v_final_norm_g)
    small = SMALL_SHARDED + REPLICATED
    pack_small = lambda d: _pack_small([d[n] for n in small])

    shards = {n: _to_panel(n, ws[n]).astype(BF16) for n in BIG}
    own = [shards["w_in"], pack_small(ws)]
    my_idx = 4 * lax.axis_index("x") + 2 * lax.axis_index("y") + lax.axis_index("c")
    gathered = [lax.dynamic_update_slice(full_, mine[None], (my_idx, 0, 0))
                for full_, mine in zip(_all_gather(own, "weight_gather"), own)]
    full = {n: ws[n].reshape(-1) for n in REPLICATED}
    full["w_in_t"] = gathered[0][:, 0:W_IN_ROWS].reshape(N_IN, D)
    flat, off = gathered[1].reshape(N_DEV, -1), 0
    for n in SMALL_SHARDED:
        k, c = ws[n].shape[-2], ws[n].shape[-1]
        full[n] = flat[:, off:off + k * c].reshape(N_DEV, k, c).transpose(1, 0, 2).reshape(k, N_DEV * c)
        off += k * c + (-(k * c)) % D

    loss, grad_x, g = _local_step(x[0], loss_target[0], full, shards)

    lands = [g["w_in_t"], g["w_up_t"]] + [g[n] for n in BIG[2:]]
    blocks = []
    for n in SMALL_SHARDED:
        k, c = ws[n].shape[-2], ws[n].shape[-1]
        b = g[n].reshape(k, N_DEV, c).transpose(1, 0, 2).reshape(N_DEV, k * c)
        blocks.append(jnp.pad(b, ((0, 0), (0, (-(k * c)) % D))))
    for n in REPLICATED:
        b = jnp.broadcast_to(g[n].reshape(1, -1), (N_DEV, g[n].size))
        blocks.append(jnp.pad(b, ((0, 0), (0, (-b.shape[1]) % D))))
    gsm = jnp.concatenate(blocks, axis=1)
    lands += _exchange([jnp.pad(gsm, ((0, 0), (0, SMALL_ROWS * D - gsm.shape[1]))).reshape(N_DEV, SMALL_ROWS, D)])

    grad, delta, new_m, new_v = {}, {}, {}, {}
    for i, n in enumerate(BIG):
        outs = _adamw(lands[i], _to_panel(n, ws[n]), _to_panel(n, ms[n]), _to_panel(n, vs[n]), ADAM_BLOCK[n], "adamw_" + n)
        grad[n], delta[n], new_m[n], new_v[n] = [_from_panel(n, p, ws[n].shape) for p in outs]
    outs = _adamw(lands[len(BIG)], pack_small(ws), pack_small(ms), pack_small(vs), SMALL_ROWS, "adamw_small")
    shapes = [ws[n].shape for n in small]
    for d, p in zip((grad, delta, new_m, new_v), outs):
        d.update(zip(small, _unpack_small(p, shapes)))

    order = ("meta_tokens", "norm_mix_g", "w_in", "w_alpha_up", "b_alpha", "gla_norm_g", "w_gla_o", "conf_dw_w", "conf_dw_b",
             "conf_ln_g", "conf_ln_b", "w_conf_o", "w_out", "norm_ffn_g", "w_up", "ffn_dw_w", "ffn_dw_b", "w_down",
             "final_norm_g")
    loss = lax.psum(loss, ("x", "y", "c"))
    return (loss, grad_x[None], *[grad[n] for n in order], *[delta[n] for n in order], *[new_m[n] for n in order],
            *[new_v[n] for n in order])
```

```python
import functools

import jax
import jax.numpy as jnp
from jax import lax
from jax.experimental import pallas as pl
from jax.experimental.pallas import tpu as pltpu

F32, BF16 = jnp.float32, jnp.bfloat16
SDS = jax.ShapeDtypeStruct

D = 1024
N_META = 16
HEADS = 4
DK, DKH, DV, DVH = 512, 128, 1024, 256
RANK = 16
TAU = 16.0
CONF_K = 31
DFF = 2816
FFN_K = 3
IN_WIDTHS = (DK, DK, DV, DV, RANK, 2 * D, D, D)
RMS_EPS, LN_EPS = 1e-6, 1e-5
ADAM_LR, ADAM_B1, ADAM_B2, ADAM_EPS, ADAM_WD, ADAM_STEP = 0.001, 0.9, 0.999, 1e-08, 0.01, 10

NPROJ = 7 * D
LANES = 128
CH = 128
TM = 640
TM_BIG = 1664
TE = 320
HALO = 32
HALO_F = 16
N_DEV = 8
VMEM_LIMIT = 60 * 1024 * 1024
MESH_T = pl.DeviceIdType.MESH

_ARB1 = pltpu.CompilerParams(dimension_semantics=("arbitrary",), vmem_limit_bytes=VMEM_LIMIT)
_ARB2 = pltpu.CompilerParams(dimension_semantics=("arbitrary", "arbitrary"), vmem_limit_bytes=VMEM_LIMIT)


def _dot(a, b):
    return jnp.dot(a, b, preferred_element_type=F32)


def _dot_nt(a, b):
    return lax.dot_general(a, b, (((1,), (1,)), ((), ())), preferred_element_type=F32)


def _dot_tn(a, b):
    return lax.dot_general(a, b, (((0,), (0,)), ((), ())), preferred_element_type=F32)


def _sigmoid(x):
    return 0.5 * jnp.tanh(0.5 * x) + 0.5


def _rows8(x):
    return x.reshape(x.shape[0] // 8, 8, x.shape[1]).sum(axis=0)


def _row_tile(t, preferred):
    return preferred if t % preferred == 0 else TM


def _row_loop(n_rows, rb, fn, unroll=1):
    def step(i, carry):
        fn(pl.multiple_of(i * rb, rb))
        return carry
    lax.fori_loop(0, n_rows // rb, step, 0, unroll=unroll)


def _resident(shape):
    return pl.BlockSpec(shape, lambda *_: (0,) * len(shape))


def _once(shape):
    return pl.BlockSpec(shape, lambda *_: (0,) * len(shape), pipeline_mode=pl.Buffered(1))


CONV_ROWS, CONV_LANES = 64, 256


def _shift_classes(offset_taps):
    return [(s, [(o - s, j) for o, j in offset_taps if o % 8 == s]) for s in range(8)]


def _shifted(win, s):
    return win if s == 0 else win[s:s + CONV_ROWS + HALO - 8, :]


def _my_place():
    return lax.axis_index("x"), lax.axis_index("y"), lax.axis_index("c")


def _flip(v, bit):
    return 1 - v if bit else v


def _exchange_copies(src_refs, land_refs, scatter, send_sems, recv_sems, local_sems, arrivals):
    x, y, c = _my_place()
    my_idx = 4 * x + 2 * y + c
    local, remote = [], []
    for p, (src, land) in enumerate(zip(src_refs, land_refs)):
        local.append(pltpu.make_async_copy(src.at[my_idx] if scatter[p] else src, land.at[my_idx], local_sems.at[p]))
    for k in range(1, N_DEV):
        px, py, pc = _flip(x, k & 4), _flip(y, k & 2), _flip(c, k & 1)
        p_idx = 4 * px + 2 * py + pc
        for p, (src, land) in enumerate(zip(src_refs, land_refs)):
            s = 7 * p + k - 1
            out = src.at[p_idx] if scatter[p] else src

            def copy(dst):
                return pltpu.make_async_remote_copy(src_ref=out, dst_ref=dst, send_sem=send_sems.at[s],
                                                    recv_sem=recv_sems.at[s], device_id=(px, py, pc), device_id_type=MESH_T)
            remote.append((copy(land.at[my_idx]), copy(land.at[p_idx]) if arrivals else None))
    return local, remote


def _carried_call(core, carry, *, grid, in_specs, out_specs, out_shape, scratch_shapes=(), **kw):
    n_in, n_out, nc, n_scr = len(in_specs), len(out_specs), len(carry), len(scratch_shapes)
    if nc == 0:
        return pl.pallas_call(core, grid=grid, in_specs=in_specs, out_specs=out_specs, out_shape=out_shape,
                              scratch_shapes=list(scratch_shapes), **kw)
    scatter = [sc for _, sc in carry]

    def body(*refs):
        ins, cin = refs[:n_in], refs[n_in:n_in + nc]
        outs, cout = refs[n_in + nc:n_in + nc + n_out], refs[n_in + nc + n_out:n_in + 2 * nc + n_out]
        scr, sems = refs[n_in + 2 * nc + n_out:n_in + 2 * nc + n_out + n_scr], refs[-3:]
        first = functools.reduce(jnp.logical_and, [pl.program_id(a) == 0 for a in range(len(grid))])
        last = functools.reduce(jnp.logical_and, [pl.program_id(a) == grid[a] - 1 for a in range(len(grid))])

        @pl.when(first)
        def _():
            local, remote = _exchange_copies(cin, cout, scatter, *sems, arrivals=False)
            for cp in local:
                cp.start()
            for send, _ in remote:
                send.start()

        core(*ins, *outs, *scr)

        @pl.when(last)
        def _():
            local, remote = _exchange_copies(cin, cout, scatter, *sems, arrivals=True)
            for _, arrival in remote:
                arrival.wait_recv()
            for send, _ in remote:
                send.wait_send()
            for cp in local:
                cp.wait()

    hbm = pl.BlockSpec(memory_space=pl.ANY)
    land_shape = [SDS((N_DEV, *(a.shape[1:] if sc else a.shape)), a.dtype) for a, sc in carry]
    sems = [pltpu.SemaphoreType.DMA((7 * nc,)), pltpu.SemaphoreType.DMA((7 * nc,)), pltpu.SemaphoreType.DMA((nc,))]
    call = pl.pallas_call(body, grid=grid, in_specs=list(in_specs) + [hbm] * nc, out_specs=list(out_specs) + [hbm] * nc,
                          out_shape=list(out_shape) + land_shape, scratch_shapes=list(scratch_shapes) + sems, **kw)
    return lambda *args: call(*args, *[a for a, _ in carry])


def _norm_matmul(h, g, w_t, tn, name, w_extra_t=None, carry=()):
    t, n = h.shape[0], w_t.shape[0]
    tm = _row_tile(t, TM_BIG)
    nt, nb = t // tm, n // tn

    def body(*refs):
        if w_extra_t is None:
            h_ref, g_ref, w_ref, u_ref, p_ref = refs
        else:
            h_ref, g_ref, w_ref, we_ref, u_ref, p_ref, e_ref = refs

        @pl.when(pl.program_id(1) == 0)
        def _():
            def blk(r0):
                x = h_ref[pl.ds(r0, 32), :]
                rinv = lax.rsqrt(jnp.mean(x * x, axis=-1, keepdims=True) + RMS_EPS)
                u_ref[pl.ds(r0, 32), :] = (x * rinv * g_ref[...]).astype(BF16)
            _row_loop(tm, 32, blk, unroll=4)
            if w_extra_t is not None:
                e_ref[...] = _dot_nt(u_ref[...], we_ref[...]).astype(BF16)

        p_ref[...] = _dot_nt(u_ref[...], w_ref[...]).astype(BF16)

    in_specs = [pl.BlockSpec((tm, D), lambda i, j: (i, 0)), _resident((1, D)), pl.BlockSpec((tn, D), lambda i, j: (j, 0))]
    out_specs = [pl.BlockSpec((tm, D), lambda i, j: (i, 0)), pl.BlockSpec((tm, tn), lambda i, j: (i, j))]
    out_shape = [SDS((t, D), BF16), SDS((t, n), BF16)]
    args = [h, g, w_t]
    if w_extra_t is not None:
        in_specs.append(_resident(w_extra_t.shape))
        out_specs.append(pl.BlockSpec((tm, w_extra_t.shape[0]), lambda i, j: (i, 0)))
        out_shape.append(SDS((t, w_extra_t.shape[0]), BF16))
        args.append(w_extra_t)
    return _carried_call(body, carry, name=name, grid=(nt, nb), in_specs=in_specs, out_specs=out_specs,
                         out_shape=out_shape, compiler_params=_ARB2)(*args)


def _in_proj(x, meta, g, w_t, w_a_t, t, carry=()):
    n_real = x.shape[0] + N_META
    n, tn = w_t.shape[0], 1024
    tm = _row_tile(t, TM_BIG)
    nt, nb = t // tm, n // tn

    def body(x_ref, xh_ref, m_ref, g_ref, w_ref, wa_ref, h_ref, u_ref, p_ref, e_ref):
        i = pl.program_id(0)

        @pl.when(pl.program_id(1) == 0)
        def _():
            def rows_of(r0, val):
                gid = i * tm + r0 + lax.broadcasted_iota(jnp.int32, (32, 1), 0)
                val = jnp.where(gid < n_real, val, 0.0)
                h_ref[pl.ds(r0, 32), :] = val
                rinv = lax.rsqrt(jnp.mean(val * val, axis=-1, keepdims=True) + RMS_EPS)
                u_ref[pl.ds(r0, 32), :] = (val * rinv * g_ref[...]).astype(BF16)

            before = jnp.where(i == 0, m_ref[...], xh_ref[...])
            rows_of(0, jnp.concatenate([before, x_ref[0:N_META, :]], axis=0))

            def blk(k, c):
                r0 = pl.multiple_of(k * 32, 32)
                rows_of(r0, x_ref[pl.ds(pl.multiple_of(r0 - N_META, N_META), 32), :])
                return c
            lax.fori_loop(1, tm // 32, blk, 0, unroll=3)
            e_ref[...] = _dot_nt(u_ref[...], wa_ref[...]).astype(BF16)

        p_ref[...] = _dot_nt(u_ref[...], w_ref[...]).astype(BF16)

    row = lambda w: pl.BlockSpec((tm, w), lambda i, j: (i, 0))
    return _carried_call(
        body, carry, name="in_proj", grid=(nt, nb),
        in_specs=[row(D), pl.BlockSpec((N_META, D), lambda i, j: (jnp.maximum(i * (tm // N_META) - 1, 0), 0)),
                  _resident((N_META, D)), _resident((1, D)), pl.BlockSpec((tn, D), lambda i, j: (j, 0)),
                  _resident(w_a_t.shape)],
        out_specs=[row(D), row(D), pl.BlockSpec((tm, tn), lambda i, j: (i, j)), row(w_a_t.shape[0])],
        out_shape=[SDS((t, D), F32), SDS((t, D), BF16), SDS((t, n), BF16), SDS((t, w_a_t.shape[0]), BF16)],
        compiler_params=_ARB2)(x, x, meta, g, w_t, w_a_t)


def _gla_decay(a_ref, wau_ref, ba_ref):
    z = _dot(a_ref[...], wau_ref[...]) + ba_ref[...]
    la = (jnp.minimum(z, 0.0) - jnp.log(1.0 + jnp.exp(-jnp.abs(z)))) * (1.0 / TAU)
    r = lax.broadcasted_iota(jnp.int32, (CH, CH), 0)
    c = lax.broadcasted_iota(jnp.int32, (CH, CH), 1)
    b = _tri_matmul((r >= c).astype(BF16), la)
    mid = jnp.broadcast_to(b[CH // 2:CH // 2 + 1, :], b.shape)
    last = jnp.broadcast_to(b[CH - 1:CH, :], b.shape)
    return z, b, mid, last, r >= c


def _tri_matmul(tri, x):
    n = x.shape[1]
    x1 = x.astype(BF16)
    r1 = x - x1.astype(F32)
    x2 = r1.astype(BF16)
    x3 = (r1 - x2.astype(F32)).astype(BF16)
    y = _dot(tri, jnp.concatenate([x1, x2, x3], axis=1))
    return y[:, 0:n] + y[:, n:2 * n] + y[:, 2 * n:3 * n]


def _gla_fwd(proj, alr, wau, balpha, gn):
    t = proj.shape[0]
    nc = t // CH

    def body(qk_ref, v_ref, r_ref, a_ref, wau_ref, ba_ref, gn_ref, o_ref, og_ref, sall_ref, s_scr):
        @pl.when(pl.program_id(0) == 0)
        def _():
            s_scr[...] = jnp.zeros_like(s_scr)

        sall_ref[0] = s_scr[...]
        _, b, bmid, blast, causal = _gla_decay(a_ref, wau_ref, ba_ref)
        for h in range(HEADS):
            ks = slice(h * DKH, (h + 1) * DKH)
            vs = slice(h * DVH, (h + 1) * DVH)
            bh, mh, lh = b[:, ks], bmid[:, ks], blast[:, ks]
            q = qk_ref[:, ks].astype(F32) * (DKH ** -0.5)
            k = qk_ref[:, DK + h * DKH:DK + (h + 1) * DKH].astype(F32)
            v = v_ref[:, vs]
            qt = (q * jnp.exp(bh - mh)).astype(BF16)
            kt = (k * jnp.exp(mh - bh)).astype(BF16)
            qg = (q * jnp.exp(bh)).astype(BF16)
            kg = (k * jnp.exp(lh - bh)).astype(BF16)
            a = jnp.where(causal, _dot_nt(qt, kt), 0.0)
            st = s_scr[vs, :]
            o = _dot(a.astype(BF16), v) + _dot_nt(qg, st.astype(BF16))
            el = jnp.exp(lh)
            s_scr[vs, :] = st * jnp.concatenate([el, el], axis=0) + _dot_tn(v, kg)
            o_ref[:, vs] = o
            on = o * lax.rsqrt(jnp.mean(o * o, axis=-1, keepdims=True) + RMS_EPS) * gn_ref[:, vs]
            rr = r_ref[:, vs].astype(F32)
            og_ref[:, vs] = (on * (rr * _sigmoid(rr))).astype(BF16)

    return pl.pallas_call(
        body, name="gla_fwd", grid=(nc,),
        in_specs=[pl.BlockSpec((CH, D), lambda c: (c, 0)), pl.BlockSpec((CH, D), lambda c: (c, 1)),
                  pl.BlockSpec((CH, D), lambda c: (c, 6)), pl.BlockSpec((CH, LANES), lambda c: (c, 0)),
                  _resident((LANES, DK)), _resident((1, DK)), _resident((1, DV))],
        out_specs=[pl.BlockSpec((CH, DV), lambda c: (c, 0)), pl.BlockSpec((CH, DV), lambda c: (c, 0)),
                   pl.BlockSpec((1, DV, DKH), lambda c: (c, 0, 0))],
        out_shape=[SDS((t, DV), F32), SDS((t, DV), BF16), SDS((nc, DV, DKH), F32)],
        scratch_shapes=[pltpu.VMEM((DV, DKH), F32)], compiler_params=_ARB1)(proj, proj, proj, alr, wau, balpha, gn)


def _conf_fwd(proj, cw, cb, lg, lb, carry=()):
    t = proj.shape[0]
    nt = t // TE

    def body(c1_ref, c2_ref, cw_ref, cb_ref, lg_ref, lb_ref, cc_ref, cs_ref, cext):
        i = pl.program_id(0)

        @pl.when(i == 0)
        def _():
            cext[0:HALO, :] = jnp.zeros((HALO, D), F32)

        @pl.when(i > 0)
        def _():
            cext[0:HALO, :] = cext[TE:TE + HALO, :]

        def glu(r0):
            c2 = c2_ref[pl.ds(r0, 32), :].astype(F32)
            cext[pl.ds(HALO + r0, 32), :] = c1_ref[pl.ds(r0, 32), :].astype(F32) * _sigmoid(c2)
        _row_loop(TE, 32, glu)

        def conv(r0):
            for part in range(D // CONV_LANES):
                cols = slice(part * CONV_LANES, (part + 1) * CONV_LANES)
                win = cext[pl.ds(r0, CONV_ROWS + HALO), cols]
                acc = jnp.zeros((CONV_ROWS, CONV_LANES), F32) + cb_ref[:, cols]
                for s, taps in _shift_classes([(2 + j, j) for j in range(CONF_K)]):
                    ws = _shifted(win, s)
                    for a8, j in taps:
                        acc = acc + jnp.tile(cw_ref[j, :, cols], (CONV_ROWS // 8, 1)) * ws[a8:a8 + CONV_ROWS, :]
                cc_ref[pl.ds(r0, CONV_ROWS), cols] = acc

        def norm(r0):
            for sub in range(CONV_ROWS // 32):
                rows = pl.ds(r0 + 32 * sub, 32)
                x = cc_ref[rows, :]
                xc = x - jnp.mean(x, axis=-1, keepdims=True)
                var = jnp.mean(xc * xc, axis=-1, keepdims=True)
                ln = xc * lax.rsqrt(var + LN_EPS) * lg_ref[...] + lb_ref[...]
                cs_ref[rows, :] = (ln * _sigmoid(ln)).astype(BF16)

        def step(r0):
            conv(r0)
            norm(r0)
        _row_loop(TE, CONV_ROWS, step)

    return _carried_call(
        body, carry, name="conf_fwd", grid=(nt,),
        in_specs=[pl.BlockSpec((TE, D), lambda i: (i, 2)), pl.BlockSpec((TE, D), lambda i: (i, 3)),
                  _resident((32, 8, D)), _resident((1, D)), _resident((1, D)), _resident((1, D))],
        out_specs=[pl.BlockSpec((TE, D), lambda i: (i, 0)), pl.BlockSpec((TE, D), lambda i: (i, 0))],
        out_shape=[SDS((t, D), F32), SDS((t, D), BF16)],
        scratch_shapes=[pltpu.VMEM((TE + HALO, D), F32)], compiler_params=_ARB1)(proj, proj, cw, cb, lg, lb)


def _mix_fwd(og, cs, proj, h0, wg, wc, wo):
    t = h0.shape[0]
    nt = t // TM

    def body(og_ref, cs_ref, g_ref, h0_ref, wg_ref, wc_ref, wo_ref, brg_ref, brc_ref, mg_ref, h1_ref):
        brg_ref[...] = _dot(og_ref[...], wg_ref[...]).astype(BF16)
        brc_ref[...] = _dot(cs_ref[...], wc_ref[...]).astype(BF16)

        def blk(r0):
            rows = pl.ds(r0, 32)
            gg = g_ref[rows, 0:D].astype(F32)
            gc = g_ref[rows, D:2 * D].astype(F32)
            m = _sigmoid(gg) * brg_ref[rows, :].astype(F32) + _sigmoid(gc) * brc_ref[rows, :].astype(F32)
            mg_ref[rows, :] = m.astype(BF16)
        _row_loop(TM, 32, blk)
        h1_ref[...] = h0_ref[...] + _dot(mg_ref[...], wo_ref[...])

    row = lambda w: pl.BlockSpec((TM, w), lambda i: (i, 0))
    return pl.pallas_call(
        body, name="mix_fwd", grid=(nt,),
        in_specs=[row(D), row(D), pl.BlockSpec((TM, 2 * D), lambda i: (i, 2)), row(D),
                  _once((D, D)), _once((D, D)), _once((D, D))],
        out_specs=[row(D), row(D), row(D), row(D)],
        out_shape=[SDS((t, D), BF16), SDS((t, D), BF16), SDS((t, D), BF16), SDS((t, D), F32)],
        compiler_params=_ARB1)(og, cs, proj, h0, wg, wc, wo)


def _ffn_out(up, fw, h1, wd, gf, target):
    t = h1.shape[0]
    nt = t // TE
    n_real = target.shape[0] + N_META

    def body(a_ref, bv_ref, fw_ref, h1_ref, wd_ref, gf_ref, tg_ref, tb_ref, f_ref, dh2_ref, red_ref, before, hs):
        i = pl.program_id(0)

        @pl.when(i == 0)
        def _():
            before[...] = jnp.zeros_like(before)
            red_ref[...] = jnp.zeros_like(red_ref)

        def conv(r0, win):
            ac = fw_ref[3] + fw_ref[0] * win[14:30, :] + fw_ref[1] * win[15:31, :] + fw_ref[2] * win[16:32, :]
            f_ref[pl.ds(r0, 16), :] = (ac * _sigmoid(ac) * bv_ref[pl.ds(r0, 16), :].astype(F32)).astype(BF16)

        conv(0, jnp.concatenate([before[...], a_ref[0:HALO_F, :].astype(F32)], axis=0))

        def conv_blk(k, c):
            r0 = pl.multiple_of(k * 16, 16)
            conv(r0, a_ref[pl.ds(r0 - HALO_F, 32), :].astype(F32))
            return c
        lax.fori_loop(1, TE // 16, conv_blk, 0)
        before[...] = a_ref[TE - HALO_F:TE, :].astype(F32)
        hs[...] = h1_ref[...] + _dot(f_ref[...], wd_ref[...])

        def head(r0, tg):
            rows = pl.ds(r0, 32)
            h2 = hs[rows, :]
            rinv = lax.rsqrt(jnp.mean(h2 * h2, axis=-1, keepdims=True) + RMS_EPS)
            hh = h2 * rinv
            gid = i * TE + r0 + lax.broadcasted_iota(jnp.int32, (32, 1), 0)
            live = jnp.logical_and(gid >= N_META, gid < n_real)
            err = jnp.where(live, hh * gf_ref[...] - tg, 0.0)
            dy = err * (1.0 / D)
            red_ref[0:8, :] += _rows8(err * err)
            red_ref[8:16, :] += _rows8(dy * hh)
            dhh = dy * gf_ref[...]
            dh2_ref[rows, :] = rinv * (dhh - hh * jnp.mean(dhh * hh, axis=-1, keepdims=True))

        head(0, jnp.concatenate([tb_ref[...], tg_ref[0:N_META, :]], axis=0))

        def blk(k, c):
            r0 = pl.multiple_of(k * 32, 32)
            head(r0, tg_ref[pl.ds(pl.multiple_of(r0 - N_META, N_META), 32), :])
            return c
        lax.fori_loop(1, TE // 32, blk, 0, unroll=9)

    row = lambda w: pl.BlockSpec((TE, w), lambda i: (i, 0))
    return pl.pallas_call(
        body, name="ffn_out", grid=(nt,),
        in_specs=[pl.BlockSpec((TE, DFF), lambda i: (i, 0)), pl.BlockSpec((TE, DFF), lambda i: (i, 1)),
                  _resident((4, 16, DFF)), row(D), _resident((DFF, D)), _resident((1, D)), row(D),
                  pl.BlockSpec((N_META, D), lambda i: (jnp.maximum(i * (TE // N_META) - 1, 0), 0))],
        out_specs=[row(DFF), row(D), _resident((16, D))],
        out_shape=[SDS((t, DFF), BF16), SDS((t, D), F32), SDS((16, D), F32)],
        scratch_shapes=[pltpu.VMEM((HALO_F, DFF), F32), pltpu.VMEM((TE, D), F32)],
        compiler_params=_ARB1)(up, up, fw, h1, wd, gf, target, target)


def _ffn_bwd(dh2, wd, up, fw):
    t = dh2.shape[0]
    nt = t // TE
    hb = TE // HALO_F

    def body(dh_ref, wd_ref, a_ref, ah_ref, bv_ref, fw_ref, dup_ref, dw_ref, dax, dfs):
        i = pl.program_id(0)
        ti = nt - 1 - i

        @pl.when(i == 0)
        def _():
            dax[TE:TE + HALO_F, :] = jnp.zeros((HALO_F, DFF), F32)
            dw_ref[...] = jnp.zeros_like(dw_ref)

        @pl.when(i > 0)
        def _():
            dax[TE:TE + HALO_F, :] = dax[0:HALO_F, :]

        dfs[...] = _dot_nt(dh_ref[...].astype(BF16), wd_ref[...])

        def act(r0, win):
            rows = pl.ds(r0, 16)
            ac = fw_ref[3] + fw_ref[0] * win[14:30, :] + fw_ref[1] * win[15:31, :] + fw_ref[2] * win[16:32, :]
            sg = _sigmoid(ac)
            df = dfs[rows, :]
            dup_ref[rows, DFF:2 * DFF] = (df * ac * sg).astype(BF16)
            dac = df * bv_ref[rows, :].astype(F32) * sg * (1.0 + ac * (1.0 - sg))
            dax[rows, :] = dac
            dw_ref[3] += _rows8(dac)
            for j in range(FFN_K):
                dw_ref[j] += _rows8(dac * win[14 + j:30 + j, :])

        before = jnp.where(ti > 0, ah_ref[...].astype(F32), 0.0)
        act(0, jnp.concatenate([before, a_ref[0:HALO_F, :].astype(F32)], axis=0))

        def act_blk(k, c):
            r0 = pl.multiple_of(k * 16, 16)
            act(r0, a_ref[pl.ds(r0 - HALO_F, 32), :].astype(F32))
            return c
        lax.fori_loop(1, TE // 16, act_blk, 0)

        def convt(r0):
            win = dax[pl.ds(r0, 32), :]
            da = fw_ref[2] * win[0:16, :] + fw_ref[1] * win[1:17, :] + fw_ref[0] * win[2:18, :]
            dup_ref[pl.ds(r0, 16), 0:DFF] = da.astype(BF16)
        _row_loop(TE, 16, convt)

    rev = lambda w: pl.BlockSpec((TE, w), lambda i: (nt - 1 - i, 0))
    return pl.pallas_call(
        body, name="ffn_bwd", grid=(nt,),
        in_specs=[rev(D), _resident((DFF, D)), rev(DFF),
                  pl.BlockSpec((HALO_F, DFF), lambda i: (jnp.maximum((nt - 1 - i) * hb - 1, 0), 0)),
                  pl.BlockSpec((TE, DFF), lambda i: (nt - 1 - i, 1)), _resident((4, 16, DFF))],
        out_specs=[rev(2 * DFF), _resident((4, 8, DFF))],
        out_shape=[SDS((t, 2 * DFF), BF16), SDS((4, 8, DFF), F32)],
        scratch_shapes=[pltpu.VMEM((TE + HALO_F, DFF), F32), pltpu.VMEM((TE, DFF), F32)],
        compiler_params=_ARB1)(dh2, wd, up, up, up, fw)


def _dgrad_norm(dy, w_t, h, g, dres, name, carry=()):
    t, k = dy.shape
    nt = t // TM

    def body(dy_ref, w_ref, h_ref, g_ref, dr_ref, dh_ref, dg_ref, acc):
        @pl.when(pl.program_id(0) == 0)
        def _():
            dg_ref[...] = jnp.zeros_like(dg_ref)

        acc[...] = _dot(dy_ref[...], w_ref[...])

        def blk(r0):
            rows = pl.ds(r0, 32)
            x = h_ref[rows, :]
            rinv = lax.rsqrt(jnp.mean(x * x, axis=-1, keepdims=True) + RMS_EPS)
            hh = x * rinv
            du = acc[rows, :]
            dg_ref[...] += _rows8(du * hh)
            dhh = du * g_ref[...]
            dh_ref[rows, :] = dr_ref[rows, :] + rinv * (dhh - hh * jnp.mean(dhh * hh, axis=-1, keepdims=True))
        _row_loop(TM, 32, blk, unroll=4)

    row = pl.BlockSpec((TM, D), lambda i: (i, 0))
    in_specs = [pl.BlockSpec((TM, k), lambda i: (i, 0)), _once((k, D)), row, _resident((1, D)), row]
    return _carried_call(
        body, carry, name=name, grid=(nt,), in_specs=in_specs, out_specs=[row, _resident((8, D))],
        out_shape=[SDS((t, D), F32), SDS((8, D), F32)],
        scratch_shapes=[pltpu.VMEM((TM, D), F32)], compiler_params=_ARB1)(dy, w_t, h, g, dres)


def _in_dgrad(dy, w_t, h, g, dres, dy_extra, w_extra_t, carry=()):
    t, k = dy.shape
    nt = t // TM

    def body(dy_ref, w_ref, h_ref, g_ref, dr_ref, de_ref, we_ref, dh_ref, dg_ref, acc):
        @pl.when(pl.program_id(0) == 0)
        def _():
            dg_ref[...] = jnp.zeros_like(dg_ref)

        acc[...] = _dot(dy_ref[...], w_ref[...])
        acc[...] += _dot(de_ref[...], we_ref[...])

        def blk(r0):
            rows = pl.ds(r0, 32)
            x = h_ref[rows, :]
            rinv = lax.rsqrt(jnp.mean(x * x, axis=-1, keepdims=True) + RMS_EPS)
            hh = x * rinv
            du = acc[rows, :]
            dg_ref[...] += _rows8(du * hh)
            dhh = du * g_ref[...]
            dh_ref[rows, :] = dr_ref[rows, :] + rinv * (dhh - hh * jnp.mean(dhh * hh, axis=-1, keepdims=True))
        _row_loop(TM, 32, blk, unroll=4)

    tile = lambda w: pl.BlockSpec((TM, w), lambda i: (i, 0))
    return _carried_call(
        body, carry, name="in_dgrad", grid=(nt,),
        in_specs=[tile(k), _once((k, D)), tile(D), _resident((1, D)), tile(D), tile(dy_extra.shape[1]),
                  _once(w_extra_t.shape)],
        out_specs=[tile(D), _resident((8, D))], out_shape=[SDS((t, D), F32), SDS((8, D), F32)],
        scratch_shapes=[pltpu.VMEM((TM, D), F32)], compiler_params=_ARB1)(dy, w_t, h, g, dres, dy_extra, w_extra_t)


def _wgrad(x, dy, tk, name):
    t, k = x.shape
    n = dy.shape[1]
    tm = _row_tile(t, TM_BIG)
    nk, nt = k // tk, t // tm

    def body(x_ref, dy_ref, o_ref, acc):
        @pl.when(pl.program_id(1) == 0)
        def _():
            acc[...] = jnp.zeros_like(acc)
        acc[...] += _dot_tn(x_ref[...], dy_ref[...].astype(BF16))

        @pl.when(pl.program_id(1) == nt - 1)
        def _():
            o_ref[...] = acc[...].astype(BF16)

    return pl.pallas_call(
        body, name=name, grid=(nk, nt),
        in_specs=[pl.BlockSpec((tm, tk), lambda j, i: (i, j)), pl.BlockSpec((tm, n), lambda j, i: (i, 0))],
        out_specs=pl.BlockSpec((tk, n), lambda j, i: (j, 0)), out_shape=SDS((k, n), BF16),
        scratch_shapes=[pltpu.VMEM((tk, n), F32)], compiler_params=_ARB2)(x, dy)


def _mix_bwd(dh1, wo, wg, wc, proj, brg, brc):
    t = dh1.shape[0]
    nt = t // TM

    def body(dh_ref, wo_ref, wg_ref, wc_ref, g_ref, brg_ref, brc_ref, dbg_ref, dbc_ref, dog_ref, dcs_ref, dp_ref, dm):
        dm[...] = _dot_nt(dh_ref[...].astype(BF16), wo_ref[...])

        def blk(r0):
            rows = pl.ds(r0, 32)
            d = dm[rows, :]
            sg = _sigmoid(g_ref[rows, 0:D].astype(F32))
            sc = _sigmoid(g_ref[rows, D:2 * D].astype(F32))
            dbg_ref[rows, :] = (d * sg).astype(BF16)
            dbc_ref[rows, :] = (d * sc).astype(BF16)
            dp_ref[rows, 0:D] = (d * brg_ref[rows, :].astype(F32) * sg * (1.0 - sg)).astype(BF16)
            dp_ref[rows, D:2 * D] = (d * brc_ref[rows, :].astype(F32) * sc * (1.0 - sc)).astype(BF16)
        _row_loop(TM, 32, blk)
        dog_ref[...] = _dot_nt(dbg_ref[...], wg_ref[...]).astype(BF16)
        dcs_ref[...] = _dot_nt(dbc_ref[...], wc_ref[...]).astype(BF16)

    row = pl.BlockSpec((TM, D), lambda i: (i, 0))
    wide = pl.BlockSpec((TM, 2 * D), lambda i: (i, 2))
    return pl.pallas_call(
        body, name="mix_bwd", grid=(nt,),
        in_specs=[row, _once((D, D)), _once((D, D)), _once((D, D)), wide, row, row],
        out_specs=[row, row, row, row, wide],
        out_shape=[SDS((t, D), BF16)] * 4 + [SDS((t, NPROJ), BF16)],
        scratch_shapes=[pltpu.VMEM((TM, D), F32)], compiler_params=_ARB1)(dh1, wo, wg, wc, proj, brg, brc)


def _glapost_bwd(dog, o, proj, gn, dproj):
    t = o.shape[0]
    nt = t // TE

    def body(dog_ref, o_ref, r_ref, gn_ref, dp_in, do_ref, dp_ref, dgn_ref):
        del dp_in

        @pl.when(pl.program_id(0) == 0)
        def _():
            dgn_ref[...] = jnp.zeros_like(dgn_ref)

        def blk(r0):
            rows = pl.ds(r0, 32)
            for h in range(HEADS):
                vs = slice(h * DVH, (h + 1) * DVH)
                x = o_ref[rows, vs]
                rinv = lax.rsqrt(jnp.mean(x * x, axis=-1, keepdims=True) + RMS_EPS)
                oh = x * rinv
                g = gn_ref[:, vs]
                rr = r_ref[rows, vs].astype(F32)
                sr = _sigmoid(rr)
                d = dog_ref[rows, vs].astype(F32)
                dp_ref[rows, vs] = (d * oh * g * sr * (1.0 + rr * (1.0 - sr))).astype(BF16)
                don = d * rr * sr
                dgn_ref[:, vs] += _rows8(don * oh)
                doh = don * g
                do_ref[rows, vs] = (rinv * (doh - oh * jnp.mean(doh * oh, axis=-1, keepdims=True))).astype(BF16)
        _row_loop(TE, 32, blk, unroll=5)

    row = pl.BlockSpec((TE, D), lambda i: (i, 0))
    rcol = pl.BlockSpec((TE, D), lambda i: (i, 6))
    return pl.pallas_call(
        body, name="glapost_bwd", grid=(nt,),
        in_specs=[row, row, rcol, _resident((1, D)), pl.BlockSpec(memory_space=pl.ANY)],
        out_specs=[row, rcol, _resident((8, D))],
        out_shape=[SDS((t, D), BF16), SDS((t, NPROJ), BF16), SDS((8, D), F32)],
        input_output_aliases={4: 1}, compiler_params=_ARB1)(dog, o, proj, gn, dproj)


def _conf_bwd(dcs, cc, proj, cw, lg, lb, dproj, carry=()):
    t = cc.shape[0]
    nt = t // TE
    hb = TE // HALO

    def body(dcs_ref, cc_ref, c1_ref, c2_ref, c1h_ref, c2h_ref, cw_ref, lg_ref, lb_ref, dp_in,
             dp_ref, dw_ref, ds_ref, cext, dext):
        del dp_in
        i = pl.program_id(0)
        ti = nt - 1 - i

        @pl.when(i == 0)
        def _():
            dext[TE:TE + HALO, :] = jnp.zeros((HALO, D), F32)
            dw_ref[...] = jnp.zeros_like(dw_ref)
            ds_ref[...] = jnp.zeros_like(ds_ref)

        @pl.when(i > 0)
        def _():
            dext[TE:TE + HALO, :] = dext[0:HALO, :]

        ch = c1h_ref[...].astype(F32) * _sigmoid(c2h_ref[...].astype(F32))
        cext[0:HALO, :] = jnp.where(ti > 0, ch, 0.0)

        def pre(r0):
            rows = pl.ds(r0, 32)
            cext[pl.ds(HALO + r0, 32), :] = c1_ref[rows, :].astype(F32) * _sigmoid(c2_ref[rows, :].astype(F32))
            x = cc_ref[rows, :]
            mu = jnp.mean(x, axis=-1, keepdims=True)
            xc = x - mu
            rstd = lax.rsqrt(jnp.mean(xc * xc, axis=-1, keepdims=True) + LN_EPS)
            xh = xc * rstd
            ln = xh * lg_ref[...] + lb_ref[...]
            sg = _sigmoid(ln)
            dln = dcs_ref[rows, :].astype(F32) * sg * (1.0 + ln * (1.0 - sg))
            ds_ref[0] += _rows8(dln * xh)
            ds_ref[1] += _rows8(dln)
            dxh = dln * lg_ref[...]
            dcc = rstd * (dxh - jnp.mean(dxh, axis=-1, keepdims=True) - xh * jnp.mean(dxh * xh, axis=-1, keepdims=True))
            dext[rows, :] = dcc
            ds_ref[2] += _rows8(dcc)
        _row_loop(TE, 32, pre, unroll=5)

        def convt(r0):
            rows = pl.ds(r0, CONV_ROWS)
            for part in range(D // CONV_LANES):
                cols = slice(part * CONV_LANES, (part + 1) * CONV_LANES)
                wd = dext[pl.ds(r0, CONV_ROWS + HALO), cols]
                dc = jnp.zeros((CONV_ROWS, CONV_LANES), F32)
                for s, taps in _shift_classes([(CONF_K - 1 - j, j) for j in range(CONF_K)]):
                    ws = _shifted(wd, s)
                    for a8, j in taps:
                        dc = dc + jnp.tile(cw_ref[j, :, cols], (CONV_ROWS // 8, 1)) * ws[a8:a8 + CONV_ROWS, :]
                dcc = wd[0:CONV_ROWS, :]
                wc = cext[pl.ds(r0, CONV_ROWS + HALO), cols]
                for s, taps in _shift_classes([(2 + j, j) for j in range(CONF_K)]):
                    ws = _shifted(wc, s)
                    for a8, j in taps:
                        dw_ref[j, :, cols] += _rows8(dcc * ws[a8:a8 + CONV_ROWS, :])
                c1 = c1_ref[rows, cols].astype(F32)
                s2 = _sigmoid(c2_ref[rows, cols].astype(F32))
                dp_ref[rows, cols] = (dc * s2).astype(BF16)
                dp_ref[rows, D + part * CONV_LANES:D + (part + 1) * CONV_LANES] = (dc * c1 * s2 * (1.0 - s2)).astype(BF16)
        _row_loop(TE, CONV_ROWS, convt)

    rev = lambda col: pl.BlockSpec((TE, D), lambda i: (nt - 1 - i, col))
    halo = lambda col: pl.BlockSpec((HALO, D), lambda i: (jnp.maximum((nt - 1 - i) * hb - 1, 0), col))
    return _carried_call(
        body, carry, name="conf_bwd", grid=(nt,),
        in_specs=[rev(0), rev(0), rev(2), rev(3), halo(2), halo(3), _resident((32, 8, D)), _resident((1, D)),
                  _resident((1, D)), pl.BlockSpec(memory_space=pl.ANY)],
        out_specs=[pl.BlockSpec((TE, 2 * D), lambda i: (nt - 1 - i, 1)), _resident((32, 8, D)), _resident((3, 8, D))],
        out_shape=[SDS((t, NPROJ), BF16), SDS((32, 8, D), F32), SDS((3, 8, D), F32)],
        scratch_shapes=[pltpu.VMEM((TE + HALO, D), F32), pltpu.VMEM((TE + HALO, D), F32)],
        input_output_aliases={9: 0}, compiler_params=_ARB1)(dcs, cc, proj, proj, proj, proj, cw, lg, lb, dproj)


def _gla_bwd(proj, alr, wau, balpha, do, sall, dproj, carry=()):
    t = proj.shape[0]
    nc = t // CH

    def body(qk_ref, v_ref, a_ref, wau_ref, ba_ref, do_ref, s_ref, dp_in, dp_ref, da_ref, dwau_ref, dba_ref, ds_scr, dla_scr):
        del dp_in

        @pl.when(pl.program_id(0) == 0)
        def _():
            ds_scr[...] = jnp.zeros_like(ds_scr)
            dwau_ref[...] = jnp.zeros_like(dwau_ref)
            dba_ref[...] = jnp.zeros_like(dba_ref)

        z, b, bmid, blast, causal = _gla_decay(a_ref, wau_ref, ba_ref)
        dlasts = []
        for h in range(HEADS):
            ks = slice(h * DKH, (h + 1) * DKH)
            vs = slice(h * DVH, (h + 1) * DVH)
            bh, mh, lh = b[:, ks], bmid[:, ks], blast[:, ks]
            q = qk_ref[:, ks].astype(F32) * (DKH ** -0.5)
            k = qk_ref[:, DK + h * DKH:DK + (h + 1) * DKH].astype(F32)
            v = v_ref[:, vs]
            dout = do_ref[:, vs]
            eq, ek, eb, eg, el = jnp.exp(bh - mh), jnp.exp(mh - bh), jnp.exp(bh), jnp.exp(lh - bh), jnp.exp(lh)
            qt, kt = (q * eq).astype(BF16), (k * ek).astype(BF16)
            qg, kg = (q * eb).astype(BF16), (k * eg).astype(BF16)
            st = s_ref[0, vs, :]
            dsn = ds_scr[vs, :]
            st16, dsn16 = st.astype(BF16), dsn.astype(BF16)
            a = jnp.where(causal, _dot_nt(qt, kt), 0.0).astype(BF16)
            da = jnp.where(causal, _dot_nt(dout, v), 0.0).astype(BF16)
            dq_inter = _dot(dout, st16) * eb
            dk_inter = _dot(v, dsn16) * eg
            dq = _dot(da, kt) * eq + dq_inter
            dk = _dot_tn(da, qt) * ek + dk_inter
            dv = _dot_tn(a, dout) + _dot_nt(kg, dsn16)
            dlasts.append(jnp.sum(k * dk_inter, axis=0, keepdims=True) + jnp.sum(st * dsn, axis=0, keepdims=True) * el[0:1, :])
            dla_scr[:, ks] = q * dq - k * dk
            ds_scr[vs, :] = dsn * jnp.concatenate([el, el], axis=0) + _dot_tn(dout, qg)
            dp_ref[:, ks] = (dq * (DKH ** -0.5)).astype(BF16)
            dp_ref[:, DK + h * DKH:DK + (h + 1) * DKH] = dk.astype(BF16)
            dp_ref[:, D + h * DVH:D + (h + 1) * DVH] = dv.astype(BF16)
        r = lax.broadcasted_iota(jnp.int32, (CH, CH), 0)
        c = lax.broadcasted_iota(jnp.int32, (CH, CH), 1)
        dla = _tri_matmul((r <= c).astype(BF16), dla_scr[...]) + jnp.concatenate(dlasts, axis=1)
        dz = (dla * (1.0 / TAU) * _sigmoid(-z)).astype(BF16)
        da_ref[...] = _dot_nt(dz, wau_ref[...]).astype(BF16)
        dwau_ref[...] += _dot_tn(a_ref[...], dz)
        dba_ref[...] += _rows8(dz.astype(F32))

    rev = lambda w, col: pl.BlockSpec((CH, w), lambda c: (nc - 1 - c, col))
    return _carried_call(
        body, carry, name="gla_bwd", grid=(nc,),
        in_specs=[rev(D, 0), rev(D, 1), rev(LANES, 0), _resident((LANES, DK)), _resident((1, DK)), rev(D, 0),
                  pl.BlockSpec((1, DV, DKH), lambda c: (nc - 1 - c, 0, 0)), pl.BlockSpec(memory_space=pl.ANY)],
        out_specs=[rev(2 * D, 0), rev(LANES, 0), _resident((LANES, DK)), _resident((8, DK))],
        out_shape=[SDS((t, NPROJ), BF16), SDS((t, LANES), BF16), SDS((LANES, DK), F32), SDS((8, DK), F32)],
        scratch_shapes=[pltpu.VMEM((DV, DKH), F32), pltpu.VMEM((CH, DK), F32)],
        input_output_aliases={7: 0}, compiler_params=_ARB1)(proj, proj, alr, wau, balpha, do, sall, dproj)


def _all_gather(xs, name):
    n = len(xs)

    def body(*refs):
        x_refs, out_refs = refs[:n], refs[n:2 * n]
        send_sems, recv_sems = refs[2 * n:]
        x, y, c = _my_place()
        me, sibling = (x, y, c), (x, y, 1 - c)
        x_nbr, y_nbr, diagonal = (1 - x, y), (x, 1 - y), (1 - x, 1 - y)

        def slot(p, px, py, pc):
            return out_refs[p].at[4 * px + 2 * py + pc]

        def copy(p, k, block, to, src=None):
            return pltpu.make_async_remote_copy(
                src_ref=slot(p, *block) if src is None else src, dst_ref=slot(p, *block),
                send_sem=send_sems.at[7 * p + k], recv_sem=recv_sems.at[7 * p + k], device_id=to, device_id_type=MESH_T)

        for p in range(n):
            for k, to in enumerate((sibling, (*x_nbr, c), (*y_nbr, c))):
                copy(p, k, me, to, src=x_refs[p]).start()
        for p in range(n):
            @pl.when(c == 1)
            def _():
                copy(p, 2, (*y_nbr, c), me).wait_recv()
                copy(p, 3, (*y_nbr, c), (*x_nbr, c)).start()
                copy(p, 1, (*x_nbr, c), me).wait_recv()

            @pl.when(c == 0)
            def _():
                copy(p, 1, (*x_nbr, c), me).wait_recv()
                copy(p, 3, (*x_nbr, c), (*y_nbr, c)).start()
                copy(p, 2, (*y_nbr, c), me).wait_recv()
        for p in range(n):
            copy(p, 4, (*x_nbr, c), sibling).start()
            copy(p, 5, (*y_nbr, c), sibling).start()
        for p in range(n):
            copy(p, 3, (*diagonal, c), me).wait_recv()
            copy(p, 6, (*diagonal, c), sibling).start()
        for p in range(n):
            copy(p, 0, sibling, me).wait_recv()
            for j, chip in enumerate((x_nbr, y_nbr, diagonal)):
                copy(p, 4 + j, (*chip, 1 - c), me).wait_recv()
        for p in range(n):
            for k in range(7):
                copy(p, k, me, sibling, src=x_refs[p]).wait_send()

    hbm = pl.BlockSpec(memory_space=pl.ANY)
    return pl.pallas_call(
        body, name=name, out_shape=[SDS((N_DEV, *a.shape), a.dtype) for a in xs],
        in_specs=[hbm] * n, out_specs=[hbm] * n,
        scratch_shapes=[pltpu.SemaphoreType.DMA((7 * n,)), pltpu.SemaphoreType.DMA((7 * n,))])(*xs)


def _exchange(gs):
    n = len(gs)

    def body(*refs):
        g_refs, land_refs = refs[:n], refs[n:2 * n]
        send_sems, recv_sems, local_sems = refs[2 * n:]
        x, y, c = _my_place()
        my_idx = 4 * x + 2 * y + c
        mine = [pltpu.make_async_copy(g_refs[p].at[my_idx], land_refs[p].at[my_idx], local_sems.at[p]) for p in range(n)]
        for cp in mine:
            cp.start()
        copies = []
        for k in range(1, N_DEV):
            px, py, pc = _flip(x, k & 4), _flip(y, k & 2), _flip(c, k & 1)
            p_idx = 4 * px + 2 * py + pc
            for p in range(n):
                s = 7 * p + k - 1
                cp = pltpu.make_async_remote_copy(
                    src_ref=g_refs[p].at[p_idx], dst_ref=land_refs[p].at[my_idx], send_sem=send_sems.at[s],
                    recv_sem=recv_sems.at[s], device_id=(px, py, pc), device_id_type=MESH_T)
                cp.start()
                arrival = pltpu.make_async_remote_copy(
                    src_ref=g_refs[p].at[p_idx], dst_ref=land_refs[p].at[p_idx], send_sem=send_sems.at[s],
                    recv_sem=recv_sems.at[s], device_id=(px, py, pc), device_id_type=MESH_T)
                copies.append((cp, arrival))
        for cp, arrival in copies:
            arrival.wait_recv()
        for cp, arrival in copies:
            cp.wait_send()
        for cp in mine:
            cp.wait()

    hbm = pl.BlockSpec(memory_space=pl.ANY)
    return pl.pallas_call(
        body, name="grad_exchange", out_shape=[SDS(g.shape, g.dtype) for g in gs],
        in_specs=[hbm] * n, out_specs=[hbm] * n,
        scratch_shapes=[pltpu.SemaphoreType.DMA((7 * n,)), pltpu.SemaphoreType.DMA((7 * n,)),
                        pltpu.SemaphoreType.DMA((n,))])(*gs)


def _adamw(land, w, m, v, rows_blk, name):
    rows = w.shape[0]

    def body(l_ref, w_ref, m_ref, v_ref, g_ref, d_ref, nm_ref, nv_ref):
        g = l_ref[0].astype(F32)
        for s in range(1, N_DEV):
            g = g + l_ref[s].astype(F32)
        nm = ADAM_B1 * m_ref[...] + (1.0 - ADAM_B1) * g
        nv = ADAM_B2 * v_ref[...] + (1.0 - ADAM_B2) * (g * g)
        m_hat = nm / (1.0 - ADAM_B1 ** ADAM_STEP)
        v_hat = nv / (1.0 - ADAM_B2 ** ADAM_STEP)
        g_ref[...] = g
        d_ref[...] = -ADAM_LR * (m_hat / (jnp.sqrt(v_hat) + ADAM_EPS) + ADAM_WD * w_ref[...])
        nm_ref[...] = nm
        nv_ref[...] = nv

    blk = pl.BlockSpec((rows_blk, D), lambda i: (i, 0))
    return pl.pallas_call(
        body, name=name, grid=(rows // rows_blk,),
        in_specs=[pl.BlockSpec((N_DEV, rows_blk, D), lambda i: (0, i, 0)), blk, blk, blk],
        out_specs=[blk] * 4, out_shape=[SDS((rows, D), F32)] * 4, compiler_params=_ARB1)(land, w, m, v)


BIG = ("w_in", "w_up", "w_down", "w_gla_o", "w_conf_o", "w_out")
BIG_TRANSPOSED = ("w_in", "w_up")
SMALL_SHARDED = ("meta_tokens", "conf_dw_w", "ffn_dw_w", "w_alpha_up")
REPLICATED = ("norm_mix_g", "b_alpha", "gla_norm_g", "conf_dw_b", "conf_ln_g", "conf_ln_b", "norm_ffn_g", "ffn_dw_b",
              "final_norm_g")
N_IN = sum(IN_WIDTHS)
W_IN_ROWS = N_IN // N_DEV
W_IN_PAD = -(-W_IN_ROWS // 16) * 16
ADAM_BLOCK = {"w_in": W_IN_PAD // 3, "w_up": 176, "w_down": 176, "w_gla_o": 128, "w_conf_o": 128, "w_out": 128}
SMALL_ROWS = 32


def _to_panel(name, shard):
    a = shard.reshape(shard.shape[-2], shard.shape[-1])
    if name in BIG_TRANSPOSED:
        a = a.T
    if name == "w_in":
        a = jnp.pad(a, ((0, W_IN_PAD - W_IN_ROWS), (0, 0)))
    return a


def _from_panel(name, panel, shape):
    a = panel[0:W_IN_ROWS] if name == "w_in" else panel
    if name in BIG_TRANSPOSED:
        a = a.T
    return a.reshape(shape)


def _pack_small(arrs):
    flat = jnp.concatenate([jnp.pad(a.reshape(-1), (0, (-a.size) % D)) for a in arrs])
    return jnp.pad(flat, (0, SMALL_ROWS * D - flat.shape[0])).reshape(SMALL_ROWS, D)


def _unpack_small(panel, shapes):
    flat, out, off = panel.reshape(-1), [], 0
    for shp in shapes:
        n = 1
        for s in shp:
            n *= s
        out.append(flat[off:off + n].reshape(shp))
        off += n + (-n) % D
    return out


def _local_step(x, target, w, shards=None):
    dist = shards is not None
    w = dict(w)

    def gather(names):
        return [(shards[n], False) for n in names] if dist else []

    def scatter(*arrs):
        return [(a.reshape(N_DEV, -1, D), True) for a in arrs] if dist else []

    s = x.shape[0]
    n_real = s + N_META
    t = -(-n_real // TM) * TM

    q0, r0, a0, c0 = 0, 2 * DK + DV, 2 * DK + 2 * DV, 2 * DK + 2 * DV + RANK
    wt = w["w_in_t"]
    w_main = jnp.concatenate([wt[q0:r0], wt[c0:N_IN], wt[r0:a0]], axis=0)
    w_a = jnp.pad(wt[a0:c0], ((0, LANES - RANK), (0, 0)))
    wau = jnp.pad(w["w_alpha_up"].astype(BF16), ((0, LANES - RANK), (0, 0)))
    row = lambda name: w[name].reshape(1, -1)
    cw = jnp.broadcast_to(jnp.pad(w["conf_dw_w"], ((0, 32 - CONF_K), (0, 0)))[:, None, :], (32, 8, D))
    fw = jnp.broadcast_to(jnp.concatenate([w["ffn_dw_w"], w["ffn_dw_b"].reshape(1, -1)], axis=0)[:, None, :],
                          (FFN_K + 1, 16, DFF))

    early = ("w_gla_o", "w_conf_o", "w_out", "w_up")
    h0, u1, proj, alr, *landed = _in_proj(x, w["meta_tokens"], row("norm_mix_g"), w_main, w_a, t, carry=gather(early))
    for n, land in zip(early, landed):
        w["w_up_t" if n == "w_up" else n] = land.reshape(-1, D)
    o, og, sall = _gla_fwd(proj, alr, wau, row("b_alpha"), row("gla_norm_g"))
    cc, cs, *landed = _conf_fwd(proj, cw, row("conf_dw_b"), row("conf_ln_g"), row("conf_ln_b"), carry=gather(("w_down",)))
    if dist:
        w["w_down"] = landed[0].reshape(-1, D)
    brg, brc, merged, h1 = _mix_fwd(og, cs, proj, h0, w["w_gla_o"], w["w_conf_o"], w["w_out"])
    u2, up = _norm_matmul(h1, row("norm_ffn_g"), w["w_up_t"], 512, "up_proj")
    f, dh2, red = _ffn_out(up, fw, h1, w["w_down"], row("final_norm_g"), target)
    loss = 0.5 / D * jnp.sum(red[0:8])

    g = {"final_norm_g": jnp.sum(red[8:16], axis=0)}
    dup, dfw = _ffn_bwd(dh2, w["w_down"], up, fw)
    g["ffn_dw_w"] = jnp.sum(dfw[0:FFN_K], axis=1)
    g["ffn_dw_b"] = jnp.sum(dfw[3], axis=0)
    g["w_down"] = _wgrad(f, dh2, 1408, "wgrad_down")
    dh1, dg2, *landed = _dgrad_norm(dup, w["w_up_t"], h1, row("norm_ffn_g"), dh2, "up_dgrad", carry=scatter(g["w_down"]))
    if dist:
        g["w_down"] = landed[0]
    g["norm_ffn_g"] = jnp.sum(dg2, axis=0)
    g["w_up_t"] = _wgrad(dup, u2, 1408, "wgrad_up")
    dbrg, dbrc, dog, dcs, dproj = _mix_bwd(dh1, w["w_out"], w["w_gla_o"], w["w_conf_o"], proj, brg, brc)
    g["w_out"] = _wgrad(merged, dh1, 1024, "wgrad_out")
    g["w_gla_o"] = _wgrad(og, dbrg, 1024, "wgrad_gla_o")
    g["w_conf_o"] = _wgrad(cs, dbrc, 1024, "wgrad_conf_o")
    do, dproj, dgn = _glapost_bwd(dog, o, proj, row("gla_norm_g"), dproj)
    g["gla_norm_g"] = jnp.sum(dgn, axis=0)
    dproj, dcw, dst, *landed = _conf_bwd(dcs, cc, proj, cw, row("conf_ln_g"), row("conf_ln_b"), dproj,
                                         carry=scatter(g["w_up_t"]))
    if dist:
        g["w_up_t"] = landed[0]
    g["conf_dw_w"] = jnp.sum(dcw[0:CONF_K], axis=1)
    g["conf_ln_g"], g["conf_ln_b"], g["conf_dw_b"] = jnp.sum(dst[0], axis=0), jnp.sum(dst[1], axis=0), jnp.sum(dst[2], axis=0)
    dproj, dalr, dwau, dba, *landed = _gla_bwd(proj, alr, wau, row("b_alpha"), do, sall, dproj,
                                               carry=scatter(g["w_out"], g["w_gla_o"], g["w_conf_o"]))
    if dist:
        g["w_out"], g["w_gla_o"], g["w_conf_o"] = landed
    g["w_alpha_up"] = dwau[0:RANK]
    g["b_alpha"] = jnp.sum(dba, axis=0)
    dw_main = _wgrad(dproj, u1, 1024, "wgrad_in")
    dw_a = _wgrad(dalr, u1, LANES, "wgrad_alr")
    g["w_in_t"] = jnp.concatenate([dw_main[0:r0], dw_main[NPROJ - DV:NPROJ], dw_a[0:RANK], dw_main[r0:NPROJ - DV]], axis=0)
    w_in_blocks = []
    if dist:
        pad = ((0, 0), (0, W_IN_PAD - W_IN_ROWS), (0, 0))
        w_in_blocks = [(jnp.pad(g["w_in_t"].reshape(N_DEV, W_IN_ROWS, D), pad), True)]
    dh0, dg1, *landed = _in_dgrad(dproj, w_main, h0, row("norm_mix_g"), dh1, dalr, w_a, carry=w_in_blocks)
    if dist:
        g["w_in_t"] = landed[0]
    g["norm_mix_g"] = jnp.sum(dg1, axis=0)
    g["meta_tokens"] = dh0[0:N_META]
    return loss, dh0[N_META:n_real], g


def kernel(x, meta_tokens, norm_mix_g, w_in, w_alpha_up, b_alpha, gla_norm_g, w_gla_o, conf_dw_w, conf_dw_b, conf_ln_g, conf_ln_b, w_conf_o, w_out, norm_ffn_g, w_up, ffn_dw_w, ffn_dw_b, w_down, final_norm_g, loss_target, m_meta_tokens, m_norm_mix_g, m_w_in, m_w_alpha_up, m_b_alpha, m_gla_norm_g, m_w_gla_o, m_conf_dw_w, m_conf_dw_b, m_conf_ln_g, m_conf_ln_b, m_w_conf_o, m_w_out, m_norm_ffn_g, m_w_up, m_ffn_dw_w, m_ffn_dw_b, m_w_down, m_final_norm_g, v_meta_tokens, v_norm_mix_g, v_w_in, v_w_alpha_up, v_b_alpha, v_gla_norm_g, v_w_gla_o, v_conf_dw_w, v_conf_dw_b, v_conf_ln_g, v_conf_ln_b, v_w_conf_o, v_w_out, v_norm_ffn_g, v_w_up, v_ffn_dw_w, v_ffn_dw_b, v_w_down, v_final_norm_g):
    ws = dict(meta_tokens=meta_tokens, norm_mix_g=norm_mix_g, w_in=w_in, w_alpha_up=w_alpha_up, b_alpha=b_alpha,
              gla_norm_g=gla_norm_g, w_gla_o=w_gla_o, conf_dw_w=conf_dw_w, conf_dw_b=conf_dw_b, conf_ln_g=conf_ln_g,
              conf_ln_b=conf_ln_b, w_conf_o=w_conf_o, w_out=w_out, norm_ffn_g=norm_ffn_g, w_up=w_up, ffn_dw_w=ffn_dw_w,
              ffn_dw_b=ffn_dw_b, w_down=w_down, final_norm_g=final_norm_g)
    ms = dict(meta_tokens=m_meta_tokens, norm_mix_g=m_norm_mix_g, w_in=m_w_in, w_alpha_up=m_w_alpha_up, b_alpha=m_b_alpha,
              gla_norm_g=m_gla_norm_g, w_gla_o=m_w_gla_o, conf_dw_w=m_conf_dw_w, conf_dw_b=m_conf_dw_b,
              conf_ln_g=m_conf_ln_g, conf_ln_b=m_conf_ln_b, w_conf_o=m_w_conf_o, w_out=m_w_out, norm_ffn_g=m_norm_ffn_g,
              w_up=m_w_up, ffn_dw_w=m_ffn_dw_w, ffn_dw_b=m_ffn_dw_b, w_down=m_w_down, final_norm_g=m_final_norm_g)
    vs = dict(meta_tokens=v_meta_tokens, norm_mix_g=v_norm_mix_g, w_in=v_w_in, w_alpha_up=v_w_alpha_up, b_alpha=v_b_alpha,
              gla_norm_g=v_gla_norm_g, w_gla_o=v_w_gla_o, conf_dw_w=v_conf_dw_w, conf_dw_b=v_conf_dw_b,
              conf_ln_g=v_conf_ln_g, conf_ln_b=v_conf_ln_b, w_conf_o=v_w_conf_o, w_out=v_w_out, norm_ffn_g=v_norm_ffn_g,
              w_up=v_w_up, ffn_dw_w=v_ffn_dw_w, ffn_dw_b=v_ffn_dw_b, w_down=v_w_down, final_norm_g=v_final_norm_g)
    small = SMALL_SHARDED + REPLICATED
    pack_small = lambda d: _pack_small([d[n] for n in small])

    shards = {n: _to_panel(n, ws[n]).astype(BF16) for n in BIG}
    own = [shards["w_in"], pack_small(ws)]
    my_idx = 4 * lax.axis_index("x") + 2 * lax.axis_index("y") + lax.axis_index("c")
    gathered = [lax.dynamic_update_slice(full_, mine[None], (my_idx, 0, 0))
                for full_, mine in zip(_all_gather(own, "weight_gather"), own)]
    full = {n: ws[n].reshape(-1) for n in REPLICATED}
    full["w_in_t"] = gathered[0][:, 0:W_IN_ROWS].reshape(N_IN, D)
    flat, off = gathered[1].reshape(N_DEV, -1), 0
    for n in SMALL_SHARDED:
        k, c = ws[n].shape[-2], ws[n].shape[-1]
        full[n] = flat[:, off:off + k * c].reshape(N_DEV, k, c).transpose(1, 0, 2).reshape(k, N_DEV * c)
        off += k * c + (-(k * c)) % D

    loss, grad_x, g = _local_step(x[0], loss_target[0], full, shards)

    lands = [g["w_in_t"], g["w_up_t"]] + [g[n] for n in BIG[2:]]
    blocks = []
    for n in SMALL_SHARDED:
        k, c = ws[n].shape[-2], ws[n].shape[-1]
        b = g[n].reshape(k, N_DEV, c).transpose(1, 0, 2).reshape(N_DEV, k * c)
        blocks.append(jnp.pad(b, ((0, 0), (0, (-(k * c)) % D))))
    for n in REPLICATED:
        b = jnp.broadcast_to(g[n].reshape(1, -1), (N_DEV, g[n].size))
        blocks.append(jnp.pad(b, ((0, 0), (0, (-b.shape[1]) % D))))
    gsm = jnp.concatenate(blocks, axis=1)
    lands += _exchange([jnp.pad(gsm, ((0, 0), (0, SMALL_ROWS * D - gsm.shape[1]))).reshape(N_DEV, SMALL_ROWS, D)])

    grad, delta, new_m, new_v = {}, {}, {}, {}
    for i, n in enumerate(BIG):
        outs = _adamw(lands[i], _to_panel(n, ws[n]), _to_panel(n, ms[n]), _to_panel(n, vs[n]), ADAM_BLOCK[n], "adamw_" + n)
        grad[n], delta[n], new_m[n], new_v[n] = [_from_panel(n, p, ws[n].shape) for p in outs]
    outs = _adamw(lands[len(BIG)], pack_small(ws), pack_small(ms), pack_small(vs), SMALL_ROWS, "adamw_small")
    shapes = [ws[n].shape for n in small]
    for d, p in zip((grad, delta, new_m, new_v), outs):
        d.update(zip(small, _unpack_small(p, shapes)))

    order = ("meta_tokens", "norm_mix_g", "w_in", "w_alpha_up", "b_alpha", "gla_norm_g", "w_gla_o", "conf_dw_w", "conf_dw_b",
             "conf_ln_g", "conf_ln_b", "w_conf_o", "w_out", "norm_ffn_g", "w_up", "ffn_dw_w", "ffn_dw_b", "w_down",
             "final_norm_g")
    loss = lax.psum(loss, ("x", "y", "c"))
    return (loss, grad_x[None], *[grad[n] for n in order], *[delta[n] for n in order], *[new_m[n] for n in order],
            *[new_v[n] for n in order])
```

```python
import functools

import jax
import jax.numpy as jnp
from jax import lax
from jax.experimental import pallas as pl
from jax.experimental.pallas import tpu as pltpu

F32, BF16 = jnp.float32, jnp.bfloat16
SDS = jax.ShapeDtypeStruct

D = 1024
N_META = 16
HEADS = 4
DK, DKH, DV, DVH = 512, 128, 1024, 256
RANK = 16
TAU = 16.0
CONF_K = 31
DFF = 2816
FFN_K = 3
IN_WIDTHS = (DK, DK, DV, DV, RANK, 2 * D, D, D)
RMS_EPS, LN_EPS = 1e-6, 1e-5
ADAM_LR, ADAM_B1, ADAM_B2, ADAM_EPS, ADAM_WD, ADAM_STEP = 0.001, 0.9, 0.999, 1e-08, 0.01, 10

NPROJ = 7 * D
LANES = 128
CH = 128
TM = 640
TM_BIG = 1664
TE = 320
HALO = 32
HALO_F = 16
N_DEV = 8
VMEM_LIMIT = 60 * 1024 * 1024
MESH_T = pl.DeviceIdType.MESH

_ARB1 = pltpu.CompilerParams(dimension_semantics=("arbitrary",), vmem_limit_bytes=VMEM_LIMIT)
_ARB2 = pltpu.CompilerParams(dimension_semantics=("arbitrary", "arbitrary"), vmem_limit_bytes=VMEM_LIMIT)


def _dot(a, b):
    return jnp.dot(a, b, preferred_element_type=F32)


def _dot_nt(a, b):
    return lax.dot_general(a, b, (((1,), (1,)), ((), ())), preferred_element_type=F32)


def _dot_tn(a, b):
    return lax.dot_general(a, b, (((0,), (0,)), ((), ())), preferred_element_type=F32)


def _sigmoid(x):
    return 0.5 * jnp.tanh(0.5 * x) + 0.5


def _rows8(x):
    return x.reshape(x.shape[0] // 8, 8, x.shape[1]).sum(axis=0)


def _row_tile(t, preferred):
    return preferred if t % preferred == 0 else TM


def _row_loop(n_rows, rb, fn, unroll=1):
    def step(i, carry):
        fn(pl.multiple_of(i * rb, rb))
        return carry
    lax.fori_loop(0, n_rows // rb, step, 0, unroll=unroll)


def _resident(shape):
    return pl.BlockSpec(shape, lambda *_: (0,) * len(shape))


def _once(shape):
    return pl.BlockSpec(shape, lambda *_: (0,) * len(shape), pipeline_mode=pl.Buffered(1))


CONV_ROWS, CONV_LANES = 64, 256


def _shift_classes(offset_taps):
    return [(s, [(o - s, j) for o, j in offset_taps if o % 8 == s]) for s in range(8)]


def _shifted(win, s):
    return win if s == 0 else win[s:s + CONV_ROWS + HALO - 8, :]


def _my_place():
    return lax.axis_index("x"), lax.axis_index("y"), lax.axis_index("c")


def _flip(v, bit):
    return 1 - v if bit else v


def _exchange_copies(src_refs, land_refs, scatter, send_sems, recv_sems, local_sems, arrivals):
    x, y, c = _my_place()
    my_idx = 4 * x + 2 * y + c
    local, remote = [], []
    for p, (src, land) in enumerate(zip(src_refs, land_refs)):
        local.append(pltpu.make_async_copy(src.at[my_idx] if scatter[p] else src, land.at[my_idx], local_sems.at[p]))
    for k in range(1, N_DEV):
        px, py, pc = _flip(x, k & 4), _flip(y, k & 2), _flip(c, k & 1)
        p_idx = 4 * px + 2 * py + pc
        for p, (src, land) in enumerate(zip(src_refs, land_refs)):
            s = 7 * p + k - 1
            out = src.at[p_idx] if scatter[p] else src

            def copy(dst):
                return pltpu.make_async_remote_copy(src_ref=out, dst_ref=dst, send_sem=send_sems.at[s],
                                                    recv_sem=recv_sems.at[s], device_id=(px, py, pc), device_id_type=MESH_T)
            remote.append((copy(land.at[my_idx]), copy(land.at[p_idx]) if arrivals else None))
    return local, remote


def _carried_call(core, carry, *, grid, in_specs, out_specs, out_shape, scratch_shapes=(), **kw):
    n_in, n_out, nc, n_scr = len(in_specs), len(out_specs), len(carry), len(scratch_shapes)
    if nc == 0:
        return pl.pallas_call(core, grid=grid, in_specs=in_specs, out_specs=out_specs, out_shape=out_shape,
                              scratch_shapes=list(scratch_shapes), **kw)
    scatter = [sc for _, sc in carry]

    def body(*refs):
        ins, cin = refs[:n_in], refs[n_in:n_in + nc]
        outs, cout = refs[n_in + nc:n_in + nc + n_out], refs[n_in + nc + n_out:n_in + 2 * nc + n_out]
        scr, sems = refs[n_in + 2 * nc + n_out:n_in + 2 * nc + n_out + n_scr], refs[-3:]
        first = functools.reduce(jnp.logical_and, [pl.program_id(a) == 0 for a in range(len(grid))])
        last = functools.reduce(jnp.logical_and, [pl.program_id(a) == grid[a] - 1 for a in range(len(grid))])

        @pl.when(first)
        def _():
            local, remote = _exchange_copies(cin, cout, scatter, *sems, arrivals=False)
            for cp in local:
                cp.start()
            for send, _ in remote:
                send.start()

        core(*ins, *outs, *scr)

        @pl.when(last)
        def _():
            local, remote = _exchange_copies(cin, cout, scatter, *sems, arrivals=True)
            for _, arrival in remote:
                arrival.wait_recv()
            for send, _ in remote:
                send.wait_send()
            for cp in local:
                cp.wait()

    hbm = pl.BlockSpec(memory_space=pl.ANY)
    land_shape = [SDS((N_DEV, *(a.shape[1:] if sc else a.shape)), a.dtype) for a, sc in carry]
    sems = [pltpu.SemaphoreType.DMA((7 * nc,)), pltpu.SemaphoreType.DMA((7 * nc,)), pltpu.SemaphoreType.DMA((nc,))]
    call = pl.pallas_call(body, grid=grid, in_specs=list(in_specs) + [hbm] * nc, out_specs=list(out_specs) + [hbm] * nc,
                          out_shape=list(out_shape) + land_shape, scratch_shapes=list(scratch_shapes) + sems, **kw)
    return lambda *args: call(*args, *[a for a, _ in carry])


def _norm_matmul(h, g, w_t, tn, name, w_extra_t=None, carry=()):
    t, n = h.shape[0], w_t.shape[0]
    tm = _row_tile(t, TM_BIG)
    nt, nb = t // tm, n // tn

    def body(*refs):
        if w_extra_t is None:
            h_ref, g_ref, w_ref, u_ref, p_ref = refs
        else:
            h_ref, g_ref, w_ref, we_ref, u_ref, p_ref, e_ref = refs

        @pl.when(pl.program_id(1) == 0)
        def _():
            def blk(r0):
                x = h_ref[pl.ds(r0, 32), :]
                rinv = lax.rsqrt(jnp.mean(x * x, axis=-1, keepdims=True) + RMS_EPS)
                u_ref[pl.ds(r0, 32), :] = (x * rinv * g_ref[...]).astype(BF16)
            _row_loop(tm, 32, blk, unroll=13)
            if w_extra_t is not None:
                e_ref[...] = _dot_nt(u_ref[...], we_ref[...]).astype(BF16)

        p_ref[...] = _dot_nt(u_ref[...], w_ref[...]).astype(BF16)

    in_specs = [pl.BlockSpec((tm, D), lambda i, j: (i, 0)), _resident((1, D)), pl.BlockSpec((tn, D), lambda i, j: (j, 0))]
    out_specs = [pl.BlockSpec((tm, D), lambda i, j: (i, 0)), pl.BlockSpec((tm, tn), lambda i, j: (i, j))]
    out_shape = [SDS((t, D), BF16), SDS((t, n), BF16)]
    args = [h, g, w_t]
    if w_extra_t is not None:
        in_specs.append(_resident(w_extra_t.shape))
        out_specs.append(pl.BlockSpec((tm, w_extra_t.shape[0]), lambda i, j: (i, 0)))
        out_shape.append(SDS((t, w_extra_t.shape[0]), BF16))
        args.append(w_extra_t)
    return _carried_call(body, carry, name=name, grid=(nt, nb), in_specs=in_specs, out_specs=out_specs,
                         out_shape=out_shape, compiler_params=_ARB2)(*args)


def _in_proj(x, meta, g, w_t, w_a_t, t, carry=()):
    n_real = x.shape[0] + N_META
    n, tn = w_t.shape[0], 1024
    tm = _row_tile(t, TM_BIG)
    nt, nb = t // tm, n // tn

    def body(x_ref, xh_ref, m_ref, g_ref, w_ref, wa_ref, h_ref, u_ref, p_ref, e_ref):
        i = pl.program_id(0)

        @pl.when(pl.program_id(1) == 0)
        def _():
            def rows_of(r0, val):
                gid = i * tm + r0 + lax.broadcasted_iota(jnp.int32, (32, 1), 0)
                val = jnp.where(gid < n_real, val, 0.0)
                h_ref[pl.ds(r0, 32), :] = val
                rinv = lax.rsqrt(jnp.mean(val * val, axis=-1, keepdims=True) + RMS_EPS)
                u_ref[pl.ds(r0, 32), :] = (val * rinv * g_ref[...]).astype(BF16)

            before = jnp.where(i == 0, m_ref[...], xh_ref[...])
            rows_of(0, jnp.concatenate([before, x_ref[0:N_META, :]], axis=0))

            def blk(k, c):
                r0 = pl.multiple_of(k * 32, 32)
                rows_of(r0, x_ref[pl.ds(pl.multiple_of(r0 - N_META, N_META), 32), :])
                return c
            lax.fori_loop(1, tm // 32, blk, 0, unroll=17)
            e_ref[...] = _dot_nt(u_ref[...], wa_ref[...]).astype(BF16)

        p_ref[...] = _dot_nt(u_ref[...], w_ref[...]).astype(BF16)

    row = lambda w: pl.BlockSpec((tm, w), lambda i, j: (i, 0))
    return _carried_call(
        body, carry, name="in_proj", grid=(nt, nb),
        in_specs=[row(D), pl.BlockSpec((N_META, D), lambda i, j: (jnp.maximum(i * (tm // N_META) - 1, 0), 0)),
                  _resident((N_META, D)), _resident((1, D)), pl.BlockSpec((tn, D), lambda i, j: (j, 0)),
                  _resident(w_a_t.shape)],
        out_specs=[row(D), row(D), pl.BlockSpec((tm, tn), lambda i, j: (i, j)), row(w_a_t.shape[0])],
        out_shape=[SDS((t, D), F32), SDS((t, D), BF16), SDS((t, n), BF16), SDS((t, w_a_t.shape[0]), BF16)],
        compiler_params=_ARB2)(x, x, meta, g, w_t, w_a_t)


def _gla_decay(a_ref, wau_ref, ba_ref):
    z = _dot(a_ref[...], wau_ref[...]) + ba_ref[...]
    la = (jnp.minimum(z, 0.0) - jnp.log(1.0 + jnp.exp(-jnp.abs(z)))) * (1.0 / TAU)
    r = lax.broadcasted_iota(jnp.int32, (CH, CH), 0)
    c = lax.broadcasted_iota(jnp.int32, (CH, CH), 1)
    b = _tri_matmul((r >= c).astype(BF16), la)
    mid = jnp.broadcast_to(b[CH // 2:CH // 2 + 1, :], b.shape)
    last = jnp.broadcast_to(b[CH - 1:CH, :], b.shape)
    return z, b, mid, last, r >= c


def _tri_matmul(tri, x):
    n = x.shape[1]
    x1 = x.astype(BF16)
    r1 = x - x1.astype(F32)
    x2 = r1.astype(BF16)
    x3 = (r1 - x2.astype(F32)).astype(BF16)
    y = _dot(tri, jnp.concatenate([x1, x2, x3], axis=1))
    return y[:, 0:n] + y[:, n:2 * n] + y[:, 2 * n:3 * n]


def _gla_fwd(proj, alr, wau, balpha, gn):
    t = proj.shape[0]
    nc = t // CH

    def body(qk_ref, v_ref, r_ref, a_ref, wau_ref, ba_ref, gn_ref, o_ref, og_ref, sall_ref, s_scr):
        @pl.when(pl.program_id(0) == 0)
        def _():
            s_scr[...] = jnp.zeros_like(s_scr)

        sall_ref[0] = s_scr[...]
        _, b, bmid, blast, causal = _gla_decay(a_ref, wau_ref, ba_ref)
        for h in range(HEADS):
            ks = slice(h * DKH, (h + 1) * DKH)
            vs = slice(h * DVH, (h + 1) * DVH)
            bh, mh, lh = b[:, ks], bmid[:, ks], blast[:, ks]
            q = qk_ref[:, ks].astype(F32) * (DKH ** -0.5)
            k = qk_ref[:, DK + h * DKH:DK + (h + 1) * DKH].astype(F32)
            v = v_ref[:, vs]
            qt = (q * jnp.exp(bh - mh)).astype(BF16)
            kt = (k * jnp.exp(mh - bh)).astype(BF16)
            qg = (q * jnp.exp(bh)).astype(BF16)
            kg = (k * jnp.exp(lh - bh)).astype(BF16)
            a = jnp.where(causal, _dot_nt(qt, kt), 0.0)
            st = s_scr[vs, :]
            o = _dot(a.astype(BF16), v) + _dot_nt(qg, st.astype(BF16))
            el = jnp.exp(lh)
            s_scr[vs, :] = st * jnp.concatenate([el, el], axis=0) + _dot_tn(v, kg)
            o_ref[:, vs] = o
            on = o * lax.rsqrt(jnp.mean(o * o, axis=-1, keepdims=True) + RMS_EPS) * gn_ref[:, vs]
            rr = r_ref[:, vs].astype(F32)
            og_ref[:, vs] = (on * (rr * _sigmoid(rr))).astype(BF16)

    return pl.pallas_call(
        body, name="gla_fwd", grid=(nc,),
        in_specs=[pl.BlockSpec((CH, D), lambda c: (c, 0)), pl.BlockSpec((CH, D), lambda c: (c, 1)),
                  pl.BlockSpec((CH, D), lambda c: (c, 6)), pl.BlockSpec((CH, LANES), lambda c: (c, 0)),
                  _resident((LANES, DK)), _resident((1, DK)), _resident((1, DV))],
        out_specs=[pl.BlockSpec((CH, DV), lambda c: (c, 0)), pl.BlockSpec((CH, DV), lambda c: (c, 0)),
                   pl.BlockSpec((1, DV, DKH), lambda c: (c, 0, 0))],
        out_shape=[SDS((t, DV), F32), SDS((t, DV), BF16), SDS((nc, DV, DKH), F32)],
        scratch_shapes=[pltpu.VMEM((DV, DKH), F32)], compiler_params=_ARB1)(proj, proj, proj, alr, wau, balpha, gn)


def _conf_fwd(proj, cw, cb, lg, lb, carry=()):
    t = proj.shape[0]
    nt = t // TE

    def body(c1_ref, c2_ref, cw_ref, cb_ref, lg_ref, lb_ref, cc_ref, cs_ref, cext):
        i = pl.program_id(0)

        @pl.when(i == 0)
        def _():
            cext[0:HALO, :] = jnp.zeros((HALO, D), F32)

        @pl.when(i > 0)
        def _():
            cext[0:HALO, :] = cext[TE:TE + HALO, :]

        def glu(r0):
            c2 = c2_ref[pl.ds(r0, 32), :].astype(F32)
            cext[pl.ds(HALO + r0, 32), :] = c1_ref[pl.ds(r0, 32), :].astype(F32) * _sigmoid(c2)
        _row_loop(TE, 32, glu)

        def conv(r0):
            for part in range(D // CONV_LANES):
                cols = slice(part * CONV_LANES, (part + 1) * CONV_LANES)
                win = cext[pl.ds(r0, CONV_ROWS + HALO), cols]
                acc = jnp.zeros((CONV_ROWS, CONV_LANES), F32) + cb_ref[:, cols]
                for s, taps in _shift_classes([(2 + j, j) for j in range(CONF_K)]):
                    ws = _shifted(win, s)
                    for a8, j in taps:
                        acc = acc + jnp.tile(cw_ref[j, :, cols], (CONV_ROWS // 8, 1)) * ws[a8:a8 + CONV_ROWS, :]
                cc_ref[pl.ds(r0, CONV_ROWS), cols] = acc

        def norm(r0):
            for sub in range(CONV_ROWS // 32):
                rows = pl.ds(r0 + 32 * sub, 32)
                x = cc_ref[rows, :]
                xc = x - jnp.mean(x, axis=-1, keepdims=True)
                var = jnp.mean(xc * xc, axis=-1, keepdims=True)
                ln = xc * lax.rsqrt(var + LN_EPS) * lg_ref[...] + lb_ref[...]
                cs_ref[rows, :] = (ln * _sigmoid(ln)).astype(BF16)

        def step(r0):
            conv(r0)
            norm(r0)
        _row_loop(TE, CONV_ROWS, step)

    return _carried_call(
        body, carry, name="conf_fwd", grid=(nt,),
        in_specs=[pl.BlockSpec((TE, D), lambda i: (i, 2)), pl.BlockSpec((TE, D), lambda i: (i, 3)),
                  _resident((32, 8, D)), _resident((1, D)), _resident((1, D)), _resident((1, D))],
        out_specs=[pl.BlockSpec((TE, D), lambda i: (i, 0)), pl.BlockSpec((TE, D), lambda i: (i, 0))],
        out_shape=[SDS((t, D), F32), SDS((t, D), BF16)],
        scratch_shapes=[pltpu.VMEM((TE + HALO, D), F32)], compiler_params=_ARB1)(proj, proj, cw, cb, lg, lb)


def _mix_fwd(og, cs, proj, h0, wg, wc, wo):
    t = h0.shape[0]
    nt = t // TM

    def body(og_ref, cs_ref, g_ref, h0_ref, wg_ref, wc_ref, wo_ref, brg_ref, brc_ref, mg_ref, h1_ref):
        brg_ref[...] = _dot(og_ref[...], wg_ref[...]).astype(BF16)
        brc_ref[...] = _dot(cs_ref[...], wc_ref[...]).astype(BF16)

        def blk(r0):
            rows = pl.ds(r0, 32)
            gg = g_ref[rows, 0:D].astype(F32)
            gc = g_ref[rows, D:2 * D].astype(F32)
            m = _sigmoid(gg) * brg_ref[rows, :].astype(F32) + _sigmoid(gc) * brc_ref[rows, :].astype(F32)
            mg_ref[rows, :] = m.astype(BF16)
        _row_loop(TM, 32, blk)
        h1_ref[...] = h0_ref[...] + _dot(mg_ref[...], wo_ref[...])

    row = lambda w: pl.BlockSpec((TM, w), lambda i: (i, 0))
    return pl.pallas_call(
        body, name="mix_fwd", grid=(nt,),
        in_specs=[row(D), row(D), pl.BlockSpec((TM, 2 * D), lambda i: (i, 2)), row(D),
                  _once((D, D)), _once((D, D)), _once((D, D))],
        out_specs=[row(D), row(D), row(D), row(D)],
        out_shape=[SDS((t, D), BF16), SDS((t, D), BF16), SDS((t, D), BF16), SDS((t, D), F32)],
        compiler_params=_ARB1)(og, cs, proj, h0, wg, wc, wo)


def _ffn_out(up, fw, h1, wd, gf, target):
    t = h1.shape[0]
    nt = t // TE
    n_real = target.shape[0] + N_META

    def body(a_ref, bv_ref, fw_ref, h1_ref, wd_ref, gf_ref, tg_ref, tb_ref, f_ref, dh2_ref, red_ref, before, hs):
        i = pl.program_id(0)

        @pl.when(i == 0)
        def _():
            before[...] = jnp.zeros_like(before)
            red_ref[...] = jnp.zeros_like(red_ref)

        def conv(r0, win):
            ac = fw_ref[3] + fw_ref[0] * win[14:30, :] + fw_ref[1] * win[15:31, :] + fw_ref[2] * win[16:32, :]
            f_ref[pl.ds(r0, 16), :] = (ac * _sigmoid(ac) * bv_ref[pl.ds(r0, 16), :].astype(F32)).astype(BF16)

        conv(0, jnp.concatenate([before[...], a_ref[0:HALO_F, :].astype(F32)], axis=0))

        def conv_blk(k, c):
            r0 = pl.multiple_of(k * 16, 16)
            conv(r0, a_ref[pl.ds(r0 - HALO_F, 32), :].astype(F32))
            return c
        lax.fori_loop(1, TE // 16, conv_blk, 0)
        before[...] = a_ref[TE - HALO_F:TE, :].astype(F32)
        hs[...] = h1_ref[...] + _dot(f_ref[...], wd_ref[...])

        def head(r0, tg):
            rows = pl.ds(r0, 32)
            h2 = hs[rows, :]
            rinv = lax.rsqrt(jnp.mean(h2 * h2, axis=-1, keepdims=True) + RMS_EPS)
            hh = h2 * rinv
            gid = i * TE + r0 + lax.broadcasted_iota(jnp.int32, (32, 1), 0)
            live = jnp.logical_and(gid >= N_META, gid < n_real)
            err = jnp.where(live, hh * gf_ref[...] - tg, 0.0)
            dy = err * (1.0 / D)
            red_ref[0:8, :] += _rows8(err * err)
            red_ref[8:16, :] += _rows8(dy * hh)
            dhh = dy * gf_ref[...]
            dh2_ref[rows, :] = rinv * (dhh - hh * jnp.mean(dhh * hh, axis=-1, keepdims=True))

        head(0, jnp.concatenate([tb_ref[...], tg_ref[0:N_META, :]], axis=0))

        def blk(k, c):
            r0 = pl.multiple_of(k * 32, 32)
            head(r0, tg_ref[pl.ds(pl.multiple_of(r0 - N_META, N_META), 32), :])
            return c
        lax.fori_loop(1, TE // 32, blk, 0, unroll=9)

    row = lambda w: pl.BlockSpec((TE, w), lambda i: (i, 0))
    return pl.pallas_call(
        body, name="ffn_out", grid=(nt,),
        in_specs=[pl.BlockSpec((TE, DFF), lambda i: (i, 0)), pl.BlockSpec((TE, DFF), lambda i: (i, 1)),
                  _resident((4, 16, DFF)), row(D), _resident((DFF, D)), _resident((1, D)), row(D),
                  pl.BlockSpec((N_META, D), lambda i: (jnp.maximum(i * (TE // N_META) - 1, 0), 0))],
        out_specs=[row(DFF), row(D), _resident((16, D))],
        out_shape=[SDS((t, DFF), BF16), SDS((t, D), F32), SDS((16, D), F32)],
        scratch_shapes=[pltpu.VMEM((HALO_F, DFF), F32), pltpu.VMEM((TE, D), F32)],
        compiler_params=_ARB1)(up, up, fw, h1, wd, gf, target, target)


def _ffn_bwd(dh2, wd, up, fw):
    t = dh2.shape[0]
    nt = t // TE
    hb = TE // HALO_F

    def body(dh_ref, wd_ref, a_ref, ah_ref, bv_ref, fw_ref, dup_ref, dw_ref, dax, dfs):
        i = pl.program_id(0)
        ti = nt - 1 - i

        @pl.when(i == 0)
        def _():
            dax[TE:TE + HALO_F, :] = jnp.zeros((HALO_F, DFF), F32)
            dw_ref[...] = jnp.zeros_like(dw_ref)

        @pl.when(i > 0)
        def _():
            dax[TE:TE + HALO_F, :] = dax[0:HALO_F, :]

        dfs[...] = _dot_nt(dh_ref[...].astype(BF16), wd_ref[...])

        def act(r0, win):
            rows = pl.ds(r0, 16)
            ac = fw_ref[3] + fw_ref[0] * win[14:30, :] + fw_ref[1] * win[15:31, :] + fw_ref[2] * win[16:32, :]
            sg = _sigmoid(ac)
            df = dfs[rows, :]
            dup_ref[rows, DFF:2 * DFF] = (df * ac * sg).astype(BF16)
            dac = df * bv_ref[rows, :].astype(F32) * sg * (1.0 + ac * (1.0 - sg))
            dax[rows, :] = dac
            dw_ref[3] += _rows8(dac)
            for j in range(FFN_K):
                dw_ref[j] += _rows8(dac * win[14 + j:30 + j, :])

        before = jnp.where(ti > 0, ah_ref[...].astype(F32), 0.0)
        act(0, jnp.concatenate([before, a_ref[0:HALO_F, :].astype(F32)], axis=0))

        def act_blk(k, c):
            r0 = pl.multiple_of(k * 16, 16)
            act(r0, a_ref[pl.ds(r0 - HALO_F, 32), :].astype(F32))
            return c
        lax.fori_loop(1, TE // 16, act_blk, 0)

        def convt(r0):
            win = dax[pl.ds(r0, 32), :]
            da = fw_ref[2] * win[0:16, :] + fw_ref[1] * win[1:17, :] + fw_ref[0] * win[2:18, :]
            dup_ref[pl.ds(r0, 16), 0:DFF] = da.astype(BF16)
        _row_loop(TE, 16, convt)

    rev = lambda w: pl.BlockSpec((TE, w), lambda i: (nt - 1 - i, 0))
    return pl.pallas_call(
        body, name="ffn_bwd", grid=(nt,),
        in_specs=[rev(D), _resident((DFF, D)), rev(DFF),
                  pl.BlockSpec((HALO_F, DFF), lambda i: (jnp.maximum((nt - 1 - i) * hb - 1, 0), 0)),
                  pl.BlockSpec((TE, DFF), lambda i: (nt - 1 - i, 1)), _resident((4, 16, DFF))],
        out_specs=[rev(2 * DFF), _resident((4, 8, DFF))],
        out_shape=[SDS((t, 2 * DFF), BF16), SDS((4, 8, DFF), F32)],
        scratch_shapes=[pltpu.VMEM((TE + HALO_F, DFF), F32), pltpu.VMEM((TE, DFF), F32)],
        compiler_params=_ARB1)(dh2, wd, up, up, up, fw)


def _dgrad_norm(dy, w_t, h, g, dres, name, carry=()):
    t, k = dy.shape
    nt = t // TM

    def body(dy_ref, w_ref, h_ref, g_ref, dr_ref, dh_ref, dg_ref, acc):
        @pl.when(pl.program_id(0) == 0)
        def _():
            dg_ref[...] = jnp.zeros_like(dg_ref)

        acc[...] = _dot(dy_ref[...], w_ref[...])

        def blk(r0):
            rows = pl.ds(r0, 32)
            x = h_ref[rows, :]
            rinv = lax.rsqrt(jnp.mean(x * x, axis=-1, keepdims=True) + RMS_EPS)
            hh = x * rinv
            du = acc[rows, :]
            dg_ref[...] += _rows8(du * hh)
            dhh = du * g_ref[...]
            dh_ref[rows, :] = dr_ref[rows, :] + rinv * (dhh - hh * jnp.mean(dhh * hh, axis=-1, keepdims=True))
        _row_loop(TM, 32, blk, unroll=10)

    row = pl.BlockSpec((TM, D), lambda i: (i, 0))
    in_specs = [pl.BlockSpec((TM, k), lambda i: (i, 0)), _once((k, D)), row, _resident((1, D)), row]
    return _carried_call(
        body, carry, name=name, grid=(nt,), in_specs=in_specs, out_specs=[row, _resident((8, D))],
        out_shape=[SDS((t, D), F32), SDS((8, D), F32)],
        scratch_shapes=[pltpu.VMEM((TM, D), F32)], compiler_params=_ARB1)(dy, w_t, h, g, dres)


def _in_dgrad(dy, w_t, h, g, dres, dy_extra, w_extra_t, carry=()):
    t, k = dy.shape
    nt = t // TM

    def body(dy_ref, w_ref, h_ref, g_ref, dr_ref, de_ref, we_ref, dh_ref, dg_ref, acc):
        @pl.when(pl.program_id(0) == 0)
        def _():
            dg_ref[...] = jnp.zeros_like(dg_ref)

        acc[...] = _dot(dy_ref[...], w_ref[...])
        acc[...] += _dot(de_ref[...], we_ref[...])

        def blk(r0):
            rows = pl.ds(r0, 32)
            x = h_ref[rows, :]
            rinv = lax.rsqrt(jnp.mean(x * x, axis=-1, keepdims=True) + RMS_EPS)
            hh = x * rinv
            du = acc[rows, :]
            dg_ref[...] += _rows8(du * hh)
            dhh = du * g_ref[...]
            dh_ref[rows, :] = dr_ref[rows, :] + rinv * (dhh - hh * jnp.mean(dhh * hh, axis=-1, keepdims=True))
        _row_loop(TM, 32, blk, unroll=4)

    tile = lambda w: pl.BlockSpec((TM, w), lambda i: (i, 0))
    return _carried_call(
        body, carry, name="in_dgrad", grid=(nt,),
        in_specs=[tile(k), _once((k, D)), tile(D), _resident((1, D)), tile(D), tile(dy_extra.shape[1]),
                  _once(w_extra_t.shape)],
        out_specs=[tile(D), _resident((8, D))], out_shape=[SDS((t, D), F32), SDS((8, D), F32)],
        scratch_shapes=[pltpu.VMEM((TM, D), F32)], compiler_params=_ARB1)(dy, w_t, h, g, dres, dy_extra, w_extra_t)


def _wgrad(x, dy, tk, name):
    t, k = x.shape
    n = dy.shape[1]
    tm = _row_tile(t, TM_BIG)
    nk, nt = k // tk, t // tm

    def body(x_ref, dy_ref, o_ref, acc):
        @pl.when(pl.program_id(1) == 0)
        def _():
            acc[...] = jnp.zeros_like(acc)
        acc[...] += _dot_tn(x_ref[...], dy_ref[...].astype(BF16))

        @pl.when(pl.program_id(1) == nt - 1)
        def _():
            o_ref[...] = acc[...].astype(BF16)

    return pl.pallas_call(
        body, name=name, grid=(nk, nt),
        in_specs=[pl.BlockSpec((tm, tk), lambda j, i: (i, j)), pl.BlockSpec((tm, n), lambda j, i: (i, 0))],
        out_specs=pl.BlockSpec((tk, n), lambda j, i: (j, 0)), out_shape=SDS((k, n), BF16),
        scratch_shapes=[pltpu.VMEM((tk, n), F32)], compiler_params=_ARB2)(x, dy)


def _mix_bwd(dh1, wo, wg, wc, proj, brg, brc):
    t = dh1.shape[0]
    nt = t // TM

    def body(dh_ref, wo_ref, wg_ref, wc_ref, g_ref, brg_ref, brc_ref, dbg_ref, dbc_ref, dog_ref, dcs_ref, dp_ref, dm):
        dm[...] = _dot_nt(dh_ref[...].astype(BF16), wo_ref[...])

        def blk(r0):
            rows = pl.ds(r0, 32)
            d = dm[rows, :]
            sg = _sigmoid(g_ref[rows, 0:D].astype(F32))
            sc = _sigmoid(g_ref[rows, D:2 * D].astype(F32))
            dbg_ref[rows, :] = (d * sg).astype(BF16)
            dbc_ref[rows, :] = (d * sc).astype(BF16)
            dp_ref[rows, 0:D] = (d * brg_ref[rows, :].astype(F32) * sg * (1.0 - sg)).astype(BF16)
            dp_ref[rows, D:2 * D] = (d * brc_ref[rows, :].astype(F32) * sc * (1.0 - sc)).astype(BF16)
        _row_loop(TM, 32, blk)
        dog_ref[...] = _dot_nt(dbg_ref[...], wg_ref[...]).astype(BF16)
        dcs_ref[...] = _dot_nt(dbc_ref[...], wc_ref[...]).astype(BF16)

    row = pl.BlockSpec((TM, D), lambda i: (i, 0))
    wide = pl.BlockSpec((TM, 2 * D), lambda i: (i, 2))
    return pl.pallas_call(
        body, name="mix_bwd", grid=(nt,),
        in_specs=[row, _once((D, D)), _once((D, D)), _once((D, D)), wide, row, row],
        out_specs=[row, row, row, row, wide],
        out_shape=[SDS((t, D), BF16)] * 4 + [SDS((t, NPROJ), BF16)],
        scratch_shapes=[pltpu.VMEM((TM, D), F32)], compiler_params=_ARB1)(dh1, wo, wg, wc, proj, brg, brc)


def _glapost_bwd(dog, o, proj, gn, dproj):
    t = o.shape[0]
    nt = t // TE

    def body(dog_ref, o_ref, r_ref, gn_ref, dp_in, do_ref, dp_ref, dgn_ref):
        del dp_in

        @pl.when(pl.program_id(0) == 0)
        def _():
            dgn_ref[...] = jnp.zeros_like(dgn_ref)

        def blk(r0):
            rows = pl.ds(r0, 32)
            for h in range(HEADS):
                vs = slice(h * DVH, (h + 1) * DVH)
                x = o_ref[rows, vs]
                rinv = lax.rsqrt(jnp.mean(x * x, axis=-1, keepdims=True) + RMS_EPS)
                oh = x * rinv
                g = gn_ref[:, vs]
                rr = r_ref[rows, vs].astype(F32)
                sr = _sigmoid(rr)
                d = dog_ref[rows, vs].astype(F32)
                dp_ref[rows, vs] = (d * oh * g * sr * (1.0 + rr * (1.0 - sr))).astype(BF16)
                don = d * rr * sr
                dgn_ref[:, vs] += _rows8(don * oh)
                doh = don * g
                do_ref[rows, vs] = (rinv * (doh - oh * jnp.mean(doh * oh, axis=-1, keepdims=True))).astype(BF16)
        _row_loop(TE, 32, blk, unroll=5)

    row = pl.BlockSpec((TE, D), lambda i: (i, 0))
    rcol = pl.BlockSpec((TE, D), lambda i: (i, 6))
    return pl.pallas_call(
        body, name="glapost_bwd", grid=(nt,),
        in_specs=[row, row, rcol, _resident((1, D)), pl.BlockSpec(memory_space=pl.ANY)],
        out_specs=[row, rcol, _resident((8, D))],
        out_shape=[SDS((t, D), BF16), SDS((t, NPROJ), BF16), SDS((8, D), F32)],
        input_output_aliases={4: 1}, compiler_params=_ARB1)(dog, o, proj, gn, dproj)


def _conf_bwd(dcs, cc, proj, cw, lg, lb, dproj, carry=()):
    t = cc.shape[0]
    nt = t // TE
    hb = TE // HALO

    def body(dcs_ref, cc_ref, c1_ref, c2_ref, c1h_ref, c2h_ref, cw_ref, lg_ref, lb_ref, dp_in,
             dp_ref, dw_ref, ds_ref, cext, dext):
        del dp_in
        i = pl.program_id(0)
        ti = nt - 1 - i

        @pl.when(i == 0)
        def _():
            dext[TE:TE + HALO, :] = jnp.zeros((HALO, D), F32)
            dw_ref[...] = jnp.zeros_like(dw_ref)
            ds_ref[...] = jnp.zeros_like(ds_ref)

        @pl.when(i > 0)
        def _():
            dext[TE:TE + HALO, :] = dext[0:HALO, :]

        ch = c1h_ref[...].astype(F32) * _sigmoid(c2h_ref[...].astype(F32))
        cext[0:HALO, :] = jnp.where(ti > 0, ch, 0.0)

        def pre(r0):
            rows = pl.ds(r0, 32)
            cext[pl.ds(HALO + r0, 32), :] = c1_ref[rows, :].astype(F32) * _sigmoid(c2_ref[rows, :].astype(F32))
            x = cc_ref[rows, :]
            mu = jnp.mean(x, axis=-1, keepdims=True)
            xc = x - mu
            rstd = lax.rsqrt(jnp.mean(xc * xc, axis=-1, keepdims=True) + LN_EPS)
            xh = xc * rstd
            ln = xh * lg_ref[...] + lb_ref[...]
            sg = _sigmoid(ln)
            dln = dcs_ref[rows, :].astype(F32) * sg * (1.0 + ln * (1.0 - sg))
            ds_ref[0] += _rows8(dln * xh)
            ds_ref[1] += _rows8(dln)
            dxh = dln * lg_ref[...]
            dcc = rstd * (dxh - jnp.mean(dxh, axis=-1, keepdims=True) - xh * jnp.mean(dxh * xh, axis=-1, keepdims=True))
            dext[rows, :] = dcc
            ds_ref[2] += _rows8(dcc)
        _row_loop(TE, 32, pre, unroll=10)

        def convt(r0):
            rows = pl.ds(r0, CONV_ROWS)
            for part in range(D // CONV_LANES):
                cols = slice(part * CONV_LANES, (part + 1) * CONV_LANES)
                wd = dext[pl.ds(r0, CONV_ROWS + HALO), cols]
                dc = jnp.zeros((CONV_ROWS, CONV_LANES), F32)
                for s, taps in _shift_classes([(CONF_K - 1 - j, j) for j in range(CONF_K)]):
                    ws = _shifted(wd, s)
                    for a8, j in taps:
                        dc = dc + jnp.tile(cw_ref[j, :, cols], (CONV_ROWS // 8, 1)) * ws[a8:a8 + CONV_ROWS, :]
                dcc = wd[0:CONV_ROWS, :]
                wc = cext[pl.ds(r0, CONV_ROWS + HALO), cols]
                for s, taps in _shift_classes([(2 + j, j) for j in range(CONF_K)]):
                    ws = _shifted(wc, s)
                    for a8, j in taps:
                        dw_ref[j, :, cols] += _rows8(dcc * ws[a8:a8 + CONV_ROWS, :])
                c1 = c1_ref[rows, cols].astype(F32)
                s2 = _sigmoid(c2_ref[rows, cols].astype(F32))
                dp_ref[rows, cols] = (dc * s2).astype(BF16)
                dp_ref[rows, D + part * CONV_LANES:D + (part + 1) * CONV_LANES] = (dc * c1 * s2 * (1.0 - s2)).astype(BF16)
        _row_loop(TE, CONV_ROWS, convt)

    rev = lambda col: pl.BlockSpec((TE, D), lambda i: (nt - 1 - i, col))
    halo = lambda col: pl.BlockSpec((HALO, D), lambda i: (jnp.maximum((nt - 1 - i) * hb - 1, 0), col))
    return _carried_call(
        body, carry, name="conf_bwd", grid=(nt,),
        in_specs=[rev(0), rev(0), rev(2), rev(3), halo(2), halo(3), _resident((32, 8, D)), _resident((1, D)),
                  _resident((1, D)), pl.BlockSpec(memory_space=pl.ANY)],
        out_specs=[pl.BlockSpec((TE, 2 * D), lambda i: (nt - 1 - i, 1)), _resident((32, 8, D)), _resident((3, 8, D))],
        out_shape=[SDS((t, NPROJ), BF16), SDS((32, 8, D), F32), SDS((3, 8, D), F32)],
        scratch_shapes=[pltpu.VMEM((TE + HALO, D), F32), pltpu.VMEM((TE + HALO, D), F32)],
        input_output_aliases={9: 0}, compiler_params=_ARB1)(dcs, cc, proj, proj, proj, proj, cw, lg, lb, dproj)


def _gla_bwd(proj, alr, wau, balpha, do, sall, dproj, carry=()):
    t = proj.shape[0]
    nc = t // CH

    def body(qk_ref, v_ref, a_ref, wau_ref, ba_ref, do_ref, s_ref, dp_in, dp_ref, da_ref, dwau_ref, dba_ref, ds_scr, dla_scr):
        del dp_in

        @pl.when(pl.program_id(0) == 0)
        def _():
            ds_scr[...] = jnp.zeros_like(ds_scr)
            dwau_ref[...] = jnp.zeros_like(dwau_ref)
            dba_ref[...] = jnp.zeros_like(dba_ref)

        z, b, bmid, blast, causal = _gla_decay(a_ref, wau_ref, ba_ref)
        dlasts = []
        for h in range(HEADS):
            ks = slice(h * DKH, (h + 1) * DKH)
            vs = slice(h * DVH, (h + 1) * DVH)
            bh, mh, lh = b[:, ks], bmid[:, ks], blast[:, ks]
            q = qk_ref[:, ks].astype(F32) * (DKH ** -0.5)
            k = qk_ref[:, DK + h * DKH:DK + (h + 1) * DKH].astype(F32)
            v = v_ref[:, vs]
            dout = do_ref[:, vs]
            eq, ek, eb, eg, el = jnp.exp(bh - mh), jnp.exp(mh - bh), jnp.exp(bh), jnp.exp(lh - bh), jnp.exp(lh)
            qt, kt = (q * eq).astype(BF16), (k * ek).astype(BF16)
            qg, kg = (q * eb).astype(BF16), (k * eg).astype(BF16)
            st = s_ref[0, vs, :]
            dsn = ds_scr[vs, :]
            st16, dsn16 = st.astype(BF16), dsn.astype(BF16)
            a = jnp.where(causal, _dot_nt(qt, kt), 0.0).astype(BF16)
            da = jnp.where(causal, _dot_nt(dout, v), 0.0).astype(BF16)
            dq_inter = _dot(dout, st16) * eb
            dk_inter = _dot(v, dsn16) * eg
            dq = _dot(da, kt) * eq + dq_inter
            dk = _dot_tn(da, qt) * ek + dk_inter
            dv = _dot_tn(a, dout) + _dot_nt(kg, dsn16)
            dlasts.append(jnp.sum(k * dk_inter, axis=0, keepdims=True) + jnp.sum(st * dsn, axis=0, keepdims=True) * el[0:1, :])
            dla_scr[:, ks] = q * dq - k * dk
            ds_scr[vs, :] = dsn * jnp.concatenate([el, el], axis=0) + _dot_tn(dout, qg)
            dp_ref[:, ks] = (dq * (DKH ** -0.5)).astype(BF16)
            dp_ref[:, DK + h * DKH:DK + (h + 1) * DKH] = dk.astype(BF16)
            dp_ref[:, D + h * DVH:D + (h + 1) * DVH] = dv.astype(BF16)
        r = lax.broadcasted_iota(jnp.int32, (CH, CH), 0)
        c = lax.broadcasted_iota(jnp.int32, (CH, CH), 1)
        dla = _tri_matmul((r <= c).astype(BF16), dla_scr[...]) + jnp.concatenate(dlasts, axis=1)
        dz = (dla * (1.0 / TAU) * _sigmoid(-z)).astype(BF16)
        da_ref[...] = _dot_nt(dz, wau_ref[...]).astype(BF16)
        dwau_ref[...] += _dot_tn(a_ref[...], dz)
        dba_ref[...] += _rows8(dz.astype(F32))

    rev = lambda w, col: pl.BlockSpec((CH, w), lambda c: (nc - 1 - c, col))
    return _carried_call(
        body, carry, name="gla_bwd", grid=(nc,),
        in_specs=[rev(D, 0), rev(D, 1), rev(LANES, 0), _resident((LANES, DK)), _resident((1, DK)), rev(D, 0),
                  pl.BlockSpec((1, DV, DKH), lambda c: (nc - 1 - c, 0, 0)), pl.BlockSpec(memory_space=pl.ANY)],
        out_specs=[rev(2 * D, 0), rev(LANES, 0), _resident((LANES, DK)), _resident((8, DK))],
        out_shape=[SDS((t, NPROJ), BF16), SDS((t, LANES), BF16), SDS((LANES, DK), F32), SDS((8, DK), F32)],
        scratch_shapes=[pltpu.VMEM((DV, DKH), F32), pltpu.VMEM((CH, DK), F32)],
        input_output_aliases={7: 0}, compiler_params=_ARB1)(proj, proj, alr, wau, balpha, do, sall, dproj)


def _all_gather(xs, name):
    n = len(xs)

    def body(*refs):
        x_refs, out_refs = refs[:n], refs[n:2 * n]
        send_sems, recv_sems = refs[2 * n:]
        x, y, c = _my_place()
        me, sibling = (x, y, c), (x, y, 1 - c)
        x_nbr, y_nbr, diagonal = (1 - x, y), (x, 1 - y), (1 - x, 1 - y)

        def slot(p, px, py, pc):
            return out_refs[p].at[4 * px + 2 * py + pc]

        def copy(p, k, block, to, src=None):
            return pltpu.make_async_remote_copy(
                src_ref=slot(p, *block) if src is None else src, dst_ref=slot(p, *block),
                send_sem=send_sems.at[7 * p + k], recv_sem=recv_sems.at[7 * p + k], device_id=to, device_id_type=MESH_T)

        for p in range(n):
            for k, to in enumerate((sibling, (*x_nbr, c), (*y_nbr, c))):
                copy(p, k, me, to, src=x_refs[p]).start()
        for p in range(n):
            @pl.when(c == 1)
            def _():
                copy(p, 2, (*y_nbr, c), me).wait_recv()
                copy(p, 3, (*y_nbr, c), (*x_nbr, c)).start()
                copy(p, 1, (*x_nbr, c), me).wait_recv()

            @pl.when(c == 0)
            def _():
                copy(p, 1, (*x_nbr, c), me).wait_recv()
                copy(p, 3, (*x_nbr, c), (*y_nbr, c)).start()
                copy(p, 2, (*y_nbr, c), me).wait_recv()
        for p in range(n):
            copy(p, 4, (*x_nbr, c), sibling).start()
            copy(p, 5, (*y_nbr, c), sibling).start()
        for p in range(n):
            copy(p, 3, (*diagonal, c), me).wait_recv()
            copy(p, 6, (*diagonal, c), sibling).start()
        for p in range(n):
            copy(p, 0, sibling, me).wait_recv()
            for j, chip in enumerate((x_nbr, y_nbr, diagonal)):
                copy(p, 4 + j, (*chip, 1 - c), me).wait_recv()
        for p in range(n):
            for k in range(7):
                copy(p, k, me, sibling, src=x_refs[p]).wait_send()

    hbm = pl.BlockSpec(memory_space=pl.ANY)
    return pl.pallas_call(
        body, name=name, out_shape=[SDS((N_DEV, *a.shape), a.dtype) for a in xs],
        in_specs=[hbm] * n, out_specs=[hbm] * n,
        scratch_shapes=[pltpu.SemaphoreType.DMA((7 * n,)), pltpu.SemaphoreType.DMA((7 * n,))])(*xs)


def _exchange(gs):
    n = len(gs)

    def body(*refs):
        g_refs, land_refs = refs[:n], refs[n:2 * n]
        send_sems, recv_sems, local_sems = refs[2 * n:]
        x, y, c = _my_place()
        my_idx = 4 * x + 2 * y + c
        mine = [pltpu.make_async_copy(g_refs[p].at[my_idx], land_refs[p].at[my_idx], local_sems.at[p]) for p in range(n)]
        for cp in mine:
            cp.start()
        copies = []
        for k in range(1, N_DEV):
            px, py, pc = _flip(x, k & 4), _flip(y, k & 2), _flip(c, k & 1)
            p_idx = 4 * px + 2 * py + pc
            for p in range(n):
                s = 7 * p + k - 1
                cp = pltpu.make_async_remote_copy(
                    src_ref=g_refs[p].at[p_idx], dst_ref=land_refs[p].at[my_idx], send_sem=send_sems.at[s],
                    recv_sem=recv_sems.at[s], device_id=(px, py, pc), device_id_type=MESH_T)
                cp.start()
                arrival = pltpu.make_async_remote_copy(
                    src_ref=g_refs[p].at[p_idx], dst_ref=land_refs[p].at[p_idx], send_sem=send_sems.at[s],
                    recv_sem=recv_sems.at[s], device_id=(px, py, pc), device_id_type=MESH_T)
                copies.append((cp, arrival))
        for cp, arrival in copies:
            arrival.wait_recv()
        for cp, arrival in copies:
            cp.wait_send()
        for cp in mine:
            cp.wait()

    hbm = pl.BlockSpec(memory_space=pl.ANY)
    return pl.pallas_call(
        body, name="grad_exchange", out_shape=[SDS(g.shape, g.dtype) for g in gs],
        in_specs=[hbm] * n, out_specs=[hbm] * n,
        scratch_shapes=[pltpu.SemaphoreType.DMA((7 * n,)), pltpu.SemaphoreType.DMA((7 * n,)),
                        pltpu.SemaphoreType.DMA((n,))])(*gs)


def _adamw(land, w, m, v, rows_blk, name):
    rows = w.shape[0]

    def body(l_ref, w_ref, m_ref, v_ref, g_ref, d_ref, nm_ref, nv_ref):
        g = l_ref[0].astype(F32)
        for s in range(1, N_DEV):
            g = g + l_ref[s].astype(F32)
        nm = ADAM_B1 * m_ref[...] + (1.0 - ADAM_B1) * g
        nv = ADAM_B2 * v_ref[...] + (1.0 - ADAM_B2) * (g * g)
        m_hat = nm / (1.0 - ADAM_B1 ** ADAM_STEP)
        v_hat = nv / (1.0 - ADAM_B2 ** ADAM_STEP)
        g_ref[...] = g
        d_ref[...] = -ADAM_LR * (m_hat / (jnp.sqrt(v_hat) + ADAM_EPS) + ADAM_WD * w_ref[...])
        nm_ref[...] = nm
        nv_ref[...] = nv

    blk = pl.BlockSpec((rows_blk, D), lambda i: (i, 0))
    return pl.pallas_call(
        body, name=name, grid=(rows // rows_blk,),
        in_specs=[pl.BlockSpec((N_DEV, rows_blk, D), lambda i: (0, i, 0)), blk, blk, blk],
        out_specs=[blk] * 4, out_shape=[SDS((rows, D), F32)] * 4, compiler_params=_ARB1)(land, w, m, v)


BIG = ("w_in", "w_up", "w_down", "w_gla_o", "w_conf_o", "w_out")
BIG_TRANSPOSED = ("w_in", "w_up")
SMALL_SHARDED = ("meta_tokens", "conf_dw_w", "ffn_dw_w", "w_alpha_up")
REPLICATED = ("norm_mix_g", "b_alpha", "gla_norm_g", "conf_dw_b", "conf_ln_g", "conf_ln_b", "norm_ffn_g", "ffn_dw_b",
              "final_norm_g")
N_IN = sum(IN_WIDTHS)
W_IN_ROWS = N_IN // N_DEV
W_IN_PAD = -(-W_IN_ROWS // 16) * 16
ADAM_BLOCK = {"w_in": W_IN_PAD // 3, "w_up": 176, "w_down": 176, "w_gla_o": 128, "w_conf_o": 128, "w_out": 128}
SMALL_ROWS = 32


def _to_panel(name, shard):
    a = shard.reshape(shard.shape[-2], shard.shape[-1])
    if name in BIG_TRANSPOSED:
        a = a.T
    if name == "w_in":
        a = jnp.pad(a, ((0, W_IN_PAD - W_IN_ROWS), (0, 0)))
    return a


def _from_panel(name, panel, shape):
    a = panel[0:W_IN_ROWS] if name == "w_in" else panel
    if name in BIG_TRANSPOSED:
        a = a.T
    return a.reshape(shape)


def _pack_small(arrs):
    flat = jnp.concatenate([jnp.pad(a.reshape(-1), (0, (-a.size) % D)) for a in arrs])
    return jnp.pad(flat, (0, SMALL_ROWS * D - flat.shape[0])).reshape(SMALL_ROWS, D)


def _unpack_small(panel, shapes):
    flat, out, off = panel.reshape(-1), [], 0
    for shp in shapes:
        n = 1
        for s in shp:
            n *= s
        out.append(flat[off:off + n].reshape(shp))
        off += n + (-n) % D
    return out


def _local_step(x, target, w, shards=None):
    dist = shards is not None
    w = dict(w)

    def gather(names):
        return [(shards[n], False) for n in names] if dist else []

    def scatter(*arrs):
        return [(a.reshape(N_DEV, -1, D), True) for a in arrs] if dist else []

    s = x.shape[0]
    n_real = s + N_META
    t = -(-n_real // TM) * TM

    q0, r0, a0, c0 = 0, 2 * DK + DV, 2 * DK + 2 * DV, 2 * DK + 2 * DV + RANK
    wt = w["w_in_t"]
    w_main = jnp.concatenate([wt[q0:r0], wt[c0:N_IN], wt[r0:a0]], axis=0)
    w_a = jnp.pad(wt[a0:c0], ((0, LANES - RANK), (0, 0)))
    wau = jnp.pad(w["w_alpha_up"].astype(BF16), ((0, LANES - RANK), (0, 0)))
    row = lambda name: w[name].reshape(1, -1)
    cw = jnp.broadcast_to(jnp.pad(w["conf_dw_w"], ((0, 32 - CONF_K), (0, 0)))[:, None, :], (32, 8, D))
    fw = jnp.broadcast_to(jnp.concatenate([w["ffn_dw_w"], w["ffn_dw_b"].reshape(1, -1)], axis=0)[:, None, :],
                          (FFN_K + 1, 16, DFF))

    early = ("w_gla_o", "w_conf_o", "w_out", "w_up")
    h0, u1, proj, alr, *landed = _in_proj(x, w["meta_tokens"], row("norm_mix_g"), w_main, w_a, t, carry=gather(early))
    for n, land in zip(early, landed):
        w["w_up_t" if n == "w_up" else n] = land.reshape(-1, D)
    o, og, sall = _gla_fwd(proj, alr, wau, row("b_alpha"), row("gla_norm_g"))
    cc, cs, *landed = _conf_fwd(proj, cw, row("conf_dw_b"), row("conf_ln_g"), row("conf_ln_b"), carry=gather(("w_down",)))
    if dist:
        w["w_down"] = landed[0].reshape(-1, D)
    brg, brc, merged, h1 = _mix_fwd(og, cs, proj, h0, w["w_gla_o"], w["w_conf_o"], w["w_out"])
    u2, up = _norm_matmul(h1, row("norm_ffn_g"), w["w_up_t"], 512, "up_proj")
    f, dh2, red = _ffn_out(up, fw, h1, w["w_down"], row("final_norm_g"), target)
    loss = 0.5 / D * jnp.sum(red[0:8])

    g = {"final_norm_g": jnp.sum(red[8:16], axis=0)}
    dup, dfw = _ffn_bwd(dh2, w["w_down"], up, fw)
    g["ffn_dw_w"] = jnp.sum(dfw[0:FFN_K], axis=1)
    g["ffn_dw_b"] = jnp.sum(dfw[3], axis=0)
    g["w_down"] = _wgrad(f, dh2, 1408, "wgrad_down")
    dh1, dg2, *landed = _dgrad_norm(dup, w["w_up_t"], h1, row("norm_ffn_g"), dh2, "up_dgrad", carry=scatter(g["w_down"]))
    if dist:
        g["w_down"] = landed[0]
    g["norm_ffn_g"] = jnp.sum(dg2, axis=0)
    g["w_up_t"] = _wgrad(dup, u2, 1408, "wgrad_up")
    dbrg, dbrc, dog, dcs, dproj = _mix_bwd(dh1, w["w_out"], w["w_gla_o"], w["w_conf_o"], proj, brg, brc)
    g["w_out"] = _wgrad(merged, dh1, 1024, "wgrad_out")
    g["w_gla_o"] = _wgrad(og, dbrg, 1024, "wgrad_gla_o")
    g["w_conf_o"] = _wgrad(cs, dbrc, 1024, "wgrad_conf_o")
    do, dproj, dgn = _glapost_bwd(dog, o, proj, row("gla_norm_g"), dproj)
    g["gla_norm_g"] = jnp.sum(dgn, axis=0)
    dproj, dcw, dst, *landed = _conf_bwd(dcs, cc, proj, cw, row("conf_ln_g"), row("conf_ln_b"), dproj,
                                         carry=scatter(g["w_up_t"]))
    if dist:
        g["w_up_t"] = landed[0]
    g["conf_dw_w"] = jnp.sum(dcw[0:CONF_K], axis=1)
    g["conf_ln_g"], g["conf_ln_b"], g["conf_dw_b"] = jnp.sum(dst[0], axis=0), jnp.sum(dst[1], axis=0), jnp.sum(dst[2], axis=0)
    dproj, dalr, dwau, dba, *landed = _gla_bwd(proj, alr, wau, row("b_alpha"), do, sall, dproj,
                                               carry=scatter(g["w_out"], g["w_gla_o"], g["w_conf_o"]))
    if dist:
        g["w_out"], g["w_gla_o"], g["w_conf_o"] = landed
    g["w_alpha_up"] = dwau[0:RANK]
    g["b_alpha"] = jnp.sum(dba, axis=0)
    dw_main = _wgrad(dproj, u1, 1024, "wgrad_in")
    dw_a = _wgrad(dalr, u1, LANES, "wgrad_alr")
    g["w_in_t"] = jnp.concatenate([dw_main[0:r0], dw_main[NPROJ - DV:NPROJ], dw_a[0:RANK], dw_main[r0:NPROJ - DV]], axis=0)
    w_in_blocks = []
    if dist:
        pad = ((0, 0), (0, W_IN_PAD - W_IN_ROWS), (0, 0))
        w_in_blocks = [(jnp.pad(g["w_in_t"].reshape(N_DEV, W_IN_ROWS, D), pad), True)]
    dh0, dg1, *landed = _in_dgrad(dproj, w_main, h0, row("norm_mix_g"), dh1, dalr, w_a, carry=w_in_blocks)
    if dist:
        g["w_in_t"] = landed[0]
    g["norm_mix_g"] = jnp.sum(dg1, axis=0)
    g["meta_tokens"] = dh0[0:N_META]
    return loss, dh0[N_META:n_real], g


def kernel(x, meta_tokens, norm_mix_g, w_in, w_alpha_up, b_alpha, gla_norm_g, w_gla_o, conf_dw_w, conf_dw_b, conf_ln_g, conf_ln_b, w_conf_o, w_out, norm_ffn_g, w_up, ffn_dw_w, ffn_dw_b, w_down, final_norm_g, loss_target, m_meta_tokens, m_norm_mix_g, m_w_in, m_w_alpha_up, m_b_alpha, m_gla_norm_g, m_w_gla_o, m_conf_dw_w, m_conf_dw_b, m_conf_ln_g, m_conf_ln_b, m_w_conf_o, m_w_out, m_norm_ffn_g, m_w_up, m_ffn_dw_w, m_ffn_dw_b, m_w_down, m_final_norm_g, v_meta_tokens, v_norm_mix_g, v_w_in, v_w_alpha_up, v_b_alpha, v_gla_norm_g, v_w_gla_o, v_conf_dw_w, v_conf_dw_b, v_conf_ln_g, v_conf_ln_b, v_w_conf_o, v_w_out, v_norm_ffn_g, v_w_up, v_ffn_dw_w, v_ffn_dw_b, v_w_down, v_final_norm_g):
    ws = dict(meta_tokens=meta_tokens, norm_mix_g=norm_mix_g, w_in=w_in, w_alpha_up=w_alpha_up, b_alpha=b_alpha,
              gla_norm_g=gla_norm_g, w_gla_o=w_gla_o, conf_dw_w=conf_dw_w, conf_dw_b=conf_dw_b, conf_ln_g=conf_ln_g,
              conf_ln_b=conf_ln_b, w_conf_o=w_conf_o, w_out=w_out, norm_ffn_g=norm_ffn_g, w_up=w_up, ffn_dw_w=ffn_dw_w,
              ffn_dw_b=ffn_dw_b, w_down=w_down, final_norm_g=final_norm_g)
    ms = dict(meta_tokens=m_meta_tokens, norm_mix_g=m_norm_mix_g, w_in=m_w_in, w_alpha_up=m_w_alpha_up, b_alpha=m_b_alpha,
              gla_norm_g=m_gla_norm_g, w_gla_o=m_w_gla_o, conf_dw_w=m_conf_dw_w, conf_dw_b=m_conf_dw_b,
              conf_ln_g=m_conf_ln_g, conf_ln_b=m_conf_ln_b, w_conf_o=m_w_conf_o, w_out=m_w_out, norm_ffn_g=m_norm_ffn_g,
              w_up=m_w_up, ffn_dw_w=m_ffn_dw_w, ffn_dw_b=m_ffn_dw_b, w_down=m_w_down, final_norm_g=m_final_norm_g)
    vs = dict(meta_tokens=v_meta_tokens, norm_mix_g=v_norm_mix_g, w_in=v_w_in, w_alpha_up=v_w_alpha_up, b_alpha=v_b_alpha,
              gla_norm_g=v_gla_norm_g, w_gla_o=v_w_gla_o, conf_dw_w=v_conf_dw_w, conf_dw_b=v_conf_dw_b,
              conf_ln_g=v_conf_ln_g, conf_ln_b=v_conf_ln_b, w_conf_o=v_w_conf_o, w_out=v_w_out, norm_ffn_g=v_norm_ffn_g,
              w_up=v_w_up, ffn_dw_w=v_ffn_dw_w, ffn_dw_b=v_ffn_dw_b, w_down=v_w_down, final_norm_g=v_final_norm_g)
    small = SMALL_SHARDED + REPLICATED
    pack_small = lambda d: _pack_small([d[n] for n in small])

    shards = {n: _to_panel(n, ws[n]).astype(BF16) for n in BIG}
    own = [shards["w_in"], pack_small(ws)]
    my_idx = 4 * lax.axis_index("x") + 2 * lax.axis_index("y") + lax.axis_index("c")
    gathered = [lax.dynamic_update_slice(full_, mine[None], (my_idx, 0, 0))
                for full_, mine in zip(_all_gather(own, "weight_gather"), own)]
    full = {n: ws[n].reshape(-1) for n in REPLICATED}
    full["w_in_t"] = gathered[0][:, 0:W_IN_ROWS].reshape(N_IN, D)
    flat, off = gathered[1].reshape(N_DEV, -1), 0
    for n in SMALL_SHARDED:
        k, c = ws[n].shape[-2], ws[n].shape[-1]
        full[n] = flat[:, off:off + k * c].reshape(N_DEV, k, c).transpose(1, 0, 2).reshape(k, N_DEV * c)
        off += k * c + (-(k * c)) % D

    loss, grad_x, g = _local_step(x[0], loss_target[0], full, shards)

    lands = [g["w_in_t"], g["w_up_t"]] + [g[n] for n in BIG[2:]]
    blocks = []
    for n in SMALL_SHARDED:
        k, c = ws[n].shape[-2], ws[n].shape[-1]
        b = g[n].reshape(k, N_DEV, c).transpose(1, 0, 2).reshape(N_DEV, k * c)
        blocks.append(jnp.pad(b, ((0, 0), (0, (-(k * c)) % D))))
    for n in REPLICATED:
        b = jnp.broadcast_to(g[n].reshape(1, -1), (N_DEV, g[n].size))
        blocks.append(jnp.pad(b, ((0, 0), (0, (-b.shape[1]) % D))))
    gsm = jnp.concatenate(blocks, axis=1)
    lands += _exchange([jnp.pad(gsm, ((0, 0), (0, SMALL_ROWS * D - gsm.shape[1]))).reshape(N_DEV, SMALL_ROWS, D)])

    grad, delta, new_m, new_v = {}, {}, {}, {}
    for i, n in enumerate(BIG):
        outs = _adamw(lands[i], _to_panel(n, ws[n]), _to_panel(n, ms[n]), _to_panel(n, vs[n]), ADAM_BLOCK[n], "adamw_" + n)
        grad[n], delta[n], new_m[n], new_v[n] = [_from_panel(n, p, ws[n].shape) for p in outs]
    outs = _adamw(lands[len(BIG)], pack_small(ws), pack_small(ms), pack_small(vs), SMALL_ROWS, "adamw_small")
    shapes = [ws[n].shape for n in small]
    for d, p in zip((grad, delta, new_m, new_v), outs):
        d.update(zip(small, _unpack_small(p, shapes)))

    order = ("meta_tokens", "norm_mix_g", "w_in", "w_alpha_up", "b_alpha", "gla_norm_g", "w_gla_o", "conf_dw_w", "conf_dw_b",
             "conf_ln_g", "conf_ln_b", "w_conf_o", "w_out", "norm_ffn_g", "w_up", "ffn_dw_w", "ffn_dw_b", "w_down",
             "final_norm_g")
    loss = lax.psum(loss, ("x", "y", "c"))
    return (loss, grad_x[None], *[grad[n] for n in order], *[delta[n] for n in order], *[new_m[n] for n in order],
            *[new_v[n] for n in order])
```

```python
import functools

import jax
import jax.numpy as jnp
from jax import lax
from jax.experimental import pallas as pl
from jax.experimental.pallas import tpu as pltpu

F32, BF16 = jnp.float32, jnp.bfloat16
SDS = jax.ShapeDtypeStruct

D = 1024
N_META = 16
HEADS = 4
DK, DKH, DV, DVH = 512, 128, 1024, 256
RANK = 16
TAU = 16.0
CONF_K = 31
DFF = 2816
FFN_K = 3
IN_WIDTHS = (DK, DK, DV, DV, RANK, 2 * D, D, D)
RMS_EPS, LN_EPS = 1e-6, 1e-5
ADAM_LR, ADAM_B1, ADAM_B2, ADAM_EPS, ADAM_WD, ADAM_STEP = 0.001, 0.9, 0.999, 1e-08, 0.01, 10

NPROJ = 7 * D
LANES = 128
CH = 128
TM = 640
TM_BIG = 1664
TE = 320
HALO = 32
HALO_F = 16
N_DEV = 8
VMEM_LIMIT = 60 * 1024 * 1024
MESH_T = pl.DeviceIdType.MESH

_ARB1 = pltpu.CompilerParams(dimension_semantics=("arbitrary",), vmem_limit_bytes=VMEM_LIMIT)
_ARB2 = pltpu.CompilerParams(dimension_semantics=("arbitrary", "arbitrary"), vmem_limit_bytes=VMEM_LIMIT)


def _dot(a, b):
    return jnp.dot(a, b, preferred_element_type=F32)


def _dot_nt(a, b):
    return lax.dot_general(a, b, (((1,), (1,)), ((), ())), preferred_element_type=F32)


def _dot_tn(a, b):
    return lax.dot_general(a, b, (((0,), (0,)), ((), ())), preferred_element_type=F32)


def _sigmoid(x):
    return 0.5 * jnp.tanh(0.5 * x) + 0.5


def _rows8(x):
    return x.reshape(x.shape[0] // 8, 8, x.shape[1]).sum(axis=0)


def _row_tile(t, preferred):
    return preferred if t % preferred == 0 else TM


def _row_loop(n_rows, rb, fn, unroll=1):
    def step(i, carry):
        fn(pl.multiple_of(i * rb, rb))
        return carry
    lax.fori_loop(0, n_rows // rb, step, 0, unroll=unroll)


def _resident(shape):
    return pl.BlockSpec(shape, lambda *_: (0,) * len(shape))


def _once(shape):
    return pl.BlockSpec(shape, lambda *_: (0,) * len(shape), pipeline_mode=pl.Buffered(1))


CONV_ROWS, CONV_LANES = 64, 256


def _shift_classes(offset_taps):
    return [(s, [(o - s, j) for o, j in offset_taps if o % 8 == s]) for s in range(8)]


def _shifted(win, s):
    return win if s == 0 else win[s:s + CONV_ROWS + HALO - 8, :]


def _my_place():
    return lax.axis_index("x"), lax.axis_index("y"), lax.axis_index("c")


def _flip(v, bit):
    return 1 - v if bit else v


def _exchange_copies(src_refs, land_refs, scatter, send_sems, recv_sems, local_sems, arrivals):
    x, y, c = _my_place()
    my_idx = 4 * x + 2 * y + c
    local, remote = [], []
    for p, (src, land) in enumerate(zip(src_refs, land_refs)):
        local.append(pltpu.make_async_copy(src.at[my_idx] if scatter[p] else src, land.at[my_idx], local_sems.at[p]))
    for k in range(1, N_DEV):
        px, py, pc = _flip(x, k & 4), _flip(y, k & 2), _flip(c, k & 1)
        p_idx = 4 * px + 2 * py + pc
        for p, (src, land) in enumerate(zip(src_refs, land_refs)):
            s = 7 * p + k - 1
            out = src.at[p_idx] if scatter[p] else src

            def copy(dst):
                return pltpu.make_async_remote_copy(src_ref=out, dst_ref=dst, send_sem=send_sems.at[s],
                                                    recv_sem=recv_sems.at[s], device_id=(px, py, pc), device_id_type=MESH_T)
            remote.append((copy(land.at[my_idx]), copy(land.at[p_idx]) if arrivals else None))
    return local, remote


def _carried_call(core, carry, *, grid, in_specs, out_specs, out_shape, scratch_shapes=(), **kw):
    n_in, n_out, nc, n_scr = len(in_specs), len(out_specs), len(carry), len(scratch_shapes)
    if nc == 0:
        return pl.pallas_call(core, grid=grid, in_specs=in_specs, out_specs=out_specs, out_shape=out_shape,
                              scratch_shapes=list(scratch_shapes), **kw)
    scatter = [sc for _, sc in carry]

    def body(*refs):
        ins, cin = refs[:n_in], refs[n_in:n_in + nc]
        outs, cout = refs[n_in + nc:n_in + nc + n_out], refs[n_in + nc + n_out:n_in + 2 * nc + n_out]
        scr, sems = refs[n_in + 2 * nc + n_out:n_in + 2 * nc + n_out + n_scr], refs[-3:]
        first = functools.reduce(jnp.logical_and, [pl.program_id(a) == 0 for a in range(len(grid))])
        last = functools.reduce(jnp.logical_and, [pl.program_id(a) == grid[a] - 1 for a in range(len(grid))])

        @pl.when(first)
        def _():
            local, remote = _exchange_copies(cin, cout, scatter, *sems, arrivals=False)
            for cp in local:
                cp.start()
            for send, _ in remote:
                send.start()

        core(*ins, *outs, *scr)

        @pl.when(last)
        def _():
            local, remote = _exchange_copies(cin, cout, scatter, *sems, arrivals=True)
            for _, arrival in remote:
                arrival.wait_recv()
            for send, _ in remote:
                send.wait_send()
            for cp in local:
                cp.wait()

    hbm = pl.BlockSpec(memory_space=pl.ANY)
    land_shape = [SDS((N_DEV, *(a.shape[1:] if sc else a.shape)), a.dtype) for a, sc in carry]
    sems = [pltpu.SemaphoreType.DMA((7 * nc,)), pltpu.SemaphoreType.DMA((7 * nc,)), pltpu.SemaphoreType.DMA((nc,))]
    call = pl.pallas_call(body, grid=grid, in_specs=list(in_specs) + [hbm] * nc, out_specs=list(out_specs) + [hbm] * nc,
                          out_shape=list(out_shape) + land_shape, scratch_shapes=list(scratch_shapes) + sems, **kw)
    return lambda *args: call(*args, *[a for a, _ in carry])


def _norm_matmul(h, g, w_t, tn, name, w_extra_t=None, carry=()):
    t, n = h.shape[0], w_t.shape[0]
    tm = _row_tile(t, TM_BIG)
    nt, nb = t // tm, n // tn

    def body(*refs):
        if w_extra_t is None:
            h_ref, g_ref, w_ref, u_ref, p_ref = refs
        else:
            h_ref, g_ref, w_ref, we_ref, u_ref, p_ref, e_ref = refs

        @pl.when(pl.program_id(1) == 0)
        def _():
            def blk(r0):
                x = h_ref[pl.ds(r0, 32), :]
                rinv = lax.rsqrt(jnp.mean(x * x, axis=-1, keepdims=True) + RMS_EPS)
                u_ref[pl.ds(r0, 32), :] = (x * rinv * g_ref[...]).astype(BF16)
            _row_loop(tm, 32, blk, unroll=13)
            if w_extra_t is not None:
                e_ref[...] = _dot_nt(u_ref[...], we_ref[...]).astype(BF16)

        p_ref[...] = _dot_nt(u_ref[...], w_ref[...]).astype(BF16)

    in_specs = [pl.BlockSpec((tm, D), lambda i, j: (i, 0)), _resident((1, D)), pl.BlockSpec((tn, D), lambda i, j: (j, 0))]
    out_specs = [pl.BlockSpec((tm, D), lambda i, j: (i, 0)), pl.BlockSpec((tm, tn), lambda i, j: (i, j))]
    out_shape = [SDS((t, D), BF16), SDS((t, n), BF16)]
    args = [h, g, w_t]
    if w_extra_t is not None:
        in_specs.append(_resident(w_extra_t.shape))
        out_specs.append(pl.BlockSpec((tm, w_extra_t.shape[0]), lambda i, j: (i, 0)))
        out_shape.append(SDS((t, w_extra_t.shape[0]), BF16))
        args.append(w_extra_t)
    return _carried_call(body, carry, name=name, grid=(nt, nb), in_specs=in_specs, out_specs=out_specs,
                         out_shape=out_shape, compiler_params=_ARB2)(*args)


def _in_proj(x, meta, g, w_t, w_a_t, t, carry=()):
    n_real = x.shape[0] + N_META
    n, tn = w_t.shape[0], 1024
    tm = _row_tile(t, TM_BIG)
    nt, nb = t // tm, n // tn

    def body(x_ref, xh_ref, m_ref, g_ref, w_ref, wa_ref, h_ref, u_ref, p_ref, e_ref):
        i = pl.program_id(0)

        @pl.when(pl.program_id(1) == 0)
        def _():
            def rows_of(r0, val):
                gid = i * tm + r0 + lax.broadcasted_iota(jnp.int32, (32, 1), 0)
                val = jnp.where(gid < n_real, val, 0.0)
                h_ref[pl.ds(r0, 32), :] = val
                rinv = lax.rsqrt(jnp.mean(val * val, axis=-1, keepdims=True) + RMS_EPS)
                u_ref[pl.ds(r0, 32), :] = (val * rinv * g_ref[...]).astype(BF16)

            before = jnp.where(i == 0, m_ref[...], xh_ref[...])
            rows_of(0, jnp.concatenate([before, x_ref[0:N_META, :]], axis=0))

            def blk(k, c):
                r0 = pl.multiple_of(k * 32, 32)
                rows_of(r0, x_ref[pl.ds(pl.multiple_of(r0 - N_META, N_META), 32), :])
                return c
            lax.fori_loop(1, tm // 32, blk, 0, unroll=17)
            e_ref[...] = _dot_nt(u_ref[...], wa_ref[...]).astype(BF16)

        p_ref[...] = _dot_nt(u_ref[...], w_ref[...]).astype(BF16)

    row = lambda w: pl.BlockSpec((tm, w), lambda i, j: (i, 0))
    return _carried_call(
        body, carry, name="in_proj", grid=(nt, nb),
        in_specs=[row(D), pl.BlockSpec((N_META, D), lambda i, j: (jnp.maximum(i * (tm // N_META) - 1, 0), 0)),
                  _resident((N_META, D)), _resident((1, D)), pl.BlockSpec((tn, D), lambda i, j: (j, 0)),
                  _resident(w_a_t.shape)],
        out_specs=[row(D), row(D), pl.BlockSpec((tm, tn), lambda i, j: (i, j)), row(w_a_t.shape[0])],
        out_shape=[SDS((t, D), F32), SDS((t, D), BF16), SDS((t, n), BF16), SDS((t, w_a_t.shape[0]), BF16)],
        compiler_params=_ARB2)(x, x, meta, g, w_t, w_a_t)


def _gla_decay(a_ref, wau_ref, ba_ref):
    z = _dot(a_ref[...], wau_ref[...]) + ba_ref[...]
    la = (jnp.minimum(z, 0.0) - jnp.log(1.0 + jnp.exp(-jnp.abs(z)))) * (1.0 / TAU)
    r = lax.broadcasted_iota(jnp.int32, (CH, CH), 0)
    c = lax.broadcasted_iota(jnp.int32, (CH, CH), 1)
    b = _tri_matmul((r >= c).astype(BF16), la)
    mid = jnp.broadcast_to(b[CH // 2:CH // 2 + 1, :], b.shape)
    last = jnp.broadcast_to(b[CH - 1:CH, :], b.shape)
    return z, b, mid, last, r >= c


def _tri_matmul(tri, x):
    n = x.shape[1]
    x1 = x.astype(BF16)
    r1 = x - x1.astype(F32)
    x2 = r1.astype(BF16)
    x3 = (r1 - x2.astype(F32)).astype(BF16)
    y = _dot(tri, jnp.concatenate([x1, x2, x3], axis=1))
    return y[:, 0:n] + y[:, n:2 * n] + y[:, 2 * n:3 * n]


def _gla_fwd(proj, alr, wau, balpha, gn):
    t = proj.shape[0]
    nc = t // CH

    def body(qk_ref, v_ref, r_ref, a_ref, wau_ref, ba_ref, gn_ref, o_ref, og_ref, sall_ref, s_scr):
        @pl.when(pl.program_id(0) == 0)
        def _():
            s_scr[...] = jnp.zeros_like(s_scr)

        sall_ref[0] = s_scr[...]
        _, b, bmid, blast, causal = _gla_decay(a_ref, wau_ref, ba_ref)
        for h in range(HEADS):
            ks = slice(h * DKH, (h + 1) * DKH)
            vs = slice(h * DVH, (h + 1) * DVH)
            bh, mh, lh = b[:, ks], bmid[:, ks], blast[:, ks]
            q = qk_ref[:, ks].astype(F32) * (DKH ** -0.5)
            k = qk_ref[:, DK + h * DKH:DK + (h + 1) * DKH].astype(F32)
            v = v_ref[:, vs]
            qt = (q * jnp.exp(bh - mh)).astype(BF16)
            kt = (k * jnp.exp(mh - bh)).astype(BF16)
            qg = (q * jnp.exp(bh)).astype(BF16)
            kg = (k * jnp.exp(lh - bh)).astype(BF16)
            a = jnp.where(causal, _dot_nt(qt, kt), 0.0)
            st = s_scr[vs, :]
            o = _dot(a.astype(BF16), v) + _dot_nt(qg, st.astype(BF16))
            el = jnp.exp(lh)
            s_scr[vs, :] = st * jnp.concatenate([el, el], axis=0) + _dot_tn(v, kg)
            o_ref[:, vs] = o
            on = o * lax.rsqrt(jnp.mean(o * o, axis=-1, keepdims=True) + RMS_EPS) * gn_ref[:, vs]
            rr = r_ref[:, vs].astype(F32)
            og_ref[:, vs] = (on * (rr * _sigmoid(rr))).astype(BF16)

    return pl.pallas_call(
        body, name="gla_fwd", grid=(nc,),
        in_specs=[pl.BlockSpec((CH, D), lambda c: (c, 0)), pl.BlockSpec((CH, D), lambda c: (c, 1)),
                  pl.BlockSpec((CH, D), lambda c: (c, 6)), pl.BlockSpec((CH, LANES), lambda c: (c, 0)),
                  _resident((LANES, DK)), _resident((1, DK)), _resident((1, DV))],
        out_specs=[pl.BlockSpec((CH, DV), lambda c: (c, 0)), pl.BlockSpec((CH, DV), lambda c: (c, 0)),
                   pl.BlockSpec((1, DV, DKH), lambda c: (c, 0, 0))],
        out_shape=[SDS((t, DV), F32), SDS((t, DV), BF16), SDS((nc, DV, DKH), F32)],
        scratch_shapes=[pltpu.VMEM((DV, DKH), F32)], compiler_params=_ARB1)(proj, proj, proj, alr, wau, balpha, gn)


def _conf_fwd(proj, cw, cb, lg, lb, carry=()):
    t = proj.shape[0]
    nt = t // TE

    def body(c1_ref, c2_ref, cw_ref, cb_ref, lg_ref, lb_ref, cc_ref, cs_ref, cext):
        i = pl.program_id(0)

        @pl.when(i == 0)
        def _():
            cext[0:HALO, :] = jnp.zeros((HALO, D), F32)

        @pl.when(i > 0)
        def _():
            cext[0:HALO, :] = cext[TE:TE + HALO, :]

        def glu(r0):
            c2 = c2_ref[pl.ds(r0, 32), :].astype(F32)
            cext[pl.ds(HALO + r0, 32), :] = c1_ref[pl.ds(r0, 32), :].astype(F32) * _sigmoid(c2)
        _row_loop(TE, 32, glu)

        def conv(r0):
            for part in range(D // CONV_LANES):
                cols = slice(part * CONV_LANES, (part + 1) * CONV_LANES)
                win = cext[pl.ds(r0, CONV_ROWS + HALO), cols]
                acc = jnp.zeros((CONV_ROWS, CONV_LANES), F32) + cb_ref[:, cols]
                for s, taps in _shift_classes([(2 + j, j) for j in range(CONF_K)]):
                    ws = _shifted(win, s)
                    for a8, j in taps:
                        acc = acc + jnp.tile(cw_ref[j, :, cols], (CONV_ROWS // 8, 1)) * ws[a8:a8 + CONV_ROWS, :]
                cc_ref[pl.ds(r0, CONV_ROWS), cols] = acc

        def norm(r0):
            for sub in range(CONV_ROWS // 32):
                rows = pl.ds(r0 + 32 * sub, 32)
                x = cc_ref[rows, :]
                xc = x - jnp.mean(x, axis=-1, keepdims=True)
                var = jnp.mean(xc * xc, axis=-1, keepdims=True)
                ln = xc * lax.rsqrt(var + LN_EPS) * lg_ref[...] + lb_ref[...]
                cs_ref[rows, :] = (ln * _sigmoid(ln)).astype(BF16)

        def step(r0):
            conv(r0)
            norm(r0)
        _row_loop(TE, CONV_ROWS, step)

    return _carried_call(
        body, carry, name="conf_fwd", grid=(nt,),
        in_specs=[pl.BlockSpec((TE, D), lambda i: (i, 2)), pl.BlockSpec((TE, D), lambda i: (i, 3)),
                  _resident((32, 8, D)), _resident((1, D)), _resident((1, D)), _resident((1, D))],
        out_specs=[pl.BlockSpec((TE, D), lambda i: (i, 0)), pl.BlockSpec((TE, D), lambda i: (i, 0))],
        out_shape=[SDS((t, D), F32), SDS((t, D), BF16)],
        scratch_shapes=[pltpu.VMEM((TE + HALO, D), F32)], compiler_params=_ARB1)(proj, proj, cw, cb, lg, lb)


def _mix_fwd(og, cs, proj, h0, wg, wc, wo):
    t = h0.shape[0]
    nt = t // TM

    def body(og_ref, cs_ref, g_ref, h0_ref, wg_ref, wc_ref, wo_ref, brg_ref, brc_ref, mg_ref, h1_ref):
        brg_ref[...] = _dot(og_ref[...], wg_ref[...]).astype(BF16)
        brc_ref[...] = _dot(cs_ref[...], wc_ref[...]).astype(BF16)

        def blk(r0):
            rows = pl.ds(r0, 32)
            gg = g_ref[rows, 0:D].astype(F32)
            gc = g_ref[rows, D:2 * D].astype(F32)
            m = _sigmoid(gg) * brg_ref[rows, :].astype(F32) + _sigmoid(gc) * brc_ref[rows, :].astype(F32)
            mg_ref[rows, :] = m.astype(BF16)
        _row_loop(TM, 32, blk)
        h1_ref[...] = h0_ref[...] + _dot(mg_ref[...], wo_ref[...])

    row = lambda w: pl.BlockSpec((TM, w), lambda i: (i, 0))
    return pl.pallas_call(
        body, name="mix_fwd", grid=(nt,),
        in_specs=[row(D), row(D), pl.BlockSpec((TM, 2 * D), lambda i: (i, 2)), row(D),
                  _once((D, D)), _once((D, D)), _once((D, D))],
        out_specs=[row(D), row(D), row(D), row(D)],
        out_shape=[SDS((t, D), BF16), SDS((t, D), BF16), SDS((t, D), BF16), SDS((t, D), F32)],
        compiler_params=_ARB1)(og, cs, proj, h0, wg, wc, wo)


def _ffn_out(up, fw, h1, wd, gf, target):
    t = h1.shape[0]
    nt = t // TE
    n_real = target.shape[0] + N_META

    def body(a_ref, bv_ref, fw_ref, h1_ref, wd_ref, gf_ref, tg_ref, tb_ref, f_ref, dh2_ref, red_ref, before, hs):
        i = pl.program_id(0)

        @pl.when(i == 0)
        def _():
            before[...] = jnp.zeros_like(before)
            red_ref[...] = jnp.zeros_like(red_ref)

        def conv(r0, win):
            ac = fw_ref[3] + fw_ref[0] * win[14:30, :] + fw_ref[1] * win[15:31, :] + fw_ref[2] * win[16:32, :]
            f_ref[pl.ds(r0, 16), :] = (ac * _sigmoid(ac) * bv_ref[pl.ds(r0, 16), :].astype(F32)).astype(BF16)

        conv(0, jnp.concatenate([before[...], a_ref[0:HALO_F, :].astype(F32)], axis=0))

        def conv_blk(k, c):
            r0 = pl.multiple_of(k * 16, 16)
            conv(r0, a_ref[pl.ds(r0 - HALO_F, 32), :].astype(F32))
            return c
        lax.fori_loop(1, TE // 16, conv_blk, 0)
        before[...] = a_ref[TE - HALO_F:TE, :].astype(F32)
        hs[...] = h1_ref[...] + _dot(f_ref[...], wd_ref[...])

        def head(r0, tg):
            rows = pl.ds(r0, 32)
            h2 = hs[rows, :]
            rinv = lax.rsqrt(jnp.mean(h2 * h2, axis=-1, keepdims=True) + RMS_EPS)
            hh = h2 * rinv
            gid = i * TE + r0 + lax.broadcasted_iota(jnp.int32, (32, 1), 0)
            live = jnp.logical_and(gid >= N_META, gid < n_real)
            err = jnp.where(live, hh * gf_ref[...] - tg, 0.0)
            dy = err * (1.0 / D)
            red_ref[0:8, :] += _rows8(err * err)
            red_ref[8:16, :] += _rows8(dy * hh)
            dhh = dy * gf_ref[...]
            dh2_ref[rows, :] = rinv * (dhh - hh * jnp.mean(dhh * hh, axis=-1, keepdims=True))

        head(0, jnp.concatenate([tb_ref[...], tg_ref[0:N_META, :]], axis=0))

        def blk(k, c):
            r0 = pl.multiple_of(k * 32, 32)
            head(r0, tg_ref[pl.ds(pl.multiple_of(r0 - N_META, N_META), 32), :])
            return c
        lax.fori_loop(1, TE // 32, blk, 0, unroll=9)

    row = lambda w: pl.BlockSpec((TE, w), lambda i: (i, 0))
    return pl.pallas_call(
        body, name="ffn_out", grid=(nt,),
        in_specs=[pl.BlockSpec((TE, DFF), lambda i: (i, 0)), pl.BlockSpec((TE, DFF), lambda i: (i, 1)),
                  _resident((4, 16, DFF)), row(D), _resident((DFF, D)), _resident((1, D)), row(D),
                  pl.BlockSpec((N_META, D), lambda i: (jnp.maximum(i * (TE // N_META) - 1, 0), 0))],
        out_specs=[row(DFF), row(D), _resident((16, D))],
        out_shape=[SDS((t, DFF), BF16), SDS((t, D), F32), SDS((16, D), F32)],
        scratch_shapes=[pltpu.VMEM((HALO_F, DFF), F32), pltpu.VMEM((TE, D), F32)],
        compiler_params=_ARB1)(up, up, fw, h1, wd, gf, target, target)


def _ffn_bwd(dh2, wd, up, fw):
    t = dh2.shape[0]
    nt = t // TE
    hb = TE // HALO_F

    def body(dh_ref, wd_ref, a_ref, ah_ref, bv_ref, fw_ref, dup_ref, dw_ref, dax, dfs):
        i = pl.program_id(0)
        ti = nt - 1 - i

        @pl.when(i == 0)
        def _():
            dax[TE:TE + HALO_F, :] = jnp.zeros((HALO_F, DFF), F32)
            dw_ref[...] = jnp.zeros_like(dw_ref)

        @pl.when(i > 0)
        def _():
            dax[TE:TE + HALO_F, :] = dax[0:HALO_F, :]

        dfs[...] = _dot_nt(dh_ref[...].astype(BF16), wd_ref[...])

        def act(r0, win):
            rows = pl.ds(r0, 16)
            ac = fw_ref[3] + fw_ref[0] * win[14:30, :] + fw_ref[1] * win[15:31, :] + fw_ref[2] * win[16:32, :]
            sg = _sigmoid(ac)
            df = dfs[rows, :]
            dup_ref[rows, DFF:2 * DFF] = (df * ac * sg).astype(BF16)
            dac = df * bv_ref[rows, :].astype(F32) * sg * (1.0 + ac * (1.0 - sg))
            dax[rows, :] = dac
            dw_ref[3] += _rows8(dac)
            for j in range(FFN_K):
                dw_ref[j] += _rows8(dac * win[14 + j:30 + j, :])

        before = jnp.where(ti > 0, ah_ref[...].astype(F32), 0.0)
        act(0, jnp.concatenate([before, a_ref[0:HALO_F, :].astype(F32)], axis=0))

        def act_blk(k, c):
            r0 = pl.multiple_of(k * 16, 16)
            act(r0, a_ref[pl.ds(r0 - HALO_F, 32), :].astype(F32))
            return c
        lax.fori_loop(1, TE // 16, act_blk, 0)

        def convt(r0):
            win = dax[pl.ds(r0, 32), :]
            da = fw_ref[2] * win[0:16, :] + fw_ref[1] * win[1:17, :] + fw_ref[0] * win[2:18, :]
            dup_ref[pl.ds(r0, 16), 0:DFF] = da.astype(BF16)
        _row_loop(TE, 16, convt)

    rev = lambda w: pl.BlockSpec((TE, w), lambda i: (nt - 1 - i, 0))
    return pl.pallas_call(
        body, name="ffn_bwd", grid=(nt,),
        in_specs=[rev(D), _resident((DFF, D)), rev(DFF),
                  pl.BlockSpec((HALO_F, DFF), lambda i: (jnp.maximum((nt - 1 - i) * hb - 1, 0), 0)),
                  pl.BlockSpec((TE, DFF), lambda i: (nt - 1 - i, 1)), _resident((4, 16, DFF))],
        out_specs=[rev(2 * DFF), _resident((4, 8, DFF))],
        out_shape=[SDS((t, 2 * DFF), BF16), SDS((4, 8, DFF), F32)],
        scratch_shapes=[pltpu.VMEM((TE + HALO_F, DFF), F32), pltpu.VMEM((TE, DFF), F32)],
        compiler_params=_ARB1)(dh2, wd, up, up, up, fw)


def _dgrad_norm(dy, w_t, h, g, dres, name, carry=()):
    t, k = dy.shape
    nt = t // TM

    def body(dy_ref, w_ref, h_ref, g_ref, dr_ref, dh_ref, dg_ref, acc):
        @pl.when(pl.program_id(0) == 0)
        def _():
            dg_ref[...] = jnp.zeros_like(dg_ref)

        acc[...] = _dot(dy_ref[...], w_ref[...])

        def blk(r0):
            rows = pl.ds(r0, 32)
            x = h_ref[rows, :]
            rinv = lax.rsqrt(jnp.mean(x * x, axis=-1, keepdims=True) + RMS_EPS)
            hh = x * rinv
            du = acc[rows, :]
            dg_ref[...] += _rows8(du * hh)
            dhh = du * g_ref[...]
            dh_ref[rows, :] = dr_ref[rows, :] + rinv * (dhh - hh * jnp.mean(dhh * hh, axis=-1, keepdims=True))
        _row_loop(TM, 32, blk, unroll=10)

    row = pl.BlockSpec((TM, D), lambda i: (i, 0))
    in_specs = [pl.BlockSpec((TM, k), lambda i: (i, 0)), _once((k, D)), row, _resident((1, D)), row]
    return _carried_call(
        body, carry, name=name, grid=(nt,), in_specs=in_specs, out_specs=[row, _resident((8, D))],
        out_shape=[SDS((t, D), F32), SDS((8, D), F32)],
        scratch_shapes=[pltpu.VMEM((TM, D), F32)], compiler_params=_ARB1)(dy, w_t, h, g, dres)


def _in_dgrad(dy, w_t, h, g, dres, dy_extra, w_extra_t, carry=()):
    t, k = dy.shape
    nt = t // TM

    def body(dy_ref, w_ref, h_ref, g_ref, dr_ref, de_ref, we_ref, dh_ref, dg_ref, acc):
        @pl.when(pl.program_id(0) == 0)
        def _():
            dg_ref[...] = jnp.zeros_like(dg_ref)

        acc[...] = _dot(dy_ref[...], w_ref[...])
        acc[...] += _dot(de_ref[...], we_ref[...])

        def blk(r0):
            rows = pl.ds(r0, 32)
            x = h_ref[rows, :]
            rinv = lax.rsqrt(jnp.mean(x * x, axis=-1, keepdims=True) + RMS_EPS)
            hh = x * rinv
            du = acc[rows, :]
            dg_ref[...] += _rows8(du * hh)
            dhh = du * g_ref[...]
            dh_ref[rows, :] = dr_ref[rows, :] + rinv * (dhh - hh * jnp.mean(dhh * hh, axis=-1, keepdims=True))
        _row_loop(TM, 32, blk, unroll=4)

    tile = lambda w: pl.BlockSpec((TM, w), lambda i: (i, 0))
    return _carried_call(
        body, carry, name="in_dgrad", grid=(nt,),
        in_specs=[tile(k), _once((k, D)), tile(D), _resident((1, D)), tile(D), tile(dy_extra.shape[1]),
                  _once(w_extra_t.shape)],
        out_specs=[tile(D), _resident((8, D))], out_shape=[SDS((t, D), F32), SDS((8, D), F32)],
        scratch_shapes=[pltpu.VMEM((TM, D), F32)], compiler_params=_ARB1)(dy, w_t, h, g, dres, dy_extra, w_extra_t)


def _wgrad(x, dy, tk, name):
    t, k = x.shape
    n = dy.shape[1]
    tm = _row_tile(t, TM_BIG)
    nk, nt = k // tk, t // tm

    def body(x_ref, dy_ref, o_ref, acc):
        @pl.when(pl.program_id(1) == 0)
        def _():
            acc[...] = jnp.zeros_like(acc)
        acc[...] += _dot_tn(x_ref[...], dy_ref[...].astype(BF16))

        @pl.when(pl.program_id(1) == nt - 1)
        def _():
            o_ref[...] = acc[...].astype(BF16)

    return pl.pallas_call(
        body, name=name, grid=(nk, nt),
        in_specs=[pl.BlockSpec((tm, tk), lambda j, i: (i, j)), pl.BlockSpec((tm, n), lambda j, i: (i, 0))],
        out_specs=pl.BlockSpec((tk, n), lambda j, i: (j, 0)), out_shape=SDS((k, n), BF16),
        scratch_shapes=[pltpu.VMEM((tk, n), F32)], compiler_params=_ARB2)(x, dy)


def _mix_bwd(dh1, wo, wg, wc, proj, brg, brc):
    t = dh1.shape[0]
    nt = t // TM

    def body(dh_ref, wo_ref, wg_ref, wc_ref, g_ref, brg_ref, brc_ref, dbg_ref, dbc_ref, dog_ref, dcs_ref, dp_ref, dm):
        dm[...] = _dot_nt(dh_ref[...].astype(BF16), wo_ref[...])

        def blk(r0):
            rows = pl.ds(r0, 32)
            d = dm[rows, :]
            sg = _sigmoid(g_ref[rows, 0:D].astype(F32))
            sc = _sigmoid(g_ref[rows, D:2 * D].astype(F32))
            dbg_ref[rows, :] = (d * sg).astype(BF16)
            dbc_ref[rows, :] = (d * sc).astype(BF16)
            dp_ref[rows, 0:D] = (d * brg_ref[rows, :].astype(F32) * sg * (1.0 - sg)).astype(BF16)
            dp_ref[rows, D:2 * D] = (d * brc_ref[rows, :].astype(F32) * sc * (1.0 - sc)).astype(BF16)
        _row_loop(TM, 32, blk)
        dog_ref[...] = _dot_nt(dbg_ref[...], wg_ref[...]).astype(BF16)
        dcs_ref[...] = _dot_nt(dbc_ref[...], wc_ref[...]).astype(BF16)

    row = pl.BlockSpec((TM, D), lambda i: (i, 0))
    wide = pl.BlockSpec((TM, 2 * D), lambda i: (i, 2))
    return pl.pallas_call(
        body, name="mix_bwd", grid=(nt,),
        in_specs=[row, _once((D, D)), _once((D, D)), _once((D, D)), wide, row, row],
        out_specs=[row, row, row, row, wide],
        out_shape=[SDS((t, D), BF16)] * 4 + [SDS((t, NPROJ), BF16)],
        scratch_shapes=[pltpu.VMEM((TM, D), F32)], compiler_params=_ARB1)(dh1, wo, wg, wc, proj, brg, brc)


def _glapost_bwd(dog, o, proj, gn, dproj):
    t = o.shape[0]
    nt = t // TE

    def body(dog_ref, o_ref, r_ref, gn_ref, dp_in, do_ref, dp_ref, dgn_ref):
        del dp_in

        @pl.when(pl.program_id(0) == 0)
        def _():
            dgn_ref[...] = jnp.zeros_like(dgn_ref)

        def blk(r0):
            rows = pl.ds(r0, 32)
            for h in range(HEADS):
                vs = slice(h * DVH, (h + 1) * DVH)
                x = o_ref[rows, vs]
                rinv = lax.rsqrt(jnp.mean(x * x, axis=-1, keepdims=True) + RMS_EPS)
                oh = x * rinv
                g = gn_ref[:, vs]
                rr = r_ref[rows, vs].astype(F32)
                sr = _sigmoid(rr)
                d = dog_ref[rows, vs].astype(F32)
                dp_ref[rows, vs] = (d * oh * g * sr * (1.0 + rr * (1.0 - sr))).astype(BF16)
                don = d * rr * sr
                dgn_ref[:, vs] += _rows8(don * oh)
                doh = don * g
                do_ref[rows, vs] = (rinv * (doh - oh * jnp.mean(doh * oh, axis=-1, keepdims=True))).astype(BF16)
        _row_loop(TE, 32, blk, unroll=5)

    row = pl.BlockSpec((TE, D), lambda i: (i, 0))
    rcol = pl.BlockSpec((TE, D), lambda i: (i, 6))
    return pl.pallas_call(
        body, name="glapost_bwd", grid=(nt,),
        in_specs=[row, row, rcol, _resident((1, D)), pl.BlockSpec(memory_space=pl.ANY)],
        out_specs=[row, rcol, _resident((8, D))],
        out_shape=[SDS((t, D), BF16), SDS((t, NPROJ), BF16), SDS((8, D), F32)],
        input_output_aliases={4: 1}, compiler_params=_ARB1)(dog, o, proj, gn, dproj)


def _conf_bwd(dcs, cc, proj, cw, lg, lb, dproj, carry=()):
    t = cc.shape[0]
    nt = t // TE

    def body(dcs_ref, cc_ref, c1_ref, c2_ref, cw_ref, lg_ref, lb_ref, dp_in, dp_ref, dw_ref, ds_ref, cval, dext):
        del dp_in
        i = pl.program_id(0)

        @pl.when(i == 0)
        def _():
            dext[TE:TE + HALO, :] = jnp.zeros((HALO, D), F32)
            dw_ref[...] = jnp.zeros_like(dw_ref)
            ds_ref[...] = jnp.zeros_like(ds_ref)

        @pl.when(i > 0)
        def _():
            dext[TE:TE + HALO, :] = dext[0:HALO, :]

        def pre(r0):
            rows = pl.ds(r0, 32)
            cval[rows, :] = c1_ref[rows, :].astype(F32) * _sigmoid(c2_ref[rows, :].astype(F32))
            x = cc_ref[rows, :]
            mu = jnp.mean(x, axis=-1, keepdims=True)
            xc = x - mu
            rstd = lax.rsqrt(jnp.mean(xc * xc, axis=-1, keepdims=True) + LN_EPS)
            xh = xc * rstd
            ln = xh * lg_ref[...] + lb_ref[...]
            sg = _sigmoid(ln)
            dln = dcs_ref[rows, :].astype(F32) * sg * (1.0 + ln * (1.0 - sg))
            ds_ref[0] += _rows8(dln * xh)
            ds_ref[1] += _rows8(dln)
            dxh = dln * lg_ref[...]
            dcc = rstd * (dxh - jnp.mean(dxh, axis=-1, keepdims=True) - xh * jnp.mean(dxh * xh, axis=-1, keepdims=True))
            dext[rows, :] = dcc
            ds_ref[2] += _rows8(dcc)
        _row_loop(TE, 32, pre, unroll=10)

        def convt(r0):
            rows = pl.ds(r0, CONV_ROWS)
            for part in range(D // CONV_LANES):
                cols = slice(part * CONV_LANES, (part + 1) * CONV_LANES)
                wd = dext[pl.ds(r0, CONV_ROWS + HALO), cols]
                cv = cval[rows, cols]
                dc = jnp.zeros((CONV_ROWS, CONV_LANES), F32)
                for s, taps in _shift_classes([(CONF_K - 1 - j, j) for j in range(CONF_K)]):
                    ws = _shifted(wd, s)
                    for a8, j in taps:
                        ahead = ws[a8:a8 + CONV_ROWS, :]
                        dc = dc + jnp.tile(cw_ref[j, :, cols], (CONV_ROWS // 8, 1)) * ahead
                        dw_ref[j, :, cols] += _rows8(cv * ahead)
                c1 = c1_ref[rows, cols].astype(F32)
                s2 = _sigmoid(c2_ref[rows, cols].astype(F32))
                dp_ref[rows, cols] = (dc * s2).astype(BF16)
                dp_ref[rows, D + part * CONV_LANES:D + (part + 1) * CONV_LANES] = (dc * c1 * s2 * (1.0 - s2)).astype(BF16)
        _row_loop(TE, CONV_ROWS, convt)

    rev = lambda col: pl.BlockSpec((TE, D), lambda i: (nt - 1 - i, col))
    return _carried_call(
        body, carry, name="conf_bwd", grid=(nt,),
        in_specs=[rev(0), rev(0), rev(2), rev(3), _resident((32, 8, D)), _resident((1, D)), _resident((1, D)),
                  pl.BlockSpec(memory_space=pl.ANY)],
        out_specs=[pl.BlockSpec((TE, 2 * D), lambda i: (nt - 1 - i, 1)), _resident((32, 8, D)), _resident((3, 8, D))],
        out_shape=[SDS((t, NPROJ), BF16), SDS((32, 8, D), F32), SDS((3, 8, D), F32)],
        scratch_shapes=[pltpu.VMEM((TE, D), F32), pltpu.VMEM((TE + HALO, D), F32)],
        input_output_aliases={7: 0}, compiler_params=_ARB1)(dcs, cc, proj, proj, cw, lg, lb, dproj)


def _gla_bwd(proj, alr, wau, balpha, do, sall, dproj, carry=()):
    t = proj.shape[0]
    nc = t // CH

    def body(qk_ref, v_ref, a_ref, wau_ref, ba_ref, do_ref, s_ref, dp_in, dp_ref, da_ref, dwau_ref, dba_ref, ds_scr, dla_scr):
        del dp_in

        @pl.when(pl.program_id(0) == 0)
        def _():
            ds_scr[...] = jnp.zeros_like(ds_scr)
            dwau_ref[...] = jnp.zeros_like(dwau_ref)
            dba_ref[...] = jnp.zeros_like(dba_ref)

        z, b, bmid, blast, causal = _gla_decay(a_ref, wau_ref, ba_ref)
        dlasts = []
        for h in range(HEADS):
            ks = slice(h * DKH, (h + 1) * DKH)
            vs = slice(h * DVH, (h + 1) * DVH)
            bh, mh, lh = b[:, ks], bmid[:, ks], blast[:, ks]
            q = qk_ref[:, ks].astype(F32) * (DKH ** -0.5)
            k = qk_ref[:, DK + h * DKH:DK + (h + 1) * DKH].astype(F32)
            v = v_ref[:, vs]
            dout = do_ref[:, vs]
            eq, ek, eb, eg, el = jnp.exp(bh - mh), jnp.exp(mh - bh), jnp.exp(bh), jnp.exp(lh - bh), jnp.exp(lh)
            qt, kt = (q * eq).astype(BF16), (k * ek).astype(BF16)
            qg, kg = (q * eb).astype(BF16), (k * eg).astype(BF16)
            st = s_ref[0, vs, :]
            dsn = ds_scr[vs, :]
            st16, dsn16 = st.astype(BF16), dsn.astype(BF16)
            a = jnp.where(causal, _dot_nt(qt, kt), 0.0).astype(BF16)
            da = jnp.where(causal, _dot_nt(dout, v), 0.0).astype(BF16)
            dq_inter = _dot(dout, st16) * eb
            dk_inter = _dot(v, dsn16) * eg
            dq = _dot(da, kt) * eq + dq_inter
            dk = _dot_tn(da, qt) * ek + dk_inter
            dv = _dot_tn(a, dout) + _dot_nt(kg, dsn16)
            dlasts.append(jnp.sum(k * dk_inter, axis=0, keepdims=True) + jnp.sum(st * dsn, axis=0, keepdims=True) * el[0:1, :])
            dla_scr[:, ks] = q * dq - k * dk
            ds_scr[vs, :] = dsn * jnp.concatenate([el, el], axis=0) + _dot_tn(dout, qg)
            dp_ref[:, ks] = (dq * (DKH ** -0.5)).astype(BF16)
            dp_ref[:, DK + h * DKH:DK + (h + 1) * DKH] = dk.astype(BF16)
            dp_ref[:, D + h * DVH:D + (h + 1) * DVH] = dv.astype(BF16)
        r = lax.broadcasted_iota(jnp.int32, (CH, CH), 0)
        c = lax.broadcasted_iota(jnp.int32, (CH, CH), 1)
        dla = _tri_matmul((r <= c).astype(BF16), dla_scr[...]) + jnp.concatenate(dlasts, axis=1)
        dz = (dla * (1.0 / TAU) * _sigmoid(-z)).astype(BF16)
        da_ref[...] = _dot_nt(dz, wau_ref[...]).astype(BF16)
        dwau_ref[...] += _dot_tn(a_ref[...], dz)
        dba_ref[...] += _rows8(dz.astype(F32))

    rev = lambda w, col: pl.BlockSpec((CH, w), lambda c: (nc - 1 - c, col))
    return _carried_call(
        body, carry, name="gla_bwd", grid=(nc,),
        in_specs=[rev(D, 0), rev(D, 1), rev(LANES, 0), _resident((LANES, DK)), _resident((1, DK)), rev(D, 0),
                  pl.BlockSpec((1, DV, DKH), lambda c: (nc - 1 - c, 0, 0)), pl.BlockSpec(memory_space=pl.ANY)],
        out_specs=[rev(2 * D, 0), rev(LANES, 0), _resident((LANES, DK)), _resident((8, DK))],
        out_shape=[SDS((t, NPROJ), BF16), SDS((t, LANES), BF16), SDS((LANES, DK), F32), SDS((8, DK), F32)],
        scratch_shapes=[pltpu.VMEM((DV, DKH), F32), pltpu.VMEM((CH, DK), F32)],
        input_output_aliases={7: 0}, compiler_params=_ARB1)(proj, proj, alr, wau, balpha, do, sall, dproj)


def _all_gather(xs, name):
    n = len(xs)

    def body(*refs):
        x_refs, out_refs = refs[:n], refs[n:2 * n]
        send_sems, recv_sems = refs[2 * n:]
        x, y, c = _my_place()
        me, sibling = (x, y, c), (x, y, 1 - c)
        x_nbr, y_nbr, diagonal = (1 - x, y), (x, 1 - y), (1 - x, 1 - y)

        def slot(p, px, py, pc):
            return out_refs[p].at[4 * px + 2 * py + pc]

        def copy(p, k, block, to, src=None):
            return pltpu.make_async_remote_copy(
                src_ref=slot(p, *block) if src is None else src, dst_ref=slot(p, *block),
                send_sem=send_sems.at[7 * p + k], recv_sem=recv_sems.at[7 * p + k], device_id=to, device_id_type=MESH_T)

        for p in range(n):
            for k, to in enumerate((sibling, (*x_nbr, c), (*y_nbr, c))):
                copy(p, k, me, to, src=x_refs[p]).start()
        for p in range(n):
            @pl.when(c == 1)
            def _():
                copy(p, 2, (*y_nbr, c), me).wait_recv()
                copy(p, 3, (*y_nbr, c), (*x_nbr, c)).start()
                copy(p, 1, (*x_nbr, c), me).wait_recv()

            @pl.when(c == 0)
            def _():
                copy(p, 1, (*x_nbr, c), me).wait_recv()
                copy(p, 3, (*x_nbr, c), (*y_nbr, c)).start()
                copy(p, 2, (*y_nbr, c), me).wait_recv()
        for p in range(n):
            copy(p, 4, (*x_nbr, c), sibling).start()
            copy(p, 5, (*y_nbr, c), sibling).start()
        for p in range(n):
            copy(p, 3, (*diagonal, c), me).wait_recv()
            copy(p, 6, (*diagonal, c), sibling).start()
        for p in range(n):
            copy(p, 0, sibling, me).wait_recv()
            for j, chip in enumerate((x_nbr, y_nbr, diagonal)):
                copy(p, 4 + j, (*chip, 1 - c), me).wait_recv()
        for p in range(n):
            for k in range(7):
                copy(p, k, me, sibling, src=x_refs[p]).wait_send()

    hbm = pl.BlockSpec(memory_space=pl.ANY)
    return pl.pallas_call(
        body, name=name, out_shape=[SDS((N_DEV, *a.shape), a.dtype) for a in xs],
        in_specs=[hbm] * n, out_specs=[hbm] * n,
        scratch_shapes=[pltpu.SemaphoreType.DMA((7 * n,)), pltpu.SemaphoreType.DMA((7 * n,))])(*xs)


def _exchange(gs):
    n = len(gs)

    def body(*refs):
        g_refs, land_refs = refs[:n], refs[n:2 * n]
        send_sems, recv_sems, local_sems = refs[2 * n:]
        x, y, c = _my_place()
        my_idx = 4 * x + 2 * y + c
        mine = [pltpu.make_async_copy(g_refs[p].at[my_idx], land_refs[p].at[my_idx], local_sems.at[p]) for p in range(n)]
        for cp in mine:
            cp.start()
        copies = []
        for k in range(1, N_DEV):
            px, py, pc = _flip(x, k & 4), _flip(y, k & 2), _flip(c, k & 1)
            p_idx = 4 * px + 2 * py + pc
            for p in range(n):
                s = 7 * p + k - 1
                cp = pltpu.make_async_remote_copy(
                    src_ref=g_refs[p].at[p_idx], dst_ref=land_refs[p].at[my_idx], send_sem=send_sems.at[s],
                    recv_sem=recv_sems.at[s], device_id=(px, py, pc), device_id_type=MESH_T)
                cp.start()
                arrival = pltpu.make_async_remote_copy(
                    src_ref=g_refs[p].at[p_idx], dst_ref=land_refs[p].at[p_idx], send_sem=send_sems.at[s],
                    recv_sem=recv_sems.at[s], device_id=(px, py, pc), device_id_type=MESH_T)
                copies.append((cp, arrival))
        for cp, arrival in copies:
            arrival.wait_recv()
        for cp, arrival in copies:
            cp.wait_send()
        for cp in mine:
            cp.wait()

    hbm = pl.BlockSpec(memory_space=pl.ANY)
    return pl.pallas_call(
        body, name="grad_exchange", out_shape=[SDS(g.shape, g.dtype) for g in gs],
        in_specs=[hbm] * n, out_specs=[hbm] * n,
        scratch_shapes=[pltpu.SemaphoreType.DMA((7 * n,)), pltpu.SemaphoreType.DMA((7 * n,)),
                        pltpu.SemaphoreType.DMA((n,))])(*gs)


def _adamw(land, w, m, v, rows_blk, name):
    rows = w.shape[0]

    def body(l_ref, w_ref, m_ref, v_ref, g_ref, d_ref, nm_ref, nv_ref):
        g = l_ref[0].astype(F32)
        for s in range(1, N_DEV):
            g = g + l_ref[s].astype(F32)
        nm = ADAM_B1 * m_ref[...] + (1.0 - ADAM_B1) * g
        nv = ADAM_B2 * v_ref[...] + (1.0 - ADAM_B2) * (g * g)
        m_hat = nm / (1.0 - ADAM_B1 ** ADAM_STEP)
        v_hat = nv / (1.0 - ADAM_B2 ** ADAM_STEP)
        g_ref[...] = g
        d_ref[...] = -ADAM_LR * (m_hat / (jnp.sqrt(v_hat) + ADAM_EPS) + ADAM_WD * w_ref[...])
        nm_ref[...] = nm
        nv_ref[...] = nv

    blk = pl.BlockSpec((rows_blk, D), lambda i: (i, 0))
    return pl.pallas_call(
        body, name=name, grid=(rows // rows_blk,),
        in_specs=[pl.BlockSpec((N_DEV, rows_blk, D), lambda i: (0, i, 0)), blk, blk, blk],
        out_specs=[blk] * 4, out_shape=[SDS((rows, D), F32)] * 4, compiler_params=_ARB1)(land, w, m, v)


BIG = ("w_in", "w_up", "w_down", "w_gla_o", "w_conf_o", "w_out")
BIG_TRANSPOSED = ("w_in", "w_up")
SMALL_SHARDED = ("meta_tokens", "conf_dw_w", "ffn_dw_w", "w_alpha_up")
REPLICATED = ("norm_mix_g", "b_alpha", "gla_norm_g", "conf_dw_b", "conf_ln_g", "conf_ln_b", "norm_ffn_g", "ffn_dw_b",
              "final_norm_g")
N_IN = sum(IN_WIDTHS)
W_IN_ROWS = N_IN // N_DEV
W_IN_PAD = -(-W_IN_ROWS // 16) * 16
ADAM_BLOCK = {"w_in": W_IN_PAD // 3, "w_up": 176, "w_down": 176, "w_gla_o": 128, "w_conf_o": 128, "w_out": 128}
SMALL_ROWS = 32


def _to_panel(name, shard):
    a = shard.reshape(shard.shape[-2], shard.shape[-1])
    if name in BIG_TRANSPOSED:
        a = a.T
    if name == "w_in":
        a = jnp.pad(a, ((0, W_IN_PAD - W_IN_ROWS), (0, 0)))
    return a


def _from_panel(name, panel, shape):
    a = panel[0:W_IN_ROWS] if name == "w_in" else panel
    if name in BIG_TRANSPOSED:
        a = a.T
    return a.reshape(shape)


def _pack_small(arrs):
    flat = jnp.concatenate([jnp.pad(a.reshape(-1), (0, (-a.size) % D)) for a in arrs])
    return jnp.pad(flat, (0, SMALL_ROWS * D - flat.shape[0])).reshape(SMALL_ROWS, D)


def _unpack_small(panel, shapes):
    flat, out, off = panel.reshape(-1), [], 0
    for shp in shapes:
        n = 1
        for s in shp:
            n *= s
        out.append(flat[off:off + n].reshape(shp))
        off += n + (-n) % D
    return out


def _local_step(x, target, w, shards=None):
    dist = shards is not None
    w = dict(w)

    def gather(names):
        return [(shards[n], False) for n in names] if dist else []

    def scatter(*arrs):
        return [(a.reshape(N_DEV, -1, D), True) for a in arrs] if dist else []

    s = x.shape[0]
    n_real = s + N_META
    t = -(-n_real // TM) * TM

    q0, r0, a0, c0 = 0, 2 * DK + DV, 2 * DK + 2 * DV, 2 * DK + 2 * DV + RANK
    wt = w["w_in_t"]
    w_main = jnp.concatenate([wt[q0:r0], wt[c0:N_IN], wt[r0:a0]], axis=0)
    w_a = jnp.pad(wt[a0:c0], ((0, LANES - RANK), (0, 0)))
    wau = jnp.pad(w["w_alpha_up"].astype(BF16), ((0, LANES - RANK), (0, 0)))
    row = lambda name: w[name].reshape(1, -1)
    cw = jnp.broadcast_to(jnp.pad(w["conf_dw_w"], ((0, 32 - CONF_K), (0, 0)))[:, None, :], (32, 8, D))
    fw = jnp.broadcast_to(jnp.concatenate([w["ffn_dw_w"], w["ffn_dw_b"].reshape(1, -1)], axis=0)[:, None, :],
                          (FFN_K + 1, 16, DFF))

    early = ("w_gla_o", "w_conf_o", "w_out", "w_up")
    h0, u1, proj, alr, *landed = _in_proj(x, w["meta_tokens"], row("norm_mix_g"), w_main, w_a, t, carry=gather(early))
    for n, land in zip(early, landed):
        w["w_up_t" if n == "w_up" else n] = land.reshape(-1, D)
    o, og, sall = _gla_fwd(proj, alr, wau, row("b_alpha"), row("gla_norm_g"))
    cc, cs, *landed = _conf_fwd(proj, cw, row("conf_dw_b"), row("conf_ln_g"), row("conf_ln_b"), carry=gather(("w_down",)))
    if dist:
        w["w_down"] = landed[0].reshape(-1, D)
    brg, brc, merged, h1 = _mix_fwd(og, cs, proj, h0, w["w_gla_o"], w["w_conf_o"], w["w_out"])
    u2, up = _norm_matmul(h1, row("norm_ffn_g"), w["w_up_t"], 512, "up_proj")
    f, dh2, red = _ffn_out(up, fw, h1, w["w_down"], row("final_norm_g"), target)
    loss = 0.5 / D * jnp.sum(red[0:8])

    g = {"final_norm_g": jnp.sum(red[8:16], axis=0)}
    dup, dfw = _ffn_bwd(dh2, w["w_down"], up, fw)
    g["ffn_dw_w"] = jnp.sum(dfw[0:FFN_K], axis=1)
    g["ffn_dw_b"] = jnp.sum(dfw[3], axis=0)
    g["w_down"] = _wgrad(f, dh2, 1408, "wgrad_down")
    dh1, dg2, *landed = _dgrad_norm(dup, w["w_up_t"], h1, row("norm_ffn_g"), dh2, "up_dgrad", carry=scatter(g["w_down"]))
    if dist:
        g["w_down"] = landed[0]
    g["norm_ffn_g"] = jnp.sum(dg2, axis=0)
    g["w_up_t"] = _wgrad(dup, u2, 1408, "wgrad_up")
    dbrg, dbrc, dog, dcs, dproj = _mix_bwd(dh1, w["w_out"], w["w_gla_o"], w["w_conf_o"], proj, brg, brc)
    g["w_out"] = _wgrad(merged, dh1, 1024, "wgrad_out")
    g["w_gla_o"] = _wgrad(og, dbrg, 1024, "wgrad_gla_o")
    g["w_conf_o"] = _wgrad(cs, dbrc, 1024, "wgrad_conf_o")
    do, dproj, dgn = _glapost_bwd(dog, o, proj, row("gla_norm_g"), dproj)
    g["gla_norm_g"] = jnp.sum(dgn, axis=0)
    dproj, dcw, dst, *landed = _conf_bwd(dcs, cc, proj, cw, row("conf_ln_g"), row("conf_ln_b"), dproj,
                                         carry=scatter(g["w_up_t"]))
    if dist:
        g["w_up_t"] = landed[0]
    g["conf_dw_w"] = jnp.sum(dcw[0:CONF_K], axis=1)
    g["conf_ln_g"], g["conf_ln_b"], g["conf_dw_b"] = jnp.sum(dst[0], axis=0), jnp.sum(dst[1], axis=0), jnp.sum(dst[2], axis=0)
    dproj, dalr, dwau, dba, *landed = _gla_bwd(proj, alr, wau, row("b_alpha"), do, sall, dproj,
                                               carry=scatter(g["w_out"], g["w_gla_o"], g["w_conf_o"]))
    if dist:
        g["w_out"], g["w_gla_o"], g["w_conf_o"] = landed
    g["w_alpha_up"] = dwau[0:RANK]
    g["b_alpha"] = jnp.sum(dba, axis=0)
    dw_main = _wgrad(dproj, u1, 1024, "wgrad_in")
    dw_a = _wgrad(dalr, u1, LANES, "wgrad_alr")
    g["w_in_t"] = jnp.concatenate([dw_main[0:r0], dw_main[NPROJ - DV:NPROJ], dw_a[0:RANK], dw_main[r0:NPROJ - DV]], axis=0)
    w_in_blocks = []
    if dist:
        pad = ((0, 0), (0, W_IN_PAD - W_IN_ROWS), (0, 0))
        w_in_blocks = [(jnp.pad(g["w_in_t"].reshape(N_DEV, W_IN_ROWS, D), pad), True)]
    dh0, dg1, *landed = _in_dgrad(dproj, w_main, h0, row("norm_mix_g"), dh1, dalr, w_a, carry=w_in_blocks)
    if dist:
        g["w_in_t"] = landed[0]
    g["norm_mix_g"] = jnp.sum(dg1, axis=0)
    g["meta_tokens"] = dh0[0:N_META]
    return loss, dh0[N_META:n_real], g


def kernel(x, meta_tokens, norm_mix_g, w_in, w_alpha_up, b_alpha, gla_norm_g, w_gla_o, conf_dw_w, conf_dw_b, conf_ln_g, conf_ln_b, w_conf_o, w_out, norm_ffn_g, w_up, ffn_dw_w, ffn_dw_b, w_down, final_norm_g, loss_target, m_meta_tokens, m_norm_mix_g, m_w_in, m_w_alpha_up, m_b_alpha, m_gla_norm_g, m_w_gla_o, m_conf_dw_w, m_conf_dw_b, m_conf_ln_g, m_conf_ln_b, m_w_conf_o, m_w_out, m_norm_ffn_g, m_w_up, m_ffn_dw_w, m_ffn_dw_b, m_w_down, m_final_norm_g, v_meta_tokens, v_norm_mix_g, v_w_in, v_w_alpha_up, v_b_alpha, v_gla_norm_g, v_w_gla_o, v_conf_dw_w, v_conf_dw_b, v_conf_ln_g, v_conf_ln_b, v_w_conf_o, v_w_out, v_norm_ffn_g, v_w_up, v_ffn_dw_w, v_ffn_dw_b, v_w_down, v_final_norm_g):
    ws = dict(meta_tokens=meta_tokens, norm_mix_g=norm_mix_g, w_in=w_in, w_alpha_up=w_alpha_up, b_alpha=b_alpha,
              gla_norm_g=gla_norm_g, w_gla_o=w_gla_o, conf_dw_w=conf_dw_w, conf_dw_b=conf_dw_b, conf_ln_g=conf_ln_g,
              conf_ln_b=conf_ln_b, w_conf_o=w_conf_o, w_out=w_out, norm_ffn_g=norm_ffn_g, w_up=w_up, ffn_dw_w=ffn_dw_w,
              ffn_dw_b=ffn_dw_b, w_down=w_down, final_norm_g=final_norm_g)
    ms = dict(meta_tokens=m_meta_tokens, norm_mix_g=m_norm_mix_g, w_in=m_w_in, w_alpha_up=m_w_alpha_up, b_alpha=m_b_alpha,
              gla_norm_g=m_gla_norm_g, w_gla_o=m_w_gla_o, conf_dw_w=m_conf_dw_w, conf_dw_b=m_conf_dw_b,
              conf_ln_g=m_conf_ln_g, conf_ln_b=m_conf_ln_b, w_conf_o=m_w_conf_o, w_out=m_w_out, norm_ffn_g=m_norm_ffn_g,
              w_up=m_w_up, ffn_dw_w=m_ffn_dw_w, ffn_dw_b=m_ffn_dw_b, w_down=m_w_down, final_norm_g=m_final_norm_g)
    vs = dict(meta_tokens=v_meta_tokens, norm_mix_g=v_norm_mix_g, w_in=v_w_in, w_alpha_up=v_w_alpha_up, b_alpha=v_b_alpha,
              gla_norm_g=v_gla_norm_g, w_gla_o=v_w_gla_o, conf_dw_w=v_conf_dw_w, conf_dw_b=v_conf_dw_b,
              conf_ln_g=v_conf_ln_g, conf_ln_b=v_conf_ln_b, w_conf_o=v_w_conf_o, w_out=v_w_out, norm_ffn_g=v_norm_ffn_g,
              w_up=v_w_up, ffn_dw_w=v_ffn_dw_w, ffn_dw_b=v_ffn_dw_b, w_down=v_w_down, final_norm_g=v_final_norm_g)
    small = SMALL_SHARDED + REPLICATED
    pack_small = lambda d: _pack_small([d[n] for n in small])

    shards = {n: _to_panel(n, ws[n]).astype(BF16) for n in BIG}
    own = [shards["w_in"], pack_small(ws)]
    my_idx = 4 * lax.axis_index("x") + 2 * lax.axis_index("y") + lax.axis_index("c")
    gathered = [lax.dynamic_update_slice(full_, mine[None], (my_idx, 0, 0))
                for full_, mine in zip(_all_gather(own, "weight_gather"), own)]
    full = {n: ws[n].reshape(-1) for n in REPLICATED}
    full["w_in_t"] = gathered[0][:, 0:W_IN_ROWS].reshape(N_IN, D)
    flat, off = gathered[1].reshape(N_DEV, -1), 0
    for n in SMALL_SHARDED:
        k, c = ws[n].shape[-2], ws[n].shape[-1]
        full[n] = flat[:, off:off + k * c].reshape(N_DEV, k, c).transpose(1, 0, 2).reshape(k, N_DEV * c)
        off += k * c + (-(k * c)) % D

    loss, grad_x, g = _local_step(x[0], loss_target[0], full, shards)

    lands = [g["w_in_t"], g["w_up_t"]] + [g[n] for n in BIG[2:]]
    blocks = []
    for n in SMALL_SHARDED:
        k, c = ws[n].shape[-2], ws[n].shape[-1]
        b = g[n].reshape(k, N_DEV, c).transpose(1, 0, 2).reshape(N_DEV, k * c)
        blocks.append(jnp.pad(b, ((0, 0), (0, (-(k * c)) % D))))
    for n in REPLICATED:
        b = jnp.broadcast_to(g[n].reshape(1, -1), (N_DEV, g[n].size))
        blocks.append(jnp.pad(b, ((0, 0), (0, (-b.shape[1]) % D))))
    gsm = jnp.concatenate(blocks, axis=1)
    lands += _exchange([jnp.pad(gsm, ((0, 0), (0, SMALL_ROWS * D - gsm.shape[1]))).reshape(N_DEV, SMALL_ROWS, D)])

    grad, delta, new_m, new_v = {}, {}, {}, {}
    for i, n in enumerate(BIG):
        outs = _adamw(lands[i], _to_panel(n, ws[n]), _to_panel(n, ms[n]), _to_panel(n, vs[n]), ADAM_BLOCK[n], "adamw_" + n)
        grad[n], delta[n], new_m[n], new_v[n] = [_from_panel(n, p, ws[n].shape) for p in outs]
    outs = _adamw(lands[len(BIG)], pack_small(ws), pack_small(ms), pack_small(vs), SMALL_ROWS, "adamw_small")
    shapes = [ws[n].shape for n in small]
    for d, p in zip((grad, delta, new_m, new_v), outs):
        d.update(zip(small, _unpack_small(p, shapes)))

    order = ("meta_tokens", "norm_mix_g", "w_in", "w_alpha_up", "b_alpha", "gla_norm_g", "w_gla_o", "conf_dw_w", "conf_dw_b",
             "conf_ln_g", "conf_ln_b", "w_conf_o", "w_out", "norm_ffn_g", "w_up", "ffn_dw_w", "ffn_dw_b", "w_down",
             "final_norm_g")
    loss = lax.psum(loss, ("x", "y", "c"))
    return (loss, grad_x[None], *[grad[n] for n in order], *[delta[n] for n in order], *[new_m[n] for n in order],
            *[new_v[n] for n in order])
```

```python
import functools

import jax
import jax.numpy as jnp
from jax import lax
from jax.experimental import pallas as pl
from jax.experimental.pallas import tpu as pltpu

F32, BF16 = jnp.float32, jnp.bfloat16
SDS = jax.ShapeDtypeStruct

D = 1024
N_META = 16
HEADS = 4
DK, DKH, DV, DVH = 512, 128, 1024, 256
RANK = 16
TAU = 16.0
CONF_K = 31
DFF = 2816
FFN_K = 3
IN_WIDTHS = (DK, DK, DV, DV, RANK, 2 * D, D, D)
RMS_EPS, LN_EPS = 1e-6, 1e-5
ADAM_LR, ADAM_B1, ADAM_B2, ADAM_EPS, ADAM_WD, ADAM_STEP = 0.001, 0.9, 0.999, 1e-08, 0.01, 10

NPROJ = 7 * D
LANES = 128
CH = 128
TM = 640
TM_BIG = 1664
TE = 320
HALO = 32
HALO_F = 16
N_DEV = 8
VMEM_LIMIT = 60 * 1024 * 1024
MESH_T = pl.DeviceIdType.MESH

_ARB1 = pltpu.CompilerParams(dimension_semantics=("arbitrary",), vmem_limit_bytes=VMEM_LIMIT)
_ARB2 = pltpu.CompilerParams(dimension_semantics=("arbitrary", "arbitrary"), vmem_limit_bytes=VMEM_LIMIT)


def _dot(a, b):
    return jnp.dot(a, b, preferred_element_type=F32)


def _dot_nt(a, b):
    return lax.dot_general(a, b, (((1,), (1,)), ((), ())), preferred_element_type=F32)


def _dot_tn(a, b):
    return lax.dot_general(a, b, (((0,), (0,)), ((), ())), preferred_element_type=F32)


def _sigmoid(x):
    return 0.5 * jnp.tanh(0.5 * x) + 0.5


def _rows8(x):
    return x.reshape(x.shape[0] // 8, 8, x.shape[1]).sum(axis=0)


def _row_tile(t, preferred):
    return preferred if t % preferred == 0 else TM


def _row_loop(n_rows, rb, fn, unroll=1):
    def step(i, carry):
        fn(pl.multiple_of(i * rb, rb))
        return carry
    lax.fori_loop(0, n_rows // rb, step, 0, unroll=unroll)


def _resident(shape):
    return pl.BlockSpec(shape, lambda *_: (0,) * len(shape))


def _once(shape):
    return pl.BlockSpec(shape, lambda *_: (0,) * len(shape), pipeline_mode=pl.Buffered(1))


CONV_ROWS, CONV_LANES = 64, 256


def _shift_classes(offset_taps):
    return [(s, [(o - s, j) for o, j in offset_taps if o % 8 == s]) for s in range(8)]


def _shifted(win, s):
    return win if s == 0 else win[s:s + CONV_ROWS + HALO - 8, :]


def _my_place():
    return lax.axis_index("x"), lax.axis_index("y"), lax.axis_index("c")


def _flip(v, bit):
    return 1 - v if bit else v


def _exchange_copies(src_refs, land_refs, scatter, send_sems, recv_sems, local_sems, arrivals):
    x, y, c = _my_place()
    my_idx = 4 * x + 2 * y + c
    local, remote = [], []
    for p, (src, land) in enumerate(zip(src_refs, land_refs)):
        local.append(pltpu.make_async_copy(src.at[my_idx] if scatter[p] else src, land.at[my_idx], local_sems.at[p]))
    for k in range(1, N_DEV):
        px, py, pc = _flip(x, k & 4), _flip(y, k & 2), _flip(c, k & 1)
        p_idx = 4 * px + 2 * py + pc
        for p, (src, land) in enumerate(zip(src_refs, land_refs)):
            s = 7 * p + k - 1
            out = src.at[p_idx] if scatter[p] else src

            def copy(dst):
                return pltpu.make_async_remote_copy(src_ref=out, dst_ref=dst, send_sem=send_sems.at[s],
                                                    recv_sem=recv_sems.at[s], device_id=(px, py, pc), device_id_type=MESH_T)
            remote.append((copy(land.at[my_idx]), copy(land.at[p_idx]) if arrivals else None))
    return local, remote


def _carried_call(core, carry, *, grid, in_specs, out_specs, out_shape, scratch_shapes=(), **kw):
    n_in, n_out, nc, n_scr = len(in_specs), len(out_specs), len(carry), len(scratch_shapes)
    if nc == 0:
        return pl.pallas_call(core, grid=grid, in_specs=in_specs, out_specs=out_specs, out_shape=out_shape,
                              scratch_shapes=list(scratch_shapes), **kw)
    scatter = [sc for _, sc in carry]

    def body(*refs):
        ins, cin = refs[:n_in], refs[n_in:n_in + nc]
        outs, cout = refs[n_in + nc:n_in + nc + n_out], refs[n_in + nc + n_out:n_in + 2 * nc + n_out]
        scr, sems = refs[n_in + 2 * nc + n_out:n_in + 2 * nc + n_out + n_scr], refs[-3:]
        first = functools.reduce(jnp.logical_and, [pl.program_id(a) == 0 for a in range(len(grid))])
        last = functools.reduce(jnp.logical_and, [pl.program_id(a) == grid[a] - 1 for a in range(len(grid))])

        @pl.when(first)
        def _():
            local, remote = _exchange_copies(cin, cout, scatter, *sems, arrivals=False)
            for cp in local:
                cp.start()
            for send, _ in remote:
                send.start()

        core(*ins, *outs, *scr)

        @pl.when(last)
        def _():
            local, remote = _exchange_copies(cin, cout, scatter, *sems, arrivals=True)
            for _, arrival in remote:
                arrival.wait_recv()
            for send, _ in remote:
                send.wait_send()
            for cp in local:
                cp.wait()

    hbm = pl.BlockSpec(memory_space=pl.ANY)
    land_shape = [SDS((N_DEV, *(a.shape[1:] if sc else a.shape)), a.dtype) for a, sc in carry]
    sems = [pltpu.SemaphoreType.DMA((7 * nc,)), pltpu.SemaphoreType.DMA((7 * nc,)), pltpu.SemaphoreType.DMA((nc,))]
    call = pl.pallas_call(body, grid=grid, in_specs=list(in_specs) + [hbm] * nc, out_specs=list(out_specs) + [hbm] * nc,
                          out_shape=list(out_shape) + land_shape, scratch_shapes=list(scratch_shapes) + sems, **kw)
    return lambda *args: call(*args, *[a for a, _ in carry])


def _norm_matmul(h, g, w_t, tn, name, w_extra_t=None, carry=()):
    t, n = h.shape[0], w_t.shape[0]
    tm = _row_tile(t, TM_BIG)
    nt, nb = t // tm, n // tn

    def body(*refs):
        if w_extra_t is None:
            h_ref, g_ref, w_ref, u_ref, p_ref = refs
        else:
            h_ref, g_ref, w_ref, we_ref, u_ref, p_ref, e_ref = refs

        @pl.when(pl.program_id(1) == 0)
        def _():
            def blk(r0):
                x = h_ref[pl.ds(r0, 32), :]
                rinv = lax.rsqrt(jnp.mean(x * x, axis=-1, keepdims=True) + RMS_EPS)
                u_ref[pl.ds(r0, 32), :] = (x * rinv * g_ref[...]).astype(BF16)
            _row_loop(tm, 32, blk, unroll=13)
            if w_extra_t is not None:
                e_ref[...] = _dot_nt(u_ref[...], we_ref[...]).astype(BF16)

        p_ref[...] = _dot_nt(u_ref[...], w_ref[...]).astype(BF16)

    in_specs = [pl.BlockSpec((tm, D), lambda i, j: (i, 0)), _resident((1, D)), pl.BlockSpec((tn, D), lambda i, j: (j, 0))]
    out_specs = [pl.BlockSpec((tm, D), lambda i, j: (i, 0)), pl.BlockSpec((tm, tn), lambda i, j: (i, j))]
    out_shape = [SDS((t, D), BF16), SDS((t, n), BF16)]
    args = [h, g, w_t]
    if w_extra_t is not None:
        in_specs.append(_resident(w_extra_t.shape))
        out_specs.append(pl.BlockSpec((tm, w_extra_t.shape[0]), lambda i, j: (i, 0)))
        out_shape.append(SDS((t, w_extra_t.shape[0]), BF16))
        args.append(w_extra_t)
    return _carried_call(body, carry, name=name, grid=(nt, nb), in_specs=in_specs, out_specs=out_specs,
                         out_shape=out_shape, compiler_params=_ARB2)(*args)


def _in_proj(x, meta, g, w_t, w_a_t, t, carry=()):
    n_real = x.shape[0] + N_META
    n, tn = w_t.shape[0], 1024
    tm = _row_tile(t, TM_BIG)
    nt, nb = t // tm, n // tn

    def body(x_ref, xh_ref, m_ref, g_ref, w_ref, wa_ref, h_ref, u_ref, p_ref, e_ref):
        i = pl.program_id(0)

        @pl.when(pl.program_id(1) == 0)
        def _():
            def rows_of(r0, val):
                gid = i * tm + r0 + lax.broadcasted_iota(jnp.int32, (32, 1), 0)
                val = jnp.where(gid < n_real, val, 0.0)
                h_ref[pl.ds(r0, 32), :] = val
                rinv = lax.rsqrt(jnp.mean(val * val, axis=-1, keepdims=True) + RMS_EPS)
                u_ref[pl.ds(r0, 32), :] = (val * rinv * g_ref[...]).astype(BF16)

            before = jnp.where(i == 0, m_ref[...], xh_ref[...])
            rows_of(0, jnp.concatenate([before, x_ref[0:N_META, :]], axis=0))

            def blk(k, c):
                r0 = pl.multiple_of(k * 32, 32)
                rows_of(r0, x_ref[pl.ds(pl.multiple_of(r0 - N_META, N_META), 32), :])
                return c
            lax.fori_loop(1, tm // 32, blk, 0, unroll=17)
            e_ref[...] = _dot_nt(u_ref[...], wa_ref[...]).astype(BF16)

        p_ref[...] = _dot_nt(u_ref[...], w_ref[...]).astype(BF16)

    row = lambda w: pl.BlockSpec((tm, w), lambda i, j: (i, 0))
    return _carried_call(
        body, carry, name="in_proj", grid=(nt, nb),
        in_specs=[row(D), pl.BlockSpec((N_META, D), lambda i, j: (jnp.maximum(i * (tm // N_META) - 1, 0), 0)),
                  _resident((N_META, D)), _resident((1, D)), pl.BlockSpec((tn, D), lambda i, j: (j, 0)),
                  _resident(w_a_t.shape)],
        out_specs=[row(D), row(D), pl.BlockSpec((tm, tn), lambda i, j: (i, j)), row(w_a_t.shape[0])],
        out_shape=[SDS((t, D), F32), SDS((t, D), BF16), SDS((t, n), BF16), SDS((t, w_a_t.shape[0]), BF16)],
        compiler_params=_ARB2)(x, x, meta, g, w_t, w_a_t)


def _gla_decay(a_ref, wau_ref, ba_ref):
    z = _dot(a_ref[...], wau_ref[...]) + ba_ref[...]
    la = (jnp.minimum(z, 0.0) - jnp.log(1.0 + jnp.exp(-jnp.abs(z)))) * (1.0 / TAU)
    r = lax.broadcasted_iota(jnp.int32, (CH, CH), 0)
    c = lax.broadcasted_iota(jnp.int32, (CH, CH), 1)
    b = _tri_matmul((r >= c).astype(BF16), la)
    mid = jnp.broadcast_to(b[CH // 2:CH // 2 + 1, :], b.shape)
    last = jnp.broadcast_to(b[CH - 1:CH, :], b.shape)
    return z, b, mid, last, r >= c


def _tri_matmul(tri, x):
    n = x.shape[1]
    x1 = x.astype(BF16)
    r1 = x - x1.astype(F32)
    x2 = r1.astype(BF16)
    x3 = (r1 - x2.astype(F32)).astype(BF16)
    y = _dot(tri, jnp.concatenate([x1, x2, x3], axis=1))
    return y[:, 0:n] + y[:, n:2 * n] + y[:, 2 * n:3 * n]


def _gla_fwd(proj, alr, wau, balpha, gn, carry=()):
    t = proj.shape[0]
    nc = t // CH

    def body(qk_ref, v_ref, r_ref, a_ref, wau_ref, ba_ref, gn_ref, o_ref, og_ref, sall_ref, s_scr):
        @pl.when(pl.program_id(0) == 0)
        def _():
            s_scr[...] = jnp.zeros_like(s_scr)

        sall_ref[0] = s_scr[...]
        _, b, bmid, blast, causal = _gla_decay(a_ref, wau_ref, ba_ref)
        for h in range(HEADS):
            ks = slice(h * DKH, (h + 1) * DKH)
            vs = slice(h * DVH, (h + 1) * DVH)
            bh, mh, lh = b[:, ks], bmid[:, ks], blast[:, ks]
            q = qk_ref[:, ks].astype(F32) * (DKH ** -0.5)
            k = qk_ref[:, DK + h * DKH:DK + (h + 1) * DKH].astype(F32)
            v = v_ref[:, vs]
            qt = (q * jnp.exp(bh - mh)).astype(BF16)
            kt = (k * jnp.exp(mh - bh)).astype(BF16)
            qg = (q * jnp.exp(bh)).astype(BF16)
            kg = (k * jnp.exp(lh - bh)).astype(BF16)
            a = jnp.where(causal, _dot_nt(qt, kt), 0.0)
            st = s_scr[vs, :]
            o = _dot(a.astype(BF16), v) + _dot_nt(qg, st.astype(BF16))
            el = jnp.exp(lh)
            s_scr[vs, :] = st * jnp.concatenate([el, el], axis=0) + _dot_tn(v, kg)
            o_ref[:, vs] = o
            on = o * lax.rsqrt(jnp.mean(o * o, axis=-1, keepdims=True) + RMS_EPS) * gn_ref[:, vs]
            rr = r_ref[:, vs].astype(F32)
            og_ref[:, vs] = (on * (rr * _sigmoid(rr))).astype(BF16)

    return _carried_call(
        body, carry, name="gla_fwd", grid=(nc,),
        in_specs=[pl.BlockSpec((CH, D), lambda c: (c, 0)), pl.BlockSpec((CH, D), lambda c: (c, 1)),
                  pl.BlockSpec((CH, D), lambda c: (c, 6)), pl.BlockSpec((CH, LANES), lambda c: (c, 0)),
                  _resident((LANES, DK)), _resident((1, DK)), _resident((1, DV))],
        out_specs=[pl.BlockSpec((CH, DV), lambda c: (c, 0)), pl.BlockSpec((CH, DV), lambda c: (c, 0)),
                   pl.BlockSpec((1, DV, DKH), lambda c: (c, 0, 0))],
        out_shape=[SDS((t, DV), F32), SDS((t, DV), BF16), SDS((nc, DV, DKH), F32)],
        scratch_shapes=[pltpu.VMEM((DV, DKH), F32)], compiler_params=_ARB1)(proj, proj, proj, alr, wau, balpha, gn)


def _conf_fwd(proj, cw, cb, lg, lb, carry=()):
    t = proj.shape[0]
    nt = t // TE

    def body(c1_ref, c2_ref, cw_ref, cb_ref, lg_ref, lb_ref, cc_ref, cs_ref, cext):
        i = pl.program_id(0)

        @pl.when(i == 0)
        def _():
            cext[0:HALO, :] = jnp.zeros((HALO, D), F32)

        @pl.when(i > 0)
        def _():
            cext[0:HALO, :] = cext[TE:TE + HALO, :]

        def glu(r0):
            c2 = c2_ref[pl.ds(r0, 32), :].astype(F32)
            cext[pl.ds(HALO + r0, 32), :] = c1_ref[pl.ds(r0, 32), :].astype(F32) * _sigmoid(c2)
        _row_loop(TE, 32, glu)

        def conv(r0):
            for part in range(D // CONV_LANES):
                cols = slice(part * CONV_LANES, (part + 1) * CONV_LANES)
                win = cext[pl.ds(r0, CONV_ROWS + HALO), cols]
                acc = jnp.zeros((CONV_ROWS, CONV_LANES), F32) + cb_ref[:, cols]
                for s, taps in _shift_classes([(2 + j, j) for j in range(CONF_K)]):
                    ws = _shifted(win, s)
                    for a8, j in taps:
                        acc = acc + jnp.tile(cw_ref[j, :, cols], (CONV_ROWS // 8, 1)) * ws[a8:a8 + CONV_ROWS, :]
                cc_ref[pl.ds(r0, CONV_ROWS), cols] = acc

        def norm(r0):
            for sub in range(CONV_ROWS // 32):
                rows = pl.ds(r0 + 32 * sub, 32)
                x = cc_ref[rows, :]
                xc = x - jnp.mean(x, axis=-1, keepdims=True)
                var = jnp.mean(xc * xc, axis=-1, keepdims=True)
                ln = xc * lax.rsqrt(var + LN_EPS) * lg_ref[...] + lb_ref[...]
                cs_ref[rows, :] = (ln * _sigmoid(ln)).astype(BF16)

        def step(r0):
            conv(r0)
            norm(r0)
        _row_loop(TE, CONV_ROWS, step)

    return _carried_call(
        body, carry, name="conf_fwd", grid=(nt,),
        in_specs=[pl.BlockSpec((TE, D), lambda i: (i, 2)), pl.BlockSpec((TE, D), lambda i: (i, 3)),
                  _resident((32, 8, D)), _resident((1, D)), _resident((1, D)), _resident((1, D))],
        out_specs=[pl.BlockSpec((TE, D), lambda i: (i, 0)), pl.BlockSpec((TE, D), lambda i: (i, 0))],
        out_shape=[SDS((t, D), F32), SDS((t, D), BF16)],
        scratch_shapes=[pltpu.VMEM((TE + HALO, D), F32)], compiler_params=_ARB1)(proj, proj, cw, cb, lg, lb)


def _mix_fwd(og, cs, proj, h0, wg, wc, wo):
    t = h0.shape[0]
    nt = t // TM

    def body(og_ref, cs_ref, g_ref, h0_ref, wg_ref, wc_ref, wo_ref, brg_ref, brc_ref, mg_ref, h1_ref):
        brg_ref[...] = _dot(og_ref[...], wg_ref[...]).astype(BF16)
        brc_ref[...] = _dot(cs_ref[...], wc_ref[...]).astype(BF16)

        def blk(r0):
            rows = pl.ds(r0, 32)
            gg = g_ref[rows, 0:D].astype(F32)
            gc = g_ref[rows, D:2 * D].astype(F32)
            m = _sigmoid(gg) * brg_ref[rows, :].astype(F32) + _sigmoid(gc) * brc_ref[rows, :].astype(F32)
            mg_ref[rows, :] = m.astype(BF16)
        _row_loop(TM, 32, blk)
        h1_ref[...] = h0_ref[...] + _dot(mg_ref[...], wo_ref[...])

    row = lambda w: pl.BlockSpec((TM, w), lambda i: (i, 0))
    return pl.pallas_call(
        body, name="mix_fwd", grid=(nt,),
        in_specs=[row(D), row(D), pl.BlockSpec((TM, 2 * D), lambda i: (i, 2)), row(D),
                  _once((D, D)), _once((D, D)), _once((D, D))],
        out_specs=[row(D), row(D), row(D), row(D)],
        out_shape=[SDS((t, D), BF16), SDS((t, D), BF16), SDS((t, D), BF16), SDS((t, D), F32)],
        compiler_params=_ARB1)(og, cs, proj, h0, wg, wc, wo)


def _ffn_out(up, fw, h1, wd, gf, target):
    t = h1.shape[0]
    nt = t // TE
    n_real = target.shape[0] + N_META

    def body(a_ref, bv_ref, fw_ref, h1_ref, wd_ref, gf_ref, tg_ref, tb_ref, f_ref, dh2_ref, red_ref, before, hs):
        i = pl.program_id(0)

        @pl.when(i == 0)
        def _():
            before[...] = jnp.zeros_like(before)
            red_ref[...] = jnp.zeros_like(red_ref)

        def conv(r0, win):
            ac = fw_ref[3] + fw_ref[0] * win[14:30, :] + fw_ref[1] * win[15:31, :] + fw_ref[2] * win[16:32, :]
            f_ref[pl.ds(r0, 16), :] = (ac * _sigmoid(ac) * bv_ref[pl.ds(r0, 16), :].astype(F32)).astype(BF16)

        conv(0, jnp.concatenate([before[...], a_ref[0:HALO_F, :].astype(F32)], axis=0))

        def conv_blk(k, c):
            r0 = pl.multiple_of(k * 16, 16)
            conv(r0, a_ref[pl.ds(r0 - HALO_F, 32), :].astype(F32))
            return c
        lax.fori_loop(1, TE // 16, conv_blk, 0)
        before[...] = a_ref[TE - HALO_F:TE, :].astype(F32)
        hs[...] = h1_ref[...] + _dot(f_ref[...], wd_ref[...])

        def head(r0, tg):
            rows = pl.ds(r0, 32)
            h2 = hs[rows, :]
            rinv = lax.rsqrt(jnp.mean(h2 * h2, axis=-1, keepdims=True) + RMS_EPS)
            hh = h2 * rinv
            gid = i * TE + r0 + lax.broadcasted_iota(jnp.int32, (32, 1), 0)
            live = jnp.logical_and(gid >= N_META, gid < n_real)
            err = jnp.where(live, hh * gf_ref[...] - tg, 0.0)
            dy = err * (1.0 / D)
            red_ref[0:8, :] += _rows8(err * err)
            red_ref[8:16, :] += _rows8(dy * hh)
            dhh = dy * gf_ref[...]
            dh2_ref[rows, :] = rinv * (dhh - hh * jnp.mean(dhh * hh, axis=-1, keepdims=True))

        head(0, jnp.concatenate([tb_ref[...], tg_ref[0:N_META, :]], axis=0))

        def blk(k, c):
            r0 = pl.multiple_of(k * 32, 32)
            head(r0, tg_ref[pl.ds(pl.multiple_of(r0 - N_META, N_META), 32), :])
            return c
        lax.fori_loop(1, TE // 32, blk, 0, unroll=9)

    row = lambda w: pl.BlockSpec((TE, w), lambda i: (i, 0))
    return pl.pallas_call(
        body, name="ffn_out", grid=(nt,),
        in_specs=[pl.BlockSpec((TE, DFF), lambda i: (i, 0)), pl.BlockSpec((TE, DFF), lambda i: (i, 1)),
                  _resident((4, 16, DFF)), row(D), _resident((DFF, D)), _resident((1, D)), row(D),
                  pl.BlockSpec((N_META, D), lambda i: (jnp.maximum(i * (TE // N_META) - 1, 0), 0))],
        out_specs=[row(DFF), row(D), _resident((16, D))],
        out_shape=[SDS((t, DFF), BF16), SDS((t, D), F32), SDS((16, D), F32)],
        scratch_shapes=[pltpu.VMEM((HALO_F, DFF), F32), pltpu.VMEM((TE, D), F32)],
        compiler_params=_ARB1)(up, up, fw, h1, wd, gf, target, target)


def _ffn_bwd(dh2, wd, up, fw):
    t = dh2.shape[0]
    nt = t // TE
    hb = TE // HALO_F

    def body(dh_ref, wd_ref, a_ref, ah_ref, bv_ref, fw_ref, dup_ref, dw_ref, dax, dfs):
        i = pl.program_id(0)
        ti = nt - 1 - i

        @pl.when(i == 0)
        def _():
            dax[TE:TE + HALO_F, :] = jnp.zeros((HALO_F, DFF), F32)
            dw_ref[...] = jnp.zeros_like(dw_ref)

        @pl.when(i > 0)
        def _():
            dax[TE:TE + HALO_F, :] = dax[0:HALO_F, :]

        dfs[...] = _dot_nt(dh_ref[...].astype(BF16), wd_ref[...])

        def act(r0, win):
            rows = pl.ds(r0, 16)
            ac = fw_ref[3] + fw_ref[0] * win[14:30, :] + fw_ref[1] * win[15:31, :] + fw_ref[2] * win[16:32, :]
            sg = _sigmoid(ac)
            df = dfs[rows, :]
            dup_ref[rows, DFF:2 * DFF] = (df * ac * sg).astype(BF16)
            dac = df * bv_ref[rows, :].astype(F32) * sg * (1.0 + ac * (1.0 - sg))
            dax[rows, :] = dac
            dw_ref[3] += _rows8(dac)
            for j in range(FFN_K):
                dw_ref[j] += _rows8(dac * win[14 + j:30 + j, :])

        before = jnp.where(ti > 0, ah_ref[...].astype(F32), 0.0)
        act(0, jnp.concatenate([before, a_ref[0:HALO_F, :].astype(F32)], axis=0))

        def act_blk(k, c):
            r0 = pl.multiple_of(k * 16, 16)
            act(r0, a_ref[pl.ds(r0 - HALO_F, 32), :].astype(F32))
            return c
        lax.fori_loop(1, TE // 16, act_blk, 0)

        def convt(r0):
            win = dax[pl.ds(r0, 32), :]
            da = fw_ref[2] * win[0:16, :] + fw_ref[1] * win[1:17, :] + fw_ref[0] * win[2:18, :]
            dup_ref[pl.ds(r0, 16), 0:DFF] = da.astype(BF16)
        _row_loop(TE, 16, convt)

    rev = lambda w: pl.BlockSpec((TE, w), lambda i: (nt - 1 - i, 0))
    return pl.pallas_call(
        body, name="ffn_bwd", grid=(nt,),
        in_specs=[rev(D), _resident((DFF, D)), rev(DFF),
                  pl.BlockSpec((HALO_F, DFF), lambda i: (jnp.maximum((nt - 1 - i) * hb - 1, 0), 0)),
                  pl.BlockSpec((TE, DFF), lambda i: (nt - 1 - i, 1)), _resident((4, 16, DFF))],
        out_specs=[rev(2 * DFF), _resident((4, 8, DFF))],
        out_shape=[SDS((t, 2 * DFF), BF16), SDS((4, 8, DFF), F32)],
        scratch_shapes=[pltpu.VMEM((TE + HALO_F, DFF), F32), pltpu.VMEM((TE, DFF), F32)],
        compiler_params=_ARB1)(dh2, wd, up, up, up, fw)


def _dgrad_norm(dy, w_t, h, g, dres, name, carry=()):
    t, k = dy.shape
    nt = t // TM

    def body(dy_ref, w_ref, h_ref, g_ref, dr_ref, dh_ref, dg_ref, acc):
        @pl.when(pl.program_id(0) == 0)
        def _():
            dg_ref[...] = jnp.zeros_like(dg_ref)

        acc[...] = _dot(dy_ref[...], w_ref[...])

        def blk(r0):
            rows = pl.ds(r0, 32)
            x = h_ref[rows, :]
            rinv = lax.rsqrt(jnp.mean(x * x, axis=-1, keepdims=True) + RMS_EPS)
            hh = x * rinv
            du = acc[rows, :]
            dg_ref[...] += _rows8(du * hh)
            dhh = du * g_ref[...]
            dh_ref[rows, :] = dr_ref[rows, :] + rinv * (dhh - hh * jnp.mean(dhh * hh, axis=-1, keepdims=True))
        _row_loop(TM, 32, blk, unroll=10)

    row = pl.BlockSpec((TM, D), lambda i: (i, 0))
    in_specs = [pl.BlockSpec((TM, k), lambda i: (i, 0)), _once((k, D)), row, _resident((1, D)), row]
    return _carried_call(
        body, carry, name=name, grid=(nt,), in_specs=in_specs, out_specs=[row, _resident((8, D))],
        out_shape=[SDS((t, D), F32), SDS((8, D), F32)],
        scratch_shapes=[pltpu.VMEM((TM, D), F32)], compiler_params=_ARB1)(dy, w_t, h, g, dres)


def _in_dgrad(dy, w_t, h, g, dres, dy_extra, w_extra_t, carry=()):
    t, k = dy.shape
    nt = t // TM

    def body(dy_ref, w_ref, h_ref, g_ref, dr_ref, de_ref, we_ref, dh_ref, dg_ref, acc):
        @pl.when(pl.program_id(0) == 0)
        def _():
            dg_ref[...] = jnp.zeros_like(dg_ref)

        acc[...] = _dot(dy_ref[...], w_ref[...])
        acc[...] += _dot(de_ref[...], we_ref[...])

        def blk(r0):
            rows = pl.ds(r0, 32)
            x = h_ref[rows, :]
            rinv = lax.rsqrt(jnp.mean(x * x, axis=-1, keepdims=True) + RMS_EPS)
            hh = x * rinv
            du = acc[rows, :]
            dg_ref[...] += _rows8(du * hh)
            dhh = du * g_ref[...]
            dh_ref[rows, :] = dr_ref[rows, :] + rinv * (dhh - hh * jnp.mean(dhh * hh, axis=-1, keepdims=True))
        _row_loop(TM, 32, blk, unroll=4)

    tile = lambda w: pl.BlockSpec((TM, w), lambda i: (i, 0))
    return _carried_call(
        body, carry, name="in_dgrad", grid=(nt,),
        in_specs=[tile(k), _once((k, D)), tile(D), _resident((1, D)), tile(D), tile(dy_extra.shape[1]),
                  _once(w_extra_t.shape)],
        out_specs=[tile(D), _resident((8, D))], out_shape=[SDS((t, D), F32), SDS((8, D), F32)],
        scratch_shapes=[pltpu.VMEM((TM, D), F32)], compiler_params=_ARB1)(dy, w_t, h, g, dres, dy_extra, w_extra_t)


def _wgrad(x, dy, tk, name):
    t, k = x.shape
    n = dy.shape[1]
    tm = _row_tile(t, TM_BIG)
    nk, nt = k // tk, t // tm

    def body(x_ref, dy_ref, o_ref, acc):
        @pl.when(pl.program_id(1) == 0)
        def _():
            acc[...] = jnp.zeros_like(acc)
        acc[...] += _dot_tn(x_ref[...], dy_ref[...].astype(BF16))

        @pl.when(pl.program_id(1) == nt - 1)
        def _():
            o_ref[...] = acc[...].astype(BF16)

    return pl.pallas_call(
        body, name=name, grid=(nk, nt),
        in_specs=[pl.BlockSpec((tm, tk), lambda j, i: (i, j)), pl.BlockSpec((tm, n), lambda j, i: (i, 0))],
        out_specs=pl.BlockSpec((tk, n), lambda j, i: (j, 0)), out_shape=SDS((k, n), BF16),
        scratch_shapes=[pltpu.VMEM((tk, n), F32)], compiler_params=_ARB2)(x, dy)


def _mix_bwd(dh1, wo, wg, wc, proj, brg, brc):
    t = dh1.shape[0]
    nt = t // TM

    def body(dh_ref, wo_ref, wg_ref, wc_ref, g_ref, brg_ref, brc_ref, dbg_ref, dbc_ref, dog_ref, dcs_ref, dp_ref, dm):
        dm[...] = _dot_nt(dh_ref[...].astype(BF16), wo_ref[...])

        def blk(r0):
            rows = pl.ds(r0, 32)
            d = dm[rows, :]
            sg = _sigmoid(g_ref[rows, 0:D].astype(F32))
            sc = _sigmoid(g_ref[rows, D:2 * D].astype(F32))
            dbg_ref[rows, :] = (d * sg).astype(BF16)
            dbc_ref[rows, :] = (d * sc).astype(BF16)
            dp_ref[rows, 0:D] = (d * brg_ref[rows, :].astype(F32) * sg * (1.0 - sg)).astype(BF16)
            dp_ref[rows, D:2 * D] = (d * brc_ref[rows, :].astype(F32) * sc * (1.0 - sc)).astype(BF16)
        _row_loop(TM, 32, blk)
        dog_ref[...] = _dot_nt(dbg_ref[...], wg_ref[...]).astype(BF16)
        dcs_ref[...] = _dot_nt(dbc_ref[...], wc_ref[...]).astype(BF16)

    row = pl.BlockSpec((TM, D), lambda i: (i, 0))
    wide = pl.BlockSpec((TM, 2 * D), lambda i: (i, 2))
    return pl.pallas_call(
        body, name="mix_bwd", grid=(nt,),
        in_specs=[row, _once((D, D)), _once((D, D)), _once((D, D)), wide, row, row],
        out_specs=[row, row, row, row, wide],
        out_shape=[SDS((t, D), BF16)] * 4 + [SDS((t, NPROJ), BF16)],
        scratch_shapes=[pltpu.VMEM((TM, D), F32)], compiler_params=_ARB1)(dh1, wo, wg, wc, proj, brg, brc)


def _glapost_bwd(dog, o, proj, gn, dproj):
    t = o.shape[0]
    nt = t // TE

    def body(dog_ref, o_ref, r_ref, gn_ref, dp_in, do_ref, dp_ref, dgn_ref):
        del dp_in

        @pl.when(pl.program_id(0) == 0)
        def _():
            dgn_ref[...] = jnp.zeros_like(dgn_ref)

        def blk(r0):
            rows = pl.ds(r0, 32)
            for h in range(HEADS):
                vs = slice(h * DVH, (h + 1) * DVH)
                x = o_ref[rows, vs]
                rinv = lax.rsqrt(jnp.mean(x * x, axis=-1, keepdims=True) + RMS_EPS)
                oh = x * rinv
                g = gn_ref[:, vs]
                rr = r_ref[rows, vs].astype(F32)
                sr = _sigmoid(rr)
                d = dog_ref[rows, vs].astype(F32)
                dp_ref[rows, vs] = (d * oh * g * sr * (1.0 + rr * (1.0 - sr))).astype(BF16)
                don = d * rr * sr
                dgn_ref[:, vs] += _rows8(don * oh)
                doh = don * g
                do_ref[rows, vs] = (rinv * (doh - oh * jnp.mean(doh * oh, axis=-1, keepdims=True))).astype(BF16)
        _row_loop(TE, 32, blk, unroll=5)

    row = pl.BlockSpec((TE, D), lambda i: (i, 0))
    rcol = pl.BlockSpec((TE, D), lambda i: (i, 6))
    return pl.pallas_call(
        body, name="glapost_bwd", grid=(nt,),
        in_specs=[row, row, rcol, _resident((1, D)), pl.BlockSpec(memory_space=pl.ANY)],
        out_specs=[row, rcol, _resident((8, D))],
        out_shape=[SDS((t, D), BF16), SDS((t, NPROJ), BF16), SDS((8, D), F32)],
        input_output_aliases={4: 1}, compiler_params=_ARB1)(dog, o, proj, gn, dproj)


def _conf_bwd(dcs, cc, proj, cw, lg, lb, dproj, carry=()):
    t = cc.shape[0]
    nt = t // TE

    def body(dcs_ref, cc_ref, c1_ref, c2_ref, cw_ref, lg_ref, lb_ref, dp_in, dp_ref, dw_ref, ds_ref, cval, dext):
        del dp_in
        i = pl.program_id(0)

        @pl.when(i == 0)
        def _():
            dext[TE:TE + HALO, :] = jnp.zeros((HALO, D), F32)
            dw_ref[...] = jnp.zeros_like(dw_ref)
            ds_ref[...] = jnp.zeros_like(ds_ref)

        @pl.when(i > 0)
        def _():
            dext[TE:TE + HALO, :] = dext[0:HALO, :]

        def pre(r0):
            rows = pl.ds(r0, 32)
            cval[rows, :] = c1_ref[rows, :].astype(F32) * _sigmoid(c2_ref[rows, :].astype(F32))
            x = cc_ref[rows, :]
            mu = jnp.mean(x, axis=-1, keepdims=True)
            xc = x - mu
            rstd = lax.rsqrt(jnp.mean(xc * xc, axis=-1, keepdims=True) + LN_EPS)
            xh = xc * rstd
            ln = xh * lg_ref[...] + lb_ref[...]
            sg = _sigmoid(ln)
            dln = dcs_ref[rows, :].astype(F32) * sg * (1.0 + ln * (1.0 - sg))
            ds_ref[0] += _rows8(dln * xh)
            ds_ref[1] += _rows8(dln)
            dxh = dln * lg_ref[...]
            dcc = rstd * (dxh - jnp.mean(dxh, axis=-1, keepdims=True) - xh * jnp.mean(dxh * xh, axis=-1, keepdims=True))
            dext[rows, :] = dcc
            ds_ref[2] += _rows8(dcc)
        _row_loop(TE, 32, pre, unroll=10)

        def convt(r0):
            rows = pl.ds(r0, CONV_ROWS)
            for part in range(D // CONV_LANES):
                cols = slice(part * CONV_LANES, (part + 1) * CONV_LANES)
                wd = dext[pl.ds(r0, CONV_ROWS + HALO), cols]
                cv = cval[rows, cols]
                dc = jnp.zeros((CONV_ROWS, CONV_LANES), F32)
                for s, taps in _shift_classes([(CONF_K - 1 - j, j) for j in range(CONF_K)]):
                    ws = _shifted(wd, s)
                    for a8, j in taps:
                        ahead = ws[a8:a8 + CONV_ROWS, :]
                        dc = dc + jnp.tile(cw_ref[j, :, cols], (CONV_ROWS // 8, 1)) * ahead
                        dw_ref[j, :, cols] += _rows8(cv * ahead)
                c1 = c1_ref[rows, cols].astype(F32)
                s2 = _sigmoid(c2_ref[rows, cols].astype(F32))
                dp_ref[rows, cols] = (dc * s2).astype(BF16)
                dp_ref[rows, D + part * CONV_LANES:D + (part + 1) * CONV_LANES] = (dc * c1 * s2 * (1.0 - s2)).astype(BF16)
        _row_loop(TE, CONV_ROWS, convt)

    rev = lambda col: pl.BlockSpec((TE, D), lambda i: (nt - 1 - i, col))
    return _carried_call(
        body, carry, name="conf_bwd", grid=(nt,),
        in_specs=[rev(0), rev(0), rev(2), rev(3), _resident((32, 8, D)), _resident((1, D)), _resident((1, D)),
                  pl.BlockSpec(memory_space=pl.ANY)],
        out_specs=[pl.BlockSpec((TE, 2 * D), lambda i: (nt - 1 - i, 1)), _resident((32, 8, D)), _resident((3, 8, D))],
        out_shape=[SDS((t, NPROJ), BF16), SDS((32, 8, D), F32), SDS((3, 8, D), F32)],
        scratch_shapes=[pltpu.VMEM((TE, D), F32), pltpu.VMEM((TE + HALO, D), F32)],
        input_output_aliases={7: 0}, compiler_params=_ARB1)(dcs, cc, proj, proj, cw, lg, lb, dproj)


def _gla_bwd(proj, alr, wau, balpha, do, sall, dproj, carry=()):
    t = proj.shape[0]
    nc = t // CH

    def body(qk_ref, v_ref, a_ref, wau_ref, ba_ref, do_ref, s_ref, dp_in, dp_ref, da_ref, dwau_ref, dba_ref, ds_scr, dla_scr):
        del dp_in

        @pl.when(pl.program_id(0) == 0)
        def _():
            ds_scr[...] = jnp.zeros_like(ds_scr)
            dwau_ref[...] = jnp.zeros_like(dwau_ref)
            dba_ref[...] = jnp.zeros_like(dba_ref)

        z, b, bmid, blast, causal = _gla_decay(a_ref, wau_ref, ba_ref)
        dlasts = []
        for h in range(HEADS):
            ks = slice(h * DKH, (h + 1) * DKH)
            vs = slice(h * DVH, (h + 1) * DVH)
            bh, mh, lh = b[:, ks], bmid[:, ks], blast[:, ks]
            q = qk_ref[:, ks].astype(F32) * (DKH ** -0.5)
            k = qk_ref[:, DK + h * DKH:DK + (h + 1) * DKH].astype(F32)
            v = v_ref[:, vs]
            dout = do_ref[:, vs]
            eq, ek, eb, eg, el = jnp.exp(bh - mh), jnp.exp(mh - bh), jnp.exp(bh), jnp.exp(lh - bh), jnp.exp(lh)
            qt, kt = (q * eq).astype(BF16), (k * ek).astype(BF16)
            qg, kg = (q * eb).astype(BF16), (k * eg).astype(BF16)
            st = s_ref[0, vs, :]
            dsn = ds_scr[vs, :]
            st16, dsn16 = st.astype(BF16), dsn.astype(BF16)
            a = jnp.where(causal, _dot_nt(qt, kt), 0.0).astype(BF16)
            da = jnp.where(causal, _dot_nt(dout, v), 0.0).astype(BF16)
            dq_inter = _dot(dout, st16) * eb
            dk_inter = _dot(v, dsn16) * eg
            dq = _dot(da, kt) * eq + dq_inter
            dk = _dot_tn(da, qt) * ek + dk_inter
            dv = _dot_tn(a, dout) + _dot_nt(kg, dsn16)
            dlasts.append(jnp.sum(k * dk_inter, axis=0, keepdims=True) + jnp.sum(st * dsn, axis=0, keepdims=True) * el[0:1, :])
            dla_scr[:, ks] = q * dq - k * dk
            ds_scr[vs, :] = dsn * jnp.concatenate([el, el], axis=0) + _dot_tn(dout, qg)
            dp_ref[:, ks] = (dq * (DKH ** -0.5)).astype(BF16)
            dp_ref[:, DK + h * DKH:DK + (h + 1) * DKH] = dk.astype(BF16)
            dp_ref[:, D + h * DVH:D + (h + 1) * DVH] = dv.astype(BF16)
        r = lax.broadcasted_iota(jnp.int32, (CH, CH), 0)
        c = lax.broadcasted_iota(jnp.int32, (CH, CH), 1)
        dla = _tri_matmul((r <= c).astype(BF16), dla_scr[...]) + jnp.concatenate(dlasts, axis=1)
        dz = (dla * (1.0 / TAU) * _sigmoid(-z)).astype(BF16)
        da_ref[...] = _dot_nt(dz, wau_ref[...]).astype(BF16)
        dwau_ref[...] += _dot_tn(a_ref[...], dz)
        dba_ref[...] += _rows8(dz.astype(F32))

    rev = lambda w, col: pl.BlockSpec((CH, w), lambda c: (nc - 1 - c, col))
    return _carried_call(
        body, carry, name="gla_bwd", grid=(nc,),
        in_specs=[rev(D, 0), rev(D, 1), rev(LANES, 0), _resident((LANES, DK)), _resident((1, DK)), rev(D, 0),
                  pl.BlockSpec((1, DV, DKH), lambda c: (nc - 1 - c, 0, 0)), pl.BlockSpec(memory_space=pl.ANY)],
        out_specs=[rev(2 * D, 0), rev(LANES, 0), _resident((LANES, DK)), _resident((8, DK))],
        out_shape=[SDS((t, NPROJ), BF16), SDS((t, LANES), BF16), SDS((LANES, DK), F32), SDS((8, DK), F32)],
        scratch_shapes=[pltpu.VMEM((DV, DKH), F32), pltpu.VMEM((CH, DK), F32)],
        input_output_aliases={7: 0}, compiler_params=_ARB1)(proj, proj, alr, wau, balpha, do, sall, dproj)


def _all_gather(xs, name):
    n = len(xs)

    def body(*refs):
        x_refs, out_refs = refs[:n], refs[n:2 * n]
        send_sems, recv_sems = refs[2 * n:]
        x, y, c = _my_place()
        me, sibling = (x, y, c), (x, y, 1 - c)
        x_nbr, y_nbr, diagonal = (1 - x, y), (x, 1 - y), (1 - x, 1 - y)

        def slot(p, px, py, pc):
            return out_refs[p].at[4 * px + 2 * py + pc]

        def copy(p, k, block, to, src=None):
            return pltpu.make_async_remote_copy(
                src_ref=slot(p, *block) if src is None else src, dst_ref=slot(p, *block),
                send_sem=send_sems.at[7 * p + k], recv_sem=recv_sems.at[7 * p + k], device_id=to, device_id_type=MESH_T)

        for p in range(n):
            for k, to in enumerate((sibling, (*x_nbr, c), (*y_nbr, c))):
                copy(p, k, me, to, src=x_refs[p]).start()
        for p in range(n):
            @pl.when(c == 1)
            def _():
                copy(p, 2, (*y_nbr, c), me).wait_recv()
                copy(p, 3, (*y_nbr, c), (*x_nbr, c)).start()
                copy(p, 1, (*x_nbr, c), me).wait_recv()

            @pl.when(c == 0)
            def _():
                copy(p, 1, (*x_nbr, c), me).wait_recv()
                copy(p, 3, (*x_nbr, c), (*y_nbr, c)).start()
                copy(p, 2, (*y_nbr, c), me).wait_recv()
        for p in range(n):
            copy(p, 4, (*x_nbr, c), sibling).start()
            copy(p, 5, (*y_nbr, c), sibling).start()
        for p in range(n):
            copy(p, 3, (*diagonal, c), me).wait_recv()
            copy(p, 6, (*diagonal, c), sibling).start()
        for p in range(n):
            copy(p, 0, sibling, me).wait_recv()
            for j, chip in enumerate((x_nbr, y_nbr, diagonal)):
                copy(p, 4 + j, (*chip, 1 - c), me).wait_recv()
        for p in range(n):
            for k in range(7):
                copy(p, k, me, sibling, src=x_refs[p]).wait_send()

    hbm = pl.BlockSpec(memory_space=pl.ANY)
    return pl.pallas_call(
        body, name=name, out_shape=[SDS((N_DEV, *a.shape), a.dtype) for a in xs],
        in_specs=[hbm] * n, out_specs=[hbm] * n,
        scratch_shapes=[pltpu.SemaphoreType.DMA((7 * n,)), pltpu.SemaphoreType.DMA((7 * n,))])(*xs)


def _exchange(gs):
    n = len(gs)

    def body(*refs):
        g_refs, land_refs = refs[:n], refs[n:2 * n]
        send_sems, recv_sems, local_sems = refs[2 * n:]
        x, y, c = _my_place()
        my_idx = 4 * x + 2 * y + c
        mine = [pltpu.make_async_copy(g_refs[p].at[my_idx], land_refs[p].at[my_idx], local_sems.at[p]) for p in range(n)]
        for cp in mine:
            cp.start()
        copies = []
        for k in range(1, N_DEV):
            px, py, pc = _flip(x, k & 4), _flip(y, k & 2), _flip(c, k & 1)
            p_idx = 4 * px + 2 * py + pc
            for p in range(n):
                s = 7 * p + k - 1
                cp = pltpu.make_async_remote_copy(
                    src_ref=g_refs[p].at[p_idx], dst_ref=land_refs[p].at[my_idx], send_sem=send_sems.at[s],
                    recv_sem=recv_sems.at[s], device_id=(px, py, pc), device_id_type=MESH_T)
                cp.start()
                arrival = pltpu.make_async_remote_copy(
                    src_ref=g_refs[p].at[p_idx], dst_ref=land_refs[p].at[p_idx], send_sem=send_sems.at[s],
                    recv_sem=recv_sems.at[s], device_id=(px, py, pc), device_id_type=MESH_T)
                copies.append((cp, arrival))
        for cp, arrival in copies:
            arrival.wait_recv()
        for cp, arrival in copies:
            cp.wait_send()
        for cp in mine:
            cp.wait()

    hbm = pl.BlockSpec(memory_space=pl.ANY)
    return pl.pallas_call(
        body, name="grad_exchange", out_shape=[SDS(g.shape, g.dtype) for g in gs],
        in_specs=[hbm] * n, out_specs=[hbm] * n,
        scratch_shapes=[pltpu.SemaphoreType.DMA((7 * n,)), pltpu.SemaphoreType.DMA((7 * n,)),
                        pltpu.SemaphoreType.DMA((n,))])(*gs)


def _adamw(land, w, m, v, rows_blk, name):
    rows = w.shape[0]

    def body(l_ref, w_ref, m_ref, v_ref, g_ref, d_ref, nm_ref, nv_ref):
        g = l_ref[0].astype(F32)
        for s in range(1, N_DEV):
            g = g + l_ref[s].astype(F32)
        nm = ADAM_B1 * m_ref[...] + (1.0 - ADAM_B1) * g
        nv = ADAM_B2 * v_ref[...] + (1.0 - ADAM_B2) * (g * g)
        m_hat = nm / (1.0 - ADAM_B1 ** ADAM_STEP)
        v_hat = nv / (1.0 - ADAM_B2 ** ADAM_STEP)
        g_ref[...] = g
        d_ref[...] = -ADAM_LR * (m_hat / (jnp.sqrt(v_hat) + ADAM_EPS) + ADAM_WD * w_ref[...])
        nm_ref[...] = nm
        nv_ref[...] = nv

    blk = pl.BlockSpec((rows_blk, D), lambda i: (i, 0))
    return pl.pallas_call(
        body, name=name, grid=(rows // rows_blk,),
        in_specs=[pl.BlockSpec((N_DEV, rows_blk, D), lambda i: (0, i, 0)), blk, blk, blk],
        out_specs=[blk] * 4, out_shape=[SDS((rows, D), F32)] * 4, compiler_params=_ARB1)(land, w, m, v)


BIG = ("w_in", "w_up", "w_down", "w_gla_o", "w_conf_o", "w_out")
BIG_TRANSPOSED = ("w_in", "w_up")
SMALL_SHARDED = ("meta_tokens", "conf_dw_w", "ffn_dw_w", "w_alpha_up")
REPLICATED = ("norm_mix_g", "b_alpha", "gla_norm_g", "conf_dw_b", "conf_ln_g", "conf_ln_b", "norm_ffn_g", "ffn_dw_b",
              "final_norm_g")
N_IN = sum(IN_WIDTHS)
W_IN_ROWS = N_IN // N_DEV
W_IN_PAD = -(-W_IN_ROWS // 16) * 16
ADAM_BLOCK = {"w_in": W_IN_PAD // 3, "w_up": 176, "w_down": 176, "w_gla_o": 128, "w_conf_o": 128, "w_out": 128}
SMALL_ROWS = 32


def _to_panel(name, shard):
    a = shard.reshape(shard.shape[-2], shard.shape[-1])
    if name in BIG_TRANSPOSED:
        a = a.T
    if name == "w_in":
        a = jnp.pad(a, ((0, W_IN_PAD - W_IN_ROWS), (0, 0)))
    return a


def _from_panel(name, panel, shape):
    a = panel[0:W_IN_ROWS] if name == "w_in" else panel
    if name in BIG_TRANSPOSED:
        a = a.T
    return a.reshape(shape)


def _pack_small(arrs):
    flat = jnp.concatenate([jnp.pad(a.reshape(-1), (0, (-a.size) % D)) for a in arrs])
    return jnp.pad(flat, (0, SMALL_ROWS * D - flat.shape[0])).reshape(SMALL_ROWS, D)


def _unpack_small(panel, shapes):
    flat, out, off = panel.reshape(-1), [], 0
    for shp in shapes:
        n = 1
        for s in shp:
            n *= s
        out.append(flat[off:off + n].reshape(shp))
        off += n + (-n) % D
    return out


def _local_step(x, target, w, shards=None):
    dist = shards is not None
    w = dict(w)

    def gather(names):
        return [(shards[n], False) for n in names] if dist else []

    def scatter(*arrs):
        return [(a.reshape(N_DEV, -1, D), True) for a in arrs] if dist else []

    s = x.shape[0]
    n_real = s + N_META
    t = -(-n_real // TM) * TM

    q0, r0, a0, c0 = 0, 2 * DK + DV, 2 * DK + 2 * DV, 2 * DK + 2 * DV + RANK
    wt = w["w_in_t"]
    w_main = jnp.concatenate([wt[q0:r0], wt[c0:N_IN], wt[r0:a0]], axis=0)
    w_a = jnp.pad(wt[a0:c0], ((0, LANES - RANK), (0, 0)))
    wau = jnp.pad(w["w_alpha_up"].astype(BF16), ((0, LANES - RANK), (0, 0)))
    row = lambda name: w[name].reshape(1, -1)
    cw = jnp.broadcast_to(jnp.pad(w["conf_dw_w"], ((0, 32 - CONF_K), (0, 0)))[:, None, :], (32, 8, D))
    fw = jnp.broadcast_to(jnp.concatenate([w["ffn_dw_w"], w["ffn_dw_b"].reshape(1, -1)], axis=0)[:, None, :],
                          (FFN_K + 1, 16, DFF))

    early = ("w_gla_o", "w_conf_o", "w_out")
    h0, u1, proj, alr, *landed = _in_proj(x, w["meta_tokens"], row("norm_mix_g"), w_main, w_a, t, carry=gather(early))
    for n, land in zip(early, landed):
        w[n] = land.reshape(-1, D)
    o, og, sall, *landed = _gla_fwd(proj, alr, wau, row("b_alpha"), row("gla_norm_g"), carry=gather(("w_down",)))
    if dist:
        w["w_down"] = landed[0].reshape(-1, D)
    cc, cs, *landed = _conf_fwd(proj, cw, row("conf_dw_b"), row("conf_ln_g"), row("conf_ln_b"), carry=gather(("w_up",)))
    if dist:
        w["w_up_t"] = landed[0].reshape(-1, D)
    brg, brc, merged, h1 = _mix_fwd(og, cs, proj, h0, w["w_gla_o"], w["w_conf_o"], w["w_out"])
    u2, up = _norm_matmul(h1, row("norm_ffn_g"), w["w_up_t"], 512, "up_proj")
    f, dh2, red = _ffn_out(up, fw, h1, w["w_down"], row("final_norm_g"), target)
    loss = 0.5 / D * jnp.sum(red[0:8])

    g = {"final_norm_g": jnp.sum(red[8:16], axis=0)}
    dup, dfw = _ffn_bwd(dh2, w["w_down"], up, fw)
    g["ffn_dw_w"] = jnp.sum(dfw[0:FFN_K], axis=1)
    g["ffn_dw_b"] = jnp.sum(dfw[3], axis=0)
    g["w_down"] = _wgrad(f, dh2, 1408, "wgrad_down")
    dh1, dg2, *landed = _dgrad_norm(dup, w["w_up_t"], h1, row("norm_ffn_g"), dh2, "up_dgrad", carry=scatter(g["w_down"]))
    if dist:
        g["w_down"] = landed[0]
    g["norm_ffn_g"] = jnp.sum(dg2, axis=0)
    g["w_up_t"] = _wgrad(dup, u2, 1408, "wgrad_up")
    dbrg, dbrc, dog, dcs, dproj = _mix_bwd(dh1, w["w_out"], w["w_gla_o"], w["w_conf_o"], proj, brg, brc)
    g["w_out"] = _wgrad(merged, dh1, 1024, "wgrad_out")
    g["w_gla_o"] = _wgrad(og, dbrg, 1024, "wgrad_gla_o")
    g["w_conf_o"] = _wgrad(cs, dbrc, 1024, "wgrad_conf_o")
    do, dproj, dgn = _glapost_bwd(dog, o, proj, row("gla_norm_g"), dproj)
    g["gla_norm_g"] = jnp.sum(dgn, axis=0)
    dproj, dcw, dst, *landed = _conf_bwd(dcs, cc, proj, cw, row("conf_ln_g"), row("conf_ln_b"), dproj,
                                         carry=scatter(g["w_up_t"]))
    if dist:
        g["w_up_t"] = landed[0]
    g["conf_dw_w"] = jnp.sum(dcw[0:CONF_K], axis=1)
    g["conf_ln_g"], g["conf_ln_b"], g["conf_dw_b"] = jnp.sum(dst[0], axis=0), jnp.sum(dst[1], axis=0), jnp.sum(dst[2], axis=0)
    dproj, dalr, dwau, dba, *landed = _gla_bwd(proj, alr, wau, row("b_alpha"), do, sall, dproj,
                                               carry=scatter(g["w_out"], g["w_gla_o"], g["w_conf_o"]))
    if dist:
        g["w_out"], g["w_gla_o"], g["w_conf_o"] = landed
    g["w_alpha_up"] = dwau[0:RANK]
    g["b_alpha"] = jnp.sum(dba, axis=0)
    dw_main = _wgrad(dproj, u1, 1024, "wgrad_in")
    dw_a = _wgrad(dalr, u1, LANES, "wgrad_alr")
    g["w_in_t"] = jnp.concatenate([dw_main[0:r0], dw_main[NPROJ - DV:NPROJ], dw_a[0:RANK], dw_main[r0:NPROJ - DV]], axis=0)
    w_in_blocks = []
    if dist:
        pad = ((0, 0), (0, W_IN_PAD - W_IN_ROWS), (0, 0))
        w_in_blocks = [(jnp.pad(g["w_in_t"].reshape(N_DEV, W_IN_ROWS, D), pad), True)]
    dh0, dg1, *landed = _in_dgrad(dproj, w_main, h0, row("norm_mix_g"), dh1, dalr, w_a, carry=w_in_blocks)
    if dist:
        g["w_in_t"] = landed[0]
    g["norm_mix_g"] = jnp.sum(dg1, axis=0)
    g["meta_tokens"] = dh0[0:N_META]
    return loss, dh0[N_META:n_real], g


def kernel(x, meta_tokens, norm_mix_g, w_in, w_alpha_up, b_alpha, gla_norm_g, w_gla_o, conf_dw_w, conf_dw_b, conf_ln_g, conf_ln_b, w_conf_o, w_out, norm_ffn_g, w_up, ffn_dw_w, ffn_dw_b, w_down, final_norm_g, loss_target, m_meta_tokens, m_norm_mix_g, m_w_in, m_w_alpha_up, m_b_alpha, m_gla_norm_g, m_w_gla_o, m_conf_dw_w, m_conf_dw_b, m_conf_ln_g, m_conf_ln_b, m_w_conf_o, m_w_out, m_norm_ffn_g, m_w_up, m_ffn_dw_w, m_ffn_dw_b, m_w_down, m_final_norm_g, v_meta_tokens, v_norm_mix_g, v_w_in, v_w_alpha_up, v_b_alpha, v_gla_norm_g, v_w_gla_o, v_conf_dw_w, v_conf_dw_b, v_conf_ln_g, v_conf_ln_b, v_w_conf_o, v_w_out, v_norm_ffn_g, v_w_up, v_ffn_dw_w, v_ffn_dw_b, v_w_down, v_final_norm_g):
    ws = dict(meta_tokens=meta_tokens, norm_mix_g=norm_mix_g, w_in=w_in, w_alpha_up=w_alpha_up, b_alpha=b_alpha,
              gla_norm_g=gla_norm_g, w_gla_o=w_gla_o, conf_dw_w=conf_dw_w, conf_dw_b=conf_dw_b, conf_ln_g=conf_ln_g,
              conf_ln_b=conf_ln_b, w_conf_o=w_conf_o, w_out=w_out, norm_ffn_g=norm_ffn_g, w_up=w_up, ffn_dw_w=ffn_dw_w,
              ffn_dw_b=ffn_dw_b, w_down=w_down, final_norm_g=final_norm_g)
    ms = dict(meta_tokens=m_meta_tokens, norm_mix_g=m_norm_mix_g, w_in=m_w_in, w_alpha_up=m_w_alpha_up, b_alpha=m_b_alpha,
              gla_norm_g=m_gla_norm_g, w_gla_o=m_w_gla_o, conf_dw_w=m_conf_dw_w, conf_dw_b=m_conf_dw_b,
              conf_ln_g=m_conf_ln_g, conf_ln_b=m_conf_ln_b, w_conf_o=m_w_conf_o, w_out=m_w_out, norm_ffn_g=m_norm_ffn_g,
              w_up=m_w_up, ffn_dw_w=m_ffn_dw_w, ffn_dw_b=m_ffn_dw_b, w_down=m_w_down, final_norm_g=m_final_norm_g)
    vs = dict(meta_tokens=v_meta_tokens, norm_mix_g=v_norm_mix_g, w_in=v_w_in, w_alpha_up=v_w_alpha_up, b_alpha=v_b_alpha,
              gla_norm_g=v_gla_norm_g, w_gla_o=v_w_gla_o, conf_dw_w=v_conf_dw_w, conf_dw_b=v_conf_dw_b,
              conf_ln_g=v_conf_ln_g, conf_ln_b=v_conf_ln_b, w_conf_o=v_w_conf_o, w_out=v_w_out, norm_ffn_g=v_norm_ffn_g,
              w_up=v_w_up, ffn_dw_w=v_ffn_dw_w, ffn_dw_b=v_ffn_dw_b, w_down=v_w_down, final_norm_g=v_final_norm_g)
    small = SMALL_SHARDED + REPLICATED
    pack_small = lambda d: _pack_small([d[n] for n in small])

    shards = {n: _to_panel(n, ws[n]).astype(BF16) for n in BIG}
    own = [shards["w_in"], pack_small(ws)]
    my_idx = 4 * lax.axis_index("x") + 2 * lax.axis_index("y") + lax.axis_index("c")
    gathered = [lax.dynamic_update_slice(full_, mine[None], (my_idx, 0, 0))
                for full_, mine in zip(_all_gather(own, "weight_gather"), own)]
    full = {n: ws[n].reshape(-1) for n in REPLICATED}
    full["w_in_t"] = gathered[0][:, 0:W_IN_ROWS].reshape(N_IN, D)
    flat, off = gathered[1].reshape(N_DEV, -1), 0
    for n in SMALL_SHARDED:
        k, c = ws[n].shape[-2], ws[n].shape[-1]
        full[n] = flat[:, off:off + k * c].reshape(N_DEV, k, c).transpose(1, 0, 2).reshape(k, N_DEV * c)
        off += k * c + (-(k * c)) % D

    loss, grad_x, g = _local_step(x[0], loss_target[0], full, shards)

    lands = [g["w_in_t"], g["w_up_t"]] + [g[n] for n in BIG[2:]]
    blocks = []
    for n in SMALL_SHARDED:
        k, c = ws[n].shape[-2], ws[n].shape[-1]
        b = g[n].reshape(k, N_DEV, c).transpose(1, 0, 2).reshape(N_DEV, k * c)
        blocks.append(jnp.pad(b, ((0, 0), (0, (-(k * c)) % D))))
    for n in REPLICATED:
        b = jnp.broadcast_to(g[n].reshape(1, -1), (N_DEV, g[n].size))
        blocks.append(jnp.pad(b, ((0, 0), (0, (-b.shape[1]) % D))))
    gsm = jnp.concatenate(blocks, axis=1)
    lands += _exchange([jnp.pad(gsm, ((0, 0), (0, SMALL_ROWS * D - gsm.shape[1]))).reshape(N_DEV, SMALL_ROWS, D)])

    grad, delta, new_m, new_v = {}, {}, {}, {}
    for i, n in enumerate(BIG):
        outs = _adamw(lands[i], _to_panel(n, ws[n]), _to_panel(n, ms[n]), _to_panel(n, vs[n]), ADAM_BLOCK[n], "adamw_" + n)
        grad[n], delta[n], new_m[n], new_v[n] = [_from_panel(n, p, ws[n].shape) for p in outs]
    outs = _adamw(lands[len(BIG)], pack_small(ws), pack_small(ms), pack_small(vs), SMALL_ROWS, "adamw_small")
    shapes = [ws[n].shape for n in small]
    for d, p in zip((grad, delta, new_m, new_v), outs):
        d.update(zip(small, _unpack_small(p, shapes)))

    order = ("meta_tokens", "norm_mix_g", "w_in", "w_alpha_up", "b_alpha", "gla_norm_g", "w_gla_o", "conf_dw_w", "conf_dw_b",
             "conf_ln_g", "conf_ln_b", "w_conf_o", "w_out", "norm_ffn_g", "w_up", "ffn_dw_w", "ffn_dw_b", "w_down",
             "final_norm_g")
    loss = lax.psum(loss, ("x", "y", "c"))
    return (loss, grad_x[None], *[grad[n] for n in order], *[delta[n] for n in order], *[new_m[n] for n in order],
            *[new_v[n] for n in order])
```

```python
import functools

import jax
import jax.numpy as jnp
from jax import lax
from jax.experimental import pallas as pl
from jax.experimental.pallas import tpu as pltpu

F32, BF16 = jnp.float32, jnp.bfloat16
SDS = jax.ShapeDtypeStruct

D = 1024
N_META = 16
HEADS = 4
DK, DKH, DV, DVH = 512, 128, 1024, 256
RANK = 16
TAU = 16.0
CONF_K = 31
DFF = 2816
FFN_K = 3
IN_WIDTHS = (DK, DK, DV, DV, RANK, 2 * D, D, D)
RMS_EPS, LN_EPS = 1e-6, 1e-5
ADAM_LR, ADAM_B1, ADAM_B2, ADAM_EPS, ADAM_WD, ADAM_STEP = 0.001, 0.9, 0.999, 1e-08, 0.01, 10

NPROJ = 7 * D
LANES = 128
CH = 128
TM = 640
TM_BIG = 1664
TE = 320
HALO = 32
HALO_F = 16
N_DEV = 8
VMEM_LIMIT = 60 * 1024 * 1024
MESH_T = pl.DeviceIdType.MESH

_ARB1 = pltpu.CompilerParams(dimension_semantics=("arbitrary",), vmem_limit_bytes=VMEM_LIMIT)
_ARB2 = pltpu.CompilerParams(dimension_semantics=("arbitrary", "arbitrary"), vmem_limit_bytes=VMEM_LIMIT)


def _dot(a, b):
    return jnp.dot(a, b, preferred_element_type=F32)


def _dot_nt(a, b):
    return lax.dot_general(a, b, (((1,), (1,)), ((), ())), preferred_element_type=F32)


def _dot_tn(a, b):
    return lax.dot_general(a, b, (((0,), (0,)), ((), ())), preferred_element_type=F32)


def _sigmoid(x):
    return 0.5 * jnp.tanh(0.5 * x) + 0.5


def _rows8(x):
    return x.reshape(x.shape[0] // 8, 8, x.shape[1]).sum(axis=0)


def _row_tile(t, preferred):
    return preferred if t % preferred == 0 else TM


def _row_loop(n_rows, rb, fn, unroll=1):
    def step(i, carry):
        fn(pl.multiple_of(i * rb, rb))
        return carry
    lax.fori_loop(0, n_rows // rb, step, 0, unroll=unroll)


def _resident(shape):
    return pl.BlockSpec(shape, lambda *_: (0,) * len(shape))


def _once(shape):
    return pl.BlockSpec(shape, lambda *_: (0,) * len(shape), pipeline_mode=pl.Buffered(1))


CONV_ROWS, CONV_LANES = 64, 256


def _shift_classes(offset_taps):
    return [(s, [(o - s, j) for o, j in offset_taps if o % 8 == s]) for s in range(8)]


def _shifted(win, s):
    return win if s == 0 else win[s:s + CONV_ROWS + HALO - 8, :]


def _my_place():
    return lax.axis_index("x"), lax.axis_index("y"), lax.axis_index("c")


def _flip(v, bit):
    return 1 - v if bit else v


def _exchange_copies(src_refs, land_refs, scatter, send_sems, recv_sems, local_sems, arrivals):
    x, y, c = _my_place()
    my_idx = 4 * x + 2 * y + c
    local, remote = [], []
    for p, (src, land) in enumerate(zip(src_refs, land_refs)):
        local.append(pltpu.make_async_copy(src.at[my_idx] if scatter[p] else src, land.at[my_idx], local_sems.at[p]))
    for k in range(1, N_DEV):
        px, py, pc = _flip(x, k & 4), _flip(y, k & 2), _flip(c, k & 1)
        p_idx = 4 * px + 2 * py + pc
        for p, (src, land) in enumerate(zip(src_refs, land_refs)):
            s = 7 * p + k - 1
            out = src.at[p_idx] if scatter[p] else src

            def copy(dst):
                return pltpu.make_async_remote_copy(src_ref=out, dst_ref=dst, send_sem=send_sems.at[s],
                                                    recv_sem=recv_sems.at[s], device_id=(px, py, pc), device_id_type=MESH_T)
            remote.append((copy(land.at[my_idx]), copy(land.at[p_idx]) if arrivals else None))
    return local, remote


def _carried_call(core, carry, *, grid, in_specs, out_specs, out_shape, scratch_shapes=(), **kw):
    n_in, n_out, nc, n_scr = len(in_specs), len(out_specs), len(carry), len(scratch_shapes)
    if nc == 0:
        return pl.pallas_call(core, grid=grid, in_specs=in_specs, out_specs=out_specs, out_shape=out_shape,
                              scratch_shapes=list(scratch_shapes), **kw)
    scatter = [sc for _, sc in carry]

    def body(*refs):
        ins, cin = refs[:n_in], refs[n_in:n_in + nc]
        outs, cout = refs[n_in + nc:n_in + nc + n_out], refs[n_in + nc + n_out:n_in + 2 * nc + n_out]
        scr, sems = refs[n_in + 2 * nc + n_out:n_in + 2 * nc + n_out + n_scr], refs[-3:]
        first = functools.reduce(jnp.logical_and, [pl.program_id(a) == 0 for a in range(len(grid))])
        last = functools.reduce(jnp.logical_and, [pl.program_id(a) == grid[a] - 1 for a in range(len(grid))])

        @pl.when(first)
        def _():
            local, remote = _exchange_copies(cin, cout, scatter, *sems, arrivals=False)
            for cp in local:
                cp.start()
            for send, _ in remote:
                send.start()

        core(*ins, *outs, *scr)

        @pl.when(last)
        def _():
            local, remote = _exchange_copies(cin, cout, scatter, *sems, arrivals=True)
            for _, arrival in remote:
                arrival.wait_recv()
            for send, _ in remote:
                send.wait_send()
            for cp in local:
                cp.wait()

    hbm = pl.BlockSpec(memory_space=pl.ANY)
    land_shape = [SDS((N_DEV, *(a.shape[1:] if sc else a.shape)), a.dtype) for a, sc in carry]
    sems = [pltpu.SemaphoreType.DMA((7 * nc,)), pltpu.SemaphoreType.DMA((7 * nc,)), pltpu.SemaphoreType.DMA((nc,))]
    call = pl.pallas_call(body, grid=grid, in_specs=list(in_specs) + [hbm] * nc, out_specs=list(out_specs) + [hbm] * nc,
                          out_shape=list(out_shape) + land_shape, scratch_shapes=list(scratch_shapes) + sems, **kw)
    return lambda *args: call(*args, *[a for a, _ in carry])


def _norm_matmul(h, g, w_t, tn, name, w_extra_t=None, carry=()):
    t, n = h.shape[0], w_t.shape[0]
    tm = _row_tile(t, TM_BIG)
    nt, nb = t // tm, n // tn

    def body(*refs):
        if w_extra_t is None:
            h_ref, g_ref, w_ref, u_ref, p_ref = refs
        else:
            h_ref, g_ref, w_ref, we_ref, u_ref, p_ref, e_ref = refs

        @pl.when(pl.program_id(1) == 0)
        def _():
            def blk(r0):
                x = h_ref[pl.ds(r0, 32), :]
                rinv = lax.rsqrt(jnp.mean(x * x, axis=-1, keepdims=True) + RMS_EPS)
                u_ref[pl.ds(r0, 32), :] = (x * rinv * g_ref[...]).astype(BF16)
            _row_loop(tm, 32, blk, unroll=13)
            if w_extra_t is not None:
                e_ref[...] = _dot_nt(u_ref[...], we_ref[...]).astype(BF16)

        p_ref[...] = _dot_nt(u_ref[...], w_ref[...]).astype(BF16)

    in_specs = [pl.BlockSpec((tm, D), lambda i, j: (i, 0)), _resident((1, D)), pl.BlockSpec((tn, D), lambda i, j: (j, 0))]
    out_specs = [pl.BlockSpec((tm, D), lambda i, j: (i, 0)), pl.BlockSpec((tm, tn), lambda i, j: (i, j))]
    out_shape = [SDS((t, D), BF16), SDS((t, n), BF16)]
    args = [h, g, w_t]
    if w_extra_t is not None:
        in_specs.append(_resident(w_extra_t.shape))
        out_specs.append(pl.BlockSpec((tm, w_extra_t.shape[0]), lambda i, j: (i, 0)))
        out_shape.append(SDS((t, w_extra_t.shape[0]), BF16))
        args.append(w_extra_t)
    return _carried_call(body, carry, name=name, grid=(nt, nb), in_specs=in_specs, out_specs=out_specs,
                         out_shape=out_shape, compiler_params=_ARB2)(*args)


def _in_proj(x, meta, g, w_t, w_a_t, t, carry=()):
    n_real = x.shape[0] + N_META
    n, tn = w_t.shape[0], 1024
    tm = _row_tile(t, TM_BIG)
    nt, nb = t // tm, n // tn

    def body(x_ref, xh_ref, m_ref, g_ref, w_ref, wa_ref, h_ref, u_ref, p_ref, e_ref):
        i = pl.program_id(0)

        @pl.when(pl.program_id(1) == 0)
        def _():
            def rows_of(r0, val):
                gid = i * tm + r0 + lax.broadcasted_iota(jnp.int32, (32, 1), 0)
                val = jnp.where(gid < n_real, val, 0.0)
                h_ref[pl.ds(r0, 32), :] = val
                rinv = lax.rsqrt(jnp.mean(val * val, axis=-1, keepdims=True) + RMS_EPS)
                u_ref[pl.ds(r0, 32), :] = (val * rinv * g_ref[...]).astype(BF16)

            before = jnp.where(i == 0, m_ref[...], xh_ref[...])
            rows_of(0, jnp.concatenate([before, x_ref[0:N_META, :]], axis=0))

            def blk(k, c):
                r0 = pl.multiple_of(k * 32, 32)
                rows_of(r0, x_ref[pl.ds(pl.multiple_of(r0 - N_META, N_META), 32), :])
                return c
            lax.fori_loop(1, tm // 32, blk, 0, unroll=17)
            e_ref[...] = _dot_nt(u_ref[...], wa_ref[...]).astype(BF16)

        p_ref[...] = _dot_nt(u_ref[...], w_ref[...]).astype(BF16)

    row = lambda w: pl.BlockSpec((tm, w), lambda i, j: (i, 0))
    return _carried_call(
        body, carry, name="in_proj", grid=(nt, nb),
        in_specs=[row(D), pl.BlockSpec((N_META, D), lambda i, j: (jnp.maximum(i * (tm // N_META) - 1, 0), 0)),
                  _resident((N_META, D)), _resident((1, D)), pl.BlockSpec((tn, D), lambda i, j: (j, 0)),
                  _resident(w_a_t.shape)],
        out_specs=[row(D), row(D), pl.BlockSpec((tm, tn), lambda i, j: (i, j)), row(w_a_t.shape[0])],
        out_shape=[SDS((t, D), F32), SDS((t, D), BF16), SDS((t, n), BF16), SDS((t, w_a_t.shape[0]), BF16)],
        compiler_params=_ARB2)(x, x, meta, g, w_t, w_a_t)


def _gla_decay(a_ref, wau_ref, ba_ref):
    z = _dot(a_ref[...], wau_ref[...]) + ba_ref[...]
    la = (jnp.minimum(z, 0.0) - jnp.log(1.0 + jnp.exp(-jnp.abs(z)))) * (1.0 / TAU)
    r = lax.broadcasted_iota(jnp.int32, (CH, CH), 0)
    c = lax.broadcasted_iota(jnp.int32, (CH, CH), 1)
    b = _tri_matmul((r >= c).astype(BF16), la)
    mid = jnp.broadcast_to(b[CH // 2:CH // 2 + 1, :], b.shape)
    last = jnp.broadcast_to(b[CH - 1:CH, :], b.shape)
    return z, b, mid, last, r >= c


def _tri_matmul(tri, x):
    n = x.shape[1]
    x1 = x.astype(BF16)
    x2 = (x - x1.astype(F32)).astype(BF16)
    y = _dot(tri, jnp.concatenate([x1, x2], axis=1))
    return y[:, 0:n] + y[:, n:2 * n]


def _gla_fwd(proj, alr, wau, balpha, gn, carry=()):
    t = proj.shape[0]
    nc = t // CH

    def body(qk_ref, v_ref, r_ref, a_ref, wau_ref, ba_ref, gn_ref, o_ref, og_ref, sall_ref, s_scr):
        @pl.when(pl.program_id(0) == 0)
        def _():
            s_scr[...] = jnp.zeros_like(s_scr)

        sall_ref[0] = s_scr[...]
        _, b, bmid, blast, causal = _gla_decay(a_ref, wau_ref, ba_ref)
        for h in range(HEADS):
            ks = slice(h * DKH, (h + 1) * DKH)
            vs = slice(h * DVH, (h + 1) * DVH)
            bh, mh, lh = b[:, ks], bmid[:, ks], blast[:, ks]
            q = qk_ref[:, ks].astype(F32) * (DKH ** -0.5)
            k = qk_ref[:, DK + h * DKH:DK + (h + 1) * DKH].astype(F32)
            v = v_ref[:, vs]
            qt = (q * jnp.exp(bh - mh)).astype(BF16)
            kt = (k * jnp.exp(mh - bh)).astype(BF16)
            qg = (q * jnp.exp(bh)).astype(BF16)
            kg = (k * jnp.exp(lh - bh)).astype(BF16)
            a = jnp.where(causal, _dot_nt(qt, kt), 0.0)
            st = s_scr[vs, :]
            o = _dot(a.astype(BF16), v) + _dot_nt(qg, st.astype(BF16))
            el = jnp.exp(lh)
            s_scr[vs, :] = st * jnp.concatenate([el, el], axis=0) + _dot_tn(v, kg)
            o_ref[:, vs] = o
            on = o * lax.rsqrt(jnp.mean(o * o, axis=-1, keepdims=True) + RMS_EPS) * gn_ref[:, vs]
            rr = r_ref[:, vs].astype(F32)
            og_ref[:, vs] = (on * (rr * _sigmoid(rr))).astype(BF16)

    return _carried_call(
        body, carry, name="gla_fwd", grid=(nc,),
        in_specs=[pl.BlockSpec((CH, D), lambda c: (c, 0)), pl.BlockSpec((CH, D), lambda c: (c, 1)),
                  pl.BlockSpec((CH, D), lambda c: (c, 6)), pl.BlockSpec((CH, LANES), lambda c: (c, 0)),
                  _resident((LANES, DK)), _resident((1, DK)), _resident((1, DV))],
        out_specs=[pl.BlockSpec((CH, DV), lambda c: (c, 0)), pl.BlockSpec((CH, DV), lambda c: (c, 0)),
                   pl.BlockSpec((1, DV, DKH), lambda c: (c, 0, 0))],
        out_shape=[SDS((t, DV), F32), SDS((t, DV), BF16), SDS((nc, DV, DKH), F32)],
        scratch_shapes=[pltpu.VMEM((DV, DKH), F32)], compiler_params=_ARB1)(proj, proj, proj, alr, wau, balpha, gn)


def _conf_fwd(proj, cw, cb, lg, lb, carry=()):
    t = proj.shape[0]
    nt = t // TE

    def body(c1_ref, c2_ref, cw_ref, cb_ref, lg_ref, lb_ref, cc_ref, cs_ref, cext):
        i = pl.program_id(0)

        @pl.when(i == 0)
        def _():
            cext[0:HALO, :] = jnp.zeros((HALO, D), F32)

        @pl.when(i > 0)
        def _():
            cext[0:HALO, :] = cext[TE:TE + HALO, :]

        def glu(r0):
            c2 = c2_ref[pl.ds(r0, 32), :].astype(F32)
            cext[pl.ds(HALO + r0, 32), :] = c1_ref[pl.ds(r0, 32), :].astype(F32) * _sigmoid(c2)
        _row_loop(TE, 32, glu)

        def conv(r0):
            for part in range(D // CONV_LANES):
                cols = slice(part * CONV_LANES, (part + 1) * CONV_LANES)
                win = cext[pl.ds(r0, CONV_ROWS + HALO), cols]
                acc = jnp.zeros((CONV_ROWS, CONV_LANES), F32) + cb_ref[:, cols]
                for s, taps in _shift_classes([(2 + j, j) for j in range(CONF_K)]):
                    ws = _shifted(win, s)
                    for a8, j in taps:
                        acc = acc + jnp.tile(cw_ref[j, :, cols], (CONV_ROWS // 8, 1)) * ws[a8:a8 + CONV_ROWS, :]
                cc_ref[pl.ds(r0, CONV_ROWS), cols] = acc

        def norm(r0):
            for sub in range(CONV_ROWS // 32):
                rows = pl.ds(r0 + 32 * sub, 32)
                x = cc_ref[rows, :]
                xc = x - jnp.mean(x, axis=-1, keepdims=True)
                var = jnp.mean(xc * xc, axis=-1, keepdims=True)
                ln = xc * lax.rsqrt(var + LN_EPS) * lg_ref[...] + lb_ref[...]
                cs_ref[rows, :] = (ln * _sigmoid(ln)).astype(BF16)

        def step(r0):
            conv(r0)
            norm(r0)
        _row_loop(TE, CONV_ROWS, step)

    return _carried_call(
        body, carry, name="conf_fwd", grid=(nt,),
        in_specs=[pl.BlockSpec((TE, D), lambda i: (i, 2)), pl.BlockSpec((TE, D), lambda i: (i, 3)),
                  _resident((32, 8, D)), _resident((1, D)), _resident((1, D)), _resident((1, D))],
        out_specs=[pl.BlockSpec((TE, D), lambda i: (i, 0)), pl.BlockSpec((TE, D), lambda i: (i, 0))],
        out_shape=[SDS((t, D), F32), SDS((t, D), BF16)],
        scratch_shapes=[pltpu.VMEM((TE + HALO, D), F32)], compiler_params=_ARB1)(proj, proj, cw, cb, lg, lb)


def _mix_fwd(og, cs, proj, h0, wg, wc, wo):
    t = h0.shape[0]
    nt = t // TM

    def body(og_ref, cs_ref, g_ref, h0_ref, wg_ref, wc_ref, wo_ref, brg_ref, brc_ref, mg_ref, h1_ref):
        brg_ref[...] = _dot(og_ref[...], wg_ref[...]).astype(BF16)
        brc_ref[...] = _dot(cs_ref[...], wc_ref[...]).astype(BF16)

        def blk(r0):
            rows = pl.ds(r0, 32)
            gg = g_ref[rows, 0:D].astype(F32)
            gc = g_ref[rows, D:2 * D].astype(F32)
            m = _sigmoid(gg) * brg_ref[rows, :].astype(F32) + _sigmoid(gc) * brc_ref[rows, :].astype(F32)
            mg_ref[rows, :] = m.astype(BF16)
        _row_loop(TM, 32, blk)
        h1_ref[...] = h0_ref[...] + _dot(mg_ref[...], wo_ref[...])

    row = lambda w: pl.BlockSpec((TM, w), lambda i: (i, 0))
    return pl.pallas_call(
        body, name="mix_fwd", grid=(nt,),
        in_specs=[row(D), row(D), pl.BlockSpec((TM, 2 * D), lambda i: (i, 2)), row(D),
                  _once((D, D)), _once((D, D)), _once((D, D))],
        out_specs=[row(D), row(D), row(D), row(D)],
        out_shape=[SDS((t, D), BF16), SDS((t, D), BF16), SDS((t, D), BF16), SDS((t, D), F32)],
        compiler_params=_ARB1)(og, cs, proj, h0, wg, wc, wo)


def _ffn_out(up, fw, h1, wd, gf, target):
    t = h1.shape[0]
    nt = t // TE
    n_real = target.shape[0] + N_META

    def body(a_ref, bv_ref, fw_ref, h1_ref, wd_ref, gf_ref, tg_ref, tb_ref, f_ref, dh2_ref, red_ref, before, hs):
        i = pl.program_id(0)

        @pl.when(i == 0)
        def _():
            before[...] = jnp.zeros_like(before)
            red_ref[...] = jnp.zeros_like(red_ref)

        def conv(r0, win):
            ac = fw_ref[3] + fw_ref[0] * win[14:30, :] + fw_ref[1] * win[15:31, :] + fw_ref[2] * win[16:32, :]
            f_ref[pl.ds(r0, 16), :] = (ac * _sigmoid(ac) * bv_ref[pl.ds(r0, 16), :].astype(F32)).astype(BF16)

        conv(0, jnp.concatenate([before[...], a_ref[0:HALO_F, :].astype(F32)], axis=0))

        def conv_blk(k, c):
            r0 = pl.multiple_of(k * 16, 16)
            conv(r0, a_ref[pl.ds(r0 - HALO_F, 32), :].astype(F32))
            return c
        lax.fori_loop(1, TE // 16, conv_blk, 0)
        before[...] = a_ref[TE - HALO_F:TE, :].astype(F32)
        hs[...] = h1_ref[...] + _dot(f_ref[...], wd_ref[...])

        def head(r0, tg):
            rows = pl.ds(r0, 32)
            h2 = hs[rows, :]
            rinv = lax.rsqrt(jnp.mean(h2 * h2, axis=-1, keepdims=True) + RMS_EPS)
            hh = h2 * rinv
            gid = i * TE + r0 + lax.broadcasted_iota(jnp.int32, (32, 1), 0)
            live = jnp.logical_and(gid >= N_META, gid < n_real)
            err = jnp.where(live, hh * gf_ref[...] - tg, 0.0)
            dy = err * (1.0 / D)
            red_ref[0:8, :] += _rows8(err * err)
            red_ref[8:16, :] += _rows8(dy * hh)
            dhh = dy * gf_ref[...]
            dh2_ref[rows, :] = rinv * (dhh - hh * jnp.mean(dhh * hh, axis=-1, keepdims=True))

        head(0, jnp.concatenate([tb_ref[...], tg_ref[0:N_META, :]], axis=0))

        def blk(k, c):
            r0 = pl.multiple_of(k * 32, 32)
            head(r0, tg_ref[pl.ds(pl.multiple_of(r0 - N_META, N_META), 32), :])
            return c
        lax.fori_loop(1, TE // 32, blk, 0, unroll=9)

    row = lambda w: pl.BlockSpec((TE, w), lambda i: (i, 0))
    return pl.pallas_call(
        body, name="ffn_out", grid=(nt,),
        in_specs=[pl.BlockSpec((TE, DFF), lambda i: (i, 0)), pl.BlockSpec((TE, DFF), lambda i: (i, 1)),
                  _resident((4, 16, DFF)), row(D), _resident((DFF, D)), _resident((1, D)), row(D),
                  pl.BlockSpec((N_META, D), lambda i: (jnp.maximum(i * (TE // N_META) - 1, 0), 0))],
        out_specs=[row(DFF), row(D), _resident((16, D))],
        out_shape=[SDS((t, DFF), BF16), SDS((t, D), F32), SDS((16, D), F32)],
        scratch_shapes=[pltpu.VMEM((HALO_F, DFF), F32), pltpu.VMEM((TE, D), F32)],
        compiler_params=_ARB1)(up, up, fw, h1, wd, gf, target, target)


def _ffn_bwd(dh2, wd, up, fw):
    t = dh2.shape[0]
    nt = t // TE
    hb = TE // HALO_F

    def body(dh_ref, wd_ref, a_ref, ah_ref, bv_ref, fw_ref, dup_ref, dw_ref, dax, dfs):
        i = pl.program_id(0)
        ti = nt - 1 - i

        @pl.when(i == 0)
        def _():
            dax[TE:TE + HALO_F, :] = jnp.zeros((HALO_F, DFF), F32)
            dw_ref[...] = jnp.zeros_like(dw_ref)

        @pl.when(i > 0)
        def _():
            dax[TE:TE + HALO_F, :] = dax[0:HALO_F, :]

        dfs[...] = _dot_nt(dh_ref[...].astype(BF16), wd_ref[...])

        def act(r0, win):
            rows = pl.ds(r0, 16)
            ac = fw_ref[3] + fw_ref[0] * win[14:30, :] + fw_ref[1] * win[15:31, :] + fw_ref[2] * win[16:32, :]
            sg = _sigmoid(ac)
            df = dfs[rows, :]
            dsg = df * sg
            dup_ref[rows, DFF:2 * DFF] = (dsg * ac).astype(BF16)
            dac = dsg * bv_ref[rows, :].astype(F32) * (1.0 + ac * (1.0 - sg))
            dax[rows, :] = dac
            dw_ref[3] += _rows8(dac)
            for j in range(FFN_K):
                dw_ref[j] += _rows8(dac * win[14 + j:30 + j, :])

        before = jnp.where(ti > 0, ah_ref[...].astype(F32), 0.0)
        act(0, jnp.concatenate([before, a_ref[0:HALO_F, :].astype(F32)], axis=0))

        def act_blk(k, c):
            r0 = pl.multiple_of(k * 16, 16)
            act(r0, a_ref[pl.ds(r0 - HALO_F, 32), :].astype(F32))
            return c
        lax.fori_loop(1, TE // 16, act_blk, 0)

        def convt(r0):
            win = dax[pl.ds(r0, 32), :]
            da = fw_ref[2] * win[0:16, :] + fw_ref[1] * win[1:17, :] + fw_ref[0] * win[2:18, :]
            dup_ref[pl.ds(r0, 16), 0:DFF] = da.astype(BF16)
        _row_loop(TE, 16, convt)

    rev = lambda w: pl.BlockSpec((TE, w), lambda i: (nt - 1 - i, 0))
    return pl.pallas_call(
        body, name="ffn_bwd", grid=(nt,),
        in_specs=[rev(D), _resident((DFF, D)), rev(DFF),
                  pl.BlockSpec((HALO_F, DFF), lambda i: (jnp.maximum((nt - 1 - i) * hb - 1, 0), 0)),
                  pl.BlockSpec((TE, DFF), lambda i: (nt - 1 - i, 1)), _resident((4, 16, DFF))],
        out_specs=[rev(2 * DFF), _resident((4, 8, DFF))],
        out_shape=[SDS((t, 2 * DFF), BF16), SDS((4, 8, DFF), F32)],
        scratch_shapes=[pltpu.VMEM((TE + HALO_F, DFF), F32), pltpu.VMEM((TE, DFF), F32)],
        compiler_params=_ARB1)(dh2, wd, up, up, up, fw)


def _dgrad_norm(dy, w_t, h, g, dres, name, carry=()):
    t, k = dy.shape
    nt = t // TM

    def body(dy_ref, w_ref, h_ref, g_ref, dr_ref, dh_ref, dg_ref, acc):
        @pl.when(pl.program_id(0) == 0)
        def _():
            dg_ref[...] = jnp.zeros_like(dg_ref)

        acc[...] = _dot(dy_ref[...], w_ref[...])

        def blk(r0):
            rows = pl.ds(r0, 32)
            x = h_ref[rows, :]
            rinv = lax.rsqrt(jnp.mean(x * x, axis=-1, keepdims=True) + RMS_EPS)
            hh = x * rinv
            du = acc[rows, :]
            dg_ref[...] += _rows8(du * hh)
            dhh = du * g_ref[...]
            dh_ref[rows, :] = dr_ref[rows, :] + rinv * (dhh - hh * jnp.mean(dhh * hh, axis=-1, keepdims=True))
        _row_loop(TM, 32, blk, unroll=10)

    row = pl.BlockSpec((TM, D), lambda i: (i, 0))
    in_specs = [pl.BlockSpec((TM, k), lambda i: (i, 0)), _once((k, D)), row, _resident((1, D)), row]
    return _carried_call(
        body, carry, name=name, grid=(nt,), in_specs=in_specs, out_specs=[row, _resident((8, D))],
        out_shape=[SDS((t, D), F32), SDS((8, D), F32)],
        scratch_shapes=[pltpu.VMEM((TM, D), F32)], compiler_params=_ARB1)(dy, w_t, h, g, dres)


def _in_dgrad(dy, w_t, h, g, dres, dy_extra, w_extra_t, carry=()):
    t, k = dy.shape
    nt = t // TM

    def body(dy_ref, w_ref, h_ref, g_ref, dr_ref, de_ref, we_ref, dh_ref, dg_ref, acc):
        @pl.when(pl.program_id(0) == 0)
        def _():
            dg_ref[...] = jnp.zeros_like(dg_ref)

        acc[...] = _dot(dy_ref[...], w_ref[...])
        acc[...] += _dot(de_ref[...], we_ref[...])

        def blk(r0):
            rows = pl.ds(r0, 32)
            x = h_ref[rows, :]
            rinv = lax.rsqrt(jnp.mean(x * x, axis=-1, keepdims=True) + RMS_EPS)
            hh = x * rinv
            du = acc[rows, :]
            dg_ref[...] += _rows8(du * hh)
            dhh = du * g_ref[...]
            dh_ref[rows, :] = dr_ref[rows, :] + rinv * (dhh - hh * jnp.mean(dhh * hh, axis=-1, keepdims=True))
        _row_loop(TM, 32, blk, unroll=4)

    tile = lambda w: pl.BlockSpec((TM, w), lambda i: (i, 0))
    return _carried_call(
        body, carry, name="in_dgrad", grid=(nt,),
        in_specs=[tile(k), _once((k, D)), tile(D), _resident((1, D)), tile(D), tile(dy_extra.shape[1]),
                  _once(w_extra_t.shape)],
        out_specs=[tile(D), _resident((8, D))], out_shape=[SDS((t, D), F32), SDS((8, D), F32)],
        scratch_shapes=[pltpu.VMEM((TM, D), F32)], compiler_params=_ARB1)(dy, w_t, h, g, dres, dy_extra, w_extra_t)


def _wgrad(x, dy, tk, name):
    t, k = x.shape
    n = dy.shape[1]
    tm = _row_tile(t, TM_BIG)
    nk, nt = k // tk, t // tm

    def body(x_ref, dy_ref, o_ref, acc):
        @pl.when(pl.program_id(1) == 0)
        def _():
            acc[...] = jnp.zeros_like(acc)
        acc[...] += _dot_tn(x_ref[...], dy_ref[...].astype(BF16))

        @pl.when(pl.program_id(1) == nt - 1)
        def _():
            o_ref[...] = acc[...].astype(BF16)

    return pl.pallas_call(
        body, name=name, grid=(nk, nt),
        in_specs=[pl.BlockSpec((tm, tk), lambda j, i: (i, j)), pl.BlockSpec((tm, n), lambda j, i: (i, 0))],
        out_specs=pl.BlockSpec((tk, n), lambda j, i: (j, 0)), out_shape=SDS((k, n), BF16),
        scratch_shapes=[pltpu.VMEM((tk, n), F32)], compiler_params=_ARB2)(x, dy)


def _mix_bwd(dh1, wo, wg, wc, proj, brg, brc):
    t = dh1.shape[0]
    nt = t // TM

    def body(dh_ref, wo_ref, wg_ref, wc_ref, g_ref, brg_ref, brc_ref, dbg_ref, dbc_ref, dog_ref, dcs_ref, dp_ref, dm):
        dm[...] = _dot_nt(dh_ref[...].astype(BF16), wo_ref[...])

        def blk(r0):
            rows = pl.ds(r0, 32)
            d = dm[rows, :]
            sg = _sigmoid(g_ref[rows, 0:D].astype(F32))
            sc = _sigmoid(g_ref[rows, D:2 * D].astype(F32))
            dbg_ref[rows, :] = (d * sg).astype(BF16)
            dbc_ref[rows, :] = (d * sc).astype(BF16)
            dp_ref[rows, 0:D] = (d * brg_ref[rows, :].astype(F32) * sg * (1.0 - sg)).astype(BF16)
            dp_ref[rows, D:2 * D] = (d * brc_ref[rows, :].astype(F32) * sc * (1.0 - sc)).astype(BF16)
        _row_loop(TM, 32, blk)
        dog_ref[...] = _dot_nt(dbg_ref[...], wg_ref[...]).astype(BF16)
        dcs_ref[...] = _dot_nt(dbc_ref[...], wc_ref[...]).astype(BF16)

    row = pl.BlockSpec((TM, D), lambda i: (i, 0))
    wide = pl.BlockSpec((TM, 2 * D), lambda i: (i, 2))
    return pl.pallas_call(
        body, name="mix_bwd", grid=(nt,),
        in_specs=[row, _once((D, D)), _once((D, D)), _once((D, D)), wide, row, row],
        out_specs=[row, row, row, row, wide],
        out_shape=[SDS((t, D), BF16)] * 4 + [SDS((t, NPROJ), BF16)],
        scratch_shapes=[pltpu.VMEM((TM, D), F32)], compiler_params=_ARB1)(dh1, wo, wg, wc, proj, brg, brc)


def _glapost_bwd(dog, o, proj, gn, dproj):
    t = o.shape[0]
    nt = t // TE

    def body(dog_ref, o_ref, r_ref, gn_ref, dp_in, do_ref, dp_ref, dgn_ref):
        del dp_in

        @pl.when(pl.program_id(0) == 0)
        def _():
            dgn_ref[...] = jnp.zeros_like(dgn_ref)

        def blk(r0):
            rows = pl.ds(r0, 32)
            for h in range(HEADS):
                vs = slice(h * DVH, (h + 1) * DVH)
                x = o_ref[rows, vs]
                rinv = lax.rsqrt(jnp.mean(x * x, axis=-1, keepdims=True) + RMS_EPS)
                oh = x * rinv
                g = gn_ref[:, vs]
                rr = r_ref[rows, vs].astype(F32)
                sr = _sigmoid(rr)
                d = dog_ref[rows, vs].astype(F32)
                dp_ref[rows, vs] = (d * oh * g * sr * (1.0 + rr * (1.0 - sr))).astype(BF16)
                don = d * rr * sr
                dgn_ref[:, vs] += _rows8(don * oh)
                doh = don * g
                do_ref[rows, vs] = (rinv * (doh - oh * jnp.mean(doh * oh, axis=-1, keepdims=True))).astype(BF16)
        _row_loop(TE, 32, blk, unroll=5)

    row = pl.BlockSpec((TE, D), lambda i: (i, 0))
    rcol = pl.BlockSpec((TE, D), lambda i: (i, 6))
    return pl.pallas_call(
        body, name="glapost_bwd", grid=(nt,),
        in_specs=[row, row, rcol, _resident((1, D)), pl.BlockSpec(memory_space=pl.ANY)],
        out_specs=[row, rcol, _resident((8, D))],
        out_shape=[SDS((t, D), BF16), SDS((t, NPROJ), BF16), SDS((8, D), F32)],
        input_output_aliases={4: 1}, compiler_params=_ARB1)(dog, o, proj, gn, dproj)


def _conf_bwd(dcs, cc, proj, cw, lg, lb, dproj, carry=()):
    t = cc.shape[0]
    nt = t // TE

    def body(dcs_ref, cc_ref, c1_ref, c2_ref, cw_ref, lg_ref, lb_ref, dp_in, dp_ref, dw_ref, ds_ref, cval, gate, dext):
        del dp_in
        i = pl.program_id(0)

        @pl.when(i == 0)
        def _():
            dext[TE:TE + HALO, :] = jnp.zeros((HALO, D), F32)
            dw_ref[...] = jnp.zeros_like(dw_ref)
            ds_ref[...] = jnp.zeros_like(ds_ref)

        @pl.when(i > 0)
        def _():
            dext[TE:TE + HALO, :] = dext[0:HALO, :]

        def pre(r0):
            rows = pl.ds(r0, 32)
            s2 = _sigmoid(c2_ref[rows, :].astype(F32))
            gate[rows, :] = s2
            cval[rows, :] = c1_ref[rows, :].astype(F32) * s2
            x = cc_ref[rows, :]
            mu = jnp.mean(x, axis=-1, keepdims=True)
            xc = x - mu
            rstd = lax.rsqrt(jnp.mean(xc * xc, axis=-1, keepdims=True) + LN_EPS)
            xh = xc * rstd
            ln = xh * lg_ref[...] + lb_ref[...]
            sg = _sigmoid(ln)
            dln = dcs_ref[rows, :].astype(F32) * sg * (1.0 + ln * (1.0 - sg))
            ds_ref[0] += _rows8(dln * xh)
            ds_ref[1] += _rows8(dln)
            dxh = dln * lg_ref[...]
            dcc = rstd * (dxh - jnp.mean(dxh, axis=-1, keepdims=True) - xh * jnp.mean(dxh * xh, axis=-1, keepdims=True))
            dext[rows, :] = dcc
            ds_ref[2] += _rows8(dcc)
        _row_loop(TE, 32, pre, unroll=10)

        def convt(r0):
            rows = pl.ds(r0, CONV_ROWS)
            for part in range(D // CONV_LANES):
                cols = slice(part * CONV_LANES, (part + 1) * CONV_LANES)
                wd = dext[pl.ds(r0, CONV_ROWS + HALO), cols]
                cv = cval[rows, cols]
                dc = jnp.zeros((CONV_ROWS, CONV_LANES), F32)
                for s, taps in _shift_classes([(CONF_K - 1 - j, j) for j in range(CONF_K)]):
                    ws = _shifted(wd, s)
                    for a8, j in taps:
                        ahead = ws[a8:a8 + CONV_ROWS, :]
                        dc = dc + jnp.tile(cw_ref[j, :, cols], (CONV_ROWS // 8, 1)) * ahead
                        dw_ref[j, :, cols] += _rows8(cv * ahead)
                c1 = c1_ref[rows, cols].astype(F32)
                s2 = gate[rows, cols]
                dp_ref[rows, cols] = (dc * s2).astype(BF16)
                dp_ref[rows, D + part * CONV_LANES:D + (part + 1) * CONV_LANES] = (dc * c1 * s2 * (1.0 - s2)).astype(BF16)
        _row_loop(TE, CONV_ROWS, convt)

    rev = lambda col: pl.BlockSpec((TE, D), lambda i: (nt - 1 - i, col))
    return _carried_call(
        body, carry, name="conf_bwd", grid=(nt,),
        in_specs=[rev(0), rev(0), rev(2), rev(3), _resident((32, 8, D)), _resident((1, D)), _resident((1, D)),
                  pl.BlockSpec(memory_space=pl.ANY)],
        out_specs=[pl.BlockSpec((TE, 2 * D), lambda i: (nt - 1 - i, 1)), _resident((32, 8, D)), _resident((3, 8, D))],
        out_shape=[SDS((t, NPROJ), BF16), SDS((32, 8, D), F32), SDS((3, 8, D), F32)],
        scratch_shapes=[pltpu.VMEM((TE, D), F32), pltpu.VMEM((TE, D), F32), pltpu.VMEM((TE + HALO, D), F32)],
        input_output_aliases={7: 0}, compiler_params=_ARB1)(dcs, cc, proj, proj, cw, lg, lb, dproj)


def _gla_bwd(proj, alr, wau, balpha, do, sall, dproj, carry=()):
    t = proj.shape[0]
    nc = t // CH

    def body(qk_ref, v_ref, a_ref, wau_ref, ba_ref, do_ref, s_ref, dp_in, dp_ref, da_ref, dwau_ref, dba_ref, ds_scr, dla_scr):
        del dp_in

        @pl.when(pl.program_id(0) == 0)
        def _():
            ds_scr[...] = jnp.zeros_like(ds_scr)
            dwau_ref[...] = jnp.zeros_like(dwau_ref)
            dba_ref[...] = jnp.zeros_like(dba_ref)

        z, b, bmid, blast, causal = _gla_decay(a_ref, wau_ref, ba_ref)
        dlasts = []
        for h in range(HEADS):
            ks = slice(h * DKH, (h + 1) * DKH)
            vs = slice(h * DVH, (h + 1) * DVH)
            bh, mh, lh = b[:, ks], bmid[:, ks], blast[:, ks]
            q = qk_ref[:, ks].astype(F32) * (DKH ** -0.5)
            k = qk_ref[:, DK + h * DKH:DK + (h + 1) * DKH].astype(F32)
            v = v_ref[:, vs]
            dout = do_ref[:, vs]
            eq, ek, eb, eg, el = jnp.exp(bh - mh), jnp.exp(mh - bh), jnp.exp(bh), jnp.exp(lh - bh), jnp.exp(lh)
            qt, kt = (q * eq).astype(BF16), (k * ek).astype(BF16)
            qg, kg = (q * eb).astype(BF16), (k * eg).astype(BF16)
            st = s_ref[0, vs, :]
            dsn = ds_scr[vs, :]
            st16, dsn16 = st.astype(BF16), dsn.astype(BF16)
            a = jnp.where(causal, _dot_nt(qt, kt), 0.0).astype(BF16)
            da = jnp.where(causal, _dot_nt(dout, v), 0.0).astype(BF16)
            dq_inter = _dot(dout, st16) * eb
            dk_inter = _dot(v, dsn16) * eg
            dq = _dot(da, kt) * eq + dq_inter
            dk = _dot_tn(da, qt) * ek + dk_inter
            dv = _dot_tn(a, dout) + _dot_nt(kg, dsn16)
            dlasts.append(jnp.sum(k * dk_inter, axis=0, keepdims=True) + jnp.sum(st * dsn, axis=0, keepdims=True) * el[0:1, :])
            dla_scr[:, ks] = q * dq - k * dk
            ds_scr[vs, :] = dsn * jnp.concatenate([el, el], axis=0) + _dot_tn(dout, qg)
            dp_ref[:, ks] = (dq * (DKH ** -0.5)).astype(BF16)
            dp_ref[:, DK + h * DKH:DK + (h + 1) * DKH] = dk.astype(BF16)
            dp_ref[:, D + h * DVH:D + (h + 1) * DVH] = dv.astype(BF16)
        r = lax.broadcasted_iota(jnp.int32, (CH, CH), 0)
        c = lax.broadcasted_iota(jnp.int32, (CH, CH), 1)
        dla = _tri_matmul((r <= c).astype(BF16), dla_scr[...]) + jnp.concatenate(dlasts, axis=1)
        dz = (dla * (1.0 / TAU) * _sigmoid(-z)).astype(BF16)
        da_ref[...] = _dot_nt(dz, wau_ref[...]).astype(BF16)
        dwau_ref[...] += _dot_tn(a_ref[...], dz)
        dba_ref[...] += _rows8(dz.astype(F32))

    rev = lambda w, col: pl.BlockSpec((CH, w), lambda c: (nc - 1 - c, col))
    return _carried_call(
        body, carry, name="gla_bwd", grid=(nc,),
        in_specs=[rev(D, 0), rev(D, 1), rev(LANES, 0), _resident((LANES, DK)), _resident((1, DK)), rev(D, 0),
                  pl.BlockSpec((1, DV, DKH), lambda c: (nc - 1 - c, 0, 0)), pl.BlockSpec(memory_space=pl.ANY)],
        out_specs=[rev(2 * D, 0), rev(LANES, 0), _resident((LANES, DK)), _resident((8, DK))],
        out_shape=[SDS((t, NPROJ), BF16), SDS((t, LANES), BF16), SDS((LANES, DK), F32), SDS((8, DK), F32)],
        scratch_shapes=[pltpu.VMEM((DV, DKH), F32), pltpu.VMEM((CH, DK), F32)],
        input_output_aliases={7: 0}, compiler_params=_ARB1)(proj, proj, alr, wau, balpha, do, sall, dproj)


def _all_gather(xs, name):
    n = len(xs)

    def body(*refs):
        x_refs, out_refs = refs[:n], refs[n:2 * n]
        send_sems, recv_sems = refs[2 * n:]
        x, y, c = _my_place()
        me, sibling = (x, y, c), (x, y, 1 - c)
        x_nbr, y_nbr, diagonal = (1 - x, y), (x, 1 - y), (1 - x, 1 - y)

        def slot(p, px, py, pc):
            return out_refs[p].at[4 * px + 2 * py + pc]

        def copy(p, k, block, to, src=None):
            return pltpu.make_async_remote_copy(
                src_ref=slot(p, *block) if src is None else src, dst_ref=slot(p, *block),
                send_sem=send_sems.at[7 * p + k], recv_sem=recv_sems.at[7 * p + k], device_id=to, device_id_type=MESH_T)

        for p in range(n):
            for k, to in enumerate((sibling, (*x_nbr, c), (*y_nbr, c))):
                copy(p, k, me, to, src=x_refs[p]).start()
        for p in range(n):
            @pl.when(c == 1)
            def _():
                copy(p, 2, (*y_nbr, c), me).wait_recv()
                copy(p, 3, (*y_nbr, c), (*x_nbr, c)).start()
                copy(p, 1, (*x_nbr, c), me).wait_recv()

            @pl.when(c == 0)
            def _():
                copy(p, 1, (*x_nbr, c), me).wait_recv()
                copy(p, 3, (*x_nbr, c), (*y_nbr, c)).start()
                copy(p, 2, (*y_nbr, c), me).wait_recv()
        for p in range(n):
            copy(p, 4, (*x_nbr, c), sibling).start()
            copy(p, 5, (*y_nbr, c), sibling).start()
        for p in range(n):
            copy(p, 3, (*diagonal, c), me).wait_recv()
            copy(p, 6, (*diagonal, c), sibling).start()
        for p in range(n):
            copy(p, 0, sibling, me).wait_recv()
            for j, chip in enumerate((x_nbr, y_nbr, diagonal)):
                copy(p, 4 + j, (*chip, 1 - c), me).wait_recv()
        for p in range(n):
            for k in range(7):
                copy(p, k, me, sibling, src=x_refs[p]).wait_send()

    hbm = pl.BlockSpec(memory_space=pl.ANY)
    return pl.pallas_call(
        body, name=name, out_shape=[SDS((N_DEV, *a.shape), a.dtype) for a in xs],
        in_specs=[hbm] * n, out_specs=[hbm] * n,
        scratch_shapes=[pltpu.SemaphoreType.DMA((7 * n,)), pltpu.SemaphoreType.DMA((7 * n,))])(*xs)


def _exchange(gs):
    n = len(gs)

    def body(*refs):
        g_refs, land_refs = refs[:n], refs[n:2 * n]
        send_sems, recv_sems, local_sems = refs[2 * n:]
        x, y, c = _my_place()
        my_idx = 4 * x + 2 * y + c
        mine = [pltpu.make_async_copy(g_refs[p].at[my_idx], land_refs[p].at[my_idx], local_sems.at[p]) for p in range(n)]
        for cp in mine:
            cp.start()
        copies = []
        for k in range(1, N_DEV):
            px, py, pc = _flip(x, k & 4), _flip(y, k & 2), _flip(c, k & 1)
            p_idx = 4 * px + 2 * py + pc
            for p in range(n):
                s = 7 * p + k - 1
                cp = pltpu.make_async_remote_copy(
                    src_ref=g_refs[p].at[p_idx], dst_ref=land_refs[p].at[my_idx], send_sem=send_sems.at[s],
                    recv_sem=recv_sems.at[s], device_id=(px, py, pc), device_id_type=MESH_T)
                cp.start()
                arrival = pltpu.make_async_remote_copy(
                    src_ref=g_refs[p].at[p_idx], dst_ref=land_refs[p].at[p_idx], send_sem=send_sems.at[s],
                    recv_sem=recv_sems.at[s], device_id=(px, py, pc), device_id_type=MESH_T)
                copies.append((cp, arrival))
        for cp, arrival in copies:
            arrival.wait_recv()
        for cp, arrival in copies:
            cp.wait_send()
        for cp in mine:
            cp.wait()

    hbm = pl.BlockSpec(memory_space=pl.ANY)
    return pl.pallas_call(
        body, name="grad_exchange", out_shape=[SDS(g.shape, g.dtype) for g in gs],
        in_specs=[hbm] * n, out_specs=[hbm] * n,
        scratch_shapes=[pltpu.SemaphoreType.DMA((7 * n,)), pltpu.SemaphoreType.DMA((7 * n,)),
                        pltpu.SemaphoreType.DMA((n,))])(*gs)


def _adamw(land, w, m, v, rows_blk, name):
    rows = w.shape[0]

    def body(l_ref, w_ref, m_ref, v_ref, g_ref, d_ref, nm_ref, nv_ref):
        g = l_ref[0].astype(F32)
        for s in range(1, N_DEV):
            g = g + l_ref[s].astype(F32)
        nm = ADAM_B1 * m_ref[...] + (1.0 - ADAM_B1) * g
        nv = ADAM_B2 * v_ref[...] + (1.0 - ADAM_B2) * (g * g)
        m_hat = nm / (1.0 - ADAM_B1 ** ADAM_STEP)
        v_hat = nv / (1.0 - ADAM_B2 ** ADAM_STEP)
        g_ref[...] = g
        d_ref[...] = -ADAM_LR * (m_hat / (jnp.sqrt(v_hat) + ADAM_EPS) + ADAM_WD * w_ref[...])
        nm_ref[...] = nm
        nv_ref[...] = nv

    blk = pl.BlockSpec((rows_blk, D), lambda i: (i, 0))
    return pl.pallas_call(
        body, name=name, grid=(rows // rows_blk,),
        in_specs=[pl.BlockSpec((N_DEV, rows_blk, D), lambda i: (0, i, 0)), blk, blk, blk],
        out_specs=[blk] * 4, out_shape=[SDS((rows, D), F32)] * 4, compiler_params=_ARB1)(land, w, m, v)


BIG = ("w_in", "w_up", "w_down", "w_gla_o", "w_conf_o", "w_out")
BIG_TRANSPOSED = ("w_in", "w_up")
SMALL_SHARDED = ("meta_tokens", "conf_dw_w", "ffn_dw_w", "w_alpha_up")
REPLICATED = ("norm_mix_g", "b_alpha", "gla_norm_g", "conf_dw_b", "conf_ln_g", "conf_ln_b", "norm_ffn_g", "ffn_dw_b",
              "final_norm_g")
N_IN = sum(IN_WIDTHS)
W_IN_ROWS = N_IN // N_DEV
W_IN_PAD = -(-W_IN_ROWS // 16) * 16
ADAM_BLOCK = {"w_in": W_IN_PAD // 3, "w_up": 176, "w_down": 176, "w_gla_o": 128, "w_conf_o": 128, "w_out": 128}
SMALL_ROWS = 32


def _to_panel(name, shard):
    a = shard.reshape(shard.shape[-2], shard.shape[-1])
    if name in BIG_TRANSPOSED:
        a = a.T
    if name == "w_in":
        a = jnp.pad(a, ((0, W_IN_PAD - W_IN_ROWS), (0, 0)))
    return a


def _from_panel(name, panel, shape):
    a = panel[0:W_IN_ROWS] if name == "w_in" else panel
    if name in BIG_TRANSPOSED:
        a = a.T
    return a.reshape(shape)


def _pack_small(arrs):
    flat = jnp.concatenate([jnp.pad(a.reshape(-1), (0, (-a.size) % D)) for a in arrs])
    return jnp.pad(flat, (0, SMALL_ROWS * D - flat.shape[0])).reshape(SMALL_ROWS, D)


def _unpack_small(panel, shapes):
    flat, out, off = panel.reshape(-1), [], 0
    for shp in shapes:
        n = 1
        for s in shp:
            n *= s
        out.append(flat[off:off + n].reshape(shp))
        off += n + (-n) % D
    return out


def _local_step(x, target, w, shards=None):
    dist = shards is not None
    w = dict(w)

    def gather(names):
        return [(shards[n], False) for n in names] if dist else []

    def scatter(*arrs):
        return [(a.reshape(N_DEV, -1, D), True) for a in arrs] if dist else []

    s = x.shape[0]
    n_real = s + N_META
    t = -(-n_real // TM) * TM

    q0, r0, a0, c0 = 0, 2 * DK + DV, 2 * DK + 2 * DV, 2 * DK + 2 * DV + RANK
    wt = w["w_in_t"]
    w_main = jnp.concatenate([wt[q0:r0], wt[c0:N_IN], wt[r0:a0]], axis=0)
    w_a = jnp.pad(wt[a0:c0], ((0, LANES - RANK), (0, 0)))
    wau = jnp.pad(w["w_alpha_up"].astype(BF16), ((0, LANES - RANK), (0, 0)))
    row = lambda name: w[name].reshape(1, -1)
    cw = jnp.broadcast_to(jnp.pad(w["conf_dw_w"], ((0, 32 - CONF_K), (0, 0)))[:, None, :], (32, 8, D))
    fw = jnp.broadcast_to(jnp.concatenate([w["ffn_dw_w"], w["ffn_dw_b"].reshape(1, -1)], axis=0)[:, None, :],
                          (FFN_K + 1, 16, DFF))

    early = ("w_gla_o", "w_conf_o", "w_out")
    h0, u1, proj, alr, *landed = _in_proj(x, w["meta_tokens"], row("norm_mix_g"), w_main, w_a, t, carry=gather(early))
    for n, land in zip(early, landed):
        w[n] = land.reshape(-1, D)
    o, og, sall, *landed = _gla_fwd(proj, alr, wau, row("b_alpha"), row("gla_norm_g"), carry=gather(("w_down",)))
    if dist:
        w["w_down"] = landed[0].reshape(-1, D)
    cc, cs, *landed = _conf_fwd(proj, cw, row("conf_dw_b"), row("conf_ln_g"), row("conf_ln_b"), carry=gather(("w_up",)))
    if dist:
        w["w_up_t"] = landed[0].reshape(-1, D)
    brg, brc, merged, h1 = _mix_fwd(og, cs, proj, h0, w["w_gla_o"], w["w_conf_o"], w["w_out"])
    u2, up = _norm_matmul(h1, row("norm_ffn_g"), w["w_up_t"], 512, "up_proj")
    f, dh2, red = _ffn_out(up, fw, h1, w["w_down"], row("final_norm_g"), target)
    loss = 0.5 / D * jnp.sum(red[0:8])

    g = {"final_norm_g": jnp.sum(red[8:16], axis=0)}
    dup, dfw = _ffn_bwd(dh2, w["w_down"], up, fw)
    g["ffn_dw_w"] = jnp.sum(dfw[0:FFN_K], axis=1)
    g["ffn_dw_b"] = jnp.sum(dfw[3], axis=0)
    g["w_down"] = _wgrad(f, dh2, 1408, "wgrad_down")
    dh1, dg2, *landed = _dgrad_norm(dup, w["w_up_t"], h1, row("norm_ffn_g"), dh2, "up_dgrad", carry=scatter(g["w_down"]))
    if dist:
        g["w_down"] = landed[0]
    g["norm_ffn_g"] = jnp.sum(dg2, axis=0)
    g["w_up_t"] = _wgrad(dup, u2, 1408, "wgrad_up")
    dbrg, dbrc, dog, dcs, dproj = _mix_bwd(dh1, w["w_out"], w["w_gla_o"], w["w_conf_o"], proj, brg, brc)
    g["w_out"] = _wgrad(merged, dh1, 1024, "wgrad_out")
    g["w_gla_o"] = _wgrad(og, dbrg, 1024, "wgrad_gla_o")
    g["w_conf_o"] = _wgrad(cs, dbrc, 1024, "wgrad_conf_o")
    do, dproj, dgn = _glapost_bwd(dog, o, proj, row("gla_norm_g"), dproj)
    g["gla_norm_g"] = jnp.sum(dgn, axis=0)
    dproj, dcw, dst, *landed = _conf_bwd(dcs, cc, proj, cw, row("conf_ln_g"), row("conf_ln_b"), dproj,
                                         carry=scatter(g["w_up_t"]))
    if dist:
        g["w_up_t"] = landed[0]
    g["conf_dw_w"] = jnp.sum(dcw[0:CONF_K], axis=1)
    g["conf_ln_g"], g["conf_ln_b"], g["conf_dw_b"] = jnp.sum(dst[0], axis=0), jnp.sum(dst[1], axis=0), jnp.sum(dst[2], axis=0)
    dproj, dalr, dwau, dba, *landed = _gla_bwd(proj, alr, wau, row("b_alpha"), do, sall, dproj,
                                               carry=scatter(g["w_out"], g["w_gla_o"], g["w_conf_o"]))
    if dist:
        g["w_out"], g["w_gla_o"], g["w_conf_o"] = landed
    g["w_alpha_up"] = dwau[0:RANK]
    g["b_alpha"] = jnp.sum(dba, axis=0)
    dw_main = _wgrad(dproj, u1, 1024, "wgrad_in")
    dw_a = _wgrad(dalr, u1, LANES, "wgrad_alr")
    g["w_in_t"] = jnp.concatenate([dw_main[0:r0], dw_main[NPROJ - DV:NPROJ], dw_a[0:RANK], dw_main[r0:NPROJ - DV]], axis=0)
    w_in_blocks = []
    if dist:
        pad = ((0, 0), (0, W_IN_PAD - W_IN_ROWS), (0, 0))
        w_in_blocks = [(jnp.pad(g["w_in_t"].reshape(N_DEV, W_IN_ROWS, D), pad), True)]
    dh0, dg1, *landed = _in_dgrad(dproj, w_main, h0, row("norm_mix_g"), dh1, dalr, w_a, carry=w_in_blocks)
    if dist:
        g["w_in_t"] = landed[0]
    g["norm_mix_g"] = jnp.sum(dg1, axis=0)
    g["meta_tokens"] = dh0[0:N_META]
    return loss, dh0[N_META:n_real], g


def kernel(x, meta_tokens, norm_mix_g, w_in, w_alpha_up, b_alpha, gla_norm_g, w_gla_o, conf_dw_w, conf_dw_b, conf_ln_g, conf_ln_b, w_conf_o, w_out, norm_ffn_g, w_up, ffn_dw_w, ffn_dw_b, w_down, final_norm_g, loss_target, m_meta_tokens, m_norm_mix_g, m_w_in, m_w_alpha_up, m_b_alpha, m_gla_norm_g, m_w_gla_o, m_conf_dw_w, m_conf_dw_b, m_conf_ln_g, m_conf_ln_b, m_w_conf_o, m_w_out, m_norm_ffn_g, m_w_up, m_ffn_dw_w, m_ffn_dw_b, m_w_down, m_final_norm_g, v_meta_tokens, v_norm_mix_g, v_w_in, v_w_alpha_up, v_b_alpha, v_gla_norm_g, v_w_gla_o, v_conf_dw_w, v_conf_dw_b, v_conf_ln_g, v_conf_ln_b, v_w_conf_o, v_w_out, v_norm_ffn_g, v_w_up, v_ffn_dw_w, v_ffn_dw_b, v_w_down, v_final_norm_g):
    ws = dict(meta_tokens=meta_tokens, norm_mix_g=norm_mix_g, w_in=w_in, w_alpha_up=w_alpha_up, b_alpha=b_alpha,
              gla_norm_g=gla_norm_g, w_gla_o=w_gla_o, conf_dw_w=conf_dw_w, conf_dw_b=conf_dw_b, conf_ln_g=conf_ln_g,
              conf_ln_b=conf_ln_b, w_conf_o=w_conf_o, w_out=w_out, norm_ffn_g=norm_ffn_g, w_up=w_up, ffn_dw_w=ffn_dw_w,
              ffn_dw_b=ffn_dw_b, w_down=w_down, final_norm_g=final_norm_g)
    ms = dict(meta_tokens=m_meta_tokens, norm_mix_g=m_norm_mix_g, w_in=m_w_in, w_alpha_up=m_w_alpha_up, b_alpha=m_b_alpha,
              gla_norm_g=m_gla_norm_g, w_gla_o=m_w_gla_o, conf_dw_w=m_conf_dw_w, conf_dw_b=m_conf_dw_b,
              conf_ln_g=m_conf_ln_g, conf_ln_b=m_conf_ln_b, w_conf_o=m_w_conf_o, w_out=m_w_out, norm_ffn_g=m_norm_ffn_g,
              w_up=m_w_up, ffn_dw_w=m_ffn_dw_w, ffn_dw_b=m_ffn_dw_b, w_down=m_w_down, final_norm_g=m_final_norm_g)
    vs = dict(meta_tokens=v_meta_tokens, norm_mix_g=v_norm_mix_g, w_in=v_w_in, w_alpha_up=v_w_alpha_up, b_alpha=v_b_alpha,
              gla_norm_g=v_gla_norm_g, w_gla_o=v_w_gla_o, conf_dw_w=v_conf_dw_w, conf_dw_b=v_conf_dw_b,
              conf_ln_g=v_conf_ln_g, conf_ln_b=v_conf_ln_b, w_conf_o=v_w_conf_o, w_out=v_w_out, norm_ffn_g=v_norm_ffn_g,
              w_up=v_w_up, ffn_dw_w=v_ffn_dw_w, ffn_dw_b=v_ffn_dw_b, w_down=v_w_down, final_norm_g=v_final_norm_g)
    small = SMALL_SHARDED + REPLICATED
    pack_small = lambda d: _pack_small([d[n] for n in small])

    shards = {n: _to_panel(n, ws[n]).astype(BF16) for n in BIG}
    own = [shards["w_in"], pack_small(ws)]
    my_idx = 4 * lax.axis_index("x") + 2 * lax.axis_index("y") + lax.axis_index("c")
    gathered = [lax.dynamic_update_slice(full_, mine[None], (my_idx, 0, 0))
                for full_, mine in zip(_all_gather(own, "weight_gather"), own)]
    full = {n: ws[n].reshape(-1) for n in REPLICATED}
    full["w_in_t"] = gathered[0][:, 0:W_IN_ROWS].reshape(N_IN, D)
    flat, off = gathered[1].reshape(N_DEV, -1), 0
    for n in SMALL_SHARDED:
        k, c = ws[n].shape[-2], ws[n].shape[-1]
        full[n] = flat[:, off:off + k * c].reshape(N_DEV, k, c).transpose(1, 0, 2).reshape(k, N_DEV * c)
        off += k * c + (-(k * c)) % D

    loss, grad_x, g = _local_step(x[0], loss_target[0], full, shards)

    lands = [g["w_in_t"], g["w_up_t"]] + [g[n] for n in BIG[2:]]
    blocks = []
    for n in SMALL_SHARDED:
        k, c = ws[n].shape[-2], ws[n].shape[-1]
        b = g[n].reshape(k, N_DEV, c).transpose(1, 0, 2).reshape(N_DEV, k * c)
        blocks.append(jnp.pad(b, ((0, 0), (0, (-(k * c)) % D))))
    for n in REPLICATED:
        b = jnp.broadcast_to(g[n].reshape(1, -1), (N_DEV, g[n].size))
        blocks.append(jnp.pad(b, ((0, 0), (0, (-b.shape[1]) % D))))
    gsm = jnp.concatenate(blocks, axis=1)
    lands += _exchange([jnp.pad(gsm, ((0, 0), (0, SMALL_ROWS * D - gsm.shape[1]))).reshape(N_DEV, SMALL_ROWS, D)])

    grad, delta, new_m, new_v = {}, {}, {}, {}
    for i, n in enumerate(BIG):
        outs = _adamw(lands[i], _to_panel(n, ws[n]), _to_panel(n, ms[n]), _to_panel(n, vs[n]), ADAM_BLOCK[n], "adamw_" + n)
        grad[n], delta[n], new_m[n], new_v[n] = [_from_panel(n, p, ws[n].shape) for p in outs]
    outs = _adamw(lands[len(BIG)], pack_small(ws), pack_small(ms), pack_small(vs), SMALL_ROWS, "adamw_small")
    shapes = [ws[n].shape for n in small]
    for d, p in zip((grad, delta, new_m, new_v), outs):
        d.update(zip(small, _unpack_small(p, shapes)))

    order = ("meta_tokens", "norm_mix_g", "w_in", "w_alpha_up", "b_alpha", "gla_norm_g", "w_gla_o", "conf_dw_w", "conf_dw_b",
             "conf_ln_g", "conf_ln_b", "w_conf_o", "w_out", "norm_ffn_g", "w_up", "ffn_dw_w", "ffn_dw_b", "w_down",
             "final_norm_g")
    loss = lax.psum(loss, ("x", "y", "c"))
    return (loss, grad_x[None], *[grad[n] for n in order], *[delta[n] for n in order], *[new_m[n] for n in order],
            *[new_v[n] for n in order])
```

```python
import functools

import jax
import jax.numpy as jnp
from jax import lax
from jax.experimental import pallas as pl
from jax.experimental.pallas import tpu as pltpu

F32, BF16 = jnp.float32, jnp.bfloat16
SDS = jax.ShapeDtypeStruct

D = 1024
N_META = 16
HEADS = 4
DK, DKH, DV, DVH = 512, 128, 1024, 256
RANK = 16
TAU = 16.0
CONF_K = 31
DFF = 2816
FFN_K = 3
IN_WIDTHS = (DK, DK, DV, DV, RANK, 2 * D, D, D)
RMS_EPS, LN_EPS = 1e-6, 1e-5
ADAM_LR, ADAM_B1, ADAM_B2, ADAM_EPS, ADAM_WD, ADAM_STEP = 0.001, 0.9, 0.999, 1e-08, 0.01, 10

NPROJ = 7 * D
LANES = 128
CH = 128
TM = 640
TM_BIG = 1664
TE = 320
HALO = 32
HALO_F = 16
N_DEV = 8
VMEM_LIMIT = 60 * 1024 * 1024
MESH_T = pl.DeviceIdType.MESH

_ARB1 = pltpu.CompilerParams(dimension_semantics=("arbitrary",), vmem_limit_bytes=VMEM_LIMIT)
_ARB2 = pltpu.CompilerParams(dimension_semantics=("arbitrary", "arbitrary"), vmem_limit_bytes=VMEM_LIMIT)


def _dot(a, b):
    return jnp.dot(a, b, preferred_element_type=F32)


def _dot_nt(a, b):
    return lax.dot_general(a, b, (((1,), (1,)), ((), ())), preferred_element_type=F32)


def _dot_tn(a, b):
    return lax.dot_general(a, b, (((0,), (0,)), ((), ())), preferred_element_type=F32)


def _sigmoid(x):
    return 0.5 * jnp.tanh(0.5 * x) + 0.5


def _rows8(x):
    return x.reshape(x.shape[0] // 8, 8, x.shape[1]).sum(axis=0)


def _row_tile(t, preferred):
    return preferred if t % preferred == 0 else TM


def _row_loop(n_rows, rb, fn, unroll=1):
    def step(i, carry):
        fn(pl.multiple_of(i * rb, rb))
        return carry
    lax.fori_loop(0, n_rows // rb, step, 0, unroll=unroll)


def _resident(shape):
    return pl.BlockSpec(shape, lambda *_: (0,) * len(shape))


def _once(shape):
    return pl.BlockSpec(shape, lambda *_: (0,) * len(shape), pipeline_mode=pl.Buffered(1))


CONV_ROWS, CONV_LANES = 64, 256


def _shift_classes(offset_taps):
    return [(s, [(o - s, j) for o, j in offset_taps if o % 8 == s]) for s in range(8)]


def _shifted(win, s):
    return win if s == 0 else win[s:s + CONV_ROWS + HALO - 8, :]


def _my_place():
    return lax.axis_index("x"), lax.axis_index("y"), lax.axis_index("c")


def _flip(v, bit):
    return 1 - v if bit else v


def _exchange_copies(src_refs, land_refs, scatter, send_sems, recv_sems, local_sems, arrivals):
    x, y, c = _my_place()
    my_idx = 4 * x + 2 * y + c
    local, remote = [], []
    for p, (src, land) in enumerate(zip(src_refs, land_refs)):
        local.append(pltpu.make_async_copy(src.at[my_idx] if scatter[p] else src, land.at[my_idx], local_sems.at[p]))
    for k in range(1, N_DEV):
        px, py, pc = _flip(x, k & 4), _flip(y, k & 2), _flip(c, k & 1)
        p_idx = 4 * px + 2 * py + pc
        for p, (src, land) in enumerate(zip(src_refs, land_refs)):
            s = 7 * p + k - 1
            out = src.at[p_idx] if scatter[p] else src

            def copy(dst):
                return pltpu.make_async_remote_copy(src_ref=out, dst_ref=dst, send_sem=send_sems.at[s],
                                                    recv_sem=recv_sems.at[s], device_id=(px, py, pc), device_id_type=MESH_T)
            remote.append((copy(land.at[my_idx]), copy(land.at[p_idx]) if arrivals else None))
    return local, remote


def _carried_call(core, carry, *, grid, in_specs, out_specs, out_shape, scratch_shapes=(), **kw):
    n_in, n_out, nc, n_scr = len(in_specs), len(out_specs), len(carry), len(scratch_shapes)
    if nc == 0:
        return pl.pallas_call(core, grid=grid, in_specs=in_specs, out_specs=out_specs, out_shape=out_shape,
                              scratch_shapes=list(scratch_shapes), **kw)
    scatter = [sc for _, sc in carry]

    def body(*refs):
        ins, cin = refs[:n_in], refs[n_in:n_in + nc]
        outs, cout = refs[n_in + nc:n_in + nc + n_out], refs[n_in + nc + n_out:n_in + 2 * nc + n_out]
        scr, sems = refs[n_in + 2 * nc + n_out:n_in + 2 * nc + n_out + n_scr], refs[-3:]
        first = functools.reduce(jnp.logical_and, [pl.program_id(a) == 0 for a in range(len(grid))])
        last = functools.reduce(jnp.logical_and, [pl.program_id(a) == grid[a] - 1 for a in range(len(grid))])

        @pl.when(first)
        def _():
            local, remote = _exchange_copies(cin, cout, scatter, *sems, arrivals=False)
            for cp in local:
                cp.start()
            for send, _ in remote:
                send.start()

        core(*ins, *outs, *scr)

        @pl.when(last)
        def _():
            local, remote = _exchange_copies(cin, cout, scatter, *sems, arrivals=True)
            for _, arrival in remote:
                arrival.wait_recv()
            for send, _ in remote:
                send.wait_send()
            for cp in local:
                cp.wait()

    hbm = pl.BlockSpec(memory_space=pl.ANY)
    land_shape = [SDS((N_DEV, *(a.shape[1:] if sc else a.shape)), a.dtype) for a, sc in carry]
    sems = [pltpu.SemaphoreType.DMA((7 * nc,)), pltpu.SemaphoreType.DMA((7 * nc,)), pltpu.SemaphoreType.DMA((nc,))]
    call = pl.pallas_call(body, grid=grid, in_specs=list(in_specs) + [hbm] * nc, out_specs=list(out_specs) + [hbm] * nc,
                          out_shape=list(out_shape) + land_shape, scratch_shapes=list(scratch_shapes) + sems, **kw)
    return lambda *args: call(*args, *[a for a, _ in carry])


def _norm_matmul(h, g, w_t, tn, name, w_extra_t=None, carry=()):
    t, n = h.shape[0], w_t.shape[0]
    tm = _row_tile(t, TM_BIG)
    nt, nb = t // tm, n // tn

    def body(*refs):
        if w_extra_t is None:
            h_ref, g_ref, w_ref, u_ref, p_ref = refs
        else:
            h_ref, g_ref, w_ref, we_ref, u_ref, p_ref, e_ref = refs

        @pl.when(pl.program_id(1) == 0)
        def _():
            def blk(r0):
                x = h_ref[pl.ds(r0, 32), :]
                rinv = lax.rsqrt(jnp.mean(x * x, axis=-1, keepdims=True) + RMS_EPS)
                u_ref[pl.ds(r0, 32), :] = (x * rinv * g_ref[...]).astype(BF16)
            _row_loop(tm, 32, blk, unroll=13)
            if w_extra_t is not None:
                e_ref[...] = _dot_nt(u_ref[...], we_ref[...]).astype(BF16)

        p_ref[...] = _dot_nt(u_ref[...], w_ref[...]).astype(BF16)

    in_specs = [pl.BlockSpec((tm, D), lambda i, j: (i, 0)), _resident((1, D)), pl.BlockSpec((tn, D), lambda i, j: (j, 0))]
    out_specs = [pl.BlockSpec((tm, D), lambda i, j: (i, 0)), pl.BlockSpec((tm, tn), lambda i, j: (i, j))]
    out_shape = [SDS((t, D), BF16), SDS((t, n), BF16)]
    args = [h, g, w_t]
    if w_extra_t is not None:
        in_specs.append(_resident(w_extra_t.shape))
        out_specs.append(pl.BlockSpec((tm, w_extra_t.shape[0]), lambda i, j: (i, 0)))
        out_shape.append(SDS((t, w_extra_t.shape[0]), BF16))
        args.append(w_extra_t)
    return _carried_call(body, carry, name=name, grid=(nt, nb), in_specs=in_specs, out_specs=out_specs,
                         out_shape=out_shape, compiler_params=_ARB2)(*args)


def _in_proj(x, meta, g, w_t, w_a_t, t, carry=()):
    n_real = x.shape[0] + N_META
    n, tn = w_t.shape[0], 1024
    tm = _row_tile(t, TM_BIG)
    nt, nb = t // tm, n // tn

    def body(x_ref, xh_ref, m_ref, g_ref, w_ref, wa_ref, h_ref, u_ref, p_ref, e_ref):
        i = pl.program_id(0)

        @pl.when(pl.program_id(1) == 0)
        def _():
            def rows_of(r0, val):
                gid = i * tm + r0 + lax.broadcasted_iota(jnp.int32, (32, 1), 0)
                val = jnp.where(gid < n_real, val, 0.0)
                h_ref[pl.ds(r0, 32), :] = val
                rinv = lax.rsqrt(jnp.mean(val * val, axis=-1, keepdims=True) + RMS_EPS)
                u_ref[pl.ds(r0, 32), :] = (val * rinv * g_ref[...]).astype(BF16)

            before = jnp.where(i == 0, m_ref[...], xh_ref[...])
            rows_of(0, jnp.concatenate([before, x_ref[0:N_META, :]], axis=0))

            def blk(k, c):
                r0 = pl.multiple_of(k * 32, 32)
                rows_of(r0, x_ref[pl.ds(pl.multiple_of(r0 - N_META, N_META), 32), :])
                return c
            lax.fori_loop(1, tm // 32, blk, 0, unroll=17)
            e_ref[...] = _dot_nt(u_ref[...], wa_ref[...]).astype(BF16)

        p_ref[...] = _dot_nt(u_ref[...], w_ref[...]).astype(BF16)

    row = lambda w: pl.BlockSpec((tm, w), lambda i, j: (i, 0))
    return _carried_call(
        body, carry, name="in_proj", grid=(nt, nb),
        in_specs=[row(D), pl.BlockSpec((N_META, D), lambda i, j: (jnp.maximum(i * (tm // N_META) - 1, 0), 0)),
                  _resident((N_META, D)), _resident((1, D)), pl.BlockSpec((tn, D), lambda i, j: (j, 0)),
                  _resident(w_a_t.shape)],
        out_specs=[row(D), row(D), pl.BlockSpec((tm, tn), lambda i, j: (i, j)), row(w_a_t.shape[0])],
        out_shape=[SDS((t, D), F32), SDS((t, D), BF16), SDS((t, n), BF16), SDS((t, w_a_t.shape[0]), BF16)],
        compiler_params=_ARB2)(x, x, meta, g, w_t, w_a_t)


def _gla_decay(a_ref, wau_ref, ba_ref):
    z = _dot(a_ref[...], wau_ref[...]) + ba_ref[...]
    la = (jnp.minimum(z, 0.0) - jnp.log(1.0 + jnp.exp(-jnp.abs(z)))) * (1.0 / TAU)
    r = lax.broadcasted_iota(jnp.int32, (CH, CH), 0)
    c = lax.broadcasted_iota(jnp.int32, (CH, CH), 1)
    b = _tri_matmul((r >= c).astype(BF16), la)
    return z, b, r >= c


def _tri_matmul(tri, x):
    n = x.shape[1]
    x1 = x.astype(BF16)
    x2 = (x - x1.astype(F32)).astype(BF16)
    y = _dot(tri, jnp.concatenate([x1, x2], axis=1))
    return y[:, 0:n] + y[:, n:2 * n]


def _gla_fwd(proj, alr, wau, balpha, gn, carry=()):
    t = proj.shape[0]
    nc = t // CH

    def body(qk_ref, v_ref, r_ref, a_ref, wau_ref, ba_ref, gn_ref, o_ref, og_ref, sall_ref, s_scr, b_scr):
        @pl.when(pl.program_id(0) == 0)
        def _():
            s_scr[...] = jnp.zeros_like(s_scr)

        sall_ref[0] = s_scr[...]
        _, b, causal = _gla_decay(a_ref, wau_ref, ba_ref)
        b_scr[...] = b
        for h in range(HEADS):
            ks = slice(h * DKH, (h + 1) * DKH)
            vs = slice(h * DVH, (h + 1) * DVH)
            bh, mh, lh = b_scr[:, ks], b_scr[CH // 2:CH // 2 + 1, ks], b_scr[CH - 1:CH, ks]
            q = qk_ref[:, ks].astype(F32) * (DKH ** -0.5)
            k = qk_ref[:, DK + h * DKH:DK + (h + 1) * DKH].astype(F32)
            v = v_ref[:, vs]
            qt = (q * jnp.exp(bh - mh)).astype(BF16)
            kt = (k * jnp.exp(mh - bh)).astype(BF16)
            qg = (q * jnp.exp(bh)).astype(BF16)
            kg = (k * jnp.exp(lh - bh)).astype(BF16)
            a = jnp.where(causal, _dot_nt(qt, kt), 0.0)
            st = s_scr[vs, :]
            o = _dot(a.astype(BF16), v) + _dot_nt(qg, st.astype(BF16))
            el = jnp.exp(lh)
            s_scr[vs, :] = st * el + _dot_tn(v, kg)
            o_ref[:, vs] = o
            on = o * lax.rsqrt(jnp.mean(o * o, axis=-1, keepdims=True) + RMS_EPS) * gn_ref[:, vs]
            rr = r_ref[:, vs].astype(F32)
            og_ref[:, vs] = (on * (rr * _sigmoid(rr))).astype(BF16)

    return _carried_call(
        body, carry, name="gla_fwd", grid=(nc,),
        in_specs=[pl.BlockSpec((CH, D), lambda c: (c, 0)), pl.BlockSpec((CH, D), lambda c: (c, 1)),
                  pl.BlockSpec((CH, D), lambda c: (c, 6)), pl.BlockSpec((CH, LANES), lambda c: (c, 0)),
                  _resident((LANES, DK)), _resident((1, DK)), _resident((1, DV))],
        out_specs=[pl.BlockSpec((CH, DV), lambda c: (c, 0)), pl.BlockSpec((CH, DV), lambda c: (c, 0)),
                   pl.BlockSpec((1, DV, DKH), lambda c: (c, 0, 0))],
        out_shape=[SDS((t, DV), F32), SDS((t, DV), BF16), SDS((nc, DV, DKH), F32)],
        scratch_shapes=[pltpu.VMEM((DV, DKH), F32), pltpu.VMEM((CH, DK), F32)],
        compiler_params=_ARB1)(proj, proj, proj, alr, wau, balpha, gn)


def _conf_fwd(proj, cw, cb, lg, lb, carry=()):
    t = proj.shape[0]
    nt = t // TE

    def body(c1_ref, c2_ref, cw_ref, cb_ref, lg_ref, lb_ref, cc_ref, cs_ref, cext):
        i = pl.program_id(0)

        @pl.when(i == 0)
        def _():
            cext[0:HALO, :] = jnp.zeros((HALO, D), F32)

        @pl.when(i > 0)
        def _():
            cext[0:HALO, :] = cext[TE:TE + HALO, :]

        def glu(r0):
            c2 = c2_ref[pl.ds(r0, 32), :].astype(F32)
            cext[pl.ds(HALO + r0, 32), :] = c1_ref[pl.ds(r0, 32), :].astype(F32) * _sigmoid(c2)
        _row_loop(TE, 32, glu)

        def conv(r0):
            for part in range(D // CONV_LANES):
                cols = slice(part * CONV_LANES, (part + 1) * CONV_LANES)
                win = cext[pl.ds(r0, CONV_ROWS + HALO), cols]
                acc = jnp.zeros((CONV_ROWS, CONV_LANES), F32) + cb_ref[:, cols]
                for s, taps in _shift_classes([(2 + j, j) for j in range(CONF_K)]):
                    ws = _shifted(win, s)
                    for a8, j in taps:
                        acc = acc + jnp.tile(cw_ref[j, :, cols], (CONV_ROWS // 8, 1)) * ws[a8:a8 + CONV_ROWS, :]
                cc_ref[pl.ds(r0, CONV_ROWS), cols] = acc

        def norm(r0):
            for sub in range(CONV_ROWS // 32):
                rows = pl.ds(r0 + 32 * sub, 32)
                x = cc_ref[rows, :]
                xc = x - jnp.mean(x, axis=-1, keepdims=True)
                var = jnp.mean(xc * xc, axis=-1, keepdims=True)
                ln = xc * lax.rsqrt(var + LN_EPS) * lg_ref[...] + lb_ref[...]
                cs_ref[rows, :] = (ln * _sigmoid(ln)).astype(BF16)

        def step(r0):
            conv(r0)
            norm(r0)
        _row_loop(TE, CONV_ROWS, step)

    return _carried_call(
        body, carry, name="conf_fwd", grid=(nt,),
        in_specs=[pl.BlockSpec((TE, D), lambda i: (i, 2)), pl.BlockSpec((TE, D), lambda i: (i, 3)),
                  _resident((32, 8, D)), _resident((1, D)), _resident((1, D)), _resident((1, D))],
        out_specs=[pl.BlockSpec((TE, D), lambda i: (i, 0)), pl.BlockSpec((TE, D), lambda i: (i, 0))],
        out_shape=[SDS((t, D), F32), SDS((t, D), BF16)],
        scratch_shapes=[pltpu.VMEM((TE + HALO, D), F32)], compiler_params=_ARB1)(proj, proj, cw, cb, lg, lb)


def _mix_fwd(og, cs, proj, h0, wg, wc, wo):
    t = h0.shape[0]
    nt = t // TM

    def body(og_ref, cs_ref, g_ref, h0_ref, wg_ref, wc_ref, wo_ref, brg_ref, brc_ref, mg_ref, h1_ref):
        brg_ref[...] = _dot(og_ref[...], wg_ref[...]).astype(BF16)
        brc_ref[...] = _dot(cs_ref[...], wc_ref[...]).astype(BF16)

        def blk(r0):
            rows = pl.ds(r0, 32)
            gg = g_ref[rows, 0:D].astype(F32)
            gc = g_ref[rows, D:2 * D].astype(F32)
            m = _sigmoid(gg) * brg_ref[rows, :].astype(F32) + _sigmoid(gc) * brc_ref[rows, :].astype(F32)
            mg_ref[rows, :] = m.astype(BF16)
        _row_loop(TM, 32, blk)
        h1_ref[...] = h0_ref[...] + _dot(mg_ref[...], wo_ref[...])

    row = lambda w: pl.BlockSpec((TM, w), lambda i: (i, 0))
    return pl.pallas_call(
        body, name="mix_fwd", grid=(nt,),
        in_specs=[row(D), row(D), pl.BlockSpec((TM, 2 * D), lambda i: (i, 2)), row(D),
                  _once((D, D)), _once((D, D)), _once((D, D))],
        out_specs=[row(D), row(D), row(D), row(D)],
        out_shape=[SDS((t, D), BF16), SDS((t, D), BF16), SDS((t, D), BF16), SDS((t, D), F32)],
        compiler_params=_ARB1)(og, cs, proj, h0, wg, wc, wo)


def _ffn_out(up, fw, h1, wd, gf, target):
    t = h1.shape[0]
    nt = t // TE
    n_real = target.shape[0] + N_META

    def body(a_ref, bv_ref, fw_ref, h1_ref, wd_ref, gf_ref, tg_ref, tb_ref, f_ref, dh2_ref, red_ref, before, hs):
        i = pl.program_id(0)

        @pl.when(i == 0)
        def _():
            before[...] = jnp.zeros_like(before)
            red_ref[...] = jnp.zeros_like(red_ref)

        def conv(r0, win):
            ac = fw_ref[3] + fw_ref[0] * win[14:30, :] + fw_ref[1] * win[15:31, :] + fw_ref[2] * win[16:32, :]
            f_ref[pl.ds(r0, 16), :] = (ac * _sigmoid(ac) * bv_ref[pl.ds(r0, 16), :].astype(F32)).astype(BF16)

        conv(0, jnp.concatenate([before[...], a_ref[0:HALO_F, :].astype(F32)], axis=0))

        def conv_blk(k, c):
            r0 = pl.multiple_of(k * 16, 16)
            conv(r0, a_ref[pl.ds(r0 - HALO_F, 32), :].astype(F32))
            return c
        lax.fori_loop(1, TE // 16, conv_blk, 0)
        before[...] = a_ref[TE - HALO_F:TE, :].astype(F32)
        hs[...] = h1_ref[...] + _dot(f_ref[...], wd_ref[...])

        def head(r0, tg):
            rows = pl.ds(r0, 32)
            h2 = hs[rows, :]
            rinv = lax.rsqrt(jnp.mean(h2 * h2, axis=-1, keepdims=True) + RMS_EPS)
            hh = h2 * rinv
            gid = i * TE + r0 + lax.broadcasted_iota(jnp.int32, (32, 1), 0)
            live = jnp.logical_and(gid >= N_META, gid < n_real)
            err = jnp.where(live, hh * gf_ref[...] - tg, 0.0)
            dy = err * (1.0 / D)
            red_ref[0:8, :] += _rows8(err * err)
            red_ref[8:16, :] += _rows8(dy * hh)
            dhh = dy * gf_ref[...]
            dh2_ref[rows, :] = rinv * (dhh - hh * jnp.mean(dhh * hh, axis=-1, keepdims=True))

        head(0, jnp.concatenate([tb_ref[...], tg_ref[0:N_META, :]], axis=0))

        def blk(k, c):
            r0 = pl.multiple_of(k * 32, 32)
            head(r0, tg_ref[pl.ds(pl.multiple_of(r0 - N_META, N_META), 32), :])
            return c
        lax.fori_loop(1, TE // 32, blk, 0, unroll=9)

    row = lambda w: pl.BlockSpec((TE, w), lambda i: (i, 0))
    return pl.pallas_call(
        body, name="ffn_out", grid=(nt,),
        in_specs=[pl.BlockSpec((TE, DFF), lambda i: (i, 0)), pl.BlockSpec((TE, DFF), lambda i: (i, 1)),
                  _resident((4, 16, DFF)), row(D), _resident((DFF, D)), _resident((1, D)), row(D),
                  pl.BlockSpec((N_META, D), lambda i: (jnp.maximum(i * (TE // N_META) - 1, 0), 0))],
        out_specs=[row(DFF), row(D), _resident((16, D))],
        out_shape=[SDS((t, DFF), BF16), SDS((t, D), F32), SDS((16, D), F32)],
        scratch_shapes=[pltpu.VMEM((HALO_F, DFF), F32), pltpu.VMEM((TE, D), F32)],
        compiler_params=_ARB1)(up, up, fw, h1, wd, gf, target, target)


def _ffn_bwd(dh2, wd, up, fw):
    t = dh2.shape[0]
    nt = t // TE
    hb = TE // HALO_F

    def body(dh_ref, wd_ref, a_ref, ah_ref, bv_ref, fw_ref, dup_ref, dw_ref, dax, dfs):
        i = pl.program_id(0)
        ti = nt - 1 - i

        @pl.when(i == 0)
        def _():
            dax[TE:TE + HALO_F, :] = jnp.zeros((HALO_F, DFF), F32)
            dw_ref[...] = jnp.zeros_like(dw_ref)

        @pl.when(i > 0)
        def _():
            dax[TE:TE + HALO_F, :] = dax[0:HALO_F, :]

        dfs[...] = _dot_nt(dh_ref[...].astype(BF16), wd_ref[...])

        def act(r0, win):
            rows = pl.ds(r0, 16)
            ac = fw_ref[3] + fw_ref[0] * win[14:30, :] + fw_ref[1] * win[15:31, :] + fw_ref[2] * win[16:32, :]
            sg = _sigmoid(ac)
            df = dfs[rows, :]
            dsg = df * sg
            dup_ref[rows, DFF:2 * DFF] = (dsg * ac).astype(BF16)
            dac = dsg * bv_ref[rows, :].astype(F32) * (1.0 + ac * (1.0 - sg))
            dax[rows, :] = dac
            dw_ref[3] += _rows8(dac)
            for j in range(FFN_K):
                dw_ref[j] += _rows8(dac * win[14 + j:30 + j, :])

        before = jnp.where(ti > 0, ah_ref[...].astype(F32), 0.0)
        act(0, jnp.concatenate([before, a_ref[0:HALO_F, :].astype(F32)], axis=0))

        def act_blk(k, c):
            r0 = pl.multiple_of(k * 16, 16)
            act(r0, a_ref[pl.ds(r0 - HALO_F, 32), :].astype(F32))
            return c
        lax.fori_loop(1, TE // 16, act_blk, 0)

        def convt(r0):
            win = dax[pl.ds(r0, 32), :]
            da = fw_ref[2] * win[0:16, :] + fw_ref[1] * win[1:17, :] + fw_ref[0] * win[2:18, :]
            dup_ref[pl.ds(r0, 16), 0:DFF] = da.astype(BF16)
        _row_loop(TE, 16, convt)

    rev = lambda w: pl.BlockSpec((TE, w), lambda i: (nt - 1 - i, 0))
    return pl.pallas_call(
        body, name="ffn_bwd", grid=(nt,),
        in_specs=[rev(D), _resident((DFF, D)), rev(DFF),
                  pl.BlockSpec((HALO_F, DFF), lambda i: (jnp.maximum((nt - 1 - i) * hb - 1, 0), 0)),
                  pl.BlockSpec((TE, DFF), lambda i: (nt - 1 - i, 1)), _resident((4, 16, DFF))],
        out_specs=[rev(2 * DFF), _resident((4, 8, DFF))],
        out_shape=[SDS((t, 2 * DFF), BF16), SDS((4, 8, DFF), F32)],
        scratch_shapes=[pltpu.VMEM((TE + HALO_F, DFF), F32), pltpu.VMEM((TE, DFF), F32)],
        compiler_params=_ARB1)(dh2, wd, up, up, up, fw)


def _dgrad_norm(dy, w_t, h, g, dres, name, carry=()):
    t, k = dy.shape
    nt = t // TM

    def body(dy_ref, w_ref, h_ref, g_ref, dr_ref, dh_ref, dg_ref, acc):
        @pl.when(pl.program_id(0) == 0)
        def _():
            dg_ref[...] = jnp.zeros_like(dg_ref)

        acc[...] = _dot(dy_ref[...], w_ref[...])

        def blk(r0):
            rows = pl.ds(r0, 32)
            x = h_ref[rows, :]
            rinv = lax.rsqrt(jnp.mean(x * x, axis=-1, keepdims=True) + RMS_EPS)
            hh = x * rinv
            du = acc[rows, :]
            dg_ref[...] += _rows8(du * hh)
            dhh = du * g_ref[...]
            dh_ref[rows, :] = dr_ref[rows, :] + rinv * (dhh - hh * jnp.mean(dhh * hh, axis=-1, keepdims=True))
        _row_loop(TM, 32, blk, unroll=10)

    row = pl.BlockSpec((TM, D), lambda i: (i, 0))
    in_specs = [pl.BlockSpec((TM, k), lambda i: (i, 0)), _once((k, D)), row, _resident((1, D)), row]
    return _carried_call(
        body, carry, name=name, grid=(nt,), in_specs=in_specs, out_specs=[row, _resident((8, D))],
        out_shape=[SDS((t, D), F32), SDS((8, D), F32)],
        scratch_shapes=[pltpu.VMEM((TM, D), F32)], compiler_params=_ARB1)(dy, w_t, h, g, dres)


def _in_dgrad(dy, w_t, h, g, dres, dy_extra, w_extra_t, carry=()):
    t, k = dy.shape
    nt = t // TM

    def body(dy_ref, w_ref, h_ref, g_ref, dr_ref, de_ref, we_ref, dh_ref, dg_ref, acc):
        @pl.when(pl.program_id(0) == 0)
        def _():
            dg_ref[...] = jnp.zeros_like(dg_ref)

        acc[...] = _dot(dy_ref[...], w_ref[...])
        acc[...] += _dot(de_ref[...], we_ref[...])

        def blk(r0):
            rows = pl.ds(r0, 32)
            x = h_ref[rows, :]
            rinv = lax.rsqrt(jnp.mean(x * x, axis=-1, keepdims=True) + RMS_EPS)
            hh = x * rinv
            du = acc[rows, :]
            dg_ref[...] += _rows8(du * hh)
            dhh = du * g_ref[...]
            dh_ref[rows, :] = dr_ref[rows, :] + rinv * (dhh - hh * jnp.mean(dhh * hh, axis=-1, keepdims=True))
        _row_loop(TM, 32, blk, unroll=4)

    tile = lambda w: pl.BlockSpec((TM, w), lambda i: (i, 0))
    return _carried_call(
        body, carry, name="in_dgrad", grid=(nt,),
        in_specs=[tile(k), _once((k, D)), tile(D), _resident((1, D)), tile(D), tile(dy_extra.shape[1]),
                  _once(w_extra_t.shape)],
        out_specs=[tile(D), _resident((8, D))], out_shape=[SDS((t, D), F32), SDS((8, D), F32)],
        scratch_shapes=[pltpu.VMEM((TM, D), F32)], compiler_params=_ARB1)(dy, w_t, h, g, dres, dy_extra, w_extra_t)


def _wgrad(x, dy, tk, name):
    t, k = x.shape
    n = dy.shape[1]
    tm = _row_tile(t, TM_BIG)
    nk, nt = k // tk, t // tm

    def body(x_ref, dy_ref, o_ref, acc):
        @pl.when(pl.program_id(1) == 0)
        def _():
            acc[...] = jnp.zeros_like(acc)
        acc[...] += _dot_tn(x_ref[...], dy_ref[...].astype(BF16))

        @pl.when(pl.program_id(1) == nt - 1)
        def _():
            o_ref[...] = acc[...].astype(BF16)

    return pl.pallas_call(
        body, name=name, grid=(nk, nt),
        in_specs=[pl.BlockSpec((tm, tk), lambda j, i: (i, j)), pl.BlockSpec((tm, n), lambda j, i: (i, 0))],
        out_specs=pl.BlockSpec((tk, n), lambda j, i: (j, 0)), out_shape=SDS((k, n), BF16),
        scratch_shapes=[pltpu.VMEM((tk, n), F32)], compiler_params=_ARB2)(x, dy)


def _mix_bwd(dh1, wo, wg, wc, proj, brg, brc):
    t = dh1.shape[0]
    nt = t // TM

    def body(dh_ref, wo_ref, wg_ref, wc_ref, g_ref, brg_ref, brc_ref, dbg_ref, dbc_ref, dog_ref, dcs_ref, dp_ref, dm):
        dm[...] = _dot_nt(dh_ref[...].astype(BF16), wo_ref[...])

        def blk(r0):
            rows = pl.ds(r0, 32)
            d = dm[rows, :]
            sg = _sigmoid(g_ref[rows, 0:D].astype(F32))
            sc = _sigmoid(g_ref[rows, D:2 * D].astype(F32))
            dbg_ref[rows, :] = (d * sg).astype(BF16)
            dbc_ref[rows, :] = (d * sc).astype(BF16)
            dp_ref[rows, 0:D] = (d * brg_ref[rows, :].astype(F32) * sg * (1.0 - sg)).astype(BF16)
            dp_ref[rows, D:2 * D] = (d * brc_ref[rows, :].astype(F32) * sc * (1.0 - sc)).astype(BF16)
        _row_loop(TM, 32, blk)
        dog_ref[...] = _dot_nt(dbg_ref[...], wg_ref[...]).astype(BF16)
        dcs_ref[...] = _dot_nt(dbc_ref[...], wc_ref[...]).astype(BF16)

    row = pl.BlockSpec((TM, D), lambda i: (i, 0))
    wide = pl.BlockSpec((TM, 2 * D), lambda i: (i, 2))
    return pl.pallas_call(
        body, name="mix_bwd", grid=(nt,),
        in_specs=[row, _once((D, D)), _once((D, D)), _once((D, D)), wide, row, row],
        out_specs=[row, row, row, row, wide],
        out_shape=[SDS((t, D), BF16)] * 4 + [SDS((t, NPROJ), BF16)],
        scratch_shapes=[pltpu.VMEM((TM, D), F32)], compiler_params=_ARB1)(dh1, wo, wg, wc, proj, brg, brc)


def _glapost_bwd(dog, o, proj, gn, dproj):
    t = o.shape[0]
    nt = t // TE

    def body(dog_ref, o_ref, r_ref, gn_ref, dp_in, do_ref, dp_ref, dgn_ref):
        del dp_in

        @pl.when(pl.program_id(0) == 0)
        def _():
            dgn_ref[...] = jnp.zeros_like(dgn_ref)

        def blk(r0):
            rows = pl.ds(r0, 32)
            for h in range(HEADS):
                vs = slice(h * DVH, (h + 1) * DVH)
                x = o_ref[rows, vs]
                rinv = lax.rsqrt(jnp.mean(x * x, axis=-1, keepdims=True) + RMS_EPS)
                oh = x * rinv
                g = gn_ref[:, vs]
                rr = r_ref[rows, vs].astype(F32)
                sr = _sigmoid(rr)
                d = dog_ref[rows, vs].astype(F32)
                dp_ref[rows, vs] = (d * oh * g * sr * (1.0 + rr * (1.0 - sr))).astype(BF16)
                don = d * rr * sr
                dgn_ref[:, vs] += _rows8(don * oh)
                doh = don * g
                do_ref[rows, vs] = (rinv * (doh - oh * jnp.mean(doh * oh, axis=-1, keepdims=True))).astype(BF16)
        _row_loop(TE, 32, blk, unroll=5)

    row = pl.BlockSpec((TE, D), lambda i: (i, 0))
    rcol = pl.BlockSpec((TE, D), lambda i: (i, 6))
    return pl.pallas_call(
        body, name="glapost_bwd", grid=(nt,),
        in_specs=[row, row, rcol, _resident((1, D)), pl.BlockSpec(memory_space=pl.ANY)],
        out_specs=[row, rcol, _resident((8, D))],
        out_shape=[SDS((t, D), BF16), SDS((t, NPROJ), BF16), SDS((8, D), F32)],
        input_output_aliases={4: 1}, compiler_params=_ARB1)(dog, o, proj, gn, dproj)


def _conf_bwd(dcs, cc, proj, cw, lg, lb, dproj, carry=()):
    t = cc.shape[0]
    nt = t // TE

    def body(dcs_ref, cc_ref, c1_ref, c2_ref, cw_ref, lg_ref, lb_ref, dp_in, dp_ref, dw_ref, ds_ref, cval, gate, dext):
        del dp_in
        i = pl.program_id(0)

        @pl.when(i == 0)
        def _():
            dext[TE:TE + HALO, :] = jnp.zeros((HALO, D), F32)
            dw_ref[...] = jnp.zeros_like(dw_ref)
            ds_ref[...] = jnp.zeros_like(ds_ref)

        @pl.when(i > 0)
        def _():
            dext[TE:TE + HALO, :] = dext[0:HALO, :]

        def pre(r0):
            rows = pl.ds(r0, 32)
            s2 = _sigmoid(c2_ref[rows, :].astype(F32))
            gate[rows, :] = s2
            cval[rows, :] = c1_ref[rows, :].astype(F32) * s2
            x = cc_ref[rows, :]
            mu = jnp.mean(x, axis=-1, keepdims=True)
            xc = x - mu
            rstd = lax.rsqrt(jnp.mean(xc * xc, axis=-1, keepdims=True) + LN_EPS)
            xh = xc * rstd
            ln = xh * lg_ref[...] + lb_ref[...]
            sg = _sigmoid(ln)
            dln = dcs_ref[rows, :].astype(F32) * sg * (1.0 + ln * (1.0 - sg))
            ds_ref[0] += _rows8(dln * xh)
            ds_ref[1] += _rows8(dln)
            dxh = dln * lg_ref[...]
            dcc = rstd * (dxh - jnp.mean(dxh, axis=-1, keepdims=True) - xh * jnp.mean(dxh * xh, axis=-1, keepdims=True))
            dext[rows, :] = dcc
            ds_ref[2] += _rows8(dcc)
        _row_loop(TE, 32, pre, unroll=10)

        def convt(r0):
            rows = pl.ds(r0, CONV_ROWS)
            for part in range(D // CONV_LANES):
                cols = slice(part * CONV_LANES, (part + 1) * CONV_LANES)
                wd = dext[pl.ds(r0, CONV_ROWS + HALO), cols]
                cv = cval[rows, cols]
                dc = jnp.zeros((CONV_ROWS, CONV_LANES), F32)
                for s, taps in _shift_classes([(CONF_K - 1 - j, j) for j in range(CONF_K)]):
                    ws = _shifted(wd, s)
                    for a8, j in taps:
                        ahead = ws[a8:a8 + CONV_ROWS, :]
                        dc = dc + jnp.tile(cw_ref[j, :, cols], (CONV_ROWS // 8, 1)) * ahead
                        dw_ref[j, :, cols] += _rows8(cv * ahead)
                c1 = c1_ref[rows, cols].astype(F32)
                s2 = gate[rows, cols]
                dp_ref[rows, cols] = (dc * s2).astype(BF16)
                dp_ref[rows, D + part * CONV_LANES:D + (part + 1) * CONV_LANES] = (dc * c1 * s2 * (1.0 - s2)).astype(BF16)
        _row_loop(TE, CONV_ROWS, convt)

    rev = lambda col: pl.BlockSpec((TE, D), lambda i: (nt - 1 - i, col))
    return _carried_call(
        body, carry, name="conf_bwd", grid=(nt,),
        in_specs=[rev(0), rev(0), rev(2), rev(3), _resident((32, 8, D)), _resident((1, D)), _resident((1, D)),
                  pl.BlockSpec(memory_space=pl.ANY)],
        out_specs=[pl.BlockSpec((TE, 2 * D), lambda i: (nt - 1 - i, 1)), _resident((32, 8, D)), _resident((3, 8, D))],
        out_shape=[SDS((t, NPROJ), BF16), SDS((32, 8, D), F32), SDS((3, 8, D), F32)],
        scratch_shapes=[pltpu.VMEM((TE, D), F32), pltpu.VMEM((TE, D), F32), pltpu.VMEM((TE + HALO, D), F32)],
        input_output_aliases={7: 0}, compiler_params=_ARB1)(dcs, cc, proj, proj, cw, lg, lb, dproj)


def _gla_bwd(proj, alr, wau, balpha, do, sall, dproj, carry=()):
    t = proj.shape[0]
    nc = t // CH

    def body(qk_ref, v_ref, a_ref, wau_ref, ba_ref, do_ref, s_ref, dp_in, dp_ref, da_ref, dwau_ref, dba_ref, ds_scr, dla_scr, b_scr):
        del dp_in

        @pl.when(pl.program_id(0) == 0)
        def _():
            ds_scr[...] = jnp.zeros_like(ds_scr)
            dwau_ref[...] = jnp.zeros_like(dwau_ref)
            dba_ref[...] = jnp.zeros_like(dba_ref)

        z, b, causal = _gla_decay(a_ref, wau_ref, ba_ref)
        b_scr[...] = b
        dlasts = []
        for h in range(HEADS):
            ks = slice(h * DKH, (h + 1) * DKH)
            vs = slice(h * DVH, (h + 1) * DVH)
            bh, mh, lh = b_scr[:, ks], b_scr[CH // 2:CH // 2 + 1, ks], b_scr[CH - 1:CH, ks]
            q = qk_ref[:, ks].astype(F32) * (DKH ** -0.5)
            k = qk_ref[:, DK + h * DKH:DK + (h + 1) * DKH].astype(F32)
            v = v_ref[:, vs]
            dout = do_ref[:, vs]
            eq, ek, eb, eg, el = jnp.exp(bh - mh), jnp.exp(mh - bh), jnp.exp(bh), jnp.exp(lh - bh), jnp.exp(lh)
            qt, kt = (q * eq).astype(BF16), (k * ek).astype(BF16)
            qg, kg = (q * eb).astype(BF16), (k * eg).astype(BF16)
            st = s_ref[0, vs, :]
            dsn = ds_scr[vs, :]
            st16, dsn16 = st.astype(BF16), dsn.astype(BF16)
            a = jnp.where(causal, _dot_nt(qt, kt), 0.0).astype(BF16)
            da = jnp.where(causal, _dot_nt(dout, v), 0.0).astype(BF16)
            dq_inter = _dot(dout, st16) * eb
            dk_inter = _dot(v, dsn16) * eg
            dq = _dot(da, kt) * eq + dq_inter
            dk = _dot_tn(da, qt) * ek + dk_inter
            dv = _dot_tn(a, dout) + _dot_nt(kg, dsn16)
            dlasts.append(jnp.sum(k * dk_inter, axis=0, keepdims=True) + jnp.sum(st * dsn, axis=0, keepdims=True) * el)
            dla_scr[:, ks] = q * dq - k * dk
            ds_scr[vs, :] = dsn * el + _dot_tn(dout, qg)
            dp_ref[:, ks] = (dq * (DKH ** -0.5)).astype(BF16)
            dp_ref[:, DK + h * DKH:DK + (h + 1) * DKH] = dk.astype(BF16)
            dp_ref[:, D + h * DVH:D + (h + 1) * DVH] = dv.astype(BF16)
        r = lax.broadcasted_iota(jnp.int32, (CH, CH), 0)
        c = lax.broadcasted_iota(jnp.int32, (CH, CH), 1)
        dla = _tri_matmul((r <= c).astype(BF16), dla_scr[...]) + jnp.concatenate(dlasts, axis=1)
        dz = (dla * (1.0 / TAU) * _sigmoid(-z)).astype(BF16)
        da_ref[...] = _dot_nt(dz, wau_ref[...]).astype(BF16)
        dwau_ref[...] += _dot_tn(a_ref[...], dz)
        dba_ref[...] += _rows8(dz.astype(F32))

    rev = lambda w, col: pl.BlockSpec((CH, w), lambda c: (nc - 1 - c, col))
    return _carried_call(
        body, carry, name="gla_bwd", grid=(nc,),
        in_specs=[rev(D, 0), rev(D, 1), rev(LANES, 0), _resident((LANES, DK)), _resident((1, DK)), rev(D, 0),
                  pl.BlockSpec((1, DV, DKH), lambda c: (nc - 1 - c, 0, 0)), pl.BlockSpec(memory_space=pl.ANY)],
        out_specs=[rev(2 * D, 0), rev(LANES, 0), _resident((LANES, DK)), _resident((8, DK))],
        out_shape=[SDS((t, NPROJ), BF16), SDS((t, LANES), BF16), SDS((LANES, DK), F32), SDS((8, DK), F32)],
        scratch_shapes=[pltpu.VMEM((DV, DKH), F32), pltpu.VMEM((CH, DK), F32), pltpu.VMEM((CH, DK), F32)],
        input_output_aliases={7: 0}, compiler_params=_ARB1)(proj, proj, alr, wau, balpha, do, sall, dproj)


def _all_gather(xs, name):
    n = len(xs)

    def body(*refs):
        x_refs, out_refs = refs[:n], refs[n:2 * n]
        send_sems, recv_sems = refs[2 * n:]
        x, y, c = _my_place()
        me, sibling = (x, y, c), (x, y, 1 - c)
        x_nbr, y_nbr, diagonal = (1 - x, y), (x, 1 - y), (1 - x, 1 - y)

        def slot(p, px, py, pc):
            return out_refs[p].at[4 * px + 2 * py + pc]

        def copy(p, k, block, to, src=None):
            return pltpu.make_async_remote_copy(
                src_ref=slot(p, *block) if src is None else src, dst_ref=slot(p, *block),
                send_sem=send_sems.at[7 * p + k], recv_sem=recv_sems.at[7 * p + k], device_id=to, device_id_type=MESH_T)

        for p in range(n):
            for k, to in enumerate((sibling, (*x_nbr, c), (*y_nbr, c))):
                copy(p, k, me, to, src=x_refs[p]).start()
        for p in range(n):
            @pl.when(c == 1)
            def _():
                copy(p, 2, (*y_nbr, c), me).wait_recv()
                copy(p, 3, (*y_nbr, c), (*x_nbr, c)).start()
                copy(p, 1, (*x_nbr, c), me).wait_recv()

            @pl.when(c == 0)
            def _():
                copy(p, 1, (*x_nbr, c), me).wait_recv()
                copy(p, 3, (*x_nbr, c), (*y_nbr, c)).start()
                copy(p, 2, (*y_nbr, c), me).wait_recv()
        for p in range(n):
            copy(p, 4, (*x_nbr, c), sibling).start()
            copy(p, 5, (*y_nbr, c), sibling).start()
        for p in range(n):
            copy(p, 3, (*diagonal, c), me).wait_recv()
            copy(p, 6, (*diagonal, c), sibling).start()
        for p in range(n):
            copy(p, 0, sibling, me).wait_recv()
            for j, chip in enumerate((x_nbr, y_nbr, diagonal)):
                copy(p, 4 + j, (*chip, 1 - c), me).wait_recv()
        for p in range(n):
            for k in range(7):
                copy(p, k, me, sibling, src=x_refs[p]).wait_send()

    hbm = pl.BlockSpec(memory_space=pl.ANY)
    return pl.pallas_call(
        body, name=name, out_shape=[SDS((N_DEV, *a.shape), a.dtype) for a in xs],
        in_specs=[hbm] * n, out_specs=[hbm] * n,
        scratch_shapes=[pltpu.SemaphoreType.DMA((7 * n,)), pltpu.SemaphoreType.DMA((7 * n,))])(*xs)


def _exchange(gs):
    n = len(gs)

    def body(*refs):
        g_refs, land_refs = refs[:n], refs[n:2 * n]
        send_sems, recv_sems, local_sems = refs[2 * n:]
        x, y, c = _my_place()
        my_idx = 4 * x + 2 * y + c
        mine = [pltpu.make_async_copy(g_refs[p].at[my_idx], land_refs[p].at[my_idx], local_sems.at[p]) for p in range(n)]
        for cp in mine:
            cp.start()
        copies = []
        for k in range(1, N_DEV):
            px, py, pc = _flip(x, k & 4), _flip(y, k & 2), _flip(c, k & 1)
            p_idx = 4 * px + 2 * py + pc
            for p in range(n):
                s = 7 * p + k - 1
                cp = pltpu.make_async_remote_copy(
                    src_ref=g_refs[p].at[p_idx], dst_ref=land_refs[p].at[my_idx], send_sem=send_sems.at[s],
                    recv_sem=recv_sems.at[s], device_id=(px, py, pc), device_id_type=MESH_T)
                cp.start()
                arrival = pltpu.make_async_remote_copy(
                    src_ref=g_refs[p].at[p_idx], dst_ref=land_refs[p].at[p_idx], send_sem=send_sems.at[s],
                    recv_sem=recv_sems.at[s], device_id=(px, py, pc), device_id_type=MESH_T)
                copies.append((cp, arrival))
        for cp, arrival in copies:
            arrival.wait_recv()
        for cp, arrival in copies:
            cp.wait_send()
        for cp in mine:
            cp.wait()

    hbm = pl.BlockSpec(memory_space=pl.ANY)
    return pl.pallas_call(
        body, name="grad_exchange", out_shape=[SDS(g.shape, g.dtype) for g in gs],
        in_specs=[hbm] * n, out_specs=[hbm] * n,
        scratch_shapes=[pltpu.SemaphoreType.DMA((7 * n,)), pltpu.SemaphoreType.DMA((7 * n,)),
                        pltpu.SemaphoreType.DMA((n,))])(*gs)


def _adamw(land, w, m, v, rows_blk, name):
    rows = w.shape[0]

    def body(l_ref, w_ref, m_ref, v_ref, g_ref, d_ref, nm_ref, nv_ref):
        g = l_ref[0].astype(F32)
        for s in range(1, N_DEV):
            g = g + l_ref[s].astype(F32)
        nm = ADAM_B1 * m_ref[...] + (1.0 - ADAM_B1) * g
        nv = ADAM_B2 * v_ref[...] + (1.0 - ADAM_B2) * (g * g)
        m_hat = nm / (1.0 - ADAM_B1 ** ADAM_STEP)
        v_hat = nv / (1.0 - ADAM_B2 ** ADAM_STEP)
        g_ref[...] = g
        d_ref[...] = -ADAM_LR * (m_hat / (jnp.sqrt(v_hat) + ADAM_EPS) + ADAM_WD * w_ref[...])
        nm_ref[...] = nm
        nv_ref[...] = nv

    blk = pl.BlockSpec((rows_blk, D), lambda i: (i, 0))
    return pl.pallas_call(
        body, name=name, grid=(rows // rows_blk,),
        in_specs=[pl.BlockSpec((N_DEV, rows_blk, D), lambda i: (0, i, 0)), blk, blk, blk],
        out_specs=[blk] * 4, out_shape=[SDS((rows, D), F32)] * 4, compiler_params=_ARB1)(land, w, m, v)


BIG = ("w_in", "w_up", "w_down", "w_gla_o", "w_conf_o", "w_out")
BIG_TRANSPOSED = ("w_in", "w_up")
SMALL_SHARDED = ("meta_tokens", "conf_dw_w", "ffn_dw_w", "w_alpha_up")
REPLICATED = ("norm_mix_g", "b_alpha", "gla_norm_g", "conf_dw_b", "conf_ln_g", "conf_ln_b", "norm_ffn_g", "ffn_dw_b",
              "final_norm_g")
N_IN = sum(IN_WIDTHS)
W_IN_ROWS = N_IN // N_DEV
W_IN_PAD = -(-W_IN_ROWS // 16) * 16
ADAM_BLOCK = {"w_in": W_IN_PAD // 3, "w_up": 176, "w_down": 176, "w_gla_o": 128, "w_conf_o": 128, "w_out": 128}
SMALL_ROWS = 32


def _to_panel(name, shard):
    a = shard.reshape(shard.shape[-2], shard.shape[-1])
    if name in BIG_TRANSPOSED:
        a = a.T
    if name == "w_in":
        a = jnp.pad(a, ((0, W_IN_PAD - W_IN_ROWS), (0, 0)))
    return a


def _from_panel(name, panel, shape):
    a = panel[0:W_IN_ROWS] if name == "w_in" else panel
    if name in BIG_TRANSPOSED:
        a = a.T
    return a.reshape(shape)


def _pack_small(arrs):
    flat = jnp.concatenate([jnp.pad(a.reshape(-1), (0, (-a.size) % D)) for a in arrs])
    return jnp.pad(flat, (0, SMALL_ROWS * D - flat.shape[0])).reshape(SMALL_ROWS, D)


def _unpack_small(panel, shapes):
    flat, out, off = panel.reshape(-1), [], 0
    for shp in shapes:
        n = 1
        for s in shp:
            n *= s
        out.append(flat[off:off + n].reshape(shp))
        off += n + (-n) % D
    return out


def _local_step(x, target, w, shards=None):
    dist = shards is not None
    w = dict(w)

    def gather(names):
        return [(shards[n], False) for n in names] if dist else []

    def scatter(*arrs):
        return [(a.reshape(N_DEV, -1, D), True) for a in arrs] if dist else []

    s = x.shape[0]
    n_real = s + N_META
    t = -(-n_real // TM) * TM

    q0, r0, a0, c0 = 0, 2 * DK + DV, 2 * DK + 2 * DV, 2 * DK + 2 * DV + RANK
    wt = w["w_in_t"]
    w_main = jnp.concatenate([wt[q0:r0], wt[c0:N_IN], wt[r0:a0]], axis=0)
    w_a = jnp.pad(wt[a0:c0], ((0, LANES - RANK), (0, 0)))
    wau = jnp.pad(w["w_alpha_up"].astype(BF16), ((0, LANES - RANK), (0, 0)))
    row = lambda name: w[name].reshape(1, -1)
    cw = jnp.broadcast_to(jnp.pad(w["conf_dw_w"], ((0, 32 - CONF_K), (0, 0)))[:, None, :], (32, 8, D))
    fw = jnp.broadcast_to(jnp.concatenate([w["ffn_dw_w"], w["ffn_dw_b"].reshape(1, -1)], axis=0)[:, None, :],
                          (FFN_K + 1, 16, DFF))

    early = ("w_gla_o", "w_conf_o", "w_out")
    h0, u1, proj, alr, *landed = _in_proj(x, w["meta_tokens"], row("norm_mix_g"), w_main, w_a, t, carry=gather(early))
    for n, land in zip(early, landed):
        w[n] = land.reshape(-1, D)
    o, og, sall, *landed = _gla_fwd(proj, alr, wau, row("b_alpha"), row("gla_norm_g"), carry=gather(("w_down",)))
    if dist:
        w["w_down"] = landed[0].reshape(-1, D)
    cc, cs, *landed = _conf_fwd(proj, cw, row("conf_dw_b"), row("conf_ln_g"), row("conf_ln_b"), carry=gather(("w_up",)))
    if dist:
        w["w_up_t"] = landed[0].reshape(-1, D)
    brg, brc, merged, h1 = _mix_fwd(og, cs, proj, h0, w["w_gla_o"], w["w_conf_o"], w["w_out"])
    u2, up = _norm_matmul(h1, row("norm_ffn_g"), w["w_up_t"], 512, "up_proj")
    f, dh2, red = _ffn_out(up, fw, h1, w["w_down"], row("final_norm_g"), target)
    loss = 0.5 / D * jnp.sum(red[0:8])

    g = {"final_norm_g": jnp.sum(red[8:16], axis=0)}
    dup, dfw = _ffn_bwd(dh2, w["w_down"], up, fw)
    g["ffn_dw_w"] = jnp.sum(dfw[0:FFN_K], axis=1)
    g["ffn_dw_b"] = jnp.sum(dfw[3], axis=0)
    g["w_down"] = _wgrad(f, dh2, 1408, "wgrad_down")
    dh1, dg2, *landed = _dgrad_norm(dup, w["w_up_t"], h1, row("norm_ffn_g"), dh2, "up_dgrad", carry=scatter(g["w_down"]))
    if dist:
        g["w_down"] = landed[0]
    g["norm_ffn_g"] = jnp.sum(dg2, axis=0)
    g["w_up_t"] = _wgrad(dup, u2, 1408, "wgrad_up")
    dbrg, dbrc, dog, dcs, dproj = _mix_bwd(dh1, w["w_out"], w["w_gla_o"], w["w_conf_o"], proj, brg, brc)
    g["w_out"] = _wgrad(merged, dh1, 1024, "wgrad_out")
    g["w_gla_o"] = _wgrad(og, dbrg, 1024, "wgrad_gla_o")
    g["w_conf_o"] = _wgrad(cs, dbrc, 1024, "wgrad_conf_o")
    do, dproj, dgn = _glapost_bwd(dog, o, proj, row("gla_norm_g"), dproj)
    g["gla_norm_g"] = jnp.sum(dgn, axis=0)
    dproj, dcw, dst, *landed = _conf_bwd(dcs, cc, proj, cw, row("conf_ln_g"), row("conf_ln_b"), dproj,
                                         carry=scatter(g["w_up_t"]))
    if dist:
        g["w_up_t"] = landed[0]
    g["conf_dw_w"] = jnp.sum(dcw[0:CONF_K], axis=1)
    g["conf_ln_g"], g["conf_ln_b"], g["conf_dw_b"] = jnp.sum(dst[0], axis=0), jnp.sum(dst[1], axis=0), jnp.sum(dst[2], axis=0)
    dproj, dalr, dwau, dba, *landed = _gla_bwd(proj, alr, wau, row("b_alpha"), do, sall, dproj,
                                               carry=scatter(g["w_out"], g["w_gla_o"], g["w_conf_o"]))
    if dist:
        g["w_out"], g["w_gla_o"], g["w_conf_o"] = landed
    g["w_alpha_up"] = dwau[0:RANK]
    g["b_alpha"] = jnp.sum(dba, axis=0)
    dw_main = _wgrad(dproj, u1, 1024, "wgrad_in")
    dw_a = _wgrad(dalr, u1, LANES, "wgrad_alr")
    g["w_in_t"] = jnp.concatenate([dw_main[0:r0], dw_main[NPROJ - DV:NPROJ], dw_a[0:RANK], dw_main[r0:NPROJ - DV]], axis=0)
    w_in_blocks = []
    if dist:
        pad = ((0, 0), (0, W_IN_PAD - W_IN_ROWS), (0, 0))
        w_in_blocks = [(jnp.pad(g["w_in_t"].reshape(N_DEV, W_IN_ROWS, D), pad), True)]
    dh0, dg1, *landed = _in_dgrad(dproj, w_main, h0, row("norm_mix_g"), dh1, dalr, w_a, carry=w_in_blocks)
    if dist:
        g["w_in_t"] = landed[0]
    g["norm_mix_g"] = jnp.sum(dg1, axis=0)
    g["meta_tokens"] = dh0[0:N_META]
    return loss, dh0[N_META:n_real], g


def kernel(x, meta_tokens, norm_mix_g, w_in, w_alpha_up, b_alpha, gla_norm_g, w_gla_o, conf_dw_w, conf_dw_b, conf_ln_g, conf_ln_b, w_conf_o, w_out, norm_ffn_g, w_up, ffn_dw_w, ffn_dw_b, w_down, final_norm_g, loss_target, m_meta_tokens, m_norm_mix_g, m_w_in, m_w_alpha_up, m_b_alpha, m_gla_norm_g, m_w_gla_o, m_conf_dw_w, m_conf_dw_b, m_conf_ln_g, m_conf_ln_b, m_w_conf_o, m_w_out, m_norm_ffn_g, m_w_up, m_ffn_dw_w, m_ffn_dw_b, m_w_down, m_final_norm_g, v_meta_tokens, v_norm_mix_g, v_w_in, v_w_alpha_up, v_b_alpha, v_gla_norm_g, v_w_gla_o, v_conf_dw_w, v_conf_dw_b, v_conf_ln_g, v_conf_ln_b, v_w_conf_o, v_w_out, v_norm_ffn_g, v_w_up, v_ffn_dw_w, v_ffn_dw_b, v_w_down, v_final_norm_g):
    ws = dict(meta_tokens=meta_tokens, norm_mix_g=norm_mix_g, w_in=w_in, w_alpha_up=w_alpha_up, b_alpha=b_alpha,
              gla_norm_g=gla_norm_g, w_gla_o=w_gla_o, conf_dw_w=conf_dw_w, conf_dw_b=conf_dw_b, conf_ln_g=conf_ln_g,
              conf_ln_b=conf_ln_b, w_conf_o=w_conf_o, w_out=w_out, norm_ffn_g=norm_ffn_g, w_up=w_up, ffn_dw_w=ffn_dw_w,
              ffn_dw_b=ffn_dw_b, w_down=w_down, final_norm_g=final_norm_g)
    ms = dict(meta_tokens=m_meta_tokens, norm_mix_g=m_norm_mix_g, w_in=m_w_in, w_alpha_up=m_w_alpha_up, b_alpha=m_b_alpha,
              gla_norm_g=m_gla_norm_g, w_gla_o=m_w_gla_o, conf_dw_w=m_conf_dw_w, conf_dw_b=m_conf_dw_b,
              conf_ln_g=m_conf_ln_g, conf_ln_b=m_conf_ln_b, w_conf_o=m_w_conf_o, w_out=m_w_out, norm_ffn_g=m_norm_ffn_g,
              w_up=m_w_up, ffn_dw_w=m_ffn_dw_w, ffn_dw_b=m_ffn_dw_b, w_down=m_w_down, final_norm_g=m_final_norm_g)
    vs = dict(meta_tokens=v_meta_tokens, norm_mix_g=v_norm_mix_g, w_in=v_w_in, w_alpha_up=v_w_alpha_up, b_alpha=v_b_alpha,
              gla_norm_g=v_gla_norm_g, w_gla_o=v_w_gla_o, conf_dw_w=v_conf_dw_w, conf_dw_b=v_conf_dw_b,
              conf_ln_g=v_conf_ln_g, conf_ln_b=v_conf_ln_b, w_conf_o=v_w_conf_o, w_out=v_w_out, norm_ffn_g=v_norm_ffn_g,
              w_up=v_w_up, ffn_dw_w=v_ffn_dw_w, ffn_dw_b=v_ffn_dw_b, w_down=v_w_down, final_norm_g=v_final_norm_g)
    small = SMALL_SHARDED + REPLICATED
    pack_small = lambda d: _pack_small([d[n] for n in small])

    shards = {n: _to_panel(n, ws[n]).astype(BF16) for n in BIG}
    own = [shards["w_in"], pack_small(ws)]
    my_idx = 4 * lax.axis_index("x") + 2 * lax.axis_index("y") + lax.axis_index("c")
    gathered = [lax.dynamic_update_slice(full_, mine[None], (my_idx, 0, 0))
                for full_, mine in zip(_all_gather(own, "weight_gather"), own)]
    full = {n: ws[n].reshape(-1) for n in REPLICATED}
    full["w_in_t"] = gathered[0][:, 0:W_IN_ROWS].reshape(N_IN, D)
    flat, off = gathered[1].reshape(N_DEV, -1), 0
    for n in SMALL_SHARDED:
        k, c = ws[n].shape[-2], ws[n].shape[-1]
        full[n] = flat[:, off:off + k * c].reshape(N_DEV, k, c).transpose(1, 0, 2).reshape(k, N_DEV * c)
        off += k * c + (-(k * c)) % D

    loss, grad_x, g = _local_step(x[0], loss_target[0], full, shards)

    lands = [g["w_in_t"], g["w_up_t"]] + [g[n] for n in BIG[2:]]
    blocks = []
    for n in SMALL_SHARDED:
        k, c = ws[n].shape[-2], ws[n].shape[-1]
        b = g[n].reshape(k, N_DEV, c).transpose(1, 0, 2).reshape(N_DEV, k * c)
        blocks.append(jnp.pad(b, ((0, 0), (0, (-(k * c)) % D))))
    for n in REPLICATED:
        b = jnp.broadcast_to(g[n].reshape(1, -1), (N_DEV, g[n].size))
        blocks.append(jnp.pad(b, ((0, 0), (0, (-b.shape[1]) % D))))
    gsm = jnp.concatenate(blocks, axis=1)
    lands += _exchange([jnp.pad(gsm, ((0, 0), (0, SMALL_ROWS * D - gsm.shape[1]))).reshape(N_DEV, SMALL_ROWS, D)])

    grad, delta, new_m, new_v = {}, {}, {}, {}
    for i, n in enumerate(BIG):
        outs = _adamw(lands[i], _to_panel(n, ws[n]), _to_panel(n, ms[n]), _to_panel(n, vs[n]), ADAM_BLOCK[n], "adamw_" + n)
        grad[n], delta[n], new_m[n], new_v[n] = [_from_panel(n, p, ws[n].shape) for p in outs]
    outs = _adamw(lands[len(BIG)], pack_small(ws), pack_small(ms), pack_small(vs), SMALL_ROWS, "adamw_small")
    shapes = [ws[n].shape for n in small]
    for d, p in zip((grad, delta, new_m, new_v), outs):
        d.update(zip(small, _unpack_small(p, shapes)))

    order = ("meta_tokens", "norm_mix_g", "w_in", "w_alpha_up", "b_alpha", "gla_norm_g", "w_gla_o", "conf_dw_w", "conf_dw_b",
             "conf_ln_g", "conf_ln_b", "w_conf_o", "w_out", "norm_ffn_g", "w_up", "ffn_dw_w", "ffn_dw_b", "w_down",
             "final_norm_g")
    loss = lax.psum(loss, ("x", "y", "c"))
    return (loss, grad_x[None], *[grad[n] for n in order], *[delta[n] for n in order], *[new_m[n] for n in order],
            *[new_v[n] for n in order])
```

```python
import functools

import jax
import jax.numpy as jnp
from jax import lax
from jax.experimental import pallas as pl
from jax.experimental.pallas import tpu as pltpu

F32, BF16 = jnp.float32, jnp.bfloat16
SDS = jax.ShapeDtypeStruct

D = 1024
N_META = 16
HEADS = 4
DK, DKH, DV, DVH = 512, 128, 1024, 256
RANK = 16
TAU = 16.0
CONF_K = 31
DFF = 2816
FFN_K = 3
IN_WIDTHS = (DK, DK, DV, DV, RANK, 2 * D, D, D)
RMS_EPS, LN_EPS = 1e-6, 1e-5
ADAM_LR, ADAM_B1, ADAM_B2, ADAM_EPS, ADAM_WD, ADAM_STEP = 0.001, 0.9, 0.999, 1e-08, 0.01, 10

NPROJ = 7 * D
LANES = 128
CH = 128
TM = 640
TM_BIG = 1664
TE = 320
HALO = 32
HALO_F = 16
N_DEV = 8
VMEM_LIMIT = 60 * 1024 * 1024
MESH_T = pl.DeviceIdType.MESH

_ARB1 = pltpu.CompilerParams(dimension_semantics=("arbitrary",), vmem_limit_bytes=VMEM_LIMIT)
_ARB2 = pltpu.CompilerParams(dimension_semantics=("arbitrary", "arbitrary"), vmem_limit_bytes=VMEM_LIMIT)


def _dot(a, b):
    return jnp.dot(a, b, preferred_element_type=F32)


def _dot_nt(a, b):
    return lax.dot_general(a, b, (((1,), (1,)), ((), ())), preferred_element_type=F32)


def _dot_tn(a, b):
    return lax.dot_general(a, b, (((0,), (0,)), ((), ())), preferred_element_type=F32)


def _sigmoid(x):
    return 0.5 * jnp.tanh(0.5 * x) + 0.5


def _rows8(x):
    return x.reshape(x.shape[0] // 8, 8, x.shape[1]).sum(axis=0)


def _row_tile(t, preferred):
    return preferred if t % preferred == 0 else TM


def _row_loop(n_rows, rb, fn, unroll=1):
    def step(i, carry):
        fn(pl.multiple_of(i * rb, rb))
        return carry
    lax.fori_loop(0, n_rows // rb, step, 0, unroll=unroll)


def _resident(shape):
    return pl.BlockSpec(shape, lambda *_: (0,) * len(shape))


def _once(shape):
    return pl.BlockSpec(shape, lambda *_: (0,) * len(shape), pipeline_mode=pl.Buffered(1))


CONV_ROWS, CONV_LANES = 64, 256


def _shift_classes(offset_taps):
    return [(s, [(o - s, j) for o, j in offset_taps if o % 8 == s]) for s in range(8)]


def _shifted(win, s):
    return win if s == 0 else win[s:s + CONV_ROWS + HALO - 8, :]


def _my_place():
    return lax.axis_index("x"), lax.axis_index("y"), lax.axis_index("c")


def _flip(v, bit):
    return 1 - v if bit else v


def _exchange_copies(src_refs, land_refs, scatter, send_sems, recv_sems, local_sems, arrivals):
    x, y, c = _my_place()
    my_idx = 4 * x + 2 * y + c
    local, remote = [], []
    for p, (src, land) in enumerate(zip(src_refs, land_refs)):
        local.append(pltpu.make_async_copy(src.at[my_idx] if scatter[p] else src, land.at[my_idx], local_sems.at[p]))
    for k in range(1, N_DEV):
        px, py, pc = _flip(x, k & 4), _flip(y, k & 2), _flip(c, k & 1)
        p_idx = 4 * px + 2 * py + pc
        for p, (src, land) in enumerate(zip(src_refs, land_refs)):
            s = 7 * p + k - 1
            out = src.at[p_idx] if scatter[p] else src

            def copy(dst):
                return pltpu.make_async_remote_copy(src_ref=out, dst_ref=dst, send_sem=send_sems.at[s],
                                                    recv_sem=recv_sems.at[s], device_id=(px, py, pc), device_id_type=MESH_T)
            remote.append((copy(land.at[my_idx]), copy(land.at[p_idx]) if arrivals else None))
    return local, remote


def _carried_call(core, carry, *, grid, in_specs, out_specs, out_shape, scratch_shapes=(), **kw):
    n_in, n_out, nc, n_scr = len(in_specs), len(out_specs), len(carry), len(scratch_shapes)
    if nc == 0:
        return pl.pallas_call(core, grid=grid, in_specs=in_specs, out_specs=out_specs, out_shape=out_shape,
                              scratch_shapes=list(scratch_shapes), **kw)
    scatter = [sc for _, sc in carry]

    def body(*refs):
        ins, cin = refs[:n_in], refs[n_in:n_in + nc]
        outs, cout = refs[n_in + nc:n_in + nc + n_out], refs[n_in + nc + n_out:n_in + 2 * nc + n_out]
        scr, sems = refs[n_in + 2 * nc + n_out:n_in + 2 * nc + n_out + n_scr], refs[-3:]
        first = functools.reduce(jnp.logical_and, [pl.program_id(a) == 0 for a in range(len(grid))])
        last = functools.reduce(jnp.logical_and, [pl.program_id(a) == grid[a] - 1 for a in range(len(grid))])

        @pl.when(first)
        def _():
            local, remote = _exchange_copies(cin, cout, scatter, *sems, arrivals=False)
            for cp in local:
                cp.start()
            for send, _ in remote:
                send.start()

        core(*ins, *outs, *scr)

        @pl.when(last)
        def _():
            local, remote = _exchange_copies(cin, cout, scatter, *sems, arrivals=True)
            for _, arrival in remote:
                arrival.wait_recv()
            for send, _ in remote:
                send.wait_send()
            for cp in local:
                cp.wait()

    hbm = pl.BlockSpec(memory_space=pl.ANY)
    land_shape = [SDS((N_DEV, *(a.shape[1:] if sc else a.shape)), a.dtype) for a, sc in carry]
    sems = [pltpu.SemaphoreType.DMA((7 * nc,)), pltpu.SemaphoreType.DMA((7 * nc,)), pltpu.SemaphoreType.DMA((nc,))]
    call = pl.pallas_call(body, grid=grid, in_specs=list(in_specs) + [hbm] * nc, out_specs=list(out_specs) + [hbm] * nc,
                          out_shape=list(out_shape) + land_shape, scratch_shapes=list(scratch_shapes) + sems, **kw)
    return lambda *args: call(*args, *[a for a, _ in carry])


def _norm_matmul(h, g, w_t, tn, name, w_extra_t=None, carry=()):
    t, n = h.shape[0], w_t.shape[0]
    tm = _row_tile(t, TM_BIG)
    nt, nb = t // tm, n // tn

    def body(*refs):
        if w_extra_t is None:
            h_ref, g_ref, w_ref, u_ref, p_ref = refs
        else:
            h_ref, g_ref, w_ref, we_ref, u_ref, p_ref, e_ref = refs

        @pl.when(pl.program_id(1) == 0)
        def _():
            def blk(r0):
                x = h_ref[pl.ds(r0, 32), :]
                rinv = lax.rsqrt(jnp.mean(x * x, axis=-1, keepdims=True) + RMS_EPS)
                u_ref[pl.ds(r0, 32), :] = (x * rinv * g_ref[...]).astype(BF16)
            _row_loop(tm, 32, blk, unroll=13)
            if w_extra_t is not None:
                e_ref[...] = _dot_nt(u_ref[...], we_ref[...]).astype(BF16)

        p_ref[...] = _dot_nt(u_ref[...], w_ref[...]).astype(BF16)

    in_specs = [pl.BlockSpec((tm, D), lambda i, j: (i, 0)), _resident((1, D)), pl.BlockSpec((tn, D), lambda i, j: (j, 0))]
    out_specs = [pl.BlockSpec((tm, D), lambda i, j: (i, 0)), pl.BlockSpec((tm, tn), lambda i, j: (i, j))]
    out_shape = [SDS((t, D), BF16), SDS((t, n), BF16)]
    args = [h, g, w_t]
    if w_extra_t is not None:
        in_specs.append(_resident(w_extra_t.shape))
        out_specs.append(pl.BlockSpec((tm, w_extra_t.shape[0]), lambda i, j: (i, 0)))
        out_shape.append(SDS((t, w_extra_t.shape[0]), BF16))
        args.append(w_extra_t)
    return _carried_call(body, carry, name=name, grid=(nt, nb), in_specs=in_specs, out_specs=out_specs,
                         out_shape=out_shape, compiler_params=_ARB2)(*args)


def _in_proj(x, meta, g, w_t, w_a_t, t, carry=()):
    n_real = x.shape[0] + N_META
    n, tn = w_t.shape[0], 1024
    tm = _row_tile(t, TM_BIG)
    nt, nb = t // tm, n // tn

    def body(x_ref, xh_ref, m_ref, g_ref, w_ref, wa_ref, h_ref, u_ref, p_ref, e_ref):
        i = pl.program_id(0)

        @pl.when(pl.program_id(1) == 0)
        def _():
            def rows_of(r0, val):
                gid = i * tm + r0 + lax.broadcasted_iota(jnp.int32, (32, 1), 0)
                val = jnp.where(gid < n_real, val, 0.0)
                h_ref[pl.ds(r0, 32), :] = val
                rinv = lax.rsqrt(jnp.mean(val * val, axis=-1, keepdims=True) + RMS_EPS)
                u_ref[pl.ds(r0, 32), :] = (val * rinv * g_ref[...]).astype(BF16)

            before = jnp.where(i == 0, m_ref[...], xh_ref[...])
            rows_of(0, jnp.concatenate([before, x_ref[0:N_META, :]], axis=0))

            def blk(k, c):
                r0 = pl.multiple_of(k * 32, 32)
                rows_of(r0, x_ref[pl.ds(pl.multiple_of(r0 - N_META, N_META), 32), :])
                return c
            lax.fori_loop(1, tm // 32, blk, 0, unroll=17)
            e_ref[...] = _dot_nt(u_ref[...], wa_ref[...]).astype(BF16)

        p_ref[...] = _dot_nt(u_ref[...], w_ref[...]).astype(BF16)

    row = lambda w: pl.BlockSpec((tm, w), lambda i, j: (i, 0))
    return _carried_call(
        body, carry, name="in_proj", grid=(nt, nb),
        in_specs=[row(D), pl.BlockSpec((N_META, D), lambda i, j: (jnp.maximum(i * (tm // N_META) - 1, 0), 0)),
                  _resident((N_META, D)), _resident((1, D)), pl.BlockSpec((tn, D), lambda i, j: (j, 0)),
                  _resident(w_a_t.shape)],
        out_specs=[row(D), row(D), pl.BlockSpec((tm, tn), lambda i, j: (i, j)), row(w_a_t.shape[0])],
        out_shape=[SDS((t, D), F32), SDS((t, D), BF16), SDS((t, n), BF16), SDS((t, w_a_t.shape[0]), BF16)],
        compiler_params=_ARB2)(x, x, meta, g, w_t, w_a_t)


def _gla_decay(a_ref, wau_ref, ba_ref):
    z = _dot(a_ref[...], wau_ref[...]) + ba_ref[...]
    la = (jnp.minimum(z, 0.0) - jnp.log(1.0 + jnp.exp(-jnp.abs(z)))) * (1.0 / TAU)
    r = lax.broadcasted_iota(jnp.int32, (CH, CH), 0)
    c = lax.broadcasted_iota(jnp.int32, (CH, CH), 1)
    b = _tri_matmul((r >= c).astype(BF16), la)
    return z, b, r >= c


def _tri_matmul(tri, x):
    n = x.shape[1]
    x1 = x.astype(BF16)
    x2 = (x - x1.astype(F32)).astype(BF16)
    y = _dot(tri, jnp.concatenate([x1, x2], axis=1))
    return y[:, 0:n] + y[:, n:2 * n]


def _gla_fwd(proj, alr, wau, balpha, gn, carry=()):
    t = proj.shape[0]
    nc = t // CH

    def body(qk_ref, v_ref, r_ref, a_ref, wau_ref, ba_ref, gn_ref, o_ref, og_ref, sall_ref, s_scr, b_scr):
        @pl.when(pl.program_id(0) == 0)
        def _():
            s_scr[...] = jnp.zeros_like(s_scr)

        sall_ref[0] = s_scr[...]
        _, b, causal = _gla_decay(a_ref, wau_ref, ba_ref)
        b_scr[...] = b
        for h in range(HEADS):
            ks = slice(h * DKH, (h + 1) * DKH)
            vs = slice(h * DVH, (h + 1) * DVH)
            bh, mh, lh = b_scr[:, ks], b_scr[CH // 2:CH // 2 + 1, ks], b_scr[CH - 1:CH, ks]
            q = qk_ref[:, ks].astype(F32) * (DKH ** -0.5)
            k = qk_ref[:, DK + h * DKH:DK + (h + 1) * DKH].astype(F32)
            v = v_ref[:, vs]
            qt = (q * jnp.exp(bh - mh)).astype(BF16)
            kt = (k * jnp.exp(mh - bh)).astype(BF16)
            qg = (q * jnp.exp(bh)).astype(BF16)
            kg = (k * jnp.exp(lh - bh)).astype(BF16)
            a = jnp.where(causal, _dot_nt(qt, kt), 0.0)
            st = s_scr[vs, :]
            o = _dot(a.astype(BF16), v) + _dot_nt(qg, st.astype(BF16))
            el = jnp.exp(lh)
            s_scr[vs, :] = st * el + _dot_tn(v, kg)
            o_ref[:, vs] = o
            on = o * lax.rsqrt(jnp.mean(o * o, axis=-1, keepdims=True) + RMS_EPS) * gn_ref[:, vs]
            rr = r_ref[:, vs].astype(F32)
            og_ref[:, vs] = (on * (rr * _sigmoid(rr))).astype(BF16)

    return _carried_call(
        body, carry, name="gla_fwd", grid=(nc,),
        in_specs=[pl.BlockSpec((CH, D), lambda c: (c, 0)), pl.BlockSpec((CH, D), lambda c: (c, 1)),
                  pl.BlockSpec((CH, D), lambda c: (c, 6)), pl.BlockSpec((CH, LANES), lambda c: (c, 0)),
                  _resident((LANES, DK)), _resident((1, DK)), _resident((1, DV))],
        out_specs=[pl.BlockSpec((CH, DV), lambda c: (c, 0)), pl.BlockSpec((CH, DV), lambda c: (c, 0)),
                   pl.BlockSpec((1, DV, DKH), lambda c: (c, 0, 0))],
        out_shape=[SDS((t, DV), F32), SDS((t, DV), BF16), SDS((nc, DV, DKH), F32)],
        scratch_shapes=[pltpu.VMEM((DV, DKH), F32), pltpu.VMEM((CH, DK), F32)],
        compiler_params=_ARB1)(proj, proj, proj, alr, wau, balpha, gn)


def _conf_fwd(proj, cw, cb, lg, lb, carry=()):
    t = proj.shape[0]
    nt = t // TE

    def body(c1_ref, c2_ref, cw_ref, cb_ref, lg_ref, lb_ref, cc_ref, cs_ref, cext):
        i = pl.program_id(0)

        @pl.when(i == 0)
        def _():
            cext[0:HALO, :] = jnp.zeros((HALO, D), F32)

        @pl.when(i > 0)
        def _():
            cext[0:HALO, :] = cext[TE:TE + HALO, :]

        def glu(r0):
            c2 = c2_ref[pl.ds(r0, 32), :].astype(F32)
            cext[pl.ds(HALO + r0, 32), :] = c1_ref[pl.ds(r0, 32), :].astype(F32) * _sigmoid(c2)
        _row_loop(TE, 32, glu)

        def conv(r0):
            for part in range(D // CONV_LANES):
                cols = slice(part * CONV_LANES, (part + 1) * CONV_LANES)
                win = cext[pl.ds(r0, CONV_ROWS + HALO), cols]
                acc = jnp.zeros((CONV_ROWS, CONV_LANES), F32) + cb_ref[:, cols]
                for s, taps in _shift_classes([(2 + j, j) for j in range(CONF_K)]):
                    ws = _shifted(win, s)
                    for a8, j in taps:
                        acc = acc + jnp.tile(cw_ref[j, :, cols], (CONV_ROWS // 8, 1)) * ws[a8:a8 + CONV_ROWS, :]
                cc_ref[pl.ds(r0, CONV_ROWS), cols] = acc

        def norm(r0):
            for sub in range(CONV_ROWS // 32):
                rows = pl.ds(r0 + 32 * sub, 32)
                x = cc_ref[rows, :]
                xc = x - jnp.mean(x, axis=-1, keepdims=True)
                var = jnp.mean(xc * xc, axis=-1, keepdims=True)
                ln = xc * lax.rsqrt(var + LN_EPS) * lg_ref[...] + lb_ref[...]
                cs_ref[rows, :] = (ln * _sigmoid(ln)).astype(BF16)

        def step(r0):
            conv(r0)
            norm(r0)
        _row_loop(TE, CONV_ROWS, step)

    return _carried_call(
        body, carry, name="conf_fwd", grid=(nt,),
        in_specs=[pl.BlockSpec((TE, D), lambda i: (i, 2)), pl.BlockSpec((TE, D), lambda i: (i, 3)),
                  _resident((32, 8, D)), _resident((1, D)), _resident((1, D)), _resident((1, D))],
        out_specs=[pl.BlockSpec((TE, D), lambda i: (i, 0)), pl.BlockSpec((TE, D), lambda i: (i, 0))],
        out_shape=[SDS((t, D), F32), SDS((t, D), BF16)],
        scratch_shapes=[pltpu.VMEM((TE + HALO, D), F32)], compiler_params=_ARB1)(proj, proj, cw, cb, lg, lb)


def _mix_fwd(og, cs, proj, h0, wg, wc, wo):
    t = h0.shape[0]
    nt = t // TM

    def body(og_ref, cs_ref, g_ref, h0_ref, wg_ref, wc_ref, wo_ref, brg_ref, brc_ref, mg_ref, h1_ref):
        brg_ref[...] = _dot(og_ref[...], wg_ref[...]).astype(BF16)
        brc_ref[...] = _dot(cs_ref[...], wc_ref[...]).astype(BF16)

        def blk(r0):
            rows = pl.ds(r0, 32)
            gg = g_ref[rows, 0:D].astype(F32)
            gc = g_ref[rows, D:2 * D].astype(F32)
            m = _sigmoid(gg) * brg_ref[rows, :].astype(F32) + _sigmoid(gc) * brc_ref[rows, :].astype(F32)
            mg_ref[rows, :] = m.astype(BF16)
        _row_loop(TM, 32, blk)
        h1_ref[...] = h0_ref[...] + _dot(mg_ref[...], wo_ref[...])

    row = lambda w: pl.BlockSpec((TM, w), lambda i: (i, 0))
    return pl.pallas_call(
        body, name="mix_fwd", grid=(nt,),
        in_specs=[row(D), row(D), pl.BlockSpec((TM, 2 * D), lambda i: (i, 2)), row(D),
                  _once((D, D)), _once((D, D)), _once((D, D))],
        out_specs=[row(D), row(D), row(D), row(D)],
        out_shape=[SDS((t, D), BF16), SDS((t, D), BF16), SDS((t, D), BF16), SDS((t, D), F32)],
        compiler_params=_ARB1)(og, cs, proj, h0, wg, wc, wo)


def _ffn_out(up, fw, h1, wd, gf, target):
    t = h1.shape[0]
    nt = t // TE
    n_real = target.shape[0] + N_META

    def body(a_ref, bv_ref, fw_ref, h1_ref, wd_ref, gf_ref, tg_ref, tb_ref, f_ref, dh2_ref, red_ref, before, hs):
        i = pl.program_id(0)

        @pl.when(i == 0)
        def _():
            before[...] = jnp.zeros_like(before)
            red_ref[...] = jnp.zeros_like(red_ref)

        def conv(r0, win):
            ac = fw_ref[3] + fw_ref[0] * win[14:30, :] + fw_ref[1] * win[15:31, :] + fw_ref[2] * win[16:32, :]
            f_ref[pl.ds(r0, 16), :] = (ac * _sigmoid(ac) * bv_ref[pl.ds(r0, 16), :].astype(F32)).astype(BF16)

        conv(0, jnp.concatenate([before[...], a_ref[0:HALO_F, :].astype(F32)], axis=0))

        def conv_blk(k, c):
            r0 = pl.multiple_of(k * 16, 16)
            conv(r0, a_ref[pl.ds(r0 - HALO_F, 32), :].astype(F32))
            return c
        lax.fori_loop(1, TE // 16, conv_blk, 0)
        before[...] = a_ref[TE - HALO_F:TE, :].astype(F32)
        hs[...] = h1_ref[...] + _dot(f_ref[...], wd_ref[...])

        def head(r0, tg):
            rows = pl.ds(r0, 32)
            h2 = hs[rows, :]
            rinv = lax.rsqrt(jnp.mean(h2 * h2, axis=-1, keepdims=True) + RMS_EPS)
            hh = h2 * rinv
            gid = i * TE + r0 + lax.broadcasted_iota(jnp.int32, (32, 1), 0)
            live = jnp.logical_and(gid >= N_META, gid < n_real)
            err = jnp.where(live, hh * gf_ref[...] - tg, 0.0)
            dy = err * (1.0 / D)
            red_ref[0:8, :] += _rows8(err * err)
            red_ref[8:16, :] += _rows8(dy * hh)
            dhh = dy * gf_ref[...]
            dh2_ref[rows, :] = rinv * (dhh - hh * jnp.mean(dhh * hh, axis=-1, keepdims=True))

        head(0, jnp.concatenate([tb_ref[...], tg_ref[0:N_META, :]], axis=0))

        def blk(k, c):
            r0 = pl.multiple_of(k * 32, 32)
            head(r0, tg_ref[pl.ds(pl.multiple_of(r0 - N_META, N_META), 32), :])
            return c
        lax.fori_loop(1, TE // 32, blk, 0, unroll=9)

    row = lambda w: pl.BlockSpec((TE, w), lambda i: (i, 0))
    return pl.pallas_call(
        body, name="ffn_out", grid=(nt,),
        in_specs=[pl.BlockSpec((TE, DFF), lambda i: (i, 0)), pl.BlockSpec((TE, DFF), lambda i: (i, 1)),
                  _resident((4, 16, DFF)), row(D), _resident((DFF, D)), _resident((1, D)), row(D),
                  pl.BlockSpec((N_META, D), lambda i: (jnp.maximum(i * (TE // N_META) - 1, 0), 0))],
        out_specs=[row(DFF), row(D), _resident((16, D))],
        out_shape=[SDS((t, DFF), BF16), SDS((t, D), F32), SDS((16, D), F32)],
        scratch_shapes=[pltpu.VMEM((HALO_F, DFF), F32), pltpu.VMEM((TE, D), F32)],
        compiler_params=_ARB1)(up, up, fw, h1, wd, gf, target, target)


def _ffn_bwd(dh2, wd, up, fw):
    t = dh2.shape[0]
    nt = t // TE
    hb = TE // HALO_F

    def body(dh_ref, wd_ref, a_ref, ah_ref, bv_ref, fw_ref, dup_ref, dw_ref, dax, dfs):
        i = pl.program_id(0)
        ti = nt - 1 - i

        @pl.when(i == 0)
        def _():
            dax[TE:TE + HALO_F, :] = jnp.zeros((HALO_F, DFF), F32)
            dw_ref[...] = jnp.zeros_like(dw_ref)

        @pl.when(i > 0)
        def _():
            dax[TE:TE + HALO_F, :] = dax[0:HALO_F, :]

        dfs[...] = _dot_nt(dh_ref[...].astype(BF16), wd_ref[...])

        def act(r0, win):
            rows = pl.ds(r0, 16)
            ac = fw_ref[3] + fw_ref[0] * win[14:30, :] + fw_ref[1] * win[15:31, :] + fw_ref[2] * win[16:32, :]
            sg = _sigmoid(ac)
            df = dfs[rows, :]
            dsg = df * sg
            dup_ref[rows, DFF:2 * DFF] = (dsg * ac).astype(BF16)
            dac = dsg * bv_ref[rows, :].astype(F32) * (1.0 + ac * (1.0 - sg))
            dax[rows, :] = dac
            dw_ref[3] += _rows8(dac)
            for j in range(FFN_K):
                dw_ref[j] += _rows8(dac * win[14 + j:30 + j, :])

        before = jnp.where(ti > 0, ah_ref[...].astype(F32), 0.0)
        act(0, jnp.concatenate([before, a_ref[0:HALO_F, :].astype(F32)], axis=0))

        def act_blk(k, c):
            r0 = pl.multiple_of(k * 16, 16)
            act(r0, a_ref[pl.ds(r0 - HALO_F, 32), :].astype(F32))
            return c
        lax.fori_loop(1, TE // 16, act_blk, 0)

        def convt(r0):
            win = dax[pl.ds(r0, 32), :]
            da = fw_ref[2] * win[0:16, :] + fw_ref[1] * win[1:17, :] + fw_ref[0] * win[2:18, :]
            dup_ref[pl.ds(r0, 16), 0:DFF] = da.astype(BF16)
        _row_loop(TE, 16, convt)

    rev = lambda w: pl.BlockSpec((TE, w), lambda i: (nt - 1 - i, 0))
    return pl.pallas_call(
        body, name="ffn_bwd", grid=(nt,),
        in_specs=[rev(D), _resident((DFF, D)), rev(DFF),
                  pl.BlockSpec((HALO_F, DFF), lambda i: (jnp.maximum((nt - 1 - i) * hb - 1, 0), 0)),
                  pl.BlockSpec((TE, DFF), lambda i: (nt - 1 - i, 1)), _resident((4, 16, DFF))],
        out_specs=[rev(2 * DFF), _resident((4, 8, DFF))],
        out_shape=[SDS((t, 2 * DFF), BF16), SDS((4, 8, DFF), F32)],
        scratch_shapes=[pltpu.VMEM((TE + HALO_F, DFF), F32), pltpu.VMEM((TE, DFF), F32)],
        compiler_params=_ARB1)(dh2, wd, up, up, up, fw)


def _dgrad_norm(dy, w_t, h, g, dres, name, carry=()):
    t, k = dy.shape
    nt = t // TM

    def body(dy_ref, w_ref, h_ref, g_ref, dr_ref, dh_ref, dg_ref, acc):
        @pl.when(pl.program_id(0) == 0)
        def _():
            dg_ref[...] = jnp.zeros_like(dg_ref)

        acc[...] = _dot(dy_ref[...], w_ref[...])

        def blk(r0):
            rows = pl.ds(r0, 32)
            x = h_ref[rows, :]
            rinv = lax.rsqrt(jnp.mean(x * x, axis=-1, keepdims=True) + RMS_EPS)
            hh = x * rinv
            du = acc[rows, :]
            dg_ref[...] += _rows8(du * hh)
            dhh = du * g_ref[...]
            dh_ref[rows, :] = dr_ref[rows, :] + rinv * (dhh - hh * jnp.mean(dhh * hh, axis=-1, keepdims=True))
        _row_loop(TM, 32, blk, unroll=10)

    row = pl.BlockSpec((TM, D), lambda i: (i, 0))
    in_specs = [pl.BlockSpec((TM, k), lambda i: (i, 0)), _once((k, D)), row, _resident((1, D)), row]
    return _carried_call(
        body, carry, name=name, grid=(nt,), in_specs=in_specs, out_specs=[row, _resident((8, D))],
        out_shape=[SDS((t, D), F32), SDS((8, D), F32)],
        scratch_shapes=[pltpu.VMEM((TM, D), F32)], compiler_params=_ARB1)(dy, w_t, h, g, dres)


def _in_dgrad(dy, w_t, h, g, dres, dy_extra, w_extra_t, carry=()):
    t, k = dy.shape
    nt = t // TM

    def body(dy_ref, w_ref, h_ref, g_ref, dr_ref, de_ref, we_ref, dh_ref, dg_ref, acc):
        @pl.when(pl.program_id(0) == 0)
        def _():
            dg_ref[...] = jnp.zeros_like(dg_ref)

        acc[...] = _dot(dy_ref[...], w_ref[...])
        acc[...] += _dot(de_ref[...], we_ref[...])

        def blk(r0):
            rows = pl.ds(r0, 32)
            x = h_ref[rows, :]
            rinv = lax.rsqrt(jnp.mean(x * x, axis=-1, keepdims=True) + RMS_EPS)
            hh = x * rinv
            du = acc[rows, :]
            dg_ref[...] += _rows8(du * hh)
            dhh = du * g_ref[...]
            dh_ref[rows, :] = dr_ref[rows, :] + rinv * (dhh - hh * jnp.mean(dhh * hh, axis=-1, keepdims=True))
        _row_loop(TM, 32, blk, unroll=4)

    tile = lambda w: pl.BlockSpec((TM, w), lambda i: (i, 0))
    return _carried_call(
        body, carry, name="in_dgrad", grid=(nt,),
        in_specs=[tile(k), _once((k, D)), tile(D), _resident((1, D)), tile(D), tile(dy_extra.shape[1]),
                  _once(w_extra_t.shape)],
        out_specs=[tile(D), _resident((8, D))], out_shape=[SDS((t, D), F32), SDS((8, D), F32)],
        scratch_shapes=[pltpu.VMEM((TM, D), F32)], compiler_params=_ARB1)(dy, w_t, h, g, dres, dy_extra, w_extra_t)


def _wgrad(x, dy, tk, name):
    t, k = x.shape
    n = dy.shape[1]
    tm = _row_tile(t, TM_BIG)
    nk, nt = k // tk, t // tm

    def body(x_ref, dy_ref, o_ref, acc):
        @pl.when(pl.program_id(1) == 0)
        def _():
            acc[...] = jnp.zeros_like(acc)
        acc[...] += _dot_tn(x_ref[...], dy_ref[...].astype(BF16))

        @pl.when(pl.program_id(1) == nt - 1)
        def _():
            o_ref[...] = acc[...].astype(BF16)

    return pl.pallas_call(
        body, name=name, grid=(nk, nt),
        in_specs=[pl.BlockSpec((tm, tk), lambda j, i: (i, j)), pl.BlockSpec((tm, n), lambda j, i: (i, 0))],
        out_specs=pl.BlockSpec((tk, n), lambda j, i: (j, 0)), out_shape=SDS((k, n), BF16),
        scratch_shapes=[pltpu.VMEM((tk, n), F32)], compiler_params=_ARB2)(x, dy)


def _mix_bwd(dh1, wo, wg, wc, proj, brg, brc):
    t = dh1.shape[0]
    nt = t // TM

    def body(dh_ref, wo_ref, wg_ref, wc_ref, g_ref, brg_ref, brc_ref, dbg_ref, dbc_ref, dog_ref, dcs_ref, dp_ref, dm):
        dm[...] = _dot_nt(dh_ref[...].astype(BF16), wo_ref[...])

        def blk(r0):
            rows = pl.ds(r0, 32)
            d = dm[rows, :]
            sg = _sigmoid(g_ref[rows, 0:D].astype(F32))
            sc = _sigmoid(g_ref[rows, D:2 * D].astype(F32))
            dbg_ref[rows, :] = (d * sg).astype(BF16)
            dbc_ref[rows, :] = (d * sc).astype(BF16)
            dp_ref[rows, 0:D] = (d * brg_ref[rows, :].astype(F32) * sg * (1.0 - sg)).astype(BF16)
            dp_ref[rows, D:2 * D] = (d * brc_ref[rows, :].astype(F32) * sc * (1.0 - sc)).astype(BF16)
        _row_loop(TM, 32, blk)
        dog_ref[...] = _dot_nt(dbg_ref[...], wg_ref[...]).astype(BF16)
        dcs_ref[...] = _dot_nt(dbc_ref[...], wc_ref[...]).astype(BF16)

    row = pl.BlockSpec((TM, D), lambda i: (i, 0))
    wide = pl.BlockSpec((TM, 2 * D), lambda i: (i, 2))
    return pl.pallas_call(
        body, name="mix_bwd", grid=(nt,),
        in_specs=[row, _once((D, D)), _once((D, D)), _once((D, D)), wide, row, row],
        out_specs=[row, row, row, row, wide],
        out_shape=[SDS((t, D), BF16)] * 4 + [SDS((t, NPROJ), BF16)],
        scratch_shapes=[pltpu.VMEM((TM, D), F32)], compiler_params=_ARB1)(dh1, wo, wg, wc, proj, brg, brc)


def _glapost_bwd(dog, o, proj, gn, dproj):
    t = o.shape[0]
    nt = t // TM

    def body(dog_ref, o_ref, r_ref, gn_ref, dp_in, do_ref, dp_ref, dgn_ref):
        del dp_in

        @pl.when(pl.program_id(0) == 0)
        def _():
            dgn_ref[...] = jnp.zeros_like(dgn_ref)

        def blk(r0):
            rows = pl.ds(r0, 32)
            for h in range(HEADS):
                vs = slice(h * DVH, (h + 1) * DVH)
                x = o_ref[rows, vs]
                rinv = lax.rsqrt(jnp.mean(x * x, axis=-1, keepdims=True) + RMS_EPS)
                oh = x * rinv
                g = gn_ref[:, vs]
                rr = r_ref[rows, vs].astype(F32)
                sr = _sigmoid(rr)
                d = dog_ref[rows, vs].astype(F32)
                dp_ref[rows, vs] = (d * oh * g * sr * (1.0 + rr * (1.0 - sr))).astype(BF16)
                don = d * rr * sr
                dgn_ref[:, vs] += _rows8(don * oh)
                doh = don * g
                do_ref[rows, vs] = (rinv * (doh - oh * jnp.mean(doh * oh, axis=-1, keepdims=True))).astype(BF16)
        _row_loop(TM, 32, blk, unroll=5)

    row = pl.BlockSpec((TM, D), lambda i: (i, 0))
    rcol = pl.BlockSpec((TM, D), lambda i: (i, 6))
    return pl.pallas_call(
        body, name="glapost_bwd", grid=(nt,),
        in_specs=[row, row, rcol, _resident((1, D)), pl.BlockSpec(memory_space=pl.ANY)],
        out_specs=[row, rcol, _resident((8, D))],
        out_shape=[SDS((t, D), BF16), SDS((t, NPROJ), BF16), SDS((8, D), F32)],
        input_output_aliases={4: 1}, compiler_params=_ARB1)(dog, o, proj, gn, dproj)


def _conf_bwd(dcs, cc, proj, cw, lg, lb, dproj, carry=()):
    t = cc.shape[0]
    nt = t // TE

    def body(dcs_ref, cc_ref, c1_ref, c2_ref, cw_ref, lg_ref, lb_ref, dp_in, dp_ref, dw_ref, ds_ref, cval, gate, dext):
        del dp_in
        i = pl.program_id(0)

        @pl.when(i == 0)
        def _():
            dext[TE:TE + HALO, :] = jnp.zeros((HALO, D), F32)
            dw_ref[...] = jnp.zeros_like(dw_ref)
            ds_ref[...] = jnp.zeros_like(ds_ref)

        @pl.when(i > 0)
        def _():
            dext[TE:TE + HALO, :] = dext[0:HALO, :]

        def pre(r0):
            rows = pl.ds(r0, 32)
            s2 = _sigmoid(c2_ref[rows, :].astype(F32))
            gate[rows, :] = s2
            cval[rows, :] = c1_ref[rows, :].astype(F32) * s2
            x = cc_ref[rows, :]
            mu = jnp.mean(x, axis=-1, keepdims=True)
            xc = x - mu
            rstd = lax.rsqrt(jnp.mean(xc * xc, axis=-1, keepdims=True) + LN_EPS)
            xh = xc * rstd
            ln = xh * lg_ref[...] + lb_ref[...]
            sg = _sigmoid(ln)
            dln = dcs_ref[rows, :].astype(F32) * sg * (1.0 + ln * (1.0 - sg))
            ds_ref[0] += _rows8(dln * xh)
            ds_ref[1] += _rows8(dln)
            dxh = dln * lg_ref[...]
            dcc = rstd * (dxh - jnp.mean(dxh, axis=-1, keepdims=True) - xh * jnp.mean(dxh * xh, axis=-1, keepdims=True))
            dext[rows, :] = dcc
            ds_ref[2] += _rows8(dcc)
        _row_loop(TE, 32, pre, unroll=10)

        def convt(r0):
            rows = pl.ds(r0, CONV_ROWS)
            for part in range(D // CONV_LANES):
                cols = slice(part * CONV_LANES, (part + 1) * CONV_LANES)
                wd = dext[pl.ds(r0, CONV_ROWS + HALO), cols]
                cv = cval[rows, cols]
                dc = jnp.zeros((CONV_ROWS, CONV_LANES), F32)
                for s, taps in _shift_classes([(CONF_K - 1 - j, j) for j in range(CONF_K)]):
                    ws = _shifted(wd, s)
                    for a8, j in taps:
                        ahead = ws[a8:a8 + CONV_ROWS, :]
                        dc = dc + jnp.tile(cw_ref[j, :, cols], (CONV_ROWS // 8, 1)) * ahead
                        dw_ref[j, :, cols] += _rows8(cv * ahead)
                c1 = c1_ref[rows, cols].astype(F32)
                s2 = gate[rows, cols]
                dp_ref[rows, cols] = (dc * s2).astype(BF16)
                dp_ref[rows, D + part * CONV_LANES:D + (part + 1) * CONV_LANES] = (dc * c1 * s2 * (1.0 - s2)).astype(BF16)
        _row_loop(TE, CONV_ROWS, convt)

    rev = lambda col: pl.BlockSpec((TE, D), lambda i: (nt - 1 - i, col))
    return _carried_call(
        body, carry, name="conf_bwd", grid=(nt,),
        in_specs=[rev(0), rev(0), rev(2), rev(3), _resident((32, 8, D)), _resident((1, D)), _resident((1, D)),
                  pl.BlockSpec(memory_space=pl.ANY)],
        out_specs=[pl.BlockSpec((TE, 2 * D), lambda i: (nt - 1 - i, 1)), _resident((32, 8, D)), _resident((3, 8, D))],
        out_shape=[SDS((t, NPROJ), BF16), SDS((32, 8, D), F32), SDS((3, 8, D), F32)],
        scratch_shapes=[pltpu.VMEM((TE, D), F32), pltpu.VMEM((TE, D), F32), pltpu.VMEM((TE + HALO, D), F32)],
        input_output_aliases={7: 0}, compiler_params=_ARB1)(dcs, cc, proj, proj, cw, lg, lb, dproj)


def _gla_bwd(proj, alr, wau, balpha, do, sall, dproj, carry=()):
    t = proj.shape[0]
    nc = t // CH

    def body(qk_ref, v_ref, a_ref, wau_ref, ba_ref, do_ref, s_ref, dp_in, dp_ref, da_ref, dwau_ref, dba_ref, ds_scr, dla_scr, b_scr):
        del dp_in

        @pl.when(pl.program_id(0) == 0)
        def _():
            ds_scr[...] = jnp.zeros_like(ds_scr)
            dwau_ref[...] = jnp.zeros_like(dwau_ref)
            dba_ref[...] = jnp.zeros_like(dba_ref)

        z, b, causal = _gla_decay(a_ref, wau_ref, ba_ref)
        b_scr[...] = b
        dlasts = []
        for h in range(HEADS):
            ks = slice(h * DKH, (h + 1) * DKH)
            vs = slice(h * DVH, (h + 1) * DVH)
            bh, mh, lh = b_scr[:, ks], b_scr[CH // 2:CH // 2 + 1, ks], b_scr[CH - 1:CH, ks]
            q = qk_ref[:, ks].astype(F32) * (DKH ** -0.5)
            k = qk_ref[:, DK + h * DKH:DK + (h + 1) * DKH].astype(F32)
            v = v_ref[:, vs]
            dout = do_ref[:, vs]
            eq, ek, eb, eg, el = jnp.exp(bh - mh), jnp.exp(mh - bh), jnp.exp(bh), jnp.exp(lh - bh), jnp.exp(lh)
            qt, kt = (q * eq).astype(BF16), (k * ek).astype(BF16)
            qg, kg = (q * eb).astype(BF16), (k * eg).astype(BF16)
            st = s_ref[0, vs, :]
            dsn = ds_scr[vs, :]
            st16, dsn16 = st.astype(BF16), dsn.astype(BF16)
            a = jnp.where(causal, _dot_nt(qt, kt), 0.0).astype(BF16)
            da = jnp.where(causal, _dot_nt(dout, v), 0.0).astype(BF16)
            dq_inter = _dot(dout, st16) * eb
            dk_inter = _dot(v, dsn16) * eg
            dq = _dot(da, kt) * eq + dq_inter
            dk = _dot_tn(da, qt) * ek + dk_inter
            dv = _dot_tn(a, dout) + _dot_nt(kg, dsn16)
            dlasts.append(jnp.sum(k * dk_inter, axis=0, keepdims=True) + jnp.sum(st * dsn, axis=0, keepdims=True) * el)
            dla_scr[:, ks] = q * dq - k * dk
            ds_scr[vs, :] = dsn * el + _dot_tn(dout, qg)
            dp_ref[:, ks] = (dq * (DKH ** -0.5)).astype(BF16)
            dp_ref[:, DK + h * DKH:DK + (h + 1) * DKH] = dk.astype(BF16)
            dp_ref[:, D + h * DVH:D + (h + 1) * DVH] = dv.astype(BF16)
        r = lax.broadcasted_iota(jnp.int32, (CH, CH), 0)
        c = lax.broadcasted_iota(jnp.int32, (CH, CH), 1)
        dla = _tri_matmul((r <= c).astype(BF16), dla_scr[...]) + jnp.concatenate(dlasts, axis=1)
        dz = (dla * (1.0 / TAU) * _sigmoid(-z)).astype(BF16)
        da_ref[...] = _dot_nt(dz, wau_ref[...]).astype(BF16)
        dwau_ref[...] += _dot_tn(a_ref[...], dz)
        dba_ref[...] += _rows8(dz.astype(F32))

    rev = lambda w, col: pl.BlockSpec((CH, w), lambda c: (nc - 1 - c, col))
    return _carried_call(
        body, carry, name="gla_bwd", grid=(nc,),
        in_specs=[rev(D, 0), rev(D, 1), rev(LANES, 0), _resident((LANES, DK)), _resident((1, DK)), rev(D, 0),
                  pl.BlockSpec((1, DV, DKH), lambda c: (nc - 1 - c, 0, 0)), pl.BlockSpec(memory_space=pl.ANY)],
        out_specs=[rev(2 * D, 0), rev(LANES, 0), _resident((LANES, DK)), _resident((8, DK))],
        out_shape=[SDS((t, NPROJ), BF16), SDS((t, LANES), BF16), SDS((LANES, DK), F32), SDS((8, DK), F32)],
        scratch_shapes=[pltpu.VMEM((DV, DKH), F32), pltpu.VMEM((CH, DK), F32), pltpu.VMEM((CH, DK), F32)],
        input_output_aliases={7: 0}, compiler_params=_ARB1)(proj, proj, alr, wau, balpha, do, sall, dproj)


def _all_gather(xs, name):
    n = len(xs)

    def body(*refs):
        x_refs, out_refs = refs[:n], refs[n:2 * n]
        send_sems, recv_sems = refs[2 * n:]
        x, y, c = _my_place()
        me, sibling = (x, y, c), (x, y, 1 - c)
        x_nbr, y_nbr, diagonal = (1 - x, y), (x, 1 - y), (1 - x, 1 - y)

        def slot(p, px, py, pc):
            return out_refs[p].at[4 * px + 2 * py + pc]

        def copy(p, k, block, to, src=None):
            return pltpu.make_async_remote_copy(
                src_ref=slot(p, *block) if src is None else src, dst_ref=slot(p, *block),
                send_sem=send_sems.at[7 * p + k], recv_sem=recv_sems.at[7 * p + k], device_id=to, device_id_type=MESH_T)

        for p in range(n):
            for k, to in enumerate((sibling, (*x_nbr, c), (*y_nbr, c))):
                copy(p, k, me, to, src=x_refs[p]).start()
        for p in range(n):
            @pl.when(c == 1)
            def _():
                copy(p, 2, (*y_nbr, c), me).wait_recv()
                copy(p, 3, (*y_nbr, c), (*x_nbr, c)).start()
                copy(p, 1, (*x_nbr, c), me).wait_recv()

            @pl.when(c == 0)
            def _():
                copy(p, 1, (*x_nbr, c), me).wait_recv()
                copy(p, 3, (*x_nbr, c), (*y_nbr, c)).start()
                copy(p, 2, (*y_nbr, c), me).wait_recv()
        for p in range(n):
            copy(p, 4, (*x_nbr, c), sibling).start()
            copy(p, 5, (*y_nbr, c), sibling).start()
        for p in range(n):
            copy(p, 3, (*diagonal, c), me).wait_recv()
            copy(p, 6, (*diagonal, c), sibling).start()
        for p in range(n):
            copy(p, 0, sibling, me).wait_recv()
            for j, chip in enumerate((x_nbr, y_nbr, diagonal)):
                copy(p, 4 + j, (*chip, 1 - c), me).wait_recv()
        for p in range(n):
            for k in range(7):
                copy(p, k, me, sibling, src=x_refs[p]).wait_send()

    hbm = pl.BlockSpec(memory_space=pl.ANY)
    return pl.pallas_call(
        body, name=name, out_shape=[SDS((N_DEV, *a.shape), a.dtype) for a in xs],
        in_specs=[hbm] * n, out_specs=[hbm] * n,
        scratch_shapes=[pltpu.SemaphoreType.DMA((7 * n,)), pltpu.SemaphoreType.DMA((7 * n,))])(*xs)


def _exchange(gs):
    n = len(gs)

    def body(*refs):
        g_refs, land_refs = refs[:n], refs[n:2 * n]
        send_sems, recv_sems, local_sems = refs[2 * n:]
        x, y, c = _my_place()
        my_idx = 4 * x + 2 * y + c
        mine = [pltpu.make_async_copy(g_refs[p].at[my_idx], land_refs[p].at[my_idx], local_sems.at[p]) for p in range(n)]
        for cp in mine:
            cp.start()
        copies = []
        for k in range(1, N_DEV):
            px, py, pc = _flip(x, k & 4), _flip(y, k & 2), _flip(c, k & 1)
            p_idx = 4 * px + 2 * py + pc
            for p in range(n):
                s = 7 * p + k - 1
                cp = pltpu.make_async_remote_copy(
                    src_ref=g_refs[p].at[p_idx], dst_ref=land_refs[p].at[my_idx], send_sem=send_sems.at[s],
                    recv_sem=recv_sems.at[s], device_id=(px, py, pc), device_id_type=MESH_T)
                cp.start()
                arrival = pltpu.make_async_remote_copy(
                    src_ref=g_refs[p].at[p_idx], dst_ref=land_refs[p].at[p_idx], send_sem=send_sems.at[s],
                    recv_sem=recv_sems.at[s], device_id=(px, py, pc), device_id_type=MESH_T)
                copies.append((cp, arrival))
        for cp, arrival in copies:
            arrival.wait_recv()
        for cp, arrival in copies:
            cp.wait_send()
        for cp in mine:
            cp.wait()

    hbm = pl.BlockSpec(memory_space=pl.ANY)
    return pl.pallas_call(
        body, name="grad_exchange", out_shape=[SDS(g.shape, g.dtype) for g in gs],
        in_specs=[hbm] * n, out_specs=[hbm] * n,
        scratch_shapes=[pltpu.SemaphoreType.DMA((7 * n,)), pltpu.SemaphoreType.DMA((7 * n,)),
                        pltpu.SemaphoreType.DMA((n,))])(*gs)


def _adamw(land, w, m, v, rows_blk, name):
    rows = w.shape[0]

    def body(l_ref, w_ref, m_ref, v_ref, g_ref, d_ref, nm_ref, nv_ref):
        g = l_ref[0].astype(F32)
        for s in range(1, N_DEV):
            g = g + l_ref[s].astype(F32)
        nm = ADAM_B1 * m_ref[...] + (1.0 - ADAM_B1) * g
        nv = ADAM_B2 * v_ref[...] + (1.0 - ADAM_B2) * (g * g)
        m_hat = nm / (1.0 - ADAM_B1 ** ADAM_STEP)
        v_hat = nv / (1.0 - ADAM_B2 ** ADAM_STEP)
        g_ref[...] = g
        d_ref[...] = -ADAM_LR * (m_hat / (jnp.sqrt(v_hat) + ADAM_EPS) + ADAM_WD * w_ref[...])
        nm_ref[...] = nm
        nv_ref[...] = nv

    blk = pl.BlockSpec((rows_blk, D), lambda i: (i, 0))
    return pl.pallas_call(
        body, name=name, grid=(rows // rows_blk,),
        in_specs=[pl.BlockSpec((N_DEV, rows_blk, D), lambda i: (0, i, 0)), blk, blk, blk],
        out_specs=[blk] * 4, out_shape=[SDS((rows, D), F32)] * 4, compiler_params=_ARB1)(land, w, m, v)


BIG = ("w_in", "w_up", "w_down", "w_gla_o", "w_conf_o", "w_out")
BIG_TRANSPOSED = ("w_in", "w_up")
SMALL_SHARDED = ("meta_tokens", "conf_dw_w", "ffn_dw_w", "w_alpha_up")
REPLICATED = ("norm_mix_g", "b_alpha", "gla_norm_g", "conf_dw_b", "conf_ln_g", "conf_ln_b", "norm_ffn_g", "ffn_dw_b",
              "final_norm_g")
N_IN = sum(IN_WIDTHS)
W_IN_ROWS = N_IN // N_DEV
W_IN_PAD = -(-W_IN_ROWS // 16) * 16
ADAM_BLOCK = {"w_in": W_IN_PAD // 3, "w_up": 176, "w_down": 176, "w_gla_o": 128, "w_conf_o": 128, "w_out": 128}
SMALL_ROWS = 32


def _to_panel(name, shard):
    a = shard.reshape(shard.shape[-2], shard.shape[-1])
    if name in BIG_TRANSPOSED:
        a = a.T
    if name == "w_in":
        a = jnp.pad(a, ((0, W_IN_PAD - W_IN_ROWS), (0, 0)))
    return a


def _from_panel(name, panel, shape):
    a = panel[0:W_IN_ROWS] if name == "w_in" else panel
    if name in BIG_TRANSPOSED:
        a = a.T
    return a.reshape(shape)


def _pack_small(arrs):
    flat = jnp.concatenate([jnp.pad(a.reshape(-1), (0, (-a.size) % D)) for a in arrs])
    return jnp.pad(flat, (0, SMALL_ROWS * D - flat.shape[0])).reshape(SMALL_ROWS, D)


def _unpack_small(panel, shapes):
    flat, out, off = panel.reshape(-1), [], 0
    for shp in shapes:
        n = 1
        for s in shp:
            n *= s
        out.append(flat[off:off + n].reshape(shp))
        off += n + (-n) % D
    return out


def _local_step(x, target, w, shards=None):
    dist = shards is not None
    w = dict(w)

    def gather(names):
        return [(shards[n], False) for n in names] if dist else []

    def scatter(*arrs):
        return [(a.reshape(N_DEV, -1, D), True) for a in arrs] if dist else []

    s = x.shape[0]
    n_real = s + N_META
    t = -(-n_real // TM) * TM

    q0, r0, a0, c0 = 0, 2 * DK + DV, 2 * DK + 2 * DV, 2 * DK + 2 * DV + RANK
    wt = w["w_in_t"]
    w_main = jnp.concatenate([wt[q0:r0], wt[c0:N_IN], wt[r0:a0]], axis=0)
    w_a = jnp.pad(wt[a0:c0], ((0, LANES - RANK), (0, 0)))
    wau = jnp.pad(w["w_alpha_up"].astype(BF16), ((0, LANES - RANK), (0, 0)))
    row = lambda name: w[name].reshape(1, -1)
    cw = jnp.broadcast_to(jnp.pad(w["conf_dw_w"], ((0, 32 - CONF_K), (0, 0)))[:, None, :], (32, 8, D))
    fw = jnp.broadcast_to(jnp.concatenate([w["ffn_dw_w"], w["ffn_dw_b"].reshape(1, -1)], axis=0)[:, None, :],
                          (FFN_K + 1, 16, DFF))

    early = ("w_gla_o", "w_conf_o", "w_out")
    h0, u1, proj, alr, *landed = _in_proj(x, w["meta_tokens"], row("norm_mix_g"), w_main, w_a, t, carry=gather(early))
    for n, land in zip(early, landed):
        w[n] = land.reshape(-1, D)
    o, og, sall, *landed = _gla_fwd(proj, alr, wau, row("b_alpha"), row("gla_norm_g"), carry=gather(("w_down",)))
    if dist:
        w["w_down"] = landed[0].reshape(-1, D)
    cc, cs, *landed = _conf_fwd(proj, cw, row("conf_dw_b"), row("conf_ln_g"), row("conf_ln_b"), carry=gather(("w_up",)))
    if dist:
        w["w_up_t"] = landed[0].reshape(-1, D)
    brg, brc, merged, h1 = _mix_fwd(og, cs, proj, h0, w["w_gla_o"], w["w_conf_o"], w["w_out"])
    u2, up = _norm_matmul(h1, row("norm_ffn_g"), w["w_up_t"], 512, "up_proj")
    f, dh2, red = _ffn_out(up, fw, h1, w["w_down"], row("final_norm_g"), target)
    loss = 0.5 / D * jnp.sum(red[0:8])

    g = {"final_norm_g": jnp.sum(red[8:16], axis=0)}
    dup, dfw = _ffn_bwd(dh2, w["w_down"], up, fw)
    g["ffn_dw_w"] = jnp.sum(dfw[0:FFN_K], axis=1)
    g["ffn_dw_b"] = jnp.sum(dfw[3], axis=0)
    g["w_down"] = _wgrad(f, dh2, 1408, "wgrad_down")
    dh1, dg2, *landed = _dgrad_norm(dup, w["w_up_t"], h1, row("norm_ffn_g"), dh2, "up_dgrad", carry=scatter(g["w_down"]))
    if dist:
        g["w_down"] = landed[0]
    g["norm_ffn_g"] = jnp.sum(dg2, axis=0)
    g["w_up_t"] = _wgrad(dup, u2, 1408, "wgrad_up")
    dbrg, dbrc, dog, dcs, dproj = _mix_bwd(dh1, w["w_out"], w["w_gla_o"], w["w_conf_o"], proj, brg, brc)
    g["w_out"] = _wgrad(merged, dh1, 1024, "wgrad_out")
    g["w_gla_o"] = _wgrad(og, dbrg, 1024, "wgrad_gla_o")
    g["w_conf_o"] = _wgrad(cs, dbrc, 1024, "wgrad_conf_o")
    do, dproj, dgn = _glapost_bwd(dog, o, proj, row("gla_norm_g"), dproj)
    g["gla_norm_g"] = jnp.sum(dgn, axis=0)
    dproj, dcw, dst, *landed = _conf_bwd(dcs, cc, proj, cw, row("conf_ln_g"), row("conf_ln_b"), dproj,
                                         carry=scatter(g["w_up_t"]))
    if dist:
        g["w_up_t"] = landed[0]
    g["conf_dw_w"] = jnp.sum(dcw[0:CONF_K], axis=1)
    g["conf_ln_g"], g["conf_ln_b"], g["conf_dw_b"] = jnp.sum(dst[0], axis=0), jnp.sum(dst[1], axis=0), jnp.sum(dst[2], axis=0)
    dproj, dalr, dwau, dba, *landed = _gla_bwd(proj, alr, wau, row("b_alpha"), do, sall, dproj,
                                               carry=scatter(g["w_out"], g["w_gla_o"], g["w_conf_o"]))
    if dist:
        g["w_out"], g["w_gla_o"], g["w_conf_o"] = landed
    g["w_alpha_up"] = dwau[0:RANK]
    g["b_alpha"] = jnp.sum(dba, axis=0)
    dw_main = _wgrad(dproj, u1, 1024, "wgrad_in")
    dw_a = _wgrad(dalr, u1, LANES, "wgrad_alr")
    g["w_in_t"] = jnp.concatenate([dw_main[0:r0], dw_main[NPROJ - DV:NPROJ], dw_a[0:RANK], dw_main[r0:NPROJ - DV]], axis=0)
    w_in_blocks = []
    if dist:
        pad = ((0, 0), (0, W_IN_PAD - W_IN_ROWS), (0, 0))
        w_in_blocks = [(jnp.pad(g["w_in_t"].reshape(N_DEV, W_IN_ROWS, D), pad), True)]
    dh0, dg1, *landed = _in_dgrad(dproj, w_main, h0, row("norm_mix_g"), dh1, dalr, w_a, carry=w_in_blocks)
    if dist:
        g["w_in_t"] = landed[0]
    g["norm_mix_g"] = jnp.sum(dg1, axis=0)
    g["meta_tokens"] = dh0[0:N_META]
    return loss, dh0[N_META:n_real], g


def kernel(x, meta_tokens, norm_mix_g, w_in, w_alpha_up, b_alpha, gla_norm_g, w_gla_o, conf_dw_w, conf_dw_b, conf_ln_g, conf_ln_b, w_conf_o, w_out, norm_ffn_g, w_up, ffn_dw_w, ffn_dw_b, w_down, final_norm_g, loss_target, m_meta_tokens, m_norm_mix_g, m_w_in, m_w_alpha_up, m_b_alpha, m_gla_norm_g, m_w_gla_o, m_conf_dw_w, m_conf_dw_b, m_conf_ln_g, m_conf_ln_b, m_w_conf_o, m_w_out, m_norm_ffn_g, m_w_up, m_ffn_dw_w, m_ffn_dw_b, m_w_down, m_final_norm_g, v_meta_tokens, v_norm_mix_g, v_w_in, v_w_alpha_up, v_b_alpha, v_gla_norm_g, v_w_gla_o, v_conf_dw_w, v_conf_dw_b, v_conf_ln_g, v_conf_ln_b, v_w_conf_o, v_w_out, v_norm_ffn_g, v_w_up, v_ffn_dw_w, v_ffn_dw_b, v_w_down, v_final_norm_g):
    ws = dict(meta_tokens=meta_tokens, norm_mix_g=norm_mix_g, w_in=w_in, w_alpha_up=w_alpha_up, b_alpha=b_alpha,
              gla_norm_g=gla_norm_g, w_gla_o=w_gla_o, conf_dw_w=conf_dw_w, conf_dw_b=conf_dw_b, conf_ln_g=conf_ln_g,
              conf_ln_b=conf_ln_b, w_conf_o=w_conf_o, w_out=w_out, norm_ffn_g=norm_ffn_g, w_up=w_up, ffn_dw_w=ffn_dw_w,
              ffn_dw_b=ffn_dw_b, w_down=w_down, final_norm_g=final_norm_g)
    ms = dict(meta_tokens=m_meta_tokens, norm_mix_g=m_norm_mix_g, w_in=m_w_in, w_alpha_up=m_w_alpha_up, b_alpha=m_b_alpha,
              gla_norm_g=m_gla_norm_g, w_gla_o=m_w_gla_o, conf_dw_w=m_conf_dw_w, conf_dw_b=m_conf_dw_b,
              conf_ln_g=m_conf_ln_g, conf_ln_b=m_conf_ln_b, w_conf_o=m_w_conf_o, w_out=m_w_out, norm_ffn_g=m_norm_ffn_g,
              w_up=m_w_up, ffn_dw_w=m_ffn_dw_w, ffn_dw_b=m_ffn_dw_b, w_down=m_w_down, final_norm_g=m_final_norm_g)
    vs = dict(meta_tokens=v_meta_tokens, norm_mix_g=v_norm_mix_g, w_in=v_w_in, w_alpha_up=v_w_alpha_up, b_alpha=v_b_alpha,
              gla_norm_g=v_gla_norm_g, w_gla_o=v_w_gla_o, conf_dw_w=v_conf_dw_w, conf_dw_b=v_conf_dw_b,
              conf_ln_g=v_conf_ln_g, conf_ln_b=v_conf_ln_b, w_conf_o=v_w_conf_o, w_out=v_w_out, norm_ffn_g=v_norm_ffn_g,
              w_up=v_w_up, ffn_dw_w=v_ffn_dw_w, ffn_dw_b=v_ffn_dw_b, w_down=v_w_down, final_norm_g=v_final_norm_g)
    small = SMALL_SHARDED + REPLICATED
    pack_small = lambda d: _pack_small([d[n] for n in small])

    shards = {n: _to_panel(n, ws[n]).astype(BF16) for n in BIG}
    own = [shards["w_in"], pack_small(ws)]
    my_idx = 4 * lax.axis_index("x") + 2 * lax.axis_index("y") + lax.axis_index("c")
    gathered = [lax.dynamic_update_slice(full_, mine[None], (my_idx, 0, 0))
                for full_, mine in zip(_all_gather(own, "weight_gather"), own)]
    full = {n: ws[n].reshape(-1) for n in REPLICATED}
    full["w_in_t"] = gathered[0][:, 0:W_IN_ROWS].reshape(N_IN, D)
    flat, off = gathered[1].reshape(N_DEV, -1), 0
    for n in SMALL_SHARDED:
        k, c = ws[n].shape[-2], ws[n].shape[-1]
        full[n] = flat[:, off:off + k * c].reshape(N_DEV, k, c).transpose(1, 0, 2).reshape(k, N_DEV * c)
        off += k * c + (-(k * c)) % D

    loss, grad_x, g = _local_step(x[0], loss_target[0], full, shards)

    lands = [g["w_in_t"], g["w_up_t"]] + [g[n] for n in BIG[2:]]
    blocks = []
    for n in SMALL_SHARDED:
        k, c = ws[n].shape[-2], ws[n].shape[-1]
        b = g[n].reshape(k, N_DEV, c).transpose(1, 0, 2).reshape(N_DEV, k * c)
        blocks.append(jnp.pad(b, ((0, 0), (0, (-(k * c)) % D))))
    for n in REPLICATED:
        b = jnp.broadcast_to(g[n].reshape(1, -1), (N_DEV, g[n].size))
        blocks.append(jnp.pad(b, ((0, 0), (0, (-b.shape[1]) % D))))
    gsm = jnp.concatenate(blocks, axis=1)
    lands += _exchange([jnp.pad(gsm, ((0, 0), (0, SMALL_ROWS * D - gsm.shape[1]))).reshape(N_DEV, SMALL_ROWS, D)])

    grad, delta, new_m, new_v = {}, {}, {}, {}
    for i, n in enumerate(BIG):
        outs = _adamw(lands[i], _to_panel(n, ws[n]), _to_panel(n, ms[n]), _to_panel(n, vs[n]), ADAM_BLOCK[n], "adamw_" + n)
        grad[n], delta[n], new_m[n], new_v[n] = [_from_panel(n, p, ws[n].shape) for p in outs]
    outs = _adamw(lands[len(BIG)], pack_small(ws), pack_small(ms), pack_small(vs), SMALL_ROWS, "adamw_small")
    shapes = [ws[n].shape for n in small]
    for d, p in zip((grad, delta, new_m, new_v), outs):
        d.update(zip(small, _unpack_small(p, shapes)))

    order = ("meta_tokens", "norm_mix_g", "w_in", "w_alpha_up", "b_alpha", "gla_norm_g", "w_gla_o", "conf_dw_w", "conf_dw_b",
             "conf_ln_g", "conf_ln_b", "w_conf_o", "w_out", "norm_ffn_g", "w_up", "ffn_dw_w", "ffn_dw_b", "w_down",
             "final_norm_g")
    loss = lax.psum(loss, ("x", "y", "c"))
    return (loss, grad_x[None], *[grad[n] for n in order], *[delta[n] for n in order], *[new_m[n] for n in order],
            *[new_v[n] for n in order])
```
